```python
import jax, jax.numpy as jnp
from jax import lax
import numpy as np

D_MODEL = 1024
BATCH = 32
SEQ = 2048
DEPTH = 2

MLA_HEADS = 8
Q_LORA = 384
KV_LORA = 256
QK_NOPE = 64
QK_ROPE = 32
V_HEAD = 64
ROPE_THETA = 10000.0
Q_BLOCK = 128

SG_GROUPS = 8
SG_DIM = 512
SG_CHUNK = 128

RWKV_HEADS = 8
RWKV_HEAD = 64
RWKV_DIM = RWKV_HEADS * RWKV_HEAD
DECAY_LORA = 64
AAA_LORA = 64
GATE_LORA = 128
N_DIR = 2
GN_EPS = 64e-5

N_BRANCH = 3
BRANCH_DIM = 512
D_FF = -(-8 * D_MODEL // (3 * 256)) * 256
NORM_EPS = 1e-6

MLA_IN = Q_LORA + KV_LORA + QK_ROPE
SG_IN = 2 * SG_DIM
RWKV_IN = 3 * RWKV_DIM + N_DIR * DECAY_LORA + N_DIR * AAA_LORA + GATE_LORA
GATE_IN = N_BRANCH * D_MODEL
N_IN = MLA_IN + SG_IN + RWKV_IN + GATE_IN

kernel_name = "hybrid_mla_gmlp_rwkv7_encoder"


def rmsnorm(x, g):
    xf = x.astype(jnp.float32)
    y = xf * lax.rsqrt(jnp.mean(xf * xf, axis=-1, keepdims=True) + NORM_EPS)
    return (y * g.astype(jnp.float32)).astype(x.dtype)


def layernorm(x, g, b, eps=1e-5):
    xf = x.astype(jnp.float32)
    mu = jnp.mean(xf, axis=-1, keepdims=True)
    var = jnp.mean(jnp.square(xf - mu), axis=-1, keepdims=True)
    y = (xf - mu) * lax.rsqrt(var + eps)
    return (y * g.astype(jnp.float32) + b.astype(jnp.float32)).astype(x.dtype)


def rope_angles(positions):
    inv_freq = 1.0 / (ROPE_THETA ** (jnp.arange(0, QK_ROPE, 2, dtype=jnp.float32) / QK_ROPE))
    ang = positions.astype(jnp.float32)[..., None] * inv_freq
    return jnp.cos(ang), jnp.sin(ang)


def apply_rope(x, cos, sin):
    half = QK_ROPE // 2
    xf = x.astype(jnp.float32)
    x1, x2 = xf[..., :half], xf[..., half:]
    return jnp.concatenate([x1 * cos - x2 * sin, x1 * sin + x2 * cos], axis=-1).astype(x.dtype)


def mla_attention(cq, ckv, positions, q_norm_g, w_uq, kv_norm_g, w_ukv):
    B, S, _ = cq.shape
    q = (rmsnorm(cq, q_norm_g) @ w_uq).reshape(B, S, MLA_HEADS, QK_NOPE + QK_ROPE)
    q_nope, q_rope = q[..., :QK_NOPE], q[..., QK_NOPE:]
    c_kv, k_rope = ckv[..., :KV_LORA], ckv[..., KV_LORA:]
    kv = (rmsnorm(c_kv, kv_norm_g) @ w_ukv).reshape(B, S, MLA_HEADS, QK_NOPE + V_HEAD)
    k_nope, v = kv[..., :QK_NOPE], kv[..., QK_NOPE:]
    cos, sin = rope_angles(positions)
    q_rope = apply_rope(q_rope, cos[:, :, None], sin[:, :, None])
    k_rope = apply_rope(k_rope, cos, sin)
    scale = (QK_NOPE + QK_ROPE) ** -0.5
    nb = S // Q_BLOCK

    def to_blocks(t):
        return jnp.swapaxes(t.reshape((B, nb, Q_BLOCK) + t.shape[2:]), 0, 1)

    def attend(blk):
        qn, qr = blk
        s = (jnp.einsum('bqhd,bkhd->bhqk', qn, k_nope)
             + jnp.einsum('bqhd,bkd->bhqk', qr, k_rope))
        p = jax.nn.softmax(s.astype(jnp.float32) * scale, axis=-1).astype(v.dtype)
        return jnp.einsum('bhqk,bkhd->bqhd', p, v)

    o = lax.map(attend, (to_blocks(q_nope), to_blocks(q_rope)))
    return jnp.swapaxes(o, 0, 1).reshape(B, S, MLA_HEADS * V_HEAD)


def spatial_gating(z, ln_g, ln_b, w_s, b_s):
    B, S, _ = z.shape
    z = jax.nn.gelu(z)
    u, v = z[..., :SG_DIM], z[..., SG_DIM:]
    v = layernorm(v, ln_g, ln_b)
    v = v.reshape(B, S // SG_CHUNK, SG_CHUNK, SG_GROUPS, SG_DIM // SG_GROUPS)
    mixed = jnp.einsum('gts,bcsgd->bctgd', w_s, v) + b_s.T[:, :, None]
    return u * mixed.reshape(B, S, SG_DIM)


def centred_delta(z):
    prev = jnp.pad(z[:, :-1], ((0, 0), (1, 0), (0, 0)))
    nxt = jnp.pad(z[:, 1:], ((0, 0), (0, 1), (0, 0)))
    return 0.5 * (prev + nxt) - z


def rwkv7_bidir(z, mu, w0, w2, a0, a2, g2, k_k, k_a, r_k, ln_g, ln_b):
    B, S, _ = z.shape
    f32 = jnp.float32
    z = z + mu * centred_delta(z)
    cuts = np.cumsum([RWKV_DIM, RWKV_DIM, RWKV_DIM, N_DIR * DECAY_LORA, N_DIR * AAA_LORA]).tolist()
    r, k, v, wl, al, gl = jnp.split(z, cuts, axis=-1)
    wl = wl.reshape(B, S, N_DIR, DECAY_LORA)
    al = al.reshape(B, S, N_DIR, AAA_LORA)
    w = (w0 + jnp.einsum('bsnl,nlc->bsnc', jnp.tanh(wl), w2)).astype(f32)
    decay = jnp.exp(-jnp.exp(-jax.nn.softplus(-w) - 0.5))
    a = jax.nn.sigmoid((a0 + jnp.einsum('bsnl,nlc->bsnc', al, a2)).astype(f32))
    g = jax.nn.sigmoid(gl) @ g2

    def heads(t):
        return t.reshape(t.shape[:-1] + (RWKV_HEADS, RWKV_HEAD))

    kk = heads((k * k_k).astype(f32))
    kk = kk / jnp.maximum(jnp.sqrt(jnp.sum(kk * kk, axis=-1, keepdims=True)), 1e-12)
    rf, vf = heads(r.astype(f32)), heads(v.astype(f32))
    k_dir = heads(k.astype(f32)[:, :, None] * (1.0 + (a - 1.0) * k_a.astype(f32)))
    b_dir = kk[:, :, None] * heads(a)
    decay_h = heads(decay)

    def bcast(t):
        return jnp.broadcast_to(t[:, :, None], (B, S, N_DIR) + t.shape[2:])

    def time_major(t):
        t = jnp.stack([t[:, :, 0], jnp.flip(t[:, :, 1], axis=1)], axis=0)
        return jnp.transpose(t, (2, 0, 1, 3, 4))

    xs = (time_major(bcast(rf)), time_major(decay_h), time_major(k_dir),
          time_major(bcast(vf)), time_major(bcast(kk)), time_major(b_dir))

    def step(st, inp):
        r_t, w_t, k_t, v_t, kk_t, b_t = inp
        sa = jnp.einsum('dbhij,dbhj->dbhi', st, kk_t)
        st = (st * w_t[..., None, :] - sa[..., :, None] * b_t[..., None, :]
              + v_t[..., :, None] * k_t[..., None, :])
        return st, jnp.einsum('dbhij,dbhj->dbhi', st, r_t)

    s0 = jnp.zeros((N_DIR, B, RWKV_HEADS, RWKV_HEAD, RWKV_HEAD), f32)
    _, ys = lax.scan(step, s0, xs)
    y = jnp.transpose(ys[:, 0] + jnp.flip(ys[:, 1], axis=0), (1, 0, 2, 3))
    mean = jnp.mean(y, axis=-1, keepdims=True)
    var = jnp.mean(jnp.square(y - mean), axis=-1, keepdims=True)
    y = ((y - mean) * lax.rsqrt(var + GN_EPS)).reshape(B, S, RWKV_DIM)
    y = y * ln_g.astype(f32) + ln_b.astype(f32)
    bonus = jnp.sum(jnp.sum(rf[:, :, None] * k_dir * r_k.astype(f32), axis=-1, keepdims=True), axis=2)
    y = y + (bonus * vf).reshape(B, S, RWKV_DIM)
    return y.astype(z.dtype) * g


def swiglu(h, w_gate, w_up, w_down):
    return (jax.nn.silu(h @ w_gate) * (h @ w_up)) @ w_down


def _fwd_setup_inputs(seed: int = 0) -> dict:
    key = jax.random.key(seed)
    ks = iter(jax.random.split(key, 40))
    nrm = lambda shape, s: jax.random.normal(next(ks), shape, jnp.float32) * s
    gain = lambda shape: 1.0 + nrm(shape, 0.02)
    L, D = DEPTH, D_MODEL
    x = jax.random.normal(next(ks), (BATCH, SEQ, D), jnp.float32)
    positions = jnp.broadcast_to(jnp.arange(SEQ, dtype=jnp.int32), (BATCH, SEQ))
    return {
        "x": x,
        "positions": positions,
        "attn_norm_g": gain((L, D)),
        "w_in": nrm((L, D, N_IN), D ** -0.5),
        "gate_b": nrm((L, N_BRANCH, D), 0.02),
        "q_norm_g": gain((L, Q_LORA)),
        "w_uq": nrm((L, Q_LORA, MLA_HEADS * (QK_NOPE + QK_ROPE)), Q_LORA ** -0.5),
        "kv_norm_g": gain((L, KV_LORA)),
        "w_ukv": nrm((L, KV_LORA, MLA_HEADS * (QK_NOPE + V_HEAD)), KV_LORA ** -0.5),
        "sg_ln_g": gain((L, SG_DIM)),
        "sg_ln_b": nrm((L, SG_DIM), 0.02),
        "sg_w": nrm((L, SG_GROUPS, SG_CHUNK, SG_CHUNK), SG_CHUNK ** -0.5),
        "sg_b": gain((L, SG_GROUPS, SG_CHUNK)),
        "rw_mu": jax.random.uniform(next(ks), (L, RWKV_IN), jnp.float32),
        "rw_w0": jax.random.uniform(next(ks), (L, N_DIR, RWKV_DIM), jnp.float32, -4.0, 1.0),
        "rw_w2": nrm((L, N_DIR, DECAY_LORA, RWKV_DIM), 0.5 * DECAY_LORA ** -0.5),
        "rw_a0": nrm((L, N_DIR, RWKV_DIM), 0.1),
        "rw_a2": nrm((L, N_DIR, AAA_LORA, RWKV_DIM), 0.5 * AAA_LORA ** -0.5),
        "rw_g2": nrm((L, GATE_LORA, RWKV_DIM), GATE_LORA ** -0.5),
        "rw_k_k": 0.85 + nrm((L, RWKV_DIM), 0.02),
        "rw_k_a": gain((L, RWKV_DIM)),
        "rw_r_k": nrm((L, RWKV_HEADS, RWKV_HEAD), 0.1),
        "rw_ln_g": gain((L, RWKV_DIM)),
        "rw_ln_b": nrm((L, RWKV_DIM), 0.02),
        "w_branch": nrm((L, N_BRANCH, BRANCH_DIM, D), BRANCH_DIM ** -0.5),
        "w_out": nrm((L, D, D), D ** -0.5),
        "ffn_norm_g": gain((L, D)),
        "w_ffn_gate": nrm((L, D, D_FF), D ** -0.5),
        "w_ffn_up": nrm((L, D, D_FF), D ** -0.5),
        "w_ffn_down": nrm((L, D_FF, D), D_FF ** -0.5),
        "final_norm_g": gain((D,)),
    }


def _fwd_reference(x, positions, attn_norm_g, w_in, gate_b, q_norm_g, w_uq, kv_norm_g, w_ukv,
              sg_ln_g, sg_ln_b, sg_w, sg_b, rw_mu, rw_w0, rw_w2, rw_a0, rw_a2, rw_g2,
              rw_k_k, rw_k_a, rw_r_k, rw_ln_g, rw_ln_b, w_branch, w_out, ffn_norm_g,
              w_ffn_gate, w_ffn_up, w_ffn_down, final_norm_g):
    B, S, D = x.shape
    cuts = np.cumsum([Q_LORA, KV_LORA + QK_ROPE, SG_IN, RWKV_IN]).tolist()
    for l in range(DEPTH):
        h = rmsnorm(x, attn_norm_g[l])
        p = h @ w_in[l]
        p_q, p_kv, p_sg, p_rw, p_gate = jnp.split(p, cuts, axis=-1)
        y_a = mla_attention(p_q, p_kv, positions, q_norm_g[l], w_uq[l], kv_norm_g[l], w_ukv[l])
        y_b = spatial_gating(p_sg, sg_ln_g[l], sg_ln_b[l], sg_w[l], sg_b[l])
        y_c = rwkv7_bidir(p_rw, rw_mu[l], rw_w0[l], rw_w2[l], rw_a0[l], rw_a2[l], rw_g2[l],
                          rw_k_k[l], rw_k_a[l], rw_r_k[l], rw_ln_g[l], rw_ln_b[l])
        branches = jnp.stack([y_a, y_b, y_c], axis=2)
        gates = jax.nn.sigmoid(p_gate.reshape(B, S, N_BRANCH, D) + gate_b[l])
        merged = jnp.sum(gates * jnp.einsum('bsnc,ncd->bsnd', branches, w_branch[l]), axis=2)
        x = x + merged @ w_out[l]
        x = x + swiglu(rmsnorm(x, ffn_norm_g[l]), w_ffn_gate[l], w_ffn_up[l], w_ffn_down[l])
    return rmsnorm(x, final_norm_g)


import jax as _jax
import jax.numpy as _jnp

TWIN_FORMAT = 'train_step'
FWD_PARAMS = ['x', 'positions', 'attn_norm_g', 'w_in', 'gate_b', 'q_norm_g', 'w_uq', 'kv_norm_g', 'w_ukv', 'sg_ln_g', 'sg_ln_b', 'sg_w', 'sg_b', 'rw_mu', 'rw_w0', 'rw_w2', 'rw_a0', 'rw_a2', 'rw_g2', 'rw_k_k', 'rw_k_a', 'rw_r_k', 'rw_ln_g', 'rw_ln_b', 'w_branch', 'w_out', 'ffn_norm_g', 'w_ffn_gate', 'w_ffn_up', 'w_ffn_down', 'final_norm_g']
TWIN_WEIGHTS = ['attn_norm_g', 'w_in', 'gate_b', 'q_norm_g', 'w_uq', 'kv_norm_g', 'w_ukv', 'sg_ln_g', 'sg_ln_b', 'sg_w', 'sg_b', 'rw_mu', 'rw_w0', 'rw_w2', 'rw_a0', 'rw_a2', 'rw_g2', 'rw_k_k', 'rw_k_a', 'rw_r_k', 'rw_ln_g', 'rw_ln_b', 'w_branch', 'w_out', 'ffn_norm_g', 'w_ffn_gate', 'w_ffn_up', 'w_ffn_down', 'final_norm_g']
TWIN_DIFF_INPUT = 'x'
TWIN_INPUTS = ['x', 'positions', 'attn_norm_g', 'w_in', 'gate_b', 'q_norm_g', 'w_uq', 'kv_norm_g', 'w_ukv', 'sg_ln_g', 'sg_ln_b', 'sg_w', 'sg_b', 'rw_mu', 'rw_w0', 'rw_w2', 'rw_a0', 'rw_a2', 'rw_g2', 'rw_k_k', 'rw_k_a', 'rw_r_k', 'rw_ln_g', 'rw_ln_b', 'w_branch', 'w_out', 'ffn_norm_g', 'w_ffn_gate', 'w_ffn_up', 'w_ffn_down', 'final_norm_g', 'loss_target', 'm_attn_norm_g', 'm_w_in', 'm_gate_b', 'm_q_norm_g', 'm_w_uq', 'm_kv_norm_g', 'm_w_ukv', 'm_sg_ln_g', 'm_sg_ln_b', 'm_sg_w', 'm_sg_b', 'm_rw_mu', 'm_rw_w0', 'm_rw_w2', 'm_rw_a0', 'm_rw_a2', 'm_rw_g2', 'm_rw_k_k', 'm_rw_k_a', 'm_rw_r_k', 'm_rw_ln_g', 'm_rw_ln_b', 'm_w_branch', 'm_w_out', 'm_ffn_norm_g', 'm_w_ffn_gate', 'm_w_ffn_up', 'm_w_ffn_down', 'm_final_norm_g', 'v_attn_norm_g', 'v_w_in', 'v_gate_b', 'v_q_norm_g', 'v_w_uq', 'v_kv_norm_g', 'v_w_ukv', 'v_sg_ln_g', 'v_sg_ln_b', 'v_sg_w', 'v_sg_b', 'v_rw_mu', 'v_rw_w0', 'v_rw_w2', 'v_rw_a0', 'v_rw_a2', 'v_rw_g2', 'v_rw_k_k', 'v_rw_k_a', 'v_rw_r_k', 'v_rw_ln_g', 'v_rw_ln_b', 'v_w_branch', 'v_w_out', 'v_ffn_norm_g', 'v_w_ffn_gate', 'v_w_ffn_up', 'v_w_ffn_down', 'v_final_norm_g']
TWIN_OUTPUTS = ['loss', 'grad_x', 'grad_attn_norm_g', 'grad_w_in', 'grad_gate_b', 'grad_q_norm_g', 'grad_w_uq', 'grad_kv_norm_g', 'grad_w_ukv', 'grad_sg_ln_g', 'grad_sg_ln_b', 'grad_sg_w', 'grad_sg_b', 'grad_rw_mu', 'grad_rw_w0', 'grad_rw_w2', 'grad_rw_a0', 'grad_rw_a2', 'grad_rw_g2', 'grad_rw_k_k', 'grad_rw_k_a', 'grad_rw_r_k', 'grad_rw_ln_g', 'grad_rw_ln_b', 'grad_w_branch', 'grad_w_out', 'grad_ffn_norm_g', 'grad_w_ffn_gate', 'grad_w_ffn_up', 'grad_w_ffn_down', 'grad_final_norm_g', 'delta_attn_norm_g', 'delta_w_in', 'delta_gate_b', 'delta_q_norm_g', 'delta_w_uq', 'delta_kv_norm_g', 'delta_w_ukv', 'delta_sg_ln_g', 'delta_sg_ln_b', 'delta_sg_w', 'delta_sg_b', 'delta_rw_mu', 'delta_rw_w0', 'delta_rw_w2', 'delta_rw_a0', 'delta_rw_a2', 'delta_rw_g2', 'delta_rw_k_k', 'delta_rw_k_a', 'delta_rw_r_k', 'delta_rw_ln_g', 'delta_rw_ln_b', 'delta_w_branch', 'delta_w_out', 'delta_ffn_norm_g', 'delta_w_ffn_gate', 'delta_w_ffn_up', 'delta_w_ffn_down', 'delta_final_norm_g', 'new_m_attn_norm_g', 'new_m_w_in', 'new_m_gate_b', 'new_m_q_norm_g', 'new_m_w_uq', 'new_m_kv_norm_g', 'new_m_w_ukv', 'new_m_sg_ln_g', 'new_m_sg_ln_b', 'new_m_sg_w', 'new_m_sg_b', 'new_m_rw_mu', 'new_m_rw_w0', 'new_m_rw_w2', 'new_m_rw_a0', 'new_m_rw_a2', 'new_m_rw_g2', 'new_m_rw_k_k', 'new_m_rw_k_a', 'new_m_rw_r_k', 'new_m_rw_ln_g', 'new_m_rw_ln_b', 'new_m_w_branch', 'new_m_w_out', 'new_m_ffn_norm_g', 'new_m_w_ffn_gate', 'new_m_w_ffn_up', 'new_m_w_ffn_down', 'new_m_final_norm_g', 'new_v_attn_norm_g', 'new_v_w_in', 'new_v_gate_b', 'new_v_q_norm_g', 'new_v_w_uq', 'new_v_kv_norm_g', 'new_v_w_ukv', 'new_v_sg_ln_g', 'new_v_sg_ln_b', 'new_v_sg_w', 'new_v_sg_b', 'new_v_rw_mu', 'new_v_rw_w0', 'new_v_rw_w2', 'new_v_rw_a0', 'new_v_rw_a2', 'new_v_rw_g2', 'new_v_rw_k_k', 'new_v_rw_k_a', 'new_v_rw_r_k', 'new_v_rw_ln_g', 'new_v_rw_ln_b', 'new_v_w_branch', 'new_v_w_out', 'new_v_ffn_norm_g', 'new_v_w_ffn_gate', 'new_v_w_ffn_up', 'new_v_w_ffn_down', 'new_v_final_norm_g']
TWIN_LEAF_KINDS = {'loss': 'loss', 'grad_x': 'grad_x', 'grad_attn_norm_g': 'grad_w', 'grad_w_in': 'grad_w', 'grad_gate_b': 'grad_w', 'grad_q_norm_g': 'grad_w', 'grad_w_uq': 'grad_w', 'grad_kv_norm_g': 'grad_w', 'grad_w_ukv': 'grad_w', 'grad_sg_ln_g': 'grad_w', 'grad_sg_ln_b': 'grad_w', 'grad_sg_w': 'grad_w', 'grad_sg_b': 'grad_w', 'grad_rw_mu': 'grad_w', 'grad_rw_w0': 'grad_w', 'grad_rw_w2': 'grad_w', 'grad_rw_a0': 'grad_w', 'grad_rw_a2': 'grad_w', 'grad_rw_g2': 'grad_w', 'grad_rw_k_k': 'grad_w', 'grad_rw_k_a': 'grad_w', 'grad_rw_r_k': 'grad_w', 'grad_rw_ln_g': 'grad_w', 'grad_rw_ln_b': 'grad_w', 'grad_w_branch': 'grad_w', 'grad_w_out': 'grad_w', 'grad_ffn_norm_g': 'grad_w', 'grad_w_ffn_gate': 'grad_w', 'grad_w_ffn_up': 'grad_w', 'grad_w_ffn_down': 'grad_w', 'grad_final_norm_g': 'grad_w', 'delta_attn_norm_g': 'delta_w', 'delta_w_in': 'delta_w', 'delta_gate_b': 'delta_w', 'delta_q_norm_g': 'delta_w', 'delta_w_uq': 'delta_w', 'delta_kv_norm_g': 'delta_w', 'delta_w_ukv': 'delta_w', 'delta_sg_ln_g': 'delta_w', 'delta_sg_ln_b': 'delta_w', 'delta_sg_w': 'delta_w', 'delta_sg_b': 'delta_w', 'delta_rw_mu': 'delta_w', 'delta_rw_w0': 'delta_w', 'delta_rw_w2': 'delta_w', 'delta_rw_a0': 'delta_w', 'delta_rw_a2': 'delta_w', 'delta_rw_g2': 'delta_w', 'delta_rw_k_k': 'delta_w', 'delta_rw_k_a': 'delta_w', 'delta_rw_r_k': 'delta_w', 'delta_rw_ln_g': 'delta_w', 'delta_rw_ln_b': 'delta_w', 'delta_w_branch': 'delta_w', 'delta_w_out': 'delta_w', 'delta_ffn_norm_g': 'delta_w', 'delta_w_ffn_gate': 'delta_w', 'delta_w_ffn_up': 'delta_w', 'delta_w_ffn_down': 'delta_w', 'delta_final_norm_g': 'delta_w', 'new_m_attn_norm_g': 'new_m', 'new_m_w_in': 'new_m', 'new_m_gate_b': 'new_m', 'new_m_q_norm_g': 'new_m', 'new_m_w_uq': 'new_m', 'new_m_kv_norm_g': 'new_m', 'new_m_w_ukv': 'new_m', 'new_m_sg_ln_g': 'new_m', 'new_m_sg_ln_b': 'new_m', 'new_m_sg_w': 'new_m', 'new_m_sg_b': 'new_m', 'new_m_rw_mu': 'new_m', 'new_m_rw_w0': 'new_m', 'new_m_rw_w2': 'new_m', 'new_m_rw_a0': 'new_m', 'new_m_rw_a2': 'new_m', 'new_m_rw_g2': 'new_m', 'new_m_rw_k_k': 'new_m', 'new_m_rw_k_a': 'new_m', 'new_m_rw_r_k': 'new_m', 'new_m_rw_ln_g': 'new_m', 'new_m_rw_ln_b': 'new_m', 'new_m_w_branch': 'new_m', 'new_m_w_out': 'new_m', 'new_m_ffn_norm_g': 'new_m', 'new_m_w_ffn_gate': 'new_m', 'new_m_w_ffn_up': 'new_m', 'new_m_w_ffn_down': 'new_m', 'new_m_final_norm_g': 'new_m', 'new_v_attn_norm_g': 'new_v', 'new_v_w_in': 'new_v', 'new_v_gate_b': 'new_v', 'new_v_q_norm_g': 'new_v', 'new_v_w_uq': 'new_v', 'new_v_kv_norm_g': 'new_v', 'new_v_w_ukv': 'new_v', 'new_v_sg_ln_g': 'new_v', 'new_v_sg_ln_b': 'new_v', 'new_v_sg_w': 'new_v', 'new_v_sg_b': 'new_v', 'new_v_rw_mu': 'new_v', 'new_v_rw_w0': 'new_v', 'new_v_rw_w2': 'new_v', 'new_v_rw_a0': 'new_v', 'new_v_rw_a2': 'new_v', 'new_v_rw_g2': 'new_v', 'new_v_rw_k_k': 'new_v', 'new_v_rw_k_a': 'new_v', 'new_v_rw_r_k': 'new_v', 'new_v_rw_ln_g': 'new_v', 'new_v_rw_ln_b': 'new_v', 'new_v_w_branch': 'new_v', 'new_v_w_out': 'new_v', 'new_v_ffn_norm_g': 'new_v', 'new_v_w_ffn_gate': 'new_v', 'new_v_w_ffn_up': 'new_v', 'new_v_w_ffn_down': 'new_v', 'new_v_final_norm_g': 'new_v'}


def _forward(args):
    return _fwd_reference(*[args[k] for k in FWD_PARAMS])


def _output_shape():
    out = _jax.eval_shape(lambda: _forward(_fwd_setup_inputs(0)))
    return out.shape, out.dtype

N_MICROBATCH = 1
ADAM_LR = 0.001
ADAM_B1 = 0.9
ADAM_B2 = 0.999
ADAM_EPS = 1e-08
ADAM_WD = 0.01
ADAM_STEP = 10
PER_EXAMPLE_BATCH_AXIS = {'x': 0, 'positions': 0, 'loss_target': 0}
SHARED_INPUTS = []
_WEIGHT_DTYPES = {'attn_norm_g': _jnp.float32, 'w_in': _jnp.float32, 'gate_b': _jnp.float32, 'q_norm_g': _jnp.float32, 'w_uq': _jnp.float32, 'kv_norm_g': _jnp.float32, 'w_ukv': _jnp.float32, 'sg_ln_g': _jnp.float32, 'sg_ln_b': _jnp.float32, 'sg_w': _jnp.float32, 'sg_b': _jnp.float32, 'rw_mu': _jnp.float32, 'rw_w0': _jnp.float32, 'rw_w2': _jnp.float32, 'rw_a0': _jnp.float32, 'rw_a2': _jnp.float32, 'rw_g2': _jnp.float32, 'rw_k_k': _jnp.float32, 'rw_k_a': _jnp.float32, 'rw_r_k': _jnp.float32, 'rw_ln_g': _jnp.float32, 'rw_ln_b': _jnp.float32, 'w_branch': _jnp.float32, 'w_out': _jnp.float32, 'ffn_norm_g': _jnp.float32, 'w_ffn_gate': _jnp.float32, 'w_ffn_up': _jnp.float32, 'w_ffn_down': _jnp.float32, 'final_norm_g': _jnp.float32}
MOMENT_SCALE = {'attn_norm_g': 2.183137e-01, 'w_in': 8.413990e-02, 'gate_b': 3.404073e-02, 'q_norm_g': 2.496102e-02, 'w_uq': 1.717933e-02, 'kv_norm_g': 5.063897e-02, 'w_ukv': 2.228464e-02, 'sg_ln_g': 1.300699e-01, 'sg_ln_b': 1.215909e-01, 'sg_w': 8.508554e-02, 'sg_b': 8.594578e-02, 'rw_mu': 1.537372e-01, 'rw_w0': 3.599114e-02, 'rw_w2': 5.975209e-03, 'rw_a0': 3.078262e-02, 'rw_a2': 2.270485e-02, 'rw_g2': 1.031736e-01, 'rw_k_k': 6.496034e-02, 'rw_k_a': 1.170163e-01, 'rw_r_k': 2.804841e-01, 'rw_ln_g': 1.045797e-01, 'rw_ln_b': 1.089124e-01, 'w_branch': 8.197811e-02, 'w_out': 1.422250e-01, 'ffn_norm_g': 1.691135e-01, 'w_ffn_gate': 7.337669e-02, 'w_ffn_up': 7.120246e-02, 'w_ffn_down': 1.178366e-01, 'final_norm_g': 6.383551e+01}


def _to_microbatches(a, axis):
    t = _jnp.moveaxis(a, axis, 0)
    t = t.reshape((N_MICROBATCH, t.shape[0] // N_MICROBATCH) + t.shape[1:])
    return _jnp.moveaxis(t, 1, axis + 1)


def setup_inputs(seed: int = 0) -> dict:
    inp = _fwd_setup_inputs(seed)
    key = _jax.random.fold_in(_jax.random.key(seed), 7919)
    shape, _ = _output_shape()
    out = dict(inp)
    out["loss_target"] = _jax.random.normal(_jax.random.fold_in(key, 0), shape, _jnp.float32)
    for i, name in enumerate(TWIN_WEIGHTS):
        w = inp[name].astype(_jnp.float32)
        if MOMENT_SCALE is None:
            s = _jnp.sqrt(_jnp.mean(_jnp.square(w)) + 1e-30)
        else:
            s = MOMENT_SCALE[name]
        km, kv = _jax.random.split(_jax.random.fold_in(key, i + 1))
        out[name] = w
        out["m_" + name] = s * _jax.random.normal(km, w.shape, _jnp.float32)
        out["v_" + name] = (s * s) * _jax.random.uniform(kv, w.shape, _jnp.float32, 0.5, 1.5)
    if N_MICROBATCH > 1:
        for name, axis in PER_EXAMPLE_BATCH_AXIS.items():
            out[name] = _to_microbatches(out[name], axis)
    return {'x': out['x'], 'positions': out['positions'], 'attn_norm_g': out['attn_norm_g'], 'w_in': out['w_in'], 'gate_b': out['gate_b'], 'q_norm_g': out['q_norm_g'], 'w_uq': out['w_uq'], 'kv_norm_g': out['kv_norm_g'], 'w_ukv': out['w_ukv'], 'sg_ln_g': out['sg_ln_g'], 'sg_ln_b': out['sg_ln_b'], 'sg_w': out['sg_w'], 'sg_b': out['sg_b'], 'rw_mu': out['rw_mu'], 'rw_w0': out['rw_w0'], 'rw_w2': out['rw_w2'], 'rw_a0': out['rw_a0'], 'rw_a2': out['rw_a2'], 'rw_g2': out['rw_g2'], 'rw_k_k': out['rw_k_k'], 'rw_k_a': out['rw_k_a'], 'rw_r_k': out['rw_r_k'], 'rw_ln_g': out['rw_ln_g'], 'rw_ln_b': out['rw_ln_b'], 'w_branch': out['w_branch'], 'w_out': out['w_out'], 'ffn_norm_g': out['ffn_norm_g'], 'w_ffn_gate': out['w_ffn_gate'], 'w_ffn_up': out['w_ffn_up'], 'w_ffn_down': out['w_ffn_down'], 'final_norm_g': out['final_norm_g'], 'loss_target': out['loss_target'], 'm_attn_norm_g': out['m_attn_norm_g'], 'm_w_in': out['m_w_in'], 'm_gate_b': out['m_gate_b'], 'm_q_norm_g': out['m_q_norm_g'], 'm_w_uq': out['m_w_uq'], 'm_kv_norm_g': out['m_kv_norm_g'], 'm_w_ukv': out['m_w_ukv'], 'm_sg_ln_g': out['m_sg_ln_g'], 'm_sg_ln_b': out['m_sg_ln_b'], 'm_sg_w': out['m_sg_w'], 'm_sg_b': out['m_sg_b'], 'm_rw_mu': out['m_rw_mu'], 'm_rw_w0': out['m_rw_w0'], 'm_rw_w2': out['m_rw_w2'], 'm_rw_a0': out['m_rw_a0'], 'm_rw_a2': out['m_rw_a2'], 'm_rw_g2': out['m_rw_g2'], 'm_rw_k_k': out['m_rw_k_k'], 'm_rw_k_a': out['m_rw_k_a'], 'm_rw_r_k': out['m_rw_r_k'], 'm_rw_ln_g': out['m_rw_ln_g'], 'm_rw_ln_b': out['m_rw_ln_b'], 'm_w_branch': out['m_w_branch'], 'm_w_out': out['m_w_out'], 'm_ffn_norm_g': out['m_ffn_norm_g'], 'm_w_ffn_gate': out['m_w_ffn_gate'], 'm_w_ffn_up': out['m_w_ffn_up'], 'm_w_ffn_down': out['m_w_ffn_down'], 'm_final_norm_g': out['m_final_norm_g'], 'v_attn_norm_g': out['v_attn_norm_g'], 'v_w_in': out['v_w_in'], 'v_gate_b': out['v_gate_b'], 'v_q_norm_g': out['v_q_norm_g'], 'v_w_uq': out['v_w_uq'], 'v_kv_norm_g': out['v_kv_norm_g'], 'v_w_ukv': out['v_w_ukv'], 'v_sg_ln_g': out['v_sg_ln_g'], 'v_sg_ln_b': out['v_sg_ln_b'], 'v_sg_w': out['v_sg_w'], 'v_sg_b': out['v_sg_b'], 'v_rw_mu': out['v_rw_mu'], 'v_rw_w0': out['v_rw_w0'], 'v_rw_w2': out['v_rw_w2'], 'v_rw_a0': out['v_rw_a0'], 'v_rw_a2': out['v_rw_a2'], 'v_rw_g2': out['v_rw_g2'], 'v_rw_k_k': out['v_rw_k_k'], 'v_rw_k_a': out['v_rw_k_a'], 'v_rw_r_k': out['v_rw_r_k'], 'v_rw_ln_g': out['v_rw_ln_g'], 'v_rw_ln_b': out['v_rw_ln_b'], 'v_w_branch': out['v_w_branch'], 'v_w_out': out['v_w_out'], 'v_ffn_norm_g': out['v_ffn_norm_g'], 'v_w_ffn_gate': out['v_w_ffn_gate'], 'v_w_ffn_up': out['v_w_ffn_up'], 'v_w_ffn_down': out['v_w_ffn_down'], 'v_final_norm_g': out['v_final_norm_g']}


def _loss(weights, diff, rest, loss_target):
    with _jax.named_scope("forward"):
        args = {**rest, TWIN_DIFF_INPUT: diff, **{k: w.astype(_WEIGHT_DTYPES[k]) for k, w in weights.items()}}
        y = _forward(args)
    with _jax.named_scope("loss_head"):
        err = _jnp.square(y.astype(_jnp.float32) - loss_target)
        return 0.5 * _jnp.sum(_jnp.mean(err, axis=-1)) if err.ndim else 0.5 * err


def _adamw(w, g, m, v):
    m = ADAM_B1 * m + (1.0 - ADAM_B1) * g
    v = ADAM_B2 * v + (1.0 - ADAM_B2) * _jnp.square(g)
    m_hat = m / (1.0 - ADAM_B1 ** ADAM_STEP)
    v_hat = v / (1.0 - ADAM_B2 ** ADAM_STEP)
    delta = -ADAM_LR * (m_hat / (_jnp.sqrt(v_hat) + ADAM_EPS) + ADAM_WD * w)
    return delta, m, v


def reference(x, positions, attn_norm_g, w_in, gate_b, q_norm_g, w_uq, kv_norm_g, w_ukv, sg_ln_g, sg_ln_b, sg_w, sg_b, rw_mu, rw_w0, rw_w2, rw_a0, rw_a2, rw_g2, rw_k_k, rw_k_a, rw_r_k, rw_ln_g, rw_ln_b, w_branch, w_out, ffn_norm_g, w_ffn_gate, w_ffn_up, w_ffn_down, final_norm_g, loss_target, m_attn_norm_g, m_w_in, m_gate_b, m_q_norm_g, m_w_uq, m_kv_norm_g, m_w_ukv, m_sg_ln_g, m_sg_ln_b, m_sg_w, m_sg_b, m_rw_mu, m_rw_w0, m_rw_w2, m_rw_a0, m_rw_a2, m_rw_g2, m_rw_k_k, m_rw_k_a, m_rw_r_k, m_rw_ln_g, m_rw_ln_b, m_w_branch, m_w_out, m_ffn_norm_g, m_w_ffn_gate, m_w_ffn_up, m_w_ffn_down, m_final_norm_g, v_attn_norm_g, v_w_in, v_gate_b, v_q_norm_g, v_w_uq, v_kv_norm_g, v_w_ukv, v_sg_ln_g, v_sg_ln_b, v_sg_w, v_sg_b, v_rw_mu, v_rw_w0, v_rw_w2, v_rw_a0, v_rw_a2, v_rw_g2, v_rw_k_k, v_rw_k_a, v_rw_r_k, v_rw_ln_g, v_rw_ln_b, v_w_branch, v_w_out, v_ffn_norm_g, v_w_ffn_gate, v_w_ffn_up, v_w_ffn_down, v_final_norm_g):
    given = dict(x=x, positions=positions, attn_norm_g=attn_norm_g, w_in=w_in, gate_b=gate_b, q_norm_g=q_norm_g, w_uq=w_uq, kv_norm_g=kv_norm_g, w_ukv=w_ukv, sg_ln_g=sg_ln_g, sg_ln_b=sg_ln_b, sg_w=sg_w, sg_b=sg_b, rw_mu=rw_mu, rw_w0=rw_w0, rw_w2=rw_w2, rw_a0=rw_a0, rw_a2=rw_a2, rw_g2=rw_g2, rw_k_k=rw_k_k, rw_k_a=rw_k_a, rw_r_k=rw_r_k, rw_ln_g=rw_ln_g, rw_ln_b=rw_ln_b, w_branch=w_branch, w_out=w_out, ffn_norm_g=ffn_norm_g, w_ffn_gate=w_ffn_gate, w_ffn_up=w_ffn_up, w_ffn_down=w_ffn_down, final_norm_g=final_norm_g, loss_target=loss_target, m_attn_norm_g=m_attn_norm_g, m_w_in=m_w_in, m_gate_b=m_gate_b, m_q_norm_g=m_q_norm_g, m_w_uq=m_w_uq, m_kv_norm_g=m_kv_norm_g, m_w_ukv=m_w_ukv, m_sg_ln_g=m_sg_ln_g, m_sg_ln_b=m_sg_ln_b, m_sg_w=m_sg_w, m_sg_b=m_sg_b, m_rw_mu=m_rw_mu, m_rw_w0=m_rw_w0, m_rw_w2=m_rw_w2, m_rw_a0=m_rw_a0, m_rw_a2=m_rw_a2, m_rw_g2=m_rw_g2, m_rw_k_k=m_rw_k_k, m_rw_k_a=m_rw_k_a, m_rw_r_k=m_rw_r_k, m_rw_ln_g=m_rw_ln_g, m_rw_ln_b=m_rw_ln_b, m_w_branch=m_w_branch, m_w_out=m_w_out, m_ffn_norm_g=m_ffn_norm_g, m_w_ffn_gate=m_w_ffn_gate, m_w_ffn_up=m_w_ffn_up, m_w_ffn_down=m_w_ffn_down, m_final_norm_g=m_final_norm_g, v_attn_norm_g=v_attn_norm_g, v_w_in=v_w_in, v_gate_b=v_gate_b, v_q_norm_g=v_q_norm_g, v_w_uq=v_w_uq, v_kv_norm_g=v_kv_norm_g, v_w_ukv=v_w_ukv, v_sg_ln_g=v_sg_ln_g, v_sg_ln_b=v_sg_ln_b, v_sg_w=v_sg_w, v_sg_b=v_sg_b, v_rw_mu=v_rw_mu, v_rw_w0=v_rw_w0, v_rw_w2=v_rw_w2, v_rw_a0=v_rw_a0, v_rw_a2=v_rw_a2, v_rw_g2=v_rw_g2, v_rw_k_k=v_rw_k_k, v_rw_k_a=v_rw_k_a, v_rw_r_k=v_rw_r_k, v_rw_ln_g=v_rw_ln_g, v_rw_ln_b=v_rw_ln_b, v_w_branch=v_w_branch, v_w_out=v_w_out, v_ffn_norm_g=v_ffn_norm_g, v_w_ffn_gate=v_w_ffn_gate, v_w_ffn_up=v_w_ffn_up, v_w_ffn_down=v_w_ffn_down, v_final_norm_g=v_final_norm_g)
    weights = {n: given[n] for n in TWIN_WEIGHTS}
    shared = {n: given[n] for n in SHARED_INPUTS}
    per_example = {n: given[n] for n in ['x', 'positions']}
    grad_fn = _jax.value_and_grad(_loss, argnums=(0, 1))

    def one_microbatch(ex, loss_target):
        ex = dict(ex)
        diff = ex.pop(TWIN_DIFF_INPUT)
        return grad_fn(weights, diff, {**shared, **ex}, loss_target)

    if N_MICROBATCH == 1:
        loss, (grad_w, grad_x) = one_microbatch(per_example, given["loss_target"])
    else:
        def body(carry, xs):
            loss_sum, grad_sum = carry
            l_k, (gw_k, gx_k) = one_microbatch(xs[0], xs[1])
            with _jax.named_scope("update"):
                return (loss_sum + l_k, _jax.tree.map(_jnp.add, grad_sum, gw_k)), gx_k

        init = (_jnp.zeros((), _jnp.float32), _jax.tree.map(_jnp.zeros_like, weights))
        (loss, grad_w), grad_x = _jax.lax.scan(body, init, (per_example, given["loss_target"]))
    with _jax.named_scope("update"):
        delta_w, new_m, new_v = {}, {}, {}
        for n in TWIN_WEIGHTS:
            delta_w[n], new_m[n], new_v[n] = _adamw(weights[n], grad_w[n], given["m_" + n], given["v_" + n])
    return (loss, grad_x, *[grad_w[n] for n in TWIN_WEIGHTS], *[delta_w[n] for n in TWIN_WEIGHTS],
            *[new_m[n] for n in TWIN_WEIGHTS], *[new_v[n] for n in TWIN_WEIGHTS])
```

```python
import functools

import jax
import jax.numpy as jnp
from jax import lax
from jax.experimental import pallas as pl
from jax.experimental.pallas import tpu as pltpu

f32 = jnp.float32
bf16 = jnp.bfloat16
HI = lax.Precision.HIGHEST
NN, NT, TN = ((1,), (0,)), ((1,), (1,)), ((0,), (0,))

N_DEV = 8
D = 1024
HEADS = 8
Q_LORA, KV_LORA, QK_NOPE, QK_ROPE, V_HEAD = 384, 256, 64, 32, 64
SG_DIM, SG_CHUNK, SG_GROUPS = 512, 128, 8
RW_DIM, RW_HEAD, LORA = 512, 64, 64
D_FF = 2816
N_IN = 6688
NORM_EPS, LN_EPS, GN_EPS = 1e-6, 1e-5, 64e-5
ATT_SCALE = (QK_NOPE + QK_ROPE) ** -0.5
P_W = 7168
O_GATE, O_SG, O_RW, O_MLA = 0, 3072, 4096, 6144
RW_W = 2048
MLA_W = 768
O_CKV, O_SLAB, O_Q = O_MLA, O_MLA + 256, O_MLA + 384
CHUNK = 128
VMEM_LIMIT = 56 * 1024 * 1024

B1, B2, LR, EPS, WD, STEP = 0.9, 0.999, 0.001, 1e-8, 0.01, 10


def _pc(body, *, name, out_shape, grid=(), in_specs=None, out_specs=None, scratch=(), sem=None, aliases=None, **cp):
    params = pltpu.CompilerParams(dimension_semantics=sem, vmem_limit_bytes=VMEM_LIMIT, **cp)
    kw = {}
    if in_specs is not None:
        kw["in_specs"] = in_specs
    if out_specs is not None:
        kw["out_specs"] = out_specs
    return pl.pallas_call(body, out_shape=out_shape, grid=grid, scratch_shapes=scratch, compiler_params=params,
                          name=name, input_output_aliases=aliases or {}, interpret=False, **kw)


def _sds(shape, dtype=f32):
    return jax.ShapeDtypeStruct(tuple(shape), dtype)


def _dot(a, b, dims, precision=None):
    return lax.dot_general(a, b, (dims, ((), ())), preferred_element_type=f32, precision=precision)


def _bdot(a, b, dims=NN):
    return _dot(a.astype(bf16), b.astype(bf16), dims)


@jax.custom_vjp
def _mm(a, w):
    return _bdot(a, w, NN)


def _mm_fwd(a, w):
    return _bdot(a, w, NN), (a, w)


def _mm_bwd(res, g):
    a, w = res
    return _bdot(g, w, NT), _bdot(a, g, TN)


_mm.defvjp(_mm_fwd, _mm_bwd)


@jax.custom_vjp
def _mm_nt(a, b):
    return _bdot(a, b, NT)


def _mm_nt_fwd(a, b):
    return _bdot(a, b, NT), (a, b)


def _mm_nt_bwd(res, g):
    a, b = res
    return _bdot(g, b, NN), _bdot(g, a, TN)


_mm_nt.defvjp(_mm_nt_fwd, _mm_nt_bwd)


@jax.custom_vjp
def _mm_tn(a, b):
    return _bdot(a, b, TN)


def _mm_tn_fwd(a, b):
    return _bdot(a, b, TN), (a, b)


def _mm_tn_bwd(res, g):
    a, b = res
    return _bdot(b, g, NT), _bdot(a, g, NN)


_mm_tn.defvjp(_mm_tn_fwd, _mm_tn_bwd)


def _rms(x, g):
    return x * lax.rsqrt(jnp.mean(x * x, axis=-1, keepdims=True) + NORM_EPS) * g


def _sigmoid(x):
    return 1.0 / (1.0 + jnp.exp(-x))


def _gelu(x):
    return 0.5 * x * (1.0 + jnp.tanh(0.7978845608028654 * (x + 0.044715 * x * x * x)))


def _softplus(x):
    return jnp.maximum(x, 0.0) + jnp.log(1.0 + jnp.exp(-jnp.abs(x)))


def _group_sum(x):
    w = x.shape[-1]
    r = lax.broadcasted_iota(jnp.int32, (w, w), 0) // RW_HEAD
    c = lax.broadcasted_iota(jnp.int32, (w, w), 1) // RW_HEAD
    return _dot(x, (r == c).astype(f32), NN, HI)


@jax.custom_vjp
def _swap(x):
    w = x.shape[-1]
    lane = lax.broadcasted_iota(jnp.int32, x.shape, 1) % 128
    lo = (lane >= 64) & (lane < 80)
    hi = (lane >= 80) & (lane < 96)
    return jnp.where(lo, pltpu.roll(x, w - 16, 1), jnp.where(hi, pltpu.roll(x, 16, 1), 0.0))


_swap.defvjp(lambda x: (_swap(x), None), lambda _, g: (_swap(g),))


def _rope(x, c, s):
    return x * c + _swap(x) * s


def _row_spec(tm, width, blk):
    return pl.BlockSpec((tm, width), lambda i, blk=blk: (i, blk))


def _full_spec(a):
    nd = a.ndim
    return pl.BlockSpec(a.shape, lambda i, nd=nd: (0,) * nd)


def _rowwise_fwd(name, f, rows, weights, outs, tm):
    n = rows[0][0].shape[0]
    nr, nw = len(rows), len(weights)

    def body(*refs):
        vals = [r[...].astype(f32) for r in refs[:nr + nw]]
        res = f(*vals)
        for o_ref, o in zip(refs[nr + nw:], res):
            o_ref[...] = o.astype(o_ref.dtype)

    return _pc(
        body, name=name, grid=(n // tm,),
        in_specs=[_row_spec(tm, w, b) for _, w, b in rows] + [_full_spec(w) for w in weights],
        out_specs=[_row_spec(tm, w, 0) for w, _ in outs],
        out_shape=[_sds((n, w), dt) for w, dt in outs], sem=("parallel",),
    )(*[a for a, _, _ in rows], *weights)


def _rowwise_bwd(name, f, rows, weights, cots, tm, drows):
    n = rows[0][0].shape[0]
    nr, nw, nc = len(rows), len(weights), len(cots)
    want = [k for k, dt in enumerate(drows) if dt is not None]

    def body(*refs):
        vals = [r[...].astype(f32) for r in refs[:nr + nw]]
        cot = tuple(r[...].astype(f32) for r in refs[nr + nw:nr + nw + nc])
        _, vjp = jax.vjp(f, *vals)
        grads = vjp(cot)
        outs = refs[nr + nw + nc:]
        for o_ref, k in zip(outs[:len(want)], want):
            o_ref[...] = grads[k].astype(o_ref.dtype)
        first = pl.program_id(0) == 0
        for o_ref, g in zip(outs[len(want):], grads[nr:]):
            @pl.when(first)
            def _(o_ref=o_ref, g=g):
                o_ref[...] = g

            @pl.when(jnp.logical_not(first))
            def _(o_ref=o_ref, g=g):
                o_ref[...] += g

    res = _pc(
        body, name=name, grid=(n // tm,),
        in_specs=[_row_spec(tm, w, b) for _, w, b in rows] + [_full_spec(w) for w in weights]
        + [_row_spec(tm, w, b) for _, w, b in cots],
        out_specs=[_row_spec(tm, rows[k][1], 0) for k in want] + [_full_spec(w) for w in weights],
        out_shape=[_sds((n, rows[k][1]), drows[k]) for k in want] + [_sds(w.shape) for w in weights],
        sem=("arbitrary",),
    )(*[a for a, _, _ in rows], *weights, *[a for a, _, _ in cots])
    return res[:len(want)], res[len(want):]


def _inproj_fwd(x2, g, w, name):
    n = x2.shape[0]
    tm, tn = min(512, n), 512

    def body(x_ref, g_ref, w_ref, p_ref, h_ref):
        @pl.when(pl.program_id(1) == 0)
        def _():
            h_ref[...] = _rms(x_ref[...], g_ref[...]).astype(bf16)

        p_ref[...] = jnp.dot(h_ref[...], w_ref[...], preferred_element_type=f32)

    return _pc(
        body, name=name, grid=(n // tm, P_W // tn),
        in_specs=[pl.BlockSpec((tm, D), lambda i, j: (i, 0)), pl.BlockSpec((1, D), lambda i, j: (0, 0)),
                  pl.BlockSpec((D, tn), lambda i, j: (0, j))],
        out_specs=[pl.BlockSpec((tm, tn), lambda i, j: (i, j)), pl.BlockSpec((tm, D), lambda i, j: (i, 0))],
        out_shape=[_sds((n, P_W)), _sds((n, D), bf16)], sem=("parallel", "arbitrary"),
    )(x2, g, w)


def _norm_matmul_bwd(dy, w, x2, g, dres, name):
    n, k = dy.shape
    tm = min(512, n)
    tk = 1024 if k % 1024 == 0 else 1408
    nk = k // tk

    def body(dy_ref, w_ref, x_ref, g_ref, dr_ref, dx_ref, dg_ref, acc):
        i, j = pl.program_id(0), pl.program_id(1)

        @pl.when(j == 0)
        def _():
            acc[...] = jnp.zeros_like(acc)

        @pl.when((i == 0) & (j == 0))
        def _():
            dg_ref[...] = jnp.zeros_like(dg_ref)

        acc[...] += _dot(dy_ref[...], w_ref[...], NT)

        @pl.when(j == nk - 1)
        def _():
            _, vjp = jax.vjp(_rms, x_ref[...], g_ref[...])
            dx, dg = vjp(acc[...])
            dx_ref[...] = dr_ref[...] + dx
            dg_ref[...] += dg

    return _pc(
        body, name=name, grid=(n // tm, nk),
        in_specs=[pl.BlockSpec((tm, tk), lambda i, j: (i, j)), pl.BlockSpec((D, tk), lambda i, j: (0, j)),
                  pl.BlockSpec((tm, D), lambda i, j: (i, 0)), pl.BlockSpec((1, D), lambda i, j: (0, 0)),
                  pl.BlockSpec((tm, D), lambda i, j: (i, 0))],
        out_specs=[pl.BlockSpec((tm, D), lambda i, j: (i, 0)), pl.BlockSpec((1, D), lambda i, j: (0, 0))],
        out_shape=[_sds((n, D)), _sds((1, D))], scratch=[pltpu.VMEM((tm, D), f32)], sem=("arbitrary", "arbitrary"),
    )(dy, w, x2, g, dres)


def _matmul_tn(a, g, name):
    n, k = a.shape
    m = g.shape[1]
    tr = min(512, n)
    tk = k if k <= 1024 else 1408
    tn = m if m <= 1024 else (512 if m % 512 == 0 else 1408)
    nr = n // tr

    def body(a_ref, g_ref, o_ref):
        @pl.when(pl.program_id(2) == 0)
        def _():
            o_ref[...] = jnp.zeros_like(o_ref)

        o_ref[...] += _dot(a_ref[...], g_ref[...], TN)

    return _pc(
        body, name=name, grid=(k // tk, m // tn, nr),
        in_specs=[pl.BlockSpec((tr, tk), lambda i, j, r: (r, i)), pl.BlockSpec((tr, tn), lambda i, j, r: (r, j))],
        out_specs=pl.BlockSpec((tk, tn), lambda i, j, r: (i, j)),
        out_shape=_sds((k, m)), sem=("parallel", "parallel", "arbitrary"),
    )(a, g)


def _f_mla_proj(ckv, slab, pq, c, s, qg, kg, wq, wk, wv):
    c8, s8 = jnp.concatenate([c] * HEADS, axis=1), jnp.concatenate([s] * HEADS, axis=1)
    q = _rope(_mm(_rms(pq, qg), wq), c8, s8)
    cn = _rms(ckv, kg)
    k = _mm(cn, wk) + jnp.concatenate([_rope(slab, c, s)] * HEADS, axis=1)
    return q, k, _mm(cn, wv)


def _attn_fwd(q, k, v, bsz, seq, name):
    n = q.shape[0]
    tq = min(256, seq)
    nq = seq // tq

    def body(q_ref, k_ref, v_ref, o_ref):
        lane = lax.broadcasted_iota(jnp.int32, (tq, 128), 1) < 64
        vv = v_ref[...]
        outs = []
        for h in range(2):
            s = _dot(q_ref[:, h * 128:(h + 1) * 128], k_ref[:, h * 128:(h + 1) * 128], NT) * ATT_SCALE
            e = jnp.exp(s - jnp.max(s, axis=-1, keepdims=True))
            p = (e / jnp.sum(e, axis=-1, keepdims=True)).astype(bf16)
            outs.append(_dot(p, vv, NN))
        o_ref[...] = jnp.where(lane, outs[0], outs[1])

    return _pc(
        body, name=name, grid=(bsz, HEADS // 2, nq),
        in_specs=[pl.BlockSpec((tq, 256), lambda b, h, i: (b * nq + i, h)),
                  pl.BlockSpec((seq, 256), lambda b, h, i: (b, h)),
                  pl.BlockSpec((seq, 128), lambda b, h, i: (b, h))],
        out_specs=pl.BlockSpec((tq, 128), lambda b, h, i: (b * nq + i, h)),
        out_shape=_sds((n, HEADS * V_HEAD)), sem=("parallel", "parallel", "parallel"),
    )(q, k, v)


def _attn_bwd(q, k, v, o, do, bsz, seq, name):
    n = q.shape[0]
    tq = min(256, seq)
    nq = seq // tq

    def body(q_ref, k_ref, v_ref, o_ref, do_ref, dq_ref, dk_ref, dv_ref):
        @pl.when(pl.program_id(2) == 0)
        def _():
            dk_ref[...] = jnp.zeros_like(dk_ref)
            dv_ref[...] = jnp.zeros_like(dv_ref)

        lane = lax.broadcasted_iota(jnp.int32, (tq, 128), 1) < 64
        vv = v_ref[...]
        for h in range(2):
            qh, kh = q_ref[:, h * 128:(h + 1) * 128], k_ref[:, h * 128:(h + 1) * 128]
            s = _dot(qh, kh, NT) * ATT_SCALE
            e = jnp.exp(s - jnp.max(s, axis=-1, keepdims=True))
            p = e / jnp.sum(e, axis=-1, keepdims=True)
            doh = jnp.where(lane if h == 0 else jnp.logical_not(lane), do_ref[...], 0.0)
            delta = jnp.sum(doh * o_ref[...], axis=-1, keepdims=True)
            dob = doh.astype(bf16)
            dp = _dot(dob, vv, NT)
            ds = (p * (dp - delta) * ATT_SCALE).astype(bf16)
            dq_ref[:, h * 128:(h + 1) * 128] = _dot(ds, kh, NN)
            dk_ref[:, h * 128:(h + 1) * 128] += _dot(ds, qh, TN)
            dv_ref[...] += _dot(p.astype(bf16), dob, TN)

    return _pc(
        body, name=name, grid=(bsz, HEADS // 2, nq),
        in_specs=[pl.BlockSpec((tq, 256), lambda b, h, i: (b * nq + i, h)),
                  pl.BlockSpec((seq, 256), lambda b, h, i: (b, h)),
                  pl.BlockSpec((seq, 128), lambda b, h, i: (b, h)),
                  pl.BlockSpec((tq, 128), lambda b, h, i: (b * nq + i, h)),
                  pl.BlockSpec((tq, 128), lambda b, h, i: (b * nq + i, h))],
        out_specs=[pl.BlockSpec((tq, 256), lambda b, h, i: (b * nq + i, h)),
                   pl.BlockSpec((seq, 256), lambda b, h, i: (b, h)),
                   pl.BlockSpec((seq, 128), lambda b, h, i: (b, h))],
        out_shape=[_sds((n, HEADS * 128)), _sds((n, HEADS * 128)), _sds((n, HEADS * V_HEAD))],
        sem=("parallel", "parallel", "arbitrary"),
    )(q, k, v, o, do)


def _f_sg(pu, pv, lg, lb, bias, *ws):
    u, vv = _gelu(pu), _gelu(pv)
    mu = jnp.mean(vv, axis=-1, keepdims=True)
    d = vv - mu
    vv = d * lax.rsqrt(jnp.mean(d * d, axis=-1, keepdims=True) + LN_EPS) * lg + lb
    group = lax.broadcasted_iota(jnp.int32, (SG_CHUNK, SG_DIM), 1) // (SG_DIM // SG_GROUPS)
    mixed = bias
    for k, w in enumerate(ws):
        mixed = mixed + jnp.where(group == k, _mm(w, vv), 0.0)
    return (u * mixed,)


def _shift_mean(a, prev_row, next_row):
    t = a.shape[0]
    row = lax.broadcasted_iota(jnp.int32, a.shape, 0)
    prev = jnp.where(row == 0, prev_row, pltpu.roll(a, 1, 0))
    nxt = jnp.where(row == t - 1, next_row, pltpu.roll(a, t - 1, 0))
    return 0.5 * (prev + nxt)


def _halo_specs(tm, width, blk, nblk8):
    h = tm // 8
    return [pl.BlockSpec((tm, width), lambda i: (i, blk)),
            pl.BlockSpec((8, width), lambda i: (jnp.maximum(i * h - 1, 0), blk)),
            pl.BlockSpec((8, width), lambda i: (jnp.minimum((i + 1) * h, nblk8 - 1), blk))]


def _edge_rows(i, tm, seq, pv_ref, nx_ref, scale=None):
    first = (i * tm) % seq == 0
    last = ((i + 1) * tm) % seq == 0
    pv, nx = pv_ref[7:8, :], nx_ref[0:1, :]
    if scale is not None:
        pv, nx = pv * scale, nx * scale
    return jnp.where(first, 0.0, pv), jnp.where(last, 0.0, nx)


def _shift_fwd(p, mu, seq, name):
    n = p.shape[0]
    tm = min(256, seq)
    blk = O_RW // RW_W

    def body(x_ref, pv_ref, nx_ref, mu_ref, z_ref):
        x = x_ref[...]
        pv, nx = _edge_rows(pl.program_id(0), tm, seq, pv_ref, nx_ref)
        z_ref[...] = x + mu_ref[...] * (_shift_mean(x, pv, nx) - x)

    return _pc(
        body, name=name, grid=(n // tm,),
        in_specs=_halo_specs(tm, RW_W, blk, n // 8) + [pl.BlockSpec((1, RW_W), lambda i: (0, 0))],
        out_specs=pl.BlockSpec((tm, RW_W), lambda i: (i, 0)), out_shape=_sds((n, RW_W)), sem=("parallel",),
    )(p, p, p, mu)


def _shift_bwd(dz, p, mu, seq, name):
    n = p.shape[0]
    tm = min(256, seq)
    blk = O_RW // RW_W

    def body(dz_ref, dpv_ref, dnx_ref, x_ref, pv_ref, nx_ref, mu_ref, dx_ref, dmu_ref):
        i = pl.program_id(0)
        mu_v = mu_ref[...]
        dzv = dz_ref[...]
        m = dzv * mu_v
        mpv, mnx = _edge_rows(i, tm, seq, dpv_ref, dnx_ref, mu_v)
        dx_ref[...] = (dzv - m + _shift_mean(m, mpv, mnx)).astype(dx_ref.dtype)
        x = x_ref[...]
        pv, nx = _edge_rows(i, tm, seq, pv_ref, nx_ref)
        part = jnp.sum(dzv * (_shift_mean(x, pv, nx) - x), axis=0, keepdims=True)

        @pl.when(i == 0)
        def _():
            dmu_ref[...] = part

        @pl.when(i != 0)
        def _():
            dmu_ref[...] += part

    return _pc(
        body, name=name, grid=(n // tm,),
        in_specs=_halo_specs(tm, RW_W, 0, n // 8) + _halo_specs(tm, RW_W, blk, n // 8)
        + [pl.BlockSpec((1, RW_W), lambda i: (0, 0))],
        out_specs=[pl.BlockSpec((tm, RW_W), lambda i: (i, 0)), pl.BlockSpec((1, RW_W), lambda i: (0, 0))],
        out_shape=[_sds((n, RW_W), bf16), _sds((1, RW_W))], sem=("arbitrary",),
    )(dz, dz, dz, p, p, p, mu)


def _f_rw_pre(k, wl, al, gl, w0, a0, w2, a2, g2, k_k, k_a):
    w = w0 + _mm(jnp.tanh(wl), w2)
    lw = -jnp.exp(-_softplus(-w) - 0.5)
    a = _sigmoid(a0 + _mm(al, a2))
    g = _mm(_sigmoid(gl), g2)
    kkr = k * k_k
    kk = kkr / jnp.maximum(jnp.sqrt(_group_sum(kkr * kkr)), 1e-12)
    two = lambda t: jnp.concatenate([t, t], axis=1)
    kd = two(k) * (1.0 + (a - 1.0) * two(k_a))
    bd = two(kk) * a
    return lw, kd, kk, bd, g


def _f_rw_post(y0, y1, r, v, kd0, kd1, g, r_k, ln_g, ln_b):
    y = y0 + y1
    mean = _group_sum(y) * (1.0 / RW_HEAD)
    d = y - mean
    var = _group_sum(d * d) * (1.0 / RW_HEAD)
    yn = d * lax.rsqrt(var + GN_EPS) * ln_g + ln_b
    bonus = _group_sum(r * (kd0 + kd1) * r_k)
    return ((yn + bonus * v) * g,)


@jax.custom_vjp
def _tri_inv(a):
    c = a.shape[0]
    row = lax.broadcasted_iota(jnp.int32, (c, c), 0)
    col = lax.broadcasted_iota(jnp.int32, (c, c), 1)
    eye = (row == col).astype(f32)
    blk = lambda b: (row // b) == (col // b)
    ld = jnp.where(blk(8), a, 0.0)
    l2 = _bdot(ld, ld)
    l4 = _bdot(l2, l2)
    t = _bdot(_bdot(eye - ld, eye + l2), eye + l4)
    b = 8
    while b < c:
        off = jnp.where(blk(2 * b) & jnp.logical_not(blk(b)), a, 0.0)
        t = t - _bdot(_bdot(t, off), t)
        b *= 2
    return t


def _tri_inv_fwd(a):
    t = _tri_inv(a)
    return t, t


def _tri_inv_bwd(t, g):
    return (-_bdot(_bdot(t, g, TN), t, NT),)


_tri_inv.defvjp(_tri_inv_fwd, _tri_inv_bwd)


def _scan_chunk(s0, r, v, kk, lw, kd, bd, rev):
    c = r.shape[0]
    row = lax.broadcasted_iota(jnp.int32, (c, c), 0)
    col = lax.broadcasted_iota(jnp.int32, (c, c), 1)
    ahead = jnp.where(rev, col - row, row - col)
    before = ahead > 0
    incl = ahead >= 0
    lane = lax.broadcasted_iota(jnp.int32, (1, 128), 1)
    m0 = (lane < 64).astype(f32)
    masks = (m0, 1.0 - m0)
    lane_c = lax.broadcasted_iota(jnp.int32, (c, 128), 1) < 64
    bd_mask = ((lax.broadcasted_iota(jnp.int32, (128, 128), 0) // 64)
               == (lax.broadcasted_iota(jnp.int32, (128, 128), 1) // 64)).astype(f32)
    tot = jnp.sum(lw, axis=0, keepdims=True)
    lp = _dot(incl.astype(f32), lw, NN, HI) - 0.5 * tot
    eg, ieg = jnp.exp(lp), jnp.exp(-lp)
    rt, kt, bt, at = r * eg, kd * ieg, bd * ieg, kk * jnp.exp(lp - lw)
    etot = jnp.exp(0.5 * tot)
    si = s0 * etot
    pair = lambda x, y, mask: [jnp.where(mask, _mm_nt(x * mh, y), 0.0) for mh in masks]
    a_ab, a_ak, a_rb, a_rk = pair(at, bt, before), pair(at, kt, before), pair(rt, bt, incl), pair(rt, kt, incl)
    t = [_tri_inv(a) for a in a_ab]
    sel = lambda a, b: jnp.where(lane_c, a, b)
    x = _mm_nt(at, si) + sel(_mm(a_ak[0], v), _mm(a_ak[1], v))
    u = -sel(_mm(t[0], x), _mm(t[1], x))
    y = _mm_nt(rt, si) + sel(_mm(a_rb[0], u) + _mm(a_rk[0], v), _mm(a_rb[1], u) + _mm(a_rk[1], v))
    se = (si + (_mm_tn(u, bt) + _mm_tn(v, kt)) * bd_mask) * etot
    return y, se


def _scan_specs(nc, bsz):
    cc = lambda d, c: jnp.where(d == 0, c, nc - 1 - c)
    rowblk = lambda d, b, c: b * nc + cc(d, c)
    zspec = lambda off: pl.BlockSpec((CHUNK, 128), lambda d, b, h, c: (rowblk(d, b, c), off + h))
    dspec = pl.BlockSpec((CHUNK, 128), lambda d, b, h, c: (rowblk(d, b, c), d * 4 + h))
    yspec = pl.BlockSpec((None, CHUNK, 128), lambda d, b, h, c: (d, rowblk(d, b, c), h))
    sspec = pl.BlockSpec((None, 128, 128), lambda d, b, h, c: (((d * bsz + b) * 4 + h) * nc + cc(d, c), 0, 0))
    return zspec, dspec, yspec, sspec


def _scan_fwd(z, lw, kd, kk, bd, bsz, seq, name):
    n = z.shape[0]
    nc = seq // CHUNK
    zspec, dspec, yspec, sspec = _scan_specs(nc, bsz)

    def body(r_ref, v_ref, kk_ref, lw_ref, kd_ref, bd_ref, y_ref, s_ref, st):
        @pl.when(pl.program_id(3) == 0)
        def _():
            st[...] = jnp.zeros_like(st)

        s0 = st[...]
        s_ref[...] = s0
        y, se = _scan_chunk(s0, r_ref[...], v_ref[...], kk_ref[...], lw_ref[...], kd_ref[...], bd_ref[...],
                            pl.program_id(0) == 1)
        y_ref[...] = y
        st[...] = se

    return _pc(
        body, name=name, grid=(2, bsz, 4, nc),
        in_specs=[zspec(0), zspec(8), zspec(0), dspec, dspec, dspec],
        out_specs=[yspec, sspec],
        out_shape=[_sds((2, n, RW_DIM)), _sds((2 * bsz * 4 * nc, 128, 128))],
        scratch=[pltpu.VMEM((128, 128), f32)], sem=("parallel", "parallel", "parallel", "arbitrary"),
    )(z, z, kk, lw, kd, bd)


def _scan_bwd(z, lw, kd, kk, bd, s_in, dy, bsz, seq, name):
    n = z.shape[0]
    nc = seq // CHUNK
    zspec, dspec, yspec, sspec = _scan_specs(nc, bsz)
    flip = lambda spec: pl.BlockSpec(spec.block_shape, lambda d, b, h, c, f=spec.index_map: f(d, b, h, nc - 1 - c))
    zspec_f = lambda off: flip(zspec(off))
    dspec, yspec, sspec = flip(dspec), flip(yspec), flip(sspec)

    def body(r_ref, v_ref, kk_ref, lw_ref, kd_ref, bd_ref, s_ref, dy_ref,
             dr_ref, dv_ref, dkk_ref, dlw_ref, dkd_ref, dbd_ref, dst):
        @pl.when(pl.program_id(3) == 0)
        def _():
            dst[...] = jnp.zeros_like(dst)

        rev = pl.program_id(0) == 1
        _, vjp = jax.vjp(functools.partial(_scan_chunk, rev=rev), s_ref[...], r_ref[...], v_ref[...], kk_ref[...],
                         lw_ref[...], kd_ref[...], bd_ref[...])
        ds, dr, dv, dkk, dlw, dkd, dbd = vjp((dy_ref[...], dst[...]))
        dst[...] = ds
        dr_ref[...] = dr
        dv_ref[...] = dv
        dkk_ref[...] = dkk
        dlw_ref[...] = dlw
        dkd_ref[...] = dkd
        dbd_ref[...] = dbd

    return _pc(
        body, name=name, grid=(2, bsz, 4, nc),
        in_specs=[zspec_f(0), zspec_f(8), zspec_f(0), dspec, dspec, dspec, sspec, yspec],
        out_specs=[yspec, yspec, yspec, dspec, dspec, dspec],
        out_shape=[_sds((2, n, RW_DIM))] * 3 + [_sds((n, 2 * RW_DIM))] * 3,
        scratch=[pltpu.VMEM((128, 128), f32)], sem=("parallel", "parallel", "parallel", "arbitrary"),
    )(z, z, kk, lw, kd, bd, s_in, dy)


def _merge_fwd(x2, p, ya, yb, yc, gb, wb, wo, name):
    n = x2.shape[0]
    tm = min(256, n)

    def body(x_ref, pg_ref, ya_ref, yb_ref, yc_ref, gb_ref, wb_ref, wo_ref, o_ref):
        gates = _sigmoid(pg_ref[...] + gb_ref[...])
        merged = jnp.zeros((tm, D), f32)
        for k, y_ref in enumerate((ya_ref, yb_ref, yc_ref)):
            merged += gates[:, k * D:(k + 1) * D] * _bdot(y_ref[...], wb_ref[k])
        o_ref[...] = x_ref[...] + _bdot(merged, wo_ref[...])

    row = lambda w, b=0: pl.BlockSpec((tm, w), lambda i, b=b: (i, b))
    return _pc(
        body, name=name, grid=(n // tm,),
        in_specs=[row(D), row(3 * D, O_GATE // (3 * D)), row(512), row(512), row(512),
                  pl.BlockSpec((1, 3 * D), lambda i: (0, 0)), pl.BlockSpec((3, 512, D), lambda i: (0, 0, 0)),
                  pl.BlockSpec((D, D), lambda i: (0, 0))],
        out_specs=row(D), out_shape=_sds((n, D)), sem=("parallel",),
    )(x2, p, ya, yb, yc, gb, wb, wo)


def _merge_bwd(dx1, p, ya, yb, yc, gb, wb, wo, name):
    n = dx1.shape[0]
    tm = min(256, n)

    def body(dx_ref, pg_ref, ya_ref, yb_ref, yc_ref, gb_ref, wb_ref, wo_ref,
             dpg_ref, dya_ref, dyb_ref, dyc_ref, dt_ref, mg_ref, dgb_ref):
        gates = _sigmoid(pg_ref[...] + gb_ref[...])
        dmerged = _bdot(dx_ref[...], wo_ref[...], NT)
        merged = jnp.zeros((tm, D), f32)
        dpg = []
        for k, (y_ref, dy_ref) in enumerate(((ya_ref, dya_ref), (yb_ref, dyb_ref), (yc_ref, dyc_ref))):
            gk = gates[:, k * D:(k + 1) * D]
            tk = _bdot(y_ref[...], wb_ref[k])
            merged += gk * tk
            dpg.append(dmerged * tk * gk * (1.0 - gk))
            dtk = dmerged * gk
            dt_ref[:, k * D:(k + 1) * D] = dtk.astype(bf16)
            dy_ref[...] = _bdot(dtk, wb_ref[k], NT)
        dpg = jnp.concatenate(dpg, axis=1)
        dpg_ref[...] = dpg.astype(bf16)
        mg_ref[...] = merged.astype(bf16)
        part = jnp.sum(dpg, axis=0, keepdims=True)

        @pl.when(pl.program_id(0) == 0)
        def _():
            dgb_ref[...] = part

        @pl.when(pl.program_id(0) != 0)
        def _():
            dgb_ref[...] += part

    row = lambda w, b=0: pl.BlockSpec((tm, w), lambda i, b=b: (i, b))
    return _pc(
        body, name=name, grid=(n // tm,),
        in_specs=[row(D), row(3 * D, O_GATE // (3 * D)), row(512), row(512), row(512),
                  pl.BlockSpec((1, 3 * D), lambda i: (0, 0)), pl.BlockSpec((3, 512, D), lambda i: (0, 0, 0)),
                  pl.BlockSpec((D, D), lambda i: (0, 0))],
        out_specs=[row(3 * D), row(512), row(512), row(512), row(3 * D), row(D),
                   pl.BlockSpec((1, 3 * D), lambda i: (0, 0))],
        out_shape=[_sds((n, 3 * D), bf16), _sds((n, 512)), _sds((n, 512)), _sds((n, 512)), _sds((n, 3 * D), bf16),
                   _sds((n, D), bf16), _sds((1, 3 * D))],
        sem=("arbitrary",),
    )(dx1, p, ya, yb, yc, gb, wb, wo)


FF_T = 1408


def _ffn_fwd(x1, g, wg, wu, wd, name):
    n = x1.shape[0]
    tm = min(512, n)
    nf = D_FF // FF_T

    def body(x_ref, g_ref, wg_ref, wu_ref, wd_ref, o_ref, hs):
        j = pl.program_id(1)

        @pl.when(j == 0)
        def _():
            hs[...] = _rms(x_ref[...], g_ref[...]).astype(bf16)
            o_ref[...] = x_ref[...]

        a = _dot(hs[...], wg_ref[...], NN)
        b = _dot(hs[...], wu_ref[...], NN)
        o_ref[...] += _bdot(a * _sigmoid(a) * b, wd_ref[...])

    return _pc(
        body, name=name, grid=(n // tm, nf),
        in_specs=[pl.BlockSpec((tm, D), lambda i, j: (i, 0)), pl.BlockSpec((1, D), lambda i, j: (0, 0)),
                  pl.BlockSpec((D, FF_T), lambda i, j: (0, j)), pl.BlockSpec((D, FF_T), lambda i, j: (0, j)),
                  pl.BlockSpec((FF_T, D), lambda i, j: (j, 0))],
        out_specs=pl.BlockSpec((tm, D), lambda i, j: (i, 0)), out_shape=_sds((n, D)),
        scratch=[pltpu.VMEM((tm, D), bf16)], sem=("parallel", "arbitrary"),
    )(x1, g, wg, wu, wd)


def _ffn_bwd(dx2, x1, g, wg, wu, wd, name):
    n = x1.shape[0]
    tm = min(512, n)
    nf = D_FF // FF_T

    def body(dx_ref, x_ref, g_ref, wg_ref, wu_ref, wd_ref, dx1_ref, dg_ref, h_ref, da_ref, db_ref, hm_ref, acc):
        i, j = pl.program_id(0), pl.program_id(1)

        @pl.when(j == 0)
        def _():
            h_ref[...] = _rms(x_ref[...], g_ref[...]).astype(bf16)
            acc[...] = jnp.zeros_like(acc)

        @pl.when((i == 0) & (j == 0))
        def _():
            dg_ref[...] = jnp.zeros_like(dg_ref)

        h = h_ref[...]
        a = _dot(h, wg_ref[...], NN)
        b = _dot(h, wu_ref[...], NN)
        sg = _sigmoid(a)
        s = a * sg
        dhm = _bdot(dx_ref[...], wd_ref[...], NT)
        da = (dhm * b * (sg * (1.0 + a * (1.0 - sg)))).astype(bf16)
        db = (dhm * s).astype(bf16)
        da_ref[...] = da
        db_ref[...] = db
        hm_ref[...] = (s * b).astype(bf16)
        acc[...] += _dot(da, wg_ref[...], NT) + _dot(db, wu_ref[...], NT)

        @pl.when(j == nf - 1)
        def _():
            _, vjp = jax.vjp(_rms, x_ref[...], g_ref[...])
            dx, dg = vjp(acc[...])
            dx1_ref[...] = dx_ref[...] + dx
            dg_ref[...] += dg

    rowf = pl.BlockSpec((tm, FF_T), lambda i, j: (i, j))
    rowd = pl.BlockSpec((tm, D), lambda i, j: (i, 0))
    vec = pl.BlockSpec((1, D), lambda i, j: (0, 0))
    return _pc(
        body, name=name, grid=(n // tm, nf),
        in_specs=[rowd, rowd, vec, pl.BlockSpec((D, FF_T), lambda i, j: (0, j)),
                  pl.BlockSpec((D, FF_T), lambda i, j: (0, j)), pl.BlockSpec((FF_T, D), lambda i, j: (j, 0))],
        out_specs=[rowd, vec, rowd, rowf, rowf, rowf],
        out_shape=[_sds((n, D)), _sds((1, D)), _sds((n, D), bf16), _sds((n, D_FF), bf16), _sds((n, D_FF), bf16),
                   _sds((n, D_FF), bf16)],
        scratch=[pltpu.VMEM((tm, D), f32)], sem=("arbitrary", "arbitrary"),
    )(dx2, x1, g, wg, wu, wd)


def _loss_head(x2, g, tgt, name):
    n = x2.shape[0]
    tm = min(512, n)

    def f(x, gg, t):
        e = _rms(x, gg) - t
        return 0.5 * jnp.sum(jnp.mean(e * e, axis=-1, keepdims=True))

    def body(x_ref, g_ref, t_ref, l_ref, dx_ref, dg_ref):
        val, vjp = jax.vjp(f, x_ref[...], g_ref[...], t_ref[...])
        dx, dg, _ = vjp(jnp.ones((), f32))
        dx_ref[...] = dx

        @pl.when(pl.program_id(0) == 0)
        def _():
            l_ref[...] = jnp.zeros_like(l_ref)
            dg_ref[...] = jnp.zeros_like(dg_ref)

        l_ref[...] += val
        dg_ref[...] += dg

    rowd = pl.BlockSpec((tm, D), lambda i: (i, 0))
    return _pc(
        body, name=name, grid=(n // tm,),
        in_specs=[rowd, pl.BlockSpec((1, D), lambda i: (0, 0)), rowd],
        out_specs=[pl.BlockSpec((8, 128), lambda i: (0, 0)), rowd, pl.BlockSpec((1, D), lambda i: (0, 0))],
        out_shape=[_sds((8, 128)), _sds((n, D)), _sds((1, D))], sem=("arbitrary",),
    )(x2, g, tgt)


def _adamw(w, g, m, v, name):
    r, c = w.shape
    tr = r
    for cand in (1024, 512, 256, 128, 64, 32, 16, 8):
        if r % cand == 0 and cand * c * 4 <= 2 * 1024 * 1024:
            tr = cand
            break

    def body(w_ref, g_ref, m_ref, v_ref, d_ref, nm_ref, nv_ref):
        gg = g_ref[...]
        nm = B1 * m_ref[...] + (1.0 - B1) * gg
        nv = B2 * v_ref[...] + (1.0 - B2) * (gg * gg)
        m_hat = nm / (1.0 - B1 ** STEP)
        v_hat = nv / (1.0 - B2 ** STEP)
        d_ref[...] = -LR * (m_hat / (jnp.sqrt(v_hat) + EPS) + WD * w_ref[...])
        nm_ref[...] = nm
        nv_ref[...] = nv

    spec = pl.BlockSpec((tr, c), lambda i: (i, 0))
    return _pc(body, name=name, grid=(r // tr,), in_specs=[spec] * 4, out_specs=[spec] * 3,
               out_shape=[_sds((r, c))] * 3, sem=("parallel",))(w, g, m, v)


def _peers():
    x, y, c = lax.axis_index("x"), lax.axis_index("y"), lax.axis_index("c")
    me = 4 * x + 2 * y + c
    peers = []
    for k in range(1, N_DEV):
        fx, fy, fc = (k >> 2) & 1, (k >> 1) & 1, k & 1
        peers.append(((1 - x) if fx else x, (1 - y) if fy else y, (1 - c) if fc else c))
    return me, peers


def _all_gather(parts, name):
    n = len(parts)

    def body(*refs):
        ins, outs = refs[:n], refs[n:2 * n]
        send_sems, recv_sems, local_sems = refs[2 * n:]
        me, peers = _peers()
        copies = []
        for a in range(n):
            loc = pltpu.make_async_copy(ins[a], outs[a].at[me], local_sems.at[a])
            loc.start()
            copies.append(loc)
            for k, peer in enumerate(peers):
                cp = pltpu.make_async_remote_copy(
                    src_ref=ins[a], dst_ref=outs[a].at[me], send_sem=send_sems.at[a * (N_DEV - 1) + k], recv_sem=recv_sems.at[a * (N_DEV - 1) + k],
                    device_id=peer, device_id_type=pl.DeviceIdType.MESH)
                cp.start()
                copies.append(cp)
        for cp in copies:
            cp.wait()

    anyspec = pl.BlockSpec(memory_space=pl.ANY)
    return _pc(
        body, name=name, in_specs=[anyspec] * n, out_specs=[anyspec] * n,
        out_shape=[_sds((N_DEV,) + p.shape, p.dtype) for p in parts],
        scratch=[pltpu.SemaphoreType.DMA((n * (N_DEV - 1),)), pltpu.SemaphoreType.DMA((n * (N_DEV - 1),)),
                 pltpu.SemaphoreType.DMA((n,))],
    )(*parts)


def _exchange_grads(shard, rep, name):
    def body(sh_ref, rep_ref, osh_ref, orep_ref, send_sems, recv_sems, local_sems):
        me, peers = _peers()
        copies = [pltpu.make_async_copy(sh_ref.at[me], osh_ref.at[me], local_sems.at[0]),
                  pltpu.make_async_copy(rep_ref, orep_ref.at[me], local_sems.at[1])]
        for cp in copies:
            cp.start()
        for k, peer in enumerate(peers):
            dev = 4 * peer[0] + 2 * peer[1] + peer[2]
            a = pltpu.make_async_remote_copy(
                src_ref=sh_ref.at[dev], dst_ref=osh_ref.at[me], send_sem=send_sems.at[k],
                recv_sem=recv_sems.at[k], device_id=peer, device_id_type=pl.DeviceIdType.MESH)
            b = pltpu.make_async_remote_copy(
                src_ref=rep_ref, dst_ref=orep_ref.at[me], send_sem=send_sems.at[N_DEV - 1 + k],
                recv_sem=recv_sems.at[N_DEV - 1 + k], device_id=peer, device_id_type=pl.DeviceIdType.MESH)
            a.start()
            b.start()
            copies += [a, b]
        for cp in copies:
            cp.wait()

    anyspec = pl.BlockSpec(memory_space=pl.ANY)
    return _pc(
        body, name=name, in_specs=[anyspec] * 2, out_specs=[anyspec] * 2,
        out_shape=[_sds(shard.shape), _sds((N_DEV,) + rep.shape)],
        scratch=[pltpu.SemaphoreType.DMA((2 * (N_DEV - 1),)), pltpu.SemaphoreType.DMA((2 * (N_DEV - 1),)),
                 pltpu.SemaphoreType.DMA((2,))],
    )(shard, rep)


def _sum_slots(a, name):
    _, r, c = a.shape
    tr = r
    for cand in (512, 256, 128, 64, 32, 16, 8):
        if r % cand == 0:
            tr = cand
            break

    def body(a_ref, o_ref):
        acc = a_ref[0]
        for k in range(1, N_DEV):
            acc = acc + a_ref[k]
        o_ref[...] = acc

    return _pc(body, name=name, grid=(r // tr,), in_specs=[pl.BlockSpec((N_DEV, tr, c), lambda i: (0, i, 0))],
               out_specs=pl.BlockSpec((tr, c), lambda i: (i, 0)), out_shape=_sds((r, c)), sem=("parallel",))(a)


SHARDED = {"w_in": 2, "gate_b": 2, "w_uq": 2, "w_ukv": 2, "rw_w0": 2, "rw_w2": 3, "rw_a0": 2, "rw_a2": 3, "rw_g2": 2,
           "w_branch": 3, "w_out": 1, "w_ffn_gate": 2, "w_ffn_up": 2, "w_ffn_down": 1}
GATHER_F32 = ("gate_b", "rw_w0", "rw_a0")
REPLICATED = ("attn_norm_g", "q_norm_g", "kv_norm_g", "sg_ln_g", "sg_ln_b", "sg_w", "sg_b", "rw_mu", "rw_k_k", "rw_k_a",
              "rw_r_k", "rw_ln_g", "rw_ln_b", "ffn_norm_g", "final_norm_g")
WEIGHTS = ("attn_norm_g", "w_in", "gate_b", "q_norm_g", "w_uq", "kv_norm_g", "w_ukv", "sg_ln_g", "sg_ln_b", "sg_w", "sg_b",
           "rw_mu", "rw_w0", "rw_w2", "rw_a0", "rw_a2", "rw_g2", "rw_k_k", "rw_k_a", "rw_r_k", "rw_ln_g", "rw_ln_b",
           "w_branch", "w_out", "ffn_norm_g", "w_ffn_gate", "w_ffn_up", "w_ffn_down", "final_norm_g")


def _pack_rows(flat, cols):
    n = flat.shape[-1]
    per = 16 * cols
    tot = -(-n // per) * per
    flat = jnp.pad(flat, [(0, 0)] * (flat.ndim - 1) + [(0, tot - n)])
    return flat.reshape(flat.shape[:-1] + (tot // cols, cols))


def _pack_local(blocks, names, dtype):
    return _pack_rows(jnp.concatenate([blocks[k].astype(dtype).reshape(-1) for k in names]), 512)


def _unpack_gathered(g, blocks, names):
    flat = g.reshape(N_DEV, -1)
    out, off = {}, 0
    for k in names:
        shp = blocks[k].shape
        sz = blocks[k].size
        ax = SHARDED[k]
        seg = flat[:, off:off + sz].reshape((N_DEV,) + shp)
        seg = jnp.moveaxis(seg, 0, ax)
        out[k] = seg.reshape(shp[:ax] + (N_DEV * shp[ax],) + shp[ax + 1:])
        off += sz
    return out


def _split_for_devices(full, ax):
    shp = full.shape
    t = full.reshape(shp[:ax] + (N_DEV, shp[ax] // N_DEV) + shp[ax + 1:])
    return jnp.moveaxis(t, ax, 0).reshape(N_DEV, -1)


def _w_in_padded(w):
    z = lambda n: jnp.zeros((w.shape[0], n), w.dtype)
    q, ckv, kr = w[:, 0:384], w[:, 384:640], w[:, 640:672]
    sg, rw, gate = w[:, 672:1696], w[:, 1696:3616], w[:, 3616:6688]
    return jnp.concatenate([gate, sg, rw, z(128), ckv, z(64), kr, z(32), q, z(P_W - O_MLA - MLA_W)], axis=1)


def _w_in_unpadded(g):
    return jnp.concatenate([g[:, O_Q:O_Q + 384], g[:, O_CKV:O_CKV + 256], g[:, O_SLAB + 64:O_SLAB + 96],
                            g[:, O_SG:O_SG + 1024], g[:, O_RW:O_RW + 1920], g[:, O_GATE:O_GATE + 3072]], axis=1)


def _layer_weights(full, rep, l):
    w = {}
    w["w_in"] = _w_in_padded(full["w_in"][l])
    uq = full["w_uq"][l].reshape(Q_LORA, HEADS, QK_NOPE + QK_ROPE)
    w["wq"] = jnp.pad(uq, ((0, 0), (0, 0), (0, 32))).reshape(Q_LORA, HEADS * 128).astype(f32)
    ukv = full["w_ukv"][l].reshape(KV_LORA, HEADS, QK_NOPE + V_HEAD)
    wk = jnp.pad(ukv[:, :, :QK_NOPE], ((0, 0), (0, 0), (0, 64))).reshape(KV_LORA, HEADS * 128)
    w["wk"], w["wv"] = wk.astype(f32), ukv[:, :, QK_NOPE:].reshape(KV_LORA, HEADS * V_HEAD).astype(f32)
    bdiag = lambda t: jnp.concatenate([jnp.concatenate([t[0], jnp.zeros_like(t[0])], axis=1),
                                       jnp.concatenate([jnp.zeros_like(t[1]), t[1]], axis=1)], axis=0).astype(f32)
    w["w2"], w["a2"] = bdiag(full["rw_w2"][l]), bdiag(full["rw_a2"][l])
    w["g2"] = full["rw_g2"][l].astype(f32)
    w["w0"], w["a0"] = full["rw_w0"][l].reshape(1, 2 * RW_DIM), full["rw_a0"][l].reshape(1, 2 * RW_DIM)
    w["gate_b"] = full["gate_b"][l].reshape(1, 3 * D)
    w["wb"], w["wo"] = full["w_branch"][l], full["w_out"][l]
    w["wg"], w["wu"], w["wd"] = full["w_ffn_gate"][l], full["w_ffn_up"][l], full["w_ffn_down"][l]
    row = lambda a: a.reshape(1, -1)
    for k in ("attn_norm_g", "q_norm_g", "kv_norm_g", "sg_ln_g", "sg_ln_b", "rw_k_k", "rw_k_a", "rw_ln_g", "rw_ln_b",
              "ffn_norm_g"):
        w[k] = row(rep[k][l])
    w["r_k"] = row(rep["rw_r_k"][l])
    w["mu"] = jnp.pad(row(rep["rw_mu"][l]), ((0, 0), (0, RW_W - 1920)))
    w["sg_w"] = [rep["sg_w"][l, k] for k in range(SG_GROUPS)]
    w["sg_bias"] = jnp.repeat(rep["sg_b"][l].T, SG_DIM // SG_GROUPS, axis=1)
    return w


def _layer_fwd(x2, w, tabs, bsz, seq, l):
    nm = lambda s: f"l{l}_{s}"
    n = x2.shape[0]
    tm = min(256, n)
    p, h = _inproj_fwd(x2, w["attn_norm_g"], w["w_in"], nm("inproj"))
    mla_rows = [(p, 256, O_CKV // 256), (p, 128, O_SLAB // 128), (p, 384, O_Q // 384), (tabs[0], 128, 0), (tabs[1], 128, 0)]
    mla_w = [w["q_norm_g"], w["kv_norm_g"], w["wq"], w["wk"], w["wv"]]
    q, k, v = _rowwise_fwd(nm("mla_proj"), _f_mla_proj, mla_rows, mla_w, [(1024, bf16), (1024, bf16), (512, bf16)], tm)
    ya = _attn_fwd(q, k, v, bsz, seq, nm("attn"))
    sg_rows = [(p, SG_DIM, O_SG // SG_DIM), (p, SG_DIM, O_SG // SG_DIM + 1)]
    sg_w = [w["sg_ln_g"], w["sg_ln_b"], w["sg_bias"]] + w["sg_w"]
    (yb,) = _rowwise_fwd(nm("sg"), _f_sg, sg_rows, sg_w, [(SG_DIM, f32)], SG_CHUNK)
    z = _shift_fwd(p, w["mu"], seq, nm("shift"))
    pre_rows = [(z, 512, 1), (z, 128, 12), (z, 128, 13), (z, 128, 14)]
    pre_w = [w["w0"], w["a0"], w["w2"], w["a2"], w["g2"], w["rw_k_k"], w["rw_k_a"]]
    lw, kd, kk, bd, g = _rowwise_fwd(nm("rw_pre"), _f_rw_pre, pre_rows, pre_w,
                                     [(1024, f32), (1024, f32), (512, f32), (1024, f32), (512, f32)], tm)
    y, s_in = _scan_fwd(z, lw, kd, kk, bd, bsz, seq, nm("scan"))
    post_rows = [(y[0], 512, 0), (y[1], 512, 0), (z, 512, 0), (z, 512, 2), (kd, 512, 0), (kd, 512, 1), (g, 512, 0)]
    post_w = [w["r_k"], w["rw_ln_g"], w["rw_ln_b"]]
    (yc,) = _rowwise_fwd(nm("rw_post"), _f_rw_post, post_rows, post_w, [(512, f32)], tm)
    x1 = _merge_fwd(x2, p, ya, yb, yc, w["gate_b"], w["wb"], w["wo"], nm("merge"))
    x3 = _ffn_fwd(x1, w["ffn_norm_g"], w["wg"], w["wu"], w["wd"], nm("ffn"))
    saved = dict(x=x2, p=p, h=h, q=q, k=k, v=v, ya=ya, yb=yb, z=z, lw=lw, kd=kd, kk=kk, bd=bd, g=g, y=y, s_in=s_in, yc=yc,
                 x1=x1, mla_rows=mla_rows, mla_w=mla_w, sg_rows=sg_rows, sg_w=sg_w, pre_rows=pre_rows, pre_w=pre_w,
                 post_rows=post_rows, post_w=post_w)
    return x3, saved


def _layer_bwd(dx3, w, sv, bsz, seq, l):
    nm = lambda s: f"l{l}_{s}_bwd"
    n = dx3.shape[0]
    tm = min(256, n)
    g = {}
    dx1, g["ffn_norm_g"], h2, da, db, hm = _ffn_bwd(dx3, sv["x1"], w["ffn_norm_g"], w["wg"], w["wu"], w["wd"], nm("ffn"))
    g["wg"] = _matmul_tn(h2, da, nm("wg"))
    g["wu"] = _matmul_tn(h2, db, nm("wu"))
    g["wd"] = _matmul_tn(hm, dx3.astype(bf16), nm("wd"))
    dpg, dya, dyb, dyc, dt, mg, g["gate_b"] = _merge_bwd(dx1, sv["p"], sv["ya"], sv["yb"], sv["yc"], w["gate_b"], w["wb"],
                                                         w["wo"], nm("merge"))
    g["wo"] = _matmul_tn(mg, dx1.astype(bf16), nm("wo"))
    ys = (sv["ya"], sv["yb"], sv["yc"])
    g["wb"] = jnp.stack([_matmul_tn(ys[k].astype(bf16), dt[:, k * D:(k + 1) * D], nm(f"wb{k}")) for k in range(3)])
    (dy0, dy1, dr_p, dv_p, dkd0, dkd1, dg_), (g["r_k"], g["rw_ln_g"], g["rw_ln_b"]) = _rowwise_bwd(
        nm("rw_post"), _f_rw_post, sv["post_rows"], sv["post_w"], [(dyc, 512, 0)], tm, [f32] * 7)
    dkd_p = jnp.concatenate([dkd0, dkd1], axis=1)
    dr_s, dv_s, dkk_s, dlw, dkd_s, dbd = _scan_bwd(sv["z"], sv["lw"], sv["kd"], sv["kk"], sv["bd"], sv["s_in"],
                                                   jnp.stack([dy0, dy1]), bsz, seq, nm("scan"))
    pre_cots = [(dlw, 1024, 0), (dkd_s + dkd_p, 1024, 0), (dkk_s[0] + dkk_s[1], 512, 0), (dbd, 1024, 0), (dg_, 512, 0)]
    (dk, dwl, dal, dgl), (g["w0"], g["a0"], g["w2"], g["a2"], g["g2"], g["rw_k_k"], g["rw_k_a"]) = _rowwise_bwd(
        nm("rw_pre"), _f_rw_pre, sv["pre_rows"], sv["pre_w"], pre_cots, tm, [f32] * 4)
    dz = jnp.concatenate([dr_s[0] + dr_s[1] + dr_p, dk, dv_s[0] + dv_s[1] + dv_p, dwl, dal, dgl,
                          jnp.zeros((n, RW_W - 1920), f32)], axis=1)
    dp_rw, g["mu"] = _shift_bwd(dz, sv["p"], w["mu"], seq, nm("shift"))
    (dp_su, dp_sv), (g["sg_ln_g"], g["sg_ln_b"], g["sg_bias"], *sgw) = _rowwise_bwd(
        nm("sg"), _f_sg, sv["sg_rows"], sv["sg_w"], [(dyb, SG_DIM, 0)], SG_CHUNK, [bf16, bf16])
    g["sg_w"] = jnp.stack(sgw)
    dq, dk_, dv_ = _attn_bwd(sv["q"], sv["k"], sv["v"], sv["ya"], dya, bsz, seq, nm("attn"))
    (dp_ckv, dp_slab, dp_q), (g["q_norm_g"], g["kv_norm_g"], g["wq"], g["wk"], g["wv"]) = _rowwise_bwd(
        nm("mla_proj"), _f_mla_proj, sv["mla_rows"], sv["mla_w"], [(dq, 1024, 0), (dk_, 1024, 0), (dv_, 512, 0)], tm,
        [bf16, bf16, bf16, None, None])
    dp = jnp.concatenate([dpg, dp_su, dp_sv, dp_rw, dp_ckv, dp_slab, dp_q, jnp.zeros((n, P_W - O_MLA - MLA_W), bf16)],
                         axis=1)
    g["w_in"] = _matmul_tn(sv["h"], dp, nm("w_in"))
    dx, g["attn_norm_g"] = _norm_matmul_bwd(dp, w["w_in"], sv["x"], w["attn_norm_g"], dx1, nm("inproj"))
    return dx, g


def _layer_grads_to_full(g):
    o = {}
    o["w_in"] = _w_in_unpadded(g["w_in"])
    o["w_uq"] = g["wq"].reshape(Q_LORA, HEADS, 128)[:, :, :QK_NOPE + QK_ROPE].reshape(Q_LORA, -1)
    gk = g["wk"].reshape(KV_LORA, HEADS, 128)[:, :, :QK_NOPE]
    gv = g["wv"].reshape(KV_LORA, HEADS, V_HEAD)
    o["w_ukv"] = jnp.concatenate([gk, gv], axis=2).reshape(KV_LORA, -1)
    unb = lambda t: jnp.stack([t[:LORA, :RW_DIM], t[LORA:, RW_DIM:]])
    o["rw_w2"], o["rw_a2"], o["rw_g2"] = unb(g["w2"]), unb(g["a2"]), g["g2"]
    o["rw_w0"], o["rw_a0"] = g["w0"].reshape(2, RW_DIM), g["a0"].reshape(2, RW_DIM)
    o["gate_b"] = g["gate_b"].reshape(3, D)
    o["w_branch"], o["w_out"] = g["wb"], g["wo"]
    o["w_ffn_gate"], o["w_ffn_up"], o["w_ffn_down"] = g["wg"], g["wu"], g["wd"]
    for k in ("attn_norm_g", "q_norm_g", "kv_norm_g", "sg_ln_g", "sg_ln_b", "rw_k_k", "rw_k_a", "rw_ln_g", "rw_ln_b",
              "ffn_norm_g"):
        o[k] = g[k].reshape(-1)
    o["rw_r_k"] = g["r_k"].reshape(HEADS, RW_HEAD)
    o["rw_mu"] = g["mu"].reshape(-1)[:1920]
    o["sg_w"] = g["sg_w"]
    o["sg_b"] = g["sg_bias"].reshape(SG_CHUNK, SG_GROUPS, SG_DIM // SG_GROUPS).sum(axis=2).T
    return o


def _rope_tables(positions):
    inv = 1.0 / (10000.0 ** (jnp.arange(0, QK_ROPE, 2, dtype=f32) / QK_ROPE))
    ang = positions.astype(f32)[:, None] * inv
    cos, sin = jnp.cos(ang), jnp.sin(ang)
    n = positions.shape[0]
    c = jnp.concatenate([jnp.ones((n, 64), f32), cos, cos, jnp.zeros((n, 32), f32)], axis=1)
    s = jnp.concatenate([jnp.zeros((n, 64), f32), -sin, sin, jnp.zeros((n, 32), f32)], axis=1)
    return c, s


def _local_step(x, positions, full, rep, loss_target):
    bsz, seq, _ = x.shape
    n = bsz * seq
    depth = rep["attn_norm_g"].shape[0]
    x2 = x.reshape(n, D)
    tabs = _rope_tables(positions.reshape(n))
    ws, saved = [], []
    for l in range(depth):
        w = _layer_weights(full, rep, l)
        x2, sv = _layer_fwd(x2, w, tabs, bsz, seq, l)
        ws.append(w)
        saved.append(sv)
    loss, dx, dgf = _loss_head(x2, rep["final_norm_g"].reshape(1, D), loss_target.reshape(n, D), "loss_head")
    per_layer = [None] * depth
    for l in reversed(range(depth)):
        dx, g = _layer_bwd(dx, ws[l], saved[l], bsz, seq, l)
        per_layer[l] = _layer_grads_to_full(g)
    grads = {k: jnp.stack([per_layer[l][k] for l in range(depth)]) for k in per_layer[0]}
    grads["final_norm_g"] = dgf.reshape(D)
    return loss[0, 0], dx.reshape(bsz, seq, D), grads


def kernel(x, positions, attn_norm_g, w_in, gate_b, q_norm_g, w_uq, kv_norm_g, w_ukv, sg_ln_g, sg_ln_b, sg_w, sg_b, rw_mu, rw_w0, rw_w2, rw_a0, rw_a2, rw_g2, rw_k_k, rw_k_a, rw_r_k, rw_ln_g, rw_ln_b, w_branch, w_out, ffn_norm_g, w_ffn_gate, w_ffn_up, w_ffn_down, final_norm_g, loss_target, m_attn_norm_g, m_w_in, m_gate_b, m_q_norm_g, m_w_uq, m_kv_norm_g, m_w_ukv, m_sg_ln_g, m_sg_ln_b, m_sg_w, m_sg_b, m_rw_mu, m_rw_w0, m_rw_w2, m_rw_a0, m_rw_a2, m_rw_g2, m_rw_k_k, m_rw_k_a, m_rw_r_k, m_rw_ln_g, m_rw_ln_b, m_w_branch, m_w_out, m_ffn_norm_g, m_w_ffn_gate, m_w_ffn_up, m_w_ffn_down, m_final_norm_g, v_attn_norm_g, v_w_in, v_gate_b, v_q_norm_g, v_w_uq, v_kv_norm_g, v_w_ukv, v_sg_ln_g, v_sg_ln_b, v_sg_w, v_sg_b, v_rw_mu, v_rw_w0, v_rw_w2, v_rw_a0, v_rw_a2, v_rw_g2, v_rw_k_k, v_rw_k_a, v_rw_r_k, v_rw_ln_g, v_rw_ln_b, v_w_branch, v_w_out, v_ffn_norm_g, v_w_ffn_gate, v_w_ffn_up, v_w_ffn_down, v_final_norm_g):
    args = locals()
    wts = {k: args[k] for k in WEIGHTS}
    mom_m = {k: args["m_" + k] for k in WEIGHTS}
    mom_v = {k: args["v_" + k] for k in WEIGHTS}
    names_bf = tuple(k for k in SHARDED if k not in GATHER_F32)
    g_bf, g_f32 = _all_gather([_pack_local(wts, names_bf, bf16), _pack_local(wts, GATHER_F32, f32)], "gather_weights")
    full = {**_unpack_gathered(g_bf, wts, names_bf), **_unpack_gathered(g_f32, wts, GATHER_F32)}
    rep = {k: wts[k] for k in REPLICATED}
    loss, grad_x, grads = _local_step(x, positions, full, rep, loss_target)
    shard = _pack_rows(jnp.concatenate([_split_for_devices(grads[k], ax) for k, ax in SHARDED.items()], axis=1), 128)
    repl = _pack_rows(jnp.concatenate([grads[k].reshape(-1) for k in REPLICATED]), 128)
    got_shard, got_rep = _exchange_grads(shard, repl, "exchange_grads")
    sum_shard = _sum_slots(got_shard, "sum_sharded").reshape(-1)
    sum_rep = _sum_slots(got_rep, "sum_replicated").reshape(-1)
    gw, off = {}, 0
    for k in SHARDED:
        gw[k] = sum_shard[off:off + wts[k].size].reshape(wts[k].shape)
        off += wts[k].size
    off = 0
    for k in REPLICATED:
        gw[k] = sum_rep[off:off + wts[k].size].reshape(wts[k].shape)
        off += wts[k].size
    loss = lax.psum(loss, ("x", "y", "c"))
    big = [k for k in WEIGHTS if wts[k].size >= (1 << 18)]
    small = [k for k in WEIGHTS if k not in big]
    delta, new_m, new_v = {}, {}, {}
    for k in big:
        two = lambda a, k=k: a.reshape(-1, wts[k].shape[-1])
        d, nm_, nv_ = _adamw(two(wts[k]), two(gw[k]), two(mom_m[k]), two(mom_v[k]), f"adamw_{k}")
        delta[k], new_m[k], new_v[k] = (t.reshape(wts[k].shape) for t in (d, nm_, nv_))
    cat = lambda dct: _pack_rows(jnp.concatenate([dct[k].reshape(-1) for k in small]), 128)
    d, nm_, nv_ = (t.reshape(-1) for t in _adamw(cat(wts), cat(gw), cat(mom_m), cat(mom_v), "adamw_small"))
    off = 0
    for k in small:
        sl = slice(off, off + wts[k].size)
        delta[k], new_m[k], new_v[k] = (t[sl].reshape(wts[k].shape) for t in (d, nm_, nv_))
        off += wts[k].size
    return (loss, grad_x, *[gw[k] for k in WEIGHTS], *[delta[k] for k in WEIGHTS], *[new_m[k] for k in WEIGHTS],
            *[new_v[k] for k in WEIGHTS])
```

```python
import functools

import jax
import jax.numpy as jnp
from jax import lax
from jax.experimental import pallas as pl
from jax.experimental.pallas import tpu as pltpu

f32 = jnp.float32
bf16 = jnp.bfloat16
HI = lax.Precision.HIGHEST
NN, NT, TN = ((1,), (0,)), ((1,), (1,)), ((0,), (0,))

N_DEV = 8
D = 1024
HEADS = 8
Q_LORA, KV_LORA, QK_NOPE, QK_ROPE, V_HEAD = 384, 256, 64, 32, 64
SG_DIM, SG_CHUNK, SG_GROUPS = 512, 128, 8
RW_DIM, RW_HEAD, LORA = 512, 64, 64
D_FF = 2816
N_IN = 6688
NORM_EPS, LN_EPS, GN_EPS = 1e-6, 1e-5, 64e-5
ATT_SCALE = (QK_NOPE + QK_ROPE) ** -0.5
P_W = 7168
O_GATE, O_SG, O_RW, O_MLA = 0, 3072, 4096, 6144
RW_W = 2048
MLA_W = 768
O_CKV, O_SLAB, O_Q = O_MLA, O_MLA + 256, O_MLA + 384
CHUNK = 128
VMEM_LIMIT = 56 * 1024 * 1024

B1, B2, LR, EPS, WD, STEP = 0.9, 0.999, 0.001, 1e-8, 0.01, 10


def _pc(body, *, name, out_shape, grid=(), in_specs=None, out_specs=None, scratch=(), sem=None, aliases=None, **cp):
    params = pltpu.CompilerParams(dimension_semantics=sem, vmem_limit_bytes=VMEM_LIMIT, **cp)
    kw = {}
    if in_specs is not None:
        kw["in_specs"] = in_specs
    if out_specs is not None:
        kw["out_specs"] = out_specs
    return pl.pallas_call(body, out_shape=out_shape, grid=grid, scratch_shapes=scratch, compiler_params=params,
                          name=name, input_output_aliases=aliases or {}, interpret=False, **kw)


def _sds(shape, dtype=f32):
    return jax.ShapeDtypeStruct(tuple(shape), dtype)


def _dot(a, b, dims, precision=None):
    return lax.dot_general(a, b, (dims, ((), ())), preferred_element_type=f32, precision=precision)


def _bdot(a, b, dims=NN):
    return _dot(a.astype(bf16), b.astype(bf16), dims)


@jax.custom_vjp
def _mm(a, w):
    return _bdot(a, w, NN)


def _mm_fwd(a, w):
    return _bdot(a, w, NN), (a, w)


def _mm_bwd(res, g):
    a, w = res
    return _bdot(g, w, NT), _bdot(a, g, TN)


_mm.defvjp(_mm_fwd, _mm_bwd)


@jax.custom_vjp
def _mm_nt(a, b):
    return _bdot(a, b, NT)


def _mm_nt_fwd(a, b):
    return _bdot(a, b, NT), (a, b)


def _mm_nt_bwd(res, g):
    a, b = res
    return _bdot(g, b, NN), _bdot(g, a, TN)


_mm_nt.defvjp(_mm_nt_fwd, _mm_nt_bwd)


@jax.custom_vjp
def _mm_tn(a, b):
    return _bdot(a, b, TN)


def _mm_tn_fwd(a, b):
    return _bdot(a, b, TN), (a, b)


def _mm_tn_bwd(res, g):
    a, b = res
    return _bdot(b, g, NT), _bdot(a, g, NN)


_mm_tn.defvjp(_mm_tn_fwd, _mm_tn_bwd)


def _rms(x, g):
    return x * lax.rsqrt(jnp.mean(x * x, axis=-1, keepdims=True) + NORM_EPS) * g


def _sigmoid(x):
    return 1.0 / (1.0 + jnp.exp(-x))


def _gelu(x):
    return 0.5 * x * (1.0 + jnp.tanh(0.7978845608028654 * (x + 0.044715 * x * x * x)))


def _softplus(x):
    return jnp.maximum(x, 0.0) + jnp.log(1.0 + jnp.exp(-jnp.abs(x)))


def _group_sum(x):
    w = x.shape[-1]
    r = lax.broadcasted_iota(jnp.int32, (w, w), 0) // RW_HEAD
    c = lax.broadcasted_iota(jnp.int32, (w, w), 1) // RW_HEAD
    return _dot(x, (r == c).astype(f32), NN, HI)


@jax.custom_vjp
def _swap(x):
    w = x.shape[-1]
    lane = lax.broadcasted_iota(jnp.int32, x.shape, 1) % 128
    lo = (lane >= 64) & (lane < 80)
    hi = (lane >= 80) & (lane < 96)
    return jnp.where(lo, pltpu.roll(x, w - 16, 1), jnp.where(hi, pltpu.roll(x, 16, 1), 0.0))


_swap.defvjp(lambda x: (_swap(x), None), lambda _, g: (_swap(g),))


def _rope(x, c, s):
    return x * c + _swap(x) * s


def _row_spec(tm, width, blk):
    return pl.BlockSpec((tm, width), lambda i, blk=blk: (i, blk))


def _full_spec(a):
    nd = a.ndim
    return pl.BlockSpec(a.shape, lambda i, nd=nd: (0,) * nd)


def _rowwise_fwd(name, f, rows, weights, outs, tm):
    n = rows[0][0].shape[0]
    nr, nw = len(rows), len(weights)

    def body(*refs):
        vals = [r[...].astype(f32) for r in refs[:nr + nw]]
        res = f(*vals)
        for o_ref, o in zip(refs[nr + nw:], res):
            o_ref[...] = o.astype(o_ref.dtype)

    return _pc(
        body, name=name, grid=(n // tm,),
        in_specs=[_row_spec(tm, w, b) for _, w, b in rows] + [_full_spec(w) for w in weights],
        out_specs=[_row_spec(tm, w, 0) for w, _ in outs],
        out_shape=[_sds((n, w), dt) for w, dt in outs], sem=("parallel",),
    )(*[a for a, _, _ in rows], *weights)


def _rowwise_bwd(name, f, rows, weights, cots, tm, drows):
    n = rows[0][0].shape[0]
    nr, nw, nc = len(rows), len(weights), len(cots)
    want = [k for k, dt in enumerate(drows) if dt is not None]

    def body(*refs):
        vals = [r[...].astype(f32) for r in refs[:nr + nw]]
        cot = tuple(r[...].astype(f32) for r in refs[nr + nw:nr + nw + nc])
        _, vjp = jax.vjp(f, *vals)
        grads = vjp(cot)
        outs = refs[nr + nw + nc:]
        for o_ref, k in zip(outs[:len(want)], want):
            o_ref[...] = grads[k].astype(o_ref.dtype)
        first = pl.program_id(0) == 0
        for o_ref, g in zip(outs[len(want):], grads[nr:]):
            @pl.when(first)
            def _(o_ref=o_ref, g=g):
                o_ref[...] = g

            @pl.when(jnp.logical_not(first))
            def _(o_ref=o_ref, g=g):
                o_ref[...] += g

    res = _pc(
        body, name=name, grid=(n // tm,),
        in_specs=[_row_spec(tm, w, b) for _, w, b in rows] + [_full_spec(w) for w in weights]
        + [_row_spec(tm, w, b) for _, w, b in cots],
        out_specs=[_row_spec(tm, rows[k][1], 0) for k in want] + [_full_spec(w) for w in weights],
        out_shape=[_sds((n, rows[k][1]), drows[k]) for k in want] + [_sds(w.shape) for w in weights],
        sem=("arbitrary",),
    )(*[a for a, _, _ in rows], *weights, *[a for a, _, _ in cots])
    return res[:len(want)], res[len(want):]


def _inproj_fwd(x2, g, w, name):
    n = x2.shape[0]
    tm, tn = min(512, n), 512

    def body(x_ref, g_ref, w_ref, p_ref, h_ref):
        @pl.when(pl.program_id(1) == 0)
        def _():
            h_ref[...] = _rms(x_ref[...], g_ref[...]).astype(bf16)

        p_ref[...] = jnp.dot(h_ref[...], w_ref[...], preferred_element_type=f32)

    return _pc(
        body, name=name, grid=(n // tm, P_W // tn),
        in_specs=[pl.BlockSpec((tm, D), lambda i, j: (i, 0)), pl.BlockSpec((1, D), lambda i, j: (0, 0)),
                  pl.BlockSpec((D, tn), lambda i, j: (0, j))],
        out_specs=[pl.BlockSpec((tm, tn), lambda i, j: (i, j)), pl.BlockSpec((tm, D), lambda i, j: (i, 0))],
        out_shape=[_sds((n, P_W)), _sds((n, D), bf16)], sem=("parallel", "arbitrary"),
    )(x2, g, w)


def _norm_matmul_bwd(dy, w, x2, g, dres, name):
    n, k = dy.shape
    tm = min(512, n)
    tk = 1024 if k % 1024 == 0 else 1408
    nk = k // tk

    def body(dy_ref, w_ref, x_ref, g_ref, dr_ref, dx_ref, dg_ref, acc):
        i, j = pl.program_id(0), pl.program_id(1)

        @pl.when(j == 0)
        def _():
            acc[...] = jnp.zeros_like(acc)

        @pl.when((i == 0) & (j == 0))
        def _():
            dg_ref[...] = jnp.zeros_like(dg_ref)

        acc[...] += _dot(dy_ref[...], w_ref[...], NT)

        @pl.when(j == nk - 1)
        def _():
            _, vjp = jax.vjp(_rms, x_ref[...], g_ref[...])
            dx, dg = vjp(acc[...])
            dx_ref[...] = dr_ref[...] + dx
            dg_ref[...] += dg

    return _pc(
        body, name=name, grid=(n // tm, nk),
        in_specs=[pl.BlockSpec((tm, tk), lambda i, j: (i, j)), pl.BlockSpec((D, tk), lambda i, j: (0, j)),
                  pl.BlockSpec((tm, D), lambda i, j: (i, 0)), pl.BlockSpec((1, D), lambda i, j: (0, 0)),
                  pl.BlockSpec((tm, D), lambda i, j: (i, 0))],
        out_specs=[pl.BlockSpec((tm, D), lambda i, j: (i, 0)), pl.BlockSpec((1, D), lambda i, j: (0, 0))],
        out_shape=[_sds((n, D)), _sds((1, D))], scratch=[pltpu.VMEM((tm, D), f32)], sem=("arbitrary", "arbitrary"),
    )(dy, w, x2, g, dres)


def _matmul_tn(a, g, name):
    n, k = a.shape
    m = g.shape[1]
    tr = min(512, n)
    tk = k if k <= 1024 else 1408
    tn = m if m <= 1024 else (512 if m % 512 == 0 else 1408)
    nr = n // tr

    def body(a_ref, g_ref, o_ref):
        @pl.when(pl.program_id(2) == 0)
        def _():
            o_ref[...] = jnp.zeros_like(o_ref)

        o_ref[...] += _dot(a_ref[...], g_ref[...], TN)

    return _pc(
        body, name=name, grid=(k // tk, m // tn, nr),
        in_specs=[pl.BlockSpec((tr, tk), lambda i, j, r: (r, i)), pl.BlockSpec((tr, tn), lambda i, j, r: (r, j))],
        out_specs=pl.BlockSpec((tk, tn), lambda i, j, r: (i, j)),
        out_shape=_sds((k, m)), sem=("parallel", "parallel", "arbitrary"),
    )(a, g)


def _f_mla_proj(ckv, slab, pq, c, s, qg, kg, wq, wk, wv):
    c8, s8 = jnp.concatenate([c] * HEADS, axis=1), jnp.concatenate([s] * HEADS, axis=1)
    q = _rope(_mm(_rms(pq, qg), wq), c8, s8)
    cn = _rms(ckv, kg)
    k = _mm(cn, wk) + jnp.concatenate([_rope(slab, c, s)] * HEADS, axis=1)
    return q, k, _mm(cn, wv)


def _attn_fwd(q, k, v, bsz, seq, name):
    n = q.shape[0]
    tq = min(256, seq)
    nq = seq // tq

    def body(q_ref, k_ref, v_ref, o_ref):
        lane = lax.broadcasted_iota(jnp.int32, (tq, 128), 1) < 64
        vv = v_ref[...]
        outs = []
        for h in range(2):
            s = _dot(q_ref[:, h * 128:(h + 1) * 128], k_ref[:, h * 128:(h + 1) * 128], NT) * ATT_SCALE
            e = jnp.exp(s - jnp.max(s, axis=-1, keepdims=True))
            p = (e / jnp.sum(e, axis=-1, keepdims=True)).astype(bf16)
            outs.append(_dot(p, vv, NN))
        o_ref[...] = jnp.where(lane, outs[0], outs[1])

    return _pc(
        body, name=name, grid=(bsz, HEADS // 2, nq),
        in_specs=[pl.BlockSpec((tq, 256), lambda b, h, i: (b * nq + i, h)),
                  pl.BlockSpec((seq, 256), lambda b, h, i: (b, h)),
                  pl.BlockSpec((seq, 128), lambda b, h, i: (b, h))],
        out_specs=pl.BlockSpec((tq, 128), lambda b, h, i: (b * nq + i, h)),
        out_shape=_sds((n, HEADS * V_HEAD)), sem=("parallel", "parallel", "parallel"),
    )(q, k, v)


def _attn_bwd(q, k, v, o, do, bsz, seq, name):
    n = q.shape[0]
    tq = min(256, seq)
    nq = seq // tq

    def body(q_ref, k_ref, v_ref, o_ref, do_ref, dq_ref, dk_ref, dv_ref):
        @pl.when(pl.program_id(2) == 0)
        def _():
            dk_ref[...] = jnp.zeros_like(dk_ref)
            dv_ref[...] = jnp.zeros_like(dv_ref)

        lane = lax.broadcasted_iota(jnp.int32, (tq, 128), 1) < 64
        vv = v_ref[...]
        for h in range(2):
            qh, kh = q_ref[:, h * 128:(h + 1) * 128], k_ref[:, h * 128:(h + 1) * 128]
            s = _dot(qh, kh, NT) * ATT_SCALE
            e = jnp.exp(s - jnp.max(s, axis=-1, keepdims=True))
            p = e / jnp.sum(e, axis=-1, keepdims=True)
            doh = jnp.where(lane if h == 0 else jnp.logical_not(lane), do_ref[...], 0.0)
            delta = jnp.sum(doh * o_ref[...], axis=-1, keepdims=True)
            dob = doh.astype(bf16)
            dp = _dot(dob, vv, NT)
            ds = (p * (dp - delta) * ATT_SCALE).astype(bf16)
            dq_ref[:, h * 128:(h + 1) * 128] = _dot(ds, kh, NN)
            dk_ref[:, h * 128:(h + 1) * 128] += _dot(ds, qh, TN)
            dv_ref[...] += _dot(p.astype(bf16), dob, TN)

    return _pc(
        body, name=name, grid=(bsz, HEADS // 2, nq),
        in_specs=[pl.BlockSpec((tq, 256), lambda b, h, i: (b * nq + i, h)),
                  pl.BlockSpec((seq, 256), lambda b, h, i: (b, h)),
                  pl.BlockSpec((seq, 128), lambda b, h, i: (b, h)),
                  pl.BlockSpec((tq, 128), lambda b, h, i: (b * nq + i, h)),
                  pl.BlockSpec((tq, 128), lambda b, h, i: (b * nq + i, h))],
        out_specs=[pl.BlockSpec((tq, 256), lambda b, h, i: (b * nq + i, h)),
                   pl.BlockSpec((seq, 256), lambda b, h, i: (b, h)),
                   pl.BlockSpec((seq, 128), lambda b, h, i: (b, h))],
        out_shape=[_sds((n, HEADS * 128)), _sds((n, HEADS * 128)), _sds((n, HEADS * V_HEAD))],
        sem=("parallel", "parallel", "arbitrary"),
    )(q, k, v, o, do)


def _f_sg(pu, pv, lg, lb, bias, *ws):
    u, vv = _gelu(pu), _gelu(pv)
    mu = jnp.mean(vv, axis=-1, keepdims=True)
    d = vv - mu
    vv = d * lax.rsqrt(jnp.mean(d * d, axis=-1, keepdims=True) + LN_EPS) * lg + lb
    group = lax.broadcasted_iota(jnp.int32, (SG_CHUNK, SG_DIM), 1) // (SG_DIM // SG_GROUPS)
    mixed = bias
    for k, w in enumerate(ws):
        mixed = mixed + jnp.where(group == k, _mm(w, vv), 0.0)
    return (u * mixed,)


def _shift_mean(a, prev_row, next_row):
    t = a.shape[0]
    row = lax.broadcasted_iota(jnp.int32, a.shape, 0)
    prev = jnp.where(row == 0, prev_row, pltpu.roll(a, 1, 0))
    nxt = jnp.where(row == t - 1, next_row, pltpu.roll(a, t - 1, 0))
    return 0.5 * (prev + nxt)


def _halo_specs(tm, width, blk, nblk8):
    h = tm // 8
    return [pl.BlockSpec((tm, width), lambda i: (i, blk)),
            pl.BlockSpec((8, width), lambda i: (jnp.maximum(i * h - 1, 0), blk)),
            pl.BlockSpec((8, width), lambda i: (jnp.minimum((i + 1) * h, nblk8 - 1), blk))]


def _edge_rows(i, tm, seq, pv_ref, nx_ref, scale=None):
    first = (i * tm) % seq == 0
    last = ((i + 1) * tm) % seq == 0
    pv, nx = pv_ref[7:8, :], nx_ref[0:1, :]
    if scale is not None:
        pv, nx = pv * scale, nx * scale
    return jnp.where(first, 0.0, pv), jnp.where(last, 0.0, nx)


def _shift_fwd(p, mu, seq, name):
    n = p.shape[0]
    tm = min(256, seq)
    blk = O_RW // RW_W

    def body(x_ref, pv_ref, nx_ref, mu_ref, z_ref):
        x = x_ref[...]
        pv, nx = _edge_rows(pl.program_id(0), tm, seq, pv_ref, nx_ref)
        z_ref[...] = x + mu_ref[...] * (_shift_mean(x, pv, nx) - x)

    return _pc(
        body, name=name, grid=(n // tm,),
        in_specs=_halo_specs(tm, RW_W, blk, n // 8) + [pl.BlockSpec((1, RW_W), lambda i: (0, 0))],
        out_specs=pl.BlockSpec((tm, RW_W), lambda i: (i, 0)), out_shape=_sds((n, RW_W)), sem=("parallel",),
    )(p, p, p, mu)


def _shift_bwd(dz, p, mu, seq, name):
    n = p.shape[0]
    tm = min(256, seq)
    blk = O_RW // RW_W

    def body(dz_ref, dpv_ref, dnx_ref, x_ref, pv_ref, nx_ref, mu_ref, dx_ref, dmu_ref):
        i = pl.program_id(0)
        mu_v = mu_ref[...]
        dzv = dz_ref[...]
        m = dzv * mu_v
        mpv, mnx = _edge_rows(i, tm, seq, dpv_ref, dnx_ref, mu_v)
        dx_ref[...] = (dzv - m + _shift_mean(m, mpv, mnx)).astype(dx_ref.dtype)
        x = x_ref[...]
        pv, nx = _edge_rows(i, tm, seq, pv_ref, nx_ref)
        part = jnp.sum(dzv * (_shift_mean(x, pv, nx) - x), axis=0, keepdims=True)

        @pl.when(i == 0)
        def _():
            dmu_ref[...] = part

        @pl.when(i != 0)
        def _():
            dmu_ref[...] += part

    return _pc(
        body, name=name, grid=(n // tm,),
        in_specs=_halo_specs(tm, RW_W, 0, n // 8) + _halo_specs(tm, RW_W, blk, n // 8)
        + [pl.BlockSpec((1, RW_W), lambda i: (0, 0))],
        out_specs=[pl.BlockSpec((tm, RW_W), lambda i: (i, 0)), pl.BlockSpec((1, RW_W), lambda i: (0, 0))],
        out_shape=[_sds((n, RW_W), bf16), _sds((1, RW_W))], sem=("arbitrary",),
    )(dz, dz, dz, p, p, p, mu)


def _f_rw_pre(k, wl, al, gl, w0, a0, w2, a2, g2, k_k, k_a):
    w = w0 + _mm(jnp.tanh(wl), w2)
    lw = -jnp.exp(-_softplus(-w) - 0.5)
    a = _sigmoid(a0 + _mm(al, a2))
    g = _mm(_sigmoid(gl), g2)
    kkr = k * k_k
    kk = kkr / jnp.maximum(jnp.sqrt(_group_sum(kkr * kkr)), 1e-12)
    two = lambda t: jnp.concatenate([t, t], axis=1)
    kd = two(k) * (1.0 + (a - 1.0) * two(k_a))
    bd = two(kk) * a
    return lw, kd, kk, bd, g


def _f_rw_post(y0, y1, r, v, kd0, kd1, g, r_k, ln_g, ln_b):
    y = y0 + y1
    mean = _group_sum(y) * (1.0 / RW_HEAD)
    d = y - mean
    var = _group_sum(d * d) * (1.0 / RW_HEAD)
    yn = d * lax.rsqrt(var + GN_EPS) * ln_g + ln_b
    bonus = _group_sum(r * (kd0 + kd1) * r_k)
    return ((yn + bonus * v) * g,)


@jax.custom_vjp
def _tri_inv(a):
    c = a.shape[0]
    row = lax.broadcasted_iota(jnp.int32, (c, c), 0)
    col = lax.broadcasted_iota(jnp.int32, (c, c), 1)
    eye = (row == col).astype(f32)
    blk = lambda b: (row // b) == (col // b)
    ld = jnp.where(blk(8), a, 0.0)
    l2 = _bdot(ld, ld)
    l4 = _bdot(l2, l2)
    t = _bdot(_bdot(eye - ld, eye + l2), eye + l4)
    b = 8
    while b < c:
        off = jnp.where(blk(2 * b) & jnp.logical_not(blk(b)), a, 0.0)
        t = t - _bdot(_bdot(t, off), t)
        b *= 2
    return t


def _tri_inv_fwd(a):
    t = _tri_inv(a)
    return t, t


def _tri_inv_bwd(t, g):
    return (-_bdot(_bdot(t, g, TN), t, NT),)


_tri_inv.defvjp(_tri_inv_fwd, _tri_inv_bwd)


def _scan_chunk(s0, r, v, kk, lw, kd, bd, rev):
    c = r.shape[0]
    row = lax.broadcasted_iota(jnp.int32, (c, c), 0)
    col = lax.broadcasted_iota(jnp.int32, (c, c), 1)
    ahead = jnp.where(rev, col - row, row - col)
    before = ahead > 0
    incl = ahead >= 0
    lane = lax.broadcasted_iota(jnp.int32, (1, 128), 1)
    m0 = (lane < 64).astype(f32)
    masks = (m0, 1.0 - m0)
    lane_c = lax.broadcasted_iota(jnp.int32, (c, 128), 1) < 64
    bd_mask = ((lax.broadcasted_iota(jnp.int32, (128, 128), 0) // 64)
               == (lax.broadcasted_iota(jnp.int32, (128, 128), 1) // 64)).astype(f32)
    tot = jnp.sum(lw, axis=0, keepdims=True)
    lp = _dot(incl.astype(f32), lw, NN, HI) - 0.5 * tot
    eg, ieg = jnp.exp(lp), jnp.exp(-lp)
    rt, kt, bt, at = r * eg, kd * ieg, bd * ieg, kk * jnp.exp(lp - lw)
    etot = jnp.exp(0.5 * tot)
    si = s0 * etot
    pair = lambda x, y, mask: [jnp.where(mask, _mm_nt(x * mh, y), 0.0) for mh in masks]
    a_ab, a_ak, a_rb, a_rk = pair(at, bt, before), pair(at, kt, before), pair(rt, bt, incl), pair(rt, kt, incl)
    t = [_tri_inv(a) for a in a_ab]
    sel = lambda a, b: jnp.where(lane_c, a, b)
    x = _mm_nt(at, si) + sel(_mm(a_ak[0], v), _mm(a_ak[1], v))
    u = -sel(_mm(t[0], x), _mm(t[1], x))
    y = _mm_nt(rt, si) + sel(_mm(a_rb[0], u) + _mm(a_rk[0], v), _mm(a_rb[1], u) + _mm(a_rk[1], v))
    se = (si + (_mm_tn(u, bt) + _mm_tn(v, kt)) * bd_mask) * etot
    return y, se


def _scan_specs(nc, bsz):
    cc = lambda d, c: jnp.where(d == 0, c, nc - 1 - c)
    rowblk = lambda d, b, c: b * nc + cc(d, c)
    zspec = lambda off: pl.BlockSpec((CHUNK, 128), lambda d, b, h, c: (rowblk(d, b, c), off + h))
    dspec = pl.BlockSpec((CHUNK, 128), lambda d, b, h, c: (rowblk(d, b, c), d * 4 + h))
    yspec = pl.BlockSpec((None, CHUNK, 128), lambda d, b, h, c: (d, rowblk(d, b, c), h))
    sspec = pl.BlockSpec((None, 128, 128), lambda d, b, h, c: (((d * bsz + b) * 4 + h) * nc + cc(d, c), 0, 0))
    return zspec, dspec, yspec, sspec


def _scan_fwd(z, lw, kd, kk, bd, bsz, seq, name):
    n = z.shape[0]
    nc = seq // CHUNK
    zspec, dspec, yspec, sspec = _scan_specs(nc, bsz)

    def body(r_ref, v_ref, kk_ref, lw_ref, kd_ref, bd_ref, y_ref, s_ref, st):
        @pl.when(pl.program_id(3) == 0)
        def _():
            st[...] = jnp.zeros_like(st)

        s0 = st[...]
        s_ref[...] = s0
        y, se = _scan_chunk(s0, r_ref[...], v_ref[...], kk_ref[...], lw_ref[...], kd_ref[...], bd_ref[...],
                            pl.program_id(0) == 1)
        y_ref[...] = y
        st[...] = se

    return _pc(
        body, name=name, grid=(2, bsz, 4, nc),
        in_specs=[zspec(0), zspec(8), zspec(0), dspec, dspec, dspec],
        out_specs=[yspec, sspec],
        out_shape=[_sds((2, n, RW_DIM)), _sds((2 * bsz * 4 * nc, 128, 128))],
        scratch=[pltpu.VMEM((128, 128), f32)], sem=("parallel", "parallel", "parallel", "arbitrary"),
    )(z, z, kk, lw, kd, bd)


def _scan_bwd(z, lw, kd, kk, bd, s_in, dy, bsz, seq, name):
    n = z.shape[0]
    nc = seq // CHUNK
    zspec, dspec, yspec, sspec = _scan_specs(nc, bsz)
    flip = lambda spec: pl.BlockSpec(spec.block_shape, lambda d, b, h, c, f=spec.index_map: f(d, b, h, nc - 1 - c))
    zspec_f = lambda off: flip(zspec(off))
    dspec, yspec, sspec = flip(dspec), flip(yspec), flip(sspec)

    def body(r_ref, v_ref, kk_ref, lw_ref, kd_ref, bd_ref, s_ref, dy_ref,
             dr_ref, dv_ref, dkk_ref, dlw_ref, dkd_ref, dbd_ref, dst):
        @pl.when(pl.program_id(3) == 0)
        def _():
            dst[...] = jnp.zeros_like(dst)

        rev = pl.program_id(0) == 1
        _, vjp = jax.vjp(functools.partial(_scan_chunk, rev=rev), s_ref[...], r_ref[...], v_ref[...], kk_ref[...],
                         lw_ref[...], kd_ref[...], bd_ref[...])
        ds, dr, dv, dkk, dlw, dkd, dbd = vjp((dy_ref[...], dst[...]))
        dst[...] = ds
        dr_ref[...] = dr
        dv_ref[...] = dv
        dkk_ref[...] = dkk
        dlw_ref[...] = dlw
        dkd_ref[...] = dkd
        dbd_ref[...] = dbd

    return _pc(
        body, name=name, grid=(2, bsz, 4, nc),
        in_specs=[zspec_f(0), zspec_f(8), zspec_f(0), dspec, dspec, dspec, sspec, zspec_f(0)],
        out_specs=[yspec, yspec, yspec, dspec, dspec, dspec],
        out_shape=[_sds((2, n, RW_DIM))] * 3 + [_sds((n, 2 * RW_DIM))] * 3,
        scratch=[pltpu.VMEM((128, 128), f32)], sem=("parallel", "parallel", "parallel", "arbitrary"),
    )(z, z, kk, lw, kd, bd, s_in, dy)


def _merge_fwd(x2, p, ya, yb, yc, gb, wb, wo, name):
    n = x2.shape[0]
    tm = min(256, n)

    def body(x_ref, pg_ref, ya_ref, yb_ref, yc_ref, gb_ref, wb_ref, wo_ref, o_ref):
        gates = _sigmoid(pg_ref[...] + gb_ref[...])
        merged = jnp.zeros((tm, D), f32)
        for k, y_ref in enumerate((ya_ref, yb_ref, yc_ref)):
            merged += gates[:, k * D:(k + 1) * D] * _bdot(y_ref[...], wb_ref[k])
        o_ref[...] = x_ref[...] + _bdot(merged, wo_ref[...])

    row = lambda w, b=0: pl.BlockSpec((tm, w), lambda i, b=b: (i, b))
    return _pc(
        body, name=name, grid=(n // tm,),
        in_specs=[row(D), row(3 * D, O_GATE // (3 * D)), row(512), row(512), row(512),
                  pl.BlockSpec((1, 3 * D), lambda i: (0, 0)), pl.BlockSpec((3, 512, D), lambda i: (0, 0, 0)),
                  pl.BlockSpec((D, D), lambda i: (0, 0))],
        out_specs=row(D), out_shape=_sds((n, D)), sem=("parallel",),
    )(x2, p, ya, yb, yc, gb, wb, wo)


def _merge_bwd(dx1, p, ya, yb, yc, gb, wb, wo, name):
    n = dx1.shape[0]
    tm = min(256, n)

    def body(dx_ref, pg_ref, ya_ref, yb_ref, yc_ref, gb_ref, wb_ref, wo_ref,
             dpg_ref, dya_ref, dyb_ref, dyc_ref, dt_ref, mg_ref, dgb_ref):
        gates = _sigmoid(pg_ref[...] + gb_ref[...])
        dmerged = _bdot(dx_ref[...], wo_ref[...], NT)
        merged = jnp.zeros((tm, D), f32)
        dpg = []
        for k, (y_ref, dy_ref) in enumerate(((ya_ref, dya_ref), (yb_ref, dyb_ref), (yc_ref, dyc_ref))):
            gk = gates[:, k * D:(k + 1) * D]
            tk = _bdot(y_ref[...], wb_ref[k])
            merged += gk * tk
            dpg.append(dmerged * tk * gk * (1.0 - gk))
            dtk = dmerged * gk
            dt_ref[:, k * D:(k + 1) * D] = dtk.astype(bf16)
            dy_ref[...] = _bdot(dtk, wb_ref[k], NT)
        dpg = jnp.concatenate(dpg, axis=1)
        dpg_ref[...] = dpg.astype(bf16)
        mg_ref[...] = merged.astype(bf16)
        part = jnp.sum(dpg, axis=0, keepdims=True)

        @pl.when(pl.program_id(0) == 0)
        def _():
            dgb_ref[...] = part

        @pl.when(pl.program_id(0) != 0)
        def _():
            dgb_ref[...] += part

    row = lambda w, b=0: pl.BlockSpec((tm, w), lambda i, b=b: (i, b))
    return _pc(
        body, name=name, grid=(n // tm,),
        in_specs=[row(D), row(3 * D, O_GATE // (3 * D)), row(512), row(512), row(512),
                  pl.BlockSpec((1, 3 * D), lambda i: (0, 0)), pl.BlockSpec((3, 512, D), lambda i: (0, 0, 0)),
                  pl.BlockSpec((D, D), lambda i: (0, 0))],
        out_specs=[row(3 * D), row(512), row(512), row(512), row(3 * D), row(D),
                   pl.BlockSpec((1, 3 * D), lambda i: (0, 0))],
        out_shape=[_sds((n, 3 * D), bf16), _sds((n, 512)), _sds((n, 512)), _sds((n, 512)), _sds((n, 3 * D), bf16),
                   _sds((n, D), bf16), _sds((1, 3 * D))],
        sem=("arbitrary",),
    )(dx1, p, ya, yb, yc, gb, wb, wo)


FF_T = 1408


def _ffn_fwd(x1, g, wg, wu, wd, name):
    n = x1.shape[0]
    tm = min(512, n)
    nf = D_FF // FF_T

    def body(x_ref, g_ref, wg_ref, wu_ref, wd_ref, o_ref, hs):
        j = pl.program_id(1)

        @pl.when(j == 0)
        def _():
            hs[...] = _rms(x_ref[...], g_ref[...]).astype(bf16)
            o_ref[...] = x_ref[...]

        a = _dot(hs[...], wg_ref[...], NN)
        b = _dot(hs[...], wu_ref[...], NN)
        o_ref[...] += _bdot(a * _sigmoid(a) * b, wd_ref[...])

    return _pc(
        body, name=name, grid=(n // tm, nf),
        in_specs=[pl.BlockSpec((tm, D), lambda i, j: (i, 0)), pl.BlockSpec((1, D), lambda i, j: (0, 0)),
                  pl.BlockSpec((D, FF_T), lambda i, j: (0, j)), pl.BlockSpec((D, FF_T), lambda i, j: (0, j)),
                  pl.BlockSpec((FF_T, D), lambda i, j: (j, 0))],
        out_specs=pl.BlockSpec((tm, D), lambda i, j: (i, 0)), out_shape=_sds((n, D)),
        scratch=[pltpu.VMEM((tm, D), bf16)], sem=("parallel", "arbitrary"),
    )(x1, g, wg, wu, wd)


def _ffn_bwd(dx2, x1, g, wg, wu, wd, name):
    n = x1.shape[0]
    tm = min(512, n)
    nf = D_FF // FF_T

    def body(dx_ref, x_ref, g_ref, wg_ref, wu_ref, wd_ref, dx1_ref, dg_ref, h_ref, da_ref, db_ref, hm_ref, acc):
        i, j = pl.program_id(0), pl.program_id(1)

        @pl.when(j == 0)
        def _():
            h_ref[...] = _rms(x_ref[...], g_ref[...]).astype(bf16)
            acc[...] = jnp.zeros_like(acc)

        @pl.when((i == 0) & (j == 0))
        def _():
            dg_ref[...] = jnp.zeros_like(dg_ref)

        h = h_ref[...]
        a = _dot(h, wg_ref[...], NN)
        b = _dot(h, wu_ref[...], NN)
        sg = _sigmoid(a)
        s = a * sg
        dhm = _bdot(dx_ref[...], wd_ref[...], NT)
        da = (dhm * b * (sg * (1.0 + a * (1.0 - sg)))).astype(bf16)
        db = (dhm * s).astype(bf16)
        da_ref[...] = da
        db_ref[...] = db
        hm_ref[...] = (s * b).astype(bf16)
        acc[...] += _dot(da, wg_ref[...], NT) + _dot(db, wu_ref[...], NT)

        @pl.when(j == nf - 1)
        def _():
            _, vjp = jax.vjp(_rms, x_ref[...], g_ref[...])
            dx, dg = vjp(acc[...])
            dx1_ref[...] = dx_ref[...] + dx
            dg_ref[...] += dg

    rowf = pl.BlockSpec((tm, FF_T), lambda i, j: (i, j))
    rowd = pl.BlockSpec((tm, D), lambda i, j: (i, 0))
    vec = pl.BlockSpec((1, D), lambda i, j: (0, 0))
    return _pc(
        body, name=name, grid=(n // tm, nf),
        in_specs=[rowd, rowd, vec, pl.BlockSpec((D, FF_T), lambda i, j: (0, j)),
                  pl.BlockSpec((D, FF_T), lambda i, j: (0, j)), pl.BlockSpec((FF_T, D), lambda i, j: (j, 0))],
        out_specs=[rowd, vec, rowd, rowf, rowf, rowf],
        out_shape=[_sds((n, D)), _sds((1, D)), _sds((n, D), bf16), _sds((n, D_FF), bf16), _sds((n, D_FF), bf16),
                   _sds((n, D_FF), bf16)],
        scratch=[pltpu.VMEM((tm, D), f32)], sem=("arbitrary", "arbitrary"),
    )(dx2, x1, g, wg, wu, wd)


def _loss_head(x2, g, tgt, name):
    n = x2.shape[0]
    tm = min(512, n)

    def f(x, gg, t):
        e = _rms(x, gg) - t
        return 0.5 * jnp.sum(jnp.mean(e * e, axis=-1, keepdims=True))

    def body(x_ref, g_ref, t_ref, l_ref, dx_ref, dg_ref):
        val, vjp = jax.vjp(f, x_ref[...], g_ref[...], t_ref[...])
        dx, dg, _ = vjp(jnp.ones((), f32))
        dx_ref[...] = dx

        @pl.when(pl.program_id(0) == 0)
        def _():
            l_ref[...] = jnp.zeros_like(l_ref)
            dg_ref[...] = jnp.zeros_like(dg_ref)

        l_ref[...] += val
        dg_ref[...] += dg

    rowd = pl.BlockSpec((tm, D), lambda i: (i, 0))
    return _pc(
        body, name=name, grid=(n // tm,),
        in_specs=[rowd, pl.BlockSpec((1, D), lambda i: (0, 0)), rowd],
        out_specs=[pl.BlockSpec((8, 128), lambda i: (0, 0)), rowd, pl.BlockSpec((1, D), lambda i: (0, 0))],
        out_shape=[_sds((8, 128)), _sds((n, D)), _sds((1, D))], sem=("arbitrary",),
    )(x2, g, tgt)


def _adamw(w, parts, m, v, name):
    r, c = w.shape
    tr = r
    for cand in (1024, 512, 256, 128, 64, 32, 16, 8):
        if r % cand == 0 and cand * c * 4 <= 1024 * 1024:
            tr = cand
            break

    def body(w_ref, p_ref, m_ref, v_ref, g_ref, d_ref, nm_ref, nv_ref):
        gg = p_ref[0].astype(f32)
        for k in range(1, N_DEV):
            gg = gg + p_ref[k].astype(f32)
        g_ref[...] = gg
        nm = B1 * m_ref[...] + (1.0 - B1) * gg
        nv = B2 * v_ref[...] + (1.0 - B2) * (gg * gg)
        m_hat = nm / (1.0 - B1 ** STEP)
        v_hat = nv / (1.0 - B2 ** STEP)
        d_ref[...] = -LR * (m_hat / (jnp.sqrt(v_hat) + EPS) + WD * w_ref[...])
        nm_ref[...] = nm
        nv_ref[...] = nv

    spec = pl.BlockSpec((tr, c), lambda i: (i, 0))
    pspec = pl.BlockSpec((N_DEV, tr, c), lambda i: (0, i, 0))
    return _pc(body, name=name, grid=(r // tr,), in_specs=[spec, pspec, spec, spec], out_specs=[spec] * 4,
               out_shape=[_sds((r, c))] * 4, sem=("parallel",))(w, parts, m, v)


def _peers():
    x, y, c = lax.axis_index("x"), lax.axis_index("y"), lax.axis_index("c")
    me = 4 * x + 2 * y + c
    peers = []
    for k in range(1, N_DEV):
        fx, fy, fc = (k >> 2) & 1, (k >> 1) & 1, k & 1
        peers.append(((1 - x) if fx else x, (1 - y) if fy else y, (1 - c) if fc else c))
    return me, peers


def _all_gather(parts, name):
    n = len(parts)

    def body(*refs):
        ins, outs = refs[:n], refs[n:2 * n]
        send_sems, recv_sems, local_sems = refs[2 * n:]
        me, peers = _peers()
        copies = []
        for a in range(n):
            loc = pltpu.make_async_copy(ins[a], outs[a].at[me], local_sems.at[a])
            loc.start()
            copies.append(loc)
            for k, peer in enumerate(peers):
                cp = pltpu.make_async_remote_copy(
                    src_ref=ins[a], dst_ref=outs[a].at[me], send_sem=send_sems.at[a * (N_DEV - 1) + k], recv_sem=recv_sems.at[a * (N_DEV - 1) + k],
                    device_id=peer, device_id_type=pl.DeviceIdType.MESH)
                cp.start()
                copies.append(cp)
        for cp in copies:
            cp.wait()

    anyspec = pl.BlockSpec(memory_space=pl.ANY)
    return _pc(
        body, name=name, in_specs=[anyspec] * n, out_specs=[anyspec] * n,
        out_shape=[_sds((N_DEV,) + p.shape, p.dtype) for p in parts],
        scratch=[pltpu.SemaphoreType.DMA((n * (N_DEV - 1),)), pltpu.SemaphoreType.DMA((n * (N_DEV - 1),)),
                 pltpu.SemaphoreType.DMA((n,))],
    )(*parts)


def _exchange_grads(shards, rep, name):
    n = len(shards)

    def body(*refs):
        ins, rep_ref = refs[:n], refs[n]
        outs, orep_ref = refs[n + 1:2 * n + 1], refs[2 * n + 1]
        send_sems, recv_sems, local_sems = refs[2 * n + 2:]
        me, peers = _peers()
        copies = [pltpu.make_async_copy(ins[a].at[me], outs[a].at[me], local_sems.at[a]) for a in range(n)]
        copies.append(pltpu.make_async_copy(rep_ref, orep_ref.at[me], local_sems.at[n]))
        for cp in copies:
            cp.start()
        for k, peer in enumerate(peers):
            dev = 4 * peer[0] + 2 * peer[1] + peer[2]
            for a in range(n + 1):
                s = a * (N_DEV - 1) + k
                cp = pltpu.make_async_remote_copy(
                    src_ref=ins[a].at[dev] if a < n else rep_ref, dst_ref=(outs[a] if a < n else orep_ref).at[me],
                    send_sem=send_sems.at[s], recv_sem=recv_sems.at[s], device_id=peer,
                    device_id_type=pl.DeviceIdType.MESH)
                cp.start()
                copies.append(cp)
        for cp in copies:
            cp.wait()

    anyspec = pl.BlockSpec(memory_space=pl.ANY)
    nsem = (n + 1) * (N_DEV - 1)
    return _pc(
        body, name=name, in_specs=[anyspec] * (n + 1), out_specs=[anyspec] * (n + 1),
        out_shape=[_sds(a.shape, a.dtype) for a in shards] + [_sds((N_DEV,) + rep.shape, rep.dtype)],
        scratch=[pltpu.SemaphoreType.DMA((nsem,)), pltpu.SemaphoreType.DMA((nsem,)), pltpu.SemaphoreType.DMA((n + 1,))],
    )(*shards, rep)


SHARDED = {"w_in": 2, "gate_b": 2, "w_uq": 2, "w_ukv": 2, "rw_w0": 2, "rw_w2": 3, "rw_a0": 2, "rw_a2": 3, "rw_g2": 2,
           "w_branch": 3, "w_out": 1, "w_ffn_gate": 2, "w_ffn_up": 2, "w_ffn_down": 1}
GATHER_F32 = ("gate_b", "rw_w0", "rw_a0")
REPLICATED = ("attn_norm_g", "q_norm_g", "kv_norm_g", "sg_ln_g", "sg_ln_b", "sg_w", "sg_b", "rw_mu", "rw_k_k", "rw_k_a",
              "rw_r_k", "rw_ln_g", "rw_ln_b", "ffn_norm_g", "final_norm_g")
WEIGHTS = ("attn_norm_g", "w_in", "gate_b", "q_norm_g", "w_uq", "kv_norm_g", "w_ukv", "sg_ln_g", "sg_ln_b", "sg_w", "sg_b",
           "rw_mu", "rw_w0", "rw_w2", "rw_a0", "rw_a2", "rw_g2", "rw_k_k", "rw_k_a", "rw_r_k", "rw_ln_g", "rw_ln_b",
           "w_branch", "w_out", "ffn_norm_g", "w_ffn_gate", "w_ffn_up", "w_ffn_down", "final_norm_g")


BIG = ("w_in", "w_branch", "w_out", "w_ffn_gate", "w_ffn_up", "w_ffn_down")
SMALL_BF = ("w_uq", "w_ukv", "rw_w2", "rw_a2", "rw_g2")
SMALL = SMALL_BF + GATHER_F32


def _pack128(blocks, names, dtype, lead=0):
    parts = [blocks[k].astype(dtype).reshape(blocks[k].shape[:lead] + (-1, 128)) for k in names]
    rows = sum(p.shape[lead] for p in parts)
    pad = -rows % 256
    if pad:
        parts.append(jnp.zeros(parts[0].shape[:lead] + (pad, 128), dtype))
    return jnp.concatenate(parts, axis=lead)


def _unpack128(packed, shapes, names, lead=0):
    out, off = {}, 0
    for k in names:
        rows = 1
        for d in shapes[k]:
            rows *= d
        rows //= 128
        idx = (slice(None),) * lead + (slice(off, off + rows),)
        out[k] = packed[idx].reshape(packed.shape[:lead] + tuple(shapes[k]))
        off += rows
    return out


def _join_blocks(g, ax):
    shp = g.shape[1:]
    return jnp.moveaxis(g, 0, ax).reshape(shp[:ax] + (N_DEV * shp[ax],) + shp[ax + 1:])


def _split_blocks(full, ax):
    shp = full.shape
    return jnp.moveaxis(full.reshape(shp[:ax] + (N_DEV, shp[ax] // N_DEV) + shp[ax + 1:]), ax, 0)


def _w_in_padded(w):
    z = lambda n: jnp.zeros((w.shape[0], n), w.dtype)
    q, ckv, kr = w[:, 0:384], w[:, 384:640], w[:, 640:672]
    sg, rw, gate = w[:, 672:1696], w[:, 1696:3616], w[:, 3616:6688]
    return jnp.concatenate([gate, sg, rw, z(128), ckv, z(64), kr, z(32), q, z(P_W - O_MLA - MLA_W)], axis=1)


def _w_in_unpadded(g):
    return jnp.concatenate([g[:, O_Q:O_Q + 384], g[:, O_CKV:O_CKV + 256], g[:, O_SLAB + 64:O_SLAB + 96],
                            g[:, O_SG:O_SG + 1024], g[:, O_RW:O_RW + 1920], g[:, O_GATE:O_GATE + 3072]], axis=1)


def _layer_weights(full, rep, l):
    w = {}
    w["w_in"] = _w_in_padded(full["w_in"][l])
    uq = full["w_uq"][l].reshape(Q_LORA, HEADS, QK_NOPE + QK_ROPE)
    w["wq"] = jnp.pad(uq, ((0, 0), (0, 0), (0, 32))).reshape(Q_LORA, HEADS * 128).astype(f32)
    ukv = full["w_ukv"][l].reshape(KV_LORA, HEADS, QK_NOPE + V_HEAD)
    wk = jnp.pad(ukv[:, :, :QK_NOPE], ((0, 0), (0, 0), (0, 64))).reshape(KV_LORA, HEADS * 128)
    w["wk"], w["wv"] = wk.astype(f32), ukv[:, :, QK_NOPE:].reshape(KV_LORA, HEADS * V_HEAD).astype(f32)
    bdiag = lambda t: jnp.concatenate([jnp.concatenate([t[0], jnp.zeros_like(t[0])], axis=1),
                                       jnp.concatenate([jnp.zeros_like(t[1]), t[1]], axis=1)], axis=0).astype(f32)
    w["w2"], w["a2"] = bdiag(full["rw_w2"][l]), bdiag(full["rw_a2"][l])
    w["g2"] = full["rw_g2"][l].astype(f32)
    w["w0"], w["a0"] = full["rw_w0"][l].reshape(1, 2 * RW_DIM), full["rw_a0"][l].reshape(1, 2 * RW_DIM)
    w["gate_b"] = full["gate_b"][l].reshape(1, 3 * D)
    w["wb"], w["wo"] = full["w_branch"][l], full["w_out"][l]
    w["wg"], w["wu"], w["wd"] = full["w_ffn_gate"][l], full["w_ffn_up"][l], full["w_ffn_down"][l]
    row = lambda a: a.reshape(1, -1)
    for k in ("attn_norm_g", "q_norm_g", "kv_norm_g", "sg_ln_g", "sg_ln_b", "rw_k_k", "rw_k_a", "rw_ln_g", "rw_ln_b",
              "ffn_norm_g"):
        w[k] = row(rep[k][l])
    w["r_k"] = row(rep["rw_r_k"][l])
    w["mu"] = jnp.pad(row(rep["rw_mu"][l]), ((0, 0), (0, RW_W - 1920)))
    w["sg_w"] = [rep["sg_w"][l, k] for k in range(SG_GROUPS)]
    w["sg_bias"] = jnp.repeat(rep["sg_b"][l].T, SG_DIM // SG_GROUPS, axis=1)
    return w


def _layer_fwd(x2, w, tabs, bsz, seq, l):
    nm = lambda s: f"l{l}_{s}"
    n = x2.shape[0]
    tm = min(256, n)
    p, h = _inproj_fwd(x2, w["attn_norm_g"], w["w_in"], nm("inproj"))
    mla_rows = [(p, 256, O_CKV // 256), (p, 128, O_SLAB // 128), (p, 384, O_Q // 384), (tabs[0], 128, 0), (tabs[1], 128, 0)]
    mla_w = [w["q_norm_g"], w["kv_norm_g"], w["wq"], w["wk"], w["wv"]]
    q, k, v = _rowwise_fwd(nm("mla_proj"), _f_mla_proj, mla_rows, mla_w, [(1024, bf16), (1024, bf16), (512, bf16)], tm)
    ya = _attn_fwd(q, k, v, bsz, seq, nm("attn"))
    sg_rows = [(p, SG_DIM, O_SG // SG_DIM), (p, SG_DIM, O_SG // SG_DIM + 1)]
    sg_w = [w["sg_ln_g"], w["sg_ln_b"], w["sg_bias"]] + w["sg_w"]
    (yb,) = _rowwise_fwd(nm("sg"), _f_sg, sg_rows, sg_w, [(SG_DIM, f32)], SG_CHUNK)
    z = _shift_fwd(p, w["mu"], seq, nm("shift"))
    pre_rows = [(z, 512, 1), (z, 128, 12), (z, 128, 13), (z, 128, 14)]
    pre_w = [w["w0"], w["a0"], w["w2"], w["a2"], w["g2"], w["rw_k_k"], w["rw_k_a"]]
    lw, kd, kk, bd, g = _rowwise_fwd(nm("rw_pre"), _f_rw_pre, pre_rows, pre_w,
                                     [(1024, f32), (1024, f32), (512, f32), (1024, f32), (512, f32)], tm)
    y, s_in = _scan_fwd(z, lw, kd, kk, bd, bsz, seq, nm("scan"))
    post_rows = [(y[0], 512, 0), (y[1], 512, 0), (z, 512, 0), (z, 512, 2), (kd, 512, 0), (kd, 512, 1), (g, 512, 0)]
    post_w = [w["r_k"], w["rw_ln_g"], w["rw_ln_b"]]
    (yc,) = _rowwise_fwd(nm("rw_post"), _f_rw_post, post_rows, post_w, [(512, f32)], tm)
    x1 = _merge_fwd(x2, p, ya, yb, yc, w["gate_b"], w["wb"], w["wo"], nm("merge"))
    x3 = _ffn_fwd(x1, w["ffn_norm_g"], w["wg"], w["wu"], w["wd"], nm("ffn"))
    saved = dict(x=x2, p=p, h=h, q=q, k=k, v=v, ya=ya, yb=yb, z=z, lw=lw, kd=kd, kk=kk, bd=bd, g=g, y=y, s_in=s_in, yc=yc,
                 x1=x1, mla_rows=mla_rows, mla_w=mla_w, sg_rows=sg_rows, sg_w=sg_w, pre_rows=pre_rows, pre_w=pre_w,
                 post_rows=post_rows, post_w=post_w)
    return x3, saved


def _layer_bwd(dx3, w, sv, bsz, seq, l):
    nm = lambda s: f"l{l}_{s}_bwd"
    n = dx3.shape[0]
    tm = min(256, n)
    g = {}
    dx1, g["ffn_norm_g"], h2, da, db, hm = _ffn_bwd(dx3, sv["x1"], w["ffn_norm_g"], w["wg"], w["wu"], w["wd"], nm("ffn"))
    g["wg"] = _matmul_tn(h2, da, nm("wg"))
    g["wu"] = _matmul_tn(h2, db, nm("wu"))
    g["wd"] = _matmul_tn(hm, dx3.astype(bf16), nm("wd"))
    dpg, dya, dyb, dyc, dt, mg, g["gate_b"] = _merge_bwd(dx1, sv["p"], sv["ya"], sv["yb"], sv["yc"], w["gate_b"], w["wb"],
                                                         w["wo"], nm("merge"))
    g["wo"] = _matmul_tn(mg, dx1.astype(bf16), nm("wo"))
    ys = (sv["ya"], sv["yb"], sv["yc"])
    g["wb"] = jnp.stack([_matmul_tn(ys[k].astype(bf16), dt[:, k * D:(k + 1) * D], nm(f"wb{k}")) for k in range(3)])
    (dy, dr_p, dv_p, dkd0, dkd1, dg_), (g["r_k"], g["rw_ln_g"], g["rw_ln_b"]) = _rowwise_bwd(
        nm("rw_post"), _f_rw_post, sv["post_rows"], sv["post_w"], [(dyc, 512, 0)], tm, [f32, None] + [f32] * 5)
    dkd_p = jnp.concatenate([dkd0, dkd1], axis=1)
    dr_s, dv_s, dkk_s, dlw, dkd_s, dbd = _scan_bwd(sv["z"], sv["lw"], sv["kd"], sv["kk"], sv["bd"], sv["s_in"], dy,
                                                   bsz, seq, nm("scan"))
    pre_cots = [(dlw, 1024, 0), (dkd_s + dkd_p, 1024, 0), (dkk_s[0] + dkk_s[1], 512, 0), (dbd, 1024, 0), (dg_, 512, 0)]
    (dk, dwl, dal, dgl), (g["w0"], g["a0"], g["w2"], g["a2"], g["g2"], g["rw_k_k"], g["rw_k_a"]) = _rowwise_bwd(
        nm("rw_pre"), _f_rw_pre, sv["pre_rows"], sv["pre_w"], pre_cots, tm, [f32] * 4)
    dz = jnp.concatenate([dr_s[0] + dr_s[1] + dr_p, dk, dv_s[0] + dv_s[1] + dv_p, dwl, dal, dgl,
                          jnp.zeros((n, RW_W - 1920), f32)], axis=1)
    dp_rw, g["mu"] = _shift_bwd(dz, sv["p"], w["mu"], seq, nm("shift"))
    (dp_su, dp_sv), (g["sg_ln_g"], g["sg_ln_b"], g["sg_bias"], *sgw) = _rowwise_bwd(
        nm("sg"), _f_sg, sv["sg_rows"], sv["sg_w"], [(dyb, SG_DIM, 0)], SG_CHUNK, [bf16, bf16])
    g["sg_w"] = jnp.stack(sgw)
    dq, dk_, dv_ = _attn_bwd(sv["q"], sv["k"], sv["v"], sv["ya"], dya, bsz, seq, nm("attn"))
    (dp_ckv, dp_slab, dp_q), (g["q_norm_g"], g["kv_norm_g"], g["wq"], g["wk"], g["wv"]) = _rowwise_bwd(
        nm("mla_proj"), _f_mla_proj, sv["mla_rows"], sv["mla_w"], [(dq, 1024, 0), (dk_, 1024, 0), (dv_, 512, 0)], tm,
        [bf16, bf16, bf16, None, None])
    dp = jnp.concatenate([dpg, dp_su, dp_sv, dp_rw, dp_ckv, dp_slab, dp_q, jnp.zeros((n, P_W - O_MLA - MLA_W), bf16)],
                         axis=1)
    g["w_in"] = _matmul_tn(sv["h"], dp, nm("w_in"))
    dx, g["attn_norm_g"] = _norm_matmul_bwd(dp, w["w_in"], sv["x"], w["attn_norm_g"], dx1, nm("inproj"))
    return dx, g


def _layer_grads_to_full(g):
    o = {}
    o["w_in"] = _w_in_unpadded(g["w_in"])
    o["w_uq"] = g["wq"].reshape(Q_LORA, HEADS, 128)[:, :, :QK_NOPE + QK_ROPE].reshape(Q_LORA, -1)
    gk = g["wk"].reshape(KV_LORA, HEADS, 128)[:, :, :QK_NOPE]
    gv = g["wv"].reshape(KV_LORA, HEADS, V_HEAD)
    o["w_ukv"] = jnp.concatenate([gk, gv], axis=2).reshape(KV_LORA, -1)
    unb = lambda t: jnp.stack([t[:LORA, :RW_DIM], t[LORA:, RW_DIM:]])
    o["rw_w2"], o["rw_a2"], o["rw_g2"] = unb(g["w2"]), unb(g["a2"]), g["g2"]
    o["rw_w0"], o["rw_a0"] = g["w0"].reshape(2, RW_DIM), g["a0"].reshape(2, RW_DIM)
    o["gate_b"] = g["gate_b"].reshape(3, D)
    o["w_branch"], o["w_out"] = g["wb"], g["wo"]
    o["w_ffn_gate"], o["w_ffn_up"], o["w_ffn_down"] = g["wg"], g["wu"], g["wd"]
    for k in ("attn_norm_g", "q_norm_g", "kv_norm_g", "sg_ln_g", "sg_ln_b", "rw_k_k", "rw_k_a", "rw_ln_g", "rw_ln_b",
              "ffn_norm_g"):
        o[k] = g[k].reshape(-1)
    o["rw_r_k"] = g["r_k"].reshape(HEADS, RW_HEAD)
    o["rw_mu"] = g["mu"].reshape(-1)[:1920]
    o["sg_w"] = g["sg_w"]
    o["sg_b"] = g["sg_bias"].reshape(SG_CHUNK, SG_GROUPS, SG_DIM // SG_GROUPS).sum(axis=2).T
    return o


def _rope_tables(positions):
    inv = 1.0 / (10000.0 ** (jnp.arange(0, QK_ROPE, 2, dtype=f32) / QK_ROPE))
    ang = positions.astype(f32)[:, None] * inv
    cos, sin = jnp.cos(ang), jnp.sin(ang)
    n = positions.shape[0]
    c = jnp.concatenate([jnp.ones((n, 64), f32), cos, cos, jnp.zeros((n, 32), f32)], axis=1)
    s = jnp.concatenate([jnp.zeros((n, 64), f32), -sin, sin, jnp.zeros((n, 32), f32)], axis=1)
    return c, s


def _local_step(x, positions, full, rep, loss_target):
    bsz, seq, _ = x.shape
    n = bsz * seq
    depth = rep["attn_norm_g"].shape[0]
    x2 = x.reshape(n, D)
    tabs = _rope_tables(positions.reshape(n))
    ws, saved = [], []
    for l in range(depth):
        w = _layer_weights(full, rep, l)
        x2, sv = _layer_fwd(x2, w, tabs, bsz, seq, l)
        ws.append(w)
        saved.append(sv)
    loss, dx, dgf = _loss_head(x2, rep["final_norm_g"].reshape(1, D), loss_target.reshape(n, D), "loss_head")
    per_layer = [None] * depth
    for l in reversed(range(depth)):
        dx, g = _layer_bwd(dx, ws[l], saved[l], bsz, seq, l)
        per_layer[l] = _layer_grads_to_full(g)
    grads = {k: jnp.stack([per_layer[l][k] for l in range(depth)]) for k in per_layer[0]}
    grads["final_norm_g"] = dgf.reshape(D)
    return loss[0, 0], dx.reshape(bsz, seq, D), grads


def kernel(x, positions, attn_norm_g, w_in, gate_b, q_norm_g, w_uq, kv_norm_g, w_ukv, sg_ln_g, sg_ln_b, sg_w, sg_b, rw_mu, rw_w0, rw_w2, rw_a0, rw_a2, rw_g2, rw_k_k, rw_k_a, rw_r_k, rw_ln_g, rw_ln_b, w_branch, w_out, ffn_norm_g, w_ffn_gate, w_ffn_up, w_ffn_down, final_norm_g, loss_target, m_attn_norm_g, m_w_in, m_gate_b, m_q_norm_g, m_w_uq, m_kv_norm_g, m_w_ukv, m_sg_ln_g, m_sg_ln_b, m_sg_w, m_sg_b, m_rw_mu, m_rw_w0, m_rw_w2, m_rw_a0, m_rw_a2, m_rw_g2, m_rw_k_k, m_rw_k_a, m_rw_r_k, m_rw_ln_g, m_rw_ln_b, m_w_branch, m_w_out, m_ffn_norm_g, m_w_ffn_gate, m_w_ffn_up, m_w_ffn_down, m_final_norm_g, v_attn_norm_g, v_w_in, v_gate_b, v_q_norm_g, v_w_uq, v_kv_norm_g, v_w_ukv, v_sg_ln_g, v_sg_ln_b, v_sg_w, v_sg_b, v_rw_mu, v_rw_w0, v_rw_w2, v_rw_a0, v_rw_a2, v_rw_g2, v_rw_k_k, v_rw_k_a, v_rw_r_k, v_rw_ln_g, v_rw_ln_b, v_w_branch, v_w_out, v_ffn_norm_g, v_w_ffn_gate, v_w_ffn_up, v_w_ffn_down, v_final_norm_g):
    args = locals()
    wts = {k: args[k] for k in WEIGHTS}
    mom_m = {k: args["m_" + k] for k in WEIGHTS}
    mom_v = {k: args["v_" + k] for k in WEIGHTS}
    shapes = {k: wts[k].shape for k in WEIGHTS}
    got = _all_gather([wts[k].astype(bf16) for k in BIG] + [_pack128(wts, SMALL_BF, bf16), _pack128(wts, GATHER_F32, f32)],
                      "gather_weights")
    blocks = dict(zip(BIG, got[:len(BIG)]))
    blocks.update(_unpack128(got[-2], shapes, SMALL_BF, lead=1))
    blocks.update(_unpack128(got[-1], shapes, GATHER_F32, lead=1))
    full = {k: _join_blocks(blocks[k], SHARDED[k]) for k in SHARDED}
    rep = {k: wts[k] for k in REPLICATED}
    loss, grad_x, grads = _local_step(x, positions, full, rep, loss_target)
    loss = lax.psum(loss, ("x", "y", "c"))
    split = {k: _split_blocks(grads[k], SHARDED[k]) for k in SHARDED}
    got = _exchange_grads([split[k].astype(bf16) for k in BIG] + [_pack128(split, SMALL, f32, lead=1)],
                          _pack128(grads, REPLICATED, f32), "exchange_grads")
    gw, delta, new_m, new_v = {}, {}, {}, {}
    for k, parts in zip(BIG, got):
        two = lambda a, k=k: a.reshape(a.shape[:a.ndim - len(shapes[k])] + (-1, shapes[k][-1]))
        res = _adamw(two(wts[k]), two(parts), two(mom_m[k]), two(mom_v[k]), f"adamw_{k}")
        gw[k], delta[k], new_m[k], new_v[k] = (t.reshape(shapes[k]) for t in res)
    for names, parts, tag in ((SMALL, got[-2], "small"), (REPLICATED, got[-1], "replicated")):
        pk = lambda dct: _pack128(dct, names, f32)
        res = _adamw(pk(wts), parts, pk(mom_m), pk(mom_v), f"adamw_{tag}")
        for dst, t in zip((gw, delta, new_m, new_v), res):
            dst.update(_unpack128(t, shapes, names))
    return (loss, grad_x, *[gw[k] for k in WEIGHTS], *[delta[k] for k in WEIGHTS], *[new_m[k] for k in WEIGHTS],
            *[new_v[k] for k in WEIGHTS])
```

```python
import functools

import jax
import jax.numpy as jnp
from jax import lax
from jax.experimental import pallas as pl
from jax.experimental.pallas import tpu as pltpu

f32 = jnp.float32
bf16 = jnp.bfloat16
HI = lax.Precision.HIGHEST
NN, NT, TN = ((1,), (0,)), ((1,), (1,)), ((0,), (0,))

N_DEV = 8
D = 1024
HEADS = 8
Q_LORA, KV_LORA, QK_NOPE, QK_ROPE, V_HEAD = 384, 256, 64, 32, 64
SG_DIM, SG_CHUNK, SG_GROUPS = 512, 128, 8
RW_DIM, RW_HEAD, LORA = 512, 64, 64
D_FF = 2816
N_IN = 6688
NORM_EPS, LN_EPS, GN_EPS = 1e-6, 1e-5, 64e-5
ATT_SCALE = (QK_NOPE + QK_ROPE) ** -0.5
P_W = 7168
O_GATE, O_SG, O_RW, O_MLA = 0, 3072, 4096, 6144
RW_W = 2048
MLA_W = 768
O_CKV, O_SLAB, O_Q = O_MLA, O_MLA + 256, O_MLA + 384
CHUNK = 128
VMEM_LIMIT = 56 * 1024 * 1024

B1, B2, LR, EPS, WD, STEP = 0.9, 0.999, 0.001, 1e-8, 0.01, 10


def _pc(body, *, name, out_shape, grid=(), in_specs=None, out_specs=None, scratch=(), sem=None, aliases=None, **cp):
    params = pltpu.CompilerParams(dimension_semantics=sem, vmem_limit_bytes=VMEM_LIMIT, **cp)
    kw = {}
    if in_specs is not None:
        kw["in_specs"] = in_specs
    if out_specs is not None:
        kw["out_specs"] = out_specs
    return pl.pallas_call(body, out_shape=out_shape, grid=grid, scratch_shapes=scratch, compiler_params=params,
                          name=name, input_output_aliases=aliases or {}, interpret=False, **kw)


def _sds(shape, dtype=f32):
    return jax.ShapeDtypeStruct(tuple(shape), dtype)


def _dot(a, b, dims, precision=None):
    return lax.dot_general(a, b, (dims, ((), ())), preferred_element_type=f32, precision=precision)


def _bdot(a, b, dims=NN):
    return _dot(a.astype(bf16), b.astype(bf16), dims)


@jax.custom_vjp
def _mm(a, w):
    return _bdot(a, w, NN)


def _mm_fwd(a, w):
    return _bdot(a, w, NN), (a, w)


def _mm_bwd(res, g):
    a, w = res
    return _bdot(g, w, NT), _bdot(a, g, TN)


_mm.defvjp(_mm_fwd, _mm_bwd)


@jax.custom_vjp
def _mm_nt(a, b):
    return _bdot(a, b, NT)


def _mm_nt_fwd(a, b):
    return _bdot(a, b, NT), (a, b)


def _mm_nt_bwd(res, g):
    a, b = res
    return _bdot(g, b, NN), _bdot(g, a, TN)


_mm_nt.defvjp(_mm_nt_fwd, _mm_nt_bwd)


@jax.custom_vjp
def _mm_tn(a, b):
    return _bdot(a, b, TN)


def _mm_tn_fwd(a, b):
    return _bdot(a, b, TN), (a, b)


def _mm_tn_bwd(res, g):
    a, b = res
    return _bdot(b, g, NT), _bdot(a, g, NN)


_mm_tn.defvjp(_mm_tn_fwd, _mm_tn_bwd)


def _rms(x, g):
    return x * lax.rsqrt(jnp.mean(x * x, axis=-1, keepdims=True) + NORM_EPS) * g


def _sigmoid(x):
    return 1.0 / (1.0 + jnp.exp(-x))


def _gelu(x):
    return 0.5 * x * (1.0 + jnp.tanh(0.7978845608028654 * (x + 0.044715 * x * x * x)))


def _softplus(x):
    return jnp.maximum(x, 0.0) + jnp.log(1.0 + jnp.exp(-jnp.abs(x)))


def _group_sum(x):
    w = x.shape[-1]
    r = lax.broadcasted_iota(jnp.int32, (w, w), 0) // RW_HEAD
    c = lax.broadcasted_iota(jnp.int32, (w, w), 1) // RW_HEAD
    return _dot(x, (r == c).astype(f32), NN, HI)


@jax.custom_vjp
def _swap(x):
    w = x.shape[-1]
    lane = lax.broadcasted_iota(jnp.int32, x.shape, 1) % 128
    lo = (lane >= 64) & (lane < 80)
    hi = (lane >= 80) & (lane < 96)
    return jnp.where(lo, pltpu.roll(x, w - 16, 1), jnp.where(hi, pltpu.roll(x, 16, 1), 0.0))


_swap.defvjp(lambda x: (_swap(x), None), lambda _, g: (_swap(g),))


def _rope(x, c, s):
    return x * c + _swap(x) * s


def _row_spec(tm, width, blk):
    return pl.BlockSpec((tm, width), lambda i, blk=blk: (i, blk))


def _full_spec(a):
    nd = a.ndim
    return pl.BlockSpec(a.shape, lambda i, nd=nd: (0,) * nd)


def _rowwise_fwd(name, f, rows, weights, outs, tm):
    n = rows[0][0].shape[0]
    nr, nw = len(rows), len(weights)

    def body(*refs):
        vals = [r[...].astype(f32) for r in refs[:nr + nw]]
        res = f(*vals)
        for o_ref, o in zip(refs[nr + nw:], res):
            o_ref[...] = o.astype(o_ref.dtype)

    return _pc(
        body, name=name, grid=(n // tm,),
        in_specs=[_row_spec(tm, w, b) for _, w, b in rows] + [_full_spec(w) for w in weights],
        out_specs=[_row_spec(tm, w, 0) for w, _ in outs],
        out_shape=[_sds((n, w), dt) for w, dt in outs], sem=("parallel",),
    )(*[a for a, _, _ in rows], *weights)


def _rowwise_bwd(name, f, rows, weights, cots, tm, drows):
    n = rows[0][0].shape[0]
    nr, nw, nc = len(rows), len(weights), len(cots)
    want = [k for k, dt in enumerate(drows) if dt is not None]

    def body(*refs):
        vals = [r[...].astype(f32) for r in refs[:nr + nw]]
        cot = tuple(r[...].astype(f32) for r in refs[nr + nw:nr + nw + nc])
        _, vjp = jax.vjp(f, *vals)
        grads = vjp(cot)
        outs = refs[nr + nw + nc:]
        for o_ref, k in zip(outs[:len(want)], want):
            o_ref[...] = grads[k].astype(o_ref.dtype)
        first = pl.program_id(0) == 0
        for o_ref, g in zip(outs[len(want):], grads[nr:]):
            @pl.when(first)
            def _(o_ref=o_ref, g=g):
                o_ref[...] = g

            @pl.when(jnp.logical_not(first))
            def _(o_ref=o_ref, g=g):
                o_ref[...] += g

    res = _pc(
        body, name=name, grid=(n // tm,),
        in_specs=[_row_spec(tm, w, b) for _, w, b in rows] + [_full_spec(w) for w in weights]
        + [_row_spec(tm, w, b) for _, w, b in cots],
        out_specs=[_row_spec(tm, rows[k][1], 0) for k in want] + [_full_spec(w) for w in weights],
        out_shape=[_sds((n, rows[k][1]), drows[k]) for k in want] + [_sds(w.shape) for w in weights],
        sem=("arbitrary",),
    )(*[a for a, _, _ in rows], *weights, *[a for a, _, _ in cots])
    return res[:len(want)], res[len(want):]


def _inproj_fwd(x2, g, w, name):
    n = x2.shape[0]
    tm, tn = min(512, n), 512

    def body(x_ref, g_ref, w_ref, p_ref, h_ref):
        @pl.when(pl.program_id(1) == 0)
        def _():
            h_ref[...] = _rms(x_ref[...], g_ref[...]).astype(bf16)

        p_ref[...] = jnp.dot(h_ref[...], w_ref[...], preferred_element_type=f32)

    return _pc(
        body, name=name, grid=(n // tm, P_W // tn),
        in_specs=[pl.BlockSpec((tm, D), lambda i, j: (i, 0)), pl.BlockSpec((1, D), lambda i, j: (0, 0)),
                  pl.BlockSpec((D, tn), lambda i, j: (0, j))],
        out_specs=[pl.BlockSpec((tm, tn), lambda i, j: (i, j)), pl.BlockSpec((tm, D), lambda i, j: (i, 0))],
        out_shape=[_sds((n, P_W)), _sds((n, D), bf16)], sem=("parallel", "arbitrary"),
    )(x2, g, w)


def _norm_matmul_bwd(dy, w, x2, g, dres, name):
    n, k = dy.shape
    tm = min(512, n)
    tk = 1024 if k % 1024 == 0 else 1408
    nk = k // tk

    def body(dy_ref, w_ref, x_ref, g_ref, dr_ref, dx_ref, dg_ref, acc):
        i, j = pl.program_id(0), pl.program_id(1)

        @pl.when(j == 0)
        def _():
            acc[...] = jnp.zeros_like(acc)

        @pl.when((i == 0) & (j == 0))
        def _():
            dg_ref[...] = jnp.zeros_like(dg_ref)

        acc[...] += _dot(dy_ref[...], w_ref[...], NT)

        @pl.when(j == nk - 1)
        def _():
            _, vjp = jax.vjp(_rms, x_ref[...], g_ref[...])
            dx, dg = vjp(acc[...])
            dx_ref[...] = dr_ref[...] + dx
            dg_ref[...] += dg

    return _pc(
        body, name=name, grid=(n // tm, nk),
        in_specs=[pl.BlockSpec((tm, tk), lambda i, j: (i, j)), pl.BlockSpec((D, tk), lambda i, j: (0, j)),
                  pl.BlockSpec((tm, D), lambda i, j: (i, 0)), pl.BlockSpec((1, D), lambda i, j: (0, 0)),
                  pl.BlockSpec((tm, D), lambda i, j: (i, 0))],
        out_specs=[pl.BlockSpec((tm, D), lambda i, j: (i, 0)), pl.BlockSpec((1, D), lambda i, j: (0, 0))],
        out_shape=[_sds((n, D)), _sds((1, D))], scratch=[pltpu.VMEM((tm, D), f32)], sem=("arbitrary", "arbitrary"),
    )(dy, w, x2, g, dres)


def _matmul_tn(a, g, name):
    n, k = a.shape
    m = g.shape[1]
    tr = min(512, n)
    tk = k if k <= 1024 else 1408
    tn = m if m <= 1024 else (512 if m % 512 == 0 else 1408)
    nr = n // tr

    def body(a_ref, g_ref, o_ref):
        @pl.when(pl.program_id(2) == 0)
        def _():
            o_ref[...] = jnp.zeros_like(o_ref)

        o_ref[...] += _dot(a_ref[...], g_ref[...], TN)

    return _pc(
        body, name=name, grid=(k // tk, m // tn, nr),
        in_specs=[pl.BlockSpec((tr, tk), lambda i, j, r: (r, i)), pl.BlockSpec((tr, tn), lambda i, j, r: (r, j))],
        out_specs=pl.BlockSpec((tk, tn), lambda i, j, r: (i, j)),
        out_shape=_sds((k, m)), sem=("parallel", "parallel", "arbitrary"),
    )(a, g)


def _f_mla_proj(ckv, slab, pq, c, s, qg, kg, wq, wk, wv):
    c8, s8 = jnp.concatenate([c] * HEADS, axis=1), jnp.concatenate([s] * HEADS, axis=1)
    q = _rope(_mm(_rms(pq, qg), wq), c8, s8)
    cn = _rms(ckv, kg)
    k = _mm(cn, wk) + jnp.concatenate([_rope(slab, c, s)] * HEADS, axis=1)
    return q, k, _mm(cn, wv)


def _attn_fwd(q, k, v, bsz, seq, name):
    n = q.shape[0]
    tq = min(256, seq)
    nq = seq // tq

    def body(q_ref, k_ref, v_ref, o_ref):
        lane = lax.broadcasted_iota(jnp.int32, (tq, 128), 1) < 64
        vv = v_ref[...]
        outs = []
        for h in range(2):
            s = _dot(q_ref[:, h * 128:(h + 1) * 128], k_ref[:, h * 128:(h + 1) * 128], NT) * ATT_SCALE
            e = jnp.exp(s - jnp.max(s, axis=-1, keepdims=True))
            p = (e / jnp.sum(e, axis=-1, keepdims=True)).astype(bf16)
            outs.append(_dot(p, vv, NN))
        o_ref[...] = jnp.where(lane, outs[0], outs[1])

    return _pc(
        body, name=name, grid=(bsz, HEADS // 2, nq),
        in_specs=[pl.BlockSpec((tq, 256), lambda b, h, i: (b * nq + i, h)),
                  pl.BlockSpec((seq, 256), lambda b, h, i: (b, h)),
                  pl.BlockSpec((seq, 128), lambda b, h, i: (b, h))],
        out_specs=pl.BlockSpec((tq, 128), lambda b, h, i: (b * nq + i, h)),
        out_shape=_sds((n, HEADS * V_HEAD)), sem=("parallel", "parallel", "parallel"),
    )(q, k, v)


def _attn_bwd(q, k, v, o, do, bsz, seq, name):
    n = q.shape[0]
    tq = min(256, seq)
    nq = seq // tq

    def body(q_ref, k_ref, v_ref, o_ref, do_ref, dq_ref, dk_ref, dv_ref):
        @pl.when(pl.program_id(2) == 0)
        def _():
            dk_ref[...] = jnp.zeros_like(dk_ref)
            dv_ref[...] = jnp.zeros_like(dv_ref)

        lane = lax.broadcasted_iota(jnp.int32, (tq, 128), 1) < 64
        vv = v_ref[...]
        for h in range(2):
            qh, kh = q_ref[:, h * 128:(h + 1) * 128], k_ref[:, h * 128:(h + 1) * 128]
            s = _dot(qh, kh, NT) * ATT_SCALE
            e = jnp.exp(s - jnp.max(s, axis=-1, keepdims=True))
            p = e / jnp.sum(e, axis=-1, keepdims=True)
            doh = jnp.where(lane if h == 0 else jnp.logical_not(lane), do_ref[...], 0.0)
            delta = jnp.sum(doh * o_ref[...], axis=-1, keepdims=True)
            dob = doh.astype(bf16)
            dp = _dot(dob, vv, NT)
            ds = (p * (dp - delta) * ATT_SCALE).astype(bf16)
            dq_ref[:, h * 128:(h + 1) * 128] = _dot(ds, kh, NN)
            dk_ref[:, h * 128:(h + 1) * 128] += _dot(ds, qh, TN)
            dv_ref[...] += _dot(p.astype(bf16), dob, TN)

    return _pc(
        body, name=name, grid=(bsz, HEADS // 2, nq),
        in_specs=[pl.BlockSpec((tq, 256), lambda b, h, i: (b * nq + i, h)),
                  pl.BlockSpec((seq, 256), lambda b, h, i: (b, h)),
                  pl.BlockSpec((seq, 128), lambda b, h, i: (b, h)),
                  pl.BlockSpec((tq, 128), lambda b, h, i: (b * nq + i, h)),
                  pl.BlockSpec((tq, 128), lambda b, h, i: (b * nq + i, h))],
        out_specs=[pl.BlockSpec((tq, 256), lambda b, h, i: (b * nq + i, h)),
                   pl.BlockSpec((seq, 256), lambda b, h, i: (b, h)),
                   pl.BlockSpec((seq, 128), lambda b, h, i: (b, h))],
        out_shape=[_sds((n, HEADS * 128)), _sds((n, HEADS * 128)), _sds((n, HEADS * V_HEAD))],
        sem=("parallel", "parallel", "arbitrary"),
    )(q, k, v, o, do)


def _f_sg(pu, pv, lg, lb, bias, *ws):
    u, vv = _gelu(pu), _gelu(pv)
    mu = jnp.mean(vv, axis=-1, keepdims=True)
    d = vv - mu
    vv = d * lax.rsqrt(jnp.mean(d * d, axis=-1, keepdims=True) + LN_EPS) * lg + lb
    group = lax.broadcasted_iota(jnp.int32, (SG_CHUNK, SG_DIM), 1) // (SG_DIM // SG_GROUPS)
    mixed = bias
    for k, w in enumerate(ws):
        mixed = mixed + jnp.where(group == k, _mm(w, vv), 0.0)
    return (u * mixed,)


def _shift_mean(a, prev_row, next_row):
    t = a.shape[0]
    row = lax.broadcasted_iota(jnp.int32, a.shape, 0)
    prev = jnp.where(row == 0, prev_row, pltpu.roll(a, 1, 0))
    nxt = jnp.where(row == t - 1, next_row, pltpu.roll(a, t - 1, 0))
    return 0.5 * (prev + nxt)


def _halo_specs(tm, width, blk, nblk8):
    h = tm // 8
    return [pl.BlockSpec((tm, width), lambda i: (i, blk)),
            pl.BlockSpec((8, width), lambda i: (jnp.maximum(i * h - 1, 0), blk)),
            pl.BlockSpec((8, width), lambda i: (jnp.minimum((i + 1) * h, nblk8 - 1), blk))]


def _edge_rows(i, tm, seq, pv_ref, nx_ref, scale=None):
    first = (i * tm) % seq == 0
    last = ((i + 1) * tm) % seq == 0
    pv, nx = pv_ref[7:8, :], nx_ref[0:1, :]
    if scale is not None:
        pv, nx = pv * scale, nx * scale
    return jnp.where(first, 0.0, pv), jnp.where(last, 0.0, nx)


def _shift_fwd(p, mu, seq, name):
    n = p.shape[0]
    tm = min(256, seq)
    blk = O_RW // RW_W

    def body(x_ref, pv_ref, nx_ref, mu_ref, z_ref):
        x = x_ref[...]
        pv, nx = _edge_rows(pl.program_id(0), tm, seq, pv_ref, nx_ref)
        z_ref[...] = x + mu_ref[...] * (_shift_mean(x, pv, nx) - x)

    return _pc(
        body, name=name, grid=(n // tm,),
        in_specs=_halo_specs(tm, RW_W, blk, n // 8) + [pl.BlockSpec((1, RW_W), lambda i: (0, 0))],
        out_specs=pl.BlockSpec((tm, RW_W), lambda i: (i, 0)), out_shape=_sds((n, RW_W)), sem=("parallel",),
    )(p, p, p, mu)


def _shift_bwd(dz, p, mu, seq, name):
    n = p.shape[0]
    tm = min(256, seq)
    blk = O_RW // RW_W

    def body(dz_ref, dpv_ref, dnx_ref, x_ref, pv_ref, nx_ref, mu_ref, dx_ref, dmu_ref):
        i = pl.program_id(0)
        mu_v = mu_ref[...]
        dzv = dz_ref[...]
        m = dzv * mu_v
        mpv, mnx = _edge_rows(i, tm, seq, dpv_ref, dnx_ref, mu_v)
        dx_ref[...] = (dzv - m + _shift_mean(m, mpv, mnx)).astype(dx_ref.dtype)
        x = x_ref[...]
        pv, nx = _edge_rows(i, tm, seq, pv_ref, nx_ref)
        part = jnp.sum(dzv * (_shift_mean(x, pv, nx) - x), axis=0, keepdims=True)

        @pl.when(i == 0)
        def _():
            dmu_ref[...] = part

        @pl.when(i != 0)
        def _():
            dmu_ref[...] += part

    return _pc(
        body, name=name, grid=(n // tm,),
        in_specs=_halo_specs(tm, RW_W, 0, n // 8) + _halo_specs(tm, RW_W, blk, n // 8)
        + [pl.BlockSpec((1, RW_W), lambda i: (0, 0))],
        out_specs=[pl.BlockSpec((tm, RW_W), lambda i: (i, 0)), pl.BlockSpec((1, RW_W), lambda i: (0, 0))],
        out_shape=[_sds((n, RW_W), bf16), _sds((1, RW_W))], sem=("arbitrary",),
    )(dz, dz, dz, p, p, p, mu)


def _f_rw_pre(k, wl, al, gl, w0, a0, w2, a2, g2, k_k, k_a):
    w = w0 + _mm(jnp.tanh(wl), w2)
    lw = -jnp.exp(-_softplus(-w) - 0.5)
    a = _sigmoid(a0 + _mm(al, a2))
    g = _mm(_sigmoid(gl), g2)
    kkr = k * k_k
    kk = kkr / jnp.maximum(jnp.sqrt(_group_sum(kkr * kkr)), 1e-12)
    two = lambda t: jnp.concatenate([t, t], axis=1)
    kd = two(k) * (1.0 + (a - 1.0) * two(k_a))
    bd = two(kk) * a
    return lw, kd, kk, bd, g


def _f_rw_post(y0, y1, r, v, kd0, kd1, g, r_k, ln_g, ln_b):
    y = y0 + y1
    mean = _group_sum(y) * (1.0 / RW_HEAD)
    d = y - mean
    var = _group_sum(d * d) * (1.0 / RW_HEAD)
    yn = d * lax.rsqrt(var + GN_EPS) * ln_g + ln_b
    bonus = _group_sum(r * (kd0 + kd1) * r_k)
    return ((yn + bonus * v) * g,)


@jax.custom_vjp
def _tri_inv(mats):
    c = mats[0].shape[0]
    row = lax.broadcasted_iota(jnp.int32, (c, c), 0)
    col = lax.broadcasted_iota(jnp.int32, (c, c), 1)
    eye = (row == col).astype(f32)
    blk = lambda b: (row // b) == (col // b)
    ld = [jnp.where(blk(8), a, 0.0) for a in mats]
    l2 = [_bdot(x, x) for x in ld]
    l4 = [_bdot(x, x) for x in l2]
    t = [_bdot(eye - x, eye + y) for x, y in zip(ld, l2)]
    t = [_bdot(x, eye + y) for x, y in zip(t, l4)]
    b = 8
    while b < c:
        sub = blk(2 * b) & jnp.logical_not(blk(b))
        p = [_bdot(x, jnp.where(sub, a, 0.0)) for x, a in zip(t, mats)]
        t = [x - _bdot(y, x) for x, y in zip(t, p)]
        b *= 2
    return tuple(t)


def _tri_inv_fwd(mats):
    t = _tri_inv(mats)
    return t, t


def _tri_inv_bwd(ts, gs):
    p = [_bdot(t, g, TN) for t, g in zip(ts, gs)]
    return (tuple(-_bdot(x, t, NT) for x, t in zip(p, ts)),)


_tri_inv.defvjp(_tri_inv_fwd, _tri_inv_bwd)


def _split3(x):
    h = x.astype(bf16)
    r = x - h.astype(f32)
    m = r.astype(bf16)
    return h, m, (r - m.astype(f32)).astype(bf16)


@jax.custom_vjp
def _mask_mm(mask, x):
    mb = mask.astype(bf16)
    return _dot(jnp.concatenate([mb, mb, mb], axis=1), jnp.concatenate(_split3(x), axis=0), NN)


def _mask_mm_bwd(mask, g):
    mb = mask.astype(bf16)
    return jnp.zeros_like(mask), _dot(jnp.concatenate([mb, mb, mb], axis=0), jnp.concatenate(_split3(g), axis=0), TN)


_mask_mm.defvjp(lambda mask, x: (_mask_mm(mask, x), mask), _mask_mm_bwd)


@jax.custom_vjp
def _split_lanes(x):
    h = x.shape[1] // 2
    return x[:, :h], x[:, h:]


_split_lanes.defvjp(lambda x: (_split_lanes(x), None), lambda _, g: (jnp.concatenate(g, axis=1),))


def _scan_chunk(s0, r, v, kk, lw, kd, bd, rev):
    n = len(r)
    each = range(n)
    c = r[0].shape[0]
    row = lax.broadcasted_iota(jnp.int32, (c, 2 * c), 0)
    col = lax.broadcasted_iota(jnp.int32, (c, 2 * c), 1) % c
    ahead = jnp.where(rev, col - row, row - col)
    before = ahead > 0
    incl = ahead >= 0
    lane = lax.broadcasted_iota(jnp.int32, (1, 128), 1)
    m0 = (lane < 64).astype(f32)
    heads = lambda t: jnp.concatenate([t * m0, t * (1.0 - m0)], axis=0)
    bd_mask = ((lax.broadcasted_iota(jnp.int32, (128, 128), 0) // 64)
               == (lax.broadcasted_iota(jnp.int32, (128, 128), 1) // 64)).astype(f32)
    tot = [jnp.sum(lw[i], axis=0, keepdims=True) for i in each]
    row1 = lax.broadcasted_iota(jnp.int32, (c, c), 0)
    col1 = lax.broadcasted_iota(jnp.int32, (c, c), 1)
    upto = (jnp.where(rev, col1 - row1, row1 - col1) >= 0).astype(f32)
    lp = [_mask_mm(upto, lw[i]) - 0.5 * tot[i] for i in each]
    eg = [jnp.exp(lp[i]) for i in each]
    ieg = [jnp.exp(-lp[i]) for i in each]
    rt = [r[i] * eg[i] for i in each]
    kt = [kd[i] * ieg[i] for i in each]
    bt = [bd[i] * ieg[i] for i in each]
    at = [kk[i] * jnp.exp(lp[i] - lw[i]) for i in each]
    etot = [jnp.exp(0.5 * tot[i]) for i in each]
    si = [s0[i] * etot[i] for i in each]
    bth = [heads(bt[i]) for i in each]
    kth = [heads(kt[i]) for i in each]
    vh = [heads(v[i]) for i in each]
    a_ab = [jnp.where(before, _mm_nt(at[i], bth[i]), 0.0) for i in each]
    a_ak = [jnp.where(before, _mm_nt(at[i], kth[i]), 0.0) for i in each]
    a_rb = [jnp.where(incl, _mm_nt(rt[i], bth[i]), 0.0) for i in each]
    a_rk = [jnp.where(incl, _mm_nt(rt[i], kth[i]), 0.0) for i in each]
    halves = [_split_lanes(a_ab[i]) for i in each]
    inv = _tri_inv(tuple(m for pair in halves for m in pair))
    t = [jnp.concatenate([inv[2 * i], inv[2 * i + 1]], axis=1) for i in each]
    x0 = [_mm_nt(at[i], si[i]) for i in each]
    x = [x0[i] + _mm(a_ak[i], vh[i]) for i in each]
    u = [-_mm(t[i], heads(x[i])) for i in each]
    y0 = [_mm_nt(rt[i], si[i]) for i in each]
    y = [y0[i] + _mm(jnp.concatenate([a_rb[i], a_rk[i]], axis=1), jnp.concatenate([heads(u[i]), vh[i]], axis=0))
         for i in each]
    ds = [_mm_tn(jnp.concatenate([u[i], v[i]], axis=0), jnp.concatenate([bt[i], kt[i]], axis=0)) for i in each]
    se = [(si[i] + ds[i] * bd_mask) * etot[i] for i in each]
    return tuple(y), tuple(se)


PAIRS = HEADS // 2


def _scan_specs(nc, bsz, flip=False):
    def cc(d, c):
        c = nc - 1 - c if flip else c
        return jnp.where(d == 0, c, nc - 1 - c)

    rowblk = lambda d, b, c: b * nc + cc(d, c)
    zspec = lambda blk: pl.BlockSpec((CHUNK, RW_DIM), lambda d, b, c: (rowblk(d, b, c), blk))
    dspec = pl.BlockSpec((CHUNK, RW_DIM), lambda d, b, c: (rowblk(d, b, c), d))
    yspec = pl.BlockSpec((None, CHUNK, RW_DIM), lambda d, b, c: (d, rowblk(d, b, c), 0))
    sspec = pl.BlockSpec((None, PAIRS, 128, 128), lambda d, b, c: ((d * bsz + b) * nc + cc(d, c), 0, 0, 0))
    return zspec, dspec, yspec, sspec


def _scan_fwd(z, lw, kd, kk, bd, bsz, seq, name):
    n = z.shape[0]
    nc = seq // CHUNK
    zspec, dspec, yspec, sspec = _scan_specs(nc, bsz)

    def body(r_ref, v_ref, kk_ref, lw_ref, kd_ref, bd_ref, y_ref, s_ref, st):
        @pl.when(pl.program_id(2) == 0)
        def _():
            st[...] = jnp.zeros_like(st)

        rev = pl.program_id(0) == 1
        lanes = [slice(h * 128, (h + 1) * 128) for h in range(PAIRS)]
        s0 = tuple(st[h] for h in range(PAIRS))
        ops = [tuple(ref[:, ln] for ln in lanes) for ref in (r_ref, v_ref, kk_ref, lw_ref, kd_ref, bd_ref)]
        y, se = _scan_chunk(s0, *ops, rev)
        for h, ln in enumerate(lanes):
            s_ref[h] = s0[h]
            y_ref[:, ln] = y[h]
            st[h] = se[h]

    return _pc(
        body, name=name, grid=(2, bsz, nc),
        in_specs=[zspec(0), zspec(2), zspec(0), dspec, dspec, dspec],
        out_specs=[yspec, sspec],
        out_shape=[_sds((2, n, RW_DIM)), _sds((2 * bsz * nc, PAIRS, 128, 128))],
        scratch=[pltpu.VMEM((PAIRS, 128, 128), f32)], sem=("parallel", "parallel", "arbitrary"),
    )(z, z, kk, lw, kd, bd)


def _scan_bwd(z, lw, kd, kk, bd, s_in, dy, bsz, seq, name):
    n = z.shape[0]
    nc = seq // CHUNK
    zspec, dspec, yspec, sspec = _scan_specs(nc, bsz, flip=True)

    def body(r_ref, v_ref, kk_ref, lw_ref, kd_ref, bd_ref, s_ref, dy_ref,
             dr_ref, dv_ref, dkk_ref, dlw_ref, dkd_ref, dbd_ref, dst):
        @pl.when(pl.program_id(2) == 0)
        def _():
            dst[...] = jnp.zeros_like(dst)

        rev = pl.program_id(0) == 1
        lanes = [slice(h * 128, (h + 1) * 128) for h in range(PAIRS)]
        s0 = tuple(s_ref[h] for h in range(PAIRS))
        ops = [tuple(ref[:, ln] for ln in lanes) for ref in (r_ref, v_ref, kk_ref, lw_ref, kd_ref, bd_ref)]
        cot = (tuple(dy_ref[:, ln] for ln in lanes), tuple(dst[h] for h in range(PAIRS)))
        _, vjp = jax.vjp(functools.partial(_scan_chunk, rev=rev), s0, *ops)
        grads = vjp(cot)
        for h, ln in enumerate(lanes):
            dst[h] = grads[0][h]
            for o_ref, g in zip((dr_ref, dv_ref, dkk_ref, dlw_ref, dkd_ref, dbd_ref), grads[1:]):
                o_ref[:, ln] = g[h]

    return _pc(
        body, name=name, grid=(2, bsz, nc),
        in_specs=[zspec(0), zspec(2), zspec(0), dspec, dspec, dspec, sspec, zspec(0)],
        out_specs=[yspec, yspec, yspec, dspec, dspec, dspec],
        out_shape=[_sds((2, n, RW_DIM))] * 3 + [_sds((n, 2 * RW_DIM))] * 3,
        scratch=[pltpu.VMEM((PAIRS, 128, 128), f32)], sem=("parallel", "parallel", "arbitrary"),
    )(z, z, kk, lw, kd, bd, s_in, dy)


def _merge_fwd(x2, p, ya, yb, yc, gb, wb, wo, name):
    n = x2.shape[0]
    tm = min(256, n)

    def body(x_ref, pg_ref, ya_ref, yb_ref, yc_ref, gb_ref, wb_ref, wo_ref, o_ref):
        gates = _sigmoid(pg_ref[...] + gb_ref[...])
        merged = jnp.zeros((tm, D), f32)
        for k, y_ref in enumerate((ya_ref, yb_ref, yc_ref)):
            merged += gates[:, k * D:(k + 1) * D] * _bdot(y_ref[...], wb_ref[k])
        o_ref[...] = x_ref[...] + _bdot(merged, wo_ref[...])

    row = lambda w, b=0: pl.BlockSpec((tm, w), lambda i, b=b: (i, b))
    return _pc(
        body, name=name, grid=(n // tm,),
        in_specs=[row(D), row(3 * D, O_GATE // (3 * D)), row(512), row(512), row(512),
                  pl.BlockSpec((1, 3 * D), lambda i: (0, 0)), pl.BlockSpec((3, 512, D), lambda i: (0, 0, 0)),
                  pl.BlockSpec((D, D), lambda i: (0, 0))],
        out_specs=row(D), out_shape=_sds((n, D)), sem=("parallel",),
    )(x2, p, ya, yb, yc, gb, wb, wo)


def _merge_bwd(dx1, p, ya, yb, yc, gb, wb, wo, name):
    n = dx1.shape[0]
    tm = min(256, n)

    def body(dx_ref, pg_ref, ya_ref, yb_ref, yc_ref, gb_ref, wb_ref, wo_ref,
             dpg_ref, dya_ref, dyb_ref, dyc_ref, dt_ref, mg_ref, dgb_ref):
        gates = _sigmoid(pg_ref[...] + gb_ref[...])
        dmerged = _bdot(dx_ref[...], wo_ref[...], NT)
        merged = jnp.zeros((tm, D), f32)
        dpg = []
        for k, (y_ref, dy_ref) in enumerate(((ya_ref, dya_ref), (yb_ref, dyb_ref), (yc_ref, dyc_ref))):
            gk = gates[:, k * D:(k + 1) * D]
            tk = _bdot(y_ref[...], wb_ref[k])
            merged += gk * tk
            dpg.append(dmerged * tk * gk * (1.0 - gk))
            dtk = dmerged * gk
            dt_ref[:, k * D:(k + 1) * D] = dtk.astype(bf16)
            dy_ref[...] = _bdot(dtk, wb_ref[k], NT)
        dpg = jnp.concatenate(dpg, axis=1)
        dpg_ref[...] = dpg.astype(bf16)
        mg_ref[...] = merged.astype(bf16)
        part = jnp.sum(dpg, axis=0, keepdims=True)

        @pl.when(pl.program_id(0) == 0)
        def _():
            dgb_ref[...] = part

        @pl.when(pl.program_id(0) != 0)
        def _():
            dgb_ref[...] += part

    row = lambda w, b=0: pl.BlockSpec((tm, w), lambda i, b=b: (i, b))
    return _pc(
        body, name=name, grid=(n // tm,),
        in_specs=[row(D), row(3 * D, O_GATE // (3 * D)), row(512), row(512), row(512),
                  pl.BlockSpec((1, 3 * D), lambda i: (0, 0)), pl.BlockSpec((3, 512, D), lambda i: (0, 0, 0)),
                  pl.BlockSpec((D, D), lambda i: (0, 0))],
        out_specs=[row(3 * D), row(512), row(512), row(512), row(3 * D), row(D),
                   pl.BlockSpec((1, 3 * D), lambda i: (0, 0))],
        out_shape=[_sds((n, 3 * D), bf16), _sds((n, 512)), _sds((n, 512)), _sds((n, 512)), _sds((n, 3 * D), bf16),
                   _sds((n, D), bf16), _sds((1, 3 * D))],
        sem=("arbitrary",),
    )(dx1, p, ya, yb, yc, gb, wb, wo)


FF_T = 1408


def _ffn_fwd(x1, g, wg, wu, wd, name):
    n = x1.shape[0]
    tm = min(512, n)
    nf = D_FF // FF_T

    def body(x_ref, g_ref, wg_ref, wu_ref, wd_ref, o_ref, hs):
        j = pl.program_id(1)

        @pl.when(j == 0)
        def _():
            hs[...] = _rms(x_ref[...], g_ref[...]).astype(bf16)
            o_ref[...] = x_ref[...]

        a = _dot(hs[...], wg_ref[...], NN)
        b = _dot(hs[...], wu_ref[...], NN)
        o_ref[...] += _bdot(a * _sigmoid(a) * b, wd_ref[...])

    return _pc(
        body, name=name, grid=(n // tm, nf),
        in_specs=[pl.BlockSpec((tm, D), lambda i, j: (i, 0)), pl.BlockSpec((1, D), lambda i, j: (0, 0)),
                  pl.BlockSpec((D, FF_T), lambda i, j: (0, j)), pl.BlockSpec((D, FF_T), lambda i, j: (0, j)),
                  pl.BlockSpec((FF_T, D), lambda i, j: (j, 0))],
        out_specs=pl.BlockSpec((tm, D), lambda i, j: (i, 0)), out_shape=_sds((n, D)),
        scratch=[pltpu.VMEM((tm, D), bf16)], sem=("parallel", "arbitrary"),
    )(x1, g, wg, wu, wd)


def _ffn_bwd(dx2, x1, g, wg, wu, wd, name):
    n = x1.shape[0]
    tm = min(512, n)
    nf = D_FF // FF_T

    def body(dx_ref, x_ref, g_ref, wg_ref, wu_ref, wd_ref, dx1_ref, dg_ref, h_ref, da_ref, db_ref, hm_ref, acc):
        i, j = pl.program_id(0), pl.program_id(1)

        @pl.when(j == 0)
        def _():
            h_ref[...] = _rms(x_ref[...], g_ref[...]).astype(bf16)
            acc[...] = jnp.zeros_like(acc)

        @pl.when((i == 0) & (j == 0))
        def _():
            dg_ref[...] = jnp.zeros_like(dg_ref)

        h = h_ref[...]
        a = _dot(h, wg_ref[...], NN)
        b = _dot(h, wu_ref[...], NN)
        sg = _sigmoid(a)
        s = a * sg
        dhm = _bdot(dx_ref[...], wd_ref[...], NT)
        da = (dhm * b * (sg * (1.0 + a * (1.0 - sg)))).astype(bf16)
        db = (dhm * s).astype(bf16)
        da_ref[...] = da
        db_ref[...] = db
        hm_ref[...] = (s * b).astype(bf16)
        acc[...] += _dot(da, wg_ref[...], NT) + _dot(db, wu_ref[...], NT)

        @pl.when(j == nf - 1)
        def _():
            _, vjp = jax.vjp(_rms, x_ref[...], g_ref[...])
            dx, dg = vjp(acc[...])
            dx1_ref[...] = dx_ref[...] + dx
            dg_ref[...] += dg

    rowf = pl.BlockSpec((tm, FF_T), lambda i, j: (i, j))
    rowd = pl.BlockSpec((tm, D), lambda i, j: (i, 0))
    vec = pl.BlockSpec((1, D), lambda i, j: (0, 0))
    return _pc(
        body, name=name, grid=(n // tm, nf),
        in_specs=[rowd, rowd, vec, pl.BlockSpec((D, FF_T), lambda i, j: (0, j)),
                  pl.BlockSpec((D, FF_T), lambda i, j: (0, j)), pl.BlockSpec((FF_T, D), lambda i, j: (j, 0))],
        out_specs=[rowd, vec, rowd, rowf, rowf, rowf],
        out_shape=[_sds((n, D)), _sds((1, D)), _sds((n, D), bf16), _sds((n, D_FF), bf16), _sds((n, D_FF), bf16),
                   _sds((n, D_FF), bf16)],
        scratch=[pltpu.VMEM((tm, D), f32)], sem=("arbitrary", "arbitrary"),
    )(dx2, x1, g, wg, wu, wd)


def _loss_head(x2, g, tgt, name):
    n = x2.shape[0]
    tm = min(512, n)

    def f(x, gg, t):
        e = _rms(x, gg) - t
        return 0.5 * jnp.sum(jnp.mean(e * e, axis=-1, keepdims=True))

    def body(x_ref, g_ref, t_ref, l_ref, dx_ref, dg_ref):
        val, vjp = jax.vjp(f, x_ref[...], g_ref[...], t_ref[...])
        dx, dg, _ = vjp(jnp.ones((), f32))
        dx_ref[...] = dx

        @pl.when(pl.program_id(0) == 0)
        def _():
            l_ref[...] = jnp.zeros_like(l_ref)
            dg_ref[...] = jnp.zeros_like(dg_ref)

        l_ref[...] += val
        dg_ref[...] += dg

    rowd = pl.BlockSpec((tm, D), lambda i: (i, 0))
    return _pc(
        body, name=name, grid=(n // tm,),
        in_specs=[rowd, pl.BlockSpec((1, D), lambda i: (0, 0)), rowd],
        out_specs=[pl.BlockSpec((8, 128), lambda i: (0, 0)), rowd, pl.BlockSpec((1, D), lambda i: (0, 0))],
        out_shape=[_sds((8, 128)), _sds((n, D)), _sds((1, D))], sem=("arbitrary",),
    )(x2, g, tgt)


def _adamw(w, parts, m, v, name):
    r, c = w.shape
    tr = r
    for cand in (1024, 512, 256, 128, 64, 32, 16, 8):
        if r % cand == 0 and cand * c * 4 <= 1024 * 1024:
            tr = cand
            break

    def body(w_ref, p_ref, m_ref, v_ref, g_ref, d_ref, nm_ref, nv_ref):
        gg = p_ref[0].astype(f32)
        for k in range(1, N_DEV):
            gg = gg + p_ref[k].astype(f32)
        g_ref[...] = gg
        nm = B1 * m_ref[...] + (1.0 - B1) * gg
        nv = B2 * v_ref[...] + (1.0 - B2) * (gg * gg)
        m_hat = nm / (1.0 - B1 ** STEP)
        v_hat = nv / (1.0 - B2 ** STEP)
        d_ref[...] = -LR * (m_hat / (jnp.sqrt(v_hat) + EPS) + WD * w_ref[...])
        nm_ref[...] = nm
        nv_ref[...] = nv

    spec = pl.BlockSpec((tr, c), lambda i: (i, 0))
    pspec = pl.BlockSpec((N_DEV, tr, c), lambda i: (0, i, 0))
    return _pc(body, name=name, grid=(r // tr,), in_specs=[spec, pspec, spec, spec], out_specs=[spec] * 4,
               out_shape=[_sds((r, c))] * 4, sem=("parallel",))(w, parts, m, v)


def _peers():
    x, y, c = lax.axis_index("x"), lax.axis_index("y"), lax.axis_index("c")
    me = 4 * x + 2 * y + c
    peers = []
    for k in range(1, N_DEV):
        fx, fy, fc = (k >> 2) & 1, (k >> 1) & 1, k & 1
        peers.append(((1 - x) if fx else x, (1 - y) if fy else y, (1 - c) if fc else c))
    return me, peers


def _all_gather(parts, name):
    n = len(parts)

    def body(*refs):
        ins, outs = refs[:n], refs[n:2 * n]
        send_sems, recv_sems, local_sems = refs[2 * n:]
        me, peers = _peers()
        copies = []
        for a in range(n):
            loc = pltpu.make_async_copy(ins[a], outs[a].at[me], local_sems.at[a])
            loc.start()
            copies.append(loc)
            for k, peer in enumerate(peers):
                cp = pltpu.make_async_remote_copy(
                    src_ref=ins[a], dst_ref=outs[a].at[me], send_sem=send_sems.at[a * (N_DEV - 1) + k], recv_sem=recv_sems.at[a * (N_DEV - 1) + k],
                    device_id=peer, device_id_type=pl.DeviceIdType.MESH)
                cp.start()
                copies.append(cp)
        for cp in copies:
            cp.wait()

    anyspec = pl.BlockSpec(memory_space=pl.ANY)
    return _pc(
        body, name=name, in_specs=[anyspec] * n, out_specs=[anyspec] * n,
        out_shape=[_sds((N_DEV,) + p.shape, p.dtype) for p in parts],
        scratch=[pltpu.SemaphoreType.DMA((n * (N_DEV - 1),)), pltpu.SemaphoreType.DMA((n * (N_DEV - 1),)),
                 pltpu.SemaphoreType.DMA((n,))],
    )(*parts)


def _exchange_grads(shards, rep, name):
    n = len(shards)

    def body(*refs):
        ins, rep_ref = refs[:n], refs[n]
        outs, orep_ref = refs[n + 1:2 * n + 1], refs[2 * n + 1]
        send_sems, recv_sems, local_sems = refs[2 * n + 2:]
        me, peers = _peers()
        copies = [pltpu.make_async_copy(ins[a].at[me], outs[a].at[me], local_sems.at[a]) for a in range(n)]
        copies.append(pltpu.make_async_copy(rep_ref, orep_ref.at[me], local_sems.at[n]))
        for cp in copies:
            cp.start()
        for k, peer in enumerate(peers):
            dev = 4 * peer[0] + 2 * peer[1] + peer[2]
            for a in range(n + 1):
                s = a * (N_DEV - 1) + k
                cp = pltpu.make_async_remote_copy(
                    src_ref=ins[a].at[dev] if a < n else rep_ref, dst_ref=(outs[a] if a < n else orep_ref).at[me],
                    send_sem=send_sems.at[s], recv_sem=recv_sems.at[s], device_id=peer,
                    device_id_type=pl.DeviceIdType.MESH)
                cp.start()
                copies.append(cp)
        for cp in copies:
            cp.wait()

    anyspec = pl.BlockSpec(memory_space=pl.ANY)
    nsem = (n + 1) * (N_DEV - 1)
    return _pc(
        body, name=name, in_specs=[anyspec] * (n + 1), out_specs=[anyspec] * (n + 1),
        out_shape=[_sds(a.shape, a.dtype) for a in shards] + [_sds((N_DEV,) + rep.shape, rep.dtype)],
        scratch=[pltpu.SemaphoreType.DMA((nsem,)), pltpu.SemaphoreType.DMA((nsem,)), pltpu.SemaphoreType.DMA((n + 1,))],
    )(*shards, rep)


SHARDED = {"w_in": 2, "gate_b": 2, "w_uq": 2, "w_ukv": 2, "rw_w0": 2, "rw_w2": 3, "rw_a0": 2, "rw_a2": 3, "rw_g2": 2,
           "w_branch": 3, "w_out": 1, "w_ffn_gate": 2, "w_ffn_up": 2, "w_ffn_down": 1}
GATHER_F32 = ("gate_b", "rw_w0", "rw_a0")
REPLICATED = ("attn_norm_g", "q_norm_g", "kv_norm_g", "sg_ln_g", "sg_ln_b", "sg_w", "sg_b", "rw_mu", "rw_k_k", "rw_k_a",
              "rw_r_k", "rw_ln_g", "rw_ln_b", "ffn_norm_g", "final_norm_g")
WEIGHTS = ("attn_norm_g", "w_in", "gate_b", "q_norm_g", "w_uq", "kv_norm_g", "w_ukv", "sg_ln_g", "sg_ln_b", "sg_w", "sg_b",
           "rw_mu", "rw_w0", "rw_w2", "rw_a0", "rw_a2", "rw_g2", "rw_k_k", "rw_k_a", "rw_r_k", "rw_ln_g", "rw_ln_b",
           "w_branch", "w_out", "ffn_norm_g", "w_ffn_gate", "w_ffn_up", "w_ffn_down", "final_norm_g")


BIG = ("w_in", "w_branch", "w_out", "w_ffn_gate", "w_ffn_up", "w_ffn_down")
SMALL_BF = ("w_uq", "w_ukv", "rw_w2", "rw_a2", "rw_g2")
SMALL = SMALL_BF + GATHER_F32


def _pack128(blocks, names, dtype, lead=0):
    parts = [blocks[k].astype(dtype).reshape(blocks[k].shape[:lead] + (-1, 128)) for k in names]
    rows = sum(p.shape[lead] for p in parts)
    pad = -rows % 256
    if pad:
        parts.append(jnp.zeros(parts[0].shape[:lead] + (pad, 128), dtype))
    return jnp.concatenate(parts, axis=lead)


def _unpack128(packed, shapes, names, lead=0):
    out, off = {}, 0
    for k in names:
        rows = 1
        for d in shapes[k]:
            rows *= d
        rows //= 128
        idx = (slice(None),) * lead + (slice(off, off + rows),)
        out[k] = packed[idx].reshape(packed.shape[:lead] + tuple(shapes[k]))
        off += rows
    return out


def _join_blocks(g, ax):
    shp = g.shape[1:]
    return jnp.moveaxis(g, 0, ax).reshape(shp[:ax] + (N_DEV * shp[ax],) + shp[ax + 1:])


def _split_blocks(full, ax):
    shp = full.shape
    return jnp.moveaxis(full.reshape(shp[:ax] + (N_DEV, shp[ax] // N_DEV) + shp[ax + 1:]), ax, 0)


def _w_in_padded(w):
    z = lambda n: jnp.zeros((w.shape[0], n), w.dtype)
    q, ckv, kr = w[:, 0:384], w[:, 384:640], w[:, 640:672]
    sg, rw, gate = w[:, 672:1696], w[:, 1696:3616], w[:, 3616:6688]
    return jnp.concatenate([gate, sg, rw, z(128), ckv, z(64), kr, z(32), q, z(P_W - O_MLA - MLA_W)], axis=1)


def _w_in_unpadded(g):
    return jnp.concatenate([g[:, O_Q:O_Q + 384], g[:, O_CKV:O_CKV + 256], g[:, O_SLAB + 64:O_SLAB + 96],
                            g[:, O_SG:O_SG + 1024], g[:, O_RW:O_RW + 1920], g[:, O_GATE:O_GATE + 3072]], axis=1)


def _layer_weights(full, rep, l):
    w = {}
    w["w_in"] = _w_in_padded(full["w_in"][l])
    uq = full["w_uq"][l].reshape(Q_LORA, HEADS, QK_NOPE + QK_ROPE)
    w["wq"] = jnp.pad(uq, ((0, 0), (0, 0), (0, 32))).reshape(Q_LORA, HEADS * 128).astype(f32)
    ukv = full["w_ukv"][l].reshape(KV_LORA, HEADS, QK_NOPE + V_HEAD)
    wk = jnp.pad(ukv[:, :, :QK_NOPE], ((0, 0), (0, 0), (0, 64))).reshape(KV_LORA, HEADS * 128)
    w["wk"], w["wv"] = wk.astype(f32), ukv[:, :, QK_NOPE:].reshape(KV_LORA, HEADS * V_HEAD).astype(f32)
    bdiag = lambda t: jnp.concatenate([jnp.concatenate([t[0], jnp.zeros_like(t[0])], axis=1),
                                       jnp.concatenate([jnp.zeros_like(t[1]), t[1]], axis=1)], axis=0).astype(f32)
    w["w2"], w["a2"] = bdiag(full["rw_w2"][l]), bdiag(full["rw_a2"][l])
    w["g2"] = full["rw_g2"][l].astype(f32)
    w["w0"], w["a0"] = full["rw_w0"][l].reshape(1, 2 * RW_DIM), full["rw_a0"][l].reshape(1, 2 * RW_DIM)
    w["gate_b"] = full["gate_b"][l].reshape(1, 3 * D)
    w["wb"], w["wo"] = full["w_branch"][l], full["w_out"][l]
    w["wg"], w["wu"], w["wd"] = full["w_ffn_gate"][l], full["w_ffn_up"][l], full["w_ffn_down"][l]
    row = lambda a: a.reshape(1, -1)
    for k in ("attn_norm_g", "q_norm_g", "kv_norm_g", "sg_ln_g", "sg_ln_b", "rw_k_k", "rw_k_a", "rw_ln_g", "rw_ln_b",
              "ffn_norm_g"):
        w[k] = row(rep[k][l])
    w["r_k"] = row(rep["rw_r_k"][l])
    w["mu"] = jnp.pad(row(rep["rw_mu"][l]), ((0, 0), (0, RW_W - 1920)))
    w["sg_w"] = [rep["sg_w"][l, k] for k in range(SG_GROUPS)]
    w["sg_bias"] = jnp.repeat(rep["sg_b"][l].T, SG_DIM // SG_GROUPS, axis=1)
    return w


def _layer_fwd(x2, w, tabs, bsz, seq, l):
    nm = lambda s: f"l{l}_{s}"
    n = x2.shape[0]
    tm = min(256, n)
    p, h = _inproj_fwd(x2, w["attn_norm_g"], w["w_in"], nm("inproj"))
    mla_rows = [(p, 256, O_CKV // 256), (p, 128, O_SLAB // 128), (p, 384, O_Q // 384), (tabs[0], 128, 0), (tabs[1], 128, 0)]
    mla_w = [w["q_norm_g"], w["kv_norm_g"], w["wq"], w["wk"], w["wv"]]
    q, k, v = _rowwise_fwd(nm("mla_proj"), _f_mla_proj, mla_rows, mla_w, [(1024, bf16), (1024, bf16), (512, bf16)], tm)
    ya = _attn_fwd(q, k, v, bsz, seq, nm("attn"))
    sg_rows = [(p, SG_DIM, O_SG // SG_DIM), (p, SG_DIM, O_SG // SG_DIM + 1)]
    sg_w = [w["sg_ln_g"], w["sg_ln_b"], w["sg_bias"]] + w["sg_w"]
    (yb,) = _rowwise_fwd(nm("sg"), _f_sg, sg_rows, sg_w, [(SG_DIM, f32)], SG_CHUNK)
    z = _shift_fwd(p, w["mu"], seq, nm("shift"))
    pre_rows = [(z, 512, 1), (z, 128, 12), (z, 128, 13), (z, 128, 14)]
    pre_w = [w["w0"], w["a0"], w["w2"], w["a2"], w["g2"], w["rw_k_k"], w["rw_k_a"]]
    lw, kd, kk, bd, g = _rowwise_fwd(nm("rw_pre"), _f_rw_pre, pre_rows, pre_w,
                                     [(1024, f32), (1024, f32), (512, f32), (1024, f32), (512, f32)], tm)
    y, s_in = _scan_fwd(z, lw, kd, kk, bd, bsz, seq, nm("scan"))
    post_rows = [(y[0], 512, 0), (y[1], 512, 0), (z, 512, 0), (z, 512, 2), (kd, 512, 0), (kd, 512, 1), (g, 512, 0)]
    post_w = [w["r_k"], w["rw_ln_g"], w["rw_ln_b"]]
    (yc,) = _rowwise_fwd(nm("rw_post"), _f_rw_post, post_rows, post_w, [(512, f32)], tm)
    x1 = _merge_fwd(x2, p, ya, yb, yc, w["gate_b"], w["wb"], w["wo"], nm("merge"))
    x3 = _ffn_fwd(x1, w["ffn_norm_g"], w["wg"], w["wu"], w["wd"], nm("ffn"))
    saved = dict(x=x2, p=p, h=h, q=q, k=k, v=v, ya=ya, yb=yb, z=z, lw=lw, kd=kd, kk=kk, bd=bd, g=g, y=y, s_in=s_in, yc=yc,
                 x1=x1, mla_rows=mla_rows, mla_w=mla_w, sg_rows=sg_rows, sg_w=sg_w, pre_rows=pre_rows, pre_w=pre_w,
                 post_rows=post_rows, post_w=post_w)
    return x3, saved


def _layer_bwd(dx3, w, sv, bsz, seq, l):
    nm = lambda s: f"l{l}_{s}_bwd"
    n = dx3.shape[0]
    tm = min(256, n)
    g = {}
    dx1, g["ffn_norm_g"], h2, da, db, hm = _ffn_bwd(dx3, sv["x1"], w["ffn_norm_g"], w["wg"], w["wu"], w["wd"], nm("ffn"))
    g["wg"] = _matmul_tn(h2, da, nm("wg"))
    g["wu"] = _matmul_tn(h2, db, nm("wu"))
    g["wd"] = _matmul_tn(hm, dx3.astype(bf16), nm("wd"))
    dpg, dya, dyb, dyc, dt, mg, g["gate_b"] = _merge_bwd(dx1, sv["p"], sv["ya"], sv["yb"], sv["yc"], w["gate_b"], w["wb"],
                                                         w["wo"], nm("merge"))
    g["wo"] = _matmul_tn(mg, dx1.astype(bf16), nm("wo"))
    ys = (sv["ya"], sv["yb"], sv["yc"])
    g["wb"] = jnp.stack([_matmul_tn(ys[k].astype(bf16), dt[:, k * D:(k + 1) * D], nm(f"wb{k}")) for k in range(3)])
    (dy, dr_p, dv_p, dkd0, dkd1, dg_), (g["r_k"], g["rw_ln_g"], g["rw_ln_b"]) = _rowwise_bwd(
        nm("rw_post"), _f_rw_post, sv["post_rows"], sv["post_w"], [(dyc, 512, 0)], tm, [f32, None] + [f32] * 5)
    dkd_p = jnp.concatenate([dkd0, dkd1], axis=1)
    dr_s, dv_s, dkk_s, dlw, dkd_s, dbd = _scan_bwd(sv["z"], sv["lw"], sv["kd"], sv["kk"], sv["bd"], sv["s_in"], dy,
                                                   bsz, seq, nm("scan"))
    pre_cots = [(dlw, 1024, 0), (dkd_s + dkd_p, 1024, 0), (dkk_s[0] + dkk_s[1], 512, 0), (dbd, 1024, 0), (dg_, 512, 0)]
    (dk, dwl, dal, dgl), (g["w0"], g["a0"], g["w2"], g["a2"], g["g2"], g["rw_k_k"], g["rw_k_a"]) = _rowwise_bwd(
        nm("rw_pre"), _f_rw_pre, sv["pre_rows"], sv["pre_w"], pre_cots, tm, [f32] * 4)
    dz = jnp.concatenate([dr_s[0] + dr_s[1] + dr_p, dk, dv_s[0] + dv_s[1] + dv_p, dwl, dal, dgl,
                          jnp.zeros((n, RW_W - 1920), f32)], axis=1)
    dp_rw, g["mu"] = _shift_bwd(dz, sv["p"], w["mu"], seq, nm("shift"))
    (dp_su, dp_sv), (g["sg_ln_g"], g["sg_ln_b"], g["sg_bias"], *sgw) = _rowwise_bwd(
        nm("sg"), _f_sg, sv["sg_rows"], sv["sg_w"], [(dyb, SG_DIM, 0)], SG_CHUNK, [bf16, bf16])
    g["sg_w"] = jnp.stack(sgw)
    dq, dk_, dv_ = _attn_bwd(sv["q"], sv["k"], sv["v"], sv["ya"], dya, bsz, seq, nm("attn"))
    (dp_ckv, dp_slab, dp_q), (g["q_norm_g"], g["kv_norm_g"], g["wq"], g["wk"], g["wv"]) = _rowwise_bwd(
        nm("mla_proj"), _f_mla_proj, sv["mla_rows"], sv["mla_w"], [(dq, 1024, 0), (dk_, 1024, 0), (dv_, 512, 0)], tm,
        [bf16, bf16, bf16, None, None])
    dp = jnp.concatenate([dpg, dp_su, dp_sv, dp_rw, dp_ckv, dp_slab, dp_q, jnp.zeros((n, P_W - O_MLA - MLA_W), bf16)],
                         axis=1)
    g["w_in"] = _matmul_tn(sv["h"], dp, nm("w_in"))
    dx, g["attn_norm_g"] = _norm_matmul_bwd(dp, w["w_in"], sv["x"], w["attn_norm_g"], dx1, nm("inproj"))
    return dx, g


def _layer_grads_to_full(g):
    o = {}
    o["w_in"] = _w_in_unpadded(g["w_in"])
    o["w_uq"] = g["wq"].reshape(Q_LORA, HEADS, 128)[:, :, :QK_NOPE + QK_ROPE].reshape(Q_LORA, -1)
    gk = g["wk"].reshape(KV_LORA, HEADS, 128)[:, :, :QK_NOPE]
    gv = g["wv"].reshape(KV_LORA, HEADS, V_HEAD)
    o["w_ukv"] = jnp.concatenate([gk, gv], axis=2).reshape(KV_LORA, -1)
    unb = lambda t: jnp.stack([t[:LORA, :RW_DIM], t[LORA:, RW_DIM:]])
    o["rw_w2"], o["rw_a2"], o["rw_g2"] = unb(g["w2"]), unb(g["a2"]), g["g2"]
    o["rw_w0"], o["rw_a0"] = g["w0"].reshape(2, RW_DIM), g["a0"].reshape(2, RW_DIM)
    o["gate_b"] = g["gate_b"].reshape(3, D)
    o["w_branch"], o["w_out"] = g["wb"], g["wo"]
    o["w_ffn_gate"], o["w_ffn_up"], o["w_ffn_down"] = g["wg"], g["wu"], g["wd"]
    for k in ("attn_norm_g", "q_norm_g", "kv_norm_g", "sg_ln_g", "sg_ln_b", "rw_k_k", "rw_k_a", "rw_ln_g", "rw_ln_b",
              "ffn_norm_g"):
        o[k] = g[k].reshape(-1)
    o["rw_r_k"] = g["r_k"].reshape(HEADS, RW_HEAD)
    o["rw_mu"] = g["mu"].reshape(-1)[:1920]
    o["sg_w"] = g["sg_w"]
    o["sg_b"] = g["sg_bias"].reshape(SG_CHUNK, SG_GROUPS, SG_DIM // SG_GROUPS).sum(axis=2).T
    return o


def _rope_tables(positions):
    inv = 1.0 / (10000.0 ** (jnp.arange(0, QK_ROPE, 2, dtype=f32) / QK_ROPE))
    ang = positions.astype(f32)[:, None] * inv
    cos, sin = jnp.cos(ang), jnp.sin(ang)
    n = positions.shape[0]
    c = jnp.concatenate([jnp.ones((n, 64), f32), cos, cos, jnp.zeros((n, 32), f32)], axis=1)
    s = jnp.concatenate([jnp.zeros((n, 64), f32), -sin, sin, jnp.zeros((n, 32), f32)], axis=1)
    return c, s


def _local_step(x, positions, full, rep, loss_target):
    bsz, seq, _ = x.shape
    n = bsz * seq
    depth = rep["attn_norm_g"].shape[0]
    x2 = x.reshape(n, D)
    tabs = _rope_tables(positions.reshape(n))
    ws, saved = [], []
    for l in range(depth):
        w = _layer_weights(full, rep, l)
        x2, sv = _layer_fwd(x2, w, tabs, bsz, seq, l)
        ws.append(w)
        saved.append(sv)
    loss, dx, dgf = _loss_head(x2, rep["final_norm_g"].reshape(1, D), loss_target.reshape(n, D), "loss_head")
    per_layer = [None] * depth
    for l in reversed(range(depth)):
        dx, g = _layer_bwd(dx, ws[l], saved[l], bsz, seq, l)
        per_layer[l] = _layer_grads_to_full(g)
    grads = {k: jnp.stack([per_layer[l][k] for l in range(depth)]) for k in per_layer[0]}
    grads["final_norm_g"] = dgf.reshape(D)
    return loss[0, 0], dx.reshape(bsz, seq, D), grads


def kernel(x, positions, attn_norm_g, w_in, gate_b, q_norm_g, w_uq, kv_norm_g, w_ukv, sg_ln_g, sg_ln_b, sg_w, sg_b, rw_mu, rw_w0, rw_w2, rw_a0, rw_a2, rw_g2, rw_k_k, rw_k_a, rw_r_k, rw_ln_g, rw_ln_b, w_branch, w_out, ffn_norm_g, w_ffn_gate, w_ffn_up, w_ffn_down, final_norm_g, loss_target, m_attn_norm_g, m_w_in, m_gate_b, m_q_norm_g, m_w_uq, m_kv_norm_g, m_w_ukv, m_sg_ln_g, m_sg_ln_b, m_sg_w, m_sg_b, m_rw_mu, m_rw_w0, m_rw_w2, m_rw_a0, m_rw_a2, m_rw_g2, m_rw_k_k, m_rw_k_a, m_rw_r_k, m_rw_ln_g, m_rw_ln_b, m_w_branch, m_w_out, m_ffn_norm_g, m_w_ffn_gate, m_w_ffn_up, m_w_ffn_down, m_final_norm_g, v_attn_norm_g, v_w_in, v_gate_b, v_q_norm_g, v_w_uq, v_kv_norm_g, v_w_ukv, v_sg_ln_g, v_sg_ln_b, v_sg_w, v_sg_b, v_rw_mu, v_rw_w0, v_rw_w2, v_rw_a0, v_rw_a2, v_rw_g2, v_rw_k_k, v_rw_k_a, v_rw_r_k, v_rw_ln_g, v_rw_ln_b, v_w_branch, v_w_out, v_ffn_norm_g, v_w_ffn_gate, v_w_ffn_up, v_w_ffn_down, v_final_norm_g):
    args = locals()
    wts = {k: args[k] for k in WEIGHTS}
    mom_m = {k: args["m_" + k] for k in WEIGHTS}
    mom_v = {k: args["v_" + k] for k in WEIGHTS}
    shapes = {k: wts[k].shape for k in WEIGHTS}
    got = _all_gather([wts[k].astype(bf16) for k in BIG] + [_pack128(wts, SMALL_BF, bf16), _pack128(wts, GATHER_F32, f32)],
                      "gather_weights")
    blocks = dict(zip(BIG, got[:len(BIG)]))
    blocks.update(_unpack128(got[-2], shapes, SMALL_BF, lead=1))
    blocks.update(_unpack128(got[-1], shapes, GATHER_F32, lead=1))
    full = {k: _join_blocks(blocks[k], SHARDED[k]) for k in SHARDED}
    rep = {k: wts[k] for k in REPLICATED}
    loss, grad_x, grads = _local_step(x, positions, full, rep, loss_target)
    loss = lax.psum(loss, ("x", "y", "c"))
    split = {k: _split_blocks(grads[k], SHARDED[k]) for k in SHARDED}
    got = _exchange_grads([split[k].astype(bf16) for k in BIG] + [_pack128(split, SMALL, f32, lead=1)],
                          _pack128(grads, REPLICATED, f32), "exchange_grads")
    gw, delta, new_m, new_v = {}, {}, {}, {}
    for k, parts in zip(BIG, got):
        two = lambda a, k=k: a.reshape(a.shape[:a.ndim - len(shapes[k])] + (-1, shapes[k][-1]))
        res = _adamw(two(wts[k]), two(parts), two(mom_m[k]), two(mom_v[k]), f"adamw_{k}")
        gw[k], delta[k], new_m[k], new_v[k] = (t.reshape(shapes[k]) for t in res)
    for names, parts, tag in ((SMALL, got[-2], "small"), (REPLICATED, got[-1], "replicated")):
        pk = lambda dct: _pack128(dct, names, f32)
        res = _adamw(pk(wts), parts, pk(mom_m), pk(mom_v), f"adamw_{tag}")
        for dst, t in zip((gw, delta, new_m, new_v), res):
            dst.update(_unpack128(t, shapes, names))
    return (loss, grad_x, *[gw[k] for k in WEIGHTS], *[delta[k] for k in WEIGHTS], *[new_m[k] for k in WEIGHTS],
            *[new_v[k] for k in WEIGHTS])
```

```python
import functools

import jax
import jax.numpy as jnp
from jax import lax
from jax.experimental import pallas as pl
from jax.experimental.pallas import tpu as pltpu

f32 = jnp.float32
bf16 = jnp.bfloat16
HI = lax.Precision.HIGHEST
NN, NT, TN = ((1,), (0,)), ((1,), (1,)), ((0,), (0,))

N_DEV = 8
D = 1024
HEADS = 8
Q_LORA, KV_LORA, QK_NOPE, QK_ROPE, V_HEAD = 384, 256, 64, 32, 64
SG_DIM, SG_CHUNK, SG_GROUPS = 512, 128, 8
RW_DIM, RW_HEAD, LORA = 512, 64, 64
D_FF = 2816
N_IN = 6688
NORM_EPS, LN_EPS, GN_EPS = 1e-6, 1e-5, 64e-5
ATT_SCALE = (QK_NOPE + QK_ROPE) ** -0.5
P_W = 7168
O_GATE, O_SG, O_RW, O_MLA = 0, 3072, 4096, 6144
RW_W = 2048
MLA_W = 768
O_CKV, O_SLAB, O_Q = O_MLA, O_MLA + 256, O_MLA + 384
CHUNK = 128
VMEM_LIMIT = 56 * 1024 * 1024

B1, B2, LR, EPS, WD, STEP = 0.9, 0.999, 0.001, 1e-8, 0.01, 10


def _pc(body, *, name, out_shape, grid=(), in_specs=(), out_specs=(), scratch=(), sem=None, ride=None):
    params = pltpu.CompilerParams(dimension_semantics=sem, vmem_limit_bytes=VMEM_LIMIT)
    if ride is None:
        return pl.pallas_call(body, out_shape=out_shape, grid=grid, in_specs=in_specs, out_specs=out_specs,
                              scratch_shapes=scratch, compiler_params=params, name=name, interpret=False)
    gathers, scatters = ride
    moved = list(gathers) + list(scatters)
    ng, nx = len(gathers), len(moved)
    single = not isinstance(out_shape, (list, tuple))
    outs = [out_shape] if single else list(out_shape)
    ospecs = [out_specs] if single else list(out_specs)
    n_in, n_out, n_scr = len(in_specs), len(outs), len(scratch)
    per = N_DEV - 1

    def riding(*refs):
        ins, xin = refs[:n_in], refs[n_in:n_in + nx]
        outs_r, xout = refs[n_in + nx:n_in + nx + n_out], refs[n_in + nx + n_out:n_in + 2 * nx + n_out]
        own = refs[n_in + 2 * nx + n_out:n_in + 2 * nx + n_out + n_scr]
        send_sems, recv_sems, local_sems = refs[n_in + 2 * nx + n_out + n_scr:]

        def copies():
            me, peers = _peers()
            cps = []
            for a in range(nx):
                whole = a < ng
                cps.append(pltpu.make_async_copy(xin[a] if whole else xin[a].at[me], xout[a].at[me], local_sems.at[a]))
                for k, peer in enumerate(peers):
                    dev = 4 * peer[0] + 2 * peer[1] + peer[2]
                    cps.append(pltpu.make_async_remote_copy(
                        src_ref=xin[a] if whole else xin[a].at[dev], dst_ref=xout[a].at[me],
                        send_sem=send_sems.at[a * per + k], recv_sem=recv_sems.at[a * per + k], device_id=peer,
                        device_id_type=pl.DeviceIdType.MESH))
            return cps

        if not grid:
            for cp in copies():
                cp.start()
            body(*ins, *outs_r, *own)
            for cp in copies():
                cp.wait()
            return
        ids = [pl.program_id(a) for a in range(len(grid))]
        first = functools.reduce(jnp.logical_and, [i == 0 for i in ids])
        last = functools.reduce(jnp.logical_and, [i == g - 1 for i, g in zip(ids, grid)])

        @pl.when(first)
        def _():
            for cp in copies():
                cp.start()

        body(*ins, *outs_r, *own)

        @pl.when(last)
        def _():
            for cp in copies():
                cp.wait()

    anyspec = pl.BlockSpec(memory_space=pl.ANY)
    call = pl.pallas_call(
        riding, grid=grid, in_specs=list(in_specs) + [anyspec] * nx, out_specs=ospecs + [anyspec] * nx,
        out_shape=outs + [_sds((N_DEV,) + a.shape, a.dtype) for a in gathers] + [_sds(a.shape, a.dtype) for a in scatters],
        scratch_shapes=list(scratch) + [pltpu.SemaphoreType.DMA((nx * per,)), pltpu.SemaphoreType.DMA((nx * per,)),
                                        pltpu.SemaphoreType.DMA((nx,))],
        compiler_params=params, name=name, interpret=False)

    def run(*args):
        res = call(*args, *moved)
        own = res[0] if single else list(res[:n_out])
        return own, list(res[n_out:])

    return run


def _sds(shape, dtype=f32):
    return jax.ShapeDtypeStruct(tuple(shape), dtype)


def _dot(a, b, dims, precision=None):
    return lax.dot_general(a, b, (dims, ((), ())), preferred_element_type=f32, precision=precision)


def _bdot(a, b, dims=NN):
    return _dot(a.astype(bf16), b.astype(bf16), dims)


@jax.custom_vjp
def _mm(a, w):
    return _bdot(a, w, NN)


def _mm_fwd(a, w):
    return _bdot(a, w, NN), (a, w)


def _mm_bwd(res, g):
    a, w = res
    return _bdot(g, w, NT), _bdot(a, g, TN)


_mm.defvjp(_mm_fwd, _mm_bwd)


@jax.custom_vjp
def _mm_nt(a, b):
    return _bdot(a, b, NT)


def _mm_nt_fwd(a, b):
    return _bdot(a, b, NT), (a, b)


def _mm_nt_bwd(res, g):
    a, b = res
    return _bdot(g, b, NN), _bdot(g, a, TN)


_mm_nt.defvjp(_mm_nt_fwd, _mm_nt_bwd)


@jax.custom_vjp
def _mm_tn(a, b):
    return _bdot(a, b, TN)


def _mm_tn_fwd(a, b):
    return _bdot(a, b, TN), (a, b)


def _mm_tn_bwd(res, g):
    a, b = res
    return _bdot(b, g, NT), _bdot(a, g, NN)


_mm_tn.defvjp(_mm_tn_fwd, _mm_tn_bwd)


def _rms(x, g):
    return x * lax.rsqrt(jnp.mean(x * x, axis=-1, keepdims=True) + NORM_EPS) * g


def _sigmoid(x):
    return 1.0 / (1.0 + jnp.exp(-x))


def _gelu(x):
    return 0.5 * x * (1.0 + jnp.tanh(0.7978845608028654 * (x + 0.044715 * x * x * x)))


def _softplus(x):
    return jnp.maximum(x, 0.0) + jnp.log(1.0 + jnp.exp(-jnp.abs(x)))


def _group_sum(x):
    w = x.shape[-1]
    r = lax.broadcasted_iota(jnp.int32, (w, w), 0) // RW_HEAD
    c = lax.broadcasted_iota(jnp.int32, (w, w), 1) // RW_HEAD
    return _dot(x, (r == c).astype(f32), NN, HI)


@jax.custom_vjp
def _swap(x):
    w = x.shape[-1]
    lane = lax.broadcasted_iota(jnp.int32, x.shape, 1) % 128
    lo = (lane >= 64) & (lane < 80)
    hi = (lane >= 80) & (lane < 96)
    return jnp.where(lo, pltpu.roll(x, w - 16, 1), jnp.where(hi, pltpu.roll(x, 16, 1), 0.0))


_swap.defvjp(lambda x: (_swap(x), None), lambda _, g: (_swap(g),))


def _rope(x, c, s):
    return x * c + _swap(x) * s


def _row_spec(tm, width, blk):
    return pl.BlockSpec((tm, width), lambda i, blk=blk: (i, blk))


def _full_spec(a):
    nd = a.ndim
    return pl.BlockSpec(a.shape, lambda i, nd=nd: (0,) * nd)


def _rowwise_fwd(name, f, rows, weights, outs, tm):
    n = rows[0][0].shape[0]
    nr, nw = len(rows), len(weights)

    def body(*refs):
        vals = [r[...].astype(f32) for r in refs[:nr + nw]]
        res = f(*vals)
        for o_ref, o in zip(refs[nr + nw:], res):
            o_ref[...] = o.astype(o_ref.dtype)

    return _pc(
        body, name=name, grid=(n // tm,),
        in_specs=[_row_spec(tm, w, b) for _, w, b in rows] + [_full_spec(w) for w in weights],
        out_specs=[_row_spec(tm, w, 0) for w, _ in outs],
        out_shape=[_sds((n, w), dt) for w, dt in outs], sem=("parallel",),
    )(*[a for a, _, _ in rows], *weights)


def _rowwise_bwd(name, f, rows, weights, cots, tm, drows):
    n = rows[0][0].shape[0]
    nr, nw, nc = len(rows), len(weights), len(cots)
    want = [k for k, dt in enumerate(drows) if dt is not None]

    def body(*refs):
        vals = [r[...].astype(f32) for r in refs[:nr + nw]]
        cot = tuple(r[...].astype(f32) for r in refs[nr + nw:nr + nw + nc])
        _, vjp = jax.vjp(f, *vals)
        grads = vjp(cot)
        outs = refs[nr + nw + nc:]
        for o_ref, k in zip(outs[:len(want)], want):
            o_ref[...] = grads[k].astype(o_ref.dtype)
        first = pl.program_id(0) == 0
        for o_ref, g in zip(outs[len(want):], grads[nr:]):
            @pl.when(first)
            def _(o_ref=o_ref, g=g):
                o_ref[...] = g

            @pl.when(jnp.logical_not(first))
            def _(o_ref=o_ref, g=g):
                o_ref[...] += g

    res = _pc(
        body, name=name, grid=(n // tm,),
        in_specs=[_row_spec(tm, w, b) for _, w, b in rows] + [_full_spec(w) for w in weights]
        + [_row_spec(tm, w, b) for _, w, b in cots],
        out_specs=[_row_spec(tm, rows[k][1], 0) for k in want] + [_full_spec(w) for w in weights],
        out_shape=[_sds((n, rows[k][1]), drows[k]) for k in want] + [_sds(w.shape) for w in weights],
        sem=("arbitrary",),
    )(*[a for a, _, _ in rows], *weights, *[a for a, _, _ in cots])
    return res[:len(want)], res[len(want):]


def _inproj_fwd(x2, g, w, name, ride=None):
    n = x2.shape[0]
    tm, tn = min(512, n), 512

    def body(x_ref, g_ref, w_ref, p_ref, h_ref):
        @pl.when(pl.program_id(1) == 0)
        def _():
            h_ref[...] = _rms(x_ref[...], g_ref[...]).astype(bf16)

        p_ref[...] = jnp.dot(h_ref[...], w_ref[...], preferred_element_type=f32)

    return _pc(
        body, name=name, grid=(n // tm, P_W // tn),
        in_specs=[pl.BlockSpec((tm, D), lambda i, j: (i, 0)), pl.BlockSpec((1, D), lambda i, j: (0, 0)),
                  pl.BlockSpec((D, tn), lambda i, j: (0, j))],
        out_specs=[pl.BlockSpec((tm, tn), lambda i, j: (i, j)), pl.BlockSpec((tm, D), lambda i, j: (i, 0))],
        out_shape=[_sds((n, P_W)), _sds((n, D), bf16)], sem=("parallel", "arbitrary"), ride=ride,
    )(x2, g, w)


def _norm_matmul_bwd(dy, w, x2, g, dres, name):
    n, k = dy.shape
    tm = min(512, n)
    tk = 1024 if k % 1024 == 0 else 1408
    nk = k // tk

    def body(dy_ref, w_ref, x_ref, g_ref, dr_ref, dx_ref, dg_ref, acc):
        i, j = pl.program_id(0), pl.program_id(1)

        @pl.when(j == 0)
        def _():
            acc[...] = jnp.zeros_like(acc)

        @pl.when((i == 0) & (j == 0))
        def _():
            dg_ref[...] = jnp.zeros_like(dg_ref)

        acc[...] += _dot(dy_ref[...], w_ref[...], NT)

        @pl.when(j == nk - 1)
        def _():
            _, vjp = jax.vjp(_rms, x_ref[...], g_ref[...])
            dx, dg = vjp(acc[...])
            dx_ref[...] = dr_ref[...] + dx
            dg_ref[...] += dg

    return _pc(
        body, name=name, grid=(n // tm, nk),
        in_specs=[pl.BlockSpec((tm, tk), lambda i, j: (i, j)), pl.BlockSpec((D, tk), lambda i, j: (0, j)),
                  pl.BlockSpec((tm, D), lambda i, j: (i, 0)), pl.BlockSpec((1, D), lambda i, j: (0, 0)),
                  pl.BlockSpec((tm, D), lambda i, j: (i, 0))],
        out_specs=[pl.BlockSpec((tm, D), lambda i, j: (i, 0)), pl.BlockSpec((1, D), lambda i, j: (0, 0))],
        out_shape=[_sds((n, D)), _sds((1, D))], scratch=[pltpu.VMEM((tm, D), f32)], sem=("arbitrary", "arbitrary"),
    )(dy, w, x2, g, dres)


def _matmul_tn(a, g, name):
    n, k = a.shape
    m = g.shape[1]
    tr = min(512, n)
    tk = k if k <= 1024 else 1408
    tn = m if m <= 1024 else (512 if m % 512 == 0 else 1408)
    nr = n // tr

    def body(a_ref, g_ref, o_ref):
        @pl.when(pl.program_id(2) == 0)
        def _():
            o_ref[...] = jnp.zeros_like(o_ref)

        o_ref[...] += _dot(a_ref[...], g_ref[...], TN)

    return _pc(
        body, name=name, grid=(k // tk, m // tn, nr),
        in_specs=[pl.BlockSpec((tr, tk), lambda i, j, r: (r, i)), pl.BlockSpec((tr, tn), lambda i, j, r: (r, j))],
        out_specs=pl.BlockSpec((tk, tn), lambda i, j, r: (i, j)),
        out_shape=_sds((k, m)), sem=("parallel", "parallel", "arbitrary"),
    )(a, g)


def _f_mla_proj(ckv, slab, pq, c, s, qg, kg, wq, wk, wv):
    c8, s8 = jnp.concatenate([c] * HEADS, axis=1), jnp.concatenate([s] * HEADS, axis=1)
    q = _rope(_mm(_rms(pq, qg), wq), c8, s8)
    cn = _rms(ckv, kg)
    k = _mm(cn, wk) + jnp.concatenate([_rope(slab, c, s)] * HEADS, axis=1)
    return q, k, _mm(cn, wv)


def _attn_fwd(q, k, v, bsz, seq, name, ride=None):
    n = q.shape[0]
    tq = min(256, seq)
    nq = seq // tq

    def body(q_ref, k_ref, v_ref, o_ref):
        lane = lax.broadcasted_iota(jnp.int32, (tq, 128), 1) < 64
        vv = v_ref[...]
        two = range(2)
        s = [_dot(q_ref[:, h * 128:(h + 1) * 128], k_ref[:, h * 128:(h + 1) * 128], NT) * ATT_SCALE for h in two]
        e = [jnp.exp(s[h] - jnp.max(s[h], axis=-1, keepdims=True)) for h in two]
        p = [(e[h] / jnp.sum(e[h], axis=-1, keepdims=True)).astype(bf16) for h in two]
        outs = [_dot(p[h], vv, NN) for h in two]
        o_ref[...] = jnp.where(lane, outs[0], outs[1])

    return _pc(
        body, name=name, grid=(bsz, HEADS // 2, nq),
        in_specs=[pl.BlockSpec((tq, 256), lambda b, h, i: (b * nq + i, h)),
                  pl.BlockSpec((seq, 256), lambda b, h, i: (b, h)),
                  pl.BlockSpec((seq, 128), lambda b, h, i: (b, h))],
        out_specs=pl.BlockSpec((tq, 128), lambda b, h, i: (b * nq + i, h)),
        out_shape=_sds((n, HEADS * V_HEAD)), sem=("parallel", "parallel", "parallel"), ride=ride,
    )(q, k, v)


def _attn_bwd(q, k, v, o, do, bsz, seq, name, ride=None):
    n = q.shape[0]
    tq = min(256, seq)
    nq = seq // tq

    def body(q_ref, k_ref, v_ref, o_ref, do_ref, dq_ref, dk_ref, dv_ref):
        @pl.when(pl.program_id(2) == 0)
        def _():
            dk_ref[...] = jnp.zeros_like(dk_ref)
            dv_ref[...] = jnp.zeros_like(dv_ref)

        lane = lax.broadcasted_iota(jnp.int32, (tq, 128), 1) < 64
        vv, do, o = v_ref[...], do_ref[...], o_ref[...]
        two = range(2)
        qh = [q_ref[:, h * 128:(h + 1) * 128] for h in two]
        kh = [k_ref[:, h * 128:(h + 1) * 128] for h in two]
        s = [_dot(qh[h], kh[h], NT) * ATT_SCALE for h in two]
        e = [jnp.exp(s[h] - jnp.max(s[h], axis=-1, keepdims=True)) for h in two]
        p = [e[h] / jnp.sum(e[h], axis=-1, keepdims=True) for h in two]
        doh = [jnp.where(lane, do, 0.0), jnp.where(lane, 0.0, do)]
        delta = [jnp.sum(doh[h] * o, axis=-1, keepdims=True) for h in two]
        dob = [doh[h].astype(bf16) for h in two]
        dp = [_dot(dob[h], vv, NT) for h in two]
        ds = [(p[h] * (dp[h] - delta[h]) * ATT_SCALE).astype(bf16) for h in two]
        dq = [_dot(ds[h], kh[h], NN) for h in two]
        dk = [_dot(ds[h], qh[h], TN) for h in two]
        dv = [_dot(p[h].astype(bf16), dob[h], TN) for h in two]
        for h in two:
            dq_ref[:, h * 128:(h + 1) * 128] = dq[h]
            dk_ref[:, h * 128:(h + 1) * 128] += dk[h]
        dv_ref[...] += dv[0] + dv[1]

    return _pc(
        body, name=name, grid=(bsz, HEADS // 2, nq),
        in_specs=[pl.BlockSpec((tq, 256), lambda b, h, i: (b * nq + i, h)),
                  pl.BlockSpec((seq, 256), lambda b, h, i: (b, h)),
                  pl.BlockSpec((seq, 128), lambda b, h, i: (b, h)),
                  pl.BlockSpec((tq, 128), lambda b, h, i: (b * nq + i, h)),
                  pl.BlockSpec((tq, 128), lambda b, h, i: (b * nq + i, h))],
        out_specs=[pl.BlockSpec((tq, 256), lambda b, h, i: (b * nq + i, h)),
                   pl.BlockSpec((seq, 256), lambda b, h, i: (b, h)),
                   pl.BlockSpec((seq, 128), lambda b, h, i: (b, h))],
        out_shape=[_sds((n, HEADS * 128)), _sds((n, HEADS * 128)), _sds((n, HEADS * V_HEAD))],
        sem=("parallel", "parallel", "arbitrary"), ride=ride,
    )(q, k, v, o, do)


def _f_sg(pu, pv, lg, lb, bias, *ws):
    u, vv = _gelu(pu), _gelu(pv)
    mu = jnp.mean(vv, axis=-1, keepdims=True)
    d = vv - mu
    vv = d * lax.rsqrt(jnp.mean(d * d, axis=-1, keepdims=True) + LN_EPS) * lg + lb
    group = lax.broadcasted_iota(jnp.int32, (SG_CHUNK, SG_DIM), 1) // (SG_DIM // SG_GROUPS)
    mixed = bias
    for k, w in enumerate(ws):
        mixed = mixed + jnp.where(group == k, _mm(w, vv), 0.0)
    return (u * mixed,)


def _shift_mean(a, prev_row, next_row):
    t = a.shape[0]
    row = lax.broadcasted_iota(jnp.int32, a.shape, 0)
    prev = jnp.where(row == 0, prev_row, pltpu.roll(a, 1, 0))
    nxt = jnp.where(row == t - 1, next_row, pltpu.roll(a, t - 1, 0))
    return 0.5 * (prev + nxt)


def _halo_specs(tm, width, blk, nblk8):
    h = tm // 8
    return [pl.BlockSpec((tm, width), lambda i: (i, blk)),
            pl.BlockSpec((8, width), lambda i: (jnp.maximum(i * h - 1, 0), blk)),
            pl.BlockSpec((8, width), lambda i: (jnp.minimum((i + 1) * h, nblk8 - 1), blk))]


def _edge_rows(i, tm, seq, pv_ref, nx_ref, scale=None):
    first = (i * tm) % seq == 0
    last = ((i + 1) * tm) % seq == 0
    pv, nx = pv_ref[7:8, :], nx_ref[0:1, :]
    if scale is not None:
        pv, nx = pv * scale, nx * scale
    return jnp.where(first, 0.0, pv), jnp.where(last, 0.0, nx)


def _shift_fwd(p, mu, seq, name):
    n = p.shape[0]
    tm = min(256, seq)
    blk = O_RW // RW_W

    def body(x_ref, pv_ref, nx_ref, mu_ref, z_ref):
        x = x_ref[...]
        pv, nx = _edge_rows(pl.program_id(0), tm, seq, pv_ref, nx_ref)
        z_ref[...] = x + mu_ref[...] * (_shift_mean(x, pv, nx) - x)

    return _pc(
        body, name=name, grid=(n // tm,),
        in_specs=_halo_specs(tm, RW_W, blk, n // 8) + [pl.BlockSpec((1, RW_W), lambda i: (0, 0))],
        out_specs=pl.BlockSpec((tm, RW_W), lambda i: (i, 0)), out_shape=_sds((n, RW_W)), sem=("parallel",),
    )(p, p, p, mu)


def _shift_bwd(dz, p, mu, seq, name):
    n = p.shape[0]
    tm = min(256, seq)
    blk = O_RW // RW_W

    def body(dz_ref, dpv_ref, dnx_ref, x_ref, pv_ref, nx_ref, mu_ref, dx_ref, dmu_ref):
        i = pl.program_id(0)
        mu_v = mu_ref[...]
        dzv = dz_ref[...]
        m = dzv * mu_v
        mpv, mnx = _edge_rows(i, tm, seq, dpv_ref, dnx_ref, mu_v)
        dx_ref[...] = (dzv - m + _shift_mean(m, mpv, mnx)).astype(dx_ref.dtype)
        x = x_ref[...]
        pv, nx = _edge_rows(i, tm, seq, pv_ref, nx_ref)
        part = jnp.sum(dzv * (_shift_mean(x, pv, nx) - x), axis=0, keepdims=True)

        @pl.when(i == 0)
        def _():
            dmu_ref[...] = part

        @pl.when(i != 0)
        def _():
            dmu_ref[...] += part

    return _pc(
        body, name=name, grid=(n // tm,),
        in_specs=_halo_specs(tm, RW_W, 0, n // 8) + _halo_specs(tm, RW_W, blk, n // 8)
        + [pl.BlockSpec((1, RW_W), lambda i: (0, 0))],
        out_specs=[pl.BlockSpec((tm, RW_W), lambda i: (i, 0)), pl.BlockSpec((1, RW_W), lambda i: (0, 0))],
        out_shape=[_sds((n, RW_W), bf16), _sds((1, RW_W))], sem=("arbitrary",),
    )(dz, dz, dz, p, p, p, mu)


def _f_rw_pre(k, wl, al, gl, w0, a0, w2, a2, g2, k_k, k_a):
    w = w0 + _mm(jnp.tanh(wl), w2)
    lw = -jnp.exp(-_softplus(-w) - 0.5)
    a = _sigmoid(a0 + _mm(al, a2))
    g = _mm(_sigmoid(gl), g2)
    kkr = k * k_k
    kk = kkr / jnp.maximum(jnp.sqrt(_group_sum(kkr * kkr)), 1e-12)
    two = lambda t: jnp.concatenate([t, t], axis=1)
    kd = two(k) * (1.0 + (a - 1.0) * two(k_a))
    bd = two(kk) * a
    return lw, kd, kk, bd, g


def _f_rw_post(y0, y1, r, v, kd0, kd1, g, r_k, ln_g, ln_b):
    y = y0 + y1
    mean = _group_sum(y) * (1.0 / RW_HEAD)
    d = y - mean
    var = _group_sum(d * d) * (1.0 / RW_HEAD)
    yn = d * lax.rsqrt(var + GN_EPS) * ln_g + ln_b
    bonus = _group_sum(r * (kd0 + kd1) * r_k)
    return ((yn + bonus * v) * g,)


@jax.custom_vjp
def _tri_inv(mats):
    c = mats[0].shape[0]
    row = lax.broadcasted_iota(jnp.int32, (c, c), 0)
    col = lax.broadcasted_iota(jnp.int32, (c, c), 1)
    eye = (row == col).astype(f32)
    blk = lambda b: (row // b) == (col // b)
    ld = [jnp.where(blk(8), a, 0.0) for a in mats]
    l2 = [_bdot(x, x) for x in ld]
    l4 = [_bdot(x, x) for x in l2]
    t = [_bdot(eye - x, eye + y) for x, y in zip(ld, l2)]
    t = [_bdot(x, eye + y) for x, y in zip(t, l4)]
    b = 8
    while b < c:
        sub = blk(2 * b) & jnp.logical_not(blk(b))
        p = [_bdot(x, jnp.where(sub, a, 0.0)) for x, a in zip(t, mats)]
        t = [x - _bdot(y, x) for x, y in zip(t, p)]
        b *= 2
    return tuple(t)


def _tri_inv_fwd(mats):
    t = _tri_inv(mats)
    return t, t


def _tri_inv_bwd(ts, gs):
    p = [_bdot(t, g, TN) for t, g in zip(ts, gs)]
    return (tuple(-_bdot(x, t, NT) for x, t in zip(p, ts)),)


_tri_inv.defvjp(_tri_inv_fwd, _tri_inv_bwd)


def _split3(x):
    h = x.astype(bf16)
    r = x - h.astype(f32)
    m = r.astype(bf16)
    return h, m, (r - m.astype(f32)).astype(bf16)


@jax.custom_vjp
def _mask_mm(mask, x):
    mb = mask.astype(bf16)
    return _dot(jnp.concatenate([mb, mb, mb], axis=1), jnp.concatenate(_split3(x), axis=0), NN)


def _mask_mm_bwd(mask, g):
    mb = mask.astype(bf16)
    return jnp.zeros_like(mask), _dot(jnp.concatenate([mb, mb, mb], axis=0), jnp.concatenate(_split3(g), axis=0), TN)


_mask_mm.defvjp(lambda mask, x: (_mask_mm(mask, x), mask), _mask_mm_bwd)


@jax.custom_vjp
def _split_lanes(x):
    h = x.shape[1] // 2
    return x[:, :h], x[:, h:]


_split_lanes.defvjp(lambda x: (_split_lanes(x), None), lambda _, g: (jnp.concatenate(g, axis=1),))


def _scan_chunk(s0, r, v, kk, lw, kd, bd, rev):
    n = len(r)
    each = range(n)
    c = r[0].shape[0]
    row = lax.broadcasted_iota(jnp.int32, (c, 2 * c), 0)
    col = lax.broadcasted_iota(jnp.int32, (c, 2 * c), 1) % c
    ahead = jnp.where(rev, col - row, row - col)
    before = ahead > 0
    incl = ahead >= 0
    lane = lax.broadcasted_iota(jnp.int32, (1, 128), 1)
    m0 = (lane < 64).astype(f32)
    heads = lambda t: jnp.concatenate([t * m0, t * (1.0 - m0)], axis=0)
    bd_mask = ((lax.broadcasted_iota(jnp.int32, (128, 128), 0) // 64)
               == (lax.broadcasted_iota(jnp.int32, (128, 128), 1) // 64)).astype(f32)
    tot = [jnp.sum(lw[i], axis=0, keepdims=True) for i in each]
    row1 = lax.broadcasted_iota(jnp.int32, (c, c), 0)
    col1 = lax.broadcasted_iota(jnp.int32, (c, c), 1)
    upto = (jnp.where(rev, col1 - row1, row1 - col1) >= 0).astype(f32)
    lp = [_mask_mm(upto, lw[i]) - 0.5 * tot[i] for i in each]
    eg = [jnp.exp(lp[i]) for i in each]
    ieg = [jnp.exp(-lp[i]) for i in each]
    rt = [r[i] * eg[i] for i in each]
    kt = [kd[i] * ieg[i] for i in each]
    bt = [bd[i] * ieg[i] for i in each]
    at = [kk[i] * jnp.exp(lp[i] - lw[i]) for i in each]
    etot = [jnp.exp(0.5 * tot[i]) for i in each]
    si = [s0[i] * etot[i] for i in each]
    bth = [heads(bt[i]) for i in each]
    kth = [heads(kt[i]) for i in each]
    vh = [heads(v[i]) for i in each]
    a_ab = [jnp.where(before, _mm_nt(at[i], bth[i]), 0.0) for i in each]
    a_ak = [jnp.where(before, _mm_nt(at[i], kth[i]), 0.0) for i in each]
    a_rb = [jnp.where(incl, _mm_nt(rt[i], bth[i]), 0.0) for i in each]
    a_rk = [jnp.where(incl, _mm_nt(rt[i], kth[i]), 0.0) for i in each]
    halves = [_split_lanes(a_ab[i]) for i in each]
    inv = _tri_inv(tuple(m for pair in halves for m in pair))
    t = [jnp.concatenate([inv[2 * i], inv[2 * i + 1]], axis=1) for i in each]
    x0 = [_mm_nt(at[i], si[i]) for i in each]
    x = [x0[i] + _mm(a_ak[i], vh[i]) for i in each]
    u = [-_mm(t[i], heads(x[i])) for i in each]
    y0 = [_mm_nt(rt[i], si[i]) for i in each]
    y = [y0[i] + _mm(jnp.concatenate([a_rb[i], a_rk[i]], axis=1), jnp.concatenate([heads(u[i]), vh[i]], axis=0))
         for i in each]
    ds = [_mm_tn(jnp.concatenate([u[i], v[i]], axis=0), jnp.concatenate([bt[i], kt[i]], axis=0)) for i in each]
    se = [(si[i] + ds[i] * bd_mask) * etot[i] for i in each]
    return tuple(y), tuple(se)


PAIRS = HEADS // 2


def _scan_specs(nc, bsz, flip=False):
    def cc(d, c):
        c = nc - 1 - c if flip else c
        return jnp.where(d == 0, c, nc - 1 - c)

    rowblk = lambda d, b, c: b * nc + cc(d, c)
    zspec = lambda blk: pl.BlockSpec((CHUNK, RW_DIM), lambda d, b, c: (rowblk(d, b, c), blk))
    dspec = pl.BlockSpec((CHUNK, RW_DIM), lambda d, b, c: (rowblk(d, b, c), d))
    yspec = pl.BlockSpec((None, CHUNK, RW_DIM), lambda d, b, c: (d, rowblk(d, b, c), 0))
    sspec = pl.BlockSpec((None, PAIRS, 128, 128), lambda d, b, c: ((d * bsz + b) * nc + cc(d, c), 0, 0, 0))
    return zspec, dspec, yspec, sspec


def _scan_fwd(z, lw, kd, kk, bd, bsz, seq, name, ride=None):
    n = z.shape[0]
    nc = seq // CHUNK
    zspec, dspec, yspec, sspec = _scan_specs(nc, bsz)

    def body(r_ref, v_ref, kk_ref, lw_ref, kd_ref, bd_ref, y_ref, s_ref, st):
        @pl.when(pl.program_id(2) == 0)
        def _():
            st[...] = jnp.zeros_like(st)

        rev = pl.program_id(0) == 1
        lanes = [slice(h * 128, (h + 1) * 128) for h in range(PAIRS)]
        s0 = tuple(st[h] for h in range(PAIRS))
        ops = [tuple(ref[:, ln] for ln in lanes) for ref in (r_ref, v_ref, kk_ref, lw_ref, kd_ref, bd_ref)]
        y, se = _scan_chunk(s0, *ops, rev)
        for h, ln in enumerate(lanes):
            s_ref[h] = s0[h]
            y_ref[:, ln] = y[h]
            st[h] = se[h]

    return _pc(
        body, name=name, grid=(2, bsz, nc),
        in_specs=[zspec(0), zspec(2), zspec(0), dspec, dspec, dspec],
        out_specs=[yspec, sspec],
        out_shape=[_sds((2, n, RW_DIM)), _sds((2 * bsz * nc, PAIRS, 128, 128))],
        scratch=[pltpu.VMEM((PAIRS, 128, 128), f32)], sem=("parallel", "parallel", "arbitrary"), ride=ride,
    )(z, z, kk, lw, kd, bd)


def _scan_bwd(z, lw, kd, kk, bd, s_in, dy, bsz, seq, name, ride=None):
    n = z.shape[0]
    nc = seq // CHUNK
    zspec, dspec, yspec, sspec = _scan_specs(nc, bsz, flip=True)

    def body(r_ref, v_ref, kk_ref, lw_ref, kd_ref, bd_ref, s_ref, dy_ref,
             dr_ref, dv_ref, dkk_ref, dlw_ref, dkd_ref, dbd_ref, dst):
        @pl.when(pl.program_id(2) == 0)
        def _():
            dst[...] = jnp.zeros_like(dst)

        rev = pl.program_id(0) == 1
        lanes = [slice(h * 128, (h + 1) * 128) for h in range(PAIRS)]
        s0 = tuple(s_ref[h] for h in range(PAIRS))
        ops = [tuple(ref[:, ln] for ln in lanes) for ref in (r_ref, v_ref, kk_ref, lw_ref, kd_ref, bd_ref)]
        cot = (tuple(dy_ref[:, ln] for ln in lanes), tuple(dst[h] for h in range(PAIRS)))
        _, vjp = jax.vjp(functools.partial(_scan_chunk, rev=rev), s0, *ops)
        grads = vjp(cot)
        for h, ln in enumerate(lanes):
            dst[h] = grads[0][h]
            for o_ref, g in zip((dr_ref, dv_ref, dkk_ref, dlw_ref, dkd_ref, dbd_ref), grads[1:]):
                o_ref[:, ln] = g[h]

    return _pc(
        body, name=name, grid=(2, bsz, nc),
        in_specs=[zspec(0), zspec(2), zspec(0), dspec, dspec, dspec, sspec, zspec(0)],
        out_specs=[yspec, yspec, yspec, dspec, dspec, dspec],
        out_shape=[_sds((2, n, RW_DIM))] * 3 + [_sds((n, 2 * RW_DIM))] * 3,
        scratch=[pltpu.VMEM((PAIRS, 128, 128), f32)], sem=("parallel", "parallel", "arbitrary"), ride=ride,
    )(z, z, kk, lw, kd, bd, s_in, dy)


def _merge_fwd(x2, p, ya, yb, yc, gb, wb, wo, name):
    n = x2.shape[0]
    tm = min(256, n)

    def body(x_ref, pg_ref, ya_ref, yb_ref, yc_ref, gb_ref, wb_ref, wo_ref, o_ref):
        gates = _sigmoid(pg_ref[...] + gb_ref[...])
        merged = jnp.zeros((tm, D), f32)
        for k, y_ref in enumerate((ya_ref, yb_ref, yc_ref)):
            merged += gates[:, k * D:(k + 1) * D] * _bdot(y_ref[...], wb_ref[k])
        o_ref[...] = x_ref[...] + _bdot(merged, wo_ref[...])

    row = lambda w, b=0: pl.BlockSpec((tm, w), lambda i, b=b: (i, b))
    return _pc(
        body, name=name, grid=(n // tm,),
        in_specs=[row(D), row(3 * D, O_GATE // (3 * D)), row(512), row(512), row(512),
                  pl.BlockSpec((1, 3 * D), lambda i: (0, 0)), pl.BlockSpec((3, 512, D), lambda i: (0, 0, 0)),
                  pl.BlockSpec((D, D), lambda i: (0, 0))],
        out_specs=row(D), out_shape=_sds((n, D)), sem=("parallel",),
    )(x2, p, ya, yb, yc, gb, wb, wo)


def _merge_bwd(dx1, p, ya, yb, yc, gb, wb, wo, name):
    n = dx1.shape[0]
    tm = min(256, n)

    def body(dx_ref, pg_ref, ya_ref, yb_ref, yc_ref, gb_ref, wb_ref, wo_ref,
             dpg_ref, dya_ref, dyb_ref, dyc_ref, dt_ref, mg_ref, dgb_ref):
        gates = _sigmoid(pg_ref[...] + gb_ref[...])
        dmerged = _bdot(dx_ref[...], wo_ref[...], NT)
        merged = jnp.zeros((tm, D), f32)
        dpg = []
        for k, (y_ref, dy_ref) in enumerate(((ya_ref, dya_ref), (yb_ref, dyb_ref), (yc_ref, dyc_ref))):
            gk = gates[:, k * D:(k + 1) * D]
            tk = _bdot(y_ref[...], wb_ref[k])
            merged += gk * tk
            dpg.append(dmerged * tk * gk * (1.0 - gk))
            dtk = dmerged * gk
            dt_ref[:, k * D:(k + 1) * D] = dtk.astype(bf16)
            dy_ref[...] = _bdot(dtk, wb_ref[k], NT)
        dpg = jnp.concatenate(dpg, axis=1)
        dpg_ref[...] = dpg.astype(bf16)
        mg_ref[...] = merged.astype(bf16)
        part = jnp.sum(dpg, axis=0, keepdims=True)

        @pl.when(pl.program_id(0) == 0)
        def _():
            dgb_ref[...] = part

        @pl.when(pl.program_id(0) != 0)
        def _():
            dgb_ref[...] += part

    row = lambda w, b=0: pl.BlockSpec((tm, w), lambda i, b=b: (i, b))
    return _pc(
        body, name=name, grid=(n // tm,),
        in_specs=[row(D), row(3 * D, O_GATE // (3 * D)), row(512), row(512), row(512),
                  pl.BlockSpec((1, 3 * D), lambda i: (0, 0)), pl.BlockSpec((3, 512, D), lambda i: (0, 0, 0)),
                  pl.BlockSpec((D, D), lambda i: (0, 0))],
        out_specs=[row(3 * D), row(512), row(512), row(512), row(3 * D), row(D),
                   pl.BlockSpec((1, 3 * D), lambda i: (0, 0))],
        out_shape=[_sds((n, 3 * D), bf16), _sds((n, 512)), _sds((n, 512)), _sds((n, 512)), _sds((n, 3 * D), bf16),
                   _sds((n, D), bf16), _sds((1, 3 * D))],
        sem=("arbitrary",),
    )(dx1, p, ya, yb, yc, gb, wb, wo)


FF_T = 1408


def _ffn_fwd(x1, g, wg, wu, wd, name):
    n = x1.shape[0]
    tm = min(512, n)
    nf = D_FF // FF_T

    def body(x_ref, g_ref, wg_ref, wu_ref, wd_ref, o_ref, hs):
        j = pl.program_id(1)

        @pl.when(j == 0)
        def _():
            hs[...] = _rms(x_ref[...], g_ref[...]).astype(bf16)
            o_ref[...] = x_ref[...]

        a = _dot(hs[...], wg_ref[...], NN)
        b = _dot(hs[...], wu_ref[...], NN)
        o_ref[...] += _bdot(a * _sigmoid(a) * b, wd_ref[...])

    return _pc(
        body, name=name, grid=(n // tm, nf),
        in_specs=[pl.BlockSpec((tm, D), lambda i, j: (i, 0)), pl.BlockSpec((1, D), lambda i, j: (0, 0)),
                  pl.BlockSpec((D, FF_T), lambda i, j: (0, j)), pl.BlockSpec((D, FF_T), lambda i, j: (0, j)),
                  pl.BlockSpec((FF_T, D), lambda i, j: (j, 0))],
        out_specs=pl.BlockSpec((tm, D), lambda i, j: (i, 0)), out_shape=_sds((n, D)),
        scratch=[pltpu.VMEM((tm, D), bf16)], sem=("parallel", "arbitrary"),
    )(x1, g, wg, wu, wd)


def _ffn_bwd(dx2, x1, g, wg, wu, wd, name):
    n = x1.shape[0]
    tm = min(512, n)
    nf = D_FF // FF_T

    def body(dx_ref, x_ref, g_ref, wg_ref, wu_ref, wd_ref, dx1_ref, dg_ref, h_ref, da_ref, db_ref, hm_ref, acc):
        i, j = pl.program_id(0), pl.program_id(1)

        @pl.when(j == 0)
        def _():
            h_ref[...] = _rms(x_ref[...], g_ref[...]).astype(bf16)
            acc[...] = jnp.zeros_like(acc)

        @pl.when((i == 0) & (j == 0))
        def _():
            dg_ref[...] = jnp.zeros_like(dg_ref)

        h = h_ref[...]
        a = _dot(h, wg_ref[...], NN)
        b = _dot(h, wu_ref[...], NN)
        sg = _sigmoid(a)
        s = a * sg
        dhm = _bdot(dx_ref[...], wd_ref[...], NT)
        da = (dhm * b * (sg * (1.0 + a * (1.0 - sg)))).astype(bf16)
        db = (dhm * s).astype(bf16)
        da_ref[...] = da
        db_ref[...] = db
        hm_ref[...] = (s * b).astype(bf16)
        acc[...] += _dot(da, wg_ref[...], NT) + _dot(db, wu_ref[...], NT)

        @pl.when(j == nf - 1)
        def _():
            _, vjp = jax.vjp(_rms, x_ref[...], g_ref[...])
            dx, dg = vjp(acc[...])
            dx1_ref[...] = dx_ref[...] + dx
            dg_ref[...] += dg

    rowf = pl.BlockSpec((tm, FF_T), lambda i, j: (i, j))
    rowd = pl.BlockSpec((tm, D), lambda i, j: (i, 0))
    vec = pl.BlockSpec((1, D), lambda i, j: (0, 0))
    return _pc(
        body, name=name, grid=(n // tm, nf),
        in_specs=[rowd, rowd, vec, pl.BlockSpec((D, FF_T), lambda i, j: (0, j)),
                  pl.BlockSpec((D, FF_T), lambda i, j: (0, j)), pl.BlockSpec((FF_T, D), lambda i, j: (j, 0))],
        out_specs=[rowd, vec, rowd, rowf, rowf, rowf],
        out_shape=[_sds((n, D)), _sds((1, D)), _sds((n, D), bf16), _sds((n, D_FF), bf16), _sds((n, D_FF), bf16),
                   _sds((n, D_FF), bf16)],
        scratch=[pltpu.VMEM((tm, D), f32)], sem=("arbitrary", "arbitrary"),
    )(dx2, x1, g, wg, wu, wd)


def _loss_head(x2, g, tgt, name):
    n = x2.shape[0]
    tm = min(512, n)

    def f(x, gg, t):
        e = _rms(x, gg) - t
        return 0.5 * jnp.sum(jnp.mean(e * e, axis=-1, keepdims=True))

    def body(x_ref, g_ref, t_ref, l_ref, dx_ref, dg_ref):
        val, vjp = jax.vjp(f, x_ref[...], g_ref[...], t_ref[...])
        dx, dg, _ = vjp(jnp.ones((), f32))
        dx_ref[...] = dx

        @pl.when(pl.program_id(0) == 0)
        def _():
            l_ref[...] = jnp.zeros_like(l_ref)
            dg_ref[...] = jnp.zeros_like(dg_ref)

        l_ref[...] += val
        dg_ref[...] += dg

    rowd = pl.BlockSpec((tm, D), lambda i: (i, 0))
    return _pc(
        body, name=name, grid=(n // tm,),
        in_specs=[rowd, pl.BlockSpec((1, D), lambda i: (0, 0)), rowd],
        out_specs=[pl.BlockSpec((8, 128), lambda i: (0, 0)), rowd, pl.BlockSpec((1, D), lambda i: (0, 0))],
        out_shape=[_sds((8, 128)), _sds((n, D)), _sds((1, D))], sem=("arbitrary",),
    )(x2, g, tgt)


def _adamw(w, parts, m, v, name):
    nl, r, c = w.shape
    tr = r
    for cand in (1024, 512, 256, 128, 64, 32, 16, 8):
        if r % cand == 0 and cand * c * 4 <= 1024 * 1024:
            tr = cand
            break

    def body(*refs):
        w_ref, p_refs, (m_ref, v_ref, g_ref, d_ref, nm_ref, nv_ref) = refs[0], refs[1:1 + nl], refs[1 + nl:]

        def update(p_ref):
            gg = p_ref[0].astype(f32)
            for k in range(1, N_DEV):
                gg = gg + p_ref[k].astype(f32)
            g_ref[...] = gg
            nm = B1 * m_ref[...] + (1.0 - B1) * gg
            nv = B2 * v_ref[...] + (1.0 - B2) * (gg * gg)
            m_hat = nm / (1.0 - B1 ** STEP)
            v_hat = nv / (1.0 - B2 ** STEP)
            d_ref[...] = -LR * (m_hat / (jnp.sqrt(v_hat) + EPS) + WD * w_ref[...])
            nm_ref[...] = nm
            nv_ref[...] = nv

        for j in range(nl):
            pl.when(pl.program_id(0) == j)(functools.partial(update, p_refs[j]))

    spec = pl.BlockSpec((None, tr, c), lambda l, i: (l, i, 0))
    pspecs = [pl.BlockSpec((N_DEV, tr, c), lambda l, i, j=j: (0, jnp.where(l == j, i, 0), 0)) for j in range(nl)]
    return _pc(body, name=name, grid=(nl, r // tr), in_specs=[spec] + pspecs + [spec, spec], out_specs=[spec] * 4,
               out_shape=[_sds((nl, r, c))] * 4, sem=("arbitrary", "arbitrary"))(w, *parts, m, v)


def _peers():
    x, y, c = lax.axis_index("x"), lax.axis_index("y"), lax.axis_index("c")
    me = 4 * x + 2 * y + c
    peers = []
    for k in range(1, N_DEV):
        fx, fy, fc = (k >> 2) & 1, (k >> 1) & 1, k & 1
        peers.append(((1 - x) if fx else x, (1 - y) if fy else y, (1 - c) if fc else c))
    return me, peers


def _exchange(gathers, scatters, name):
    _, got = _pc(lambda: None, name=name, out_shape=[], ride=(gathers, scatters))()
    return got


SHARDED = {"w_in": 2, "gate_b": 2, "w_uq": 2, "w_ukv": 2, "rw_w0": 2, "rw_w2": 3, "rw_a0": 2, "rw_a2": 3, "rw_g2": 2,
           "w_branch": 3, "w_out": 1, "w_ffn_gate": 2, "w_ffn_up": 2, "w_ffn_down": 1}
GATHER_F32 = ("gate_b", "rw_w0", "rw_a0")
REPLICATED = ("attn_norm_g", "q_norm_g", "kv_norm_g", "sg_ln_g", "sg_ln_b", "sg_w", "sg_b", "rw_mu", "rw_k_k", "rw_k_a",
              "rw_r_k", "rw_ln_g", "rw_ln_b", "ffn_norm_g", "final_norm_g")
WEIGHTS = ("attn_norm_g", "w_in", "gate_b", "q_norm_g", "w_uq", "kv_norm_g", "w_ukv", "sg_ln_g", "sg_ln_b", "sg_w", "sg_b",
           "rw_mu", "rw_w0", "rw_w2", "rw_a0", "rw_a2", "rw_g2", "rw_k_k", "rw_k_a", "rw_r_k", "rw_ln_g", "rw_ln_b",
           "w_branch", "w_out", "ffn_norm_g", "w_ffn_gate", "w_ffn_up", "w_ffn_down", "final_norm_g")


BIG = ("w_in", "w_branch", "w_out", "w_ffn_gate", "w_ffn_up", "w_ffn_down")
SMALL_BF = ("w_uq", "w_ukv", "rw_w2", "rw_a2", "rw_g2")
SMALL = SMALL_BF + GATHER_F32


def _pack128(blocks, names, dtype, lead=0):
    parts = [blocks[k].astype(dtype).reshape(blocks[k].shape[:lead] + (-1, 128)) for k in names]
    rows = sum(p.shape[lead] for p in parts)
    pad = -rows % 256
    if pad:
        parts.append(jnp.zeros(parts[0].shape[:lead] + (pad, 128), dtype))
    return jnp.concatenate(parts, axis=lead)


def _unpack128(packed, shapes, names, lead=0):
    out, off = {}, 0
    for k in names:
        rows = 1
        for d in shapes[k]:
            rows *= d
        rows //= 128
        idx = (slice(None),) * lead + (slice(off, off + rows),)
        out[k] = packed[idx].reshape(packed.shape[:lead] + tuple(shapes[k]))
        off += rows
    return out


def _join_blocks(g, ax):
    shp = g.shape[1:]
    return jnp.moveaxis(g, 0, ax).reshape(shp[:ax] + (N_DEV * shp[ax],) + shp[ax + 1:])


def _split_blocks(full, ax):
    shp = full.shape
    return jnp.moveaxis(full.reshape(shp[:ax] + (N_DEV, shp[ax] // N_DEV) + shp[ax + 1:]), ax, 0)


def _w_in_padded(w):
    z = lambda n: jnp.zeros((w.shape[0], n), w.dtype)
    q, ckv, kr = w[:, 0:384], w[:, 384:640], w[:, 640:672]
    sg, rw, gate = w[:, 672:1696], w[:, 1696:3616], w[:, 3616:6688]
    return jnp.concatenate([gate, sg, rw, z(128), ckv, z(64), kr, z(32), q, z(P_W - O_MLA - MLA_W)], axis=1)


def _w_in_unpadded(g):
    return jnp.concatenate([g[:, O_Q:O_Q + 384], g[:, O_CKV:O_CKV + 256], g[:, O_SLAB + 64:O_SLAB + 96],
                            g[:, O_SG:O_SG + 1024], g[:, O_RW:O_RW + 1920], g[:, O_GATE:O_GATE + 3072]], axis=1)


REST = ("w_branch", "w_out", "w_ffn_gate", "w_ffn_up", "w_ffn_down")


def _rest_weights(full, l):
    return dict(wb=full["w_branch"][l], wo=full["w_out"][l], wg=full["w_ffn_gate"][l], wu=full["w_ffn_up"][l],
                wd=full["w_ffn_down"][l])


def _layer_weights(full, rep, l):
    w = {}
    w["w_in"] = _w_in_padded(full["w_in"][l])
    if full["w_branch"][l] is not None:
        w.update(_rest_weights(full, l))
    uq = full["w_uq"][l].reshape(Q_LORA, HEADS, QK_NOPE + QK_ROPE)
    w["wq"] = jnp.pad(uq, ((0, 0), (0, 0), (0, 32))).reshape(Q_LORA, HEADS * 128).astype(f32)
    ukv = full["w_ukv"][l].reshape(KV_LORA, HEADS, QK_NOPE + V_HEAD)
    wk = jnp.pad(ukv[:, :, :QK_NOPE], ((0, 0), (0, 0), (0, 64))).reshape(KV_LORA, HEADS * 128)
    w["wk"], w["wv"] = wk.astype(f32), ukv[:, :, QK_NOPE:].reshape(KV_LORA, HEADS * V_HEAD).astype(f32)
    bdiag = lambda t: jnp.concatenate([jnp.concatenate([t[0], jnp.zeros_like(t[0])], axis=1),
                                       jnp.concatenate([jnp.zeros_like(t[1]), t[1]], axis=1)], axis=0).astype(f32)
    w["w2"], w["a2"] = bdiag(full["rw_w2"][l]), bdiag(full["rw_a2"][l])
    w["g2"] = full["rw_g2"][l].astype(f32)
    w["w0"], w["a0"] = full["rw_w0"][l].reshape(1, 2 * RW_DIM), full["rw_a0"][l].reshape(1, 2 * RW_DIM)
    w["gate_b"] = full["gate_b"][l].reshape(1, 3 * D)
    row = lambda a: a.reshape(1, -1)
    for k in ("attn_norm_g", "q_norm_g", "kv_norm_g", "sg_ln_g", "sg_ln_b", "rw_k_k", "rw_k_a", "rw_ln_g", "rw_ln_b",
              "ffn_norm_g"):
        w[k] = row(rep[k][l])
    w["r_k"] = row(rep["rw_r_k"][l])
    w["mu"] = jnp.pad(row(rep["rw_mu"][l]), ((0, 0), (0, RW_W - 1920)))
    w["sg_w"] = [rep["sg_w"][l, k] for k in range(SG_GROUPS)]
    w["sg_bias"] = jnp.repeat(rep["sg_b"][l].T, SG_DIM // SG_GROUPS, axis=1)
    return w


def _riding(res, ride, got, key):
    if ride is None:
        return res
    got[key] = res[1]
    return res[0]


def _layer_fwd(x2, w, tabs, bsz, seq, l, rides=None, on_inproj=None):
    nm = lambda s: f"l{l}_{s}"
    n = x2.shape[0]
    tm = min(256, n)
    rides = rides or {}
    ride = lambda key: (rides[key], []) if key in rides else None
    got = {}
    p, h = _riding(_inproj_fwd(x2, w["attn_norm_g"], w["w_in"], nm("inproj"), ride("inproj")), ride("inproj"), got, "inproj")
    if on_inproj is not None:
        w.update(on_inproj(got["inproj"]))
    mla_rows = [(p, 256, O_CKV // 256), (p, 128, O_SLAB // 128), (p, 384, O_Q // 384), (tabs[0], 128, 0), (tabs[1], 128, 0)]
    mla_w = [w["q_norm_g"], w["kv_norm_g"], w["wq"], w["wk"], w["wv"]]
    q, k, v = _rowwise_fwd(nm("mla_proj"), _f_mla_proj, mla_rows, mla_w, [(1024, bf16), (1024, bf16), (512, bf16)], tm)
    ya = _riding(_attn_fwd(q, k, v, bsz, seq, nm("attn"), ride("attn")), ride("attn"), got, "attn")
    sg_rows = [(p, SG_DIM, O_SG // SG_DIM), (p, SG_DIM, O_SG // SG_DIM + 1)]
    sg_w = [w["sg_ln_g"], w["sg_ln_b"], w["sg_bias"]] + w["sg_w"]
    (yb,) = _rowwise_fwd(nm("sg"), _f_sg, sg_rows, sg_w, [(SG_DIM, f32)], SG_CHUNK)
    z = _shift_fwd(p, w["mu"], seq, nm("shift"))
    pre_rows = [(z, 512, 1), (z, 128, 12), (z, 128, 13), (z, 128, 14)]
    pre_w = [w["w0"], w["a0"], w["w2"], w["a2"], w["g2"], w["rw_k_k"], w["rw_k_a"]]
    lw, kd, kk, bd, g = _rowwise_fwd(nm("rw_pre"), _f_rw_pre, pre_rows, pre_w,
                                     [(1024, f32), (1024, f32), (512, f32), (1024, f32), (512, f32)], tm)
    y, s_in = _riding(_scan_fwd(z, lw, kd, kk, bd, bsz, seq, nm("scan"), ride("scan")), ride("scan"), got, "scan")
    post_rows = [(y[0], 512, 0), (y[1], 512, 0), (z, 512, 0), (z, 512, 2), (kd, 512, 0), (kd, 512, 1), (g, 512, 0)]
    post_w = [w["r_k"], w["rw_ln_g"], w["rw_ln_b"]]
    (yc,) = _rowwise_fwd(nm("rw_post"), _f_rw_post, post_rows, post_w, [(512, f32)], tm)
    x1 = _merge_fwd(x2, p, ya, yb, yc, w["gate_b"], w["wb"], w["wo"], nm("merge"))
    x3 = _ffn_fwd(x1, w["ffn_norm_g"], w["wg"], w["wu"], w["wd"], nm("ffn"))
    saved = dict(x=x2, p=p, h=h, q=q, k=k, v=v, ya=ya, yb=yb, z=z, lw=lw, kd=kd, kk=kk, bd=bd, g=g, y=y, s_in=s_in, yc=yc,
                 x1=x1, mla_rows=mla_rows, mla_w=mla_w, sg_rows=sg_rows, sg_w=sg_w, pre_rows=pre_rows, pre_w=pre_w,
                 post_rows=post_rows, post_w=post_w)
    return x3, saved, got


def _layer_bwd(dx3, w, sv, bsz, seq, l, rides=None):
    nm = lambda s: f"l{l}_{s}_bwd"
    n = dx3.shape[0]
    tm = min(256, n)
    g = {}
    rides = rides or {}
    ride = lambda key: ([], rides[key](g)) if key in rides else None
    got = {}
    dx1, g["ffn_norm_g"], h2, da, db, hm = _ffn_bwd(dx3, sv["x1"], w["ffn_norm_g"], w["wg"], w["wu"], w["wd"], nm("ffn"))
    g["wg"] = _matmul_tn(h2, da, nm("wg"))
    g["wu"] = _matmul_tn(h2, db, nm("wu"))
    g["wd"] = _matmul_tn(hm, dx3.astype(bf16), nm("wd"))
    dpg, dya, dyb, dyc, dt, mg, g["gate_b"] = _merge_bwd(dx1, sv["p"], sv["ya"], sv["yb"], sv["yc"], w["gate_b"], w["wb"],
                                                         w["wo"], nm("merge"))
    g["wo"] = _matmul_tn(mg, dx1.astype(bf16), nm("wo"))
    ys = (sv["ya"], sv["yb"], sv["yc"])
    g["wb"] = jnp.stack([_matmul_tn(ys[k].astype(bf16), dt[:, k * D:(k + 1) * D], nm(f"wb{k}")) for k in range(3)])
    (dy, dr_p, dv_p, dkd0, dkd1, dg_), (g["r_k"], g["rw_ln_g"], g["rw_ln_b"]) = _rowwise_bwd(
        nm("rw_post"), _f_rw_post, sv["post_rows"], sv["post_w"], [(dyc, 512, 0)], tm, [f32, None] + [f32] * 5)
    dkd_p = jnp.concatenate([dkd0, dkd1], axis=1)
    rd = ride("scan")
    dr_s, dv_s, dkk_s, dlw, dkd_s, dbd = _riding(
        _scan_bwd(sv["z"], sv["lw"], sv["kd"], sv["kk"], sv["bd"], sv["s_in"], dy, bsz, seq, nm("scan"), rd), rd, got, "scan")
    pre_cots = [(dlw, 1024, 0), (dkd_s + dkd_p, 1024, 0), (dkk_s[0] + dkk_s[1], 512, 0), (dbd, 1024, 0), (dg_, 512, 0)]
    (dk, dwl, dal, dgl), (g["w0"], g["a0"], g["w2"], g["a2"], g["g2"], g["rw_k_k"], g["rw_k_a"]) = _rowwise_bwd(
        nm("rw_pre"), _f_rw_pre, sv["pre_rows"], sv["pre_w"], pre_cots, tm, [f32] * 4)
    dz = jnp.concatenate([dr_s[0] + dr_s[1] + dr_p, dk, dv_s[0] + dv_s[1] + dv_p, dwl, dal, dgl,
                          jnp.zeros((n, RW_W - 1920), f32)], axis=1)
    dp_rw, g["mu"] = _shift_bwd(dz, sv["p"], w["mu"], seq, nm("shift"))
    (dp_su, dp_sv), (g["sg_ln_g"], g["sg_ln_b"], g["sg_bias"], *sgw) = _rowwise_bwd(
        nm("sg"), _f_sg, sv["sg_rows"], sv["sg_w"], [(dyb, SG_DIM, 0)], SG_CHUNK, [bf16, bf16])
    g["sg_w"] = jnp.stack(sgw)
    rd = ride("attn")
    dq, dk_, dv_ = _riding(_attn_bwd(sv["q"], sv["k"], sv["v"], sv["ya"], dya, bsz, seq, nm("attn"), rd), rd, got, "attn")
    (dp_ckv, dp_slab, dp_q), (g["q_norm_g"], g["kv_norm_g"], g["wq"], g["wk"], g["wv"]) = _rowwise_bwd(
        nm("mla_proj"), _f_mla_proj, sv["mla_rows"], sv["mla_w"], [(dq, 1024, 0), (dk_, 1024, 0), (dv_, 512, 0)], tm,
        [bf16, bf16, bf16, None, None])
    dp = jnp.concatenate([dpg, dp_su, dp_sv, dp_rw, dp_ckv, dp_slab, dp_q, jnp.zeros((n, P_W - O_MLA - MLA_W), bf16)],
                         axis=1)
    g["w_in"] = _matmul_tn(sv["h"], dp, nm("w_in"))
    dx, g["attn_norm_g"] = _norm_matmul_bwd(dp, w["w_in"], sv["x"], w["attn_norm_g"], dx1, nm("inproj"))
    return dx, g, got


def _layer_grads_to_full(g):
    o = {}
    o["w_in"] = _w_in_unpadded(g["w_in"])
    o["w_uq"] = g["wq"].reshape(Q_LORA, HEADS, 128)[:, :, :QK_NOPE + QK_ROPE].reshape(Q_LORA, -1)
    gk = g["wk"].reshape(KV_LORA, HEADS, 128)[:, :, :QK_NOPE]
    gv = g["wv"].reshape(KV_LORA, HEADS, V_HEAD)
    o["w_ukv"] = jnp.concatenate([gk, gv], axis=2).reshape(KV_LORA, -1)
    unb = lambda t: jnp.stack([t[:LORA, :RW_DIM], t[LORA:, RW_DIM:]])
    o["rw_w2"], o["rw_a2"], o["rw_g2"] = unb(g["w2"]), unb(g["a2"]), g["g2"]
    o["rw_w0"], o["rw_a0"] = g["w0"].reshape(2, RW_DIM), g["a0"].reshape(2, RW_DIM)
    o["gate_b"] = g["gate_b"].reshape(3, D)
    o["w_branch"], o["w_out"] = g["wb"], g["wo"]
    o["w_ffn_gate"], o["w_ffn_up"], o["w_ffn_down"] = g["wg"], g["wu"], g["wd"]
    for k in ("attn_norm_g", "q_norm_g", "kv_norm_g", "sg_ln_g", "sg_ln_b", "rw_k_k", "rw_k_a", "rw_ln_g", "rw_ln_b",
              "ffn_norm_g"):
        o[k] = g[k].reshape(-1)
    o["rw_r_k"] = g["r_k"].reshape(HEADS, RW_HEAD)
    o["rw_mu"] = g["mu"].reshape(-1)[:1920]
    o["sg_w"] = g["sg_w"]
    o["sg_b"] = g["sg_bias"].reshape(SG_CHUNK, SG_GROUPS, SG_DIM // SG_GROUPS).sum(axis=2).T
    return o


def _rope_tables(positions):
    inv = 1.0 / (10000.0 ** (jnp.arange(0, QK_ROPE, 2, dtype=f32) / QK_ROPE))
    ang = positions.astype(f32)[:, None] * inv
    cos, sin = jnp.cos(ang), jnp.sin(ang)
    n = positions.shape[0]
    c = jnp.concatenate([jnp.ones((n, 64), f32), cos, cos, jnp.zeros((n, 32), f32)], axis=1)
    s = jnp.concatenate([jnp.zeros((n, 64), f32), -sin, sin, jnp.zeros((n, 32), f32)], axis=1)
    return c, s


def _grad_parts(grad, name):
    return _split_blocks(grad, SHARDED[name] - 1).astype(bf16)


def _local_step(x, positions, full, rep, loss_target, blocks=None):
    bsz, seq, _ = x.shape
    n = bsz * seq
    x2 = x.reshape(n, D)
    tabs = _rope_tables(positions.reshape(n))
    join = lambda k, g: _join_blocks(g, SHARDED[k] - 1)
    rides, on_inproj = None, None
    if blocks is not None:
        rides = {"inproj": [blocks[k][0] for k in REST], "attn": [blocks["w_in"][1]], "scan": [blocks[k][1] for k in REST]}

        def on_inproj(got):
            for k, g in zip(REST, got):
                full[k][0] = join(k, g)
            return _rest_weights(full, 0)

    w0 = _layer_weights(full, rep, 0)
    x2, sv0, got = _layer_fwd(x2, w0, tabs, bsz, seq, 0, rides, on_inproj)
    if blocks is not None:
        full["w_in"][1] = join("w_in", got["attn"][0])
        for k, g in zip(REST, got["scan"]):
            full[k][1] = join(k, g)
    w1 = _layer_weights(full, rep, 1)
    x2, sv1, _ = _layer_fwd(x2, w1, tabs, bsz, seq, 1)
    loss, dx, dgf = _loss_head(x2, rep["final_norm_g"].reshape(1, D), loss_target.reshape(n, D), "loss_head")
    dx, g1, _ = _layer_bwd(dx, w1, sv1, bsz, seq, 1)
    grads1 = _layer_grads_to_full(g1)
    rides = None
    if blocks is not None:
        short = dict(w_branch="wb", w_out="wo", w_ffn_gate="wg", w_ffn_up="wu", w_ffn_down="wd")
        rides = {"scan": lambda g: [_grad_parts(grads1[k], k) for k in BIG],
                 "attn": lambda g: [_grad_parts(g[short[k]], k) for k in REST]}
    dx, g0, got = _layer_bwd(dx, w0, sv0, bsz, seq, 0, rides)
    grads0 = _layer_grads_to_full(g0)
    grads = {k: [grads0[k], grads1[k]] for k in grads0}
    grads["final_norm_g"] = dgf.reshape(D)
    parts = {}
    if blocks is not None:
        parts = {k: [None, p] for k, p in zip(BIG, got["scan"])}
        for k, p in zip(REST, got["attn"]):
            parts[k][0] = p
    return loss[0, 0], dx.reshape(bsz, seq, D), grads, parts


def kernel(x, positions, attn_norm_g, w_in, gate_b, q_norm_g, w_uq, kv_norm_g, w_ukv, sg_ln_g, sg_ln_b, sg_w, sg_b, rw_mu, rw_w0, rw_w2, rw_a0, rw_a2, rw_g2, rw_k_k, rw_k_a, rw_r_k, rw_ln_g, rw_ln_b, w_branch, w_out, ffn_norm_g, w_ffn_gate, w_ffn_up, w_ffn_down, final_norm_g, loss_target, m_attn_norm_g, m_w_in, m_gate_b, m_q_norm_g, m_w_uq, m_kv_norm_g, m_w_ukv, m_sg_ln_g, m_sg_ln_b, m_sg_w, m_sg_b, m_rw_mu, m_rw_w0, m_rw_w2, m_rw_a0, m_rw_a2, m_rw_g2, m_rw_k_k, m_rw_k_a, m_rw_r_k, m_rw_ln_g, m_rw_ln_b, m_w_branch, m_w_out, m_ffn_norm_g, m_w_ffn_gate, m_w_ffn_up, m_w_ffn_down, m_final_norm_g, v_attn_norm_g, v_w_in, v_gate_b, v_q_norm_g, v_w_uq, v_kv_norm_g, v_w_ukv, v_sg_ln_g, v_sg_ln_b, v_sg_w, v_sg_b, v_rw_mu, v_rw_w0, v_rw_w2, v_rw_a0, v_rw_a2, v_rw_g2, v_rw_k_k, v_rw_k_a, v_rw_r_k, v_rw_ln_g, v_rw_ln_b, v_w_branch, v_w_out, v_ffn_norm_g, v_w_ffn_gate, v_w_ffn_up, v_w_ffn_down, v_final_norm_g):
    args = locals()
    wts = {k: args[k] for k in WEIGHTS}
    mom_m = {k: args["m_" + k] for k in WEIGHTS}
    mom_v = {k: args["v_" + k] for k in WEIGHTS}
    shapes = {k: wts[k].shape for k in WEIGHTS}
    blocks = {k: wts[k].astype(bf16) for k in BIG}
    got = _exchange([blocks["w_in"][0], _pack128(wts, SMALL_BF, bf16), _pack128(wts, GATHER_F32, f32)], [], "gather_first")
    small = {**_unpack128(got[1], shapes, SMALL_BF, lead=1), **_unpack128(got[2], shapes, GATHER_F32, lead=1)}
    full = {k: list(_join_blocks(small[k], SHARDED[k])) for k in SMALL}
    full["w_in"] = [_join_blocks(got[0], SHARDED["w_in"] - 1), None]
    full.update({k: [None, None] for k in REST})
    rep = {k: wts[k] for k in REPLICATED}
    loss, grad_x, grads, parts = _local_step(x, positions, full, rep, loss_target, blocks)
    loss = lax.psum(loss, ("x", "y", "c"))
    both = {k: (jnp.stack(g) if isinstance(g, list) else g) for k, g in grads.items()}
    split = {k: _split_blocks(both[k], SHARDED[k]) for k in SMALL}
    rep_parts, parts["w_in"][0], small_parts = _exchange(
        [_pack128(both, REPLICATED, f32)], [_grad_parts(grads["w_in"][0], "w_in"), _pack128(split, SMALL, f32, lead=1)],
        "exchange_last")
    gw, delta, new_m, new_v = {}, {}, {}, {}
    for k in BIG:
        three = lambda a, k=k: a.reshape(a.shape[0], -1, shapes[k][-1])
        res = _adamw(three(wts[k]), [three(p) for p in parts[k]], three(mom_m[k]), three(mom_v[k]), f"adamw_{k}")
        gw[k], delta[k], new_m[k], new_v[k] = (t.reshape(shapes[k]) for t in res)
    for names, got, tag in ((SMALL, small_parts, "small"), (REPLICATED, rep_parts, "replicated")):
        pk = lambda dct: _pack128(dct, names, f32)[None]
        res = _adamw(pk(wts), [got], pk(mom_m), pk(mom_v), f"adamw_{tag}")
        for dst, t in zip((gw, delta, new_m, new_v), res):
            dst.update(_unpack128(t[0], shapes, names))
    return (loss, grad_x, *[gw[k] for k in WEIGHTS], *[delta[k] for k in WEIGHTS], *[new_m[k] for k in WEIGHTS],
            *[new_v[k] for k in WEIGHTS])
```

```python
import functools

import jax
import jax.numpy as jnp
from jax import lax
from jax.experimental import pallas as pl
from jax.experimental.pallas import tpu as pltpu

f32 = jnp.float32
bf16 = jnp.bfloat16
HI = lax.Precision.HIGHEST
NN, NT, TN = ((1,), (0,)), ((1,), (1,)), ((0,), (0,))

N_DEV = 8
D = 1024
HEADS = 8
Q_LORA, KV_LORA, QK_NOPE, QK_ROPE, V_HEAD = 384, 256, 64, 32, 64
SG_DIM, SG_CHUNK, SG_GROUPS = 512, 128, 8
RW_DIM, RW_HEAD, LORA = 512, 64, 64
D_FF = 2816
N_IN = 6688
NORM_EPS, LN_EPS, GN_EPS = 1e-6, 1e-5, 64e-5
ATT_SCALE = (QK_NOPE + QK_ROPE) ** -0.5
P_W = 7168
O_GATE, O_SG, O_RW, O_MLA = 0, 3072, 4096, 6144
RW_W = 2048
MLA_W = 768
O_CKV, O_SLAB, O_Q = O_MLA, O_MLA + 256, O_MLA + 384
CHUNK = 128
VMEM_LIMIT = 56 * 1024 * 1024

B1, B2, LR, EPS, WD, STEP = 0.9, 0.999, 0.001, 1e-8, 0.01, 10


def _pc(body, *, name, out_shape, grid=(), in_specs=(), out_specs=(), scratch=(), sem=None, ride=None):
    params = pltpu.CompilerParams(dimension_semantics=sem, vmem_limit_bytes=VMEM_LIMIT)
    if ride is None:
        return pl.pallas_call(body, out_shape=out_shape, grid=grid, in_specs=in_specs, out_specs=out_specs,
                              scratch_shapes=scratch, compiler_params=params, name=name, interpret=False)
    gathers, scatters = ride
    moved = list(gathers) + list(scatters)
    ng, nx = len(gathers), len(moved)
    single = not isinstance(out_shape, (list, tuple))
    outs = [out_shape] if single else list(out_shape)
    ospecs = [out_specs] if single else list(out_specs)
    n_in, n_out, n_scr = len(in_specs), len(outs), len(scratch)
    per = N_DEV - 1

    def riding(*refs):
        ins, xin = refs[:n_in], refs[n_in:n_in + nx]
        outs_r, xout = refs[n_in + nx:n_in + nx + n_out], refs[n_in + nx + n_out:n_in + 2 * nx + n_out]
        own = refs[n_in + 2 * nx + n_out:n_in + 2 * nx + n_out + n_scr]
        send_sems, recv_sems, local_sems = refs[n_in + 2 * nx + n_out + n_scr:]

        def copies():
            me, peers = _peers()
            cps = []
            for a in range(nx):
                whole = a < ng
                cps.append(pltpu.make_async_copy(xin[a] if whole else xin[a].at[me], xout[a].at[me], local_sems.at[a]))
                for k, peer in enumerate(peers):
                    dev = 4 * peer[0] + 2 * peer[1] + peer[2]
                    cps.append(pltpu.make_async_remote_copy(
                        src_ref=xin[a] if whole else xin[a].at[dev], dst_ref=xout[a].at[me],
                        send_sem=send_sems.at[a * per + k], recv_sem=recv_sems.at[a * per + k], device_id=peer,
                        device_id_type=pl.DeviceIdType.MESH))
            return cps

        if not grid:
            for cp in copies():
                cp.start()
            body(*ins, *outs_r, *own)
            for cp in copies():
                cp.wait()
            return
        ids = [pl.program_id(a) for a in range(len(grid))]
        first = functools.reduce(jnp.logical_and, [i == 0 for i in ids])
        last = functools.reduce(jnp.logical_and, [i == g - 1 for i, g in zip(ids, grid)])

        @pl.when(first)
        def _():
            for cp in copies():
                cp.start()

        body(*ins, *outs_r, *own)

        @pl.when(last)
        def _():
            for cp in copies():
                cp.wait()

    anyspec = pl.BlockSpec(memory_space=pl.ANY)
    call = pl.pallas_call(
        riding, grid=grid, in_specs=list(in_specs) + [anyspec] * nx, out_specs=ospecs + [anyspec] * nx,
        out_shape=outs + [_sds((N_DEV,) + a.shape, a.dtype) for a in gathers] + [_sds(a.shape, a.dtype) for a in scatters],
        scratch_shapes=list(scratch) + [pltpu.SemaphoreType.DMA((nx * per,)), pltpu.SemaphoreType.DMA((nx * per,)),
                                        pltpu.SemaphoreType.DMA((nx,))],
        compiler_params=params, name=name, interpret=False)

    def run(*args):
        res = call(*args, *moved)
        own = res[0] if single else list(res[:n_out])
        return own, list(res[n_out:])

    return run


def _sds(shape, dtype=f32):
    return jax.ShapeDtypeStruct(tuple(shape), dtype)


def _dot(a, b, dims, precision=None):
    return lax.dot_general(a, b, (dims, ((), ())), preferred_element_type=f32, precision=precision)


def _bdot(a, b, dims=NN):
    return _dot(a.astype(bf16), b.astype(bf16), dims)


@jax.custom_vjp
def _mm(a, w):
    return _bdot(a, w, NN)


def _mm_fwd(a, w):
    return _bdot(a, w, NN), (a, w)


def _mm_bwd(res, g):
    a, w = res
    return _bdot(g, w, NT), _bdot(a, g, TN)


_mm.defvjp(_mm_fwd, _mm_bwd)


@jax.custom_vjp
def _mm_nt(a, b):
    return _bdot(a, b, NT)


def _mm_nt_fwd(a, b):
    return _bdot(a, b, NT), (a, b)


def _mm_nt_bwd(res, g):
    a, b = res
    return _bdot(g, b, NN), _bdot(g, a, TN)


_mm_nt.defvjp(_mm_nt_fwd, _mm_nt_bwd)


@jax.custom_vjp
def _mm_tn(a, b):
    return _bdot(a, b, TN)


def _mm_tn_fwd(a, b):
    return _bdot(a, b, TN), (a, b)


def _mm_tn_bwd(res, g):
    a, b = res
    return _bdot(b, g, NT), _bdot(a, g, NN)


_mm_tn.defvjp(_mm_tn_fwd, _mm_tn_bwd)


def _rms(x, g):
    return x * lax.rsqrt(jnp.mean(x * x, axis=-1, keepdims=True) + NORM_EPS) * g


def _sigmoid(x):
    return 1.0 / (1.0 + jnp.exp(-x))


def _gelu(x):
    return 0.5 * x * (1.0 + jnp.tanh(0.7978845608028654 * (x + 0.044715 * x * x * x)))


def _softplus(x):
    return jnp.maximum(x, 0.0) + jnp.log(1.0 + jnp.exp(-jnp.abs(x)))


@jax.custom_vjp
def _group_sum(x):
    w = x.shape[-1]
    lane = lax.broadcasted_iota(jnp.int32, x.shape, 1) % RW_HEAD
    s = x
    for sh in (32, 16, 8, 4, 2, 1):
        s = s + jnp.where(lane < RW_HEAD - sh, pltpu.roll(s, w - sh, 1), pltpu.roll(s, RW_HEAD - sh, 1))
    return s


_group_sum.defvjp(lambda x: (_group_sum(x), None), lambda _, g: (_group_sum(g),))


@jax.custom_vjp
def _swap(x):
    w = x.shape[-1]
    lane = lax.broadcasted_iota(jnp.int32, x.shape, 1) % 128
    lo = (lane >= 64) & (lane < 80)
    hi = (lane >= 80) & (lane < 96)
    return jnp.where(lo, pltpu.roll(x, w - 16, 1), jnp.where(hi, pltpu.roll(x, 16, 1), 0.0))


_swap.defvjp(lambda x: (_swap(x), None), lambda _, g: (_swap(g),))


def _rope(x, c, s):
    return x * c + _swap(x) * s


def _row_spec(tm, width, blk):
    return pl.BlockSpec((tm, width), lambda i, blk=blk: (i, blk))


def _full_spec(a):
    nd = a.ndim
    return pl.BlockSpec(a.shape, lambda i, nd=nd: (0,) * nd)


def _rowwise_fwd(name, f, rows, weights, outs, tm):
    n = rows[0][0].shape[0]
    nr, nw = len(rows), len(weights)

    def body(*refs):
        vals = [r[...].astype(f32) for r in refs[:nr + nw]]
        res = f(*vals)
        for o_ref, o in zip(refs[nr + nw:], res):
            o_ref[...] = o.astype(o_ref.dtype)

    return _pc(
        body, name=name, grid=(n // tm,),
        in_specs=[_row_spec(tm, w, b) for _, w, b in rows] + [_full_spec(w) for w in weights],
        out_specs=[_row_spec(tm, w, 0) for w, _ in outs],
        out_shape=[_sds((n, w), dt) for w, dt in outs], sem=("parallel",),
    )(*[a for a, _, _ in rows], *weights)


def _rowwise_bwd(name, f, rows, weights, cots, tm, drows):
    n = rows[0][0].shape[0]
    nr, nw, nc = len(rows), len(weights), len(cots)
    want = [k for k, dt in enumerate(drows) if dt is not None]

    def body(*refs):
        vals = [r[...].astype(f32) for r in refs[:nr + nw]]
        cot = tuple(r[...].astype(f32) for r in refs[nr + nw:nr + nw + nc])
        _, vjp = jax.vjp(f, *vals)
        grads = vjp(cot)
        outs = refs[nr + nw + nc:]
        for o_ref, k in zip(outs[:len(want)], want):
            o_ref[...] = grads[k].astype(o_ref.dtype)
        first = pl.program_id(0) == 0
        for o_ref, g in zip(outs[len(want):], grads[nr:]):
            @pl.when(first)
            def _(o_ref=o_ref, g=g):
                o_ref[...] = g

            @pl.when(jnp.logical_not(first))
            def _(o_ref=o_ref, g=g):
                o_ref[...] += g

    res = _pc(
        body, name=name, grid=(n // tm,),
        in_specs=[_row_spec(tm, w, b) for _, w, b in rows] + [_full_spec(w) for w in weights]
        + [_row_spec(tm, w, b) for _, w, b in cots],
        out_specs=[_row_spec(tm, rows[k][1], 0) for k in want] + [_full_spec(w) for w in weights],
        out_shape=[_sds((n, rows[k][1]), drows[k]) for k in want] + [_sds(w.shape) for w in weights],
        sem=("arbitrary",),
    )(*[a for a, _, _ in rows], *weights, *[a for a, _, _ in cots])
    return res[:len(want)], res[len(want):]


def _inproj_fwd(x2, g, w, name, ride=None):
    n = x2.shape[0]
    tm, tn = min(512, n), 512

    def body(x_ref, g_ref, w_ref, p_ref, h_ref):
        @pl.when(pl.program_id(1) == 0)
        def _():
            h_ref[...] = _rms(x_ref[...], g_ref[...]).astype(bf16)

        p_ref[...] = jnp.dot(h_ref[...], w_ref[...], preferred_element_type=f32)

    return _pc(
        body, name=name, grid=(n // tm, P_W // tn),
        in_specs=[pl.BlockSpec((tm, D), lambda i, j: (i, 0)), pl.BlockSpec((1, D), lambda i, j: (0, 0)),
                  pl.BlockSpec((D, tn), lambda i, j: (0, j))],
        out_specs=[pl.BlockSpec((tm, tn), lambda i, j: (i, j)), pl.BlockSpec((tm, D), lambda i, j: (i, 0))],
        out_shape=[_sds((n, P_W)), _sds((n, D), bf16)], sem=("parallel", "arbitrary"), ride=ride,
    )(x2, g, w)


def _norm_matmul_bwd(dy, w, x2, g, dres, name, ride=None):
    n, k = dy.shape
    tm = min(512, n)
    tk = 1024 if k % 1024 == 0 else 1408
    nk = k // tk

    def body(dy_ref, w_ref, x_ref, g_ref, dr_ref, dx_ref, dg_ref, acc):
        i, j = pl.program_id(0), pl.program_id(1)

        @pl.when(j == 0)
        def _():
            acc[...] = jnp.zeros_like(acc)

        @pl.when((i == 0) & (j == 0))
        def _():
            dg_ref[...] = jnp.zeros_like(dg_ref)

        acc[...] += _dot(dy_ref[...], w_ref[...], NT)

        @pl.when(j == nk - 1)
        def _():
            _, vjp = jax.vjp(_rms, x_ref[...], g_ref[...])
            dx, dg = vjp(acc[...])
            dx_ref[...] = dr_ref[...] + dx
            dg_ref[...] += dg

    return _pc(
        body, name=name, grid=(n // tm, nk),
        in_specs=[pl.BlockSpec((tm, tk), lambda i, j: (i, j)), pl.BlockSpec((D, tk), lambda i, j: (0, j)),
                  pl.BlockSpec((tm, D), lambda i, j: (i, 0)), pl.BlockSpec((1, D), lambda i, j: (0, 0)),
                  pl.BlockSpec((tm, D), lambda i, j: (i, 0))],
        out_specs=[pl.BlockSpec((tm, D), lambda i, j: (i, 0)), pl.BlockSpec((1, D), lambda i, j: (0, 0))],
        out_shape=[_sds((n, D)), _sds((1, D))], scratch=[pltpu.VMEM((tm, D), f32)], sem=("arbitrary", "arbitrary"),
        ride=ride,
    )(dy, w, x2, g, dres)


def _matmul_tn(a, g, name):
    n, k = a.shape
    m = g.shape[1]
    tr = min(512, n)
    tk = k if k <= 1024 else 1408
    tn = m if m <= 1024 else (512 if m % 512 == 0 else 1408)
    nr = n // tr

    def body(a_ref, g_ref, o_ref):
        @pl.when(pl.program_id(2) == 0)
        def _():
            o_ref[...] = jnp.zeros_like(o_ref)

        o_ref[...] += _dot(a_ref[...], g_ref[...], TN)

    return _pc(
        body, name=name, grid=(k // tk, m // tn, nr),
        in_specs=[pl.BlockSpec((tr, tk), lambda i, j, r: (r, i)), pl.BlockSpec((tr, tn), lambda i, j, r: (r, j))],
        out_specs=pl.BlockSpec((tk, tn), lambda i, j, r: (i, j)),
        out_shape=_sds((k, m)), sem=("parallel", "parallel", "arbitrary"),
    )(a, g)


def _f_mla_proj(ckv, slab, pq, c, s, qg, kg, wq, wk, wv):
    c8, s8 = jnp.concatenate([c] * HEADS, axis=1), jnp.concatenate([s] * HEADS, axis=1)
    q = _rope(_mm(_rms(pq, qg), wq), c8, s8)
    cn = _rms(ckv, kg)
    k = _mm(cn, wk) + jnp.concatenate([_rope(slab, c, s)] * HEADS, axis=1)
    return q, k, _mm(cn, wv)


def _attn_fwd(q, k, v, bsz, seq, name, ride=None):
    n = q.shape[0]
    tq = min(256, seq)
    nq = seq // tq

    def body(q_ref, k_ref, v_ref, o_ref):
        lane = lax.broadcasted_iota(jnp.int32, (tq, 128), 1) < 64
        vv = v_ref[...]
        two = range(2)
        s = [_dot(q_ref[:, h * 128:(h + 1) * 128], k_ref[:, h * 128:(h + 1) * 128], NT) * ATT_SCALE for h in two]
        e = [jnp.exp(s[h] - jnp.max(s[h], axis=-1, keepdims=True)) for h in two]
        p = [(e[h] / jnp.sum(e[h], axis=-1, keepdims=True)).astype(bf16) for h in two]
        outs = [_dot(p[h], vv, NN) for h in two]
        o_ref[...] = jnp.where(lane, outs[0], outs[1])

    return _pc(
        body, name=name, grid=(bsz, HEADS // 2, nq),
        in_specs=[pl.BlockSpec((tq, 256), lambda b, h, i: (b * nq + i, h)),
                  pl.BlockSpec((seq, 256), lambda b, h, i: (b, h)),
                  pl.BlockSpec((seq, 128), lambda b, h, i: (b, h))],
        out_specs=pl.BlockSpec((tq, 128), lambda b, h, i: (b * nq + i, h)),
        out_shape=_sds((n, HEADS * V_HEAD)), sem=("parallel", "parallel", "parallel"), ride=ride,
    )(q, k, v)


def _attn_bwd(q, k, v, o, do, bsz, seq, name, ride=None):
    n = q.shape[0]
    tq = min(256, seq)
    nq = seq // tq

    def body(q_ref, k_ref, v_ref, o_ref, do_ref, dq_ref, dk_ref, dv_ref):
        @pl.when(pl.program_id(2) == 0)
        def _():
            dk_ref[...] = jnp.zeros_like(dk_ref)
            dv_ref[...] = jnp.zeros_like(dv_ref)

        lane = lax.broadcasted_iota(jnp.int32, (tq, 128), 1) < 64
        vv = v_ref[...]
        for h in range(2):
            qh, kh = q_ref[:, h * 128:(h + 1) * 128], k_ref[:, h * 128:(h + 1) * 128]
            s = _dot(qh, kh, NT) * ATT_SCALE
            e = jnp.exp(s - jnp.max(s, axis=-1, keepdims=True))
            p = e / jnp.sum(e, axis=-1, keepdims=True)
            doh = jnp.where(lane if h == 0 else jnp.logical_not(lane), do_ref[...], 0.0)
            delta = jnp.sum(doh * o_ref[...], axis=-1, keepdims=True)
            dob = doh.astype(bf16)
            dp = _dot(dob, vv, NT)
            ds = (p * (dp - delta) * ATT_SCALE).astype(bf16)
            dq_ref[:, h * 128:(h + 1) * 128] = _dot(ds, kh, NN)
            dk_ref[:, h * 128:(h + 1) * 128] += _dot(ds, qh, TN)
            dv_ref[...] += _dot(p.astype(bf16), dob, TN)

    return _pc(
        body, name=name, grid=(bsz, HEADS // 2, nq),
        in_specs=[pl.BlockSpec((tq, 256), lambda b, h, i: (b * nq + i, h)),
                  pl.BlockSpec((seq, 256), lambda b, h, i: (b, h)),
                  pl.BlockSpec((seq, 128), lambda b, h, i: (b, h)),
                  pl.BlockSpec((tq, 128), lambda b, h, i: (b * nq + i, h)),
                  pl.BlockSpec((tq, 128), lambda b, h, i: (b * nq + i, h))],
        out_specs=[pl.BlockSpec((tq, 256), lambda b, h, i: (b * nq + i, h)),
                   pl.BlockSpec((seq, 256), lambda b, h, i: (b, h)),
                   pl.BlockSpec((seq, 128), lambda b, h, i: (b, h))],
        out_shape=[_sds((n, HEADS * 128)), _sds((n, HEADS * 128)), _sds((n, HEADS * V_HEAD))],
        sem=("parallel", "parallel", "arbitrary"), ride=ride,
    )(q, k, v, o, do)


def _f_sg(pu, pv, lg, lb, bias, *ws):
    u, vv = _gelu(pu), _gelu(pv)
    mu = jnp.mean(vv, axis=-1, keepdims=True)
    d = vv - mu
    vv = d * lax.rsqrt(jnp.mean(d * d, axis=-1, keepdims=True) + LN_EPS) * lg + lb
    group = lax.broadcasted_iota(jnp.int32, (SG_CHUNK, SG_DIM), 1) // (SG_DIM // SG_GROUPS)
    mixed = bias
    for k, w in enumerate(ws):
        mixed = mixed + jnp.where(group == k, _mm(w, vv), 0.0)
    return (u * mixed,)


def _shift_mean(a, prev_row, next_row):
    t = a.shape[0]
    row = lax.broadcasted_iota(jnp.int32, a.shape, 0)
    prev = jnp.where(row == 0, prev_row, pltpu.roll(a, 1, 0))
    nxt = jnp.where(row == t - 1, next_row, pltpu.roll(a, t - 1, 0))
    return 0.5 * (prev + nxt)


def _halo_specs(tm, width, blk, nblk8):
    h = tm // 8
    return [pl.BlockSpec((tm, width), lambda i: (i, blk)),
            pl.BlockSpec((8, width), lambda i: (jnp.maximum(i * h - 1, 0), blk)),
            pl.BlockSpec((8, width), lambda i: (jnp.minimum((i + 1) * h, nblk8 - 1), blk))]


def _edge_rows(i, tm, seq, pv_ref, nx_ref, scale=None):
    first = (i * tm) % seq == 0
    last = ((i + 1) * tm) % seq == 0
    pv, nx = pv_ref[7:8, :], nx_ref[0:1, :]
    if scale is not None:
        pv, nx = pv * scale, nx * scale
    return jnp.where(first, 0.0, pv), jnp.where(last, 0.0, nx)


def _shift_fwd(p, mu, seq, name):
    n = p.shape[0]
    tm = min(256, seq)
    blk = O_RW // RW_W

    def body(x_ref, pv_ref, nx_ref, mu_ref, z_ref):
        x = x_ref[...]
        pv, nx = _edge_rows(pl.program_id(0), tm, seq, pv_ref, nx_ref)
        z_ref[...] = x + mu_ref[...] * (_shift_mean(x, pv, nx) - x)

    return _pc(
        body, name=name, grid=(n // tm,),
        in_specs=_halo_specs(tm, RW_W, blk, n // 8) + [pl.BlockSpec((1, RW_W), lambda i: (0, 0))],
        out_specs=pl.BlockSpec((tm, RW_W), lambda i: (i, 0)), out_shape=_sds((n, RW_W)), sem=("parallel",),
    )(p, p, p, mu)


def _shift_bwd(dz, p, mu, seq, name):
    n = p.shape[0]
    tm = min(256, seq)
    blk = O_RW // RW_W

    def body(dz_ref, dpv_ref, dnx_ref, x_ref, pv_ref, nx_ref, mu_ref, dx_ref, dmu_ref):
        i = pl.program_id(0)
        mu_v = mu_ref[...]
        dzv = dz_ref[...]
        m = dzv * mu_v
        mpv, mnx = _edge_rows(i, tm, seq, dpv_ref, dnx_ref, mu_v)
        dx_ref[...] = (dzv - m + _shift_mean(m, mpv, mnx)).astype(dx_ref.dtype)
        x = x_ref[...]
        pv, nx = _edge_rows(i, tm, seq, pv_ref, nx_ref)
        part = jnp.sum(dzv * (_shift_mean(x, pv, nx) - x), axis=0, keepdims=True)

        @pl.when(i == 0)
        def _():
            dmu_ref[...] = part

        @pl.when(i != 0)
        def _():
            dmu_ref[...] += part

    return _pc(
        body, name=name, grid=(n // tm,),
        in_specs=_halo_specs(tm, RW_W, 0, n // 8) + _halo_specs(tm, RW_W, blk, n // 8)
        + [pl.BlockSpec((1, RW_W), lambda i: (0, 0))],
        out_specs=[pl.BlockSpec((tm, RW_W), lambda i: (i, 0)), pl.BlockSpec((1, RW_W), lambda i: (0, 0))],
        out_shape=[_sds((n, RW_W), bf16), _sds((1, RW_W))], sem=("arbitrary",),
    )(dz, dz, dz, p, p, p, mu)


def _f_rw_pre(k, wl, al, gl, w0, a0, w2, a2, g2, k_k, k_a):
    w = w0 + _mm(jnp.tanh(wl), w2)
    lw = -jnp.exp(-_softplus(-w) - 0.5)
    a = _sigmoid(a0 + _mm(al, a2))
    g = _mm(_sigmoid(gl), g2)
    kkr = k * k_k
    kk = kkr / jnp.maximum(jnp.sqrt(_group_sum(kkr * kkr)), 1e-12)
    two = lambda t: jnp.concatenate([t, t], axis=1)
    kd = two(k) * (1.0 + (a - 1.0) * two(k_a))
    bd = two(kk) * a
    return lw, kd, kk, bd, g


def _f_rw_post(y0, y1, r, v, kd0, kd1, g, r_k, ln_g, ln_b):
    y = y0 + y1
    mean = _group_sum(y) * (1.0 / RW_HEAD)
    d = y - mean
    var = _group_sum(d * d) * (1.0 / RW_HEAD)
    yn = d * lax.rsqrt(var + GN_EPS) * ln_g + ln_b
    bonus = _group_sum(r * (kd0 + kd1) * r_k)
    return ((yn + bonus * v) * g,)


@jax.custom_vjp
def _tri_inv(mats):
    c = mats[0].shape[0]
    row = lax.broadcasted_iota(jnp.int32, (c, c), 0)
    col = lax.broadcasted_iota(jnp.int32, (c, c), 1)
    eye = (row == col).astype(f32)
    blk = lambda b: (row // b) == (col // b)
    ld = [jnp.where(blk(8), a, 0.0) for a in mats]
    l2 = [_bdot(x, x) for x in ld]
    l4 = [_bdot(x, x) for x in l2]
    t = [_bdot(eye - x, eye + y) for x, y in zip(ld, l2)]
    t = [_bdot(x, eye + y) for x, y in zip(t, l4)]
    b = 8
    while b < c:
        sub = blk(2 * b) & jnp.logical_not(blk(b))
        p = [_bdot(x, jnp.where(sub, a, 0.0)) for x, a in zip(t, mats)]
        t = [x - _bdot(y, x) for x, y in zip(t, p)]
        b *= 2
    return tuple(t)


def _tri_inv_fwd(mats):
    t = _tri_inv(mats)
    return t, t


def _tri_inv_bwd(ts, gs):
    p = [_bdot(t, g, TN) for t, g in zip(ts, gs)]
    return (tuple(-_bdot(x, t, NT) for x, t in zip(p, ts)),)


_tri_inv.defvjp(_tri_inv_fwd, _tri_inv_bwd)


def _split3(x):
    h = x.astype(bf16)
    r = x - h.astype(f32)
    m = r.astype(bf16)
    return h, m, (r - m.astype(f32)).astype(bf16)


@jax.custom_vjp
def _mask_mm(mask, x):
    mb = mask.astype(bf16)
    return _dot(jnp.concatenate([mb, mb, mb], axis=1), jnp.concatenate(_split3(x), axis=0), NN)


def _mask_mm_bwd(mask, g):
    mb = mask.astype(bf16)
    return jnp.zeros_like(mask), _dot(jnp.concatenate([mb, mb, mb], axis=0), jnp.concatenate(_split3(g), axis=0), TN)


_mask_mm.defvjp(lambda mask, x: (_mask_mm(mask, x), mask), _mask_mm_bwd)


@jax.custom_vjp
def _split_lanes(x):
    h = x.shape[1] // 2
    return x[:, :h], x[:, h:]


_split_lanes.defvjp(lambda x: (_split_lanes(x), None), lambda _, g: (jnp.concatenate(g, axis=1),))


def _scan_chunk(s0, r, v, kk, lw, kd, bd, rev):
    n = len(r)
    each = range(n)
    c = r[0].shape[0]
    row = lax.broadcasted_iota(jnp.int32, (c, 2 * c), 0)
    col = lax.broadcasted_iota(jnp.int32, (c, 2 * c), 1) % c
    ahead = jnp.where(rev, col - row, row - col)
    before = ahead > 0
    incl = ahead >= 0
    lane = lax.broadcasted_iota(jnp.int32, (1, 128), 1)
    m0 = (lane < 64).astype(f32)
    heads = lambda t: jnp.concatenate([t * m0, t * (1.0 - m0)], axis=0)
    bd_mask = ((lax.broadcasted_iota(jnp.int32, (128, 128), 0) // 64)
               == (lax.broadcasted_iota(jnp.int32, (128, 128), 1) // 64)).astype(f32)
    tot = [jnp.sum(lw[i], axis=0, keepdims=True) for i in each]
    row1 = lax.broadcasted_iota(jnp.int32, (c, c), 0)
    col1 = lax.broadcasted_iota(jnp.int32, (c, c), 1)
    upto = (jnp.where(rev, col1 - row1, row1 - col1) >= 0).astype(f32)
    lp = [_mask_mm(upto, lw[i]) - 0.5 * tot[i] for i in each]
    eg = [jnp.exp(lp[i]) for i in each]
    ieg = [jnp.exp(-lp[i]) for i in each]
    rt = [r[i] * eg[i] for i in each]
    kt = [kd[i] * ieg[i] for i in each]
    bt = [bd[i] * ieg[i] for i in each]
    at = [kk[i] * jnp.exp(lp[i] - lw[i]) for i in each]
    etot = [jnp.exp(0.5 * tot[i]) for i in each]
    si = [s0[i] * etot[i] for i in each]
    bth = [heads(bt[i]) for i in each]
    kth = [heads(kt[i]) for i in each]
    vh = [heads(v[i]) for i in each]
    a_ab = [jnp.where(before, _mm_nt(at[i], bth[i]), 0.0) for i in each]
    a_ak = [jnp.where(before, _mm_nt(at[i], kth[i]), 0.0) for i in each]
    a_rb = [jnp.where(incl, _mm_nt(rt[i], bth[i]), 0.0) for i in each]
    a_rk = [jnp.where(incl, _mm_nt(rt[i], kth[i]), 0.0) for i in each]
    halves = [_split_lanes(a_ab[i]) for i in each]
    inv = _tri_inv(tuple(m for pair in halves for m in pair))
    t = [jnp.concatenate([inv[2 * i], inv[2 * i + 1]], axis=1) for i in each]
    x0 = [_mm_nt(at[i], si[i]) for i in each]
    x = [x0[i] + _mm(a_ak[i], vh[i]) for i in each]
    u = [-_mm(t[i], heads(x[i])) for i in each]
    y0 = [_mm_nt(rt[i], si[i]) for i in each]
    y = [y0[i] + _mm(jnp.concatenate([a_rb[i], a_rk[i]], axis=1), jnp.concatenate([heads(u[i]), vh[i]], axis=0))
         for i in each]
    ds = [_mm_tn(jnp.concatenate([u[i], v[i]], axis=0), jnp.concatenate([bt[i], kt[i]], axis=0)) for i in each]
    se = [(si[i] + ds[i] * bd_mask) * etot[i] for i in each]
    return tuple(y), tuple(se)


PAIRS = HEADS // 2


def _scan_specs(nc, bsz, flip=False):
    def cc(d, c):
        c = nc - 1 - c if flip else c
        return jnp.where(d == 0, c, nc - 1 - c)

    rowblk = lambda d, b, c: b * nc + cc(d, c)
    zspec = lambda blk: pl.BlockSpec((CHUNK, RW_DIM), lambda d, b, c: (rowblk(d, b, c), blk))
    dspec = pl.BlockSpec((CHUNK, RW_DIM), lambda d, b, c: (rowblk(d, b, c), d))
    yspec = pl.BlockSpec((None, CHUNK, RW_DIM), lambda d, b, c: (d, rowblk(d, b, c), 0))
    sspec = pl.BlockSpec((None, PAIRS, 128, 128), lambda d, b, c: ((d * bsz + b) * nc + cc(d, c), 0, 0, 0))
    return zspec, dspec, yspec, sspec


def _scan_fwd(z, lw, kd, kk, bd, bsz, seq, name, ride=None):
    n = z.shape[0]
    nc = seq // CHUNK
    zspec, dspec, yspec, sspec = _scan_specs(nc, bsz)

    def body(r_ref, v_ref, kk_ref, lw_ref, kd_ref, bd_ref, y_ref, s_ref, st):
        @pl.when(pl.program_id(2) == 0)
        def _():
            st[...] = jnp.zeros_like(st)

        rev = pl.program_id(0) == 1
        lanes = [slice(h * 128, (h + 1) * 128) for h in range(PAIRS)]
        s0 = tuple(st[h] for h in range(PAIRS))
        ops = [tuple(ref[:, ln] for ln in lanes) for ref in (r_ref, v_ref, kk_ref, lw_ref, kd_ref, bd_ref)]
        y, se = _scan_chunk(s0, *ops, rev)
        for h, ln in enumerate(lanes):
            s_ref[h] = s0[h]
            y_ref[:, ln] = y[h]
            st[h] = se[h]

    return _pc(
        body, name=name, grid=(2, bsz, nc),
        in_specs=[zspec(0), zspec(2), zspec(0), dspec, dspec, dspec],
        out_specs=[yspec, sspec],
        out_shape=[_sds((2, n, RW_DIM)), _sds((2 * bsz * nc, PAIRS, 128, 128))],
        scratch=[pltpu.VMEM((PAIRS, 128, 128), f32)], sem=("parallel", "parallel", "arbitrary"), ride=ride,
    )(z, z, kk, lw, kd, bd)


def _scan_bwd(z, lw, kd, kk, bd, s_in, dy, bsz, seq, name, ride=None):
    n = z.shape[0]
    nc = seq // CHUNK
    zspec, dspec, yspec, sspec = _scan_specs(nc, bsz, flip=True)

    def body(r_ref, v_ref, kk_ref, lw_ref, kd_ref, bd_ref, s_ref, dy_ref,
             dr_ref, dv_ref, dkk_ref, dlw_ref, dkd_ref, dbd_ref, dst):
        @pl.when(pl.program_id(2) == 0)
        def _():
            dst[...] = jnp.zeros_like(dst)

        rev = pl.program_id(0) == 1
        lanes = [slice(h * 128, (h + 1) * 128) for h in range(PAIRS)]
        s0 = tuple(s_ref[h] for h in range(PAIRS))
        ops = [tuple(ref[:, ln] for ln in lanes) for ref in (r_ref, v_ref, kk_ref, lw_ref, kd_ref, bd_ref)]
        cot = (tuple(dy_ref[:, ln] for ln in lanes), tuple(dst[h] for h in range(PAIRS)))
        _, vjp = jax.vjp(functools.partial(_scan_chunk, rev=rev), s0, *ops)
        grads = vjp(cot)
        for h, ln in enumerate(lanes):
            dst[h] = grads[0][h]
            for o_ref, g in zip((dr_ref, dv_ref, dkk_ref, dlw_ref, dkd_ref, dbd_ref), grads[1:]):
                o_ref[:, ln] = g[h]

    return _pc(
        body, name=name, grid=(2, bsz, nc),
        in_specs=[zspec(0), zspec(2), zspec(0), dspec, dspec, dspec, sspec, zspec(0)],
        out_specs=[yspec, yspec, yspec, dspec, dspec, dspec],
        out_shape=[_sds((2, n, RW_DIM))] * 3 + [_sds((n, 2 * RW_DIM))] * 3,
        scratch=[pltpu.VMEM((PAIRS, 128, 128), f32)], sem=("parallel", "parallel", "arbitrary"), ride=ride,
    )(z, z, kk, lw, kd, bd, s_in, dy)


def _merge_fwd(x2, p, ya, yb, yc, gb, wb, wo, name):
    n = x2.shape[0]
    tm = min(256, n)

    def body(x_ref, pg_ref, ya_ref, yb_ref, yc_ref, gb_ref, wb_ref, wo_ref, o_ref):
        gates = _sigmoid(pg_ref[...] + gb_ref[...])
        merged = jnp.zeros((tm, D), f32)
        for k, y_ref in enumerate((ya_ref, yb_ref, yc_ref)):
            merged += gates[:, k * D:(k + 1) * D] * _bdot(y_ref[...], wb_ref[k])
        o_ref[...] = x_ref[...] + _bdot(merged, wo_ref[...])

    row = lambda w, b=0: pl.BlockSpec((tm, w), lambda i, b=b: (i, b))
    return _pc(
        body, name=name, grid=(n // tm,),
        in_specs=[row(D), row(3 * D, O_GATE // (3 * D)), row(512), row(512), row(512),
                  pl.BlockSpec((1, 3 * D), lambda i: (0, 0)), pl.BlockSpec((3, 512, D), lambda i: (0, 0, 0)),
                  pl.BlockSpec((D, D), lambda i: (0, 0))],
        out_specs=row(D), out_shape=_sds((n, D)), sem=("parallel",),
    )(x2, p, ya, yb, yc, gb, wb, wo)


def _merge_bwd(dx1, p, ya, yb, yc, gb, wb, wo, name):
    n = dx1.shape[0]
    tm = min(256, n)

    def body(dx_ref, pg_ref, ya_ref, yb_ref, yc_ref, gb_ref, wb_ref, wo_ref,
             dpg_ref, dya_ref, dyb_ref, dyc_ref, dt_ref, mg_ref, dgb_ref):
        gates = _sigmoid(pg_ref[...] + gb_ref[...])
        dmerged = _bdot(dx_ref[...], wo_ref[...], NT)
        merged = jnp.zeros((tm, D), f32)
        dpg = []
        for k, (y_ref, dy_ref) in enumerate(((ya_ref, dya_ref), (yb_ref, dyb_ref), (yc_ref, dyc_ref))):
            gk = gates[:, k * D:(k + 1) * D]
            tk = _bdot(y_ref[...], wb_ref[k])
            merged += gk * tk
            dpg.append(dmerged * tk * gk * (1.0 - gk))
            dtk = dmerged * gk
            dt_ref[:, k * D:(k + 1) * D] = dtk.astype(bf16)
            dy_ref[...] = _bdot(dtk, wb_ref[k], NT)
        dpg = jnp.concatenate(dpg, axis=1)
        dpg_ref[...] = dpg.astype(bf16)
        mg_ref[...] = merged.astype(bf16)
        part = jnp.sum(dpg, axis=0, keepdims=True)

        @pl.when(pl.program_id(0) == 0)
        def _():
            dgb_ref[...] = part

        @pl.when(pl.program_id(0) != 0)
        def _():
            dgb_ref[...] += part

    row = lambda w, b=0: pl.BlockSpec((tm, w), lambda i, b=b: (i, b))
    return _pc(
        body, name=name, grid=(n // tm,),
        in_specs=[row(D), row(3 * D, O_GATE // (3 * D)), row(512), row(512), row(512),
                  pl.BlockSpec((1, 3 * D), lambda i: (0, 0)), pl.BlockSpec((3, 512, D), lambda i: (0, 0, 0)),
                  pl.BlockSpec((D, D), lambda i: (0, 0))],
        out_specs=[row(3 * D), row(512), row(512), row(512), row(3 * D), row(D),
                   pl.BlockSpec((1, 3 * D), lambda i: (0, 0))],
        out_shape=[_sds((n, 3 * D), bf16), _sds((n, 512)), _sds((n, 512)), _sds((n, 512)), _sds((n, 3 * D), bf16),
                   _sds((n, D), bf16), _sds((1, 3 * D))],
        sem=("arbitrary",),
    )(dx1, p, ya, yb, yc, gb, wb, wo)


FF_T = 1408


def _ffn_fwd(x1, g, wg, wu, wd, name):
    n = x1.shape[0]
    tm = min(512, n)
    nf = D_FF // FF_T

    def body(x_ref, g_ref, wg_ref, wu_ref, wd_ref, o_ref, hs):
        j = pl.program_id(1)

        @pl.when(j == 0)
        def _():
            hs[...] = _rms(x_ref[...], g_ref[...]).astype(bf16)
            o_ref[...] = x_ref[...]

        a = _dot(hs[...], wg_ref[...], NN)
        b = _dot(hs[...], wu_ref[...], NN)
        o_ref[...] += _bdot(a * _sigmoid(a) * b, wd_ref[...])

    return _pc(
        body, name=name, grid=(n // tm, nf),
        in_specs=[pl.BlockSpec((tm, D), lambda i, j: (i, 0)), pl.BlockSpec((1, D), lambda i, j: (0, 0)),
                  pl.BlockSpec((D, FF_T), lambda i, j: (0, j)), pl.BlockSpec((D, FF_T), lambda i, j: (0, j)),
                  pl.BlockSpec((FF_T, D), lambda i, j: (j, 0))],
        out_specs=pl.BlockSpec((tm, D), lambda i, j: (i, 0)), out_shape=_sds((n, D)),
        scratch=[pltpu.VMEM((tm, D), bf16)], sem=("parallel", "arbitrary"),
    )(x1, g, wg, wu, wd)


def _ffn_bwd(dx2, x1, g, wg, wu, wd, name):
    n = x1.shape[0]
    tm = min(512, n)
    nf = D_FF // FF_T

    def body(dx_ref, x_ref, g_ref, wg_ref, wu_ref, wd_ref, dx1_ref, dg_ref, h_ref, da_ref, db_ref, hm_ref, acc):
        i, j = pl.program_id(0), pl.program_id(1)

        @pl.when(j == 0)
        def _():
            h_ref[...] = _rms(x_ref[...], g_ref[...]).astype(bf16)
            acc[...] = jnp.zeros_like(acc)

        @pl.when((i == 0) & (j == 0))
        def _():
            dg_ref[...] = jnp.zeros_like(dg_ref)

        h = h_ref[...]
        a = _dot(h, wg_ref[...], NN)
        b = _dot(h, wu_ref[...], NN)
        sg = _sigmoid(a)
        s = a * sg
        dhm = _bdot(dx_ref[...], wd_ref[...], NT)
        da = (dhm * b * (sg * (1.0 + a * (1.0 - sg)))).astype(bf16)
        db = (dhm * s).astype(bf16)
        da_ref[...] = da
        db_ref[...] = db
        hm_ref[...] = (s * b).astype(bf16)
        acc[...] += _dot(da, wg_ref[...], NT) + _dot(db, wu_ref[...], NT)

        @pl.when(j == nf - 1)
        def _():
            _, vjp = jax.vjp(_rms, x_ref[...], g_ref[...])
            dx, dg = vjp(acc[...])
            dx1_ref[...] = dx_ref[...] + dx
            dg_ref[...] += dg

    rowf = pl.BlockSpec((tm, FF_T), lambda i, j: (i, j))
    rowd = pl.BlockSpec((tm, D), lambda i, j: (i, 0))
    vec = pl.BlockSpec((1, D), lambda i, j: (0, 0))
    return _pc(
        body, name=name, grid=(n // tm, nf),
        in_specs=[rowd, rowd, vec, pl.BlockSpec((D, FF_T), lambda i, j: (0, j)),
                  pl.BlockSpec((D, FF_T), lambda i, j: (0, j)), pl.BlockSpec((FF_T, D), lambda i, j: (j, 0))],
        out_specs=[rowd, vec, rowd, rowf, rowf, rowf],
        out_shape=[_sds((n, D)), _sds((1, D)), _sds((n, D), bf16), _sds((n, D_FF), bf16), _sds((n, D_FF), bf16),
                   _sds((n, D_FF), bf16)],
        scratch=[pltpu.VMEM((tm, D), f32)], sem=("arbitrary", "arbitrary"),
    )(dx2, x1, g, wg, wu, wd)


def _loss_head(x2, g, tgt, name):
    n = x2.shape[0]
    tm = min(512, n)

    def f(x, gg, t):
        e = _rms(x, gg) - t
        return 0.5 * jnp.sum(jnp.mean(e * e, axis=-1, keepdims=True))

    def body(x_ref, g_ref, t_ref, l_ref, dx_ref, dg_ref):
        val, vjp = jax.vjp(f, x_ref[...], g_ref[...], t_ref[...])
        dx, dg, _ = vjp(jnp.ones((), f32))
        dx_ref[...] = dx

        @pl.when(pl.program_id(0) == 0)
        def _():
            l_ref[...] = jnp.zeros_like(l_ref)
            dg_ref[...] = jnp.zeros_like(dg_ref)

        l_ref[...] += val
        dg_ref[...] += dg

    rowd = pl.BlockSpec((tm, D), lambda i: (i, 0))
    return _pc(
        body, name=name, grid=(n // tm,),
        in_specs=[rowd, pl.BlockSpec((1, D), lambda i: (0, 0)), rowd],
        out_specs=[pl.BlockSpec((8, 128), lambda i: (0, 0)), rowd, pl.BlockSpec((1, D), lambda i: (0, 0))],
        out_shape=[_sds((8, 128)), _sds((n, D)), _sds((1, D))], sem=("arbitrary",),
    )(x2, g, tgt)


def _adamw(w, parts, m, v, name):
    nl, r, c = w.shape
    tr = r
    for cand in (1024, 512, 256, 128, 64, 32, 16, 8):
        if r % cand == 0 and cand * c * 4 <= 1024 * 1024:
            tr = cand
            break

    def body(*refs):
        w_ref, p_refs, (m_ref, v_ref, g_ref, d_ref, nm_ref, nv_ref) = refs[0], refs[1:1 + nl], refs[1 + nl:]

        def update(p_ref):
            gg = p_ref[0].astype(f32)
            for k in range(1, N_DEV):
                gg = gg + p_ref[k].astype(f32)
            g_ref[...] = gg
            nm = B1 * m_ref[...] + (1.0 - B1) * gg
            nv = B2 * v_ref[...] + (1.0 - B2) * (gg * gg)
            m_hat = nm / (1.0 - B1 ** STEP)
            v_hat = nv / (1.0 - B2 ** STEP)
            d_ref[...] = -LR * (m_hat / (jnp.sqrt(v_hat) + EPS) + WD * w_ref[...])
            nm_ref[...] = nm
            nv_ref[...] = nv

        for j in range(nl):
            pl.when(pl.program_id(0) == j)(functools.partial(update, p_refs[j]))

    spec = pl.BlockSpec((None, tr, c), lambda l, i: (l, i, 0))
    pspecs = [pl.BlockSpec((N_DEV, tr, c), lambda l, i, j=j: (0, jnp.where(l == j, i, 0), 0)) for j in range(nl)]
    return _pc(body, name=name, grid=(nl, r // tr), in_specs=[spec] + pspecs + [spec, spec], out_specs=[spec] * 4,
               out_shape=[_sds((nl, r, c))] * 4, sem=("arbitrary", "arbitrary"))(w, *parts, m, v)


def _peers():
    x, y, c = lax.axis_index("x"), lax.axis_index("y"), lax.axis_index("c")
    me = 4 * x + 2 * y + c
    peers = []
    for k in range(1, N_DEV):
        fx, fy, fc = (k >> 2) & 1, (k >> 1) & 1, k & 1
        peers.append(((1 - x) if fx else x, (1 - y) if fy else y, (1 - c) if fc else c))
    return me, peers


def _exchange(gathers, scatters, name):
    _, got = _pc(lambda: None, name=name, out_shape=[], ride=(gathers, scatters))()
    return got


SHARDED = {"w_in": 2, "gate_b": 2, "w_uq": 2, "w_ukv": 2, "rw_w0": 2, "rw_w2": 3, "rw_a0": 2, "rw_a2": 3, "rw_g2": 2,
           "w_branch": 3, "w_out": 1, "w_ffn_gate": 2, "w_ffn_up": 2, "w_ffn_down": 1}
GATHER_F32 = ("gate_b", "rw_w0", "rw_a0")
REPLICATED = ("attn_norm_g", "q_norm_g", "kv_norm_g", "sg_ln_g", "sg_ln_b", "sg_w", "sg_b", "rw_mu", "rw_k_k", "rw_k_a",
              "rw_r_k", "rw_ln_g", "rw_ln_b", "ffn_norm_g", "final_norm_g")
WEIGHTS = ("attn_norm_g", "w_in", "gate_b", "q_norm_g", "w_uq", "kv_norm_g", "w_ukv", "sg_ln_g", "sg_ln_b", "sg_w", "sg_b",
           "rw_mu", "rw_w0", "rw_w2", "rw_a0", "rw_a2", "rw_g2", "rw_k_k", "rw_k_a", "rw_r_k", "rw_ln_g", "rw_ln_b",
           "w_branch", "w_out", "ffn_norm_g", "w_ffn_gate", "w_ffn_up", "w_ffn_down", "final_norm_g")


REP_MAIN = tuple(k for k in REPLICATED if k != "attn_norm_g")
BIG = ("w_in", "w_branch", "w_out", "w_ffn_gate", "w_ffn_up", "w_ffn_down")
SMALL_BF = ("w_uq", "w_ukv", "rw_w2", "rw_a2", "rw_g2")
SMALL = SMALL_BF + GATHER_F32


def _pack128(blocks, names, dtype, lead=0, to=256):
    parts = [blocks[k].astype(dtype).reshape(blocks[k].shape[:lead] + (-1, 128)) for k in names]
    rows = sum(p.shape[lead] for p in parts)
    pad = -rows % to
    if pad:
        parts.append(jnp.zeros(parts[0].shape[:lead] + (pad, 128), dtype))
    return jnp.concatenate(parts, axis=lead)


def _unpack128(packed, shapes, names, lead=0):
    out, off = {}, 0
    for k in names:
        rows = 1
        for d in shapes[k]:
            rows *= d
        rows //= 128
        idx = (slice(None),) * lead + (slice(off, off + rows),)
        out[k] = packed[idx].reshape(packed.shape[:lead] + tuple(shapes[k]))
        off += rows
    return out


def _join_blocks(g, ax):
    shp = g.shape[1:]
    return jnp.moveaxis(g, 0, ax).reshape(shp[:ax] + (N_DEV * shp[ax],) + shp[ax + 1:])


def _split_blocks(full, ax):
    shp = full.shape
    return jnp.moveaxis(full.reshape(shp[:ax] + (N_DEV, shp[ax] // N_DEV) + shp[ax + 1:]), ax, 0)


def _w_in_padded(w):
    z = lambda n: jnp.zeros((w.shape[0], n), w.dtype)
    q, ckv, kr = w[:, 0:384], w[:, 384:640], w[:, 640:672]
    sg, rw, gate = w[:, 672:1696], w[:, 1696:3616], w[:, 3616:6688]
    return jnp.concatenate([gate, sg, rw, z(128), ckv, z(64), kr, z(32), q, z(P_W - O_MLA - MLA_W)], axis=1)


def _w_in_unpadded(g):
    return jnp.concatenate([g[:, O_Q:O_Q + 384], g[:, O_CKV:O_CKV + 256], g[:, O_SLAB + 64:O_SLAB + 96],
                            g[:, O_SG:O_SG + 1024], g[:, O_RW:O_RW + 1920], g[:, O_GATE:O_GATE + 3072]], axis=1)


REST = ("w_branch", "w_out", "w_ffn_gate", "w_ffn_up", "w_ffn_down")


def _rest_weights(full, l):
    return dict(wb=full["w_branch"][l], wo=full["w_out"][l], wg=full["w_ffn_gate"][l], wu=full["w_ffn_up"][l],
                wd=full["w_ffn_down"][l])


def _layer_weights(full, rep, l):
    w = {}
    w["w_in"] = _w_in_padded(full["w_in"][l])
    if full["w_branch"][l] is not None:
        w.update(_rest_weights(full, l))
    uq = full["w_uq"][l].reshape(Q_LORA, HEADS, QK_NOPE + QK_ROPE)
    w["wq"] = jnp.pad(uq, ((0, 0), (0, 0), (0, 32))).reshape(Q_LORA, HEADS * 128).astype(f32)
    ukv = full["w_ukv"][l].reshape(KV_LORA, HEADS, QK_NOPE + V_HEAD)
    wk = jnp.pad(ukv[:, :, :QK_NOPE], ((0, 0), (0, 0), (0, 64))).reshape(KV_LORA, HEADS * 128)
    w["wk"], w["wv"] = wk.astype(f32), ukv[:, :, QK_NOPE:].reshape(KV_LORA, HEADS * V_HEAD).astype(f32)
    bdiag = lambda t: jnp.concatenate([jnp.concatenate([t[0], jnp.zeros_like(t[0])], axis=1),
                                       jnp.concatenate([jnp.zeros_like(t[1]), t[1]], axis=1)], axis=0).astype(f32)
    w["w2"], w["a2"] = bdiag(full["rw_w2"][l]), bdiag(full["rw_a2"][l])
    w["g2"] = full["rw_g2"][l].astype(f32)
    w["w0"], w["a0"] = full["rw_w0"][l].reshape(1, 2 * RW_DIM), full["rw_a0"][l].reshape(1, 2 * RW_DIM)
    w["gate_b"] = full["gate_b"][l].reshape(1, 3 * D)
    row = lambda a: a.reshape(1, -1)
    for k in ("attn_norm_g", "q_norm_g", "kv_norm_g", "sg_ln_g", "sg_ln_b", "rw_k_k", "rw_k_a", "rw_ln_g", "rw_ln_b",
              "ffn_norm_g"):
        w[k] = row(rep[k][l])
    w["r_k"] = row(rep["rw_r_k"][l])
    w["mu"] = jnp.pad(row(rep["rw_mu"][l]), ((0, 0), (0, RW_W - 1920)))
    w["sg_w"] = [rep["sg_w"][l, k] for k in range(SG_GROUPS)]
    w["sg_bias"] = jnp.repeat(rep["sg_b"][l].T, SG_DIM // SG_GROUPS, axis=1)
    return w


def _riding(res, ride, got, key):
    if ride is None:
        return res
    got[key] = res[1]
    return res[0]


def _layer_fwd(x2, w, tabs, bsz, seq, l, rides=None, on_inproj=None):
    nm = lambda s: f"l{l}_{s}"
    n = x2.shape[0]
    tm = min(256, n)
    rides = rides or {}
    ride = lambda key: (rides[key], []) if key in rides else None
    got = {}
    p, h = _riding(_inproj_fwd(x2, w["attn_norm_g"], w["w_in"], nm("inproj"), ride("inproj")), ride("inproj"), got, "inproj")
    if on_inproj is not None:
        w.update(on_inproj(got["inproj"]))
    mla_rows = [(p, 256, O_CKV // 256), (p, 128, O_SLAB // 128), (p, 384, O_Q // 384), (tabs[0], 128, 0), (tabs[1], 128, 0)]
    mla_w = [w["q_norm_g"], w["kv_norm_g"], w["wq"], w["wk"], w["wv"]]
    q, k, v = _rowwise_fwd(nm("mla_proj"), _f_mla_proj, mla_rows, mla_w, [(1024, bf16), (1024, bf16), (512, bf16)], tm)
    ya = _riding(_attn_fwd(q, k, v, bsz, seq, nm("attn"), ride("attn")), ride("attn"), got, "attn")
    sg_rows = [(p, SG_DIM, O_SG // SG_DIM), (p, SG_DIM, O_SG // SG_DIM + 1)]
    sg_w = [w["sg_ln_g"], w["sg_ln_b"], w["sg_bias"]] + w["sg_w"]
    (yb,) = _rowwise_fwd(nm("sg"), _f_sg, sg_rows, sg_w, [(SG_DIM, f32)], SG_CHUNK)
    z = _shift_fwd(p, w["mu"], seq, nm("shift"))
    pre_rows = [(z, 512, 1), (z, 128, 12), (z, 128, 13), (z, 128, 14)]
    pre_w = [w["w0"], w["a0"], w["w2"], w["a2"], w["g2"], w["rw_k_k"], w["rw_k_a"]]
    lw, kd, kk, bd, g = _rowwise_fwd(nm("rw_pre"), _f_rw_pre, pre_rows, pre_w,
                                     [(1024, f32), (1024, f32), (512, f32), (1024, f32), (512, f32)], tm)
    y, s_in = _riding(_scan_fwd(z, lw, kd, kk, bd, bsz, seq, nm("scan"), ride("scan")), ride("scan"), got, "scan")
    post_rows = [(y[0], 512, 0), (y[1], 512, 0), (z, 512, 0), (z, 512, 2), (kd, 512, 0), (kd, 512, 1), (g, 512, 0)]
    post_w = [w["r_k"], w["rw_ln_g"], w["rw_ln_b"]]
    (yc,) = _rowwise_fwd(nm("rw_post"), _f_rw_post, post_rows, post_w, [(512, f32)], tm)
    x1 = _merge_fwd(x2, p, ya, yb, yc, w["gate_b"], w["wb"], w["wo"], nm("merge"))
    x3 = _ffn_fwd(x1, w["ffn_norm_g"], w["wg"], w["wu"], w["wd"], nm("ffn"))
    saved = dict(x=x2, p=p, h=h, q=q, k=k, v=v, ya=ya, yb=yb, z=z, lw=lw, kd=kd, kk=kk, bd=bd, g=g, y=y, s_in=s_in, yc=yc,
                 x1=x1, mla_rows=mla_rows, mla_w=mla_w, sg_rows=sg_rows, sg_w=sg_w, pre_rows=pre_rows, pre_w=pre_w,
                 post_rows=post_rows, post_w=post_w)
    return x3, saved, got


def _layer_bwd(dx3, w, sv, bsz, seq, l, rides=None):
    nm = lambda s: f"l{l}_{s}_bwd"
    n = dx3.shape[0]
    tm = min(256, n)
    g = {}
    rides = rides or {}
    ride = lambda key: rides[key](g) if key in rides else None
    got = {}
    dx1, g["ffn_norm_g"], h2, da, db, hm = _ffn_bwd(dx3, sv["x1"], w["ffn_norm_g"], w["wg"], w["wu"], w["wd"], nm("ffn"))
    g["wg"] = _matmul_tn(h2, da, nm("wg"))
    g["wu"] = _matmul_tn(h2, db, nm("wu"))
    g["wd"] = _matmul_tn(hm, dx3.astype(bf16), nm("wd"))
    dpg, dya, dyb, dyc, dt, mg, g["gate_b"] = _merge_bwd(dx1, sv["p"], sv["ya"], sv["yb"], sv["yc"], w["gate_b"], w["wb"],
                                                         w["wo"], nm("merge"))
    g["wo"] = _matmul_tn(mg, dx1.astype(bf16), nm("wo"))
    ys = (sv["ya"], sv["yb"], sv["yc"])
    g["wb"] = jnp.stack([_matmul_tn(ys[k].astype(bf16), dt[:, k * D:(k + 1) * D], nm(f"wb{k}")) for k in range(3)])
    (dy, dr_p, dv_p, dkd0, dkd1, dg_), (g["r_k"], g["rw_ln_g"], g["rw_ln_b"]) = _rowwise_bwd(
        nm("rw_post"), _f_rw_post, sv["post_rows"], sv["post_w"], [(dyc, 512, 0)], tm, [f32, None] + [f32] * 5)
    dkd_p = jnp.concatenate([dkd0, dkd1], axis=1)
    rd = ride("scan")
    dr_s, dv_s, dkk_s, dlw, dkd_s, dbd = _riding(
        _scan_bwd(sv["z"], sv["lw"], sv["kd"], sv["kk"], sv["bd"], sv["s_in"], dy, bsz, seq, nm("scan"), rd), rd, got, "scan")
    pre_cots = [(dlw, 1024, 0), (dkd_s + dkd_p, 1024, 0), (dkk_s[0] + dkk_s[1], 512, 0), (dbd, 1024, 0), (dg_, 512, 0)]
    (dk, dwl, dal, dgl), (g["w0"], g["a0"], g["w2"], g["a2"], g["g2"], g["rw_k_k"], g["rw_k_a"]) = _rowwise_bwd(
        nm("rw_pre"), _f_rw_pre, sv["pre_rows"], sv["pre_w"], pre_cots, tm, [f32] * 4)
    dz = jnp.concatenate([dr_s[0] + dr_s[1] + dr_p, dk, dv_s[0] + dv_s[1] + dv_p, dwl, dal, dgl,
                          jnp.zeros((n, RW_W - 1920), f32)], axis=1)
    dp_rw, g["mu"] = _shift_bwd(dz, sv["p"], w["mu"], seq, nm("shift"))
    (dp_su, dp_sv), (g["sg_ln_g"], g["sg_ln_b"], g["sg_bias"], *sgw) = _rowwise_bwd(
        nm("sg"), _f_sg, sv["sg_rows"], sv["sg_w"], [(dyb, SG_DIM, 0)], SG_CHUNK, [bf16, bf16])
    g["sg_w"] = jnp.stack(sgw)
    rd = ride("attn")
    dq, dk_, dv_ = _riding(_attn_bwd(sv["q"], sv["k"], sv["v"], sv["ya"], dya, bsz, seq, nm("attn"), rd), rd, got, "attn")
    (dp_ckv, dp_slab, dp_q), (g["q_norm_g"], g["kv_norm_g"], g["wq"], g["wk"], g["wv"]) = _rowwise_bwd(
        nm("mla_proj"), _f_mla_proj, sv["mla_rows"], sv["mla_w"], [(dq, 1024, 0), (dk_, 1024, 0), (dv_, 512, 0)], tm,
        [bf16, bf16, bf16, None, None])
    dp = jnp.concatenate([dpg, dp_su, dp_sv, dp_rw, dp_ckv, dp_slab, dp_q, jnp.zeros((n, P_W - O_MLA - MLA_W), bf16)],
                         axis=1)
    g["w_in"] = _matmul_tn(sv["h"], dp, nm("w_in"))
    rd = ride("inproj")
    dx, g["attn_norm_g"] = _riding(_norm_matmul_bwd(dp, w["w_in"], sv["x"], w["attn_norm_g"], dx1, nm("inproj"), rd), rd, got,
                                   "inproj")
    return dx, g, got


def _layer_grads_to_full(g):
    o = {}
    o["w_in"] = _w_in_unpadded(g["w_in"])
    o["w_uq"] = g["wq"].reshape(Q_LORA, HEADS, 128)[:, :, :QK_NOPE + QK_ROPE].reshape(Q_LORA, -1)
    gk = g["wk"].reshape(KV_LORA, HEADS, 128)[:, :, :QK_NOPE]
    gv = g["wv"].reshape(KV_LORA, HEADS, V_HEAD)
    o["w_ukv"] = jnp.concatenate([gk, gv], axis=2).reshape(KV_LORA, -1)
    unb = lambda t: jnp.stack([t[:LORA, :RW_DIM], t[LORA:, RW_DIM:]])
    o["rw_w2"], o["rw_a2"], o["rw_g2"] = unb(g["w2"]), unb(g["a2"]), g["g2"]
    o["rw_w0"], o["rw_a0"] = g["w0"].reshape(2, RW_DIM), g["a0"].reshape(2, RW_DIM)
    o["gate_b"] = g["gate_b"].reshape(3, D)
    o["w_branch"], o["w_out"] = g["wb"], g["wo"]
    o["w_ffn_gate"], o["w_ffn_up"], o["w_ffn_down"] = g["wg"], g["wu"], g["wd"]
    for k in ("attn_norm_g", "q_norm_g", "kv_norm_g", "sg_ln_g", "sg_ln_b", "rw_k_k", "rw_k_a", "rw_ln_g", "rw_ln_b",
              "ffn_norm_g"):
        if k in g:
            o[k] = g[k].reshape(-1)
    o["rw_r_k"] = g["r_k"].reshape(HEADS, RW_HEAD)
    o["rw_mu"] = g["mu"].reshape(-1)[:1920]
    o["sg_w"] = g["sg_w"]
    o["sg_b"] = g["sg_bias"].reshape(SG_CHUNK, SG_GROUPS, SG_DIM // SG_GROUPS).sum(axis=2).T
    return o


def _rope_tables(positions):
    inv = 1.0 / (10000.0 ** (jnp.arange(0, QK_ROPE, 2, dtype=f32) / QK_ROPE))
    ang = positions.astype(f32)[:, None] * inv
    cos, sin = jnp.cos(ang), jnp.sin(ang)
    n = positions.shape[0]
    c = jnp.concatenate([jnp.ones((n, 64), f32), cos, cos, jnp.zeros((n, 32), f32)], axis=1)
    s = jnp.concatenate([jnp.zeros((n, 64), f32), -sin, sin, jnp.zeros((n, 32), f32)], axis=1)
    return c, s


def _grad_parts(grad, name):
    return _split_blocks(grad, SHARDED[name] - 1).astype(bf16)


def _local_step(x, positions, full, rep, loss_target, blocks=None):
    bsz, seq, _ = x.shape
    n = bsz * seq
    x2 = x.reshape(n, D)
    tabs = _rope_tables(positions.reshape(n))
    join = lambda k, g: _join_blocks(g, SHARDED[k] - 1)
    rides, on_inproj = None, None
    if blocks is not None:
        rides = {"inproj": [blocks[k][0] for k in REST], "attn": [blocks["w_in"][1]], "scan": [blocks[k][1] for k in REST]}

        def on_inproj(got):
            for k, g in zip(REST, got):
                full[k][0] = join(k, g)
            return _rest_weights(full, 0)

    w0 = _layer_weights(full, rep, 0)
    x2, sv0, got = _layer_fwd(x2, w0, tabs, bsz, seq, 0, rides, on_inproj)
    if blocks is not None:
        full["w_in"][1] = join("w_in", got["attn"][0])
        for k, g in zip(REST, got["scan"]):
            full[k][1] = join(k, g)
    w1 = _layer_weights(full, rep, 1)
    x2, sv1, _ = _layer_fwd(x2, w1, tabs, bsz, seq, 1)
    loss, dx, dgf = _loss_head(x2, rep["final_norm_g"].reshape(1, D), loss_target.reshape(n, D), "loss_head")
    dx, g1, _ = _layer_bwd(dx, w1, sv1, bsz, seq, 1)
    grads1 = _layer_grads_to_full(g1)
    rides = None
    if blocks is not None:
        short = dict(w_branch="wb", w_out="wo", w_ffn_gate="wg", w_ffn_up="wu", w_ffn_down="wd")

        def beside_inproj(g):
            g0 = _layer_grads_to_full(g)
            both = {k: jnp.stack([g0[k], grads1[k]]) for k in g0}
            both["final_norm_g"] = dgf.reshape(D)
            split = {k: _split_blocks(both[k], SHARDED[k]) for k in SMALL}
            return ([_pack128(both, REP_MAIN, f32)],
                    [_grad_parts(g0["w_in"], "w_in"), _pack128(split, SMALL, f32, lead=1)])

        rides = {"scan": lambda g: ([], [_grad_parts(grads1[k], k) for k in BIG]),
                 "attn": lambda g: ([], [_grad_parts(g[short[k]], k) for k in REST]),
                 "inproj": beside_inproj}
    dx, g0, got = _layer_bwd(dx, w0, sv0, bsz, seq, 0, rides)
    grads0 = _layer_grads_to_full(g0)
    grads = {k: [grads0[k], grads1[k]] for k in grads0}
    grads["final_norm_g"] = dgf.reshape(D)
    parts = {}
    if blocks is not None:
        parts = {k: [None, p] for k, p in zip(BIG, got["scan"])}
        for k, p in zip(REST, got["attn"]):
            parts[k][0] = p
        parts["replicated"], parts["w_in"][0], parts["small"] = got["inproj"]
    return loss[0, 0], dx.reshape(bsz, seq, D), grads, parts


def kernel(x, positions, attn_norm_g, w_in, gate_b, q_norm_g, w_uq, kv_norm_g, w_ukv, sg_ln_g, sg_ln_b, sg_w, sg_b, rw_mu, rw_w0, rw_w2, rw_a0, rw_a2, rw_g2, rw_k_k, rw_k_a, rw_r_k, rw_ln_g, rw_ln_b, w_branch, w_out, ffn_norm_g, w_ffn_gate, w_ffn_up, w_ffn_down, final_norm_g, loss_target, m_attn_norm_g, m_w_in, m_gate_b, m_q_norm_g, m_w_uq, m_kv_norm_g, m_w_ukv, m_sg_ln_g, m_sg_ln_b, m_sg_w, m_sg_b, m_rw_mu, m_rw_w0, m_rw_w2, m_rw_a0, m_rw_a2, m_rw_g2, m_rw_k_k, m_rw_k_a, m_rw_r_k, m_rw_ln_g, m_rw_ln_b, m_w_branch, m_w_out, m_ffn_norm_g, m_w_ffn_gate, m_w_ffn_up, m_w_ffn_down, m_final_norm_g, v_attn_norm_g, v_w_in, v_gate_b, v_q_norm_g, v_w_uq, v_kv_norm_g, v_w_ukv, v_sg_ln_g, v_sg_ln_b, v_sg_w, v_sg_b, v_rw_mu, v_rw_w0, v_rw_w2, v_rw_a0, v_rw_a2, v_rw_g2, v_rw_k_k, v_rw_k_a, v_rw_r_k, v_rw_ln_g, v_rw_ln_b, v_w_branch, v_w_out, v_ffn_norm_g, v_w_ffn_gate, v_w_ffn_up, v_w_ffn_down, v_final_norm_g):
    args = locals()
    wts = {k: args[k] for k in WEIGHTS}
    mom_m = {k: args["m_" + k] for k in WEIGHTS}
    mom_v = {k: args["v_" + k] for k in WEIGHTS}
    shapes = {k: wts[k].shape for k in WEIGHTS}
    blocks = {k: wts[k].astype(bf16) for k in BIG}
    got = _exchange([blocks["w_in"][0], _pack128(wts, SMALL_BF, bf16), _pack128(wts, GATHER_F32, f32)], [], "gather_first")
    small = {**_unpack128(got[1], shapes, SMALL_BF, lead=1), **_unpack128(got[2], shapes, GATHER_F32, lead=1)}
    full = {k: list(_join_blocks(small[k], SHARDED[k])) for k in SMALL}
    full["w_in"] = [_join_blocks(got[0], SHARDED["w_in"] - 1), None]
    full.update({k: [None, None] for k in REST})
    rep = {k: wts[k] for k in REPLICATED}
    loss, grad_x, grads, parts = _local_step(x, positions, full, rep, loss_target, blocks)
    loss = lax.psum(loss, ("x", "y", "c"))
    last = ("attn_norm_g",)
    (last_parts,) = _exchange([_pack128({"attn_norm_g": jnp.stack(grads["attn_norm_g"])}, last, f32, to=16)], [],
                              "exchange_last")
    gw, delta, new_m, new_v = {}, {}, {}, {}
    for k in BIG:
        three = lambda a, k=k: a.reshape(a.shape[0], -1, shapes[k][-1])
        res = _adamw(three(wts[k]), [three(p) for p in parts[k]], three(mom_m[k]), three(mom_v[k]), f"adamw_{k}")
        gw[k], delta[k], new_m[k], new_v[k] = (t.reshape(shapes[k]) for t in res)
    for names, got, to in ((SMALL, parts["small"], 256), (REP_MAIN, parts["replicated"], 256), (last, last_parts, 16)):
        pk = lambda dct: _pack128(dct, names, f32, to=to)[None]
        res = _adamw(pk(wts), [got], pk(mom_m), pk(mom_v), f"adamw_{names[0]}")
        for dst, t in zip((gw, delta, new_m, new_v), res):
            dst.update(_unpack128(t[0], shapes, names))
    return (loss, grad_x, *[gw[k] for k in WEIGHTS], *[delta[k] for k in WEIGHTS], *[new_m[k] for k in WEIGHTS],
            *[new_v[k] for k in WEIGHTS])
```

```python
import functools

import jax
import jax.numpy as jnp
from jax import lax
from jax.experimental import pallas as pl
from jax.experimental.pallas import tpu as pltpu

f32 = jnp.float32
bf16 = jnp.bfloat16
HI = lax.Precision.HIGHEST
NN, NT, TN = ((1,), (0,)), ((1,), (1,)), ((0,), (0,))

N_DEV = 8
D = 1024
HEADS = 8
Q_LORA, KV_LORA, QK_NOPE, QK_ROPE, V_HEAD = 384, 256, 64, 32, 64
SG_DIM, SG_CHUNK, SG_GROUPS = 512, 128, 8
RW_DIM, RW_HEAD, LORA = 512, 64, 64
D_FF = 2816
N_IN = 6688
NORM_EPS, LN_EPS, GN_EPS = 1e-6, 1e-5, 64e-5
ATT_SCALE = (QK_NOPE + QK_ROPE) ** -0.5
P_W = 7168
O_GATE, O_SG, O_RW, O_MLA = 0, 3072, 4096, 6144
RW_W = 2048
MLA_W = 768
O_CKV, O_SLAB, O_Q = O_MLA, O_MLA + 256, O_MLA + 384
CHUNK = 128
VMEM_LIMIT = 56 * 1024 * 1024

B1, B2, LR, EPS, WD, STEP = 0.9, 0.999, 0.001, 1e-8, 0.01, 10


def _pc(body, *, name, out_shape, grid=(), in_specs=(), out_specs=(), scratch=(), sem=None, ride=None):
    params = pltpu.CompilerParams(dimension_semantics=sem, vmem_limit_bytes=VMEM_LIMIT)
    if ride is None:
        return pl.pallas_call(body, out_shape=out_shape, grid=grid, in_specs=in_specs, out_specs=out_specs,
                              scratch_shapes=scratch, compiler_params=params, name=name, interpret=False)
    gathers, scatters = ride
    moved = list(gathers) + list(scatters)
    ng, nx = len(gathers), len(moved)
    single = not isinstance(out_shape, (list, tuple))
    outs = [out_shape] if single else list(out_shape)
    ospecs = [out_specs] if single else list(out_specs)
    n_in, n_out, n_scr = len(in_specs), len(outs), len(scratch)
    per = N_DEV - 1

    def riding(*refs):
        ins, xin = refs[:n_in], refs[n_in:n_in + nx]
        outs_r, xout = refs[n_in + nx:n_in + nx + n_out], refs[n_in + nx + n_out:n_in + 2 * nx + n_out]
        own = refs[n_in + 2 * nx + n_out:n_in + 2 * nx + n_out + n_scr]
        send_sems, recv_sems, local_sems = refs[n_in + 2 * nx + n_out + n_scr:]

        def copies():
            me, peers = _peers()
            cps = []
            for a in range(nx):
                whole = a < ng
                cps.append(pltpu.make_async_copy(xin[a] if whole else xin[a].at[me], xout[a].at[me], local_sems.at[a]))
                for k, peer in enumerate(peers):
                    dev = 4 * peer[0] + 2 * peer[1] + peer[2]
                    cps.append(pltpu.make_async_remote_copy(
                        src_ref=xin[a] if whole else xin[a].at[dev], dst_ref=xout[a].at[me],
                        send_sem=send_sems.at[a * per + k], recv_sem=recv_sems.at[a * per + k], device_id=peer,
                        device_id_type=pl.DeviceIdType.MESH))
            return cps

        if not grid:
            for cp in copies():
                cp.start()
            body(*ins, *outs_r, *own)
            for cp in copies():
                cp.wait()
            return
        ids = [pl.program_id(a) for a in range(len(grid))]
        first = functools.reduce(jnp.logical_and, [i == 0 for i in ids])
        last = functools.reduce(jnp.logical_and, [i == g - 1 for i, g in zip(ids, grid)])

        @pl.when(first)
        def _():
            for cp in copies():
                cp.start()

        body(*ins, *outs_r, *own)

        @pl.when(last)
        def _():
            for cp in copies():
                cp.wait()

    anyspec = pl.BlockSpec(memory_space=pl.ANY)
    call = pl.pallas_call(
        riding, grid=grid, in_specs=list(in_specs) + [anyspec] * nx, out_specs=ospecs + [anyspec] * nx,
        out_shape=outs + [_sds((N_DEV,) + a.shape, a.dtype) for a in gathers] + [_sds(a.shape, a.dtype) for a in scatters],
        scratch_shapes=list(scratch) + [pltpu.SemaphoreType.DMA((nx * per,)), pltpu.SemaphoreType.DMA((nx * per,)),
                                        pltpu.SemaphoreType.DMA((nx,))],
        compiler_params=params, name=name, interpret=False)

    def run(*args):
        res = call(*args, *moved)
        own = res[0] if single else list(res[:n_out])
        return own, list(res[n_out:])

    return run


def _sds(shape, dtype=f32):
    return jax.ShapeDtypeStruct(tuple(shape), dtype)


def _dot(a, b, dims, precision=None):
    return lax.dot_general(a, b, (dims, ((), ())), preferred_element_type=f32, precision=precision)


def _bdot(a, b, dims=NN):
    return _dot(a.astype(bf16), b.astype(bf16), dims)


@jax.custom_vjp
def _mm(a, w):
    return _bdot(a, w, NN)


def _mm_fwd(a, w):
    return _bdot(a, w, NN), (a, w)


def _mm_bwd(res, g):
    a, w = res
    return _bdot(g, w, NT), _bdot(a, g, TN)


_mm.defvjp(_mm_fwd, _mm_bwd)


@jax.custom_vjp
def _mm_nt(a, b):
    return _bdot(a, b, NT)


def _mm_nt_fwd(a, b):
    return _bdot(a, b, NT), (a, b)


def _mm_nt_bwd(res, g):
    a, b = res
    return _bdot(g, b, NN), _bdot(g, a, TN)


_mm_nt.defvjp(_mm_nt_fwd, _mm_nt_bwd)


@jax.custom_vjp
def _mm_tn(a, b):
    return _bdot(a, b, TN)


def _mm_tn_fwd(a, b):
    return _bdot(a, b, TN), (a, b)


def _mm_tn_bwd(res, g):
    a, b = res
    return _bdot(b, g, NT), _bdot(a, g, NN)


_mm_tn.defvjp(_mm_tn_fwd, _mm_tn_bwd)


def _rms(x, g):
    return x * lax.rsqrt(jnp.mean(x * x, axis=-1, keepdims=True) + NORM_EPS) * g


def _sigmoid(x):
    return 1.0 / (1.0 + jnp.exp(-x))


def _gelu(x):
    return 0.5 * x * (1.0 + jnp.tanh(0.7978845608028654 * (x + 0.044715 * x * x * x)))


def _softplus(x):
    return jnp.maximum(x, 0.0) + jnp.log(1.0 + jnp.exp(-jnp.abs(x)))


@jax.custom_vjp
def _group_sum(x):
    w = x.shape[-1]
    r = lax.broadcasted_iota(jnp.int32, (w, w), 0) // RW_HEAD
    c = lax.broadcasted_iota(jnp.int32, (w, w), 1) // RW_HEAD
    ones = (r == c).astype(bf16)
    hi = x.astype(bf16)
    lo = (x - hi.astype(f32)).astype(bf16)
    return _dot(jnp.concatenate([hi, lo], axis=1), jnp.concatenate([ones, ones], axis=0), NN)


_group_sum.defvjp(lambda x: (_group_sum(x), None), lambda _, g: (_group_sum(g),))


@jax.custom_vjp
def _swap(x):
    w = x.shape[-1]
    lane = lax.broadcasted_iota(jnp.int32, x.shape, 1) % 128
    lo = (lane >= 64) & (lane < 80)
    hi = (lane >= 80) & (lane < 96)
    return jnp.where(lo, pltpu.roll(x, w - 16, 1), jnp.where(hi, pltpu.roll(x, 16, 1), 0.0))


_swap.defvjp(lambda x: (_swap(x), None), lambda _, g: (_swap(g),))


def _rope(x, c, s):
    return x * c + _swap(x) * s


def _row_spec(tm, width, blk):
    return pl.BlockSpec((tm, width), lambda i, blk=blk: (i, blk))


def _full_spec(a):
    nd = a.ndim
    return pl.BlockSpec(a.shape, lambda i, nd=nd: (0,) * nd)


def _rowwise_fwd(name, f, rows, weights, outs, tm):
    n = rows[0][0].shape[0]
    nr, nw = len(rows), len(weights)

    def body(*refs):
        vals = [r[...].astype(f32) for r in refs[:nr + nw]]
        res = f(*vals)
        for o_ref, o in zip(refs[nr + nw:], res):
            o_ref[...] = o.astype(o_ref.dtype)

    return _pc(
        body, name=name, grid=(n // tm,),
        in_specs=[_row_spec(tm, w, b) for _, w, b in rows] + [_full_spec(w) for w in weights],
        out_specs=[_row_spec(tm, w, 0) for w, _ in outs],
        out_shape=[_sds((n, w), dt) for w, dt in outs], sem=("parallel",),
    )(*[a for a, _, _ in rows], *weights)


def _rowwise_bwd(name, f, rows, weights, cots, tm, drows):
    n = rows[0][0].shape[0]
    nr, nw, nc = len(rows), len(weights), len(cots)
    want = [k for k, dt in enumerate(drows) if dt is not None]

    def body(*refs):
        vals = [r[...].astype(f32) for r in refs[:nr + nw]]
        cot = tuple(r[...].astype(f32) for r in refs[nr + nw:nr + nw + nc])
        _, vjp = jax.vjp(f, *vals)
        grads = vjp(cot)
        outs = refs[nr + nw + nc:]
        for o_ref, k in zip(outs[:len(want)], want):
            o_ref[...] = grads[k].astype(o_ref.dtype)
        first = pl.program_id(0) == 0
        for o_ref, g in zip(outs[len(want):], grads[nr:]):
            @pl.when(first)
            def _(o_ref=o_ref, g=g):
                o_ref[...] = g

            @pl.when(jnp.logical_not(first))
            def _(o_ref=o_ref, g=g):
                o_ref[...] += g

    res = _pc(
        body, name=name, grid=(n // tm,),
        in_specs=[_row_spec(tm, w, b) for _, w, b in rows] + [_full_spec(w) for w in weights]
        + [_row_spec(tm, w, b) for _, w, b in cots],
        out_specs=[_row_spec(tm, rows[k][1], 0) for k in want] + [_full_spec(w) for w in weights],
        out_shape=[_sds((n, rows[k][1]), drows[k]) for k in want] + [_sds(w.shape) for w in weights],
        sem=("arbitrary",),
    )(*[a for a, _, _ in rows], *weights, *[a for a, _, _ in cots])
    return res[:len(want)], res[len(want):]


def _inproj_fwd(x2, g, w, name, ride=None):
    n = x2.shape[0]
    tm, tn = min(512, n), 512

    def body(x_ref, g_ref, w_ref, p_ref, h_ref):
        @pl.when(pl.program_id(1) == 0)
        def _():
            h_ref[...] = _rms(x_ref[...], g_ref[...]).astype(bf16)

        p_ref[...] = jnp.dot(h_ref[...], w_ref[...], preferred_element_type=f32)

    return _pc(
        body, name=name, grid=(n // tm, P_W // tn),
        in_specs=[pl.BlockSpec((tm, D), lambda i, j: (i, 0)), pl.BlockSpec((1, D), lambda i, j: (0, 0)),
                  pl.BlockSpec((D, tn), lambda i, j: (0, j))],
        out_specs=[pl.BlockSpec((tm, tn), lambda i, j: (i, j)), pl.BlockSpec((tm, D), lambda i, j: (i, 0))],
        out_shape=[_sds((n, P_W)), _sds((n, D), bf16)], sem=("parallel", "arbitrary"), ride=ride,
    )(x2, g, w)


def _norm_matmul_bwd(dy, w, x2, g, dres, name, ride=None):
    n, k = dy.shape
    tm = min(512, n)
    tk = 1024 if k % 1024 == 0 else 1408
    nk = k // tk

    def body(dy_ref, w_ref, x_ref, g_ref, dr_ref, dx_ref, dg_ref, acc):
        i, j = pl.program_id(0), pl.program_id(1)

        @pl.when(j == 0)
        def _():
            acc[...] = jnp.zeros_like(acc)

        @pl.when((i == 0) & (j == 0))
        def _():
            dg_ref[...] = jnp.zeros_like(dg_ref)

        acc[...] += _dot(dy_ref[...], w_ref[...], NT)

        @pl.when(j == nk - 1)
        def _():
            _, vjp = jax.vjp(_rms, x_ref[...], g_ref[...])
            dx, dg = vjp(acc[...])
            dx_ref[...] = dr_ref[...] + dx
            dg_ref[...] += dg

    return _pc(
        body, name=name, grid=(n // tm, nk),
        in_specs=[pl.BlockSpec((tm, tk), lambda i, j: (i, j)), pl.BlockSpec((D, tk), lambda i, j: (0, j)),
                  pl.BlockSpec((tm, D), lambda i, j: (i, 0)), pl.BlockSpec((1, D), lambda i, j: (0, 0)),
                  pl.BlockSpec((tm, D), lambda i, j: (i, 0))],
        out_specs=[pl.BlockSpec((tm, D), lambda i, j: (i, 0)), pl.BlockSpec((1, D), lambda i, j: (0, 0))],
        out_shape=[_sds((n, D)), _sds((1, D))], scratch=[pltpu.VMEM((tm, D), f32)], sem=("arbitrary", "arbitrary"),
        ride=ride,
    )(dy, w, x2, g, dres)


def _matmul_tn(a, g, name, ride=None):
    n, k = a.shape
    m = g.shape[1]
    tr = min(512, n)
    tk = k if k <= 1024 else 1408
    tn = m if m <= 1024 else (512 if m % 512 == 0 else 1408)
    nr = n // tr

    def body(a_ref, g_ref, o_ref):
        @pl.when(pl.program_id(2) == 0)
        def _():
            o_ref[...] = jnp.zeros_like(o_ref)

        o_ref[...] += _dot(a_ref[...], g_ref[...], TN)

    return _pc(
        body, name=name, grid=(k // tk, m // tn, nr),
        in_specs=[pl.BlockSpec((tr, tk), lambda i, j, r: (r, i)), pl.BlockSpec((tr, tn), lambda i, j, r: (r, j))],
        out_specs=pl.BlockSpec((tk, tn), lambda i, j, r: (i, j)),
        out_shape=_sds((k, m)), sem=("parallel", "parallel", "arbitrary"), ride=ride,
    )(a, g)


def _f_mla_proj(ckv, slab, pq, c, s, qg, kg, wq, wk, wv):
    c8, s8 = jnp.concatenate([c] * HEADS, axis=1), jnp.concatenate([s] * HEADS, axis=1)
    q = _rope(_mm(_rms(pq, qg), wq), c8, s8)
    cn = _rms(ckv, kg)
    k = _mm(cn, wk) + jnp.concatenate([_rope(slab, c, s)] * HEADS, axis=1)
    return q, k, _mm(cn, wv)


def _attn_fwd(q, k, v, bsz, seq, name, ride=None):
    n = q.shape[0]
    tq = min(256, seq)
    nq = seq // tq

    def body(q_ref, k_ref, v_ref, o_ref):
        lane = lax.broadcasted_iota(jnp.int32, (tq, 128), 1) < 64
        vv = v_ref[...]
        two = range(2)
        s = [_dot(q_ref[:, h * 128:(h + 1) * 128], k_ref[:, h * 128:(h + 1) * 128], NT) * ATT_SCALE for h in two]
        e = [jnp.exp(s[h] - jnp.max(s[h], axis=-1, keepdims=True)) for h in two]
        p = [(e[h] / jnp.sum(e[h], axis=-1, keepdims=True)).astype(bf16) for h in two]
        outs = [_dot(p[h], vv, NN) for h in two]
        o_ref[...] = jnp.where(lane, outs[0], outs[1])

    return _pc(
        body, name=name, grid=(bsz, HEADS // 2, nq),
        in_specs=[pl.BlockSpec((tq, 256), lambda b, h, i: (b * nq + i, h)),
                  pl.BlockSpec((seq, 256), lambda b, h, i: (b, h)),
                  pl.BlockSpec((seq, 128), lambda b, h, i: (b, h))],
        out_specs=pl.BlockSpec((tq, 128), lambda b, h, i: (b * nq + i, h)),
        out_shape=_sds((n, HEADS * V_HEAD)), sem=("parallel", "parallel", "parallel"), ride=ride,
    )(q, k, v)


def _attn_bwd(q, k, v, o, do, bsz, seq, name, ride=None):
    n = q.shape[0]
    tq = min(256, seq)
    nq = seq // tq

    def body(q_ref, k_ref, v_ref, o_ref, do_ref, dq_ref, dk_ref, dv_ref):
        @pl.when(pl.program_id(2) == 0)
        def _():
            dk_ref[...] = jnp.zeros_like(dk_ref)
            dv_ref[...] = jnp.zeros_like(dv_ref)

        lane = lax.broadcasted_iota(jnp.int32, (tq, 128), 1) < 64
        vv = v_ref[...]
        for h in range(2):
            qh, kh = q_ref[:, h * 128:(h + 1) * 128], k_ref[:, h * 128:(h + 1) * 128]
            s = _dot(qh, kh, NT) * ATT_SCALE
            e = jnp.exp(s - jnp.max(s, axis=-1, keepdims=True))
            p = e / jnp.sum(e, axis=-1, keepdims=True)
            doh = jnp.where(lane if h == 0 else jnp.logical_not(lane), do_ref[...], 0.0)
            delta = jnp.sum(doh * o_ref[...], axis=-1, keepdims=True)
            dob = doh.astype(bf16)
            dp = _dot(dob, vv, NT)
            ds = (p * (dp - delta) * ATT_SCALE).astype(bf16)
            dq_ref[:, h * 128:(h + 1) * 128] = _dot(ds, kh, NN)
            dk_ref[:, h * 128:(h + 1) * 128] += _dot(ds, qh, TN)
            dv_ref[...] += _dot(p.astype(bf16), dob, TN)

    return _pc(
        body, name=name, grid=(bsz, HEADS // 2, nq),
        in_specs=[pl.BlockSpec((tq, 256), lambda b, h, i: (b * nq + i, h)),
                  pl.BlockSpec((seq, 256), lambda b, h, i: (b, h)),
                  pl.BlockSpec((seq, 128), lambda b, h, i: (b, h)),
                  pl.BlockSpec((tq, 128), lambda b, h, i: (b * nq + i, h)),
                  pl.BlockSpec((tq, 128), lambda b, h, i: (b * nq + i, h))],
        out_specs=[pl.BlockSpec((tq, 256), lambda b, h, i: (b * nq + i, h)),
                   pl.BlockSpec((seq, 256), lambda b, h, i: (b, h)),
                   pl.BlockSpec((seq, 128), lambda b, h, i: (b, h))],
        out_shape=[_sds((n, HEADS * 128)), _sds((n, HEADS * 128)), _sds((n, HEADS * V_HEAD))],
        sem=("parallel", "parallel", "arbitrary"), ride=ride,
    )(q, k, v, o, do)


def _f_sg(pu, pv, lg, lb, bias, *ws):
    u, vv = _gelu(pu), _gelu(pv)
    mu = jnp.mean(vv, axis=-1, keepdims=True)
    d = vv - mu
    vv = d * lax.rsqrt(jnp.mean(d * d, axis=-1, keepdims=True) + LN_EPS) * lg + lb
    group = lax.broadcasted_iota(jnp.int32, (SG_CHUNK, SG_DIM), 1) // (SG_DIM // SG_GROUPS)
    mixed = bias
    for k, w in enumerate(ws):
        mixed = mixed + jnp.where(group == k, _mm(w, vv), 0.0)
    return (u * mixed,)


def _shift_mean(a, prev_row, next_row):
    t = a.shape[0]
    row = lax.broadcasted_iota(jnp.int32, a.shape, 0)
    prev = jnp.where(row == 0, prev_row, pltpu.roll(a, 1, 0))
    nxt = jnp.where(row == t - 1, next_row, pltpu.roll(a, t - 1, 0))
    return 0.5 * (prev + nxt)


def _halo_specs(tm, width, blk, nblk8):
    h = tm // 8
    return [pl.BlockSpec((tm, width), lambda i: (i, blk)),
            pl.BlockSpec((8, width), lambda i: (jnp.maximum(i * h - 1, 0), blk)),
            pl.BlockSpec((8, width), lambda i: (jnp.minimum((i + 1) * h, nblk8 - 1), blk))]


def _edge_rows(i, tm, seq, pv_ref, nx_ref, scale=None):
    first = (i * tm) % seq == 0
    last = ((i + 1) * tm) % seq == 0
    pv, nx = pv_ref[7:8, :], nx_ref[0:1, :]
    if scale is not None:
        pv, nx = pv * scale, nx * scale
    return jnp.where(first, 0.0, pv), jnp.where(last, 0.0, nx)


def _shift_fwd(p, mu, seq, name):
    n = p.shape[0]
    tm = min(256, seq)
    blk = O_RW // RW_W

    def body(x_ref, pv_ref, nx_ref, mu_ref, z_ref):
        x = x_ref[...]
        pv, nx = _edge_rows(pl.program_id(0), tm, seq, pv_ref, nx_ref)
        z_ref[...] = x + mu_ref[...] * (_shift_mean(x, pv, nx) - x)

    return _pc(
        body, name=name, grid=(n // tm,),
        in_specs=_halo_specs(tm, RW_W, blk, n // 8) + [pl.BlockSpec((1, RW_W), lambda i: (0, 0))],
        out_specs=pl.BlockSpec((tm, RW_W), lambda i: (i, 0)), out_shape=_sds((n, RW_W)), sem=("parallel",),
    )(p, p, p, mu)


def _shift_bwd(dz, p, mu, seq, name):
    n = p.shape[0]
    tm = min(256, seq)
    blk = O_RW // RW_W

    def body(dz_ref, dpv_ref, dnx_ref, x_ref, pv_ref, nx_ref, mu_ref, dx_ref, dmu_ref):
        i = pl.program_id(0)
        mu_v = mu_ref[...]
        dzv = dz_ref[...]
        m = dzv * mu_v
        mpv, mnx = _edge_rows(i, tm, seq, dpv_ref, dnx_ref, mu_v)
        dx_ref[...] = (dzv - m + _shift_mean(m, mpv, mnx)).astype(dx_ref.dtype)
        x = x_ref[...]
        pv, nx = _edge_rows(i, tm, seq, pv_ref, nx_ref)
        part = jnp.sum(dzv * (_shift_mean(x, pv, nx) - x), axis=0, keepdims=True)

        @pl.when(i == 0)
        def _():
            dmu_ref[...] = part

        @pl.when(i != 0)
        def _():
            dmu_ref[...] += part

    return _pc(
        body, name=name, grid=(n // tm,),
        in_specs=_halo_specs(tm, RW_W, 0, n // 8) + _halo_specs(tm, RW_W, blk, n // 8)
        + [pl.BlockSpec((1, RW_W), lambda i: (0, 0))],
        out_specs=[pl.BlockSpec((tm, RW_W), lambda i: (i, 0)), pl.BlockSpec((1, RW_W), lambda i: (0, 0))],
        out_shape=[_sds((n, RW_W), bf16), _sds((1, RW_W))], sem=("arbitrary",),
    )(dz, dz, dz, p, p, p, mu)


def _f_rw_pre(k, wl, al, gl, w0, a0, w2, a2, g2, k_k, k_a):
    w = w0 + _mm(jnp.tanh(wl), w2)
    lw = -jnp.exp(-_softplus(-w) - 0.5)
    a = _sigmoid(a0 + _mm(al, a2))
    g = _mm(_sigmoid(gl), g2)
    kkr = k * k_k
    kk = kkr / jnp.maximum(jnp.sqrt(_group_sum(kkr * kkr)), 1e-12)
    two = lambda t: jnp.concatenate([t, t], axis=1)
    kd = two(k) * (1.0 + (a - 1.0) * two(k_a))
    bd = two(kk) * a
    return lw, kd, kk, bd, g


def _f_rw_post(y0, y1, r, v, kd0, kd1, g, r_k, ln_g, ln_b):
    y = y0 + y1
    mean = _group_sum(y) * (1.0 / RW_HEAD)
    d = y - mean
    var = _group_sum(d * d) * (1.0 / RW_HEAD)
    yn = d * lax.rsqrt(var + GN_EPS) * ln_g + ln_b
    bonus = _group_sum(r * (kd0 + kd1) * r_k)
    return ((yn + bonus * v) * g,)


@jax.custom_vjp
def _tri_inv(mats):
    c = mats[0].shape[0]
    row = lax.broadcasted_iota(jnp.int32, (c, c), 0)
    col = lax.broadcasted_iota(jnp.int32, (c, c), 1)
    eye = (row == col).astype(f32)
    blk = lambda b: (row // b) == (col // b)
    ld = [jnp.where(blk(8), a, 0.0) for a in mats]
    l2 = [_bdot(x, x) for x in ld]
    l4 = [_bdot(x, x) for x in l2]
    t = [_bdot(eye - x, eye + y) for x, y in zip(ld, l2)]
    t = [_bdot(x, eye + y) for x, y in zip(t, l4)]
    b = 8
    while b < c:
        sub = blk(2 * b) & jnp.logical_not(blk(b))
        p = [_bdot(x, jnp.where(sub, a, 0.0)) for x, a in zip(t, mats)]
        t = [x - _bdot(y, x) for x, y in zip(t, p)]
        b *= 2
    return tuple(t)


def _tri_inv_fwd(mats):
    t = _tri_inv(mats)
    return t, t


def _tri_inv_bwd(ts, gs):
    p = [_bdot(t, g, TN) for t, g in zip(ts, gs)]
    return (tuple(-_bdot(x, t, NT) for x, t in zip(p, ts)),)


_tri_inv.defvjp(_tri_inv_fwd, _tri_inv_bwd)


def _split3(x):
    h = x.astype(bf16)
    r = x - h.astype(f32)
    m = r.astype(bf16)
    return h, m, (r - m.astype(f32)).astype(bf16)


@jax.custom_vjp
def _mask_mm(mask, x):
    mb = mask.astype(bf16)
    return _dot(jnp.concatenate([mb, mb, mb], axis=1), jnp.concatenate(_split3(x), axis=0), NN)


def _mask_mm_bwd(mask, g):
    mb = mask.astype(bf16)
    return jnp.zeros_like(mask), _dot(jnp.concatenate([mb, mb, mb], axis=0), jnp.concatenate(_split3(g), axis=0), TN)


_mask_mm.defvjp(lambda mask, x: (_mask_mm(mask, x), mask), _mask_mm_bwd)


@jax.custom_vjp
def _split_lanes(x):
    h = x.shape[1] // 2
    return x[:, :h], x[:, h:]


_split_lanes.defvjp(lambda x: (_split_lanes(x), None), lambda _, g: (jnp.concatenate(g, axis=1),))


def _scan_chunk(s0, r, v, kk, lw, kd, bd, rev):
    n = len(r)
    each = range(n)
    c = r[0].shape[0]
    row = lax.broadcasted_iota(jnp.int32, (c, 2 * c), 0)
    col = lax.broadcasted_iota(jnp.int32, (c, 2 * c), 1) % c
    ahead = jnp.where(rev, col - row, row - col)
    before = ahead > 0
    incl = ahead >= 0
    lane = lax.broadcasted_iota(jnp.int32, (1, 128), 1)
    m0 = (lane < 64).astype(f32)
    heads = lambda t: jnp.concatenate([t * m0, t * (1.0 - m0)], axis=0)
    bd_mask = ((lax.broadcasted_iota(jnp.int32, (128, 128), 0) // 64)
               == (lax.broadcasted_iota(jnp.int32, (128, 128), 1) // 64)).astype(f32)
    tot = [jnp.sum(lw[i], axis=0, keepdims=True) for i in each]
    row1 = lax.broadcasted_iota(jnp.int32, (c, c), 0)
    col1 = lax.broadcasted_iota(jnp.int32, (c, c), 1)
    upto = (jnp.where(rev, col1 - row1, row1 - col1) >= 0).astype(f32)
    lp = [_mask_mm(upto, lw[i]) - 0.5 * tot[i] for i in each]
    eg = [jnp.exp(lp[i]) for i in each]
    ieg = [jnp.exp(-lp[i]) for i in each]
    rt = [r[i] * eg[i] for i in each]
    kt = [kd[i] * ieg[i] for i in each]
    bt = [bd[i] * ieg[i] for i in each]
    at = [kk[i] * jnp.exp(lp[i] - lw[i]) for i in each]
    etot = [jnp.exp(0.5 * tot[i]) for i in each]
    si = [s0[i] * etot[i] for i in each]
    bth = [heads(bt[i]) for i in each]
    kth = [heads(kt[i]) for i in each]
    vh = [heads(v[i]) for i in each]
    a_ab = [jnp.where(before, _mm_nt(at[i], bth[i]), 0.0) for i in each]
    a_ak = [jnp.where(before, _mm_nt(at[i], kth[i]), 0.0) for i in each]
    a_rb = [jnp.where(incl, _mm_nt(rt[i], bth[i]), 0.0) for i in each]
    a_rk = [jnp.where(incl, _mm_nt(rt[i], kth[i]), 0.0) for i in each]
    halves = [_split_lanes(a_ab[i]) for i in each]
    inv = _tri_inv(tuple(m for pair in halves for m in pair))
    t = [jnp.concatenate([inv[2 * i], inv[2 * i + 1]], axis=1) for i in each]
    x0 = [_mm_nt(at[i], si[i]) for i in each]
    x = [x0[i] + _mm(a_ak[i], vh[i]) for i in each]
    u = [-_mm(t[i], heads(x[i])) for i in each]
    y0 = [_mm_nt(rt[i], si[i]) for i in each]
    y = [y0[i] + _mm(jnp.concatenate([a_rb[i], a_rk[i]], axis=1), jnp.concatenate([heads(u[i]), vh[i]], axis=0))
         for i in each]
    ds = [_mm_tn(jnp.concatenate([u[i], v[i]], axis=0), jnp.concatenate([bt[i], kt[i]], axis=0)) for i in each]
    se = [(si[i] + ds[i] * bd_mask) * etot[i] for i in each]
    return tuple(y), tuple(se)


PAIRS = HEADS // 2


def _scan_specs(nc, bsz, flip=False):
    def cc(d, c):
        c = nc - 1 - c if flip else c
        return jnp.where(d == 0, c, nc - 1 - c)

    rowblk = lambda d, b, c: b * nc + cc(d, c)
    zspec = lambda blk: pl.BlockSpec((CHUNK, RW_DIM), lambda d, b, c: (rowblk(d, b, c), blk))
    dspec = pl.BlockSpec((CHUNK, RW_DIM), lambda d, b, c: (rowblk(d, b, c), d))
    yspec = pl.BlockSpec((None, CHUNK, RW_DIM), lambda d, b, c: (d, rowblk(d, b, c), 0))
    sspec = pl.BlockSpec((None, PAIRS, 128, 128), lambda d, b, c: ((d * bsz + b) * nc + cc(d, c), 0, 0, 0))
    return zspec, dspec, yspec, sspec


def _scan_fwd(z, lw, kd, kk, bd, bsz, seq, name, ride=None):
    n = z.shape[0]
    nc = seq // CHUNK
    zspec, dspec, yspec, sspec = _scan_specs(nc, bsz)

    def body(r_ref, v_ref, kk_ref, lw_ref, kd_ref, bd_ref, y_ref, s_ref, st):
        @pl.when(pl.program_id(2) == 0)
        def _():
            st[...] = jnp.zeros_like(st)

        rev = pl.program_id(0) == 1
        lanes = [slice(h * 128, (h + 1) * 128) for h in range(PAIRS)]
        s0 = tuple(st[h] for h in range(PAIRS))
        ops = [tuple(ref[:, ln] for ln in lanes) for ref in (r_ref, v_ref, kk_ref, lw_ref, kd_ref, bd_ref)]
        y, se = _scan_chunk(s0, *ops, rev)
        for h, ln in enumerate(lanes):
            s_ref[h] = s0[h]
            y_ref[:, ln] = y[h]
            st[h] = se[h]

    return _pc(
        body, name=name, grid=(2, bsz, nc),
        in_specs=[zspec(0), zspec(2), zspec(0), dspec, dspec, dspec],
        out_specs=[yspec, sspec],
        out_shape=[_sds((2, n, RW_DIM)), _sds((2 * bsz * nc, PAIRS, 128, 128))],
        scratch=[pltpu.VMEM((PAIRS, 128, 128), f32)], sem=("parallel", "parallel", "arbitrary"), ride=ride,
    )(z, z, kk, lw, kd, bd)


def _scan_bwd(z, lw, kd, kk, bd, s_in, dy, bsz, seq, name, ride=None):
    n = z.shape[0]
    nc = seq // CHUNK
    zspec, dspec, yspec, sspec = _scan_specs(nc, bsz, flip=True)

    def body(r_ref, v_ref, kk_ref, lw_ref, kd_ref, bd_ref, s_ref, dy_ref,
             dr_ref, dv_ref, dkk_ref, dlw_ref, dkd_ref, dbd_ref, dst):
        @pl.when(pl.program_id(2) == 0)
        def _():
            dst[...] = jnp.zeros_like(dst)

        rev = pl.program_id(0) == 1
        lanes = [slice(h * 128, (h + 1) * 128) for h in range(PAIRS)]
        s0 = tuple(s_ref[h] for h in range(PAIRS))
        ops = [tuple(ref[:, ln] for ln in lanes) for ref in (r_ref, v_ref, kk_ref, lw_ref, kd_ref, bd_ref)]
        cot = (tuple(dy_ref[:, ln] for ln in lanes), tuple(dst[h] for h in range(PAIRS)))
        _, vjp = jax.vjp(functools.partial(_scan_chunk, rev=rev), s0, *ops)
        grads = vjp(cot)
        for h, ln in enumerate(lanes):
            dst[h] = grads[0][h]
            for o_ref, g in zip((dr_ref, dv_ref, dkk_ref, dlw_ref, dkd_ref, dbd_ref), grads[1:]):
                o_ref[:, ln] = g[h]

    return _pc(
        body, name=name, grid=(2, bsz, nc),
        in_specs=[zspec(0), zspec(2), zspec(0), dspec, dspec, dspec, sspec, zspec(0)],
        out_specs=[yspec, yspec, yspec, dspec, dspec, dspec],
        out_shape=[_sds((2, n, RW_DIM))] * 3 + [_sds((n, 2 * RW_DIM))] * 3,
        scratch=[pltpu.VMEM((PAIRS, 128, 128), f32)], sem=("parallel", "parallel", "arbitrary"), ride=ride,
    )(z, z, kk, lw, kd, bd, s_in, dy)


def _merge_fwd(x2, p, ya, yb, yc, gb, wb, wo, name):
    n = x2.shape[0]
    tm = min(256, n)

    def body(x_ref, pg_ref, ya_ref, yb_ref, yc_ref, gb_ref, wb_ref, wo_ref, o_ref):
        gates = _sigmoid(pg_ref[...] + gb_ref[...])
        merged = jnp.zeros((tm, D), f32)
        for k, y_ref in enumerate((ya_ref, yb_ref, yc_ref)):
            merged += gates[:, k * D:(k + 1) * D] * _bdot(y_ref[...], wb_ref[k])
        o_ref[...] = x_ref[...] + _bdot(merged, wo_ref[...])

    row = lambda w, b=0: pl.BlockSpec((tm, w), lambda i, b=b: (i, b))
    return _pc(
        body, name=name, grid=(n // tm,),
        in_specs=[row(D), row(3 * D, O_GATE // (3 * D)), row(512), row(512), row(512),
                  pl.BlockSpec((1, 3 * D), lambda i: (0, 0)), pl.BlockSpec((3, 512, D), lambda i: (0, 0, 0)),
                  pl.BlockSpec((D, D), lambda i: (0, 0))],
        out_specs=row(D), out_shape=_sds((n, D)), sem=("parallel",),
    )(x2, p, ya, yb, yc, gb, wb, wo)


def _merge_bwd(dx1, p, ya, yb, yc, gb, wb, wo, name):
    n = dx1.shape[0]
    tm = min(256, n)

    def body(dx_ref, pg_ref, ya_ref, yb_ref, yc_ref, gb_ref, wb_ref, wo_ref,
             dpg_ref, dya_ref, dyb_ref, dyc_ref, dt_ref, mg_ref, dgb_ref):
        gates = _sigmoid(pg_ref[...] + gb_ref[...])
        dmerged = _bdot(dx_ref[...], wo_ref[...], NT)
        merged = jnp.zeros((tm, D), f32)
        dpg = []
        for k, (y_ref, dy_ref) in enumerate(((ya_ref, dya_ref), (yb_ref, dyb_ref), (yc_ref, dyc_ref))):
            gk = gates[:, k * D:(k + 1) * D]
            tk = _bdot(y_ref[...], wb_ref[k])
            merged += gk * tk
            dpg.append(dmerged * tk * gk * (1.0 - gk))
            dtk = dmerged * gk
            dt_ref[:, k * D:(k + 1) * D] = dtk.astype(bf16)
            dy_ref[...] = _bdot(dtk, wb_ref[k], NT)
        dpg = jnp.concatenate(dpg, axis=1)
        dpg_ref[...] = dpg.astype(bf16)
        mg_ref[...] = merged.astype(bf16)
        part = jnp.sum(dpg, axis=0, keepdims=True)

        @pl.when(pl.program_id(0) == 0)
        def _():
            dgb_ref[...] = part

        @pl.when(pl.program_id(0) != 0)
        def _():
            dgb_ref[...] += part

    row = lambda w, b=0: pl.BlockSpec((tm, w), lambda i, b=b: (i, b))
    return _pc(
        body, name=name, grid=(n // tm,),
        in_specs=[row(D), row(3 * D, O_GATE // (3 * D)), row(512), row(512), row(512),
                  pl.BlockSpec((1, 3 * D), lambda i: (0, 0)), pl.BlockSpec((3, 512, D), lambda i: (0, 0, 0)),
                  pl.BlockSpec((D, D), lambda i: (0, 0))],
        out_specs=[row(3 * D), row(512), row(512), row(512), row(3 * D), row(D),
                   pl.BlockSpec((1, 3 * D), lambda i: (0, 0))],
        out_shape=[_sds((n, 3 * D), bf16), _sds((n, 512)), _sds((n, 512)), _sds((n, 512)), _sds((n, 3 * D), bf16),
                   _sds((n, D), bf16), _sds((1, 3 * D))],
        sem=("arbitrary",),
    )(dx1, p, ya, yb, yc, gb, wb, wo)


FF_T = 1408


def _ffn_fwd(x1, g, wg, wu, wd, name):
    n = x1.shape[0]
    tm = min(512, n)
    nf = D_FF // FF_T

    def body(x_ref, g_ref, wg_ref, wu_ref, wd_ref, o_ref, hs):
        j = pl.program_id(1)

        @pl.when(j == 0)
        def _():
            hs[...] = _rms(x_ref[...], g_ref[...]).astype(bf16)
            o_ref[...] = x_ref[...]

        a = _dot(hs[...], wg_ref[...], NN)
        b = _dot(hs[...], wu_ref[...], NN)
        o_ref[...] += _bdot(a * _sigmoid(a) * b, wd_ref[...])

    return _pc(
        body, name=name, grid=(n // tm, nf),
        in_specs=[pl.BlockSpec((tm, D), lambda i, j: (i, 0)), pl.BlockSpec((1, D), lambda i, j: (0, 0)),
                  pl.BlockSpec((D, FF_T), lambda i, j: (0, j)), pl.BlockSpec((D, FF_T), lambda i, j: (0, j)),
                  pl.BlockSpec((FF_T, D), lambda i, j: (j, 0))],
        out_specs=pl.BlockSpec((tm, D), lambda i, j: (i, 0)), out_shape=_sds((n, D)),
        scratch=[pltpu.VMEM((tm, D), bf16)], sem=("parallel", "arbitrary"),
    )(x1, g, wg, wu, wd)


def _ffn_bwd(dx2, x1, g, wg, wu, wd, name):
    n = x1.shape[0]
    tm = min(512, n)
    nf = D_FF // FF_T

    def body(dx_ref, x_ref, g_ref, wg_ref, wu_ref, wd_ref, dx1_ref, dg_ref, h_ref, da_ref, db_ref, hm_ref, acc):
        i, j = pl.program_id(0), pl.program_id(1)

        @pl.when(j == 0)
        def _():
            h_ref[...] = _rms(x_ref[...], g_ref[...]).astype(bf16)
            acc[...] = jnp.zeros_like(acc)

        @pl.when((i == 0) & (j == 0))
        def _():
            dg_ref[...] = jnp.zeros_like(dg_ref)

        h = h_ref[...]
        a = _dot(h, wg_ref[...], NN)
        b = _dot(h, wu_ref[...], NN)
        sg = _sigmoid(a)
        s = a * sg
        dhm = _bdot(dx_ref[...], wd_ref[...], NT)
        da = (dhm * b * (sg * (1.0 + a * (1.0 - sg)))).astype(bf16)
        db = (dhm * s).astype(bf16)
        da_ref[...] = da
        db_ref[...] = db
        hm_ref[...] = (s * b).astype(bf16)
        acc[...] += _dot(da, wg_ref[...], NT) + _dot(db, wu_ref[...], NT)

        @pl.when(j == nf - 1)
        def _():
            _, vjp = jax.vjp(_rms, x_ref[...], g_ref[...])
            dx, dg = vjp(acc[...])
            dx1_ref[...] = dx_ref[...] + dx
            dg_ref[...] += dg

    rowf = pl.BlockSpec((tm, FF_T), lambda i, j: (i, j))
    rowd = pl.BlockSpec((tm, D), lambda i, j: (i, 0))
    vec = pl.BlockSpec((1, D), lambda i, j: (0, 0))
    return _pc(
        body, name=name, grid=(n // tm, nf),
        in_specs=[rowd, rowd, vec, pl.BlockSpec((D, FF_T), lambda i, j: (0, j)),
                  pl.BlockSpec((D, FF_T), lambda i, j: (0, j)), pl.BlockSpec((FF_T, D), lambda i, j: (j, 0))],
        out_specs=[rowd, vec, rowd, rowf, rowf, rowf],
        out_shape=[_sds((n, D)), _sds((1, D)), _sds((n, D), bf16), _sds((n, D_FF), bf16), _sds((n, D_FF), bf16),
                   _sds((n, D_FF), bf16)],
        scratch=[pltpu.VMEM((tm, D), f32)], sem=("arbitrary", "arbitrary"),
    )(dx2, x1, g, wg, wu, wd)


def _loss_head(x2, g, tgt, name):
    n = x2.shape[0]
    tm = min(512, n)

    def f(x, gg, t):
        e = _rms(x, gg) - t
        return 0.5 * jnp.sum(jnp.mean(e * e, axis=-1, keepdims=True))

    def body(x_ref, g_ref, t_ref, l_ref, dx_ref, dg_ref):
        val, vjp = jax.vjp(f, x_ref[...], g_ref[...], t_ref[...])
        dx, dg, _ = vjp(jnp.ones((), f32))
        dx_ref[...] = dx

        @pl.when(pl.program_id(0) == 0)
        def _():
            l_ref[...] = jnp.zeros_like(l_ref)
            dg_ref[...] = jnp.zeros_like(dg_ref)

        l_ref[...] += val
        dg_ref[...] += dg

    rowd = pl.BlockSpec((tm, D), lambda i: (i, 0))
    return _pc(
        body, name=name, grid=(n // tm,),
        in_specs=[rowd, pl.BlockSpec((1, D), lambda i: (0, 0)), rowd],
        out_specs=[pl.BlockSpec((8, 128), lambda i: (0, 0)), rowd, pl.BlockSpec((1, D), lambda i: (0, 0))],
        out_shape=[_sds((8, 128)), _sds((n, D)), _sds((1, D))], sem=("arbitrary",),
    )(x2, g, tgt)


def _adamw(w, parts, m, v, name):
    nl, r, c = w.shape
    tr = r
    for cand in (1024, 512, 256, 128, 64, 32, 16, 8):
        if r % cand == 0 and cand * c * 4 <= 1024 * 1024:
            tr = cand
            break

    def body(*refs):
        w_ref, p_refs, (m_ref, v_ref, g_ref, d_ref, nm_ref, nv_ref) = refs[0], refs[1:1 + nl], refs[1 + nl:]

        def update(p_ref):
            gg = p_ref[0].astype(f32)
            for k in range(1, N_DEV):
                gg = gg + p_ref[k].astype(f32)
            g_ref[...] = gg
            nm = B1 * m_ref[...] + (1.0 - B1) * gg
            nv = B2 * v_ref[...] + (1.0 - B2) * (gg * gg)
            m_hat = nm / (1.0 - B1 ** STEP)
            v_hat = nv / (1.0 - B2 ** STEP)
            d_ref[...] = -LR * (m_hat / (jnp.sqrt(v_hat) + EPS) + WD * w_ref[...])
            nm_ref[...] = nm
            nv_ref[...] = nv

        for j in range(nl):
            pl.when(pl.program_id(0) == j)(functools.partial(update, p_refs[j]))

    spec = pl.BlockSpec((None, tr, c), lambda l, i: (l, i, 0))
    pspecs = [pl.BlockSpec((N_DEV, tr, c), lambda l, i, j=j: (0, jnp.where(l == j, i, 0), 0)) for j in range(nl)]
    return _pc(body, name=name, grid=(nl, r // tr), in_specs=[spec] + pspecs + [spec, spec], out_specs=[spec] * 4,
               out_shape=[_sds((nl, r, c))] * 4, sem=("arbitrary", "arbitrary"))(w, *parts, m, v)


def _peers():
    x, y, c = lax.axis_index("x"), lax.axis_index("y"), lax.axis_index("c")
    me = 4 * x + 2 * y + c
    peers = []
    for k in range(1, N_DEV):
        fx, fy, fc = (k >> 2) & 1, (k >> 1) & 1, k & 1
        peers.append(((1 - x) if fx else x, (1 - y) if fy else y, (1 - c) if fc else c))
    return me, peers


def _exchange(gathers, scatters, name):
    _, got = _pc(lambda: None, name=name, out_shape=[], ride=(gathers, scatters))()
    return got


SHARDED = {"w_in": 2, "gate_b": 2, "w_uq": 2, "w_ukv": 2, "rw_w0": 2, "rw_w2": 3, "rw_a0": 2, "rw_a2": 3, "rw_g2": 2,
           "w_branch": 3, "w_out": 1, "w_ffn_gate": 2, "w_ffn_up": 2, "w_ffn_down": 1}
GATHER_F32 = ("gate_b", "rw_w0", "rw_a0")
REPLICATED = ("attn_norm_g", "q_norm_g", "kv_norm_g", "sg_ln_g", "sg_ln_b", "sg_w", "sg_b", "rw_mu", "rw_k_k", "rw_k_a",
              "rw_r_k", "rw_ln_g", "rw_ln_b", "ffn_norm_g", "final_norm_g")
WEIGHTS = ("attn_norm_g", "w_in", "gate_b", "q_norm_g", "w_uq", "kv_norm_g", "w_ukv", "sg_ln_g", "sg_ln_b", "sg_w", "sg_b",
           "rw_mu", "rw_w0", "rw_w2", "rw_a0", "rw_a2", "rw_g2", "rw_k_k", "rw_k_a", "rw_r_k", "rw_ln_g", "rw_ln_b",
           "w_branch", "w_out", "ffn_norm_g", "w_ffn_gate", "w_ffn_up", "w_ffn_down", "final_norm_g")


REP_MAIN = tuple(k for k in REPLICATED if k != "attn_norm_g")
BIG = ("w_in", "w_branch", "w_out", "w_ffn_gate", "w_ffn_up", "w_ffn_down")
SMALL_BF = ("w_uq", "w_ukv", "rw_w2", "rw_a2", "rw_g2")
SMALL = SMALL_BF + GATHER_F32


def _pack128(blocks, names, dtype, lead=0, to=256):
    parts = [blocks[k].astype(dtype).reshape(blocks[k].shape[:lead] + (-1, 128)) for k in names]
    rows = sum(p.shape[lead] for p in parts)
    pad = -rows % to
    if pad:
        parts.append(jnp.zeros(parts[0].shape[:lead] + (pad, 128), dtype))
    return jnp.concatenate(parts, axis=lead)


def _unpack128(packed, shapes, names, lead=0):
    out, off = {}, 0
    for k in names:
        rows = 1
        for d in shapes[k]:
            rows *= d
        rows //= 128
        idx = (slice(None),) * lead + (slice(off, off + rows),)
        out[k] = packed[idx].reshape(packed.shape[:lead] + tuple(shapes[k]))
        off += rows
    return out


def _join_blocks(g, ax):
    shp = g.shape[1:]
    return jnp.moveaxis(g, 0, ax).reshape(shp[:ax] + (N_DEV * shp[ax],) + shp[ax + 1:])


def _split_blocks(full, ax):
    shp = full.shape
    return jnp.moveaxis(full.reshape(shp[:ax] + (N_DEV, shp[ax] // N_DEV) + shp[ax + 1:]), ax, 0)


def _w_in_padded(w):
    z = lambda n: jnp.zeros((w.shape[0], n), w.dtype)
    q, ckv, kr = w[:, 0:384], w[:, 384:640], w[:, 640:672]
    sg, rw, gate = w[:, 672:1696], w[:, 1696:3616], w[:, 3616:6688]
    return jnp.concatenate([gate, sg, rw, z(128), ckv, z(64), kr, z(32), q, z(P_W - O_MLA - MLA_W)], axis=1)


def _w_in_unpadded(g):
    return jnp.concatenate([g[:, O_Q:O_Q + 384], g[:, O_CKV:O_CKV + 256], g[:, O_SLAB + 64:O_SLAB + 96],
                            g[:, O_SG:O_SG + 1024], g[:, O_RW:O_RW + 1920], g[:, O_GATE:O_GATE + 3072]], axis=1)


REST = ("w_branch", "w_out", "w_ffn_gate", "w_ffn_up", "w_ffn_down")


def _rest_weights(full, l):
    return dict(wb=full["w_branch"][l], wo=full["w_out"][l], wg=full["w_ffn_gate"][l], wu=full["w_ffn_up"][l],
                wd=full["w_ffn_down"][l])


def _layer_weights(full, rep, l):
    w = {}
    w["w_in"] = _w_in_padded(full["w_in"][l])
    if full["w_branch"][l] is not None:
        w.update(_rest_weights(full, l))
    uq = full["w_uq"][l].reshape(Q_LORA, HEADS, QK_NOPE + QK_ROPE)
    w["wq"] = jnp.pad(uq, ((0, 0), (0, 0), (0, 32))).reshape(Q_LORA, HEADS * 128).astype(f32)
    ukv = full["w_ukv"][l].reshape(KV_LORA, HEADS, QK_NOPE + V_HEAD)
    wk = jnp.pad(ukv[:, :, :QK_NOPE], ((0, 0), (0, 0), (0, 64))).reshape(KV_LORA, HEADS * 128)
    w["wk"], w["wv"] = wk.astype(f32), ukv[:, :, QK_NOPE:].reshape(KV_LORA, HEADS * V_HEAD).astype(f32)
    bdiag = lambda t: jnp.concatenate([jnp.concatenate([t[0], jnp.zeros_like(t[0])], axis=1),
                                       jnp.concatenate([jnp.zeros_like(t[1]), t[1]], axis=1)], axis=0).astype(f32)
    w["w2"], w["a2"] = bdiag(full["rw_w2"][l]), bdiag(full["rw_a2"][l])
    w["g2"] = full["rw_g2"][l].astype(f32)
    w["w0"], w["a0"] = full["rw_w0"][l].reshape(1, 2 * RW_DIM), full["rw_a0"][l].reshape(1, 2 * RW_DIM)
    w["gate_b"] = full["gate_b"][l].reshape(1, 3 * D)
    row = lambda a: a.reshape(1, -1)
    for k in ("attn_norm_g", "q_norm_g", "kv_norm_g", "sg_ln_g", "sg_ln_b", "rw_k_k", "rw_k_a", "rw_ln_g", "rw_ln_b",
              "ffn_norm_g"):
        w[k] = row(rep[k][l])
    w["r_k"] = row(rep["rw_r_k"][l])
    w["mu"] = jnp.pad(row(rep["rw_mu"][l]), ((0, 0), (0, RW_W - 1920)))
    w["sg_w"] = [rep["sg_w"][l, k] for k in range(SG_GROUPS)]
    w["sg_bias"] = jnp.repeat(rep["sg_b"][l].T, SG_DIM // SG_GROUPS, axis=1)
    return w


def _riding(res, ride, got, key):
    if ride is None:
        return res
    got[key] = res[1]
    return res[0]


def _layer_fwd(x2, w, tabs, bsz, seq, l, rides=None, on_inproj=None):
    nm = lambda s: f"l{l}_{s}"
    n = x2.shape[0]
    tm = min(256, n)
    rides = rides or {}
    ride = lambda key: (rides[key], []) if key in rides else None
    got = {}
    p, h = _riding(_inproj_fwd(x2, w["attn_norm_g"], w["w_in"], nm("inproj"), ride("inproj")), ride("inproj"), got, "inproj")
    if on_inproj is not None:
        w.update(on_inproj(got["inproj"]))
    mla_rows = [(p, 256, O_CKV // 256), (p, 128, O_SLAB // 128), (p, 384, O_Q // 384), (tabs[0], 128, 0), (tabs[1], 128, 0)]
    mla_w = [w["q_norm_g"], w["kv_norm_g"], w["wq"], w["wk"], w["wv"]]
    q, k, v = _rowwise_fwd(nm("mla_proj"), _f_mla_proj, mla_rows, mla_w, [(1024, bf16), (1024, bf16), (512, bf16)], tm)
    ya = _riding(_attn_fwd(q, k, v, bsz, seq, nm("attn"), ride("attn")), ride("attn"), got, "attn")
    sg_rows = [(p, SG_DIM, O_SG // SG_DIM), (p, SG_DIM, O_SG // SG_DIM + 1)]
    sg_w = [w["sg_ln_g"], w["sg_ln_b"], w["sg_bias"]] + w["sg_w"]
    (yb,) = _rowwise_fwd(nm("sg"), _f_sg, sg_rows, sg_w, [(SG_DIM, f32)], SG_CHUNK)
    z = _shift_fwd(p, w["mu"], seq, nm("shift"))
    pre_rows = [(z, 512, 1), (z, 128, 12), (z, 128, 13), (z, 128, 14)]
    pre_w = [w["w0"], w["a0"], w["w2"], w["a2"], w["g2"], w["rw_k_k"], w["rw_k_a"]]
    lw, kd, kk, bd, g = _rowwise_fwd(nm("rw_pre"), _f_rw_pre, pre_rows, pre_w,
                                     [(1024, f32), (1024, f32), (512, f32), (1024, f32), (512, f32)], tm)
    y, s_in = _riding(_scan_fwd(z, lw, kd, kk, bd, bsz, seq, nm("scan"), ride("scan")), ride("scan"), got, "scan")
    post_rows = [(y[0], 512, 0), (y[1], 512, 0), (z, 512, 0), (z, 512, 2), (kd, 512, 0), (kd, 512, 1), (g, 512, 0)]
    post_w = [w["r_k"], w["rw_ln_g"], w["rw_ln_b"]]
    (yc,) = _rowwise_fwd(nm("rw_post"), _f_rw_post, post_rows, post_w, [(512, f32)], tm)
    x1 = _merge_fwd(x2, p, ya, yb, yc, w["gate_b"], w["wb"], w["wo"], nm("merge"))
    x3 = _ffn_fwd(x1, w["ffn_norm_g"], w["wg"], w["wu"], w["wd"], nm("ffn"))
    saved = dict(x=x2, p=p, h=h, q=q, k=k, v=v, ya=ya, yb=yb, z=z, lw=lw, kd=kd, kk=kk, bd=bd, g=g, y=y, s_in=s_in, yc=yc,
                 x1=x1, mla_rows=mla_rows, mla_w=mla_w, sg_rows=sg_rows, sg_w=sg_w, pre_rows=pre_rows, pre_w=pre_w,
                 post_rows=post_rows, post_w=post_w)
    return x3, saved, got


def _layer_bwd(dx3, w, sv, bsz, seq, l, rides=None):
    nm = lambda s: f"l{l}_{s}_bwd"
    n = dx3.shape[0]
    tm = min(256, n)
    g = {}
    rides = rides or {}
    ride = lambda key: rides[key](g) if key in rides else None
    got = {}
    dx1, g["ffn_norm_g"], h2, da, db, hm = _ffn_bwd(dx3, sv["x1"], w["ffn_norm_g"], w["wg"], w["wu"], w["wd"], nm("ffn"))
    g["wg"] = _matmul_tn(h2, da, nm("wg"))
    g["wu"] = _matmul_tn(h2, db, nm("wu"))
    g["wd"] = _matmul_tn(hm, dx3.astype(bf16), nm("wd"))
    dpg, dya, dyb, dyc, dt, mg, g["gate_b"] = _merge_bwd(dx1, sv["p"], sv["ya"], sv["yb"], sv["yc"], w["gate_b"], w["wb"],
                                                         w["wo"], nm("merge"))
    g["wo"] = _matmul_tn(mg, dx1.astype(bf16), nm("wo"))
    ys = (sv["ya"], sv["yb"], sv["yc"])
    g["wb"] = jnp.stack([_matmul_tn(ys[k].astype(bf16), dt[:, k * D:(k + 1) * D], nm(f"wb{k}")) for k in range(3)])
    (dy, dr_p, dv_p, dkd0, dkd1, dg_), (g["r_k"], g["rw_ln_g"], g["rw_ln_b"]) = _rowwise_bwd(
        nm("rw_post"), _f_rw_post, sv["post_rows"], sv["post_w"], [(dyc, 512, 0)], tm, [f32, None] + [f32] * 5)
    dkd_p = jnp.concatenate([dkd0, dkd1], axis=1)
    rd = ride("scan")
    dr_s, dv_s, dkk_s, dlw, dkd_s, dbd = _riding(
        _scan_bwd(sv["z"], sv["lw"], sv["kd"], sv["kk"], sv["bd"], sv["s_in"], dy, bsz, seq, nm("scan"), rd), rd, got, "scan")
    pre_cots = [(dlw, 1024, 0), (dkd_s + dkd_p, 1024, 0), (dkk_s[0] + dkk_s[1], 512, 0), (dbd, 1024, 0), (dg_, 512, 0)]
    (dk, dwl, dal, dgl), (g["w0"], g["a0"], g["w2"], g["a2"], g["g2"], g["rw_k_k"], g["rw_k_a"]) = _rowwise_bwd(
        nm("rw_pre"), _f_rw_pre, sv["pre_rows"], sv["pre_w"], pre_cots, tm, [f32] * 4)
    dz = jnp.concatenate([dr_s[0] + dr_s[1] + dr_p, dk, dv_s[0] + dv_s[1] + dv_p, dwl, dal, dgl,
                          jnp.zeros((n, RW_W - 1920), f32)], axis=1)
    dp_rw, g["mu"] = _shift_bwd(dz, sv["p"], w["mu"], seq, nm("shift"))
    (dp_su, dp_sv), (g["sg_ln_g"], g["sg_ln_b"], g["sg_bias"], *sgw) = _rowwise_bwd(
        nm("sg"), _f_sg, sv["sg_rows"], sv["sg_w"], [(dyb, SG_DIM, 0)], SG_CHUNK, [bf16, bf16])
    g["sg_w"] = jnp.stack(sgw)
    rd = ride("attn")
    dq, dk_, dv_ = _riding(_attn_bwd(sv["q"], sv["k"], sv["v"], sv["ya"], dya, bsz, seq, nm("attn"), rd), rd, got, "attn")
    (dp_ckv, dp_slab, dp_q), (g["q_norm_g"], g["kv_norm_g"], g["wq"], g["wk"], g["wv"]) = _rowwise_bwd(
        nm("mla_proj"), _f_mla_proj, sv["mla_rows"], sv["mla_w"], [(dq, 1024, 0), (dk_, 1024, 0), (dv_, 512, 0)], tm,
        [bf16, bf16, bf16, None, None])
    dp = jnp.concatenate([dpg, dp_su, dp_sv, dp_rw, dp_ckv, dp_slab, dp_q, jnp.zeros((n, P_W - O_MLA - MLA_W), bf16)],
                         axis=1)
    rd = ride("w_in")
    g["w_in"] = _riding(_matmul_tn(sv["h"], dp, nm("w_in"), rd), rd, got, "w_in")
    rd = ride("inproj")
    dx, g["attn_norm_g"] = _riding(_norm_matmul_bwd(dp, w["w_in"], sv["x"], w["attn_norm_g"], dx1, nm("inproj"), rd), rd, got,
                                   "inproj")
    return dx, g, got


def _layer_grads_to_full(g):
    o = {}
    if "w_in" in g:
        o["w_in"] = _w_in_unpadded(g["w_in"])
    o["w_uq"] = g["wq"].reshape(Q_LORA, HEADS, 128)[:, :, :QK_NOPE + QK_ROPE].reshape(Q_LORA, -1)
    gk = g["wk"].reshape(KV_LORA, HEADS, 128)[:, :, :QK_NOPE]
    gv = g["wv"].reshape(KV_LORA, HEADS, V_HEAD)
    o["w_ukv"] = jnp.concatenate([gk, gv], axis=2).reshape(KV_LORA, -1)
    unb = lambda t: jnp.stack([t[:LORA, :RW_DIM], t[LORA:, RW_DIM:]])
    o["rw_w2"], o["rw_a2"], o["rw_g2"] = unb(g["w2"]), unb(g["a2"]), g["g2"]
    o["rw_w0"], o["rw_a0"] = g["w0"].reshape(2, RW_DIM), g["a0"].reshape(2, RW_DIM)
    o["gate_b"] = g["gate_b"].reshape(3, D)
    o["w_branch"], o["w_out"] = g["wb"], g["wo"]
    o["w_ffn_gate"], o["w_ffn_up"], o["w_ffn_down"] = g["wg"], g["wu"], g["wd"]
    for k in ("attn_norm_g", "q_norm_g", "kv_norm_g", "sg_ln_g", "sg_ln_b", "rw_k_k", "rw_k_a", "rw_ln_g", "rw_ln_b",
              "ffn_norm_g"):
        if k in g:
            o[k] = g[k].reshape(-1)
    o["rw_r_k"] = g["r_k"].reshape(HEADS, RW_HEAD)
    o["rw_mu"] = g["mu"].reshape(-1)[:1920]
    o["sg_w"] = g["sg_w"]
    o["sg_b"] = g["sg_bias"].reshape(SG_CHUNK, SG_GROUPS, SG_DIM // SG_GROUPS).sum(axis=2).T
    return o


def _rope_tables(positions):
    inv = 1.0 / (10000.0 ** (jnp.arange(0, QK_ROPE, 2, dtype=f32) / QK_ROPE))
    ang = positions.astype(f32)[:, None] * inv
    cos, sin = jnp.cos(ang), jnp.sin(ang)
    n = positions.shape[0]
    c = jnp.concatenate([jnp.ones((n, 64), f32), cos, cos, jnp.zeros((n, 32), f32)], axis=1)
    s = jnp.concatenate([jnp.zeros((n, 64), f32), -sin, sin, jnp.zeros((n, 32), f32)], axis=1)
    return c, s


def _grad_parts(grad, name):
    return _split_blocks(grad, SHARDED[name] - 1).astype(bf16)


def _local_step(x, positions, full, rep, loss_target, blocks=None):
    bsz, seq, _ = x.shape
    n = bsz * seq
    x2 = x.reshape(n, D)
    tabs = _rope_tables(positions.reshape(n))
    join = lambda k, g: _join_blocks(g, SHARDED[k] - 1)
    rides, on_inproj = None, None
    if blocks is not None:
        rides = {"inproj": [blocks[k][0] for k in REST], "attn": [blocks["w_in"][1]], "scan": [blocks[k][1] for k in REST]}

        def on_inproj(got):
            for k, g in zip(REST, got):
                full[k][0] = join(k, g)
            return _rest_weights(full, 0)

    w0 = _layer_weights(full, rep, 0)
    x2, sv0, got = _layer_fwd(x2, w0, tabs, bsz, seq, 0, rides, on_inproj)
    if blocks is not None:
        full["w_in"][1] = join("w_in", got["attn"][0])
        for k, g in zip(REST, got["scan"]):
            full[k][1] = join(k, g)
    w1 = _layer_weights(full, rep, 1)
    x2, sv1, _ = _layer_fwd(x2, w1, tabs, bsz, seq, 1)
    loss, dx, dgf = _loss_head(x2, rep["final_norm_g"].reshape(1, D), loss_target.reshape(n, D), "loss_head")
    dx, g1, _ = _layer_bwd(dx, w1, sv1, bsz, seq, 1)
    grads1 = _layer_grads_to_full(g1)
    rides = None
    if blocks is not None:
        short = dict(w_branch="wb", w_out="wo", w_ffn_gate="wg", w_ffn_up="wu", w_ffn_down="wd")

        def beside_w_in(g):
            g0 = _layer_grads_to_full(g)
            both = {k: jnp.stack([g0[k], grads1[k]]) for k in g0}
            both["final_norm_g"] = dgf.reshape(D)
            split = {k: _split_blocks(both[k], SHARDED[k]) for k in SMALL}
            return [_pack128(both, REP_MAIN, f32)], [_pack128(split, SMALL, f32, lead=1)]

        rides = {"scan": lambda g: ([], [_grad_parts(grads1[k], k) for k in BIG]),
                 "attn": lambda g: ([], [_grad_parts(g[short[k]], k) for k in REST]),
                 "w_in": beside_w_in,
                 "inproj": lambda g: ([], [_grad_parts(_w_in_unpadded(g["w_in"]), "w_in")])}
    dx, g0, got = _layer_bwd(dx, w0, sv0, bsz, seq, 0, rides)
    grads0 = _layer_grads_to_full(g0)
    grads = {k: [grads0[k], grads1[k]] for k in grads0}
    grads["final_norm_g"] = dgf.reshape(D)
    parts = {}
    if blocks is not None:
        parts = {k: [None, p] for k, p in zip(BIG, got["scan"])}
        for k, p in zip(REST, got["attn"]):
            parts[k][0] = p
        parts["replicated"], parts["small"] = got["w_in"]
        (parts["w_in"][0],) = got["inproj"]
    return loss[0, 0], dx.reshape(bsz, seq, D), grads, parts


def kernel(x, positions, attn_norm_g, w_in, gate_b, q_norm_g, w_uq, kv_norm_g, w_ukv, sg_ln_g, sg_ln_b, sg_w, sg_b, rw_mu, rw_w0, rw_w2, rw_a0, rw_a2, rw_g2, rw_k_k, rw_k_a, rw_r_k, rw_ln_g, rw_ln_b, w_branch, w_out, ffn_norm_g, w_ffn_gate, w_ffn_up, w_ffn_down, final_norm_g, loss_target, m_attn_norm_g, m_w_in, m_gate_b, m_q_norm_g, m_w_uq, m_kv_norm_g, m_w_ukv, m_sg_ln_g, m_sg_ln_b, m_sg_w, m_sg_b, m_rw_mu, m_rw_w0, m_rw_w2, m_rw_a0, m_rw_a2, m_rw_g2, m_rw_k_k, m_rw_k_a, m_rw_r_k, m_rw_ln_g, m_rw_ln_b, m_w_branch, m_w_out, m_ffn_norm_g, m_w_ffn_gate, m_w_ffn_up, m_w_ffn_down, m_final_norm_g, v_attn_norm_g, v_w_in, v_gate_b, v_q_norm_g, v_w_uq, v_kv_norm_g, v_w_ukv, v_sg_ln_g, v_sg_ln_b, v_sg_w, v_sg_b, v_rw_mu, v_rw_w0, v_rw_w2, v_rw_a0, v_rw_a2, v_rw_g2, v_rw_k_k, v_rw_k_a, v_rw_r_k, v_rw_ln_g, v_rw_ln_b, v_w_branch, v_w_out, v_ffn_norm_g, v_w_ffn_gate, v_w_ffn_up, v_w_ffn_down, v_final_norm_g):
    args = locals()
    wts = {k: args[k] for k in WEIGHTS}
    mom_m = {k: args["m_" + k] for k in WEIGHTS}
    mom_v = {k: args["v_" + k] for k in WEIGHTS}
    shapes = {k: wts[k].shape for k in WEIGHTS}
    blocks = {k: wts[k].astype(bf16) for k in BIG}
    got = _exchange([blocks["w_in"][0], _pack128(wts, SMALL_BF, bf16), _pack128(wts, GATHER_F32, f32)], [], "gather_first")
    small = {**_unpack128(got[1], shapes, SMALL_BF, lead=1), **_unpack128(got[2], shapes, GATHER_F32, lead=1)}
    full = {k: list(_join_blocks(small[k], SHARDED[k])) for k in SMALL}
    full["w_in"] = [_join_blocks(got[0], SHARDED["w_in"] - 1), None]
    full.update({k: [None, None] for k in REST})
    rep = {k: wts[k] for k in REPLICATED}
    loss, grad_x, grads, parts = _local_step(x, positions, full, rep, loss_target, blocks)
    loss = lax.psum(loss, ("x", "y", "c"))
    last = ("attn_norm_g",)
    (last_parts,) = _exchange([_pack128({"attn_norm_g": jnp.stack(grads["attn_norm_g"])}, last, f32, to=16)], [],
                              "exchange_last")
    gw, delta, new_m, new_v = {}, {}, {}, {}
    for k in BIG:
        three = lambda a, k=k: a.reshape(a.shape[0], -1, shapes[k][-1])
        res = _adamw(three(wts[k]), [three(p) for p in parts[k]], three(mom_m[k]), three(mom_v[k]), f"adamw_{k}")
        gw[k], delta[k], new_m[k], new_v[k] = (t.reshape(shapes[k]) for t in res)
    for names, got, to in ((SMALL, parts["small"], 256), (REP_MAIN, parts["replicated"], 256), (last, last_parts, 16)):
        pk = lambda dct: _pack128(dct, names, f32, to=to)[None]
        res = _adamw(pk(wts), [got], pk(mom_m), pk(mom_v), f"adamw_{names[0]}")
        for dst, t in zip((gw, delta, new_m, new_v), res):
            dst.update(_unpack128(t[0], shapes, names))
    return (loss, grad_x, *[gw[k] for k in WEIGHTS], *[delta[k] for k in WEIGHTS], *[new_m[k] for k in WEIGHTS],
            *[new_v[k] for k in WEIGHTS])
```

```python
import functools

import jax
import jax.numpy as jnp
from jax import lax
from jax.experimental import pallas as pl
from jax.experimental.pallas import tpu as pltpu

f32 = jnp.float32
bf16 = jnp.bfloat16
HI = lax.Precision.HIGHEST
NN, NT, TN = ((1,), (0,)), ((1,), (1,)), ((0,), (0,))

N_DEV = 8
D = 1024
HEADS = 8
Q_LORA, KV_LORA, QK_NOPE, QK_ROPE, V_HEAD = 384, 256, 64, 32, 64
SG_DIM, SG_CHUNK, SG_GROUPS = 512, 128, 8
RW_DIM, RW_HEAD, LORA = 512, 64, 64
D_FF = 2816
N_IN = 6688
NORM_EPS, LN_EPS, GN_EPS = 1e-6, 1e-5, 64e-5
ATT_SCALE = (QK_NOPE + QK_ROPE) ** -0.5
P_W = 7168
O_GATE, O_SG, O_RW, O_MLA = 0, 3072, 4096, 6144
RW_W = 2048
MLA_W = 768
O_CKV, O_SLAB, O_Q = O_MLA, O_MLA + 256, O_MLA + 384
CHUNK = 128
VMEM_LIMIT = 56 * 1024 * 1024

B1, B2, LR, EPS, WD, STEP = 0.9, 0.999, 0.001, 1e-8, 0.01, 10


def _pc(body, *, name, out_shape, grid=(), in_specs=(), out_specs=(), scratch=(), sem=None, ride=None):
    params = pltpu.CompilerParams(dimension_semantics=sem, vmem_limit_bytes=VMEM_LIMIT)
    if ride is None:
        return pl.pallas_call(body, out_shape=out_shape, grid=grid, in_specs=in_specs, out_specs=out_specs,
                              scratch_shapes=scratch, compiler_params=params, name=name, interpret=False)
    gathers, scatters = ride
    moved = list(gathers) + list(scatters)
    ng, nx = len(gathers), len(moved)
    single = not isinstance(out_shape, (list, tuple))
    outs = [out_shape] if single else list(out_shape)
    ospecs = [out_specs] if single else list(out_specs)
    n_in, n_out, n_scr = len(in_specs), len(outs), len(scratch)
    per = N_DEV - 1

    def riding(*refs):
        ins, xin = refs[:n_in], refs[n_in:n_in + nx]
        outs_r, xout = refs[n_in + nx:n_in + nx + n_out], refs[n_in + nx + n_out:n_in + 2 * nx + n_out]
        own = refs[n_in + 2 * nx + n_out:n_in + 2 * nx + n_out + n_scr]
        send_sems, recv_sems, local_sems = refs[n_in + 2 * nx + n_out + n_scr:]

        def copies():
            me, peers = _peers()
            cps = []
            for a in range(nx):
                whole = a < ng
                cps.append(pltpu.make_async_copy(xin[a] if whole else xin[a].at[me], xout[a].at[me], local_sems.at[a]))
                for k, peer in enumerate(peers):
                    dev = 4 * peer[0] + 2 * peer[1] + peer[2]
                    cps.append(pltpu.make_async_remote_copy(
                        src_ref=xin[a] if whole else xin[a].at[dev], dst_ref=xout[a].at[me],
                        send_sem=send_sems.at[a * per + k], recv_sem=recv_sems.at[a * per + k], device_id=peer,
                        device_id_type=pl.DeviceIdType.MESH))
            return cps

        if not grid:
            for cp in copies():
                cp.start()
            body(*ins, *outs_r, *own)
            for cp in copies():
                cp.wait()
            return
        ids = [pl.program_id(a) for a in range(len(grid))]
        first = functools.reduce(jnp.logical_and, [i == 0 for i in ids])
        last = functools.reduce(jnp.logical_and, [i == g - 1 for i, g in zip(ids, grid)])

        @pl.when(first)
        def _():
            for cp in copies():
                cp.start()

        body(*ins, *outs_r, *own)

        @pl.when(last)
        def _():
            for cp in copies():
                cp.wait()

    anyspec = pl.BlockSpec(memory_space=pl.ANY)
    call = pl.pallas_call(
        riding, grid=grid, in_specs=list(in_specs) + [anyspec] * nx, out_specs=ospecs + [anyspec] * nx,
        out_shape=outs + [_sds((N_DEV,) + a.shape, a.dtype) for a in gathers] + [_sds(a.shape, a.dtype) for a in scatters],
        scratch_shapes=list(scratch) + [pltpu.SemaphoreType.DMA((nx * per,)), pltpu.SemaphoreType.DMA((nx * per,)),
                                        pltpu.SemaphoreType.DMA((nx,))],
        compiler_params=params, name=name, interpret=False)

    def run(*args):
        res = call(*args, *moved)
        own = res[0] if single else list(res[:n_out])
        return own, list(res[n_out:])

    return run


def _sds(shape, dtype=f32):
    return jax.ShapeDtypeStruct(tuple(shape), dtype)


def _dot(a, b, dims, precision=None):
    return lax.dot_general(a, b, (dims, ((), ())), preferred_element_type=f32, precision=precision)


def _bdot(a, b, dims=NN):
    return _dot(a.astype(bf16), b.astype(bf16), dims)


@jax.custom_vjp
def _mm(a, w):
    return _bdot(a, w, NN)


def _mm_fwd(a, w):
    return _bdot(a, w, NN), (a, w)


def _mm_bwd(res, g):
    a, w = res
    return _bdot(g, w, NT), _bdot(a, g, TN)


_mm.defvjp(_mm_fwd, _mm_bwd)


@jax.custom_vjp
def _mm_nt(a, b):
    return _bdot(a, b, NT)


def _mm_nt_fwd(a, b):
    return _bdot(a, b, NT), (a, b)


def _mm_nt_bwd(res, g):
    a, b = res
    return _bdot(g, b, NN), _bdot(g, a, TN)


_mm_nt.defvjp(_mm_nt_fwd, _mm_nt_bwd)


@jax.custom_vjp
def _mm_tn(a, b):
    return _bdot(a, b, TN)


def _mm_tn_fwd(a, b):
    return _bdot(a, b, TN), (a, b)


def _mm_tn_bwd(res, g):
    a, b = res
    return _bdot(b, g, NT), _bdot(a, g, NN)


_mm_tn.defvjp(_mm_tn_fwd, _mm_tn_bwd)


def _rms(x, g):
    return x * lax.rsqrt(jnp.mean(x * x, axis=-1, keepdims=True) + NORM_EPS) * g


def _sigmoid(x):
    return 1.0 / (1.0 + jnp.exp(-x))


def _gelu(x):
    return 0.5 * x * (1.0 + jnp.tanh(0.7978845608028654 * (x + 0.044715 * x * x * x)))


def _softplus(x):
    return jnp.maximum(x, 0.0) + jnp.log(1.0 + jnp.exp(-jnp.abs(x)))


@jax.custom_vjp
def _group_sum(x):
    w = x.shape[-1]
    r = lax.broadcasted_iota(jnp.int32, (w, w), 0) // RW_HEAD
    c = lax.broadcasted_iota(jnp.int32, (w, w), 1) // RW_HEAD
    ones = (r == c).astype(bf16)
    hi = x.astype(bf16)
    lo = (x - hi.astype(f32)).astype(bf16)
    return _dot(jnp.concatenate([hi, lo], axis=1), jnp.concatenate([ones, ones], axis=0), NN)


_group_sum.defvjp(lambda x: (_group_sum(x), None), lambda _, g: (_group_sum(g),))


@jax.custom_vjp
def _swap(x):
    w = x.shape[-1]
    lane = lax.broadcasted_iota(jnp.int32, x.shape, 1) % 128
    lo = (lane >= 64) & (lane < 80)
    hi = (lane >= 80) & (lane < 96)
    return jnp.where(lo, pltpu.roll(x, w - 16, 1), jnp.where(hi, pltpu.roll(x, 16, 1), 0.0))


_swap.defvjp(lambda x: (_swap(x), None), lambda _, g: (_swap(g),))


def _rope(x, c, s):
    return x * c + _swap(x) * s


def _row_spec(tm, width, blk):
    return pl.BlockSpec((tm, width), lambda i, blk=blk: (i, blk))


def _full_spec(a):
    nd = a.ndim
    return pl.BlockSpec(a.shape, lambda i, nd=nd: (0,) * nd)


def _rowwise_fwd(name, f, rows, weights, outs, tm):
    n = rows[0][0].shape[0]
    nr, nw = len(rows), len(weights)

    def body(*refs):
        vals = [r[...].astype(f32) for r in refs[:nr + nw]]
        res = f(*vals)
        for o_ref, o in zip(refs[nr + nw:], res):
            o_ref[...] = o.astype(o_ref.dtype)

    return _pc(
        body, name=name, grid=(n // tm,),
        in_specs=[_row_spec(tm, w, b) for _, w, b in rows] + [_full_spec(w) for w in weights],
        out_specs=[_row_spec(tm, w, 0) for w, _ in outs],
        out_shape=[_sds((n, w), dt) for w, dt in outs], sem=("parallel",),
    )(*[a for a, _, _ in rows], *weights)


def _rowwise_bwd(name, f, rows, weights, cots, tm, drows):
    n = rows[0][0].shape[0]
    nr, nw, nc = len(rows), len(weights), len(cots)
    want = [k for k, dt in enumerate(drows) if dt is not None]

    def body(*refs):
        vals = [r[...].astype(f32) for r in refs[:nr + nw]]
        cot = tuple(r[...].astype(f32) for r in refs[nr + nw:nr + nw + nc])
        _, vjp = jax.vjp(f, *vals)
        grads = vjp(cot)
        outs = refs[nr + nw + nc:]
        for o_ref, k in zip(outs[:len(want)], want):
            o_ref[...] = grads[k].astype(o_ref.dtype)
        first = pl.program_id(0) == 0
        for o_ref, g in zip(outs[len(want):], grads[nr:]):
            @pl.when(first)
            def _(o_ref=o_ref, g=g):
                o_ref[...] = g

            @pl.when(jnp.logical_not(first))
            def _(o_ref=o_ref, g=g):
                o_ref[...] += g

    res = _pc(
        body, name=name, grid=(n // tm,),
        in_specs=[_row_spec(tm, w, b) for _, w, b in rows] + [_full_spec(w) for w in weights]
        + [_row_spec(tm, w, b) for _, w, b in cots],
        out_specs=[_row_spec(tm, rows[k][1], 0) for k in want] + [_full_spec(w) for w in weights],
        out_shape=[_sds((n, rows[k][1]), drows[k]) for k in want] + [_sds(w.shape) for w in weights],
        sem=("arbitrary",),
    )(*[a for a, _, _ in rows], *weights, *[a for a, _, _ in cots])
    return res[:len(want)], res[len(want):]


def _inproj_fwd(x2, g, w, name, ride=None):
    n = x2.shape[0]
    tm, tn = min(512, n), 1024

    def body(x_ref, g_ref, w_ref, p_ref, h_ref):
        @pl.when(pl.program_id(1) == 0)
        def _():
            h_ref[...] = _rms(x_ref[...], g_ref[...]).astype(bf16)

        p_ref[...] = jnp.dot(h_ref[...], w_ref[...], preferred_element_type=f32)

    return _pc(
        body, name=name, grid=(n // tm, P_W // tn),
        in_specs=[pl.BlockSpec((tm, D), lambda i, j: (i, 0)), pl.BlockSpec((1, D), lambda i, j: (0, 0)),
                  pl.BlockSpec((D, tn), lambda i, j: (0, j))],
        out_specs=[pl.BlockSpec((tm, tn), lambda i, j: (i, j)), pl.BlockSpec((tm, D), lambda i, j: (i, 0))],
        out_shape=[_sds((n, P_W)), _sds((n, D), bf16)], sem=("parallel", "arbitrary"), ride=ride,
    )(x2, g, w)


def _norm_matmul_bwd(dy, w, x2, g, dres, name, ride=None):
    n, k = dy.shape
    tm = min(512, n)
    tk = 1024 if k % 1024 == 0 else 1408
    nk = k // tk

    def body(dy_ref, w_ref, x_ref, g_ref, dr_ref, dx_ref, dg_ref, acc):
        i, j = pl.program_id(0), pl.program_id(1)

        @pl.when(j == 0)
        def _():
            acc[...] = jnp.zeros_like(acc)

        @pl.when((i == 0) & (j == 0))
        def _():
            dg_ref[...] = jnp.zeros_like(dg_ref)

        acc[...] += _dot(dy_ref[...], w_ref[...], NT)

        @pl.when(j == nk - 1)
        def _():
            _, vjp = jax.vjp(_rms, x_ref[...], g_ref[...])
            dx, dg = vjp(acc[...])
            dx_ref[...] = dr_ref[...] + dx
            dg_ref[...] += dg

    return _pc(
        body, name=name, grid=(n // tm, nk),
        in_specs=[pl.BlockSpec((tm, tk), lambda i, j: (i, j)), pl.BlockSpec((D, tk), lambda i, j: (0, j)),
                  pl.BlockSpec((tm, D), lambda i, j: (i, 0)), pl.BlockSpec((1, D), lambda i, j: (0, 0)),
                  pl.BlockSpec((tm, D), lambda i, j: (i, 0))],
        out_specs=[pl.BlockSpec((tm, D), lambda i, j: (i, 0)), pl.BlockSpec((1, D), lambda i, j: (0, 0))],
        out_shape=[_sds((n, D)), _sds((1, D))], scratch=[pltpu.VMEM((tm, D), f32)], sem=("arbitrary", "arbitrary"),
        ride=ride,
    )(dy, w, x2, g, dres)


def _matmul_tn(a, g, name, ride=None):
    n, k = a.shape
    m = g.shape[1]
    tr = min(512, n)
    tk = k if k <= 1024 else 1408
    tn = m if m <= 1024 else (1024 if m % 1024 == 0 else 1408)
    nr = n // tr

    def body(a_ref, g_ref, o_ref):
        @pl.when(pl.program_id(2) == 0)
        def _():
            o_ref[...] = jnp.zeros_like(o_ref)

        o_ref[...] += _dot(a_ref[...], g_ref[...], TN)

    return _pc(
        body, name=name, grid=(k // tk, m // tn, nr),
        in_specs=[pl.BlockSpec((tr, tk), lambda i, j, r: (r, i)), pl.BlockSpec((tr, tn), lambda i, j, r: (r, j))],
        out_specs=pl.BlockSpec((tk, tn), lambda i, j, r: (i, j)),
        out_shape=_sds((k, m)), sem=("parallel", "parallel", "arbitrary"), ride=ride,
    )(a, g)


def _f_mla_proj(ckv, slab, pq, c, s, qg, kg, wq, wk, wv):
    c8, s8 = jnp.concatenate([c] * HEADS, axis=1), jnp.concatenate([s] * HEADS, axis=1)
    q = _rope(_mm(_rms(pq, qg), wq), c8, s8)
    cn = _rms(ckv, kg)
    k = _mm(cn, wk) + jnp.concatenate([_rope(slab, c, s)] * HEADS, axis=1)
    return q, k, _mm(cn, wv)


def _attn_fwd(q, k, v, bsz, seq, name, ride=None):
    n = q.shape[0]
    tq = min(256, seq)
    nq = seq // tq

    def body(q_ref, k_ref, v_ref, o_ref):
        lane = lax.broadcasted_iota(jnp.int32, (tq, 128), 1) < 64
        vv = v_ref[...]
        two = range(2)
        s = [_dot(q_ref[:, h * 128:(h + 1) * 128], k_ref[:, h * 128:(h + 1) * 128], NT) * ATT_SCALE for h in two]
        e = [jnp.exp(s[h] - jnp.max(s[h], axis=-1, keepdims=True)) for h in two]
        p = [(e[h] / jnp.sum(e[h], axis=-1, keepdims=True)).astype(bf16) for h in two]
        outs = [_dot(p[h], vv, NN) for h in two]
        o_ref[...] = jnp.where(lane, outs[0], outs[1])

    return _pc(
        body, name=name, grid=(bsz, HEADS // 2, nq),
        in_specs=[pl.BlockSpec((tq, 256), lambda b, h, i: (b * nq + i, h)),
                  pl.BlockSpec((seq, 256), lambda b, h, i: (b, h)),
                  pl.BlockSpec((seq, 128), lambda b, h, i: (b, h))],
        out_specs=pl.BlockSpec((tq, 128), lambda b, h, i: (b * nq + i, h)),
        out_shape=_sds((n, HEADS * V_HEAD)), sem=("parallel", "parallel", "parallel"), ride=ride,
    )(q, k, v)


def _attn_bwd(q, k, v, o, do, bsz, seq, name, ride=None):
    n = q.shape[0]
    tq = min(256, seq)
    nq = seq // tq

    def body(q_ref, k_ref, v_ref, o_ref, do_ref, dq_ref, dk_ref, dv_ref):
        @pl.when(pl.program_id(2) == 0)
        def _():
            dk_ref[...] = jnp.zeros_like(dk_ref)
            dv_ref[...] = jnp.zeros_like(dv_ref)

        lane = lax.broadcasted_iota(jnp.int32, (tq, 128), 1) < 64
        vv = v_ref[...]
        for h in range(2):
            qh, kh = q_ref[:, h * 128:(h + 1) * 128], k_ref[:, h * 128:(h + 1) * 128]
            s = _dot(qh, kh, NT) * ATT_SCALE
            e = jnp.exp(s - jnp.max(s, axis=-1, keepdims=True))
            p = e / jnp.sum(e, axis=-1, keepdims=True)
            doh = jnp.where(lane if h == 0 else jnp.logical_not(lane), do_ref[...], 0.0)
            delta = jnp.sum(doh * o_ref[...], axis=-1, keepdims=True)
            dob = doh.astype(bf16)
            dp = _dot(dob, vv, NT)
            ds = (p * (dp - delta) * ATT_SCALE).astype(bf16)
            dq_ref[:, h * 128:(h + 1) * 128] = _dot(ds, kh, NN)
            dk_ref[:, h * 128:(h + 1) * 128] += _dot(ds, qh, TN)
            dv_ref[...] += _dot(p.astype(bf16), dob, TN)

    return _pc(
        body, name=name, grid=(bsz, HEADS // 2, nq),
        in_specs=[pl.BlockSpec((tq, 256), lambda b, h, i: (b * nq + i, h)),
                  pl.BlockSpec((seq, 256), lambda b, h, i: (b, h)),
                  pl.BlockSpec((seq, 128), lambda b, h, i: (b, h)),
                  pl.BlockSpec((tq, 128), lambda b, h, i: (b * nq + i, h)),
                  pl.BlockSpec((tq, 128), lambda b, h, i: (b * nq + i, h))],
        out_specs=[pl.BlockSpec((tq, 256), lambda b, h, i: (b * nq + i, h)),
                   pl.BlockSpec((seq, 256), lambda b, h, i: (b, h)),
                   pl.BlockSpec((seq, 128), lambda b, h, i: (b, h))],
        out_shape=[_sds((n, HEADS * 128)), _sds((n, HEADS * 128)), _sds((n, HEADS * V_HEAD))],
        sem=("parallel", "parallel", "arbitrary"), ride=ride,
    )(q, k, v, o, do)


def _f_sg(pu, pv, lg, lb, bias, *ws):
    u, vv = _gelu(pu), _gelu(pv)
    mu = jnp.mean(vv, axis=-1, keepdims=True)
    d = vv - mu
    vv = d * lax.rsqrt(jnp.mean(d * d, axis=-1, keepdims=True) + LN_EPS) * lg + lb
    group = lax.broadcasted_iota(jnp.int32, (SG_CHUNK, SG_DIM), 1) // (SG_DIM // SG_GROUPS)
    mixed = bias
    for k, w in enumerate(ws):
        mixed = mixed + jnp.where(group == k, _mm(w, vv), 0.0)
    return (u * mixed,)


def _shift_mean(a, prev_row, next_row):
    t = a.shape[0]
    row = lax.broadcasted_iota(jnp.int32, a.shape, 0)
    prev = jnp.where(row == 0, prev_row, pltpu.roll(a, 1, 0))
    nxt = jnp.where(row == t - 1, next_row, pltpu.roll(a, t - 1, 0))
    return 0.5 * (prev + nxt)


def _halo_specs(tm, width, blk, nblk8):
    h = tm // 8
    return [pl.BlockSpec((tm, width), lambda i: (i, blk)),
            pl.BlockSpec((8, width), lambda i: (jnp.maximum(i * h - 1, 0), blk)),
            pl.BlockSpec((8, width), lambda i: (jnp.minimum((i + 1) * h, nblk8 - 1), blk))]


def _edge_rows(i, tm, seq, pv_ref, nx_ref, scale=None):
    first = (i * tm) % seq == 0
    last = ((i + 1) * tm) % seq == 0
    pv, nx = pv_ref[7:8, :], nx_ref[0:1, :]
    if scale is not None:
        pv, nx = pv * scale, nx * scale
    return jnp.where(first, 0.0, pv), jnp.where(last, 0.0, nx)


def _shift_fwd(p, mu, seq, name):
    n = p.shape[0]
    tm = min(256, seq)
    blk = O_RW // RW_W

    def body(x_ref, pv_ref, nx_ref, mu_ref, z_ref):
        x = x_ref[...]
        pv, nx = _edge_rows(pl.program_id(0), tm, seq, pv_ref, nx_ref)
        z_ref[...] = x + mu_ref[...] * (_shift_mean(x, pv, nx) - x)

    return _pc(
        body, name=name, grid=(n // tm,),
        in_specs=_halo_specs(tm, RW_W, blk, n // 8) + [pl.BlockSpec((1, RW_W), lambda i: (0, 0))],
        out_specs=pl.BlockSpec((tm, RW_W), lambda i: (i, 0)), out_shape=_sds((n, RW_W)), sem=("parallel",),
    )(p, p, p, mu)


def _shift_bwd(dz, p, mu, seq, name):
    n = p.shape[0]
    tm = min(256, seq)
    blk = O_RW // RW_W

    def body(dz_ref, dpv_ref, dnx_ref, x_ref, pv_ref, nx_ref, mu_ref, dx_ref, dmu_ref):
        i = pl.program_id(0)
        mu_v = mu_ref[...]
        dzv = dz_ref[...]
        m = dzv * mu_v
        mpv, mnx = _edge_rows(i, tm, seq, dpv_ref, dnx_ref, mu_v)
        dx_ref[...] = (dzv - m + _shift_mean(m, mpv, mnx)).astype(dx_ref.dtype)
        x = x_ref[...]
        pv, nx = _edge_rows(i, tm, seq, pv_ref, nx_ref)
        part = jnp.sum(dzv * (_shift_mean(x, pv, nx) - x), axis=0, keepdims=True)

        @pl.when(i == 0)
        def _():
            dmu_ref[...] = part

        @pl.when(i != 0)
        def _():
            dmu_ref[...] += part

    return _pc(
        body, name=name, grid=(n // tm,),
        in_specs=_halo_specs(tm, RW_W, 0, n // 8) + _halo_specs(tm, RW_W, blk, n // 8)
        + [pl.BlockSpec((1, RW_W), lambda i: (0, 0))],
        out_specs=[pl.BlockSpec((tm, RW_W), lambda i: (i, 0)), pl.BlockSpec((1, RW_W), lambda i: (0, 0))],
        out_shape=[_sds((n, RW_W), bf16), _sds((1, RW_W))], sem=("arbitrary",),
    )(dz, dz, dz, p, p, p, mu)


def _f_rw_pre(k, wl, al, gl, w0, a0, w2, a2, g2, k_k, k_a):
    w = w0 + _mm(jnp.tanh(wl), w2)
    lw = -jnp.exp(-_softplus(-w) - 0.5)
    a = _sigmoid(a0 + _mm(al, a2))
    g = _mm(_sigmoid(gl), g2)
    kkr = k * k_k
    kk = kkr / jnp.maximum(jnp.sqrt(_group_sum(kkr * kkr)), 1e-12)
    two = lambda t: jnp.concatenate([t, t], axis=1)
    kd = two(k) * (1.0 + (a - 1.0) * two(k_a))
    bd = two(kk) * a
    return lw, kd, kk, bd, g


def _f_rw_post(y0, y1, r, v, kd0, kd1, g, r_k, ln_g, ln_b):
    y = y0 + y1
    mean = _group_sum(y) * (1.0 / RW_HEAD)
    d = y - mean
    var = _group_sum(d * d) * (1.0 / RW_HEAD)
    yn = d * lax.rsqrt(var + GN_EPS) * ln_g + ln_b
    bonus = _group_sum(r * (kd0 + kd1) * r_k)
    return ((yn + bonus * v) * g,)


@jax.custom_vjp
def _tri_inv(mats):
    c = mats[0].shape[0]
    row = lax.broadcasted_iota(jnp.int32, (c, c), 0)
    col = lax.broadcasted_iota(jnp.int32, (c, c), 1)
    eye = (row == col).astype(f32)
    blk = lambda b: (row // b) == (col // b)
    ld = [jnp.where(blk(8), a, 0.0) for a in mats]
    l2 = [_bdot(x, x) for x in ld]
    l4 = [_bdot(x, x) for x in l2]
    t = [_bdot(eye - x, eye + y) for x, y in zip(ld, l2)]
    t = [_bdot(x, eye + y) for x, y in zip(t, l4)]
    b = 8
    while b < c:
        sub = blk(2 * b) & jnp.logical_not(blk(b))
        p = [_bdot(x, jnp.where(sub, a, 0.0)) for x, a in zip(t, mats)]
        t = [x - _bdot(y, x) for x, y in zip(t, p)]
        b *= 2
    return tuple(t)


def _tri_inv_fwd(mats):
    t = _tri_inv(mats)
    return t, t


def _tri_inv_bwd(ts, gs):
    p = [_bdot(t, g, TN) for t, g in zip(ts, gs)]
    return (tuple(-_bdot(x, t, NT) for x, t in zip(p, ts)),)


_tri_inv.defvjp(_tri_inv_fwd, _tri_inv_bwd)


@jax.custom_vjp
def _tri_inv_saved(mats, ts):
    return ts


_tri_inv_saved.defvjp(lambda mats, ts: (ts, ts),
                      lambda ts, gs: (_tri_inv_bwd(ts, gs)[0], tuple(jnp.zeros_like(t) for t in ts)))


def _split3(x):
    h = x.astype(bf16)
    r = x - h.astype(f32)
    m = r.astype(bf16)
    return h, m, (r - m.astype(f32)).astype(bf16)


@jax.custom_vjp
def _mask_mm(mask, x):
    mb = mask.astype(bf16)
    return _dot(jnp.concatenate([mb, mb, mb], axis=1), jnp.concatenate(_split3(x), axis=0), NN)


def _mask_mm_bwd(mask, g):
    mb = mask.astype(bf16)
    return jnp.zeros_like(mask), _dot(jnp.concatenate([mb, mb, mb], axis=0), jnp.concatenate(_split3(g), axis=0), TN)


_mask_mm.defvjp(lambda mask, x: (_mask_mm(mask, x), mask), _mask_mm_bwd)


@jax.custom_vjp
def _split_lanes(x):
    h = x.shape[1] // 2
    return x[:, :h], x[:, h:]


_split_lanes.defvjp(lambda x: (_split_lanes(x), None), lambda _, g: (jnp.concatenate(g, axis=1),))


def _scan_chunk(s0, r, v, kk, lw, kd, bd, rev, inv=None):
    n = len(r)
    each = range(n)
    c = r[0].shape[0]
    row = lax.broadcasted_iota(jnp.int32, (c, 2 * c), 0)
    col = lax.broadcasted_iota(jnp.int32, (c, 2 * c), 1) % c
    ahead = jnp.where(rev, col - row, row - col)
    before = ahead > 0
    incl = ahead >= 0
    lane = lax.broadcasted_iota(jnp.int32, (1, 128), 1)
    m0 = (lane < 64).astype(f32)
    heads = lambda t: jnp.concatenate([t * m0, t * (1.0 - m0)], axis=0)
    bd_mask = ((lax.broadcasted_iota(jnp.int32, (128, 128), 0) // 64)
               == (lax.broadcasted_iota(jnp.int32, (128, 128), 1) // 64)).astype(f32)
    tot = [jnp.sum(lw[i], axis=0, keepdims=True) for i in each]
    row1 = lax.broadcasted_iota(jnp.int32, (c, c), 0)
    col1 = lax.broadcasted_iota(jnp.int32, (c, c), 1)
    upto = (jnp.where(rev, col1 - row1, row1 - col1) >= 0).astype(f32)
    lp = [_mask_mm(upto, lw[i]) - 0.5 * tot[i] for i in each]
    eg = [jnp.exp(lp[i]) for i in each]
    ieg = [jnp.exp(-lp[i]) for i in each]
    rt = [r[i] * eg[i] for i in each]
    kt = [kd[i] * ieg[i] for i in each]
    bt = [bd[i] * ieg[i] for i in each]
    at = [kk[i] * jnp.exp(lp[i] - lw[i]) for i in each]
    etot = [jnp.exp(0.5 * tot[i]) for i in each]
    si = [s0[i] * etot[i] for i in each]
    bth = [heads(bt[i]) for i in each]
    kth = [heads(kt[i]) for i in each]
    vh = [heads(v[i]) for i in each]
    a_ab = [jnp.where(before, _mm_nt(at[i], bth[i]), 0.0) for i in each]
    a_ak = [jnp.where(before, _mm_nt(at[i], kth[i]), 0.0) for i in each]
    a_rb = [jnp.where(incl, _mm_nt(rt[i], bth[i]), 0.0) for i in each]
    a_rk = [jnp.where(incl, _mm_nt(rt[i], kth[i]), 0.0) for i in each]
    halves = [_split_lanes(a_ab[i]) for i in each]
    mats = tuple(m for pair in halves for m in pair)
    inv = _tri_inv(mats) if inv is None else _tri_inv_saved(mats, inv)
    t = [jnp.concatenate([inv[2 * i], inv[2 * i + 1]], axis=1) for i in each]
    x0 = [_mm_nt(at[i], si[i]) for i in each]
    x = [x0[i] + _mm(a_ak[i], vh[i]) for i in each]
    u = [-_mm(t[i], heads(x[i])) for i in each]
    y0 = [_mm_nt(rt[i], si[i]) for i in each]
    y = [y0[i] + _mm(jnp.concatenate([a_rb[i], a_rk[i]], axis=1), jnp.concatenate([heads(u[i]), vh[i]], axis=0))
         for i in each]
    ds = [_mm_tn(jnp.concatenate([u[i], v[i]], axis=0), jnp.concatenate([bt[i], kt[i]], axis=0)) for i in each]
    se = [(si[i] + ds[i] * bd_mask) * etot[i] for i in each]
    return tuple(y), tuple(se), inv


PAIRS = HEADS // 2


def _scan_specs(nc, bsz, flip=False):
    def cc(d, c):
        c = nc - 1 - c if flip else c
        return jnp.where(d == 0, c, nc - 1 - c)

    rowblk = lambda d, b, c: b * nc + cc(d, c)
    zspec = lambda blk: pl.BlockSpec((CHUNK, RW_DIM), lambda d, b, c: (rowblk(d, b, c), blk))
    dspec = pl.BlockSpec((CHUNK, RW_DIM), lambda d, b, c: (rowblk(d, b, c), d))
    yspec = pl.BlockSpec((None, CHUNK, RW_DIM), lambda d, b, c: (d, rowblk(d, b, c), 0))
    sspec = pl.BlockSpec((None, PAIRS, 128, 128), lambda d, b, c: ((d * bsz + b) * nc + cc(d, c), 0, 0, 0))
    tspec = pl.BlockSpec((None, PAIRS, 128, 256), lambda d, b, c: ((d * bsz + b) * nc + cc(d, c), 0, 0, 0))
    return zspec, dspec, yspec, sspec, tspec


def _scan_fwd(z, lw, kd, kk, bd, bsz, seq, name, ride=None):
    n = z.shape[0]
    nc = seq // CHUNK
    zspec, dspec, yspec, sspec, tspec = _scan_specs(nc, bsz)

    def body(r_ref, v_ref, kk_ref, lw_ref, kd_ref, bd_ref, y_ref, s_ref, t_ref, st):
        @pl.when(pl.program_id(2) == 0)
        def _():
            st[...] = jnp.zeros_like(st)

        rev = pl.program_id(0) == 1
        lanes = [slice(h * 128, (h + 1) * 128) for h in range(PAIRS)]
        s0 = tuple(st[h] for h in range(PAIRS))
        ops = [tuple(ref[:, ln] for ln in lanes) for ref in (r_ref, v_ref, kk_ref, lw_ref, kd_ref, bd_ref)]
        y, se, inv = _scan_chunk(s0, *ops, rev)
        for h, ln in enumerate(lanes):
            s_ref[h] = s0[h]
            t_ref[h, :, :128] = inv[2 * h]
            t_ref[h, :, 128:] = inv[2 * h + 1]
            y_ref[:, ln] = y[h]
            st[h] = se[h]

    return _pc(
        body, name=name, grid=(2, bsz, nc),
        in_specs=[zspec(0), zspec(2), zspec(0), dspec, dspec, dspec],
        out_specs=[yspec, sspec, tspec],
        out_shape=[_sds((2, n, RW_DIM)), _sds((2 * bsz * nc, PAIRS, 128, 128)), _sds((2 * bsz * nc, PAIRS, 128, 256))],
        scratch=[pltpu.VMEM((PAIRS, 128, 128), f32)], sem=("parallel", "parallel", "arbitrary"), ride=ride,
    )(z, z, kk, lw, kd, bd)


def _scan_bwd(z, lw, kd, kk, bd, s_in, t_in, dy, bsz, seq, name, ride=None):
    n = z.shape[0]
    nc = seq // CHUNK
    zspec, dspec, yspec, sspec, tspec = _scan_specs(nc, bsz, flip=True)

    def body(r_ref, v_ref, kk_ref, lw_ref, kd_ref, bd_ref, s_ref, t_ref, dy_ref,
             dr_ref, dv_ref, dkk_ref, dlw_ref, dkd_ref, dbd_ref, dst):
        @pl.when(pl.program_id(2) == 0)
        def _():
            dst[...] = jnp.zeros_like(dst)

        rev = pl.program_id(0) == 1
        lanes = [slice(h * 128, (h + 1) * 128) for h in range(PAIRS)]
        s0 = tuple(s_ref[h] for h in range(PAIRS))
        inv = tuple(t_ref[h, :, a * 128:(a + 1) * 128] for h in range(PAIRS) for a in range(2))
        ops = [tuple(ref[:, ln] for ln in lanes) for ref in (r_ref, v_ref, kk_ref, lw_ref, kd_ref, bd_ref)]
        cot = (tuple(dy_ref[:, ln] for ln in lanes), tuple(dst[h] for h in range(PAIRS)))
        _, vjp = jax.vjp(lambda *a: _scan_chunk(*a, rev=rev, inv=inv)[:2], s0, *ops)
        grads = vjp(cot)
        for h, ln in enumerate(lanes):
            dst[h] = grads[0][h]
            for o_ref, g in zip((dr_ref, dv_ref, dkk_ref, dlw_ref, dkd_ref, dbd_ref), grads[1:]):
                o_ref[:, ln] = g[h]

    return _pc(
        body, name=name, grid=(2, bsz, nc),
        in_specs=[zspec(0), zspec(2), zspec(0), dspec, dspec, dspec, sspec, tspec, zspec(0)],
        out_specs=[yspec, yspec, yspec, dspec, dspec, dspec],
        out_shape=[_sds((2, n, RW_DIM))] * 3 + [_sds((n, 2 * RW_DIM))] * 3,
        scratch=[pltpu.VMEM((PAIRS, 128, 128), f32)], sem=("parallel", "parallel", "arbitrary"), ride=ride,
    )(z, z, kk, lw, kd, bd, s_in, t_in, dy)


def _merge_fwd(x2, p, ya, yb, yc, gb, wb, wo, name):
    n = x2.shape[0]
    tm = min(256, n)

    def body(x_ref, pg_ref, ya_ref, yb_ref, yc_ref, gb_ref, wb_ref, wo_ref, o_ref):
        gates = _sigmoid(pg_ref[...] + gb_ref[...])
        merged = jnp.zeros((tm, D), f32)
        for k, y_ref in enumerate((ya_ref, yb_ref, yc_ref)):
            merged += gates[:, k * D:(k + 1) * D] * _bdot(y_ref[...], wb_ref[k])
        o_ref[...] = x_ref[...] + _bdot(merged, wo_ref[...])

    row = lambda w, b=0: pl.BlockSpec((tm, w), lambda i, b=b: (i, b))
    return _pc(
        body, name=name, grid=(n // tm,),
        in_specs=[row(D), row(3 * D, O_GATE // (3 * D)), row(512), row(512), row(512),
                  pl.BlockSpec((1, 3 * D), lambda i: (0, 0)), pl.BlockSpec((3, 512, D), lambda i: (0, 0, 0)),
                  pl.BlockSpec((D, D), lambda i: (0, 0))],
        out_specs=row(D), out_shape=_sds((n, D)), sem=("parallel",),
    )(x2, p, ya, yb, yc, gb, wb, wo)


def _merge_bwd(dx1, p, ya, yb, yc, gb, wb, wo, name):
    n = dx1.shape[0]
    tm = min(256, n)

    def body(dx_ref, pg_ref, ya_ref, yb_ref, yc_ref, gb_ref, wb_ref, wo_ref,
             dpg_ref, dya_ref, dyb_ref, dyc_ref, dt_ref, mg_ref, dgb_ref):
        gates = _sigmoid(pg_ref[...] + gb_ref[...])
        dmerged = _bdot(dx_ref[...], wo_ref[...], NT)
        merged = jnp.zeros((tm, D), f32)
        dpg = []
        for k, (y_ref, dy_ref) in enumerate(((ya_ref, dya_ref), (yb_ref, dyb_ref), (yc_ref, dyc_ref))):
            gk = gates[:, k * D:(k + 1) * D]
            tk = _bdot(y_ref[...], wb_ref[k])
            merged += gk * tk
            dpg.append(dmerged * tk * gk * (1.0 - gk))
            dtk = dmerged * gk
            dt_ref[:, k * D:(k + 1) * D] = dtk.astype(bf16)
            dy_ref[...] = _bdot(dtk, wb_ref[k], NT)
        dpg = jnp.concatenate(dpg, axis=1)
        dpg_ref[...] = dpg.astype(bf16)
        mg_ref[...] = merged.astype(bf16)
        part = jnp.sum(dpg, axis=0, keepdims=True)

        @pl.when(pl.program_id(0) == 0)
        def _():
            dgb_ref[...] = part

        @pl.when(pl.program_id(0) != 0)
        def _():
            dgb_ref[...] += part

    row = lambda w, b=0: pl.BlockSpec((tm, w), lambda i, b=b: (i, b))
    return _pc(
        body, name=name, grid=(n // tm,),
        in_specs=[row(D), row(3 * D, O_GATE // (3 * D)), row(512), row(512), row(512),
                  pl.BlockSpec((1, 3 * D), lambda i: (0, 0)), pl.BlockSpec((3, 512, D), lambda i: (0, 0, 0)),
                  pl.BlockSpec((D, D), lambda i: (0, 0))],
        out_specs=[row(3 * D), row(512), row(512), row(512), row(3 * D), row(D),
                   pl.BlockSpec((1, 3 * D), lambda i: (0, 0))],
        out_shape=[_sds((n, 3 * D), bf16), _sds((n, 512)), _sds((n, 512)), _sds((n, 512)), _sds((n, 3 * D), bf16),
                   _sds((n, D), bf16), _sds((1, 3 * D))],
        sem=("arbitrary",),
    )(dx1, p, ya, yb, yc, gb, wb, wo)


FF_T = 1408


def _ffn_fwd(x1, g, wg, wu, wd, name):
    n = x1.shape[0]
    tm = min(512, n)
    nf = D_FF // FF_T

    def body(x_ref, g_ref, wg_ref, wu_ref, wd_ref, o_ref, hs):
        j = pl.program_id(1)

        @pl.when(j == 0)
        def _():
            hs[...] = _rms(x_ref[...], g_ref[...]).astype(bf16)
            o_ref[...] = x_ref[...]

        a = _dot(hs[...], wg_ref[...], NN)
        b = _dot(hs[...], wu_ref[...], NN)
        o_ref[...] += _bdot(a * _sigmoid(a) * b, wd_ref[...])

    return _pc(
        body, name=name, grid=(n // tm, nf),
        in_specs=[pl.BlockSpec((tm, D), lambda i, j: (i, 0)), pl.BlockSpec((1, D), lambda i, j: (0, 0)),
                  pl.BlockSpec((D, FF_T), lambda i, j: (0, j)), pl.BlockSpec((D, FF_T), lambda i, j: (0, j)),
                  pl.BlockSpec((FF_T, D), lambda i, j: (j, 0))],
        out_specs=pl.BlockSpec((tm, D), lambda i, j: (i, 0)), out_shape=_sds((n, D)),
        scratch=[pltpu.VMEM((tm, D), bf16)], sem=("parallel", "arbitrary"),
    )(x1, g, wg, wu, wd)


def _ffn_bwd(dx2, x1, g, wg, wu, wd, name):
    n = x1.shape[0]
    tm = min(512, n)
    nf = D_FF // FF_T

    def body(dx_ref, x_ref, g_ref, wg_ref, wu_ref, wd_ref, dx1_ref, dg_ref, h_ref, da_ref, db_ref, hm_ref, acc):
        i, j = pl.program_id(0), pl.program_id(1)

        @pl.when(j == 0)
        def _():
            h_ref[...] = _rms(x_ref[...], g_ref[...]).astype(bf16)
            acc[...] = jnp.zeros_like(acc)

        @pl.when((i == 0) & (j == 0))
        def _():
            dg_ref[...] = jnp.zeros_like(dg_ref)

        h = h_ref[...]
        a = _dot(h, wg_ref[...], NN)
        b = _dot(h, wu_ref[...], NN)
        sg = _sigmoid(a)
        s = a * sg
        dhm = _bdot(dx_ref[...], wd_ref[...], NT)
        da = (dhm * b * (sg * (1.0 + a * (1.0 - sg)))).astype(bf16)
        db = (dhm * s).astype(bf16)
        da_ref[...] = da
        db_ref[...] = db
        hm_ref[...] = (s * b).astype(bf16)
        acc[...] += _dot(da, wg_ref[...], NT) + _dot(db, wu_ref[...], NT)

        @pl.when(j == nf - 1)
        def _():
            _, vjp = jax.vjp(_rms, x_ref[...], g_ref[...])
            dx, dg = vjp(acc[...])
            dx1_ref[...] = dx_ref[...] + dx
            dg_ref[...] += dg

    rowf = pl.BlockSpec((tm, FF_T), lambda i, j: (i, j))
    rowd = pl.BlockSpec((tm, D), lambda i, j: (i, 0))
    vec = pl.BlockSpec((1, D), lambda i, j: (0, 0))
    return _pc(
        body, name=name, grid=(n // tm, nf),
        in_specs=[rowd, rowd, vec, pl.BlockSpec((D, FF_T), lambda i, j: (0, j)),
                  pl.BlockSpec((D, FF_T), lambda i, j: (0, j)), pl.BlockSpec((FF_T, D), lambda i, j: (j, 0))],
        out_specs=[rowd, vec, rowd, rowf, rowf, rowf],
        out_shape=[_sds((n, D)), _sds((1, D)), _sds((n, D), bf16), _sds((n, D_FF), bf16), _sds((n, D_FF), bf16),
                   _sds((n, D_FF), bf16)],
        scratch=[pltpu.VMEM((tm, D), f32)], sem=("arbitrary", "arbitrary"),
    )(dx2, x1, g, wg, wu, wd)


def _loss_head(x2, g, tgt, name):
    n = x2.shape[0]
    tm = min(512, n)

    def f(x, gg, t):
        e = _rms(x, gg) - t
        return 0.5 * jnp.sum(jnp.mean(e * e, axis=-1, keepdims=True))

    def body(x_ref, g_ref, t_ref, l_ref, dx_ref, dg_ref):
        val, vjp = jax.vjp(f, x_ref[...], g_ref[...], t_ref[...])
        dx, dg, _ = vjp(jnp.ones((), f32))
        dx_ref[...] = dx

        @pl.when(pl.program_id(0) == 0)
        def _():
            l_ref[...] = jnp.zeros_like(l_ref)
            dg_ref[...] = jnp.zeros_like(dg_ref)

        l_ref[...] += val
        dg_ref[...] += dg

    rowd = pl.BlockSpec((tm, D), lambda i: (i, 0))
    return _pc(
        body, name=name, grid=(n // tm,),
        in_specs=[rowd, pl.BlockSpec((1, D), lambda i: (0, 0)), rowd],
        out_specs=[pl.BlockSpec((8, 128), lambda i: (0, 0)), rowd, pl.BlockSpec((1, D), lambda i: (0, 0))],
        out_shape=[_sds((8, 128)), _sds((n, D)), _sds((1, D))], sem=("arbitrary",),
    )(x2, g, tgt)


def _adamw(w, parts, m, v, name):
    nl, r, c = w.shape
    tr = r
    for cand in (1024, 512, 256, 128, 64, 32, 16, 8):
        if r % cand == 0 and cand * c * 4 <= 1024 * 1024:
            tr = cand
            break

    def body(*refs):
        w_ref, p_refs, (m_ref, v_ref, g_ref, d_ref, nm_ref, nv_ref) = refs[0], refs[1:1 + nl], refs[1 + nl:]

        def update(p_ref):
            gg = p_ref[0].astype(f32)
            for k in range(1, N_DEV):
                gg = gg + p_ref[k].astype(f32)
            g_ref[...] = gg
            nm = B1 * m_ref[...] + (1.0 - B1) * gg
            nv = B2 * v_ref[...] + (1.0 - B2) * (gg * gg)
            m_hat = nm / (1.0 - B1 ** STEP)
            v_hat = nv / (1.0 - B2 ** STEP)
            d_ref[...] = -LR * (m_hat / (jnp.sqrt(v_hat) + EPS) + WD * w_ref[...])
            nm_ref[...] = nm
            nv_ref[...] = nv

        for j in range(nl):
            pl.when(pl.program_id(0) == j)(functools.partial(update, p_refs[j]))

    spec = pl.BlockSpec((None, tr, c), lambda l, i: (l, i, 0))
    pspecs = [pl.BlockSpec((N_DEV, tr, c), lambda l, i, j=j: (0, jnp.where(l == j, i, 0), 0)) for j in range(nl)]
    return _pc(body, name=name, grid=(nl, r // tr), in_specs=[spec] + pspecs + [spec, spec], out_specs=[spec] * 4,
               out_shape=[_sds((nl, r, c))] * 4, sem=("arbitrary", "arbitrary"))(w, *parts, m, v)


def _peers():
    x, y, c = lax.axis_index("x"), lax.axis_index("y"), lax.axis_index("c")
    me = 4 * x + 2 * y + c
    peers = []
    for k in range(1, N_DEV):
        fx, fy, fc = (k >> 2) & 1, (k >> 1) & 1, k & 1
        peers.append(((1 - x) if fx else x, (1 - y) if fy else y, (1 - c) if fc else c))
    return me, peers


def _exchange(gathers, scatters, name):
    _, got = _pc(lambda: None, name=name, out_shape=[], ride=(gathers, scatters))()
    return got


SHARDED = {"w_in": 2, "gate_b": 2, "w_uq": 2, "w_ukv": 2, "rw_w0": 2, "rw_w2": 3, "rw_a0": 2, "rw_a2": 3, "rw_g2": 2,
           "w_branch": 3, "w_out": 1, "w_ffn_gate": 2, "w_ffn_up": 2, "w_ffn_down": 1}
GATHER_F32 = ("gate_b", "rw_w0", "rw_a0")
REPLICATED = ("attn_norm_g", "q_norm_g", "kv_norm_g", "sg_ln_g", "sg_ln_b", "sg_w", "sg_b", "rw_mu", "rw_k_k", "rw_k_a",
              "rw_r_k", "rw_ln_g", "rw_ln_b", "ffn_norm_g", "final_norm_g")
WEIGHTS = ("attn_norm_g", "w_in", "gate_b", "q_norm_g", "w_uq", "kv_norm_g", "w_ukv", "sg_ln_g", "sg_ln_b", "sg_w", "sg_b",
           "rw_mu", "rw_w0", "rw_w2", "rw_a0", "rw_a2", "rw_g2", "rw_k_k", "rw_k_a", "rw_r_k", "rw_ln_g", "rw_ln_b",
           "w_branch", "w_out", "ffn_norm_g", "w_ffn_gate", "w_ffn_up", "w_ffn_down", "final_norm_g")


REP_MAIN = tuple(k for k in REPLICATED if k not in ("attn_norm_g", "sg_w"))
BIG = ("w_in", "w_branch", "w_out", "w_ffn_gate", "w_ffn_up", "w_ffn_down")
SMALL_BF = ("w_uq", "w_ukv", "rw_w2", "rw_a2", "rw_g2")
SMALL = SMALL_BF + GATHER_F32


def _pack128(blocks, names, dtype, lead=0, to=256):
    parts = [blocks[k].astype(dtype).reshape(blocks[k].shape[:lead] + (-1, 128)) for k in names]
    rows = sum(p.shape[lead] for p in parts)
    pad = -rows % to
    if pad:
        parts.append(jnp.zeros(parts[0].shape[:lead] + (pad, 128), dtype))
    return jnp.concatenate(parts, axis=lead)


def _unpack128(packed, shapes, names, lead=0):
    out, off = {}, 0
    for k in names:
        rows = 1
        for d in shapes[k]:
            rows *= d
        rows //= 128
        idx = (slice(None),) * lead + (slice(off, off + rows),)
        out[k] = packed[idx].reshape(packed.shape[:lead] + tuple(shapes[k]))
        off += rows
    return out


def _join_blocks(g, ax):
    shp = g.shape[1:]
    return jnp.moveaxis(g, 0, ax).reshape(shp[:ax] + (N_DEV * shp[ax],) + shp[ax + 1:])


def _split_blocks(full, ax):
    shp = full.shape
    return jnp.moveaxis(full.reshape(shp[:ax] + (N_DEV, shp[ax] // N_DEV) + shp[ax + 1:]), ax, 0)


def _w_in_padded(w):
    z = lambda n: jnp.zeros((w.shape[0], n), w.dtype)
    q, ckv, kr = w[:, 0:384], w[:, 384:640], w[:, 640:672]
    sg, rw, gate = w[:, 672:1696], w[:, 1696:3616], w[:, 3616:6688]
    return jnp.concatenate([gate, sg, rw, z(128), ckv, z(64), kr, z(32), q, z(P_W - O_MLA - MLA_W)], axis=1)


def _w_in_unpadded(g):
    return jnp.concatenate([g[:, O_Q:O_Q + 384], g[:, O_CKV:O_CKV + 256], g[:, O_SLAB + 64:O_SLAB + 96],
                            g[:, O_SG:O_SG + 1024], g[:, O_RW:O_RW + 1920], g[:, O_GATE:O_GATE + 3072]], axis=1)


REST = ("w_branch", "w_out", "w_ffn_gate", "w_ffn_up", "w_ffn_down")


def _rest_weights(full, l):
    return dict(wb=full["w_branch"][l], wo=full["w_out"][l], wg=full["w_ffn_gate"][l], wu=full["w_ffn_up"][l],
                wd=full["w_ffn_down"][l])


def _layer_weights(full, rep, l):
    w = {}
    w["w_in"] = _w_in_padded(full["w_in"][l])
    if full["w_branch"][l] is not None:
        w.update(_rest_weights(full, l))
    uq = full["w_uq"][l].reshape(Q_LORA, HEADS, QK_NOPE + QK_ROPE)
    w["wq"] = jnp.pad(uq, ((0, 0), (0, 0), (0, 32))).reshape(Q_LORA, HEADS * 128).astype(f32)
    ukv = full["w_ukv"][l].reshape(KV_LORA, HEADS, QK_NOPE + V_HEAD)
    wk = jnp.pad(ukv[:, :, :QK_NOPE], ((0, 0), (0, 0), (0, 64))).reshape(KV_LORA, HEADS * 128)
    w["wk"], w["wv"] = wk.astype(f32), ukv[:, :, QK_NOPE:].reshape(KV_LORA, HEADS * V_HEAD).astype(f32)
    bdiag = lambda t: jnp.concatenate([jnp.concatenate([t[0], jnp.zeros_like(t[0])], axis=1),
                                       jnp.concatenate([jnp.zeros_like(t[1]), t[1]], axis=1)], axis=0).astype(f32)
    w["w2"], w["a2"] = bdiag(full["rw_w2"][l]), bdiag(full["rw_a2"][l])
    w["g2"] = full["rw_g2"][l].astype(f32)
    w["w0"], w["a0"] = full["rw_w0"][l].reshape(1, 2 * RW_DIM), full["rw_a0"][l].reshape(1, 2 * RW_DIM)
    w["gate_b"] = full["gate_b"][l].reshape(1, 3 * D)
    row = lambda a: a.reshape(1, -1)
    for k in ("attn_norm_g", "q_norm_g", "kv_norm_g", "sg_ln_g", "sg_ln_b", "rw_k_k", "rw_k_a", "rw_ln_g", "rw_ln_b",
              "ffn_norm_g"):
        w[k] = row(rep[k][l])
    w["r_k"] = row(rep["rw_r_k"][l])
    w["mu"] = jnp.pad(row(rep["rw_mu"][l]), ((0, 0), (0, RW_W - 1920)))
    w["sg_w"] = [rep["sg_w"][l, k] for k in range(SG_GROUPS)]
    w["sg_bias"] = jnp.repeat(rep["sg_b"][l].T, SG_DIM // SG_GROUPS, axis=1)
    return w


def _riding(res, ride, got, key):
    if ride is None:
        return res
    got[key] = res[1]
    return res[0]


def _layer_fwd(x2, w, tabs, bsz, seq, l, rides=None, on_inproj=None):
    nm = lambda s: f"l{l}_{s}"
    n = x2.shape[0]
    tm = min(256, n)
    rides = rides or {}
    ride = lambda key: (rides[key], []) if key in rides else None
    got = {}
    p, h = _riding(_inproj_fwd(x2, w["attn_norm_g"], w["w_in"], nm("inproj"), ride("inproj")), ride("inproj"), got, "inproj")
    if on_inproj is not None:
        w.update(on_inproj(got["inproj"]))
    mla_rows = [(p, 256, O_CKV // 256), (p, 128, O_SLAB // 128), (p, 384, O_Q // 384), (tabs[0], 128, 0), (tabs[1], 128, 0)]
    mla_w = [w["q_norm_g"], w["kv_norm_g"], w["wq"], w["wk"], w["wv"]]
    q, k, v = _rowwise_fwd(nm("mla_proj"), _f_mla_proj, mla_rows, mla_w, [(1024, bf16), (1024, bf16), (512, bf16)], tm)
    ya = _riding(_attn_fwd(q, k, v, bsz, seq, nm("attn"), ride("attn")), ride("attn"), got, "attn")
    sg_rows = [(p, SG_DIM, O_SG // SG_DIM), (p, SG_DIM, O_SG // SG_DIM + 1)]
    sg_w = [w["sg_ln_g"], w["sg_ln_b"], w["sg_bias"]] + w["sg_w"]
    (yb,) = _rowwise_fwd(nm("sg"), _f_sg, sg_rows, sg_w, [(SG_DIM, f32)], SG_CHUNK)
    z = _shift_fwd(p, w["mu"], seq, nm("shift"))
    pre_rows = [(z, 512, 1), (z, 128, 12), (z, 128, 13), (z, 128, 14)]
    pre_w = [w["w0"], w["a0"], w["w2"], w["a2"], w["g2"], w["rw_k_k"], w["rw_k_a"]]
    lw, kd, kk, bd, g = _rowwise_fwd(nm("rw_pre"), _f_rw_pre, pre_rows, pre_w,
                                     [(1024, f32), (1024, f32), (512, f32), (1024, f32), (512, f32)], tm)
    y, s_in, t_in = _riding(_scan_fwd(z, lw, kd, kk, bd, bsz, seq, nm("scan"), ride("scan")), ride("scan"), got, "scan")
    post_rows = [(y[0], 512, 0), (y[1], 512, 0), (z, 512, 0), (z, 512, 2), (kd, 512, 0), (kd, 512, 1), (g, 512, 0)]
    post_w = [w["r_k"], w["rw_ln_g"], w["rw_ln_b"]]
    (yc,) = _rowwise_fwd(nm("rw_post"), _f_rw_post, post_rows, post_w, [(512, f32)], tm)
    x1 = _merge_fwd(x2, p, ya, yb, yc, w["gate_b"], w["wb"], w["wo"], nm("merge"))
    x3 = _ffn_fwd(x1, w["ffn_norm_g"], w["wg"], w["wu"], w["wd"], nm("ffn"))
    saved = dict(x=x2, p=p, h=h, q=q, k=k, v=v, ya=ya, yb=yb, z=z, lw=lw, kd=kd, kk=kk, bd=bd, g=g, y=y, s_in=s_in, t_in=t_in, yc=yc,
                 x1=x1, mla_rows=mla_rows, mla_w=mla_w, sg_rows=sg_rows, sg_w=sg_w, pre_rows=pre_rows, pre_w=pre_w,
                 post_rows=post_rows, post_w=post_w)
    return x3, saved, got


def _layer_bwd(dx3, w, sv, bsz, seq, l, rides=None):
    nm = lambda s: f"l{l}_{s}_bwd"
    n = dx3.shape[0]
    tm = min(256, n)
    g = {}
    rides = rides or {}
    ride = lambda key: rides[key](g) if key in rides else None
    got = {}
    dx1, g["ffn_norm_g"], h2, da, db, hm = _ffn_bwd(dx3, sv["x1"], w["ffn_norm_g"], w["wg"], w["wu"], w["wd"], nm("ffn"))
    g["wg"] = _matmul_tn(h2, da, nm("wg"))
    g["wu"] = _matmul_tn(h2, db, nm("wu"))
    g["wd"] = _matmul_tn(hm, dx3.astype(bf16), nm("wd"))
    dpg, dya, dyb, dyc, dt, mg, g["gate_b"] = _merge_bwd(dx1, sv["p"], sv["ya"], sv["yb"], sv["yc"], w["gate_b"], w["wb"],
                                                         w["wo"], nm("merge"))
    g["wo"] = _matmul_tn(mg, dx1.astype(bf16), nm("wo"))
    ys = (sv["ya"], sv["yb"], sv["yc"])
    g["wb"] = jnp.stack([_matmul_tn(ys[k].astype(bf16), dt[:, k * D:(k + 1) * D], nm(f"wb{k}")) for k in range(3)])
    (dy, dr_p, dv_p, dkd0, dkd1, dg_), (g["r_k"], g["rw_ln_g"], g["rw_ln_b"]) = _rowwise_bwd(
        nm("rw_post"), _f_rw_post, sv["post_rows"], sv["post_w"], [(dyc, 512, 0)], tm, [f32, None] + [f32] * 5)
    dkd_p = jnp.concatenate([dkd0, dkd1], axis=1)
    rd = ride("scan")
    dr_s, dv_s, dkk_s, dlw, dkd_s, dbd = _riding(
        _scan_bwd(sv["z"], sv["lw"], sv["kd"], sv["kk"], sv["bd"], sv["s_in"], sv["t_in"], dy, bsz, seq, nm("scan"), rd),
        rd, got, "scan")
    pre_cots = [(dlw, 1024, 0), (dkd_s + dkd_p, 1024, 0), (dkk_s[0] + dkk_s[1], 512, 0), (dbd, 1024, 0), (dg_, 512, 0)]
    (dk, dwl, dal, dgl), (g["w0"], g["a0"], g["w2"], g["a2"], g["g2"], g["rw_k_k"], g["rw_k_a"]) = _rowwise_bwd(
        nm("rw_pre"), _f_rw_pre, sv["pre_rows"], sv["pre_w"], pre_cots, tm, [f32] * 4)
    dz = jnp.concatenate([dr_s[0] + dr_s[1] + dr_p, dk, dv_s[0] + dv_s[1] + dv_p, dwl, dal, dgl,
                          jnp.zeros((n, RW_W - 1920), f32)], axis=1)
    dp_rw, g["mu"] = _shift_bwd(dz, sv["p"], w["mu"], seq, nm("shift"))
    (dp_su, dp_sv), (g["sg_ln_g"], g["sg_ln_b"], g["sg_bias"], *sgw) = _rowwise_bwd(
        nm("sg"), _f_sg, sv["sg_rows"], sv["sg_w"], [(dyb, SG_DIM, 0)], SG_CHUNK, [bf16, bf16])
    g["sg_w"] = jnp.stack(sgw)
    rd = ride("attn")
    dq, dk_, dv_ = _riding(_attn_bwd(sv["q"], sv["k"], sv["v"], sv["ya"], dya, bsz, seq, nm("attn"), rd), rd, got, "attn")
    (dp_ckv, dp_slab, dp_q), (g["q_norm_g"], g["kv_norm_g"], g["wq"], g["wk"], g["wv"]) = _rowwise_bwd(
        nm("mla_proj"), _f_mla_proj, sv["mla_rows"], sv["mla_w"], [(dq, 1024, 0), (dk_, 1024, 0), (dv_, 512, 0)], tm,
        [bf16, bf16, bf16, None, None])
    dp = jnp.concatenate([dpg, dp_su, dp_sv, dp_rw, dp_ckv, dp_slab, dp_q, jnp.zeros((n, P_W - O_MLA - MLA_W), bf16)],
                         axis=1)
    rd = ride("w_in")
    g["w_in"] = _riding(_matmul_tn(sv["h"], dp, nm("w_in"), rd), rd, got, "w_in")
    rd = ride("inproj")
    dx, g["attn_norm_g"] = _riding(_norm_matmul_bwd(dp, w["w_in"], sv["x"], w["attn_norm_g"], dx1, nm("inproj"), rd), rd, got,
                                   "inproj")
    return dx, g, got


def _layer_grads_to_full(g):
    o = {}
    if "w_in" in g:
        o["w_in"] = _w_in_unpadded(g["w_in"])
    o["w_uq"] = g["wq"].reshape(Q_LORA, HEADS, 128)[:, :, :QK_NOPE + QK_ROPE].reshape(Q_LORA, -1)
    gk = g["wk"].reshape(KV_LORA, HEADS, 128)[:, :, :QK_NOPE]
    gv = g["wv"].reshape(KV_LORA, HEADS, V_HEAD)
    o["w_ukv"] = jnp.concatenate([gk, gv], axis=2).reshape(KV_LORA, -1)
    unb = lambda t: jnp.stack([t[:LORA, :RW_DIM], t[LORA:, RW_DIM:]])
    o["rw_w2"], o["rw_a2"], o["rw_g2"] = unb(g["w2"]), unb(g["a2"]), g["g2"]
    o["rw_w0"], o["rw_a0"] = g["w0"].reshape(2, RW_DIM), g["a0"].reshape(2, RW_DIM)
    o["gate_b"] = g["gate_b"].reshape(3, D)
    o["w_branch"], o["w_out"] = g["wb"], g["wo"]
    o["w_ffn_gate"], o["w_ffn_up"], o["w_ffn_down"] = g["wg"], g["wu"], g["wd"]
    for k in ("attn_norm_g", "q_norm_g", "kv_norm_g", "sg_ln_g", "sg_ln_b", "rw_k_k", "rw_k_a", "rw_ln_g", "rw_ln_b",
              "ffn_norm_g"):
        if k in g:
            o[k] = g[k].reshape(-1)
    o["rw_r_k"] = g["r_k"].reshape(HEADS, RW_HEAD)
    o["rw_mu"] = g["mu"].reshape(-1)[:1920]
    o["sg_w"] = g["sg_w"]
    o["sg_b"] = g["sg_bias"].reshape(SG_CHUNK, SG_GROUPS, SG_DIM // SG_GROUPS).sum(axis=2).T
    return o


def _rope_tables(positions):
    inv = 1.0 / (10000.0 ** (jnp.arange(0, QK_ROPE, 2, dtype=f32) / QK_ROPE))
    ang = positions.astype(f32)[:, None] * inv
    cos, sin = jnp.cos(ang), jnp.sin(ang)
    n = positions.shape[0]
    c = jnp.concatenate([jnp.ones((n, 64), f32), cos, cos, jnp.zeros((n, 32), f32)], axis=1)
    s = jnp.concatenate([jnp.zeros((n, 64), f32), -sin, sin, jnp.zeros((n, 32), f32)], axis=1)
    return c, s


def _grad_parts(grad, name):
    return _split_blocks(grad, SHARDED[name] - 1).astype(bf16)


def _local_step(x, positions, full, rep, loss_target, blocks=None):
    bsz, seq, _ = x.shape
    n = bsz * seq
    x2 = x.reshape(n, D)
    tabs = _rope_tables(positions.reshape(n))
    join = lambda k, g: _join_blocks(g, SHARDED[k] - 1)
    rides, on_inproj = None, None
    if blocks is not None:
        rides = {"inproj": [blocks[k][0] for k in REST], "attn": [blocks["w_in"][1]], "scan": [blocks[k][1] for k in REST]}

        def on_inproj(got):
            for k, g in zip(REST, got):
                full[k][0] = join(k, g)
            return _rest_weights(full, 0)

    w0 = _layer_weights(full, rep, 0)
    x2, sv0, got = _layer_fwd(x2, w0, tabs, bsz, seq, 0, rides, on_inproj)
    if blocks is not None:
        full["w_in"][1] = join("w_in", got["attn"][0])
        for k, g in zip(REST, got["scan"]):
            full[k][1] = join(k, g)
    w1 = _layer_weights(full, rep, 1)
    x2, sv1, _ = _layer_fwd(x2, w1, tabs, bsz, seq, 1)
    loss, dx, dgf = _loss_head(x2, rep["final_norm_g"].reshape(1, D), loss_target.reshape(n, D), "loss_head")
    dx, g1, _ = _layer_bwd(dx, w1, sv1, bsz, seq, 1)
    grads1 = _layer_grads_to_full(g1)
    rides = None
    if blocks is not None:
        short = dict(w_branch="wb", w_out="wo", w_ffn_gate="wg", w_ffn_up="wu", w_ffn_down="wd")

        def beside_w_in(g):
            g0 = _layer_grads_to_full(g)
            both = {k: jnp.stack([g0[k], grads1[k]]) for k in g0}
            both["final_norm_g"] = dgf.reshape(D)
            split = {k: _split_blocks(both[k], SHARDED[k]) for k in SMALL}
            return [_pack128(both, REP_MAIN, f32), both["sg_w"]], [_pack128(split, SMALL, f32, lead=1)]

        rides = {"scan": lambda g: ([], [_grad_parts(grads1[k], k) for k in BIG]),
                 "attn": lambda g: ([], [_grad_parts(g[short[k]], k) for k in REST]),
                 "w_in": beside_w_in,
                 "inproj": lambda g: ([], [_grad_parts(_w_in_unpadded(g["w_in"]), "w_in")])}
    dx, g0, got = _layer_bwd(dx, w0, sv0, bsz, seq, 0, rides)
    grads0 = _layer_grads_to_full(g0)
    grads = {k: [grads0[k], grads1[k]] for k in grads0}
    grads["final_norm_g"] = dgf.reshape(D)
    parts = {}
    if blocks is not None:
        parts = {k: [None, p] for k, p in zip(BIG, got["scan"])}
        for k, p in zip(REST, got["attn"]):
            parts[k][0] = p
        parts["replicated"], parts["sg_w"], parts["small"] = got["w_in"]
        (parts["w_in"][0],) = got["inproj"]
    return loss[0, 0], dx.reshape(bsz, seq, D), grads, parts


def kernel(x, positions, attn_norm_g, w_in, gate_b, q_norm_g, w_uq, kv_norm_g, w_ukv, sg_ln_g, sg_ln_b, sg_w, sg_b, rw_mu, rw_w0, rw_w2, rw_a0, rw_a2, rw_g2, rw_k_k, rw_k_a, rw_r_k, rw_ln_g, rw_ln_b, w_branch, w_out, ffn_norm_g, w_ffn_gate, w_ffn_up, w_ffn_down, final_norm_g, loss_target, m_attn_norm_g, m_w_in, m_gate_b, m_q_norm_g, m_w_uq, m_kv_norm_g, m_w_ukv, m_sg_ln_g, m_sg_ln_b, m_sg_w, m_sg_b, m_rw_mu, m_rw_w0, m_rw_w2, m_rw_a0, m_rw_a2, m_rw_g2, m_rw_k_k, m_rw_k_a, m_rw_r_k, m_rw_ln_g, m_rw_ln_b, m_w_branch, m_w_out, m_ffn_norm_g, m_w_ffn_gate, m_w_ffn_up, m_w_ffn_down, m_final_norm_g, v_attn_norm_g, v_w_in, v_gate_b, v_q_norm_g, v_w_uq, v_kv_norm_g, v_w_ukv, v_sg_ln_g, v_sg_ln_b, v_sg_w, v_sg_b, v_rw_mu, v_rw_w0, v_rw_w2, v_rw_a0, v_rw_a2, v_rw_g2, v_rw_k_k, v_rw_k_a, v_rw_r_k, v_rw_ln_g, v_rw_ln_b, v_w_branch, v_w_out, v_ffn_norm_g, v_w_ffn_gate, v_w_ffn_up, v_w_ffn_down, v_final_norm_g):
    args = locals()
    wts = {k: args[k] for k in WEIGHTS}
    mom_m = {k: args["m_" + k] for k in WEIGHTS}
    mom_v = {k: args["v_" + k] for k in WEIGHTS}
    shapes = {k: wts[k].shape for k in WEIGHTS}
    blocks = {k: wts[k].astype(bf16) for k in BIG}
    got = _exchange([blocks["w_in"][0], _pack128(wts, SMALL_BF, bf16), _pack128(wts, GATHER_F32, f32)], [], "gather_first")
    small = {**_unpack128(got[1], shapes, SMALL_BF, lead=1), **_unpack128(got[2], shapes, GATHER_F32, lead=1)}
    full = {k: list(_join_blocks(small[k], SHARDED[k])) for k in SMALL}
    full["w_in"] = [_join_blocks(got[0], SHARDED["w_in"] - 1), None]
    full.update({k: [None, None] for k in REST})
    rep = {k: wts[k] for k in REPLICATED}
    loss, grad_x, grads, parts = _local_step(x, positions, full, rep, loss_target, blocks)
    loss = lax.psum(loss, ("x", "y", "c"))
    last = ("attn_norm_g",)
    (last_parts,) = _exchange([_pack128({"attn_norm_g": jnp.stack(grads["attn_norm_g"])}, last, f32, to=16)], [],
                              "exchange_last")
    gw, delta, new_m, new_v = {}, {}, {}, {}
    for k in BIG:
        three = lambda a, k=k: a.reshape(a.shape[0], -1, shapes[k][-1])
        res = _adamw(three(wts[k]), [three(p) for p in parts[k]], three(mom_m[k]), three(mom_v[k]), f"adamw_{k}")
        gw[k], delta[k], new_m[k], new_v[k] = (t.reshape(shapes[k]) for t in res)
    rows = lambda a: a.reshape(1, -1, 128)
    res = _adamw(rows(wts["sg_w"]), [parts["sg_w"].reshape(N_DEV, -1, 128)], rows(mom_m["sg_w"]), rows(mom_v["sg_w"]),
                 "adamw_sg_w")
    gw["sg_w"], delta["sg_w"], new_m["sg_w"], new_v["sg_w"] = (t.reshape(shapes["sg_w"]) for t in res)
    for names, got, to in ((SMALL, parts["small"], 256), (REP_MAIN, parts["replicated"], 256), (last, last_parts, 16)):
        pk = lambda dct: _pack128(dct, names, f32, to=to)[None]
        res = _adamw(pk(wts), [got], pk(mom_m), pk(mom_v), f"adamw_{names[0]}")
        for dst, t in zip((gw, delta, new_m, new_v), res):
            dst.update(_unpack128(t[0], shapes, names))
    return (loss, grad_x, *[gw[k] for k in WEIGHTS], *[delta[k] for k in WEIGHTS], *[new_m[k] for k in WEIGHTS],
            *[new_v[k] for k in WEIGHTS])
```

```python
import functools

import jax
import jax.numpy as jnp
from jax import lax
from jax.experimental import pallas as pl
from jax.experimental.pallas import tpu as pltpu

f32 = jnp.float32
bf16 = jnp.bfloat16
HI = lax.Precision.HIGHEST
NN, NT, TN = ((1,), (0,)), ((1,), (1,)), ((0,), (0,))

N_DEV = 8
D = 1024
HEADS = 8
Q_LORA, KV_LORA, QK_NOPE, QK_ROPE, V_HEAD = 384, 256, 64, 32, 64
SG_DIM, SG_CHUNK, SG_GROUPS = 512, 128, 8
RW_DIM, RW_HEAD, LORA = 512, 64, 64
D_FF = 2816
N_IN = 6688
NORM_EPS, LN_EPS, GN_EPS = 1e-6, 1e-5, 64e-5
ATT_SCALE = (QK_NOPE + QK_ROPE) ** -0.5
P_W = 7168
O_GATE, O_SG, O_RW, O_MLA = 0, 3072, 4096, 6144
RW_W = 2048
MLA_W = 768
O_CKV, O_SLAB, O_Q = O_MLA, O_MLA + 256, O_MLA + 384
CHUNK = 128
VMEM_LIMIT = 56 * 1024 * 1024

B1, B2, LR, EPS, WD, STEP = 0.9, 0.999, 0.001, 1e-8, 0.01, 10


def _pc(body, *, name, out_shape, grid=(), in_specs=(), out_specs=(), scratch=(), sem=None, ride=None):
    params = pltpu.CompilerParams(dimension_semantics=sem, vmem_limit_bytes=VMEM_LIMIT)
    if ride is None:
        return pl.pallas_call(body, out_shape=out_shape, grid=grid, in_specs=in_specs, out_specs=out_specs,
                              scratch_shapes=scratch, compiler_params=params, name=name, interpret=False)
    gathers, scatters = ride
    moved = list(gathers) + list(scatters)
    ng, nx = len(gathers), len(moved)
    single = not isinstance(out_shape, (list, tuple))
    outs = [out_shape] if single else list(out_shape)
    ospecs = [out_specs] if single else list(out_specs)
    n_in, n_out, n_scr = len(in_specs), len(outs), len(scratch)
    per = N_DEV - 1

    def riding(*refs):
        ins, xin = refs[:n_in], refs[n_in:n_in + nx]
        outs_r, xout = refs[n_in + nx:n_in + nx + n_out], refs[n_in + nx + n_out:n_in + 2 * nx + n_out]
        own = refs[n_in + 2 * nx + n_out:n_in + 2 * nx + n_out + n_scr]
        send_sems, recv_sems, local_sems = refs[n_in + 2 * nx + n_out + n_scr:]

        def copies():
            me, peers = _peers()
            cps = []
            for a in range(nx):
                whole = a < ng
                cps.append(pltpu.make_async_copy(xin[a] if whole else xin[a].at[me], xout[a].at[me], local_sems.at[a]))
                for k, peer in enumerate(peers):
                    dev = 4 * peer[0] + 2 * peer[1] + peer[2]
                    cps.append(pltpu.make_async_remote_copy(
                        src_ref=xin[a] if whole else xin[a].at[dev], dst_ref=xout[a].at[me],
                        send_sem=send_sems.at[a * per + k], recv_sem=recv_sems.at[a * per + k], device_id=peer,
                        device_id_type=pl.DeviceIdType.MESH))
            return cps

        if not grid:
            for cp in copies():
                cp.start()
            body(*ins, *outs_r, *own)
            for cp in copies():
                cp.wait()
            return
        ids = [pl.program_id(a) for a in range(len(grid))]
        first = functools.reduce(jnp.logical_and, [i == 0 for i in ids])
        last = functools.reduce(jnp.logical_and, [i == g - 1 for i, g in zip(ids, grid)])

        @pl.when(first)
        def _():
            for cp in copies():
                cp.start()

        body(*ins, *outs_r, *own)

        @pl.when(last)
        def _():
            for cp in copies():
                cp.wait()

    anyspec = pl.BlockSpec(memory_space=pl.ANY)
    call = pl.pallas_call(
        riding, grid=grid, in_specs=list(in_specs) + [anyspec] * nx, out_specs=ospecs + [anyspec] * nx,
        out_shape=outs + [_sds((N_DEV,) + a.shape, a.dtype) for a in gathers] + [_sds(a.shape, a.dtype) for a in scatters],
        scratch_shapes=list(scratch) + [pltpu.SemaphoreType.DMA((nx * per,)), pltpu.SemaphoreType.DMA((nx * per,)),
                                        pltpu.SemaphoreType.DMA((nx,))],
        compiler_params=params, name=name, interpret=False)

    def run(*args):
        res = call(*args, *moved)
        own = res[0] if single else list(res[:n_out])
        return own, list(res[n_out:])

    return run


def _sds(shape, dtype=f32):
    return jax.ShapeDtypeStruct(tuple(shape), dtype)


def _dot(a, b, dims, precision=None):
    return lax.dot_general(a, b, (dims, ((), ())), preferred_element_type=f32, precision=precision)


def _bdot(a, b, dims=NN):
    return _dot(a.astype(bf16), b.astype(bf16), dims)


@jax.custom_vjp
def _mm(a, w):
    return _bdot(a, w, NN)


def _mm_fwd(a, w):
    return _bdot(a, w, NN), (a, w)


def _mm_bwd(res, g):
    a, w = res
    return _bdot(g, w, NT), _bdot(a, g, TN)


_mm.defvjp(_mm_fwd, _mm_bwd)


@jax.custom_vjp
def _mm_nt(a, b):
    return _bdot(a, b, NT)


def _mm_nt_fwd(a, b):
    return _bdot(a, b, NT), (a, b)


def _mm_nt_bwd(res, g):
    a, b = res
    return _bdot(g, b, NN), _bdot(g, a, TN)


_mm_nt.defvjp(_mm_nt_fwd, _mm_nt_bwd)


@jax.custom_vjp
def _mm_tn(a, b):
    return _bdot(a, b, TN)


def _mm_tn_fwd(a, b):
    return _bdot(a, b, TN), (a, b)


def _mm_tn_bwd(res, g):
    a, b = res
    return _bdot(b, g, NT), _bdot(a, g, NN)


_mm_tn.defvjp(_mm_tn_fwd, _mm_tn_bwd)


def _rms(x, g):
    return x * lax.rsqrt(jnp.mean(x * x, axis=-1, keepdims=True) + NORM_EPS) * g


def _sigmoid(x):
    return 1.0 / (1.0 + jnp.exp(-x))


def _gelu(x):
    return 0.5 * x * (1.0 + jnp.tanh(0.7978845608028654 * (x + 0.044715 * x * x * x)))


def _softplus(x):
    return jnp.maximum(x, 0.0) + jnp.log(1.0 + jnp.exp(-jnp.abs(x)))


@jax.custom_vjp
def _group_sum(x):
    w = x.shape[-1]
    r = lax.broadcasted_iota(jnp.int32, (w, w), 0) // RW_HEAD
    c = lax.broadcasted_iota(jnp.int32, (w, w), 1) // RW_HEAD
    ones = (r == c).astype(bf16)
    hi = x.astype(bf16)
    lo = (x - hi.astype(f32)).astype(bf16)
    return _dot(jnp.concatenate([hi, lo], axis=1), jnp.concatenate([ones, ones], axis=0), NN)


_group_sum.defvjp(lambda x: (_group_sum(x), None), lambda _, g: (_group_sum(g),))


@jax.custom_vjp
def _swap(x):
    w = x.shape[-1]
    lane = lax.broadcasted_iota(jnp.int32, x.shape, 1) % 128
    lo = (lane >= 64) & (lane < 80)
    hi = (lane >= 80) & (lane < 96)
    return jnp.where(lo, pltpu.roll(x, w - 16, 1), jnp.where(hi, pltpu.roll(x, 16, 1), 0.0))


_swap.defvjp(lambda x: (_swap(x), None), lambda _, g: (_swap(g),))


def _rope(x, c, s):
    return x * c + _swap(x) * s


def _row_spec(tm, width, blk):
    return pl.BlockSpec((tm, width), lambda i, blk=blk: (i, blk))


def _full_spec(a):
    nd = a.ndim
    return pl.BlockSpec(a.shape, lambda i, nd=nd: (0,) * nd)


def _rowwise_fwd(name, f, rows, weights, outs, tm):
    n = rows[0][0].shape[0]
    nr, nw = len(rows), len(weights)

    def body(*refs):
        vals = [r[...].astype(f32) for r in refs[:nr + nw]]
        res = f(*vals)
        for o_ref, o in zip(refs[nr + nw:], res):
            o_ref[...] = o.astype(o_ref.dtype)

    return _pc(
        body, name=name, grid=(n // tm,),
        in_specs=[_row_spec(tm, w, b) for _, w, b in rows] + [_full_spec(w) for w in weights],
        out_specs=[_row_spec(tm, w, 0) for w, _ in outs],
        out_shape=[_sds((n, w), dt) for w, dt in outs], sem=("parallel",),
    )(*[a for a, _, _ in rows], *weights)


def _rowwise_bwd(name, f, rows, weights, cots, tm, drows):
    n = rows[0][0].shape[0]
    nr, nw, nc = len(rows), len(weights), len(cots)
    want = [k for k, dt in enumerate(drows) if dt is not None]

    def body(*refs):
        vals = [r[...].astype(f32) for r in refs[:nr + nw]]
        cot = tuple(r[...].astype(f32) for r in refs[nr + nw:nr + nw + nc])
        _, vjp = jax.vjp(f, *vals)
        grads = vjp(cot)
        outs = refs[nr + nw + nc:]
        for o_ref, k in zip(outs[:len(want)], want):
            o_ref[...] = grads[k].astype(o_ref.dtype)
        first = pl.program_id(0) == 0
        for o_ref, g in zip(outs[len(want):], grads[nr:]):
            @pl.when(first)
            def _(o_ref=o_ref, g=g):
                o_ref[...] = g

            @pl.when(jnp.logical_not(first))
            def _(o_ref=o_ref, g=g):
                o_ref[...] += g

    res = _pc(
        body, name=name, grid=(n // tm,),
        in_specs=[_row_spec(tm, w, b) for _, w, b in rows] + [_full_spec(w) for w in weights]
        + [_row_spec(tm, w, b) for _, w, b in cots],
        out_specs=[_row_spec(tm, rows[k][1], 0) for k in want] + [_full_spec(w) for w in weights],
        out_shape=[_sds((n, rows[k][1]), drows[k]) for k in want] + [_sds(w.shape) for w in weights],
        sem=("arbitrary",),
    )(*[a for a, _, _ in rows], *weights, *[a for a, _, _ in cots])
    return res[:len(want)], res[len(want):]


def _inproj_fwd(x2, g, w, name, ride=None):
    n = x2.shape[0]
    tm, tn = min(1024, n), 1024

    def body(x_ref, g_ref, w_ref, p_ref, h_ref):
        @pl.when(pl.program_id(1) == 0)
        def _():
            h_ref[...] = _rms(x_ref[...], g_ref[...]).astype(bf16)

        p_ref[...] = jnp.dot(h_ref[...], w_ref[...], preferred_element_type=f32)

    return _pc(
        body, name=name, grid=(n // tm, P_W // tn),
        in_specs=[pl.BlockSpec((tm, D), lambda i, j: (i, 0)), pl.BlockSpec((1, D), lambda i, j: (0, 0)),
                  pl.BlockSpec((D, tn), lambda i, j: (0, j))],
        out_specs=[pl.BlockSpec((tm, tn), lambda i, j: (i, j)), pl.BlockSpec((tm, D), lambda i, j: (i, 0))],
        out_shape=[_sds((n, P_W)), _sds((n, D), bf16)], sem=("parallel", "arbitrary"), ride=ride,
    )(x2, g, w)


def _norm_matmul_bwd(dy, w, x2, g, dres, name, ride=None):
    n, k = dy.shape
    tm = min(512, n)
    tk = 1024 if k % 1024 == 0 else 1408
    nk = k // tk

    def body(dy_ref, w_ref, x_ref, g_ref, dr_ref, dx_ref, dg_ref, acc):
        i, j = pl.program_id(0), pl.program_id(1)

        @pl.when(j == 0)
        def _():
            acc[...] = jnp.zeros_like(acc)

        @pl.when((i == 0) & (j == 0))
        def _():
            dg_ref[...] = jnp.zeros_like(dg_ref)

        acc[...] += _dot(dy_ref[...], w_ref[...], NT)

        @pl.when(j == nk - 1)
        def _():
            _, vjp = jax.vjp(_rms, x_ref[...], g_ref[...])
            dx, dg = vjp(acc[...])
            dx_ref[...] = dr_ref[...] + dx
            dg_ref[...] += dg

    return _pc(
        body, name=name, grid=(n // tm, nk),
        in_specs=[pl.BlockSpec((tm, tk), lambda i, j: (i, j)), pl.BlockSpec((D, tk), lambda i, j: (0, j)),
                  pl.BlockSpec((tm, D), lambda i, j: (i, 0)), pl.BlockSpec((1, D), lambda i, j: (0, 0)),
                  pl.BlockSpec((tm, D), lambda i, j: (i, 0))],
        out_specs=[pl.BlockSpec((tm, D), lambda i, j: (i, 0)), pl.BlockSpec((1, D), lambda i, j: (0, 0))],
        out_shape=[_sds((n, D)), _sds((1, D))], scratch=[pltpu.VMEM((tm, D), f32)], sem=("arbitrary", "arbitrary"),
        ride=ride,
    )(dy, w, x2, g, dres)


def _matmul_tn(a, g, name, ride=None):
    n, k = a.shape
    m = g.shape[1]
    tr = min(512, n)
    tk = k if k <= 1024 else 1408
    tn = m if m <= 1024 else (1024 if m % 1024 == 0 else 1408)
    nr = n // tr

    def body(a_ref, g_ref, o_ref):
        @pl.when(pl.program_id(2) == 0)
        def _():
            o_ref[...] = jnp.zeros_like(o_ref)

        o_ref[...] += _dot(a_ref[...], g_ref[...], TN)

    return _pc(
        body, name=name, grid=(k // tk, m // tn, nr),
        in_specs=[pl.BlockSpec((tr, tk), lambda i, j, r: (r, i)), pl.BlockSpec((tr, tn), lambda i, j, r: (r, j))],
        out_specs=pl.BlockSpec((tk, tn), lambda i, j, r: (i, j)),
        out_shape=_sds((k, m)), sem=("parallel", "parallel", "arbitrary"), ride=ride,
    )(a, g)


def _f_mla_proj(ckv, slab, pq, c, s, qg, kg, wq, wk, wv):
    c8, s8 = jnp.concatenate([c] * HEADS, axis=1), jnp.concatenate([s] * HEADS, axis=1)
    q = _rope(_mm(_rms(pq, qg), wq), c8, s8)
    cn = _rms(ckv, kg)
    k = _mm(cn, wk) + jnp.concatenate([_rope(slab, c, s)] * HEADS, axis=1)
    return q, k, _mm(cn, wv)


def _attn_fwd(q, k, v, bsz, seq, name, ride=None):
    n = q.shape[0]
    tq = min(256, seq)
    nq = seq // tq

    def body(q_ref, k_ref, v_ref, o_ref, lse_ref):
        lane = lax.broadcasted_iota(jnp.int32, (tq, 128), 1) < 64
        vv = v_ref[...]
        two = range(2)
        s = [_dot(q_ref[:, h * 128:(h + 1) * 128], k_ref[:, h * 128:(h + 1) * 128], NT) * ATT_SCALE for h in two]
        m = [jnp.max(s[h], axis=-1, keepdims=True) for h in two]
        e = [jnp.exp(s[h] - m[h]) for h in two]
        l = [jnp.sum(e[h], axis=-1, keepdims=True) for h in two]
        p = [(e[h] / l[h]).astype(bf16) for h in two]
        outs = [_dot(p[h], vv, NN) for h in two]
        o_ref[...] = jnp.where(lane, outs[0], outs[1])
        lse_ref[...] = jnp.where(lane, m[0] + jnp.log(l[0]), m[1] + jnp.log(l[1]))

    row = pl.BlockSpec((tq, 128), lambda b, h, i: (b * nq + i, h))
    return _pc(
        body, name=name, grid=(bsz, HEADS // 2, nq),
        in_specs=[pl.BlockSpec((tq, 256), lambda b, h, i: (b * nq + i, h)),
                  pl.BlockSpec((seq, 256), lambda b, h, i: (b, h)),
                  pl.BlockSpec((seq, 128), lambda b, h, i: (b, h))],
        out_specs=[row, row],
        out_shape=[_sds((n, HEADS * V_HEAD)), _sds((n, HEADS * V_HEAD))], sem=("parallel", "parallel", "parallel"),
        ride=ride,
    )(q, k, v)


def _attn_bwd(q, k, v, o, lse, do, bsz, seq, name, ride=None):
    n = q.shape[0]
    tq = min(256, seq)
    nq = seq // tq

    def body(q_ref, k_ref, v_ref, o_ref, lse_ref, do_ref, dq_ref, dk_ref, dv_ref):
        @pl.when(pl.program_id(2) == 0)
        def _():
            dk_ref[...] = jnp.zeros_like(dk_ref)
            dv_ref[...] = jnp.zeros_like(dv_ref)

        lane = lax.broadcasted_iota(jnp.int32, (tq, 128), 1) < 64
        vv = v_ref[...]
        for h in range(2):
            qh, kh = q_ref[:, h * 128:(h + 1) * 128], k_ref[:, h * 128:(h + 1) * 128]
            p = jnp.exp(_dot(qh, kh, NT) * ATT_SCALE - lse_ref[:, 64 * h:64 * h + 1])
            doh = jnp.where(lane if h == 0 else jnp.logical_not(lane), do_ref[...], 0.0)
            delta = jnp.sum(doh * o_ref[...], axis=-1, keepdims=True)
            dob = doh.astype(bf16)
            dp = _dot(dob, vv, NT)
            ds = (p * (dp - delta) * ATT_SCALE).astype(bf16)
            dq_ref[:, h * 128:(h + 1) * 128] = _dot(ds, kh, NN)
            dk_ref[:, h * 128:(h + 1) * 128] += _dot(ds, qh, TN)
            dv_ref[...] += _dot(p.astype(bf16), dob, TN)

    return _pc(
        body, name=name, grid=(bsz, HEADS // 2, nq),
        in_specs=[pl.BlockSpec((tq, 256), lambda b, h, i: (b * nq + i, h)),
                  pl.BlockSpec((seq, 256), lambda b, h, i: (b, h)),
                  pl.BlockSpec((seq, 128), lambda b, h, i: (b, h)),
                  pl.BlockSpec((tq, 128), lambda b, h, i: (b * nq + i, h)),
                  pl.BlockSpec((tq, 128), lambda b, h, i: (b * nq + i, h)),
                  pl.BlockSpec((tq, 128), lambda b, h, i: (b * nq + i, h))],
        out_specs=[pl.BlockSpec((tq, 256), lambda b, h, i: (b * nq + i, h)),
                   pl.BlockSpec((seq, 256), lambda b, h, i: (b, h)),
                   pl.BlockSpec((seq, 128), lambda b, h, i: (b, h))],
        out_shape=[_sds((n, HEADS * 128)), _sds((n, HEADS * 128)), _sds((n, HEADS * V_HEAD))],
        sem=("parallel", "parallel", "arbitrary"), ride=ride,
    )(q, k, v, o, lse, do)


def _f_sg(pu, pv, lg, lb, bias, *ws):
    u, vv = _gelu(pu), _gelu(pv)
    mu = jnp.mean(vv, axis=-1, keepdims=True)
    d = vv - mu
    vv = d * lax.rsqrt(jnp.mean(d * d, axis=-1, keepdims=True) + LN_EPS) * lg + lb
    group = lax.broadcasted_iota(jnp.int32, (SG_CHUNK, SG_DIM), 1) // (SG_DIM // SG_GROUPS)
    mixed = bias
    for k, w in enumerate(ws):
        mixed = mixed + jnp.where(group == k, _mm(w, vv), 0.0)
    return (u * mixed,)


def _shift_mean(a, prev_row, next_row):
    t = a.shape[0]
    row = lax.broadcasted_iota(jnp.int32, a.shape, 0)
    prev = jnp.where(row == 0, prev_row, pltpu.roll(a, 1, 0))
    nxt = jnp.where(row == t - 1, next_row, pltpu.roll(a, t - 1, 0))
    return 0.5 * (prev + nxt)


def _halo_specs(tm, width, blk, nblk8):
    h = tm // 8
    return [pl.BlockSpec((tm, width), lambda i: (i, blk)),
            pl.BlockSpec((8, width), lambda i: (jnp.maximum(i * h - 1, 0), blk)),
            pl.BlockSpec((8, width), lambda i: (jnp.minimum((i + 1) * h, nblk8 - 1), blk))]


def _edge_rows(i, tm, seq, pv_ref, nx_ref, scale=None):
    first = (i * tm) % seq == 0
    last = ((i + 1) * tm) % seq == 0
    pv, nx = pv_ref[7:8, :], nx_ref[0:1, :]
    if scale is not None:
        pv, nx = pv * scale, nx * scale
    return jnp.where(first, 0.0, pv), jnp.where(last, 0.0, nx)


def _shift_fwd(p, mu, seq, name):
    n = p.shape[0]
    tm = min(256, seq)
    blk = O_RW // RW_W

    def body(x_ref, pv_ref, nx_ref, mu_ref, z_ref):
        x = x_ref[...]
        pv, nx = _edge_rows(pl.program_id(0), tm, seq, pv_ref, nx_ref)
        z_ref[...] = x + mu_ref[...] * (_shift_mean(x, pv, nx) - x)

    return _pc(
        body, name=name, grid=(n // tm,),
        in_specs=_halo_specs(tm, RW_W, blk, n // 8) + [pl.BlockSpec((1, RW_W), lambda i: (0, 0))],
        out_specs=pl.BlockSpec((tm, RW_W), lambda i: (i, 0)), out_shape=_sds((n, RW_W)), sem=("parallel",),
    )(p, p, p, mu)


def _shift_bwd(dz, p, mu, seq, name):
    n = p.shape[0]
    tm = min(256, seq)
    blk = O_RW // RW_W

    def body(dz_ref, dpv_ref, dnx_ref, x_ref, pv_ref, nx_ref, mu_ref, dx_ref, dmu_ref):
        i = pl.program_id(0)
        mu_v = mu_ref[...]
        dzv = dz_ref[...]
        m = dzv * mu_v
        mpv, mnx = _edge_rows(i, tm, seq, dpv_ref, dnx_ref, mu_v)
        dx_ref[...] = (dzv - m + _shift_mean(m, mpv, mnx)).astype(dx_ref.dtype)
        x = x_ref[...]
        pv, nx = _edge_rows(i, tm, seq, pv_ref, nx_ref)
        part = jnp.sum(dzv * (_shift_mean(x, pv, nx) - x), axis=0, keepdims=True)

        @pl.when(i == 0)
        def _():
            dmu_ref[...] = part

        @pl.when(i != 0)
        def _():
            dmu_ref[...] += part

    return _pc(
        body, name=name, grid=(n // tm,),
        in_specs=_halo_specs(tm, RW_W, 0, n // 8) + _halo_specs(tm, RW_W, blk, n // 8)
        + [pl.BlockSpec((1, RW_W), lambda i: (0, 0))],
        out_specs=[pl.BlockSpec((tm, RW_W), lambda i: (i, 0)), pl.BlockSpec((1, RW_W), lambda i: (0, 0))],
        out_shape=[_sds((n, RW_W), bf16), _sds((1, RW_W))], sem=("arbitrary",),
    )(dz, dz, dz, p, p, p, mu)


def _f_rw_pre(k, wl, al, gl, w0, a0, w2, a2, g2, k_k, k_a):
    w = w0 + _mm(jnp.tanh(wl), w2)
    lw = -jnp.exp(-_softplus(-w) - 0.5)
    a = _sigmoid(a0 + _mm(al, a2))
    g = _mm(_sigmoid(gl), g2)
    kkr = k * k_k
    kk = kkr / jnp.maximum(jnp.sqrt(_group_sum(kkr * kkr)), 1e-12)
    two = lambda t: jnp.concatenate([t, t], axis=1)
    kd = two(k) * (1.0 + (a - 1.0) * two(k_a))
    bd = two(kk) * a
    return lw, kd, kk, bd, g


def _f_rw_post(y0, y1, r, v, kd0, kd1, g, r_k, ln_g, ln_b):
    y = y0 + y1
    mean = _group_sum(y) * (1.0 / RW_HEAD)
    d = y - mean
    var = _group_sum(d * d) * (1.0 / RW_HEAD)
    yn = d * lax.rsqrt(var + GN_EPS) * ln_g + ln_b
    bonus = _group_sum(r * (kd0 + kd1) * r_k)
    return ((yn + bonus * v) * g,)


@jax.custom_vjp
def _tri_inv(mats):
    c = mats[0].shape[0]
    row = lax.broadcasted_iota(jnp.int32, (c, c), 0)
    col = lax.broadcasted_iota(jnp.int32, (c, c), 1)
    eye = (row == col).astype(f32)
    blk = lambda b: (row // b) == (col // b)
    ld = [jnp.where(blk(8), a, 0.0) for a in mats]
    l2 = [_bdot(x, x) for x in ld]
    l4 = [_bdot(x, x) for x in l2]
    t = [_bdot(eye - x, eye + y) for x, y in zip(ld, l2)]
    t = [_bdot(x, eye + y) for x, y in zip(t, l4)]
    b = 8
    while b < c:
        sub = blk(2 * b) & jnp.logical_not(blk(b))
        p = [_bdot(x, jnp.where(sub, a, 0.0)) for x, a in zip(t, mats)]
        t = [x - _bdot(y, x) for x, y in zip(t, p)]
        b *= 2
    return tuple(t)


def _tri_inv_fwd(mats):
    t = _tri_inv(mats)
    return t, t


def _tri_inv_bwd(ts, gs):
    p = [_bdot(t, g, TN) for t, g in zip(ts, gs)]
    return (tuple(-_bdot(x, t, NT) for x, t in zip(p, ts)),)


_tri_inv.defvjp(_tri_inv_fwd, _tri_inv_bwd)


@jax.custom_vjp
def _tri_inv_saved(mats, ts):
    return ts


_tri_inv_saved.defvjp(lambda mats, ts: (ts, ts),
                      lambda ts, gs: (_tri_inv_bwd(ts, gs)[0], tuple(jnp.zeros_like(t) for t in ts)))


def _split3(x):
    h = x.astype(bf16)
    r = x - h.astype(f32)
    m = r.astype(bf16)
    return h, m, (r - m.astype(f32)).astype(bf16)


@jax.custom_vjp
def _mask_mm(mask, x):
    mb = mask.astype(bf16)
    return _dot(jnp.concatenate([mb, mb, mb], axis=1), jnp.concatenate(_split3(x), axis=0), NN)


def _mask_mm_bwd(mask, g):
    mb = mask.astype(bf16)
    return jnp.zeros_like(mask), _dot(jnp.concatenate([mb, mb, mb], axis=0), jnp.concatenate(_split3(g), axis=0), TN)


_mask_mm.defvjp(lambda mask, x: (_mask_mm(mask, x), mask), _mask_mm_bwd)


@jax.custom_vjp
def _split_lanes(x):
    h = x.shape[1] // 2
    return x[:, :h], x[:, h:]


_split_lanes.defvjp(lambda x: (_split_lanes(x), None), lambda _, g: (jnp.concatenate(g, axis=1),))


def _scan_chunk(s0, r, v, kk, lw, kd, bd, rev, inv=None):
    n = len(r)
    each = range(n)
    c = r[0].shape[0]
    row = lax.broadcasted_iota(jnp.int32, (c, 2 * c), 0)
    col = lax.broadcasted_iota(jnp.int32, (c, 2 * c), 1) % c
    ahead = jnp.where(rev, col - row, row - col)
    before = ahead > 0
    incl = ahead >= 0
    lane = lax.broadcasted_iota(jnp.int32, (1, 128), 1)
    m0 = (lane < 64).astype(f32)
    heads = lambda t: jnp.concatenate([t * m0, t * (1.0 - m0)], axis=0)
    bd_mask = ((lax.broadcasted_iota(jnp.int32, (128, 128), 0) // 64)
               == (lax.broadcasted_iota(jnp.int32, (128, 128), 1) // 64)).astype(f32)
    tot = [jnp.sum(lw[i], axis=0, keepdims=True) for i in each]
    row1 = lax.broadcasted_iota(jnp.int32, (c, c), 0)
    col1 = lax.broadcasted_iota(jnp.int32, (c, c), 1)
    upto = (jnp.where(rev, col1 - row1, row1 - col1) >= 0).astype(f32)
    lp = [_mask_mm(upto, lw[i]) - 0.5 * tot[i] for i in each]
    eg = [jnp.exp(lp[i]) for i in each]
    ieg = [jnp.exp(-lp[i]) for i in each]
    rt = [r[i] * eg[i] for i in each]
    kt = [kd[i] * ieg[i] for i in each]
    bt = [bd[i] * ieg[i] for i in each]
    at = [kk[i] * jnp.exp(lp[i] - lw[i]) for i in each]
    etot = [jnp.exp(0.5 * tot[i]) for i in each]
    si = [s0[i] * etot[i] for i in each]
    bth = [heads(bt[i]) for i in each]
    kth = [heads(kt[i]) for i in each]
    vh = [heads(v[i]) for i in each]
    a_ab = [jnp.where(before, _mm_nt(at[i], bth[i]), 0.0) for i in each]
    a_ak = [jnp.where(before, _mm_nt(at[i], kth[i]), 0.0) for i in each]
    a_rb = [jnp.where(incl, _mm_nt(rt[i], bth[i]), 0.0) for i in each]
    a_rk = [jnp.where(incl, _mm_nt(rt[i], kth[i]), 0.0) for i in each]
    halves = [_split_lanes(a_ab[i]) for i in each]
    mats = tuple(m for pair in halves for m in pair)
    inv = _tri_inv(mats) if inv is None else _tri_inv_saved(mats, inv)
    t = [jnp.concatenate([inv[2 * i], inv[2 * i + 1]], axis=1) for i in each]
    x0 = [_mm_nt(at[i], si[i]) for i in each]
    x = [x0[i] + _mm(a_ak[i], vh[i]) for i in each]
    u = [-_mm(t[i], heads(x[i])) for i in each]
    y0 = [_mm_nt(rt[i], si[i]) for i in each]
    y = [y0[i] + _mm(jnp.concatenate([a_rb[i], a_rk[i]], axis=1), jnp.concatenate([heads(u[i]), vh[i]], axis=0))
         for i in each]
    ds = [_mm_tn(jnp.concatenate([u[i], v[i]], axis=0), jnp.concatenate([bt[i], kt[i]], axis=0)) for i in each]
    se = [(si[i] + ds[i] * bd_mask) * etot[i] for i in each]
    return tuple(y), tuple(se), inv


PAIRS = HEADS // 2


def _scan_specs(nc, bsz, flip=False):
    def cc(d, c):
        c = nc - 1 - c if flip else c
        return jnp.where(d == 0, c, nc - 1 - c)

    rowblk = lambda d, b, c: b * nc + cc(d, c)
    zspec = lambda blk: pl.BlockSpec((CHUNK, RW_DIM), lambda d, b, c: (rowblk(d, b, c), blk))
    dspec = pl.BlockSpec((CHUNK, RW_DIM), lambda d, b, c: (rowblk(d, b, c), d))
    yspec = pl.BlockSpec((None, CHUNK, RW_DIM), lambda d, b, c: (d, rowblk(d, b, c), 0))
    sspec = pl.BlockSpec((None, PAIRS, 128, 128), lambda d, b, c: ((d * bsz + b) * nc + cc(d, c), 0, 0, 0))
    tspec = pl.BlockSpec((None, PAIRS, 128, 256), lambda d, b, c: ((d * bsz + b) * nc + cc(d, c), 0, 0, 0))
    return zspec, dspec, yspec, sspec, tspec


def _scan_fwd(z, lw, kd, kk, bd, bsz, seq, name, ride=None):
    n = z.shape[0]
    nc = seq // CHUNK
    zspec, dspec, yspec, sspec, tspec = _scan_specs(nc, bsz)

    def body(r_ref, v_ref, kk_ref, lw_ref, kd_ref, bd_ref, y_ref, s_ref, t_ref, st):
        @pl.when(pl.program_id(2) == 0)
        def _():
            st[...] = jnp.zeros_like(st)

        rev = pl.program_id(0) == 1
        lanes = [slice(h * 128, (h + 1) * 128) for h in range(PAIRS)]
        s0 = tuple(st[h] for h in range(PAIRS))
        ops = [tuple(ref[:, ln] for ln in lanes) for ref in (r_ref, v_ref, kk_ref, lw_ref, kd_ref, bd_ref)]
        y, se, inv = _scan_chunk(s0, *ops, rev)
        for h, ln in enumerate(lanes):
            s_ref[h] = s0[h]
            t_ref[h, :, :128] = inv[2 * h]
            t_ref[h, :, 128:] = inv[2 * h + 1]
            y_ref[:, ln] = y[h]
            st[h] = se[h]

    return _pc(
        body, name=name, grid=(2, bsz, nc),
        in_specs=[zspec(0), zspec(2), zspec(0), dspec, dspec, dspec],
        out_specs=[yspec, sspec, tspec],
        out_shape=[_sds((2, n, RW_DIM)), _sds((2 * bsz * nc, PAIRS, 128, 128)), _sds((2 * bsz * nc, PAIRS, 128, 256))],
        scratch=[pltpu.VMEM((PAIRS, 128, 128), f32)], sem=("parallel", "parallel", "arbitrary"), ride=ride,
    )(z, z, kk, lw, kd, bd)


def _scan_bwd(z, lw, kd, kk, bd, s_in, t_in, dy, bsz, seq, name, ride=None):
    n = z.shape[0]
    nc = seq // CHUNK
    zspec, dspec, yspec, sspec, tspec = _scan_specs(nc, bsz, flip=True)

    def body(r_ref, v_ref, kk_ref, lw_ref, kd_ref, bd_ref, s_ref, t_ref, dy_ref,
             dr_ref, dv_ref, dkk_ref, dlw_ref, dkd_ref, dbd_ref, dst):
        @pl.when(pl.program_id(2) == 0)
        def _():
            dst[...] = jnp.zeros_like(dst)

        rev = pl.program_id(0) == 1
        lanes = [slice(h * 128, (h + 1) * 128) for h in range(PAIRS)]
        s0 = tuple(s_ref[h] for h in range(PAIRS))
        inv = tuple(t_ref[h, :, a * 128:(a + 1) * 128] for h in range(PAIRS) for a in range(2))
        ops = [tuple(ref[:, ln] for ln in lanes) for ref in (r_ref, v_ref, kk_ref, lw_ref, kd_ref, bd_ref)]
        cot = (tuple(dy_ref[:, ln] for ln in lanes), tuple(dst[h] for h in range(PAIRS)))
        _, vjp = jax.vjp(lambda *a: _scan_chunk(*a, rev=rev, inv=inv)[:2], s0, *ops)
        grads = vjp(cot)
        for h, ln in enumerate(lanes):
            dst[h] = grads[0][h]
            for o_ref, g in zip((dr_ref, dv_ref, dkk_ref, dlw_ref, dkd_ref, dbd_ref), grads[1:]):
                o_ref[:, ln] = g[h]

    return _pc(
        body, name=name, grid=(2, bsz, nc),
        in_specs=[zspec(0), zspec(2), zspec(0), dspec, dspec, dspec, sspec, tspec, zspec(0)],
        out_specs=[yspec, yspec, yspec, dspec, dspec, dspec],
        out_shape=[_sds((2, n, RW_DIM))] * 3 + [_sds((n, 2 * RW_DIM))] * 3,
        scratch=[pltpu.VMEM((PAIRS, 128, 128), f32)], sem=("parallel", "parallel", "arbitrary"), ride=ride,
    )(z, z, kk, lw, kd, bd, s_in, t_in, dy)


def _merge_fwd(x2, p, ya, yb, yc, gb, wb, wo, name):
    n = x2.shape[0]
    tm = min(256, n)

    def body(x_ref, pg_ref, ya_ref, yb_ref, yc_ref, gb_ref, wb_ref, wo_ref, o_ref):
        gates = _sigmoid(pg_ref[...] + gb_ref[...])
        merged = jnp.zeros((tm, D), f32)
        for k, y_ref in enumerate((ya_ref, yb_ref, yc_ref)):
            merged += gates[:, k * D:(k + 1) * D] * _bdot(y_ref[...], wb_ref[k])
        o_ref[...] = x_ref[...] + _bdot(merged, wo_ref[...])

    row = lambda w, b=0: pl.BlockSpec((tm, w), lambda i, b=b: (i, b))
    return _pc(
        body, name=name, grid=(n // tm,),
        in_specs=[row(D), row(3 * D, O_GATE // (3 * D)), row(512), row(512), row(512),
                  pl.BlockSpec((1, 3 * D), lambda i: (0, 0)), pl.BlockSpec((3, 512, D), lambda i: (0, 0, 0)),
                  pl.BlockSpec((D, D), lambda i: (0, 0))],
        out_specs=row(D), out_shape=_sds((n, D)), sem=("parallel",),
    )(x2, p, ya, yb, yc, gb, wb, wo)


def _merge_bwd(dx1, p, ya, yb, yc, gb, wb, wo, name):
    n = dx1.shape[0]
    tm = min(256, n)

    def body(dx_ref, pg_ref, ya_ref, yb_ref, yc_ref, gb_ref, wb_ref, wo_ref,
             dpg_ref, dya_ref, dyb_ref, dyc_ref, dt_ref, mg_ref, dgb_ref):
        gates = _sigmoid(pg_ref[...] + gb_ref[...])
        dmerged = _bdot(dx_ref[...], wo_ref[...], NT)
        merged = jnp.zeros((tm, D), f32)
        dpg = []
        for k, (y_ref, dy_ref) in enumerate(((ya_ref, dya_ref), (yb_ref, dyb_ref), (yc_ref, dyc_ref))):
            gk = gates[:, k * D:(k + 1) * D]
            tk = _bdot(y_ref[...], wb_ref[k])
            merged += gk * tk
            dpg.append(dmerged * tk * gk * (1.0 - gk))
            dtk = dmerged * gk
            dt_ref[:, k * D:(k + 1) * D] = dtk.astype(bf16)
            dy_ref[...] = _bdot(dtk, wb_ref[k], NT)
        dpg = jnp.concatenate(dpg, axis=1)
        dpg_ref[...] = dpg.astype(bf16)
        mg_ref[...] = merged.astype(bf16)
        part = jnp.sum(dpg, axis=0, keepdims=True)

        @pl.when(pl.program_id(0) == 0)
        def _():
            dgb_ref[...] = part

        @pl.when(pl.program_id(0) != 0)
        def _():
            dgb_ref[...] += part

    row = lambda w, b=0: pl.BlockSpec((tm, w), lambda i, b=b: (i, b))
    return _pc(
        body, name=name, grid=(n // tm,),
        in_specs=[row(D), row(3 * D, O_GATE // (3 * D)), row(512), row(512), row(512),
                  pl.BlockSpec((1, 3 * D), lambda i: (0, 0)), pl.BlockSpec((3, 512, D), lambda i: (0, 0, 0)),
                  pl.BlockSpec((D, D), lambda i: (0, 0))],
        out_specs=[row(3 * D), row(512), row(512), row(512), row(3 * D), row(D),
                   pl.BlockSpec((1, 3 * D), lambda i: (0, 0))],
        out_shape=[_sds((n, 3 * D), bf16), _sds((n, 512)), _sds((n, 512)), _sds((n, 512)), _sds((n, 3 * D), bf16),
                   _sds((n, D), bf16), _sds((1, 3 * D))],
        sem=("arbitrary",),
    )(dx1, p, ya, yb, yc, gb, wb, wo)


FF_T = 1408


def _ffn_fwd(x1, g, wg, wu, wd, name):
    n = x1.shape[0]
    tm = min(512, n)
    nf = D_FF // FF_T

    def body(x_ref, g_ref, wg_ref, wu_ref, wd_ref, o_ref, hs):
        j = pl.program_id(1)

        @pl.when(j == 0)
        def _():
            hs[...] = _rms(x_ref[...], g_ref[...]).astype(bf16)
            o_ref[...] = x_ref[...]

        a = _dot(hs[...], wg_ref[...], NN)
        b = _dot(hs[...], wu_ref[...], NN)
        o_ref[...] += _bdot(a * _sigmoid(a) * b, wd_ref[...])

    return _pc(
        body, name=name, grid=(n // tm, nf),
        in_specs=[pl.BlockSpec((tm, D), lambda i, j: (i, 0)), pl.BlockSpec((1, D), lambda i, j: (0, 0)),
                  pl.BlockSpec((D, FF_T), lambda i, j: (0, j)), pl.BlockSpec((D, FF_T), lambda i, j: (0, j)),
                  pl.BlockSpec((FF_T, D), lambda i, j: (j, 0))],
        out_specs=pl.BlockSpec((tm, D), lambda i, j: (i, 0)), out_shape=_sds((n, D)),
        scratch=[pltpu.VMEM((tm, D), bf16)], sem=("parallel", "arbitrary"),
    )(x1, g, wg, wu, wd)


def _ffn_bwd(dx2, x1, g, wg, wu, wd, name):
    n = x1.shape[0]
    tm = min(512, n)
    nf = D_FF // FF_T

    def body(dx_ref, x_ref, g_ref, wg_ref, wu_ref, wd_ref, dx1_ref, dg_ref, h_ref, da_ref, db_ref, hm_ref, acc):
        i, j = pl.program_id(0), pl.program_id(1)

        @pl.when(j == 0)
        def _():
            h_ref[...] = _rms(x_ref[...], g_ref[...]).astype(bf16)
            acc[...] = jnp.zeros_like(acc)

        @pl.when((i == 0) & (j == 0))
        def _():
            dg_ref[...] = jnp.zeros_like(dg_ref)

        h = h_ref[...]
        a = _dot(h, wg_ref[...], NN)
        b = _dot(h, wu_ref[...], NN)
        sg = _sigmoid(a)
        s = a * sg
        dhm = _bdot(dx_ref[...], wd_ref[...], NT)
        da = (dhm * b * (sg * (1.0 + a * (1.0 - sg)))).astype(bf16)
        db = (dhm * s).astype(bf16)
        da_ref[...] = da
        db_ref[...] = db
        hm_ref[...] = (s * b).astype(bf16)
        acc[...] += _dot(da, wg_ref[...], NT) + _dot(db, wu_ref[...], NT)

        @pl.when(j == nf - 1)
        def _():
            _, vjp = jax.vjp(_rms, x_ref[...], g_ref[...])
            dx, dg = vjp(acc[...])
            dx1_ref[...] = dx_ref[...] + dx
            dg_ref[...] += dg

    rowf = pl.BlockSpec((tm, FF_T), lambda i, j: (i, j))
    rowd = pl.BlockSpec((tm, D), lambda i, j: (i, 0))
    vec = pl.BlockSpec((1, D), lambda i, j: (0, 0))
    return _pc(
        body, name=name, grid=(n // tm, nf),
        in_specs=[rowd, rowd, vec, pl.BlockSpec((D, FF_T), lambda i, j: (0, j)),
                  pl.BlockSpec((D, FF_T), lambda i, j: (0, j)), pl.BlockSpec((FF_T, D), lambda i, j: (j, 0))],
        out_specs=[rowd, vec, rowd, rowf, rowf, rowf],
        out_shape=[_sds((n, D)), _sds((1, D)), _sds((n, D), bf16), _sds((n, D_FF), bf16), _sds((n, D_FF), bf16),
                   _sds((n, D_FF), bf16)],
        scratch=[pltpu.VMEM((tm, D), f32)], sem=("arbitrary", "arbitrary"),
    )(dx2, x1, g, wg, wu, wd)


def _loss_head(x2, g, tgt, name):
    n = x2.shape[0]
    tm = min(512, n)

    def f(x, gg, t):
        e = _rms(x, gg) - t
        return 0.5 * jnp.sum(jnp.mean(e * e, axis=-1, keepdims=True))

    def body(x_ref, g_ref, t_ref, l_ref, dx_ref, dg_ref):
        val, vjp = jax.vjp(f, x_ref[...], g_ref[...], t_ref[...])
        dx, dg, _ = vjp(jnp.ones((), f32))
        dx_ref[...] = dx

        @pl.when(pl.program_id(0) == 0)
        def _():
            l_ref[...] = jnp.zeros_like(l_ref)
            dg_ref[...] = jnp.zeros_like(dg_ref)

        l_ref[...] += val
        dg_ref[...] += dg

    rowd = pl.BlockSpec((tm, D), lambda i: (i, 0))
    return _pc(
        body, name=name, grid=(n // tm,),
        in_specs=[rowd, pl.BlockSpec((1, D), lambda i: (0, 0)), rowd],
        out_specs=[pl.BlockSpec((8, 128), lambda i: (0, 0)), rowd, pl.BlockSpec((1, D), lambda i: (0, 0))],
        out_shape=[_sds((8, 128)), _sds((n, D)), _sds((1, D))], sem=("arbitrary",),
    )(x2, g, tgt)


def _adamw(w, parts, m, v, name):
    nl, r, c = w.shape
    tr = r
    for cand in (1024, 512, 256, 128, 64, 32, 16, 8):
        if r % cand == 0 and cand * c * 4 <= 1024 * 1024:
            tr = cand
            break

    def body(*refs):
        w_ref, p_refs, (m_ref, v_ref, g_ref, d_ref, nm_ref, nv_ref) = refs[0], refs[1:1 + nl], refs[1 + nl:]

        def update(p_ref):
            gg = p_ref[0].astype(f32)
            for k in range(1, N_DEV):
                gg = gg + p_ref[k].astype(f32)
            g_ref[...] = gg
            nm = B1 * m_ref[...] + (1.0 - B1) * gg
            nv = B2 * v_ref[...] + (1.0 - B2) * (gg * gg)
            m_hat = nm / (1.0 - B1 ** STEP)
            v_hat = nv / (1.0 - B2 ** STEP)
            d_ref[...] = -LR * (m_hat / (jnp.sqrt(v_hat) + EPS) + WD * w_ref[...])
            nm_ref[...] = nm
            nv_ref[...] = nv

        for j in range(nl):
            pl.when(pl.program_id(0) == j)(functools.partial(update, p_refs[j]))

    spec = pl.BlockSpec((None, tr, c), lambda l, i: (l, i, 0))
    pspecs = [pl.BlockSpec((N_DEV, tr, c), lambda l, i, j=j: (0, jnp.where(l == j, i, 0), 0)) for j in range(nl)]
    return _pc(body, name=name, grid=(nl, r // tr), in_specs=[spec] + pspecs + [spec, spec], out_specs=[spec] * 4,
               out_shape=[_sds((nl, r, c))] * 4, sem=("arbitrary", "arbitrary"))(w, *parts, m, v)


def _peers():
    x, y, c = lax.axis_index("x"), lax.axis_index("y"), lax.axis_index("c")
    me = 4 * x + 2 * y + c
    peers = []
    for k in range(1, N_DEV):
        fx, fy, fc = (k >> 2) & 1, (k >> 1) & 1, k & 1
        peers.append(((1 - x) if fx else x, (1 - y) if fy else y, (1 - c) if fc else c))
    return me, peers


def _exchange(gathers, scatters, name):
    _, got = _pc(lambda: None, name=name, out_shape=[], ride=(gathers, scatters))()
    return got


SHARDED = {"w_in": 2, "gate_b": 2, "w_uq": 2, "w_ukv": 2, "rw_w0": 2, "rw_w2": 3, "rw_a0": 2, "rw_a2": 3, "rw_g2": 2,
           "w_branch": 3, "w_out": 1, "w_ffn_gate": 2, "w_ffn_up": 2, "w_ffn_down": 1}
GATHER_F32 = ("gate_b", "rw_w0", "rw_a0")
REPLICATED = ("attn_norm_g", "q_norm_g", "kv_norm_g", "sg_ln_g", "sg_ln_b", "sg_w", "sg_b", "rw_mu", "rw_k_k", "rw_k_a",
              "rw_r_k", "rw_ln_g", "rw_ln_b", "ffn_norm_g", "final_norm_g")
WEIGHTS = ("attn_norm_g", "w_in", "gate_b", "q_norm_g", "w_uq", "kv_norm_g", "w_ukv", "sg_ln_g", "sg_ln_b", "sg_w", "sg_b",
           "rw_mu", "rw_w0", "rw_w2", "rw_a0", "rw_a2", "rw_g2", "rw_k_k", "rw_k_a", "rw_r_k", "rw_ln_g", "rw_ln_b",
           "w_branch", "w_out", "ffn_norm_g", "w_ffn_gate", "w_ffn_up", "w_ffn_down", "final_norm_g")


REP_MAIN = tuple(k for k in REPLICATED if k not in ("attn_norm_g", "sg_w"))
BIG = ("w_in", "w_branch", "w_out", "w_ffn_gate", "w_ffn_up", "w_ffn_down")
SMALL_BF = ("w_uq", "w_ukv", "rw_w2", "rw_a2", "rw_g2")
SMALL = SMALL_BF + GATHER_F32


def _pack128(blocks, names, dtype, lead=0, to=256):
    parts = [blocks[k].astype(dtype).reshape(blocks[k].shape[:lead] + (-1, 128)) for k in names]
    rows = sum(p.shape[lead] for p in parts)
    pad = -rows % to
    if pad:
        parts.append(jnp.zeros(parts[0].shape[:lead] + (pad, 128), dtype))
    return jnp.concatenate(parts, axis=lead)


def _unpack128(packed, shapes, names, lead=0):
    out, off = {}, 0
    for k in names:
        rows = 1
        for d in shapes[k]:
            rows *= d
        rows //= 128
        idx = (slice(None),) * lead + (slice(off, off + rows),)
        out[k] = packed[idx].reshape(packed.shape[:lead] + tuple(shapes[k]))
        off += rows
    return out


def _join_blocks(g, ax):
    shp = g.shape[1:]
    return jnp.moveaxis(g, 0, ax).reshape(shp[:ax] + (N_DEV * shp[ax],) + shp[ax + 1:])


def _split_blocks(full, ax):
    shp = full.shape
    return jnp.moveaxis(full.reshape(shp[:ax] + (N_DEV, shp[ax] // N_DEV) + shp[ax + 1:]), ax, 0)


def _w_in_padded(w):
    z = lambda n: jnp.zeros((w.shape[0], n), w.dtype)
    q, ckv, kr = w[:, 0:384], w[:, 384:640], w[:, 640:672]
    sg, rw, gate = w[:, 672:1696], w[:, 1696:3616], w[:, 3616:6688]
    return jnp.concatenate([gate, sg, rw, z(128), ckv, z(64), kr, z(32), q, z(P_W - O_MLA - MLA_W)], axis=1)


def _w_in_unpadded(g):
    return jnp.concatenate([g[:, O_Q:O_Q + 384], g[:, O_CKV:O_CKV + 256], g[:, O_SLAB + 64:O_SLAB + 96],
                            g[:, O_SG:O_SG + 1024], g[:, O_RW:O_RW + 1920], g[:, O_GATE:O_GATE + 3072]], axis=1)


REST = ("w_branch", "w_out", "w_ffn_gate", "w_ffn_up", "w_ffn_down")


def _rest_weights(full, l):
    return dict(wb=full["w_branch"][l], wo=full["w_out"][l], wg=full["w_ffn_gate"][l], wu=full["w_ffn_up"][l],
                wd=full["w_ffn_down"][l])


def _layer_weights(full, rep, l):
    w = {}
    w["w_in"] = _w_in_padded(full["w_in"][l])
    if full["w_branch"][l] is not None:
        w.update(_rest_weights(full, l))
    uq = full["w_uq"][l].reshape(Q_LORA, HEADS, QK_NOPE + QK_ROPE)
    w["wq"] = jnp.pad(uq, ((0, 0), (0, 0), (0, 32))).reshape(Q_LORA, HEADS * 128).astype(f32)
    ukv = full["w_ukv"][l].reshape(KV_LORA, HEADS, QK_NOPE + V_HEAD)
    wk = jnp.pad(ukv[:, :, :QK_NOPE], ((0, 0), (0, 0), (0, 64))).reshape(KV_LORA, HEADS * 128)
    w["wk"], w["wv"] = wk.astype(f32), ukv[:, :, QK_NOPE:].reshape(KV_LORA, HEADS * V_HEAD).astype(f32)
    bdiag = lambda t: jnp.concatenate([jnp.concatenate([t[0], jnp.zeros_like(t[0])], axis=1),
                                       jnp.concatenate([jnp.zeros_like(t[1]), t[1]], axis=1)], axis=0).astype(f32)
    w["w2"], w["a2"] = bdiag(full["rw_w2"][l]), bdiag(full["rw_a2"][l])
    w["g2"] = full["rw_g2"][l].astype(f32)
    w["w0"], w["a0"] = full["rw_w0"][l].reshape(1, 2 * RW_DIM), full["rw_a0"][l].reshape(1, 2 * RW_DIM)
    w["gate_b"] = full["gate_b"][l].reshape(1, 3 * D)
    row = lambda a: a.reshape(1, -1)
    for k in ("attn_norm_g", "q_norm_g", "kv_norm_g", "sg_ln_g", "sg_ln_b", "rw_k_k", "rw_k_a", "rw_ln_g", "rw_ln_b",
              "ffn_norm_g"):
        w[k] = row(rep[k][l])
    w["r_k"] = row(rep["rw_r_k"][l])
    w["mu"] = jnp.pad(row(rep["rw_mu"][l]), ((0, 0), (0, RW_W - 1920)))
    w["sg_w"] = [rep["sg_w"][l, k] for k in range(SG_GROUPS)]
    w["sg_bias"] = jnp.repeat(rep["sg_b"][l].T, SG_DIM // SG_GROUPS, axis=1)
    return w


def _riding(res, ride, got, key):
    if ride is None:
        return res
    got[key] = res[1]
    return res[0]


def _layer_fwd(x2, w, tabs, bsz, seq, l, rides=None, on_inproj=None):
    nm = lambda s: f"l{l}_{s}"
    n = x2.shape[0]
    tm = min(256, n)
    rides = rides or {}
    ride = lambda key: (rides[key], []) if key in rides else None
    got = {}
    p, h = _riding(_inproj_fwd(x2, w["attn_norm_g"], w["w_in"], nm("inproj"), ride("inproj")), ride("inproj"), got, "inproj")
    if on_inproj is not None:
        w.update(on_inproj(got["inproj"]))
    mla_rows = [(p, 256, O_CKV // 256), (p, 128, O_SLAB // 128), (p, 384, O_Q // 384), (tabs[0], 128, 0), (tabs[1], 128, 0)]
    mla_w = [w["q_norm_g"], w["kv_norm_g"], w["wq"], w["wk"], w["wv"]]
    q, k, v = _rowwise_fwd(nm("mla_proj"), _f_mla_proj, mla_rows, mla_w, [(1024, bf16), (1024, bf16), (512, bf16)], tm)
    ya, lse = _riding(_attn_fwd(q, k, v, bsz, seq, nm("attn"), ride("attn")), ride("attn"), got, "attn")
    sg_rows = [(p, SG_DIM, O_SG // SG_DIM), (p, SG_DIM, O_SG // SG_DIM + 1)]
    sg_w = [w["sg_ln_g"], w["sg_ln_b"], w["sg_bias"]] + w["sg_w"]
    (yb,) = _rowwise_fwd(nm("sg"), _f_sg, sg_rows, sg_w, [(SG_DIM, f32)], SG_CHUNK)
    z = _shift_fwd(p, w["mu"], seq, nm("shift"))
    pre_rows = [(z, 512, 1), (z, 128, 12), (z, 128, 13), (z, 128, 14)]
    pre_w = [w["w0"], w["a0"], w["w2"], w["a2"], w["g2"], w["rw_k_k"], w["rw_k_a"]]
    lw, kd, kk, bd, g = _rowwise_fwd(nm("rw_pre"), _f_rw_pre, pre_rows, pre_w,
                                     [(1024, f32), (1024, f32), (512, f32), (1024, f32), (512, f32)], tm)
    y, s_in, t_in = _riding(_scan_fwd(z, lw, kd, kk, bd, bsz, seq, nm("scan"), ride("scan")), ride("scan"), got, "scan")
    post_rows = [(y[0], 512, 0), (y[1], 512, 0), (z, 512, 0), (z, 512, 2), (kd, 512, 0), (kd, 512, 1), (g, 512, 0)]
    post_w = [w["r_k"], w["rw_ln_g"], w["rw_ln_b"]]
    (yc,) = _rowwise_fwd(nm("rw_post"), _f_rw_post, post_rows, post_w, [(512, f32)], tm)
    x1 = _merge_fwd(x2, p, ya, yb, yc, w["gate_b"], w["wb"], w["wo"], nm("merge"))
    x3 = _ffn_fwd(x1, w["ffn_norm_g"], w["wg"], w["wu"], w["wd"], nm("ffn"))
    saved = dict(x=x2, p=p, h=h, q=q, k=k, v=v, ya=ya, lse=lse, yb=yb, z=z, lw=lw, kd=kd, kk=kk, bd=bd, g=g, y=y, s_in=s_in, t_in=t_in, yc=yc,
                 x1=x1, mla_rows=mla_rows, mla_w=mla_w, sg_rows=sg_rows, sg_w=sg_w, pre_rows=pre_rows, pre_w=pre_w,
                 post_rows=post_rows, post_w=post_w)
    return x3, saved, got


def _layer_bwd(dx3, w, sv, bsz, seq, l, rides=None):
    nm = lambda s: f"l{l}_{s}_bwd"
    n = dx3.shape[0]
    tm = min(256, n)
    g = {}
    rides = rides or {}
    ride = lambda key: rides[key](g) if key in rides else None
    got = {}
    dx1, g["ffn_norm_g"], h2, da, db, hm = _ffn_bwd(dx3, sv["x1"], w["ffn_norm_g"], w["wg"], w["wu"], w["wd"], nm("ffn"))
    g["wg"] = _matmul_tn(h2, da, nm("wg"))
    g["wu"] = _matmul_tn(h2, db, nm("wu"))
    g["wd"] = _matmul_tn(hm, dx3.astype(bf16), nm("wd"))
    dpg, dya, dyb, dyc, dt, mg, g["gate_b"] = _merge_bwd(dx1, sv["p"], sv["ya"], sv["yb"], sv["yc"], w["gate_b"], w["wb"],
                                                         w["wo"], nm("merge"))
    g["wo"] = _matmul_tn(mg, dx1.astype(bf16), nm("wo"))
    ys = (sv["ya"], sv["yb"], sv["yc"])
    g["wb"] = jnp.stack([_matmul_tn(ys[k].astype(bf16), dt[:, k * D:(k + 1) * D], nm(f"wb{k}")) for k in range(3)])
    (dy, dr_p, dv_p, dkd0, dkd1, dg_), (g["r_k"], g["rw_ln_g"], g["rw_ln_b"]) = _rowwise_bwd(
        nm("rw_post"), _f_rw_post, sv["post_rows"], sv["post_w"], [(dyc, 512, 0)], tm, [f32, None] + [f32] * 5)
    dkd_p = jnp.concatenate([dkd0, dkd1], axis=1)
    rd = ride("scan")
    dr_s, dv_s, dkk_s, dlw, dkd_s, dbd = _riding(
        _scan_bwd(sv["z"], sv["lw"], sv["kd"], sv["kk"], sv["bd"], sv["s_in"], sv["t_in"], dy, bsz, seq, nm("scan"), rd),
        rd, got, "scan")
    pre_cots = [(dlw, 1024, 0), (dkd_s + dkd_p, 1024, 0), (dkk_s[0] + dkk_s[1], 512, 0), (dbd, 1024, 0), (dg_, 512, 0)]
    (dk, dwl, dal, dgl), (g["w0"], g["a0"], g["w2"], g["a2"], g["g2"], g["rw_k_k"], g["rw_k_a"]) = _rowwise_bwd(
        nm("rw_pre"), _f_rw_pre, sv["pre_rows"], sv["pre_w"], pre_cots, tm, [f32] * 4)
    dz = jnp.concatenate([dr_s[0] + dr_s[1] + dr_p, dk, dv_s[0] + dv_s[1] + dv_p, dwl, dal, dgl,
                          jnp.zeros((n, RW_W - 1920), f32)], axis=1)
    dp_rw, g["mu"] = _shift_bwd(dz, sv["p"], w["mu"], seq, nm("shift"))
    (dp_su, dp_sv), (g["sg_ln_g"], g["sg_ln_b"], g["sg_bias"], *sgw) = _rowwise_bwd(
        nm("sg"), _f_sg, sv["sg_rows"], sv["sg_w"], [(dyb, SG_DIM, 0)], SG_CHUNK, [bf16, bf16])
    g["sg_w"] = jnp.stack(sgw)
    rd = ride("attn")
    dq, dk_, dv_ = _riding(_attn_bwd(sv["q"], sv["k"], sv["v"], sv["ya"], sv["lse"], dya, bsz, seq, nm("attn"), rd), rd, got,
                           "attn")
    (dp_ckv, dp_slab, dp_q), (g["q_norm_g"], g["kv_norm_g"], g["wq"], g["wk"], g["wv"]) = _rowwise_bwd(
        nm("mla_proj"), _f_mla_proj, sv["mla_rows"], sv["mla_w"], [(dq, 1024, 0), (dk_, 1024, 0), (dv_, 512, 0)], tm,
        [bf16, bf16, bf16, None, None])
    dp = jnp.concatenate([dpg, dp_su, dp_sv, dp_rw, dp_ckv, dp_slab, dp_q, jnp.zeros((n, P_W - O_MLA - MLA_W), bf16)],
                         axis=1)
    rd = ride("w_in")
    g["w_in"] = _riding(_matmul_tn(sv["h"], dp, nm("w_in"), rd), rd, got, "w_in")
    rd = ride("inproj")
    dx, g["attn_norm_g"] = _riding(_norm_matmul_bwd(dp, w["w_in"], sv["x"], w["attn_norm_g"], dx1, nm("inproj"), rd), rd, got,
                                   "inproj")
    return dx, g, got


def _layer_grads_to_full(g):
    o = {}
    if "w_in" in g:
        o["w_in"] = _w_in_unpadded(g["w_in"])
    o["w_uq"] = g["wq"].reshape(Q_LORA, HEADS, 128)[:, :, :QK_NOPE + QK_ROPE].reshape(Q_LORA, -1)
    gk = g["wk"].reshape(KV_LORA, HEADS, 128)[:, :, :QK_NOPE]
    gv = g["wv"].reshape(KV_LORA, HEADS, V_HEAD)
    o["w_ukv"] = jnp.concatenate([gk, gv], axis=2).reshape(KV_LORA, -1)
    unb = lambda t: jnp.stack([t[:LORA, :RW_DIM], t[LORA:, RW_DIM:]])
    o["rw_w2"], o["rw_a2"], o["rw_g2"] = unb(g["w2"]), unb(g["a2"]), g["g2"]
    o["rw_w0"], o["rw_a0"] = g["w0"].reshape(2, RW_DIM), g["a0"].reshape(2, RW_DIM)
    o["gate_b"] = g["gate_b"].reshape(3, D)
    o["w_branch"], o["w_out"] = g["wb"], g["wo"]
    o["w_ffn_gate"], o["w_ffn_up"], o["w_ffn_down"] = g["wg"], g["wu"], g["wd"]
    for k in ("attn_norm_g", "q_norm_g", "kv_norm_g", "sg_ln_g", "sg_ln_b", "rw_k_k", "rw_k_a", "rw_ln_g", "rw_ln_b",
              "ffn_norm_g"):
        if k in g:
            o[k] = g[k].reshape(-1)
    o["rw_r_k"] = g["r_k"].reshape(HEADS, RW_HEAD)
    o["rw_mu"] = g["mu"].reshape(-1)[:1920]
    o["sg_w"] = g["sg_w"]
    o["sg_b"] = g["sg_bias"].reshape(SG_CHUNK, SG_GROUPS, SG_DIM // SG_GROUPS).sum(axis=2).T
    return o


def _rope_tables(positions):
    inv = 1.0 / (10000.0 ** (jnp.arange(0, QK_ROPE, 2, dtype=f32) / QK_ROPE))
    ang = positions.astype(f32)[:, None] * inv
    cos, sin = jnp.cos(ang), jnp.sin(ang)
    n = positions.shape[0]
    c = jnp.concatenate([jnp.ones((n, 64), f32), cos, cos, jnp.zeros((n, 32), f32)], axis=1)
    s = jnp.concatenate([jnp.zeros((n, 64), f32), -sin, sin, jnp.zeros((n, 32), f32)], axis=1)
    return c, s


def _grad_parts(grad, name):
    return _split_blocks(grad, SHARDED[name] - 1).astype(bf16)


def _local_step(x, positions, full, rep, loss_target, blocks=None):
    bsz, seq, _ = x.shape
    n = bsz * seq
    x2 = x.reshape(n, D)
    tabs = _rope_tables(positions.reshape(n))
    join = lambda k, g: _join_blocks(g, SHARDED[k] - 1)
    rides, on_inproj = None, None
    if blocks is not None:
        rides = {"inproj": [blocks[k][0] for k in REST], "attn": [blocks["w_in"][1]], "scan": [blocks[k][1] for k in REST]}

        def on_inproj(got):
            for k, g in zip(REST, got):
                full[k][0] = join(k, g)
            return _rest_weights(full, 0)

    w0 = _layer_weights(full, rep, 0)
    x2, sv0, got = _layer_fwd(x2, w0, tabs, bsz, seq, 0, rides, on_inproj)
    if blocks is not None:
        full["w_in"][1] = join("w_in", got["attn"][0])
        for k, g in zip(REST, got["scan"]):
            full[k][1] = join(k, g)
    w1 = _layer_weights(full, rep, 1)
    x2, sv1, _ = _layer_fwd(x2, w1, tabs, bsz, seq, 1)
    loss, dx, dgf = _loss_head(x2, rep["final_norm_g"].reshape(1, D), loss_target.reshape(n, D), "loss_head")
    dx, g1, _ = _layer_bwd(dx, w1, sv1, bsz, seq, 1)
    grads1 = _layer_grads_to_full(g1)
    rides = None
    if blocks is not None:
        short = dict(w_branch="wb", w_out="wo", w_ffn_gate="wg", w_ffn_up="wu", w_ffn_down="wd")

        def beside_w_in(g):
            g0 = _layer_grads_to_full(g)
            both = {k: jnp.stack([g0[k], grads1[k]]) for k in g0}
            both["final_norm_g"] = dgf.reshape(D)
            split = {k: _split_blocks(both[k], SHARDED[k]) for k in SMALL}
            return [_pack128(both, REP_MAIN, f32), both["sg_w"]], [_pack128(split, SMALL, f32, lead=1)]

        rides = {"scan": lambda g: ([], [_grad_parts(grads1[k], k) for k in BIG]),
                 "attn": lambda g: ([], [_grad_parts(g[short[k]], k) for k in REST]),
                 "w_in": beside_w_in,
                 "inproj": lambda g: ([], [_grad_parts(_w_in_unpadded(g["w_in"]), "w_in")])}
    dx, g0, got = _layer_bwd(dx, w0, sv0, bsz, seq, 0, rides)
    grads0 = _layer_grads_to_full(g0)
    grads = {k: [grads0[k], grads1[k]] for k in grads0}
    grads["final_norm_g"] = dgf.reshape(D)
    parts = {}
    if blocks is not None:
        parts = {k: [None, p] for k, p in zip(BIG, got["scan"])}
        for k, p in zip(REST, got["attn"]):
            parts[k][0] = p
        parts["replicated"], parts["sg_w"], parts["small"] = got["w_in"]
        (parts["w_in"][0],) = got["inproj"]
    return loss[0, 0], dx.reshape(bsz, seq, D), grads, parts


def kernel(x, positions, attn_norm_g, w_in, gate_b, q_norm_g, w_uq, kv_norm_g, w_ukv, sg_ln_g, sg_ln_b, sg_w, sg_b, rw_mu, rw_w0, rw_w2, rw_a0, rw_a2, rw_g2, rw_k_k, rw_k_a, rw_r_k, rw_ln_g, rw_ln_b, w_branch, w_out, ffn_norm_g, w_ffn_gate, w_ffn_up, w_ffn_down, final_norm_g, loss_target, m_attn_norm_g, m_w_in, m_gate_b, m_q_norm_g, m_w_uq, m_kv_norm_g, m_w_ukv, m_sg_ln_g, m_sg_ln_b, m_sg_w, m_sg_b, m_rw_mu, m_rw_w0, m_rw_w2, m_rw_a0, m_rw_a2, m_rw_g2, m_rw_k_k, m_rw_k_a, m_rw_r_k, m_rw_ln_g, m_rw_ln_b, m_w_branch, m_w_out, m_ffn_norm_g, m_w_ffn_gate, m_w_ffn_up, m_w_ffn_down, m_final_norm_g, v_attn_norm_g, v_w_in, v_gate_b, v_q_norm_g, v_w_uq, v_kv_norm_g, v_w_ukv, v_sg_ln_g, v_sg_ln_b, v_sg_w, v_sg_b, v_rw_mu, v_rw_w0, v_rw_w2, v_rw_a0, v_rw_a2, v_rw_g2, v_rw_k_k, v_rw_k_a, v_rw_r_k, v_rw_ln_g, v_rw_ln_b, v_w_branch, v_w_out, v_ffn_norm_g, v_w_ffn_gate, v_w_ffn_up, v_w_ffn_down, v_final_norm_g):
    args = locals()
    wts = {k: args[k] for k in WEIGHTS}
    mom_m = {k: args["m_" + k] for k in WEIGHTS}
    mom_v = {k: args["v_" + k] for k in WEIGHTS}
    shapes = {k: wts[k].shape for k in WEIGHTS}
    blocks = {k: wts[k].astype(bf16) for k in BIG}
    got = _exchange([blocks["w_in"][0], _pack128(wts, SMALL_BF, bf16), _pack128(wts, GATHER_F32, f32)], [], "gather_first")
    small = {**_unpack128(got[1], shapes, SMALL_BF, lead=1), **_unpack128(got[2], shapes, GATHER_F32, lead=1)}
    full = {k: list(_join_blocks(small[k], SHARDED[k])) for k in SMALL}
    full["w_in"] = [_join_blocks(got[0], SHARDED["w_in"] - 1), None]
    full.update({k: [None, None] for k in REST})
    rep = {k: wts[k] for k in REPLICATED}
    loss, grad_x, grads, parts = _local_step(x, positions, full, rep, loss_target, blocks)
    loss = lax.psum(loss, ("x", "y", "c"))
    last = ("attn_norm_g",)
    (last_parts,) = _exchange([_pack128({"attn_norm_g": jnp.stack(grads["attn_norm_g"])}, last, f32, to=16)], [],
                              "exchange_last")
    gw, delta, new_m, new_v = {}, {}, {}, {}
    for k in BIG:
        three = lambda a, k=k: a.reshape(a.shape[0], -1, shapes[k][-1])
        res = _adamw(three(wts[k]), [three(p) for p in parts[k]], three(mom_m[k]), three(mom_v[k]), f"adamw_{k}")
        gw[k], delta[k], new_m[k], new_v[k] = (t.reshape(shapes[k]) for t in res)
    rows = lambda a: a.reshape(1, -1, 128)
    res = _adamw(rows(wts["sg_w"]), [parts["sg_w"].reshape(N_DEV, -1, 128)], rows(mom_m["sg_w"]), rows(mom_v["sg_w"]),
                 "adamw_sg_w")
    gw["sg_w"], delta["sg_w"], new_m["sg_w"], new_v["sg_w"] = (t.reshape(shapes["sg_w"]) for t in res)
    for names, got, to in ((SMALL, parts["small"], 256), (REP_MAIN, parts["replicated"], 256), (last, last_parts, 16)):
        pk = lambda dct: _pack128(dct, names, f32, to=to)[None]
        res = _adamw(pk(wts), [got], pk(mom_m), pk(mom_v), f"adamw_{names[0]}")
        for dst, t in zip((gw, delta, new_m, new_v), res):
            dst.update(_unpack128(t[0], shapes, names))
    return (loss, grad_x, *[gw[k] for k in WEIGHTS], *[delta[k] for k in WEIGHTS], *[new_m[k] for k in WEIGHTS],
            *[new_v[k] for k in WEIGHTS])
```

```python
import functools

import jax
import jax.numpy as jnp
from jax import lax
from jax.experimental import pallas as pl
from jax.experimental.pallas import tpu as pltpu

f32 = jnp.float32
bf16 = jnp.bfloat16
HI = lax.Precision.HIGHEST
NN, NT, TN = ((1,), (0,)), ((1,), (1,)), ((0,), (0,))

N_DEV = 8
D = 1024
HEADS = 8
Q_LORA, KV_LORA, QK_NOPE, QK_ROPE, V_HEAD = 384, 256, 64, 32, 64
SG_DIM, SG_CHUNK, SG_GROUPS = 512, 128, 8
RW_DIM, RW_HEAD, LORA = 512, 64, 64
D_FF = 2816
N_IN = 6688
NORM_EPS, LN_EPS, GN_EPS = 1e-6, 1e-5, 64e-5
ATT_SCALE = (QK_NOPE + QK_ROPE) ** -0.5
P_W = 7168
O_GATE, O_SG, O_RW, O_MLA = 0, 3072, 4096, 6144
RW_W = 2048
MLA_W = 768
O_CKV, O_SLAB, O_Q = O_MLA, O_MLA + 256, O_MLA + 384
CHUNK = 128
VMEM_LIMIT = 56 * 1024 * 1024

B1, B2, LR, EPS, WD, STEP = 0.9, 0.999, 0.001, 1e-8, 0.01, 10


def _pc(body, *, name, out_shape, grid=(), in_specs=(), out_specs=(), scratch=(), sem=None, ride=None):
    params = pltpu.CompilerParams(dimension_semantics=sem, vmem_limit_bytes=VMEM_LIMIT)
    if ride is None:
        return pl.pallas_call(body, out_shape=out_shape, grid=grid, in_specs=in_specs, out_specs=out_specs,
                              scratch_shapes=scratch, compiler_params=params, name=name, interpret=False)
    gathers, scatters = ride
    moved = list(gathers) + list(scatters)
    ng, nx = len(gathers), len(moved)
    single = not isinstance(out_shape, (list, tuple))
    outs = [out_shape] if single else list(out_shape)
    ospecs = [out_specs] if single else list(out_specs)
    n_in, n_out, n_scr = len(in_specs), len(outs), len(scratch)
    per = N_DEV - 1

    def riding(*refs):
        ins, xin = refs[:n_in], refs[n_in:n_in + nx]
        outs_r, xout = refs[n_in + nx:n_in + nx + n_out], refs[n_in + nx + n_out:n_in + 2 * nx + n_out]
        own = refs[n_in + 2 * nx + n_out:n_in + 2 * nx + n_out + n_scr]
        send_sems, recv_sems, local_sems = refs[n_in + 2 * nx + n_out + n_scr:]

        def copies():
            me, peers = _peers()
            cps = []
            for a in range(nx):
                whole = a < ng
                cps.append(pltpu.make_async_copy(xin[a] if whole else xin[a].at[me], xout[a].at[me], local_sems.at[a]))
                for k, peer in enumerate(peers):
                    dev = 4 * peer[0] + 2 * peer[1] + peer[2]
                    cps.append(pltpu.make_async_remote_copy(
                        src_ref=xin[a] if whole else xin[a].at[dev], dst_ref=xout[a].at[me],
                        send_sem=send_sems.at[a * per + k], recv_sem=recv_sems.at[a * per + k], device_id=peer,
                        device_id_type=pl.DeviceIdType.MESH))
            return cps

        if not grid:
            for cp in copies():
                cp.start()
            body(*ins, *outs_r, *own)
            for cp in copies():
                cp.wait()
            return
        ids = [pl.program_id(a) for a in range(len(grid))]
        first = functools.reduce(jnp.logical_and, [i == 0 for i in ids])
        last = functools.reduce(jnp.logical_and, [i == g - 1 for i, g in zip(ids, grid)])

        @pl.when(first)
        def _():
            for cp in copies():
                cp.start()

        body(*ins, *outs_r, *own)

        @pl.when(last)
        def _():
            for cp in copies():
                cp.wait()

    anyspec = pl.BlockSpec(memory_space=pl.ANY)
    call = pl.pallas_call(
        riding, grid=grid, in_specs=list(in_specs) + [anyspec] * nx, out_specs=ospecs + [anyspec] * nx,
        out_shape=outs + [_sds((N_DEV,) + a.shape, a.dtype) for a in gathers] + [_sds(a.shape, a.dtype) for a in scatters],
        scratch_shapes=list(scratch) + [pltpu.SemaphoreType.DMA((nx * per,)), pltpu.SemaphoreType.DMA((nx * per,)),
                                        pltpu.SemaphoreType.DMA((nx,))],
        compiler_params=params, name=name, interpret=False)

    def run(*args):
        res = call(*args, *moved)
        own = res[0] if single else list(res[:n_out])
        return own, list(res[n_out:])

    return run


def _sds(shape, dtype=f32):
    return jax.ShapeDtypeStruct(tuple(shape), dtype)


def _dot(a, b, dims, precision=None):
    return lax.dot_general(a, b, (dims, ((), ())), preferred_element_type=f32, precision=precision)


def _bdot(a, b, dims=NN):
    return _dot(a.astype(bf16), b.astype(bf16), dims)


@jax.custom_vjp
def _mm(a, w):
    return _bdot(a, w, NN)


def _mm_fwd(a, w):
    return _bdot(a, w, NN), (a, w)


def _mm_bwd(res, g):
    a, w = res
    return _bdot(g, w, NT), _bdot(a, g, TN)


_mm.defvjp(_mm_fwd, _mm_bwd)


@jax.custom_vjp
def _mm_nt(a, b):
    return _bdot(a, b, NT)


def _mm_nt_fwd(a, b):
    return _bdot(a, b, NT), (a, b)


def _mm_nt_bwd(res, g):
    a, b = res
    return _bdot(g, b, NN), _bdot(g, a, TN)


_mm_nt.defvjp(_mm_nt_fwd, _mm_nt_bwd)


@jax.custom_vjp
def _mm_tn(a, b):
    return _bdot(a, b, TN)


def _mm_tn_fwd(a, b):
    return _bdot(a, b, TN), (a, b)


def _mm_tn_bwd(res, g):
    a, b = res
    return _bdot(b, g, NT), _bdot(a, g, NN)


_mm_tn.defvjp(_mm_tn_fwd, _mm_tn_bwd)


def _rms(x, g):
    return x * lax.rsqrt(jnp.mean(x * x, axis=-1, keepdims=True) + NORM_EPS) * g


def _sigmoid(x):
    return 1.0 / (1.0 + jnp.exp(-x))


def _gelu(x):
    return 0.5 * x * (1.0 + jnp.tanh(0.7978845608028654 * (x + 0.044715 * x * x * x)))


def _softplus(x):
    return jnp.maximum(x, 0.0) + jnp.log(1.0 + jnp.exp(-jnp.abs(x)))


@jax.custom_vjp
def _group_sum(x):
    w = x.shape[-1]
    r = lax.broadcasted_iota(jnp.int32, (w, w), 0) // RW_HEAD
    c = lax.broadcasted_iota(jnp.int32, (w, w), 1) // RW_HEAD
    ones = (r == c).astype(bf16)
    hi = x.astype(bf16)
    lo = (x - hi.astype(f32)).astype(bf16)
    return _dot(jnp.concatenate([hi, lo], axis=1), jnp.concatenate([ones, ones], axis=0), NN)


_group_sum.defvjp(lambda x: (_group_sum(x), None), lambda _, g: (_group_sum(g),))


@jax.custom_vjp
def _swap(x):
    w = x.shape[-1]
    lane = lax.broadcasted_iota(jnp.int32, x.shape, 1) % 128
    lo = (lane >= 64) & (lane < 80)
    hi = (lane >= 80) & (lane < 96)
    return jnp.where(lo, pltpu.roll(x, w - 16, 1), jnp.where(hi, pltpu.roll(x, 16, 1), 0.0))


_swap.defvjp(lambda x: (_swap(x), None), lambda _, g: (_swap(g),))


def _rope(x, c, s):
    return x * c + _swap(x) * s


def _row_spec(tm, width, blk):
    return pl.BlockSpec((tm, width), lambda i, blk=blk: (i, blk))


def _full_spec(a):
    nd = a.ndim
    return pl.BlockSpec(a.shape, lambda i, nd=nd: (0,) * nd)


def _rowwise_fwd(name, f, rows, weights, outs, tm):
    n = rows[0][0].shape[0]
    nr, nw = len(rows), len(weights)

    def body(*refs):
        vals = [r[...].astype(f32) for r in refs[:nr + nw]]
        res = f(*vals)
        for o_ref, o in zip(refs[nr + nw:], res):
            o_ref[...] = o.astype(o_ref.dtype)

    return _pc(
        body, name=name, grid=(n // tm,),
        in_specs=[_row_spec(tm, w, b) for _, w, b in rows] + [_full_spec(w) for w in weights],
        out_specs=[_row_spec(tm, w, 0) for w, _ in outs],
        out_shape=[_sds((n, w), dt) for w, dt in outs], sem=("parallel",),
    )(*[a for a, _, _ in rows], *weights)


def _rowwise_bwd(name, f, rows, weights, cots, tm, drows):
    n = rows[0][0].shape[0]
    nr, nw, nc = len(rows), len(weights), len(cots)
    want = [k for k, dt in enumerate(drows) if dt is not None]

    def body(*refs):
        vals = [r[...].astype(f32) for r in refs[:nr + nw]]
        cot = tuple(r[...].astype(f32) for r in refs[nr + nw:nr + nw + nc])
        _, vjp = jax.vjp(f, *vals)
        grads = vjp(cot)
        outs = refs[nr + nw + nc:]
        for o_ref, k in zip(outs[:len(want)], want):
            o_ref[...] = grads[k].astype(o_ref.dtype)
        first = pl.program_id(0) == 0
        for o_ref, g in zip(outs[len(want):], grads[nr:]):
            @pl.when(first)
            def _(o_ref=o_ref, g=g):
                o_ref[...] = g

            @pl.when(jnp.logical_not(first))
            def _(o_ref=o_ref, g=g):
                o_ref[...] += g

    res = _pc(
        body, name=name, grid=(n // tm,),
        in_specs=[_row_spec(tm, w, b) for _, w, b in rows] + [_full_spec(w) for w in weights]
        + [_row_spec(tm, w, b) for _, w, b in cots],
        out_specs=[_row_spec(tm, rows[k][1], 0) for k in want] + [_full_spec(w) for w in weights],
        out_shape=[_sds((n, rows[k][1]), drows[k]) for k in want] + [_sds(w.shape) for w in weights],
        sem=("arbitrary",),
    )(*[a for a, _, _ in rows], *weights, *[a for a, _, _ in cots])
    return res[:len(want)], res[len(want):]


def _inproj_fwd(x2, g, w, name, ride=None):
    n = x2.shape[0]
    tm, tn = min(1024, n), 1024

    def body(x_ref, g_ref, w_ref, p_ref, h_ref):
        @pl.when(pl.program_id(1) == 0)
        def _():
            h_ref[...] = _rms(x_ref[...], g_ref[...]).astype(bf16)

        p_ref[...] = jnp.dot(h_ref[...], w_ref[...], preferred_element_type=f32)

    return _pc(
        body, name=name, grid=(n // tm, P_W // tn),
        in_specs=[pl.BlockSpec((tm, D), lambda i, j: (i, 0)), pl.BlockSpec((1, D), lambda i, j: (0, 0)),
                  pl.BlockSpec((D, tn), lambda i, j: (0, j))],
        out_specs=[pl.BlockSpec((tm, tn), lambda i, j: (i, j)), pl.BlockSpec((tm, D), lambda i, j: (i, 0))],
        out_shape=[_sds((n, P_W)), _sds((n, D), bf16)], sem=("parallel", "arbitrary"), ride=ride,
    )(x2, g, w)


def _norm_matmul_bwd(dy, w, x2, g, dres, name, ride=None):
    n, k = dy.shape
    tm = min(1024, n)
    tk = 1024 if k % 1024 == 0 else 1408
    nk = k // tk

    def body(dy_ref, w_ref, x_ref, g_ref, dr_ref, dx_ref, dg_ref, acc):
        i, j = pl.program_id(0), pl.program_id(1)

        @pl.when(j == 0)
        def _():
            acc[...] = jnp.zeros_like(acc)

        @pl.when((i == 0) & (j == 0))
        def _():
            dg_ref[...] = jnp.zeros_like(dg_ref)

        acc[...] += _dot(dy_ref[...], w_ref[...], NT)

        @pl.when(j == nk - 1)
        def _():
            _, vjp = jax.vjp(_rms, x_ref[...], g_ref[...])
            dx, dg = vjp(acc[...])
            dx_ref[...] = dr_ref[...] + dx
            dg_ref[...] += dg

    return _pc(
        body, name=name, grid=(n // tm, nk),
        in_specs=[pl.BlockSpec((tm, tk), lambda i, j: (i, j)), pl.BlockSpec((D, tk), lambda i, j: (0, j)),
                  pl.BlockSpec((tm, D), lambda i, j: (i, 0)), pl.BlockSpec((1, D), lambda i, j: (0, 0)),
                  pl.BlockSpec((tm, D), lambda i, j: (i, 0))],
        out_specs=[pl.BlockSpec((tm, D), lambda i, j: (i, 0)), pl.BlockSpec((1, D), lambda i, j: (0, 0))],
        out_shape=[_sds((n, D)), _sds((1, D))], scratch=[pltpu.VMEM((tm, D), f32)], sem=("arbitrary", "arbitrary"),
        ride=ride,
    )(dy, w, x2, g, dres)


def _matmul_tn(a, g, name, ride=None):
    n, k = a.shape
    m = g.shape[1]
    tr = min(1024, n)
    tk = k if k <= 1024 else 1408
    tn = m if m <= 1024 else (1024 if m % 1024 == 0 else 1408)
    nr = n // tr

    def body(a_ref, g_ref, o_ref):
        @pl.when(pl.program_id(2) == 0)
        def _():
            o_ref[...] = jnp.zeros_like(o_ref)

        o_ref[...] += _bdot(a_ref[...], g_ref[...], TN)

    return _pc(
        body, name=name, grid=(k // tk, m // tn, nr),
        in_specs=[pl.BlockSpec((tr, tk), lambda i, j, r: (r, i)), pl.BlockSpec((tr, tn), lambda i, j, r: (r, j))],
        out_specs=pl.BlockSpec((tk, tn), lambda i, j, r: (i, j)),
        out_shape=_sds((k, m)), sem=("parallel", "parallel", "arbitrary"), ride=ride,
    )(a, g)


def _f_mla_proj(ckv, slab, pq, c, s, qg, kg, wq, wk, wv):
    c8, s8 = jnp.concatenate([c] * HEADS, axis=1), jnp.concatenate([s] * HEADS, axis=1)
    q = _rope(_mm(_rms(pq, qg), wq), c8, s8)
    cn = _rms(ckv, kg)
    k = _mm(cn, wk) + jnp.concatenate([_rope(slab, c, s)] * HEADS, axis=1)
    return q, k, _mm(cn, wv)


def _attn_fwd(q, k, v, bsz, seq, name, ride=None):
    n = q.shape[0]
    tq = min(256, seq)
    nq = seq // tq

    def body(q_ref, k_ref, v_ref, o_ref, lse_ref):
        lane = lax.broadcasted_iota(jnp.int32, (tq, 128), 1) < 64
        vv = v_ref[...]
        two = range(2)
        s = [_dot(q_ref[:, h * 128:(h + 1) * 128], k_ref[:, h * 128:(h + 1) * 128], NT) * ATT_SCALE for h in two]
        m = [jnp.max(s[h], axis=-1, keepdims=True) for h in two]
        e = [jnp.exp(s[h] - m[h]) for h in two]
        l = [jnp.sum(e[h], axis=-1, keepdims=True) for h in two]
        p = [(e[h] / l[h]).astype(bf16) for h in two]
        outs = [_dot(p[h], vv, NN) for h in two]
        o_ref[...] = jnp.where(lane, outs[0], outs[1])
        lse_ref[...] = jnp.where(lane, m[0] + jnp.log(l[0]), m[1] + jnp.log(l[1]))

    row = pl.BlockSpec((tq, 128), lambda b, h, i: (b * nq + i, h))
    return _pc(
        body, name=name, grid=(bsz, HEADS // 2, nq),
        in_specs=[pl.BlockSpec((tq, 256), lambda b, h, i: (b * nq + i, h)),
                  pl.BlockSpec((seq, 256), lambda b, h, i: (b, h)),
                  pl.BlockSpec((seq, 128), lambda b, h, i: (b, h))],
        out_specs=[row, row],
        out_shape=[_sds((n, HEADS * V_HEAD)), _sds((n, HEADS * V_HEAD))], sem=("parallel", "parallel", "parallel"),
        ride=ride,
    )(q, k, v)


def _attn_bwd(q, k, v, o, lse, do, bsz, seq, name, ride=None):
    n = q.shape[0]
    tq = min(256, seq)
    nq = seq // tq

    def body(q_ref, k_ref, v_ref, o_ref, lse_ref, do_ref, dq_ref, dk_ref, dv_ref):
        @pl.when(pl.program_id(2) == 0)
        def _():
            dk_ref[...] = jnp.zeros_like(dk_ref)
            dv_ref[...] = jnp.zeros_like(dv_ref)

        lane = lax.broadcasted_iota(jnp.int32, (tq, 128), 1) < 64
        vv = v_ref[...]
        for h in range(2):
            qh, kh = q_ref[:, h * 128:(h + 1) * 128], k_ref[:, h * 128:(h + 1) * 128]
            p = jnp.exp(_dot(qh, kh, NT) * ATT_SCALE - lse_ref[:, 64 * h:64 * h + 1])
            doh = jnp.where(lane if h == 0 else jnp.logical_not(lane), do_ref[...], 0.0)
            delta = jnp.sum(doh * o_ref[...], axis=-1, keepdims=True)
            dob = doh.astype(bf16)
            dp = _dot(dob, vv, NT)
            ds = (p * (dp - delta) * ATT_SCALE).astype(bf16)
            dq_ref[:, h * 128:(h + 1) * 128] = _dot(ds, kh, NN)
            dk_ref[:, h * 128:(h + 1) * 128] += _dot(ds, qh, TN)
            dv_ref[...] += _dot(p.astype(bf16), dob, TN)

    return _pc(
        body, name=name, grid=(bsz, HEADS // 2, nq),
        in_specs=[pl.BlockSpec((tq, 256), lambda b, h, i: (b * nq + i, h)),
                  pl.BlockSpec((seq, 256), lambda b, h, i: (b, h)),
                  pl.BlockSpec((seq, 128), lambda b, h, i: (b, h)),
                  pl.BlockSpec((tq, 128), lambda b, h, i: (b * nq + i, h)),
                  pl.BlockSpec((tq, 128), lambda b, h, i: (b * nq + i, h)),
                  pl.BlockSpec((tq, 128), lambda b, h, i: (b * nq + i, h))],
        out_specs=[pl.BlockSpec((tq, 256), lambda b, h, i: (b * nq + i, h)),
                   pl.BlockSpec((seq, 256), lambda b, h, i: (b, h)),
                   pl.BlockSpec((seq, 128), lambda b, h, i: (b, h))],
        out_shape=[_sds((n, HEADS * 128)), _sds((n, HEADS * 128)), _sds((n, HEADS * V_HEAD))],
        sem=("parallel", "parallel", "arbitrary"), ride=ride,
    )(q, k, v, o, lse, do)


def _f_sg(pu, pv, lg, lb, bias, *ws):
    u, vv = _gelu(pu), _gelu(pv)
    mu = jnp.mean(vv, axis=-1, keepdims=True)
    d = vv - mu
    vv = d * lax.rsqrt(jnp.mean(d * d, axis=-1, keepdims=True) + LN_EPS) * lg + lb
    group = lax.broadcasted_iota(jnp.int32, (SG_CHUNK, SG_DIM), 1) // (SG_DIM // SG_GROUPS)
    mixed = bias
    for k, w in enumerate(ws):
        mixed = mixed + jnp.where(group == k, _mm(w, vv), 0.0)
    return (u * mixed,)


def _shift_mean(a, prev_row, next_row):
    t = a.shape[0]
    row = lax.broadcasted_iota(jnp.int32, a.shape, 0)
    prev = jnp.where(row == 0, prev_row, pltpu.roll(a, 1, 0))
    nxt = jnp.where(row == t - 1, next_row, pltpu.roll(a, t - 1, 0))
    return 0.5 * (prev + nxt)


def _halo_specs(tm, width, blk, nblk8):
    h = tm // 8
    return [pl.BlockSpec((tm, width), lambda i: (i, blk)),
            pl.BlockSpec((8, width), lambda i: (jnp.maximum(i * h - 1, 0), blk)),
            pl.BlockSpec((8, width), lambda i: (jnp.minimum((i + 1) * h, nblk8 - 1), blk))]


def _edge_rows(i, tm, seq, pv_ref, nx_ref, scale=None):
    first = (i * tm) % seq == 0
    last = ((i + 1) * tm) % seq == 0
    pv, nx = pv_ref[7:8, :], nx_ref[0:1, :]
    if scale is not None:
        pv, nx = pv * scale, nx * scale
    return jnp.where(first, 0.0, pv), jnp.where(last, 0.0, nx)


def _shift_fwd(p, mu, seq, name):
    n = p.shape[0]
    tm = min(256, seq)
    blk = O_RW // RW_W

    def body(x_ref, pv_ref, nx_ref, mu_ref, z_ref):
        x = x_ref[...]
        pv, nx = _edge_rows(pl.program_id(0), tm, seq, pv_ref, nx_ref)
        z_ref[...] = x + mu_ref[...] * (_shift_mean(x, pv, nx) - x)

    return _pc(
        body, name=name, grid=(n // tm,),
        in_specs=_halo_specs(tm, RW_W, blk, n // 8) + [pl.BlockSpec((1, RW_W), lambda i: (0, 0))],
        out_specs=pl.BlockSpec((tm, RW_W), lambda i: (i, 0)), out_shape=_sds((n, RW_W)), sem=("parallel",),
    )(p, p, p, mu)


def _shift_bwd(dz, p, mu, seq, name):
    n = p.shape[0]
    tm = min(256, seq)
    blk = O_RW // RW_W

    def body(dz_ref, dpv_ref, dnx_ref, x_ref, pv_ref, nx_ref, mu_ref, dx_ref, dmu_ref):
        i = pl.program_id(0)
        mu_v = mu_ref[...]
        dzv = dz_ref[...]
        m = dzv * mu_v
        mpv, mnx = _edge_rows(i, tm, seq, dpv_ref, dnx_ref, mu_v)
        dx_ref[...] = (dzv - m + _shift_mean(m, mpv, mnx)).astype(dx_ref.dtype)
        x = x_ref[...]
        pv, nx = _edge_rows(i, tm, seq, pv_ref, nx_ref)
        part = jnp.sum(dzv * (_shift_mean(x, pv, nx) - x), axis=0, keepdims=True)

        @pl.when(i == 0)
        def _():
            dmu_ref[...] = part

        @pl.when(i != 0)
        def _():
            dmu_ref[...] += part

    return _pc(
        body, name=name, grid=(n // tm,),
        in_specs=_halo_specs(tm, RW_W, 0, n // 8) + _halo_specs(tm, RW_W, blk, n // 8)
        + [pl.BlockSpec((1, RW_W), lambda i: (0, 0))],
        out_specs=[pl.BlockSpec((tm, RW_W), lambda i: (i, 0)), pl.BlockSpec((1, RW_W), lambda i: (0, 0))],
        out_shape=[_sds((n, RW_W), bf16), _sds((1, RW_W))], sem=("arbitrary",),
    )(dz, dz, dz, p, p, p, mu)


def _f_rw_pre(k, wl, al, gl, w0, a0, w2, a2, g2, k_k, k_a):
    w = w0 + _mm(jnp.tanh(wl), w2)
    lw = -jnp.exp(-_softplus(-w) - 0.5)
    a = _sigmoid(a0 + _mm(al, a2))
    g = _mm(_sigmoid(gl), g2)
    kkr = k * k_k
    kk = kkr / jnp.maximum(jnp.sqrt(_group_sum(kkr * kkr)), 1e-12)
    two = lambda t: jnp.concatenate([t, t], axis=1)
    kd = two(k) * (1.0 + (a - 1.0) * two(k_a))
    bd = two(kk) * a
    return lw, kd, kk, bd, g


def _f_rw_post(y0, y1, r, v, kd0, kd1, g, r_k, ln_g, ln_b):
    y = y0 + y1
    mean = _group_sum(y) * (1.0 / RW_HEAD)
    d = y - mean
    var = _group_sum(d * d) * (1.0 / RW_HEAD)
    yn = d * lax.rsqrt(var + GN_EPS) * ln_g + ln_b
    bonus = _group_sum(r * (kd0 + kd1) * r_k)
    return ((yn + bonus * v) * g,)


@jax.custom_vjp
def _tri_inv(mats):
    c = mats[0].shape[0]
    row = lax.broadcasted_iota(jnp.int32, (c, c), 0)
    col = lax.broadcasted_iota(jnp.int32, (c, c), 1)
    eye = (row == col).astype(f32)
    blk = lambda b: (row // b) == (col // b)
    ld = [jnp.where(blk(8), a, 0.0) for a in mats]
    l2 = [_bdot(x, x) for x in ld]
    l4 = [_bdot(x, x) for x in l2]
    t = [_bdot(eye - x, eye + y) for x, y in zip(ld, l2)]
    t = [_bdot(x, eye + y) for x, y in zip(t, l4)]
    b = 8
    while b < c:
        sub = blk(2 * b) & jnp.logical_not(blk(b))
        p = [_bdot(x, jnp.where(sub, a, 0.0)) for x, a in zip(t, mats)]
        t = [x - _bdot(y, x) for x, y in zip(t, p)]
        b *= 2
    return tuple(t)


def _tri_inv_fwd(mats):
    t = _tri_inv(mats)
    return t, t


def _tri_inv_bwd(ts, gs):
    p = [_bdot(t, g, TN) for t, g in zip(ts, gs)]
    return (tuple(-_bdot(x, t, NT) for x, t in zip(p, ts)),)


_tri_inv.defvjp(_tri_inv_fwd, _tri_inv_bwd)


@jax.custom_vjp
def _tri_inv_saved(mats, ts):
    return ts


_tri_inv_saved.defvjp(lambda mats, ts: (ts, ts),
                      lambda ts, gs: (_tri_inv_bwd(ts, gs)[0], tuple(jnp.zeros_like(t) for t in ts)))


def _split3(x):
    h = x.astype(bf16)
    r = x - h.astype(f32)
    m = r.astype(bf16)
    return h, m, (r - m.astype(f32)).astype(bf16)


@jax.custom_vjp
def _mask_mm(mask, x):
    mb = mask.astype(bf16)
    return _dot(jnp.concatenate([mb, mb, mb], axis=1), jnp.concatenate(_split3(x), axis=0), NN)


def _mask_mm_bwd(mask, g):
    mb = mask.astype(bf16)
    return jnp.zeros_like(mask), _dot(jnp.concatenate([mb, mb, mb], axis=0), jnp.concatenate(_split3(g), axis=0), TN)


_mask_mm.defvjp(lambda mask, x: (_mask_mm(mask, x), mask), _mask_mm_bwd)


@jax.custom_vjp
def _split_lanes(x):
    h = x.shape[1] // 2
    return x[:, :h], x[:, h:]


_split_lanes.defvjp(lambda x: (_split_lanes(x), None), lambda _, g: (jnp.concatenate(g, axis=1),))


def _scan_chunk(s0, r, v, kk, lw, kd, bd, rev, inv=None):
    n = len(r)
    each = range(n)
    c = r[0].shape[0]
    row = lax.broadcasted_iota(jnp.int32, (c, 2 * c), 0)
    col = lax.broadcasted_iota(jnp.int32, (c, 2 * c), 1) % c
    ahead = jnp.where(rev, col - row, row - col)
    before = ahead > 0
    incl = ahead >= 0
    lane = lax.broadcasted_iota(jnp.int32, (1, 128), 1)
    m0 = (lane < 64).astype(f32)
    heads = lambda t: jnp.concatenate([t * m0, t * (1.0 - m0)], axis=0)
    bd_mask = ((lax.broadcasted_iota(jnp.int32, (128, 128), 0) // 64)
               == (lax.broadcasted_iota(jnp.int32, (128, 128), 1) // 64)).astype(f32)
    tot = [jnp.sum(lw[i], axis=0, keepdims=True) for i in each]
    row1 = lax.broadcasted_iota(jnp.int32, (c, c), 0)
    col1 = lax.broadcasted_iota(jnp.int32, (c, c), 1)
    upto = (jnp.where(rev, col1 - row1, row1 - col1) >= 0).astype(f32)
    lp = [_mask_mm(upto, lw[i]) - 0.5 * tot[i] for i in each]
    eg = [jnp.exp(lp[i]) for i in each]
    ieg = [jnp.exp(-lp[i]) for i in each]
    rt = [r[i] * eg[i] for i in each]
    kt = [kd[i] * ieg[i] for i in each]
    bt = [bd[i] * ieg[i] for i in each]
    at = [kk[i] * jnp.exp(lp[i] - lw[i]) for i in each]
    etot = [jnp.exp(0.5 * tot[i]) for i in each]
    si = [s0[i] * etot[i] for i in each]
    bth = [heads(bt[i]) for i in each]
    kth = [heads(kt[i]) for i in each]
    vh = [heads(v[i]) for i in each]
    a_ab = [jnp.where(before, _mm_nt(at[i], bth[i]), 0.0) for i in each]
    a_ak = [jnp.where(before, _mm_nt(at[i], kth[i]), 0.0) for i in each]
    a_rb = [jnp.where(incl, _mm_nt(rt[i], bth[i]), 0.0) for i in each]
    a_rk = [jnp.where(incl, _mm_nt(rt[i], kth[i]), 0.0) for i in each]
    halves = [_split_lanes(a_ab[i]) for i in each]
    mats = tuple(m for pair in halves for m in pair)
    inv = _tri_inv(mats) if inv is None else _tri_inv_saved(mats, inv)
    t = [jnp.concatenate([inv[2 * i], inv[2 * i + 1]], axis=1) for i in each]
    x0 = [_mm_nt(at[i], si[i]) for i in each]
    x = [x0[i] + _mm(a_ak[i], vh[i]) for i in each]
    u = [-_mm(t[i], heads(x[i])) for i in each]
    y0 = [_mm_nt(rt[i], si[i]) for i in each]
    y = [y0[i] + _mm(jnp.concatenate([a_rb[i], a_rk[i]], axis=1), jnp.concatenate([heads(u[i]), vh[i]], axis=0))
         for i in each]
    ds = [_mm_tn(jnp.concatenate([u[i], v[i]], axis=0), jnp.concatenate([bt[i], kt[i]], axis=0)) for i in each]
    se = [(si[i] + ds[i] * bd_mask) * etot[i] for i in each]
    return tuple(y), tuple(se), inv


PAIRS = HEADS // 2


def _scan_specs(nc, bsz, flip=False):
    def cc(d, c):
        c = nc - 1 - c if flip else c
        return jnp.where(d == 0, c, nc - 1 - c)

    rowblk = lambda d, b, c: b * nc + cc(d, c)
    zspec = lambda blk: pl.BlockSpec((CHUNK, RW_DIM), lambda d, b, c: (rowblk(d, b, c), blk))
    dspec = pl.BlockSpec((CHUNK, RW_DIM), lambda d, b, c: (rowblk(d, b, c), d))
    yspec = pl.BlockSpec((None, CHUNK, RW_DIM), lambda d, b, c: (d, rowblk(d, b, c), 0))
    sspec = pl.BlockSpec((None, PAIRS, 128, 128), lambda d, b, c: ((d * bsz + b) * nc + cc(d, c), 0, 0, 0))
    tspec = pl.BlockSpec((None, PAIRS, 128, 256), lambda d, b, c: ((d * bsz + b) * nc + cc(d, c), 0, 0, 0))
    return zspec, dspec, yspec, sspec, tspec


def _scan_fwd(z, lw, kd, kk, bd, bsz, seq, name, ride=None):
    n = z.shape[0]
    nc = seq // CHUNK
    zspec, dspec, yspec, sspec, tspec = _scan_specs(nc, bsz)

    def body(r_ref, v_ref, kk_ref, lw_ref, kd_ref, bd_ref, y_ref, s_ref, t_ref, st):
        @pl.when(pl.program_id(2) == 0)
        def _():
            st[...] = jnp.zeros_like(st)

        rev = pl.program_id(0) == 1
        lanes = [slice(h * 128, (h + 1) * 128) for h in range(PAIRS)]
        s0 = tuple(st[h] for h in range(PAIRS))
        ops = [tuple(ref[:, ln] for ln in lanes) for ref in (r_ref, v_ref, kk_ref, lw_ref, kd_ref, bd_ref)]
        y, se, inv = _scan_chunk(s0, *ops, rev)
        for h, ln in enumerate(lanes):
            s_ref[h] = s0[h]
            t_ref[h, :, :128] = inv[2 * h]
            t_ref[h, :, 128:] = inv[2 * h + 1]
            y_ref[:, ln] = y[h]
            st[h] = se[h]

    return _pc(
        body, name=name, grid=(2, bsz, nc),
        in_specs=[zspec(0), zspec(2), zspec(0), dspec, dspec, dspec],
        out_specs=[yspec, sspec, tspec],
        out_shape=[_sds((2, n, RW_DIM)), _sds((2 * bsz * nc, PAIRS, 128, 128)), _sds((2 * bsz * nc, PAIRS, 128, 256))],
        scratch=[pltpu.VMEM((PAIRS, 128, 128), f32)], sem=("parallel", "parallel", "arbitrary"), ride=ride,
    )(z, z, kk, lw, kd, bd)


def _scan_bwd(z, lw, kd, kk, bd, s_in, t_in, dy, bsz, seq, name, ride=None):
    n = z.shape[0]
    nc = seq // CHUNK
    zspec, dspec, yspec, sspec, tspec = _scan_specs(nc, bsz, flip=True)

    def body(r_ref, v_ref, kk_ref, lw_ref, kd_ref, bd_ref, s_ref, t_ref, dy_ref,
             dr_ref, dv_ref, dkk_ref, dlw_ref, dkd_ref, dbd_ref, dst):
        @pl.when(pl.program_id(2) == 0)
        def _():
            dst[...] = jnp.zeros_like(dst)

        rev = pl.program_id(0) == 1
        lanes = [slice(h * 128, (h + 1) * 128) for h in range(PAIRS)]
        s0 = tuple(s_ref[h] for h in range(PAIRS))
        inv = tuple(t_ref[h, :, a * 128:(a + 1) * 128] for h in range(PAIRS) for a in range(2))
        ops = [tuple(ref[:, ln] for ln in lanes) for ref in (r_ref, v_ref, kk_ref, lw_ref, kd_ref, bd_ref)]
        cot = (tuple(dy_ref[:, ln] for ln in lanes), tuple(dst[h] for h in range(PAIRS)))
        _, vjp = jax.vjp(lambda *a: _scan_chunk(*a, rev=rev, inv=inv)[:2], s0, *ops)
        grads = vjp(cot)
        for h, ln in enumerate(lanes):
            dst[h] = grads[0][h]
            for o_ref, g in zip((dr_ref, dv_ref, dkk_ref, dlw_ref, dkd_ref, dbd_ref), grads[1:]):
                o_ref[:, ln] = g[h]

    return _pc(
        body, name=name, grid=(2, bsz, nc),
        in_specs=[zspec(0), zspec(2), zspec(0), dspec, dspec, dspec, sspec, tspec, zspec(0)],
        out_specs=[yspec, yspec, yspec, dspec, dspec, dspec],
        out_shape=[_sds((2, n, RW_DIM))] * 3 + [_sds((n, 2 * RW_DIM))] * 3,
        scratch=[pltpu.VMEM((PAIRS, 128, 128), f32)], sem=("parallel", "parallel", "arbitrary"), ride=ride,
    )(z, z, kk, lw, kd, bd, s_in, t_in, dy)


def _merge_fwd(x2, p, ya, yb, yc, gb, wb, wo, name):
    n = x2.shape[0]
    tm = min(256, n)

    def body(x_ref, pg_ref, ya_ref, yb_ref, yc_ref, gb_ref, wb_ref, wo_ref, o_ref):
        gates = _sigmoid(pg_ref[...] + gb_ref[...])
        merged = jnp.zeros((tm, D), f32)
        for k, y_ref in enumerate((ya_ref, yb_ref, yc_ref)):
            merged += gates[:, k * D:(k + 1) * D] * _bdot(y_ref[...], wb_ref[k])
        o_ref[...] = x_ref[...] + _bdot(merged, wo_ref[...])

    row = lambda w, b=0: pl.BlockSpec((tm, w), lambda i, b=b: (i, b))
    return _pc(
        body, name=name, grid=(n // tm,),
        in_specs=[row(D), row(3 * D, O_GATE // (3 * D)), row(512), row(512), row(512),
                  pl.BlockSpec((1, 3 * D), lambda i: (0, 0)), pl.BlockSpec((3, 512, D), lambda i: (0, 0, 0)),
                  pl.BlockSpec((D, D), lambda i: (0, 0))],
        out_specs=row(D), out_shape=_sds((n, D)), sem=("parallel",),
    )(x2, p, ya, yb, yc, gb, wb, wo)


def _merge_bwd(dx1, p, ya, yb, yc, gb, wb, wo, name):
    n = dx1.shape[0]
    tm = min(256, n)

    def body(dx_ref, pg_ref, ya_ref, yb_ref, yc_ref, gb_ref, wb_ref, wo_ref,
             dpg_ref, dya_ref, dyb_ref, dyc_ref, dt_ref, mg_ref, dgb_ref):
        gates = _sigmoid(pg_ref[...] + gb_ref[...])
        dmerged = _bdot(dx_ref[...], wo_ref[...], NT)
        merged = jnp.zeros((tm, D), f32)
        dpg = []
        for k, (y_ref, dy_ref) in enumerate(((ya_ref, dya_ref), (yb_ref, dyb_ref), (yc_ref, dyc_ref))):
            gk = gates[:, k * D:(k + 1) * D]
            tk = _bdot(y_ref[...], wb_ref[k])
            merged += gk * tk
            dpg.append(dmerged * tk * gk * (1.0 - gk))
            dtk = dmerged * gk
            dt_ref[:, k * D:(k + 1) * D] = dtk.astype(bf16)
            dy_ref[...] = _bdot(dtk, wb_ref[k], NT)
        dpg = jnp.concatenate(dpg, axis=1)
        dpg_ref[...] = dpg.astype(bf16)
        mg_ref[...] = merged.astype(bf16)
        part = jnp.sum(dpg, axis=0, keepdims=True)

        @pl.when(pl.program_id(0) == 0)
        def _():
            dgb_ref[...] = part

        @pl.when(pl.program_id(0) != 0)
        def _():
            dgb_ref[...] += part

    row = lambda w, b=0: pl.BlockSpec((tm, w), lambda i, b=b: (i, b))
    return _pc(
        body, name=name, grid=(n // tm,),
        in_specs=[row(D), row(3 * D, O_GATE // (3 * D)), row(512), row(512), row(512),
                  pl.BlockSpec((1, 3 * D), lambda i: (0, 0)), pl.BlockSpec((3, 512, D), lambda i: (0, 0, 0)),
                  pl.BlockSpec((D, D), lambda i: (0, 0))],
        out_specs=[row(3 * D), row(512), row(512), row(512), row(3 * D), row(D),
                   pl.BlockSpec((1, 3 * D), lambda i: (0, 0))],
        out_shape=[_sds((n, 3 * D), bf16), _sds((n, 512)), _sds((n, 512)), _sds((n, 512)), _sds((n, 3 * D), bf16),
                   _sds((n, D), bf16), _sds((1, 3 * D))],
        sem=("arbitrary",),
    )(dx1, p, ya, yb, yc, gb, wb, wo)


FF_T = 1408


def _ffn_fwd(x1, g, wg, wu, wd, name):
    n = x1.shape[0]
    tm = min(512, n)
    nf = D_FF // FF_T

    def body(x_ref, g_ref, wg_ref, wu_ref, wd_ref, o_ref, hs):
        j = pl.program_id(1)

        @pl.when(j == 0)
        def _():
            hs[...] = _rms(x_ref[...], g_ref[...]).astype(bf16)
            o_ref[...] = x_ref[...]

        a = _dot(hs[...], wg_ref[...], NN)
        b = _dot(hs[...], wu_ref[...], NN)
        o_ref[...] += _bdot(a * _sigmoid(a) * b, wd_ref[...])

    return _pc(
        body, name=name, grid=(n // tm, nf),
        in_specs=[pl.BlockSpec((tm, D), lambda i, j: (i, 0)), pl.BlockSpec((1, D), lambda i, j: (0, 0)),
                  pl.BlockSpec((D, FF_T), lambda i, j: (0, j)), pl.BlockSpec((D, FF_T), lambda i, j: (0, j)),
                  pl.BlockSpec((FF_T, D), lambda i, j: (j, 0))],
        out_specs=pl.BlockSpec((tm, D), lambda i, j: (i, 0)), out_shape=_sds((n, D)),
        scratch=[pltpu.VMEM((tm, D), bf16)], sem=("parallel", "arbitrary"),
    )(x1, g, wg, wu, wd)


def _ffn_bwd(dx2, x1, g, wg, wu, wd, name):
    n = x1.shape[0]
    tm = min(512, n)
    nf = D_FF // FF_T

    def body(dx_ref, x_ref, g_ref, wg_ref, wu_ref, wd_ref, dx1_ref, dg_ref, h_ref, da_ref, db_ref, hm_ref, acc):
        i, j = pl.program_id(0), pl.program_id(1)

        @pl.when(j == 0)
        def _():
            h_ref[...] = _rms(x_ref[...], g_ref[...]).astype(bf16)
            acc[...] = jnp.zeros_like(acc)

        @pl.when((i == 0) & (j == 0))
        def _():
            dg_ref[...] = jnp.zeros_like(dg_ref)

        h = h_ref[...]
        a = _dot(h, wg_ref[...], NN)
        b = _dot(h, wu_ref[...], NN)
        sg = _sigmoid(a)
        s = a * sg
        dhm = _bdot(dx_ref[...], wd_ref[...], NT)
        da = (dhm * b * (sg * (1.0 + a * (1.0 - sg)))).astype(bf16)
        db = (dhm * s).astype(bf16)
        da_ref[...] = da
        db_ref[...] = db
        hm_ref[...] = (s * b).astype(bf16)
        acc[...] += _dot(da, wg_ref[...], NT) + _dot(db, wu_ref[...], NT)

        @pl.when(j == nf - 1)
        def _():
            _, vjp = jax.vjp(_rms, x_ref[...], g_ref[...])
            dx, dg = vjp(acc[...])
            dx1_ref[...] = dx_ref[...] + dx
            dg_ref[...] += dg

    rowf = pl.BlockSpec((tm, FF_T), lambda i, j: (i, j))
    rowd = pl.BlockSpec((tm, D), lambda i, j: (i, 0))
    vec = pl.BlockSpec((1, D), lambda i, j: (0, 0))
    return _pc(
        body, name=name, grid=(n // tm, nf),
        in_specs=[rowd, rowd, vec, pl.BlockSpec((D, FF_T), lambda i, j: (0, j)),
                  pl.BlockSpec((D, FF_T), lambda i, j: (0, j)), pl.BlockSpec((FF_T, D), lambda i, j: (j, 0))],
        out_specs=[rowd, vec, rowd, rowf, rowf, rowf],
        out_shape=[_sds((n, D)), _sds((1, D)), _sds((n, D), bf16), _sds((n, D_FF), bf16), _sds((n, D_FF), bf16),
                   _sds((n, D_FF), bf16)],
        scratch=[pltpu.VMEM((tm, D), f32)], sem=("arbitrary", "arbitrary"),
    )(dx2, x1, g, wg, wu, wd)


def _loss_head(x2, g, tgt, name):
    n = x2.shape[0]
    tm = min(512, n)

    def f(x, gg, t):
        e = _rms(x, gg) - t
        return 0.5 * jnp.sum(jnp.mean(e * e, axis=-1, keepdims=True))

    def body(x_ref, g_ref, t_ref, l_ref, dx_ref, dg_ref):
        val, vjp = jax.vjp(f, x_ref[...], g_ref[...], t_ref[...])
        dx, dg, _ = vjp(jnp.ones((), f32))
        dx_ref[...] = dx

        @pl.when(pl.program_id(0) == 0)
        def _():
            l_ref[...] = jnp.zeros_like(l_ref)
            dg_ref[...] = jnp.zeros_like(dg_ref)

        l_ref[...] += val
        dg_ref[...] += dg

    rowd = pl.BlockSpec((tm, D), lambda i: (i, 0))
    return _pc(
        body, name=name, grid=(n // tm,),
        in_specs=[rowd, pl.BlockSpec((1, D), lambda i: (0, 0)), rowd],
        out_specs=[pl.BlockSpec((8, 128), lambda i: (0, 0)), rowd, pl.BlockSpec((1, D), lambda i: (0, 0))],
        out_shape=[_sds((8, 128)), _sds((n, D)), _sds((1, D))], sem=("arbitrary",),
    )(x2, g, tgt)


def _adamw(w, parts, m, v, name):
    nl, r, c = w.shape
    tr = r
    for cand in (1024, 512, 256, 128, 64, 32, 16, 8):
        if r % cand == 0 and cand * c * 4 <= 1024 * 1024:
            tr = cand
            break

    def body(*refs):
        w_ref, p_refs, (m_ref, v_ref, g_ref, d_ref, nm_ref, nv_ref) = refs[0], refs[1:1 + nl], refs[1 + nl:]

        def update(p_ref):
            gg = p_ref[0].astype(f32)
            for k in range(1, N_DEV):
                gg = gg + p_ref[k].astype(f32)
            g_ref[...] = gg
            nm = B1 * m_ref[...] + (1.0 - B1) * gg
            nv = B2 * v_ref[...] + (1.0 - B2) * (gg * gg)
            m_hat = nm / (1.0 - B1 ** STEP)
            v_hat = nv / (1.0 - B2 ** STEP)
            d_ref[...] = -LR * (m_hat / (jnp.sqrt(v_hat) + EPS) + WD * w_ref[...])
            nm_ref[...] = nm
            nv_ref[...] = nv

        for j in range(nl):
            pl.when(pl.program_id(0) == j)(functools.partial(update, p_refs[j]))

    spec = pl.BlockSpec((None, tr, c), lambda l, i: (l, i, 0))
    pspecs = [pl.BlockSpec((N_DEV, tr, c), lambda l, i, j=j: (0, jnp.where(l == j, i, 0), 0)) for j in range(nl)]
    return _pc(body, name=name, grid=(nl, r // tr), in_specs=[spec] + pspecs + [spec, spec], out_specs=[spec] * 4,
               out_shape=[_sds((nl, r, c))] * 4, sem=("arbitrary", "arbitrary"))(w, *parts, m, v)


def _peers():
    x, y, c = lax.axis_index("x"), lax.axis_index("y"), lax.axis_index("c")
    me = 4 * x + 2 * y + c
    peers = []
    for k in range(1, N_DEV):
        fx, fy, fc = (k >> 2) & 1, (k >> 1) & 1, k & 1
        peers.append(((1 - x) if fx else x, (1 - y) if fy else y, (1 - c) if fc else c))
    return me, peers


def _exchange(gathers, scatters, name):
    _, got = _pc(lambda: None, name=name, out_shape=[], ride=(gathers, scatters))()
    return got


SHARDED = {"w_in": 2, "gate_b": 2, "w_uq": 2, "w_ukv": 2, "rw_w0": 2, "rw_w2": 3, "rw_a0": 2, "rw_a2": 3, "rw_g2": 2,
           "w_branch": 3, "w_out": 1, "w_ffn_gate": 2, "w_ffn_up": 2, "w_ffn_down": 1}
GATHER_F32 = ("gate_b", "rw_w0", "rw_a0")
REPLICATED = ("attn_norm_g", "q_norm_g", "kv_norm_g", "sg_ln_g", "sg_ln_b", "sg_w", "sg_b", "rw_mu", "rw_k_k", "rw_k_a",
              "rw_r_k", "rw_ln_g", "rw_ln_b", "ffn_norm_g", "final_norm_g")
WEIGHTS = ("attn_norm_g", "w_in", "gate_b", "q_norm_g", "w_uq", "kv_norm_g", "w_ukv", "sg_ln_g", "sg_ln_b", "sg_w", "sg_b",
           "rw_mu", "rw_w0", "rw_w2", "rw_a0", "rw_a2", "rw_g2", "rw_k_k", "rw_k_a", "rw_r_k", "rw_ln_g", "rw_ln_b",
           "w_branch", "w_out", "ffn_norm_g", "w_ffn_gate", "w_ffn_up", "w_ffn_down", "final_norm_g")


REP_MAIN = tuple(k for k in REPLICATED if k not in ("attn_norm_g", "sg_w"))
BIG = ("w_in", "w_branch", "w_out", "w_ffn_gate", "w_ffn_up", "w_ffn_down")
SMALL_BF = ("w_uq", "w_ukv", "rw_w2", "rw_a2", "rw_g2")
SMALL = SMALL_BF + GATHER_F32


def _pack128(blocks, names, dtype, lead=0, to=256):
    parts = [blocks[k].astype(dtype).reshape(blocks[k].shape[:lead] + (-1, 128)) for k in names]
    rows = sum(p.shape[lead] for p in parts)
    pad = -rows % to
    if pad:
        parts.append(jnp.zeros(parts[0].shape[:lead] + (pad, 128), dtype))
    return jnp.concatenate(parts, axis=lead)


def _unpack128(packed, shapes, names, lead=0):
    out, off = {}, 0
    for k in names:
        rows = 1
        for d in shapes[k]:
            rows *= d
        rows //= 128
        idx = (slice(None),) * lead + (slice(off, off + rows),)
        out[k] = packed[idx].reshape(packed.shape[:lead] + tuple(shapes[k]))
        off += rows
    return out


def _join_blocks(g, ax):
    shp = g.shape[1:]
    return jnp.moveaxis(g, 0, ax).reshape(shp[:ax] + (N_DEV * shp[ax],) + shp[ax + 1:])


def _split_blocks(full, ax):
    shp = full.shape
    return jnp.moveaxis(full.reshape(shp[:ax] + (N_DEV, shp[ax] // N_DEV) + shp[ax + 1:]), ax, 0)


def _w_in_padded(w):
    z = lambda n: jnp.zeros((w.shape[0], n), w.dtype)
    q, ckv, kr = w[:, 0:384], w[:, 384:640], w[:, 640:672]
    sg, rw, gate = w[:, 672:1696], w[:, 1696:3616], w[:, 3616:6688]
    return jnp.concatenate([gate, sg, rw, z(128), ckv, z(64), kr, z(32), q, z(P_W - O_MLA - MLA_W)], axis=1)


def _w_in_unpadded(g):
    return jnp.concatenate([g[:, O_Q:O_Q + 384], g[:, O_CKV:O_CKV + 256], g[:, O_SLAB + 64:O_SLAB + 96],
                            g[:, O_SG:O_SG + 1024], g[:, O_RW:O_RW + 1920], g[:, O_GATE:O_GATE + 3072]], axis=1)


REST = ("w_branch", "w_out", "w_ffn_gate", "w_ffn_up", "w_ffn_down")


def _rest_weights(full, l):
    return dict(wb=full["w_branch"][l], wo=full["w_out"][l], wg=full["w_ffn_gate"][l], wu=full["w_ffn_up"][l],
                wd=full["w_ffn_down"][l])


def _layer_weights(full, rep, l):
    w = {}
    w["w_in"] = _w_in_padded(full["w_in"][l])
    if full["w_branch"][l] is not None:
        w.update(_rest_weights(full, l))
    uq = full["w_uq"][l].reshape(Q_LORA, HEADS, QK_NOPE + QK_ROPE)
    w["wq"] = jnp.pad(uq, ((0, 0), (0, 0), (0, 32))).reshape(Q_LORA, HEADS * 128).astype(f32)
    ukv = full["w_ukv"][l].reshape(KV_LORA, HEADS, QK_NOPE + V_HEAD)
    wk = jnp.pad(ukv[:, :, :QK_NOPE], ((0, 0), (0, 0), (0, 64))).reshape(KV_LORA, HEADS * 128)
    w["wk"], w["wv"] = wk.astype(f32), ukv[:, :, QK_NOPE:].reshape(KV_LORA, HEADS * V_HEAD).astype(f32)
    bdiag = lambda t: jnp.concatenate([jnp.concatenate([t[0], jnp.zeros_like(t[0])], axis=1),
                                       jnp.concatenate([jnp.zeros_like(t[1]), t[1]], axis=1)], axis=0).astype(f32)
    w["w2"], w["a2"] = bdiag(full["rw_w2"][l]), bdiag(full["rw_a2"][l])
    w["g2"] = full["rw_g2"][l].astype(f32)
    w["w0"], w["a0"] = full["rw_w0"][l].reshape(1, 2 * RW_DIM), full["rw_a0"][l].reshape(1, 2 * RW_DIM)
    w["gate_b"] = full["gate_b"][l].reshape(1, 3 * D)
    row = lambda a: a.reshape(1, -1)
    for k in ("attn_norm_g", "q_norm_g", "kv_norm_g", "sg_ln_g", "sg_ln_b", "rw_k_k", "rw_k_a", "rw_ln_g", "rw_ln_b",
              "ffn_norm_g"):
        w[k] = row(rep[k][l])
    w["r_k"] = row(rep["rw_r_k"][l])
    w["mu"] = jnp.pad(row(rep["rw_mu"][l]), ((0, 0), (0, RW_W - 1920)))
    w["sg_w"] = [rep["sg_w"][l, k] for k in range(SG_GROUPS)]
    w["sg_bias"] = jnp.repeat(rep["sg_b"][l].T, SG_DIM // SG_GROUPS, axis=1)
    return w


def _riding(res, ride, got, key):
    if ride is None:
        return res
    got[key] = res[1]
    return res[0]


def _layer_fwd(x2, w, tabs, bsz, seq, l, rides=None, on_gathered=None):
    nm = lambda s: f"l{l}_{s}"
    n = x2.shape[0]
    tm = min(256, n)
    rides = rides or {}
    ride = lambda key: (rides[key], []) if key in rides else None
    got = {}
    p, h = _riding(_inproj_fwd(x2, w["attn_norm_g"], w["w_in"], nm("inproj"), ride("inproj")), ride("inproj"), got, "inproj")
    mla_rows = [(p, 256, O_CKV // 256), (p, 128, O_SLAB // 128), (p, 384, O_Q // 384), (tabs[0], 128, 0), (tabs[1], 128, 0)]
    mla_w = [w["q_norm_g"], w["kv_norm_g"], w["wq"], w["wk"], w["wv"]]
    q, k, v = _rowwise_fwd(nm("mla_proj"), _f_mla_proj, mla_rows, mla_w, [(1024, bf16), (1024, bf16), (512, bf16)], tm)
    ya, lse = _riding(_attn_fwd(q, k, v, bsz, seq, nm("attn"), ride("attn")), ride("attn"), got, "attn")
    sg_rows = [(p, SG_DIM, O_SG // SG_DIM), (p, SG_DIM, O_SG // SG_DIM + 1)]
    sg_w = [w["sg_ln_g"], w["sg_ln_b"], w["sg_bias"]] + w["sg_w"]
    (yb,) = _rowwise_fwd(nm("sg"), _f_sg, sg_rows, sg_w, [(SG_DIM, f32)], SG_CHUNK)
    z = _shift_fwd(p, w["mu"], seq, nm("shift"))
    pre_rows = [(z, 512, 1), (z, 128, 12), (z, 128, 13), (z, 128, 14)]
    pre_w = [w["w0"], w["a0"], w["w2"], w["a2"], w["g2"], w["rw_k_k"], w["rw_k_a"]]
    lw, kd, kk, bd, g = _rowwise_fwd(nm("rw_pre"), _f_rw_pre, pre_rows, pre_w,
                                     [(1024, f32), (1024, f32), (512, f32), (1024, f32), (512, f32)], tm)
    y, s_in, t_in = _riding(_scan_fwd(z, lw, kd, kk, bd, bsz, seq, nm("scan"), ride("scan")), ride("scan"), got, "scan")
    post_rows = [(y[0], 512, 0), (y[1], 512, 0), (z, 512, 0), (z, 512, 2), (kd, 512, 0), (kd, 512, 1), (g, 512, 0)]
    post_w = [w["r_k"], w["rw_ln_g"], w["rw_ln_b"]]
    (yc,) = _rowwise_fwd(nm("rw_post"), _f_rw_post, post_rows, post_w, [(512, f32)], tm)
    if on_gathered is not None:
        w.update(on_gathered(got))
    x1 = _merge_fwd(x2, p, ya, yb, yc, w["gate_b"], w["wb"], w["wo"], nm("merge"))
    x3 = _ffn_fwd(x1, w["ffn_norm_g"], w["wg"], w["wu"], w["wd"], nm("ffn"))
    saved = dict(x=x2, p=p, h=h, q=q, k=k, v=v, ya=ya, lse=lse, yb=yb, z=z, lw=lw, kd=kd, kk=kk, bd=bd, g=g, y=y, s_in=s_in, t_in=t_in, yc=yc,
                 x1=x1, mla_rows=mla_rows, mla_w=mla_w, sg_rows=sg_rows, sg_w=sg_w, pre_rows=pre_rows, pre_w=pre_w,
                 post_rows=post_rows, post_w=post_w)
    return x3, saved, got


def _layer_bwd(dx3, w, sv, bsz, seq, l, rides=None):
    nm = lambda s: f"l{l}_{s}_bwd"
    n = dx3.shape[0]
    tm = min(256, n)
    g = {}
    rides = rides or {}
    ride = lambda key: rides[key](g) if key in rides else None
    got = {}
    dx1, g["ffn_norm_g"], h2, da, db, hm = _ffn_bwd(dx3, sv["x1"], w["ffn_norm_g"], w["wg"], w["wu"], w["wd"], nm("ffn"))
    g["wg"] = _matmul_tn(h2, da, nm("wg"))
    g["wu"] = _matmul_tn(h2, db, nm("wu"))
    g["wd"] = _matmul_tn(hm, dx3, nm("wd"))
    dpg, dya, dyb, dyc, dt, mg, g["gate_b"] = _merge_bwd(dx1, sv["p"], sv["ya"], sv["yb"], sv["yc"], w["gate_b"], w["wb"],
                                                         w["wo"], nm("merge"))
    g["wo"] = _matmul_tn(mg, dx1, nm("wo"))
    ys = (sv["ya"], sv["yb"], sv["yc"])
    g["wb"] = jnp.stack([_matmul_tn(ys[k], dt[:, k * D:(k + 1) * D], nm(f"wb{k}")) for k in range(3)])
    (dy, dr_p, dv_p, dkd0, dkd1, dg_), (g["r_k"], g["rw_ln_g"], g["rw_ln_b"]) = _rowwise_bwd(
        nm("rw_post"), _f_rw_post, sv["post_rows"], sv["post_w"], [(dyc, 512, 0)], tm, [f32, None] + [f32] * 5)
    dkd_p = jnp.concatenate([dkd0, dkd1], axis=1)
    rd = ride("scan")
    dr_s, dv_s, dkk_s, dlw, dkd_s, dbd = _riding(
        _scan_bwd(sv["z"], sv["lw"], sv["kd"], sv["kk"], sv["bd"], sv["s_in"], sv["t_in"], dy, bsz, seq, nm("scan"), rd),
        rd, got, "scan")
    pre_cots = [(dlw, 1024, 0), (dkd_s + dkd_p, 1024, 0), (dkk_s[0] + dkk_s[1], 512, 0), (dbd, 1024, 0), (dg_, 512, 0)]
    (dk, dwl, dal, dgl), (g["w0"], g["a0"], g["w2"], g["a2"], g["g2"], g["rw_k_k"], g["rw_k_a"]) = _rowwise_bwd(
        nm("rw_pre"), _f_rw_pre, sv["pre_rows"], sv["pre_w"], pre_cots, tm, [f32] * 4)
    dz = jnp.concatenate([dr_s[0] + dr_s[1] + dr_p, dk, dv_s[0] + dv_s[1] + dv_p, dwl, dal, dgl,
                          jnp.zeros((n, RW_W - 1920), f32)], axis=1)
    dp_rw, g["mu"] = _shift_bwd(dz, sv["p"], w["mu"], seq, nm("shift"))
    (dp_su, dp_sv), (g["sg_ln_g"], g["sg_ln_b"], g["sg_bias"], *sgw) = _rowwise_bwd(
        nm("sg"), _f_sg, sv["sg_rows"], sv["sg_w"], [(dyb, SG_DIM, 0)], SG_CHUNK, [bf16, bf16])
    g["sg_w"] = jnp.stack(sgw)
    rd = ride("attn")
    dq, dk_, dv_ = _riding(_attn_bwd(sv["q"], sv["k"], sv["v"], sv["ya"], sv["lse"], dya, bsz, seq, nm("attn"), rd), rd, got,
                           "attn")
    (dp_ckv, dp_slab, dp_q), (g["q_norm_g"], g["kv_norm_g"], g["wq"], g["wk"], g["wv"]) = _rowwise_bwd(
        nm("mla_proj"), _f_mla_proj, sv["mla_rows"], sv["mla_w"], [(dq, 1024, 0), (dk_, 1024, 0), (dv_, 512, 0)], tm,
        [bf16, bf16, bf16, None, None])
    dp = jnp.concatenate([dpg, dp_su, dp_sv, dp_rw, dp_ckv, dp_slab, dp_q, jnp.zeros((n, P_W - O_MLA - MLA_W), bf16)],
                         axis=1)
    rd = ride("w_in")
    g["w_in"] = _riding(_matmul_tn(sv["h"], dp, nm("w_in"), rd), rd, got, "w_in")
    rd = ride("inproj")
    dx, g["attn_norm_g"] = _riding(_norm_matmul_bwd(dp, w["w_in"], sv["x"], w["attn_norm_g"], dx1, nm("inproj"), rd), rd, got,
                                   "inproj")
    return dx, g, got


def _layer_grads_to_full(g):
    o = {}
    if "w_in" in g:
        o["w_in"] = _w_in_unpadded(g["w_in"])
    o["w_uq"] = g["wq"].reshape(Q_LORA, HEADS, 128)[:, :, :QK_NOPE + QK_ROPE].reshape(Q_LORA, -1)
    gk = g["wk"].reshape(KV_LORA, HEADS, 128)[:, :, :QK_NOPE]
    gv = g["wv"].reshape(KV_LORA, HEADS, V_HEAD)
    o["w_ukv"] = jnp.concatenate([gk, gv], axis=2).reshape(KV_LORA, -1)
    unb = lambda t: jnp.stack([t[:LORA, :RW_DIM], t[LORA:, RW_DIM:]])
    o["rw_w2"], o["rw_a2"], o["rw_g2"] = unb(g["w2"]), unb(g["a2"]), g["g2"]
    o["rw_w0"], o["rw_a0"] = g["w0"].reshape(2, RW_DIM), g["a0"].reshape(2, RW_DIM)
    o["gate_b"] = g["gate_b"].reshape(3, D)
    o["w_branch"], o["w_out"] = g["wb"], g["wo"]
    o["w_ffn_gate"], o["w_ffn_up"], o["w_ffn_down"] = g["wg"], g["wu"], g["wd"]
    for k in ("attn_norm_g", "q_norm_g", "kv_norm_g", "sg_ln_g", "sg_ln_b", "rw_k_k", "rw_k_a", "rw_ln_g", "rw_ln_b",
              "ffn_norm_g"):
        if k in g:
            o[k] = g[k].reshape(-1)
    o["rw_r_k"] = g["r_k"].reshape(HEADS, RW_HEAD)
    o["rw_mu"] = g["mu"].reshape(-1)[:1920]
    o["sg_w"] = g["sg_w"]
    o["sg_b"] = g["sg_bias"].reshape(SG_CHUNK, SG_GROUPS, SG_DIM // SG_GROUPS).sum(axis=2).T
    return o


def _rope_tables(positions):
    inv = 1.0 / (10000.0 ** (jnp.arange(0, QK_ROPE, 2, dtype=f32) / QK_ROPE))
    ang = positions.astype(f32)[:, None] * inv
    cos, sin = jnp.cos(ang), jnp.sin(ang)
    n = positions.shape[0]
    c = jnp.concatenate([jnp.ones((n, 64), f32), cos, cos, jnp.zeros((n, 32), f32)], axis=1)
    s = jnp.concatenate([jnp.zeros((n, 64), f32), -sin, sin, jnp.zeros((n, 32), f32)], axis=1)
    return c, s


def _grad_parts(grad, name):
    return _split_blocks(grad, SHARDED[name] - 1).astype(bf16)


def _local_step(x, positions, full, rep, loss_target, blocks=None):
    bsz, seq, _ = x.shape
    n = bsz * seq
    x2 = x.reshape(n, D)
    tabs = _rope_tables(positions.reshape(n))
    join = lambda k, g: _join_blocks(g, SHARDED[k] - 1)
    rides, on_gathered = None, None
    if blocks is not None:
        carried = {"inproj": [("w_branch", 0), ("w_out", 0)],
                   "attn": [("w_in", 1), ("w_ffn_gate", 0), ("w_ffn_up", 0)],
                   "scan": [("w_ffn_down", 0)] + [(k, 1) for k in REST]}
        rides = {key: [blocks[k][l] for k, l in what] for key, what in carried.items()}

        def on_gathered(got):
            for key, what in carried.items():
                for (k, l), g in zip(what, got[key]):
                    full[k][l] = join(k, g)
            return _rest_weights(full, 0)

    w0 = _layer_weights(full, rep, 0)
    x2, sv0, got = _layer_fwd(x2, w0, tabs, bsz, seq, 0, rides, on_gathered)
    w1 = _layer_weights(full, rep, 1)
    x2, sv1, _ = _layer_fwd(x2, w1, tabs, bsz, seq, 1)
    loss, dx, dgf = _loss_head(x2, rep["final_norm_g"].reshape(1, D), loss_target.reshape(n, D), "loss_head")
    dx, g1, _ = _layer_bwd(dx, w1, sv1, bsz, seq, 1)
    grads1 = _layer_grads_to_full(g1)
    rides = None
    if blocks is not None:
        short = dict(w_branch="wb", w_out="wo", w_ffn_gate="wg", w_ffn_up="wu", w_ffn_down="wd")

        def beside_w_in(g):
            g0 = _layer_grads_to_full(g)
            both = {k: jnp.stack([g0[k], grads1[k]]) for k in g0}
            both["final_norm_g"] = dgf.reshape(D)
            split = {k: _split_blocks(both[k], SHARDED[k]) for k in SMALL}
            return [_pack128(both, REP_MAIN, f32), both["sg_w"]], [_pack128(split, SMALL, f32, lead=1)]

        rides = {"scan": lambda g: ([], [_grad_parts(grads1[k], k) for k in BIG]),
                 "attn": lambda g: ([], [_grad_parts(g[short[k]], k) for k in REST]),
                 "w_in": beside_w_in,
                 "inproj": lambda g: ([], [_grad_parts(_w_in_unpadded(g["w_in"]), "w_in")])}
    dx, g0, got = _layer_bwd(dx, w0, sv0, bsz, seq, 0, rides)
    grads0 = _layer_grads_to_full(g0)
    grads = {k: [grads0[k], grads1[k]] for k in grads0}
    grads["final_norm_g"] = dgf.reshape(D)
    parts = {}
    if blocks is not None:
        parts = {k: [None, p] for k, p in zip(BIG, got["scan"])}
        for k, p in zip(REST, got["attn"]):
            parts[k][0] = p
        parts["replicated"], parts["sg_w"], parts["small"] = got["w_in"]
        (parts["w_in"][0],) = got["inproj"]
    return loss[0, 0], dx.reshape(bsz, seq, D), grads, parts


def kernel(x, positions, attn_norm_g, w_in, gate_b, q_norm_g, w_uq, kv_norm_g, w_ukv, sg_ln_g, sg_ln_b, sg_w, sg_b, rw_mu, rw_w0, rw_w2, rw_a0, rw_a2, rw_g2, rw_k_k, rw_k_a, rw_r_k, rw_ln_g, rw_ln_b, w_branch, w_out, ffn_norm_g, w_ffn_gate, w_ffn_up, w_ffn_down, final_norm_g, loss_target, m_attn_norm_g, m_w_in, m_gate_b, m_q_norm_g, m_w_uq, m_kv_norm_g, m_w_ukv, m_sg_ln_g, m_sg_ln_b, m_sg_w, m_sg_b, m_rw_mu, m_rw_w0, m_rw_w2, m_rw_a0, m_rw_a2, m_rw_g2, m_rw_k_k, m_rw_k_a, m_rw_r_k, m_rw_ln_g, m_rw_ln_b, m_w_branch, m_w_out, m_ffn_norm_g, m_w_ffn_gate, m_w_ffn_up, m_w_ffn_down, m_final_norm_g, v_attn_norm_g, v_w_in, v_gate_b, v_q_norm_g, v_w_uq, v_kv_norm_g, v_w_ukv, v_sg_ln_g, v_sg_ln_b, v_sg_w, v_sg_b, v_rw_mu, v_rw_w0, v_rw_w2, v_rw_a0, v_rw_a2, v_rw_g2, v_rw_k_k, v_rw_k_a, v_rw_r_k, v_rw_ln_g, v_rw_ln_b, v_w_branch, v_w_out, v_ffn_norm_g, v_w_ffn_gate, v_w_ffn_up, v_w_ffn_down, v_final_norm_g):
    args = locals()
    wts = {k: args[k] for k in WEIGHTS}
    mom_m = {k: args["m_" + k] for k in WEIGHTS}
    mom_v = {k: args["v_" + k] for k in WEIGHTS}
    shapes = {k: wts[k].shape for k in WEIGHTS}
    blocks = {k: wts[k].astype(bf16) for k in BIG}
    got = _exchange([blocks["w_in"][0], _pack128(wts, SMALL_BF, bf16), _pack128(wts, GATHER_F32, f32)], [], "gather_first")
    small = {**_unpack128(got[1], shapes, SMALL_BF, lead=1), **_unpack128(got[2], shapes, GATHER_F32, lead=1)}
    full = {k: list(_join_blocks(small[k], SHARDED[k])) for k in SMALL}
    full["w_in"] = [_join_blocks(got[0], SHARDED["w_in"] - 1), None]
    full.update({k: [None, None] for k in REST})
    rep = {k: wts[k] for k in REPLICATED}
    loss, grad_x, grads, parts = _local_step(x, positions, full, rep, loss_target, blocks)
    loss = lax.psum(loss, ("x", "y", "c"))
    last = ("attn_norm_g",)
    (last_parts,) = _exchange([_pack128({"attn_norm_g": jnp.stack(grads["attn_norm_g"])}, last, f32, to=16)], [],
                              "exchange_last")
    gw, delta, new_m, new_v = {}, {}, {}, {}
    for k in BIG:
        three = lambda a, k=k: a.reshape(a.shape[0], -1, shapes[k][-1])
        res = _adamw(three(wts[k]), [three(p) for p in parts[k]], three(mom_m[k]), three(mom_v[k]), f"adamw_{k}")
        gw[k], delta[k], new_m[k], new_v[k] = (t.reshape(shapes[k]) for t in res)
    rows = lambda a: a.reshape(1, -1, 128)
    res = _adamw(rows(wts["sg_w"]), [parts["sg_w"].reshape(N_DEV, -1, 128)], rows(mom_m["sg_w"]), rows(mom_v["sg_w"]),
                 "adamw_sg_w")
    gw["sg_w"], delta["sg_w"], new_m["sg_w"], new_v["sg_w"] = (t.reshape(shapes["sg_w"]) for t in res)
    for names, got, to in ((SMALL, parts["small"], 256), (REP_MAIN, parts["replicated"], 256), (last, last_parts, 16)):
        pk = lambda dct: _pack128(dct, names, f32, to=to)[None]
        res = _adamw(pk(wts), [got], pk(mom_m), pk(mom_v), f"adamw_{names[0]}")
        for dst, t in zip((gw, delta, new_m, new_v), res):
            dst.update(_unpack128(t[0], shapes, names))
    return (loss, grad_x, *[gw[k] for k in WEIGHTS], *[delta[k] for k in WEIGHTS], *[new_m[k] for k in WEIGHTS],
            *[new_v[k] for k in WEIGHTS])
```

```python
import functools

import jax
import jax.numpy as jnp
from jax import lax
from jax.experimental import pallas as pl
from jax.experimental.pallas import tpu as pltpu

f32 = jnp.float32
bf16 = jnp.bfloat16
HI = lax.Precision.HIGHEST
NN, NT, TN = ((1,), (0,)), ((1,), (1,)), ((0,), (0,))

N_DEV = 8
D = 1024
HEADS = 8
Q_LORA, KV_LORA, QK_NOPE, QK_ROPE, V_HEAD = 384, 256, 64, 32, 64
SG_DIM, SG_CHUNK, SG_GROUPS = 512, 128, 8
RW_DIM, RW_HEAD, LORA = 512, 64, 64
D_FF = 2816
N_IN = 6688
NORM_EPS, LN_EPS, GN_EPS = 1e-6, 1e-5, 64e-5
ATT_SCALE = (QK_NOPE + QK_ROPE) ** -0.5
P_W = 7168
O_GATE, O_SG, O_RW, O_MLA = 0, 3072, 4096, 6144
RW_W = 2048
MLA_W = 768
O_CKV, O_SLAB, O_Q = O_MLA, O_MLA + 256, O_MLA + 384
CHUNK = 128
VMEM_LIMIT = 56 * 1024 * 1024

B1, B2, LR, EPS, WD, STEP = 0.9, 0.999, 0.001, 1e-8, 0.01, 10


def _pc(body, *, name, out_shape, grid=(), in_specs=(), out_specs=(), scratch=(), sem=None, ride=None):
    params = pltpu.CompilerParams(dimension_semantics=sem, vmem_limit_bytes=VMEM_LIMIT)
    if ride is None:
        return pl.pallas_call(body, out_shape=out_shape, grid=grid, in_specs=in_specs, out_specs=out_specs,
                              scratch_shapes=scratch, compiler_params=params, name=name, interpret=False)
    gathers, scatters = ride
    moved = list(gathers) + list(scatters)
    ng, nx = len(gathers), len(moved)
    single = not isinstance(out_shape, (list, tuple))
    outs = [out_shape] if single else list(out_shape)
    ospecs = [out_specs] if single else list(out_specs)
    n_in, n_out, n_scr = len(in_specs), len(outs), len(scratch)
    per = N_DEV - 1

    def riding(*refs):
        ins, xin = refs[:n_in], refs[n_in:n_in + nx]
        outs_r, xout = refs[n_in + nx:n_in + nx + n_out], refs[n_in + nx + n_out:n_in + 2 * nx + n_out]
        own = refs[n_in + 2 * nx + n_out:n_in + 2 * nx + n_out + n_scr]
        send_sems, recv_sems, local_sems = refs[n_in + 2 * nx + n_out + n_scr:]

        def copies():
            me, peers = _peers()
            cps = []
            for a in range(nx):
                whole = a < ng
                cps.append(pltpu.make_async_copy(xin[a] if whole else xin[a].at[me], xout[a].at[me], local_sems.at[a]))
                for k, peer in enumerate(peers):
                    dev = 4 * peer[0] + 2 * peer[1] + peer[2]
                    cps.append(pltpu.make_async_remote_copy(
                        src_ref=xin[a] if whole else xin[a].at[dev], dst_ref=xout[a].at[me],
                        send_sem=send_sems.at[a * per + k], recv_sem=recv_sems.at[a * per + k], device_id=peer,
                        device_id_type=pl.DeviceIdType.MESH))
            return cps

        if not grid:
            for cp in copies():
                cp.start()
            body(*ins, *outs_r, *own)
            for cp in copies():
                cp.wait()
            return
        ids = [pl.program_id(a) for a in range(len(grid))]
        first = functools.reduce(jnp.logical_and, [i == 0 for i in ids])
        last = functools.reduce(jnp.logical_and, [i == g - 1 for i, g in zip(ids, grid)])

        @pl.when(first)
        def _():
            for cp in copies():
                cp.start()

        body(*ins, *outs_r, *own)

        @pl.when(last)
        def _():
            for cp in copies():
                cp.wait()

    anyspec = pl.BlockSpec(memory_space=pl.ANY)
    call = pl.pallas_call(
        riding, grid=grid, in_specs=list(in_specs) + [anyspec] * nx, out_specs=ospecs + [anyspec] * nx,
        out_shape=outs + [_sds((N_DEV,) + a.shape, a.dtype) for a in gathers] + [_sds(a.shape, a.dtype) for a in scatters],
        scratch_shapes=list(scratch) + [pltpu.SemaphoreType.DMA((nx * per,)), pltpu.SemaphoreType.DMA((nx * per,)),
                                        pltpu.SemaphoreType.DMA((nx,))],
        compiler_params=params, name=name, interpret=False)

    def run(*args):
        res = call(*args, *moved)
        own = res[0] if single else list(res[:n_out])
        return own, list(res[n_out:])

    return run


def _sds(shape, dtype=f32):
    return jax.ShapeDtypeStruct(tuple(shape), dtype)


def _dot(a, b, dims, precision=None):
    return lax.dot_general(a, b, (dims, ((), ())), preferred_element_type=f32, precision=precision)


def _bdot(a, b, dims=NN):
    return _dot(a.astype(bf16), b.astype(bf16), dims)


@jax.custom_vjp
def _mm(a, w):
    return _bdot(a, w, NN)


def _mm_fwd(a, w):
    return _bdot(a, w, NN), (a, w)


def _mm_bwd(res, g):
    a, w = res
    return _bdot(g, w, NT), _bdot(a, g, TN)


_mm.defvjp(_mm_fwd, _mm_bwd)


@jax.custom_vjp
def _mm_nt(a, b):
    return _bdot(a, b, NT)


def _mm_nt_fwd(a, b):
    return _bdot(a, b, NT), (a, b)


def _mm_nt_bwd(res, g):
    a, b = res
    return _bdot(g, b, NN), _bdot(g, a, TN)


_mm_nt.defvjp(_mm_nt_fwd, _mm_nt_bwd)


@jax.custom_vjp
def _mm_tn(a, b):
    return _bdot(a, b, TN)


def _mm_tn_fwd(a, b):
    return _bdot(a, b, TN), (a, b)


def _mm_tn_bwd(res, g):
    a, b = res
    return _bdot(b, g, NT), _bdot(a, g, NN)


_mm_tn.defvjp(_mm_tn_fwd, _mm_tn_bwd)


def _rms(x, g):
    return x * lax.rsqrt(jnp.mean(x * x, axis=-1, keepdims=True) + NORM_EPS) * g


def _sigmoid(x):
    return 1.0 / (1.0 + jnp.exp(-x))


def _gelu(x):
    return 0.5 * x * (1.0 + jnp.tanh(0.7978845608028654 * (x + 0.044715 * x * x * x)))


def _softplus(x):
    return jnp.maximum(x, 0.0) + jnp.log(1.0 + jnp.exp(-jnp.abs(x)))


@jax.custom_vjp
def _group_sum(x):
    w = x.shape[-1]
    r = lax.broadcasted_iota(jnp.int32, (w, w), 0) // RW_HEAD
    c = lax.broadcasted_iota(jnp.int32, (w, w), 1) // RW_HEAD
    ones = (r == c).astype(bf16)
    hi = x.astype(bf16)
    lo = (x - hi.astype(f32)).astype(bf16)
    return _dot(jnp.concatenate([hi, lo], axis=1), jnp.concatenate([ones, ones], axis=0), NN)


_group_sum.defvjp(lambda x: (_group_sum(x), None), lambda _, g: (_group_sum(g),))


@jax.custom_vjp
def _swap(x):
    w = x.shape[-1]
    lane = lax.broadcasted_iota(jnp.int32, x.shape, 1) % 128
    lo = (lane >= 64) & (lane < 80)
    hi = (lane >= 80) & (lane < 96)
    return jnp.where(lo, pltpu.roll(x, w - 16, 1), jnp.where(hi, pltpu.roll(x, 16, 1), 0.0))


_swap.defvjp(lambda x: (_swap(x), None), lambda _, g: (_swap(g),))


def _rope(x, c, s):
    return x * c + _swap(x) * s


def _row_spec(tm, width, blk):
    return pl.BlockSpec((tm, width), lambda i, blk=blk: (i, blk))


def _full_spec(a):
    nd = a.ndim
    return pl.BlockSpec(a.shape, lambda i, nd=nd: (0,) * nd)


def _rowwise_fwd(name, f, rows, weights, outs, tm):
    n = rows[0][0].shape[0]
    nr, nw = len(rows), len(weights)

    def body(*refs):
        vals = [r[...].astype(f32) for r in refs[:nr + nw]]
        res = f(*vals)
        for o_ref, o in zip(refs[nr + nw:], res):
            o_ref[...] = o.astype(o_ref.dtype)

    return _pc(
        body, name=name, grid=(n // tm,),
        in_specs=[_row_spec(tm, w, b) for _, w, b in rows] + [_full_spec(w) for w in weights],
        out_specs=[_row_spec(tm, w, 0) for w, _ in outs],
        out_shape=[_sds((n, w), dt) for w, dt in outs], sem=("parallel",),
    )(*[a for a, _, _ in rows], *weights)


def _rowwise_bwd(name, f, rows, weights, cots, tm, drows):
    n = rows[0][0].shape[0]
    nr, nw, nc = len(rows), len(weights), len(cots)
    want = [k for k, dt in enumerate(drows) if dt is not None]

    def body(*refs):
        vals = [r[...].astype(f32) for r in refs[:nr + nw]]
        cot = tuple(r[...].astype(f32) for r in refs[nr + nw:nr + nw + nc])
        _, vjp = jax.vjp(f, *vals)
        grads = vjp(cot)
        outs = refs[nr + nw + nc:]
        for o_ref, k in zip(outs[:len(want)], want):
            o_ref[...] = grads[k].astype(o_ref.dtype)
        first = pl.program_id(0) == 0
        for o_ref, g in zip(outs[len(want):], grads[nr:]):
            @pl.when(first)
            def _(o_ref=o_ref, g=g):
                o_ref[...] = g

            @pl.when(jnp.logical_not(first))
            def _(o_ref=o_ref, g=g):
                o_ref[...] += g

    res = _pc(
        body, name=name, grid=(n // tm,),
        in_specs=[_row_spec(tm, w, b) for _, w, b in rows] + [_full_spec(w) for w in weights]
        + [_row_spec(tm, w, b) for _, w, b in cots],
        out_specs=[_row_spec(tm, rows[k][1], 0) for k in want] + [_full_spec(w) for w in weights],
        out_shape=[_sds((n, rows[k][1]), drows[k]) for k in want] + [_sds(w.shape) for w in weights],
        sem=("arbitrary",),
    )(*[a for a, _, _ in rows], *weights, *[a for a, _, _ in cots])
    return res[:len(want)], res[len(want):]


def _inproj_fwd(x2, g, w, name, ride=None):
    n = x2.shape[0]
    tm, tn = min(1024, n), 1024

    def body(x_ref, g_ref, w_ref, p_ref, h_ref):
        @pl.when(pl.program_id(1) == 0)
        def _():
            h_ref[...] = _rms(x_ref[...], g_ref[...]).astype(bf16)

        p_ref[...] = jnp.dot(h_ref[...], w_ref[...], preferred_element_type=f32)

    return _pc(
        body, name=name, grid=(n // tm, P_W // tn),
        in_specs=[pl.BlockSpec((tm, D), lambda i, j: (i, 0)), pl.BlockSpec((1, D), lambda i, j: (0, 0)),
                  pl.BlockSpec((D, tn), lambda i, j: (0, j))],
        out_specs=[pl.BlockSpec((tm, tn), lambda i, j: (i, j)), pl.BlockSpec((tm, D), lambda i, j: (i, 0))],
        out_shape=[_sds((n, P_W)), _sds((n, D), bf16)], sem=("parallel", "arbitrary"), ride=ride,
    )(x2, g, w)


def _norm_matmul_bwd(dy, w, x2, g, dres, name, ride=None):
    n, k = dy.shape
    tm = min(1024, n)
    tk = 1024 if k % 1024 == 0 else 1408
    nk = k // tk

    def body(dy_ref, w_ref, x_ref, g_ref, dr_ref, dx_ref, dg_ref, acc):
        i, j = pl.program_id(0), pl.program_id(1)

        @pl.when(j == 0)
        def _():
            acc[...] = jnp.zeros_like(acc)

        @pl.when((i == 0) & (j == 0))
        def _():
            dg_ref[...] = jnp.zeros_like(dg_ref)

        acc[...] += _dot(dy_ref[...], w_ref[...], NT)

        @pl.when(j == nk - 1)
        def _():
            _, vjp = jax.vjp(_rms, x_ref[...], g_ref[...])
            dx, dg = vjp(acc[...])
            dx_ref[...] = dr_ref[...] + dx
            dg_ref[...] += dg

    return _pc(
        body, name=name, grid=(n // tm, nk),
        in_specs=[pl.BlockSpec((tm, tk), lambda i, j: (i, j)), pl.BlockSpec((D, tk), lambda i, j: (0, j)),
                  pl.BlockSpec((tm, D), lambda i, j: (i, 0)), pl.BlockSpec((1, D), lambda i, j: (0, 0)),
                  pl.BlockSpec((tm, D), lambda i, j: (i, 0))],
        out_specs=[pl.BlockSpec((tm, D), lambda i, j: (i, 0)), pl.BlockSpec((1, D), lambda i, j: (0, 0))],
        out_shape=[_sds((n, D)), _sds((1, D))], scratch=[pltpu.VMEM((tm, D), f32)], sem=("arbitrary", "arbitrary"),
        ride=ride,
    )(dy, w, x2, g, dres)


def _matmul_tn(a, g, name, ride=None):
    n, k = a.shape
    m = g.shape[1]
    tr = min(1024, n)
    tk = k if k <= 1024 else 1408
    tn = m if m <= 1024 else (1024 if m % 1024 == 0 else 1408)
    nr = n // tr

    def body(a_ref, g_ref, o_ref):
        @pl.when(pl.program_id(2) == 0)
        def _():
            o_ref[...] = jnp.zeros_like(o_ref)

        o_ref[...] += _bdot(a_ref[...], g_ref[...], TN)

    return _pc(
        body, name=name, grid=(k // tk, m // tn, nr),
        in_specs=[pl.BlockSpec((tr, tk), lambda i, j, r: (r, i)), pl.BlockSpec((tr, tn), lambda i, j, r: (r, j))],
        out_specs=pl.BlockSpec((tk, tn), lambda i, j, r: (i, j)),
        out_shape=_sds((k, m)), sem=("parallel", "parallel", "arbitrary"), ride=ride,
    )(a, g)


def _f_mla_proj(ckv, slab, pq, c, s, qg, kg, wq, wk, wv):
    c8, s8 = jnp.concatenate([c] * HEADS, axis=1), jnp.concatenate([s] * HEADS, axis=1)
    q = _rope(_mm(_rms(pq, qg), wq), c8, s8)
    cn = _rms(ckv, kg)
    k = _mm(cn, wk) + jnp.concatenate([_rope(slab, c, s)] * HEADS, axis=1)
    return q, k, _mm(cn, wv)


def _attn_fwd(q, k, v, bsz, seq, name, ride=None):
    n = q.shape[0]
    tq = min(256, seq)
    nq = seq // tq

    def body(q_ref, k_ref, v_ref, o_ref, lse_ref):
        lane = lax.broadcasted_iota(jnp.int32, (tq, 128), 1) < 64
        vv = v_ref[...]
        two = range(2)
        s = [_dot(q_ref[:, h * 128:(h + 1) * 128], k_ref[:, h * 128:(h + 1) * 128], NT) * ATT_SCALE for h in two]
        m = [jnp.max(s[h], axis=-1, keepdims=True) for h in two]
        e = [jnp.exp(s[h] - m[h]) for h in two]
        l = [jnp.sum(e[h], axis=-1, keepdims=True) for h in two]
        p = [(e[h] / l[h]).astype(bf16) for h in two]
        outs = [_dot(p[h], vv, NN) for h in two]
        o_ref[...] = jnp.where(lane, outs[0], outs[1])
        lse_ref[...] = jnp.where(lane, m[0] + jnp.log(l[0]), m[1] + jnp.log(l[1]))

    row = pl.BlockSpec((tq, 128), lambda b, h, i: (b * nq + i, h))
    return _pc(
        body, name=name, grid=(bsz, HEADS // 2, nq),
        in_specs=[pl.BlockSpec((tq, 256), lambda b, h, i: (b * nq + i, h)),
                  pl.BlockSpec((seq, 256), lambda b, h, i: (b, h)),
                  pl.BlockSpec((seq, 128), lambda b, h, i: (b, h))],
        out_specs=[row, row],
        out_shape=[_sds((n, HEADS * V_HEAD)), _sds((n, HEADS * V_HEAD))], sem=("parallel", "parallel", "parallel"),
        ride=ride,
    )(q, k, v)


def _attn_bwd(q, k, v, o, lse, do, bsz, seq, name, ride=None):
    n = q.shape[0]
    tq = min(256, seq)
    nq = seq // tq

    def body(q_ref, k_ref, v_ref, o_ref, lse_ref, do_ref, dq_ref, dk_ref, dv_ref):
        @pl.when(pl.program_id(2) == 0)
        def _():
            dk_ref[...] = jnp.zeros_like(dk_ref)
            dv_ref[...] = jnp.zeros_like(dv_ref)

        lane = lax.broadcasted_iota(jnp.int32, (tq, 128), 1) < 64
        vv = v_ref[...]
        for h in range(2):
            qh, kh = q_ref[:, h * 128:(h + 1) * 128], k_ref[:, h * 128:(h + 1) * 128]
            p = jnp.exp(_dot(qh, kh, NT) * ATT_SCALE - lse_ref[:, 64 * h:64 * h + 1])
            doh = jnp.where(lane if h == 0 else jnp.logical_not(lane), do_ref[...], 0.0)
            delta = jnp.sum(doh * o_ref[...], axis=-1, keepdims=True)
            dob = doh.astype(bf16)
            dp = _dot(dob, vv, NT)
            ds = (p * (dp - delta) * ATT_SCALE).astype(bf16)
            dq_ref[:, h * 128:(h + 1) * 128] = _dot(ds, kh, NN)
            dk_ref[:, h * 128:(h + 1) * 128] += _dot(ds, qh, TN)
            dv_ref[...] += _dot(p.astype(bf16), dob, TN)

    return _pc(
        body, name=name, grid=(bsz, HEADS // 2, nq),
        in_specs=[pl.BlockSpec((tq, 256), lambda b, h, i: (b * nq + i, h)),
                  pl.BlockSpec((seq, 256), lambda b, h, i: (b, h)),
                  pl.BlockSpec((seq, 128), lambda b, h, i: (b, h)),
                  pl.BlockSpec((tq, 128), lambda b, h, i: (b * nq + i, h)),
                  pl.BlockSpec((tq, 128), lambda b, h, i: (b * nq + i, h)),
                  pl.BlockSpec((tq, 128), lambda b, h, i: (b * nq + i, h))],
        out_specs=[pl.BlockSpec((tq, 256), lambda b, h, i: (b * nq + i, h)),
                   pl.BlockSpec((seq, 256), lambda b, h, i: (b, h)),
                   pl.BlockSpec((seq, 128), lambda b, h, i: (b, h))],
        out_shape=[_sds((n, HEADS * 128)), _sds((n, HEADS * 128)), _sds((n, HEADS * V_HEAD))],
        sem=("parallel", "parallel", "arbitrary"), ride=ride,
    )(q, k, v, o, lse, do)


@jax.custom_vjp
def _group_mix(vv, *ws):
    lane = lax.broadcasted_iota(jnp.int32, (SG_CHUNK, 128), 1) < 64
    slabs = []
    for j in range(SG_GROUPS // 2):
        vp = vv[:, 128 * j:128 * (j + 1)]
        slabs.append(jnp.where(lane, _bdot(ws[2 * j], vp), _bdot(ws[2 * j + 1], vp)))
    return jnp.concatenate(slabs, axis=1)


def _group_mix_bwd(res, g):
    vv, ws = res
    lane = lax.broadcasted_iota(jnp.int32, (SG_CHUNK, 128), 1) < 64
    dvs, dws = [], []
    for j in range(SG_GROUPS // 2):
        vp, gp = vv[:, 128 * j:128 * (j + 1)], g[:, 128 * j:128 * (j + 1)]
        lo, hi = jnp.where(lane, gp, 0.0), jnp.where(lane, 0.0, gp)
        dvs.append(_bdot(ws[2 * j], lo, TN) + _bdot(ws[2 * j + 1], hi, TN))
        dws += [_bdot(lo, vp, NT), _bdot(hi, vp, NT)]
    return (jnp.concatenate(dvs, axis=1), *dws)


_group_mix.defvjp(lambda vv, *ws: (_group_mix(vv, *ws), (vv, ws)), _group_mix_bwd)


def _f_sg(pu, pv, lg, lb, bias, *ws):
    u, vv = _gelu(pu), _gelu(pv)
    mu = jnp.mean(vv, axis=-1, keepdims=True)
    d = vv - mu
    vv = d * lax.rsqrt(jnp.mean(d * d, axis=-1, keepdims=True) + LN_EPS) * lg + lb
    return (u * (bias + _group_mix(vv, *ws)),)


def _shift_mean(a, prev_row, next_row):
    t = a.shape[0]
    row = lax.broadcasted_iota(jnp.int32, a.shape, 0)
    prev = jnp.where(row == 0, prev_row, pltpu.roll(a, 1, 0))
    nxt = jnp.where(row == t - 1, next_row, pltpu.roll(a, t - 1, 0))
    return 0.5 * (prev + nxt)


def _halo_specs(tm, width, blk, nblk8):
    h = tm // 8
    return [pl.BlockSpec((tm, width), lambda i: (i, blk)),
            pl.BlockSpec((8, width), lambda i: (jnp.maximum(i * h - 1, 0), blk)),
            pl.BlockSpec((8, width), lambda i: (jnp.minimum((i + 1) * h, nblk8 - 1), blk))]


def _edge_rows(i, tm, seq, pv_ref, nx_ref, scale=None):
    first = (i * tm) % seq == 0
    last = ((i + 1) * tm) % seq == 0
    pv, nx = pv_ref[7:8, :], nx_ref[0:1, :]
    if scale is not None:
        pv, nx = pv * scale, nx * scale
    return jnp.where(first, 0.0, pv), jnp.where(last, 0.0, nx)


def _shift_fwd(p, mu, seq, name):
    n = p.shape[0]
    tm = min(256, seq)
    blk = O_RW // RW_W

    def body(x_ref, pv_ref, nx_ref, mu_ref, z_ref):
        x = x_ref[...]
        pv, nx = _edge_rows(pl.program_id(0), tm, seq, pv_ref, nx_ref)
        z_ref[...] = x + mu_ref[...] * (_shift_mean(x, pv, nx) - x)

    return _pc(
        body, name=name, grid=(n // tm,),
        in_specs=_halo_specs(tm, RW_W, blk, n // 8) + [pl.BlockSpec((1, RW_W), lambda i: (0, 0))],
        out_specs=pl.BlockSpec((tm, RW_W), lambda i: (i, 0)), out_shape=_sds((n, RW_W)), sem=("parallel",),
    )(p, p, p, mu)


def _shift_bwd(dz, p, mu, seq, name):
    n = p.shape[0]
    tm = min(256, seq)
    blk = O_RW // RW_W

    def body(dz_ref, dpv_ref, dnx_ref, x_ref, pv_ref, nx_ref, mu_ref, dx_ref, dmu_ref):
        i = pl.program_id(0)
        mu_v = mu_ref[...]
        dzv = dz_ref[...]
        m = dzv * mu_v
        mpv, mnx = _edge_rows(i, tm, seq, dpv_ref, dnx_ref, mu_v)
        dx_ref[...] = (dzv - m + _shift_mean(m, mpv, mnx)).astype(dx_ref.dtype)
        x = x_ref[...]
        pv, nx = _edge_rows(i, tm, seq, pv_ref, nx_ref)
        part = jnp.sum(dzv * (_shift_mean(x, pv, nx) - x), axis=0, keepdims=True)

        @pl.when(i == 0)
        def _():
            dmu_ref[...] = part

        @pl.when(i != 0)
        def _():
            dmu_ref[...] += part

    return _pc(
        body, name=name, grid=(n // tm,),
        in_specs=_halo_specs(tm, RW_W, 0, n // 8) + _halo_specs(tm, RW_W, blk, n // 8)
        + [pl.BlockSpec((1, RW_W), lambda i: (0, 0))],
        out_specs=[pl.BlockSpec((tm, RW_W), lambda i: (i, 0)), pl.BlockSpec((1, RW_W), lambda i: (0, 0))],
        out_shape=[_sds((n, RW_W), bf16), _sds((1, RW_W))], sem=("arbitrary",),
    )(dz, dz, dz, p, p, p, mu)


def _f_rw_pre(k, wl, al, gl, w0, a0, w2, a2, g2, k_k, k_a):
    w = w0 + _mm(jnp.tanh(wl), w2)
    lw = -jnp.exp(-_softplus(-w) - 0.5)
    a = _sigmoid(a0 + _mm(al, a2))
    g = _mm(_sigmoid(gl), g2)
    kkr = k * k_k
    kk = kkr / jnp.maximum(jnp.sqrt(_group_sum(kkr * kkr)), 1e-12)
    two = lambda t: jnp.concatenate([t, t], axis=1)
    kd = two(k) * (1.0 + (a - 1.0) * two(k_a))
    bd = two(kk) * a
    return lw, kd, kk, bd, g


def _f_rw_post(y0, y1, r, v, kd0, kd1, g, r_k, ln_g, ln_b):
    y = y0 + y1
    mean = _group_sum(y) * (1.0 / RW_HEAD)
    d = y - mean
    var = _group_sum(d * d) * (1.0 / RW_HEAD)
    yn = d * lax.rsqrt(var + GN_EPS) * ln_g + ln_b
    bonus = _group_sum(r * (kd0 + kd1) * r_k)
    return ((yn + bonus * v) * g,)


@jax.custom_vjp
def _tri_inv(mats):
    c = mats[0].shape[0]
    row = lax.broadcasted_iota(jnp.int32, (c, c), 0)
    col = lax.broadcasted_iota(jnp.int32, (c, c), 1)
    eye = (row == col).astype(f32)
    blk = lambda b: (row // b) == (col // b)
    ld = [jnp.where(blk(8), a, 0.0) for a in mats]
    l2 = [_bdot(x, x) for x in ld]
    l4 = [_bdot(x, x) for x in l2]
    t = [_bdot(eye - x, eye + y) for x, y in zip(ld, l2)]
    t = [_bdot(x, eye + y) for x, y in zip(t, l4)]
    b = 8
    while b < c:
        sub = blk(2 * b) & jnp.logical_not(blk(b))
        p = [_bdot(x, jnp.where(sub, a, 0.0)) for x, a in zip(t, mats)]
        t = [x - _bdot(y, x) for x, y in zip(t, p)]
        b *= 2
    return tuple(t)


def _tri_inv_fwd(mats):
    t = _tri_inv(mats)
    return t, t


def _tri_inv_bwd(ts, gs):
    p = [_bdot(t, g, TN) for t, g in zip(ts, gs)]
    return (tuple(-_bdot(x, t, NT) for x, t in zip(p, ts)),)


_tri_inv.defvjp(_tri_inv_fwd, _tri_inv_bwd)


@jax.custom_vjp
def _tri_inv_saved(mats, ts):
    return ts


_tri_inv_saved.defvjp(lambda mats, ts: (ts, ts),
                      lambda ts, gs: (_tri_inv_bwd(ts, gs)[0], tuple(jnp.zeros_like(t) for t in ts)))


def _split3(x):
    h = x.astype(bf16)
    r = x - h.astype(f32)
    m = r.astype(bf16)
    return h, m, (r - m.astype(f32)).astype(bf16)


@jax.custom_vjp
def _mask_mm(mask, x):
    mb = mask.astype(bf16)
    return _dot(jnp.concatenate([mb, mb, mb], axis=1), jnp.concatenate(_split3(x), axis=0), NN)


def _mask_mm_bwd(mask, g):
    mb = mask.astype(bf16)
    return jnp.zeros_like(mask), _dot(jnp.concatenate([mb, mb, mb], axis=0), jnp.concatenate(_split3(g), axis=0), TN)


_mask_mm.defvjp(lambda mask, x: (_mask_mm(mask, x), mask), _mask_mm_bwd)


@jax.custom_vjp
def _split_lanes(x):
    h = x.shape[1] // 2
    return x[:, :h], x[:, h:]


_split_lanes.defvjp(lambda x: (_split_lanes(x), None), lambda _, g: (jnp.concatenate(g, axis=1),))


def _scan_chunk(s0, r, v, kk, lw, kd, bd, rev, inv=None):
    n = len(r)
    each = range(n)
    c = r[0].shape[0]
    row = lax.broadcasted_iota(jnp.int32, (c, 2 * c), 0)
    col = lax.broadcasted_iota(jnp.int32, (c, 2 * c), 1) % c
    ahead = jnp.where(rev, col - row, row - col)
    before = ahead > 0
    incl = ahead >= 0
    lane = lax.broadcasted_iota(jnp.int32, (1, 128), 1)
    m0 = (lane < 64).astype(f32)
    heads = lambda t: jnp.concatenate([t * m0, t * (1.0 - m0)], axis=0)
    bd_mask = ((lax.broadcasted_iota(jnp.int32, (128, 128), 0) // 64)
               == (lax.broadcasted_iota(jnp.int32, (128, 128), 1) // 64)).astype(f32)
    tot = [jnp.sum(lw[i], axis=0, keepdims=True) for i in each]
    row1 = lax.broadcasted_iota(jnp.int32, (c, c), 0)
    col1 = lax.broadcasted_iota(jnp.int32, (c, c), 1)
    upto = (jnp.where(rev, col1 - row1, row1 - col1) >= 0).astype(f32)
    lp = [_mask_mm(upto, lw[i]) - 0.5 * tot[i] for i in each]
    eg = [jnp.exp(lp[i]) for i in each]
    ieg = [jnp.exp(-lp[i]) for i in each]
    rt = [r[i] * eg[i] for i in each]
    kt = [kd[i] * ieg[i] for i in each]
    bt = [bd[i] * ieg[i] for i in each]
    at = [kk[i] * jnp.exp(lp[i] - lw[i]) for i in each]
    etot = [jnp.exp(0.5 * tot[i]) for i in each]
    si = [s0[i] * etot[i] for i in each]
    bth = [heads(bt[i]) for i in each]
    kth = [heads(kt[i]) for i in each]
    vh = [heads(v[i]) for i in each]
    a_ab = [jnp.where(before, _mm_nt(at[i], bth[i]), 0.0) for i in each]
    a_ak = [jnp.where(before, _mm_nt(at[i], kth[i]), 0.0) for i in each]
    a_rb = [jnp.where(incl, _mm_nt(rt[i], bth[i]), 0.0) for i in each]
    a_rk = [jnp.where(incl, _mm_nt(rt[i], kth[i]), 0.0) for i in each]
    halves = [_split_lanes(a_ab[i]) for i in each]
    mats = tuple(m for pair in halves for m in pair)
    inv = _tri_inv(mats) if inv is None else _tri_inv_saved(mats, inv)
    t = [jnp.concatenate([inv[2 * i], inv[2 * i + 1]], axis=1) for i in each]
    x0 = [_mm_nt(at[i], si[i]) for i in each]
    x = [x0[i] + _mm(a_ak[i], vh[i]) for i in each]
    u = [-_mm(t[i], heads(x[i])) for i in each]
    y0 = [_mm_nt(rt[i], si[i]) for i in each]
    y = [y0[i] + _mm(jnp.concatenate([a_rb[i], a_rk[i]], axis=1), jnp.concatenate([heads(u[i]), vh[i]], axis=0))
         for i in each]
    ds = [_mm_tn(jnp.concatenate([u[i], v[i]], axis=0), jnp.concatenate([bt[i], kt[i]], axis=0)) for i in each]
    se = [(si[i] + ds[i] * bd_mask) * etot[i] for i in each]
    return tuple(y), tuple(se), inv


PAIRS = HEADS // 2


def _scan_specs(nc, bsz, flip=False):
    def cc(d, c):
        c = nc - 1 - c if flip else c
        return jnp.where(d == 0, c, nc - 1 - c)

    rowblk = lambda d, b, c: b * nc + cc(d, c)
    zspec = lambda blk: pl.BlockSpec((CHUNK, RW_DIM), lambda d, b, c: (rowblk(d, b, c), blk))
    dspec = pl.BlockSpec((CHUNK, RW_DIM), lambda d, b, c: (rowblk(d, b, c), d))
    yspec = pl.BlockSpec((None, CHUNK, RW_DIM), lambda d, b, c: (d, rowblk(d, b, c), 0))
    sspec = pl.BlockSpec((None, PAIRS, 128, 128), lambda d, b, c: ((d * bsz + b) * nc + cc(d, c), 0, 0, 0))
    tspec = pl.BlockSpec((None, PAIRS, 128, 256), lambda d, b, c: ((d * bsz + b) * nc + cc(d, c), 0, 0, 0))
    return zspec, dspec, yspec, sspec, tspec


def _scan_fwd(z, lw, kd, kk, bd, bsz, seq, name, ride=None):
    n = z.shape[0]
    nc = seq // CHUNK
    zspec, dspec, yspec, sspec, tspec = _scan_specs(nc, bsz)

    def body(r_ref, v_ref, kk_ref, lw_ref, kd_ref, bd_ref, y_ref, s_ref, t_ref, st):
        @pl.when(pl.program_id(2) == 0)
        def _():
            st[...] = jnp.zeros_like(st)

        rev = pl.program_id(0) == 1
        lanes = [slice(h * 128, (h + 1) * 128) for h in range(PAIRS)]
        s0 = tuple(st[h] for h in range(PAIRS))
        ops = [tuple(ref[:, ln] for ln in lanes) for ref in (r_ref, v_ref, kk_ref, lw_ref, kd_ref, bd_ref)]
        y, se, inv = _scan_chunk(s0, *ops, rev)
        for h, ln in enumerate(lanes):
            s_ref[h] = s0[h]
            t_ref[h, :, :128] = inv[2 * h]
            t_ref[h, :, 128:] = inv[2 * h + 1]
            y_ref[:, ln] = y[h]
            st[h] = se[h]

    return _pc(
        body, name=name, grid=(2, bsz, nc),
        in_specs=[zspec(0), zspec(2), zspec(0), dspec, dspec, dspec],
        out_specs=[yspec, sspec, tspec],
        out_shape=[_sds((2, n, RW_DIM)), _sds((2 * bsz * nc, PAIRS, 128, 128)), _sds((2 * bsz * nc, PAIRS, 128, 256))],
        scratch=[pltpu.VMEM((PAIRS, 128, 128), f32)], sem=("parallel", "parallel", "arbitrary"), ride=ride,
    )(z, z, kk, lw, kd, bd)


def _scan_bwd(z, lw, kd, kk, bd, s_in, t_in, dy, bsz, seq, name, ride=None):
    n = z.shape[0]
    nc = seq // CHUNK
    zspec, dspec, yspec, sspec, tspec = _scan_specs(nc, bsz, flip=True)

    def body(r_ref, v_ref, kk_ref, lw_ref, kd_ref, bd_ref, s_ref, t_ref, dy_ref,
             dr_ref, dv_ref, dkk_ref, dlw_ref, dkd_ref, dbd_ref, dst):
        @pl.when(pl.program_id(2) == 0)
        def _():
            dst[...] = jnp.zeros_like(dst)

        rev = pl.program_id(0) == 1
        lanes = [slice(h * 128, (h + 1) * 128) for h in range(PAIRS)]
        s0 = tuple(s_ref[h] for h in range(PAIRS))
        inv = tuple(t_ref[h, :, a * 128:(a + 1) * 128] for h in range(PAIRS) for a in range(2))
        ops = [tuple(ref[:, ln] for ln in lanes) for ref in (r_ref, v_ref, kk_ref, lw_ref, kd_ref, bd_ref)]
        cot = (tuple(dy_ref[:, ln] for ln in lanes), tuple(dst[h] for h in range(PAIRS)))
        _, vjp = jax.vjp(lambda *a: _scan_chunk(*a, rev=rev, inv=inv)[:2], s0, *ops)
        grads = vjp(cot)
        for h, ln in enumerate(lanes):
            dst[h] = grads[0][h]
            for o_ref, g in zip((dr_ref, dv_ref, dkk_ref, dlw_ref, dkd_ref, dbd_ref), grads[1:]):
                o_ref[:, ln] = g[h]

    return _pc(
        body, name=name, grid=(2, bsz, nc),
        in_specs=[zspec(0), zspec(2), zspec(0), dspec, dspec, dspec, sspec, tspec, zspec(0)],
        out_specs=[yspec, yspec, yspec, dspec, dspec, dspec],
        out_shape=[_sds((2, n, RW_DIM))] * 3 + [_sds((n, 2 * RW_DIM))] * 3,
        scratch=[pltpu.VMEM((PAIRS, 128, 128), f32)], sem=("parallel", "parallel", "arbitrary"), ride=ride,
    )(z, z, kk, lw, kd, bd, s_in, t_in, dy)


def _merge_fwd(x2, p, ya, yb, yc, gb, wb, wo, name):
    n = x2.shape[0]
    tm = min(256, n)

    def body(x_ref, pg_ref, ya_ref, yb_ref, yc_ref, gb_ref, wb_ref, wo_ref, o_ref):
        gates = _sigmoid(pg_ref[...] + gb_ref[...])
        merged = jnp.zeros((tm, D), f32)
        for k, y_ref in enumerate((ya_ref, yb_ref, yc_ref)):
            merged += gates[:, k * D:(k + 1) * D] * _bdot(y_ref[...], wb_ref[k])
        o_ref[...] = x_ref[...] + _bdot(merged, wo_ref[...])

    row = lambda w, b=0: pl.BlockSpec((tm, w), lambda i, b=b: (i, b))
    return _pc(
        body, name=name, grid=(n // tm,),
        in_specs=[row(D), row(3 * D, O_GATE // (3 * D)), row(512), row(512), row(512),
                  pl.BlockSpec((1, 3 * D), lambda i: (0, 0)), pl.BlockSpec((3, 512, D), lambda i: (0, 0, 0)),
                  pl.BlockSpec((D, D), lambda i: (0, 0))],
        out_specs=row(D), out_shape=_sds((n, D)), sem=("parallel",),
    )(x2, p, ya, yb, yc, gb, wb, wo)


def _merge_bwd(dx1, p, ya, yb, yc, gb, wb, wo, name):
    n = dx1.shape[0]
    tm = min(256, n)

    def body(dx_ref, pg_ref, ya_ref, yb_ref, yc_ref, gb_ref, wb_ref, wo_ref,
             dpg_ref, dya_ref, dyb_ref, dyc_ref, dt_ref, mg_ref, dgb_ref):
        gates = _sigmoid(pg_ref[...] + gb_ref[...])
        dmerged = _bdot(dx_ref[...], wo_ref[...], NT)
        merged = jnp.zeros((tm, D), f32)
        dpg = []
        for k, (y_ref, dy_ref) in enumerate(((ya_ref, dya_ref), (yb_ref, dyb_ref), (yc_ref, dyc_ref))):
            gk = gates[:, k * D:(k + 1) * D]
            tk = _bdot(y_ref[...], wb_ref[k])
            merged += gk * tk
            dpg.append(dmerged * tk * gk * (1.0 - gk))
            dtk = dmerged * gk
            dt_ref[:, k * D:(k + 1) * D] = dtk.astype(bf16)
            dy_ref[...] = _bdot(dtk, wb_ref[k], NT)
        dpg = jnp.concatenate(dpg, axis=1)
        dpg_ref[...] = dpg.astype(bf16)
        mg_ref[...] = merged.astype(bf16)
        part = jnp.sum(dpg, axis=0, keepdims=True)

        @pl.when(pl.program_id(0) == 0)
        def _():
            dgb_ref[...] = part

        @pl.when(pl.program_id(0) != 0)
        def _():
            dgb_ref[...] += part

    row = lambda w, b=0: pl.BlockSpec((tm, w), lambda i, b=b: (i, b))
    return _pc(
        body, name=name, grid=(n // tm,),
        in_specs=[row(D), row(3 * D, O_GATE // (3 * D)), row(512), row(512), row(512),
                  pl.BlockSpec((1, 3 * D), lambda i: (0, 0)), pl.BlockSpec((3, 512, D), lambda i: (0, 0, 0)),
                  pl.BlockSpec((D, D), lambda i: (0, 0))],
        out_specs=[row(3 * D), row(512), row(512), row(512), row(3 * D), row(D),
                   pl.BlockSpec((1, 3 * D), lambda i: (0, 0))],
        out_shape=[_sds((n, 3 * D), bf16), _sds((n, 512)), _sds((n, 512)), _sds((n, 512)), _sds((n, 3 * D), bf16),
                   _sds((n, D), bf16), _sds((1, 3 * D))],
        sem=("arbitrary",),
    )(dx1, p, ya, yb, yc, gb, wb, wo)


FF_T = 1408


def _ffn_fwd(x1, g, wg, wu, wd, name):
    n = x1.shape[0]
    tm = min(512, n)
    nf = D_FF // FF_T

    def body(x_ref, g_ref, wg_ref, wu_ref, wd_ref, o_ref, hs):
        j = pl.program_id(1)

        @pl.when(j == 0)
        def _():
            hs[...] = _rms(x_ref[...], g_ref[...]).astype(bf16)
            o_ref[...] = x_ref[...]

        a = _dot(hs[...], wg_ref[...], NN)
        b = _dot(hs[...], wu_ref[...], NN)
        o_ref[...] += _bdot(a * _sigmoid(a) * b, wd_ref[...])

    return _pc(
        body, name=name, grid=(n // tm, nf),
        in_specs=[pl.BlockSpec((tm, D), lambda i, j: (i, 0)), pl.BlockSpec((1, D), lambda i, j: (0, 0)),
                  pl.BlockSpec((D, FF_T), lambda i, j: (0, j)), pl.BlockSpec((D, FF_T), lambda i, j: (0, j)),
                  pl.BlockSpec((FF_T, D), lambda i, j: (j, 0))],
        out_specs=pl.BlockSpec((tm, D), lambda i, j: (i, 0)), out_shape=_sds((n, D)),
        scratch=[pltpu.VMEM((tm, D), bf16)], sem=("parallel", "arbitrary"),
    )(x1, g, wg, wu, wd)


def _ffn_bwd(dx2, x1, g, wg, wu, wd, name):
    n = x1.shape[0]
    tm = min(512, n)
    nf = D_FF // FF_T

    def body(dx_ref, x_ref, g_ref, wg_ref, wu_ref, wd_ref, dx1_ref, dg_ref, h_ref, da_ref, db_ref, hm_ref, acc):
        i, j = pl.program_id(0), pl.program_id(1)

        @pl.when(j == 0)
        def _():
            h_ref[...] = _rms(x_ref[...], g_ref[...]).astype(bf16)
            acc[...] = jnp.zeros_like(acc)

        @pl.when((i == 0) & (j == 0))
        def _():
            dg_ref[...] = jnp.zeros_like(dg_ref)

        h = h_ref[...]
        a = _dot(h, wg_ref[...], NN)
        b = _dot(h, wu_ref[...], NN)
        sg = _sigmoid(a)
        s = a * sg
        dhm = _bdot(dx_ref[...], wd_ref[...], NT)
        da = (dhm * b * (sg * (1.0 + a * (1.0 - sg)))).astype(bf16)
        db = (dhm * s).astype(bf16)
        da_ref[...] = da
        db_ref[...] = db
        hm_ref[...] = (s * b).astype(bf16)
        acc[...] += _dot(da, wg_ref[...], NT) + _dot(db, wu_ref[...], NT)

        @pl.when(j == nf - 1)
        def _():
            _, vjp = jax.vjp(_rms, x_ref[...], g_ref[...])
            dx, dg = vjp(acc[...])
            dx1_ref[...] = dx_ref[...] + dx
            dg_ref[...] += dg

    rowf = pl.BlockSpec((tm, FF_T), lambda i, j: (i, j))
    rowd = pl.BlockSpec((tm, D), lambda i, j: (i, 0))
    vec = pl.BlockSpec((1, D), lambda i, j: (0, 0))
    return _pc(
        body, name=name, grid=(n // tm, nf),
        in_specs=[rowd, rowd, vec, pl.BlockSpec((D, FF_T), lambda i, j: (0, j)),
                  pl.BlockSpec((D, FF_T), lambda i, j: (0, j)), pl.BlockSpec((FF_T, D), lambda i, j: (j, 0))],
        out_specs=[rowd, vec, rowd, rowf, rowf, rowf],
        out_shape=[_sds((n, D)), _sds((1, D)), _sds((n, D), bf16), _sds((n, D_FF), bf16), _sds((n, D_FF), bf16),
                   _sds((n, D_FF), bf16)],
        scratch=[pltpu.VMEM((tm, D), f32)], sem=("arbitrary", "arbitrary"),
    )(dx2, x1, g, wg, wu, wd)


def _loss_head(x2, g, tgt, name):
    n = x2.shape[0]
    tm = min(512, n)

    def f(x, gg, t):
        e = _rms(x, gg) - t
        return 0.5 * jnp.sum(jnp.mean(e * e, axis=-1, keepdims=True))

    def body(x_ref, g_ref, t_ref, l_ref, dx_ref, dg_ref):
        val, vjp = jax.vjp(f, x_ref[...], g_ref[...], t_ref[...])
        dx, dg, _ = vjp(jnp.ones((), f32))
        dx_ref[...] = dx

        @pl.when(pl.program_id(0) == 0)
        def _():
            l_ref[...] = jnp.zeros_like(l_ref)
            dg_ref[...] = jnp.zeros_like(dg_ref)

        l_ref[...] += val
        dg_ref[...] += dg

    rowd = pl.BlockSpec((tm, D), lambda i: (i, 0))
    return _pc(
        body, name=name, grid=(n // tm,),
        in_specs=[rowd, pl.BlockSpec((1, D), lambda i: (0, 0)), rowd],
        out_specs=[pl.BlockSpec((8, 128), lambda i: (0, 0)), rowd, pl.BlockSpec((1, D), lambda i: (0, 0))],
        out_shape=[_sds((8, 128)), _sds((n, D)), _sds((1, D))], sem=("arbitrary",),
    )(x2, g, tgt)


def _adamw(w, parts, m, v, name):
    nl, r, c = w.shape
    tr = r
    for cand in (1024, 512, 256, 128, 64, 32, 16, 8):
        if r % cand == 0 and cand * c * 4 <= 1024 * 1024:
            tr = cand
            break

    def body(*refs):
        w_ref, p_refs, (m_ref, v_ref, g_ref, d_ref, nm_ref, nv_ref) = refs[0], refs[1:1 + nl], refs[1 + nl:]

        def update(p_ref):
            gg = p_ref[0].astype(f32)
            for k in range(1, N_DEV):
                gg = gg + p_ref[k].astype(f32)
            g_ref[...] = gg
            nm = B1 * m_ref[...] + (1.0 - B1) * gg
            nv = B2 * v_ref[...] + (1.0 - B2) * (gg * gg)
            m_hat = nm / (1.0 - B1 ** STEP)
            v_hat = nv / (1.0 - B2 ** STEP)
            d_ref[...] = -LR * (m_hat / (jnp.sqrt(v_hat) + EPS) + WD * w_ref[...])
            nm_ref[...] = nm
            nv_ref[...] = nv

        for j in range(nl):
            pl.when(pl.program_id(0) == j)(functools.partial(update, p_refs[j]))

    spec = pl.BlockSpec((None, tr, c), lambda l, i: (l, i, 0))
    pspecs = [pl.BlockSpec((N_DEV, tr, c), lambda l, i, j=j: (0, jnp.where(l == j, i, 0), 0)) for j in range(nl)]
    return _pc(body, name=name, grid=(nl, r // tr), in_specs=[spec] + pspecs + [spec, spec], out_specs=[spec] * 4,
               out_shape=[_sds((nl, r, c))] * 4, sem=("arbitrary", "arbitrary"))(w, *parts, m, v)


def _peers():
    x, y, c = lax.axis_index("x"), lax.axis_index("y"), lax.axis_index("c")
    me = 4 * x + 2 * y + c
    peers = []
    for k in range(1, N_DEV):
        fx, fy, fc = (k >> 2) & 1, (k >> 1) & 1, k & 1
        peers.append(((1 - x) if fx else x, (1 - y) if fy else y, (1 - c) if fc else c))
    return me, peers


def _exchange(gathers, scatters, name):
    _, got = _pc(lambda: None, name=name, out_shape=[], ride=(gathers, scatters))()
    return got


SHARDED = {"w_in": 2, "gate_b": 2, "w_uq": 2, "w_ukv": 2, "rw_w0": 2, "rw_w2": 3, "rw_a0": 2, "rw_a2": 3, "rw_g2": 2,
           "w_branch": 3, "w_out": 1, "w_ffn_gate": 2, "w_ffn_up": 2, "w_ffn_down": 1}
GATHER_F32 = ("gate_b", "rw_w0", "rw_a0")
REPLICATED = ("attn_norm_g", "q_norm_g", "kv_norm_g", "sg_ln_g", "sg_ln_b", "sg_w", "sg_b", "rw_mu", "rw_k_k", "rw_k_a",
              "rw_r_k", "rw_ln_g", "rw_ln_b", "ffn_norm_g", "final_norm_g")
WEIGHTS = ("attn_norm_g", "w_in", "gate_b", "q_norm_g", "w_uq", "kv_norm_g", "w_ukv", "sg_ln_g", "sg_ln_b", "sg_w", "sg_b",
           "rw_mu", "rw_w0", "rw_w2", "rw_a0", "rw_a2", "rw_g2", "rw_k_k", "rw_k_a", "rw_r_k", "rw_ln_g", "rw_ln_b",
           "w_branch", "w_out", "ffn_norm_g", "w_ffn_gate", "w_ffn_up", "w_ffn_down", "final_norm_g")


REP_MAIN = tuple(k for k in REPLICATED if k not in ("attn_norm_g", "sg_w"))
BIG = ("w_in", "w_branch", "w_out", "w_ffn_gate", "w_ffn_up", "w_ffn_down")
SMALL_BF = ("w_uq", "w_ukv", "rw_w2", "rw_a2", "rw_g2")
SMALL = SMALL_BF + GATHER_F32


def _pack128(blocks, names, dtype, lead=0, to=256):
    parts = [blocks[k].astype(dtype).reshape(blocks[k].shape[:lead] + (-1, 128)) for k in names]
    rows = sum(p.shape[lead] for p in parts)
    pad = -rows % to
    if pad:
        parts.append(jnp.zeros(parts[0].shape[:lead] + (pad, 128), dtype))
    return jnp.concatenate(parts, axis=lead)


def _unpack128(packed, shapes, names, lead=0):
    out, off = {}, 0
    for k in names:
        rows = 1
        for d in shapes[k]:
            rows *= d
        rows //= 128
        idx = (slice(None),) * lead + (slice(off, off + rows),)
        out[k] = packed[idx].reshape(packed.shape[:lead] + tuple(shapes[k]))
        off += rows
    return out


def _join_blocks(g, ax):
    shp = g.shape[1:]
    return jnp.moveaxis(g, 0, ax).reshape(shp[:ax] + (N_DEV * shp[ax],) + shp[ax + 1:])


def _split_blocks(full, ax):
    shp = full.shape
    return jnp.moveaxis(full.reshape(shp[:ax] + (N_DEV, shp[ax] // N_DEV) + shp[ax + 1:]), ax, 0)


def _w_in_padded(w):
    z = lambda n: jnp.zeros((w.shape[0], n), w.dtype)
    q, ckv, kr = w[:, 0:384], w[:, 384:640], w[:, 640:672]
    sg, rw, gate = w[:, 672:1696], w[:, 1696:3616], w[:, 3616:6688]
    return jnp.concatenate([gate, sg, rw, z(128), ckv, z(64), kr, z(32), q, z(P_W - O_MLA - MLA_W)], axis=1)


def _w_in_unpadded(g):
    return jnp.concatenate([g[:, O_Q:O_Q + 384], g[:, O_CKV:O_CKV + 256], g[:, O_SLAB + 64:O_SLAB + 96],
                            g[:, O_SG:O_SG + 1024], g[:, O_RW:O_RW + 1920], g[:, O_GATE:O_GATE + 3072]], axis=1)


REST = ("w_branch", "w_out", "w_ffn_gate", "w_ffn_up", "w_ffn_down")


def _rest_weights(full, l):
    return dict(wb=full["w_branch"][l], wo=full["w_out"][l], wg=full["w_ffn_gate"][l], wu=full["w_ffn_up"][l],
                wd=full["w_ffn_down"][l])


def _layer_weights(full, rep, l):
    w = {}
    w["w_in"] = _w_in_padded(full["w_in"][l])
    if full["w_branch"][l] is not None:
        w.update(_rest_weights(full, l))
    uq = full["w_uq"][l].reshape(Q_LORA, HEADS, QK_NOPE + QK_ROPE)
    w["wq"] = jnp.pad(uq, ((0, 0), (0, 0), (0, 32))).reshape(Q_LORA, HEADS * 128).astype(f32)
    ukv = full["w_ukv"][l].reshape(KV_LORA, HEADS, QK_NOPE + V_HEAD)
    wk = jnp.pad(ukv[:, :, :QK_NOPE], ((0, 0), (0, 0), (0, 64))).reshape(KV_LORA, HEADS * 128)
    w["wk"], w["wv"] = wk.astype(f32), ukv[:, :, QK_NOPE:].reshape(KV_LORA, HEADS * V_HEAD).astype(f32)
    bdiag = lambda t: jnp.concatenate([jnp.concatenate([t[0], jnp.zeros_like(t[0])], axis=1),
                                       jnp.concatenate([jnp.zeros_like(t[1]), t[1]], axis=1)], axis=0).astype(f32)
    w["w2"], w["a2"] = bdiag(full["rw_w2"][l]), bdiag(full["rw_a2"][l])
    w["g2"] = full["rw_g2"][l].astype(f32)
    w["w0"], w["a0"] = full["rw_w0"][l].reshape(1, 2 * RW_DIM), full["rw_a0"][l].reshape(1, 2 * RW_DIM)
    w["gate_b"] = full["gate_b"][l].reshape(1, 3 * D)
    row = lambda a: a.reshape(1, -1)
    for k in ("attn_norm_g", "q_norm_g", "kv_norm_g", "sg_ln_g", "sg_ln_b", "rw_k_k", "rw_k_a", "rw_ln_g", "rw_ln_b",
              "ffn_norm_g"):
        w[k] = row(rep[k][l])
    w["r_k"] = row(rep["rw_r_k"][l])
    w["mu"] = jnp.pad(row(rep["rw_mu"][l]), ((0, 0), (0, RW_W - 1920)))
    w["sg_w"] = [rep["sg_w"][l, k] for k in range(SG_GROUPS)]
    w["sg_bias"] = jnp.repeat(rep["sg_b"][l].T, SG_DIM // SG_GROUPS, axis=1)
    return w


def _riding(res, ride, got, key):
    if ride is None:
        return res
    got[key] = res[1]
    return res[0]


def _layer_fwd(x2, w, tabs, bsz, seq, l, rides=None, on_gathered=None):
    nm = lambda s: f"l{l}_{s}"
    n = x2.shape[0]
    tm = min(256, n)
    rides = rides or {}
    ride = lambda key: (rides[key], []) if key in rides else None
    got = {}
    p, h = _riding(_inproj_fwd(x2, w["attn_norm_g"], w["w_in"], nm("inproj"), ride("inproj")), ride("inproj"), got, "inproj")
    mla_rows = [(p, 256, O_CKV // 256), (p, 128, O_SLAB // 128), (p, 384, O_Q // 384), (tabs[0], 128, 0), (tabs[1], 128, 0)]
    mla_w = [w["q_norm_g"], w["kv_norm_g"], w["wq"], w["wk"], w["wv"]]
    q, k, v = _rowwise_fwd(nm("mla_proj"), _f_mla_proj, mla_rows, mla_w, [(1024, bf16), (1024, bf16), (512, bf16)], tm)
    ya, lse = _riding(_attn_fwd(q, k, v, bsz, seq, nm("attn"), ride("attn")), ride("attn"), got, "attn")
    sg_rows = [(p, SG_DIM, O_SG // SG_DIM), (p, SG_DIM, O_SG // SG_DIM + 1)]
    sg_w = [w["sg_ln_g"], w["sg_ln_b"], w["sg_bias"]] + w["sg_w"]
    (yb,) = _rowwise_fwd(nm("sg"), _f_sg, sg_rows, sg_w, [(SG_DIM, f32)], SG_CHUNK)
    z = _shift_fwd(p, w["mu"], seq, nm("shift"))
    pre_rows = [(z, 512, 1), (z, 128, 12), (z, 128, 13), (z, 128, 14)]
    pre_w = [w["w0"], w["a0"], w["w2"], w["a2"], w["g2"], w["rw_k_k"], w["rw_k_a"]]
    lw, kd, kk, bd, g = _rowwise_fwd(nm("rw_pre"), _f_rw_pre, pre_rows, pre_w,
                                     [(1024, f32), (1024, f32), (512, f32), (1024, f32), (512, f32)], tm)
    y, s_in, t_in = _riding(_scan_fwd(z, lw, kd, kk, bd, bsz, seq, nm("scan"), ride("scan")), ride("scan"), got, "scan")
    post_rows = [(y[0], 512, 0), (y[1], 512, 0), (z, 512, 0), (z, 512, 2), (kd, 512, 0), (kd, 512, 1), (g, 512, 0)]
    post_w = [w["r_k"], w["rw_ln_g"], w["rw_ln_b"]]
    (yc,) = _rowwise_fwd(nm("rw_post"), _f_rw_post, post_rows, post_w, [(512, f32)], tm)
    if on_gathered is not None:
        w.update(on_gathered(got))
    x1 = _merge_fwd(x2, p, ya, yb, yc, w["gate_b"], w["wb"], w["wo"], nm("merge"))
    x3 = _ffn_fwd(x1, w["ffn_norm_g"], w["wg"], w["wu"], w["wd"], nm("ffn"))
    saved = dict(x=x2, p=p, h=h, q=q, k=k, v=v, ya=ya, lse=lse, yb=yb, z=z, lw=lw, kd=kd, kk=kk, bd=bd, g=g, y=y, s_in=s_in, t_in=t_in, yc=yc,
                 x1=x1, mla_rows=mla_rows, mla_w=mla_w, sg_rows=sg_rows, sg_w=sg_w, pre_rows=pre_rows, pre_w=pre_w,
                 post_rows=post_rows, post_w=post_w)
    return x3, saved, got


def _layer_bwd(dx3, w, sv, bsz, seq, l, rides=None):
    nm = lambda s: f"l{l}_{s}_bwd"
    n = dx3.shape[0]
    tm = min(256, n)
    g = {}
    rides = rides or {}
    ride = lambda key: rides[key](g) if key in rides else None
    got = {}
    dx1, g["ffn_norm_g"], h2, da, db, hm = _ffn_bwd(dx3, sv["x1"], w["ffn_norm_g"], w["wg"], w["wu"], w["wd"], nm("ffn"))
    g["wg"] = _matmul_tn(h2, da, nm("wg"))
    g["wu"] = _matmul_tn(h2, db, nm("wu"))
    g["wd"] = _matmul_tn(hm, dx3, nm("wd"))
    dpg, dya, dyb, dyc, dt, mg, g["gate_b"] = _merge_bwd(dx1, sv["p"], sv["ya"], sv["yb"], sv["yc"], w["gate_b"], w["wb"],
                                                         w["wo"], nm("merge"))
    g["wo"] = _matmul_tn(mg, dx1, nm("wo"))
    ys = (sv["ya"], sv["yb"], sv["yc"])
    g["wb"] = jnp.stack([_matmul_tn(ys[k], dt[:, k * D:(k + 1) * D], nm(f"wb{k}")) for k in range(3)])
    (dy, dr_p, dv_p, dkd0, dkd1, dg_), (g["r_k"], g["rw_ln_g"], g["rw_ln_b"]) = _rowwise_bwd(
        nm("rw_post"), _f_rw_post, sv["post_rows"], sv["post_w"], [(dyc, 512, 0)], tm, [f32, None] + [f32] * 5)
    dkd_p = jnp.concatenate([dkd0, dkd1], axis=1)
    rd = ride("scan")
    dr_s, dv_s, dkk_s, dlw, dkd_s, dbd = _riding(
        _scan_bwd(sv["z"], sv["lw"], sv["kd"], sv["kk"], sv["bd"], sv["s_in"], sv["t_in"], dy, bsz, seq, nm("scan"), rd),
        rd, got, "scan")
    pre_cots = [(dlw, 1024, 0), (dkd_s + dkd_p, 1024, 0), (dkk_s[0] + dkk_s[1], 512, 0), (dbd, 1024, 0), (dg_, 512, 0)]
    (dk, dwl, dal, dgl), (g["w0"], g["a0"], g["w2"], g["a2"], g["g2"], g["rw_k_k"], g["rw_k_a"]) = _rowwise_bwd(
        nm("rw_pre"), _f_rw_pre, sv["pre_rows"], sv["pre_w"], pre_cots, tm, [f32] * 4)
    dz = jnp.concatenate([dr_s[0] + dr_s[1] + dr_p, dk, dv_s[0] + dv_s[1] + dv_p, dwl, dal, dgl,
                          jnp.zeros((n, RW_W - 1920), f32)], axis=1)
    dp_rw, g["mu"] = _shift_bwd(dz, sv["p"], w["mu"], seq, nm("shift"))
    (dp_su, dp_sv), (g["sg_ln_g"], g["sg_ln_b"], g["sg_bias"], *sgw) = _rowwise_bwd(
        nm("sg"), _f_sg, sv["sg_rows"], sv["sg_w"], [(dyb, SG_DIM, 0)], SG_CHUNK, [bf16, bf16])
    g["sg_w"] = jnp.stack(sgw)
    rd = ride("attn")
    dq, dk_, dv_ = _riding(_attn_bwd(sv["q"], sv["k"], sv["v"], sv["ya"], sv["lse"], dya, bsz, seq, nm("attn"), rd), rd, got,
                           "attn")
    (dp_ckv, dp_slab, dp_q), (g["q_norm_g"], g["kv_norm_g"], g["wq"], g["wk"], g["wv"]) = _rowwise_bwd(
        nm("mla_proj"), _f_mla_proj, sv["mla_rows"], sv["mla_w"], [(dq, 1024, 0), (dk_, 1024, 0), (dv_, 512, 0)], tm,
        [bf16, bf16, bf16, None, None])
    dp = jnp.concatenate([dpg, dp_su, dp_sv, dp_rw, dp_ckv, dp_slab, dp_q, jnp.zeros((n, P_W - O_MLA - MLA_W), bf16)],
                         axis=1)
    rd = ride("w_in")
    g["w_in"] = _riding(_matmul_tn(sv["h"], dp, nm("w_in"), rd), rd, got, "w_in")
    rd = ride("inproj")
    dx, g["attn_norm_g"] = _riding(_norm_matmul_bwd(dp, w["w_in"], sv["x"], w["attn_norm_g"], dx1, nm("inproj"), rd), rd, got,
                                   "inproj")
    return dx, g, got


def _layer_grads_to_full(g):
    o = {}
    if "w_in" in g:
        o["w_in"] = _w_in_unpadded(g["w_in"])
    o["w_uq"] = g["wq"].reshape(Q_LORA, HEADS, 128)[:, :, :QK_NOPE + QK_ROPE].reshape(Q_LORA, -1)
    gk = g["wk"].reshape(KV_LORA, HEADS, 128)[:, :, :QK_NOPE]
    gv = g["wv"].reshape(KV_LORA, HEADS, V_HEAD)
    o["w_ukv"] = jnp.concatenate([gk, gv], axis=2).reshape(KV_LORA, -1)
    unb = lambda t: jnp.stack([t[:LORA, :RW_DIM], t[LORA:, RW_DIM:]])
    o["rw_w2"], o["rw_a2"], o["rw_g2"] = unb(g["w2"]), unb(g["a2"]), g["g2"]
    o["rw_w0"], o["rw_a0"] = g["w0"].reshape(2, RW_DIM), g["a0"].reshape(2, RW_DIM)
    o["gate_b"] = g["gate_b"].reshape(3, D)
    o["w_branch"], o["w_out"] = g["wb"], g["wo"]
    o["w_ffn_gate"], o["w_ffn_up"], o["w_ffn_down"] = g["wg"], g["wu"], g["wd"]
    for k in ("attn_norm_g", "q_norm_g", "kv_norm_g", "sg_ln_g", "sg_ln_b", "rw_k_k", "rw_k_a", "rw_ln_g", "rw_ln_b",
              "ffn_norm_g"):
        if k in g:
            o[k] = g[k].reshape(-1)
    o["rw_r_k"] = g["r_k"].reshape(HEADS, RW_HEAD)
    o["rw_mu"] = g["mu"].reshape(-1)[:1920]
    o["sg_w"] = g["sg_w"]
    o["sg_b"] = g["sg_bias"].reshape(SG_CHUNK, SG_GROUPS, SG_DIM // SG_GROUPS).sum(axis=2).T
    return o


def _rope_tables(positions):
    inv = 1.0 / (10000.0 ** (jnp.arange(0, QK_ROPE, 2, dtype=f32) / QK_ROPE))
    ang = positions.astype(f32)[:, None] * inv
    cos, sin = jnp.cos(ang), jnp.sin(ang)
    n = positions.shape[0]
    c = jnp.concatenate([jnp.ones((n, 64), f32), cos, cos, jnp.zeros((n, 32), f32)], axis=1)
    s = jnp.concatenate([jnp.zeros((n, 64), f32), -sin, sin, jnp.zeros((n, 32), f32)], axis=1)
    return c, s


def _grad_parts(grad, name):
    return _split_blocks(grad, SHARDED[name] - 1).astype(bf16)


def _local_step(x, positions, full, rep, loss_target, blocks=None):
    bsz, seq, _ = x.shape
    n = bsz * seq
    x2 = x.reshape(n, D)
    tabs = _rope_tables(positions.reshape(n))
    join = lambda k, g: _join_blocks(g, SHARDED[k] - 1)
    rides, on_gathered = None, None
    if blocks is not None:
        carried = {"inproj": [("w_branch", 0), ("w_out", 0)],
                   "attn": [("w_in", 1), ("w_ffn_gate", 0), ("w_ffn_up", 0)],
                   "scan": [("w_ffn_down", 0)] + [(k, 1) for k in REST]}
        rides = {key: [blocks[k][l] for k, l in what] for key, what in carried.items()}

        def on_gathered(got):
            for key, what in carried.items():
                for (k, l), g in zip(what, got[key]):
                    full[k][l] = join(k, g)
            return _rest_weights(full, 0)

    w0 = _layer_weights(full, rep, 0)
    x2, sv0, got = _layer_fwd(x2, w0, tabs, bsz, seq, 0, rides, on_gathered)
    w1 = _layer_weights(full, rep, 1)
    x2, sv1, _ = _layer_fwd(x2, w1, tabs, bsz, seq, 1)
    loss, dx, dgf = _loss_head(x2, rep["final_norm_g"].reshape(1, D), loss_target.reshape(n, D), "loss_head")
    dx, g1, _ = _layer_bwd(dx, w1, sv1, bsz, seq, 1)
    grads1 = _layer_grads_to_full(g1)
    rides = None
    if blocks is not None:
        short = dict(w_branch="wb", w_out="wo", w_ffn_gate="wg", w_ffn_up="wu", w_ffn_down="wd")

        def beside_w_in(g):
            g0 = _layer_grads_to_full(g)
            both = {k: jnp.stack([g0[k], grads1[k]]) for k in g0}
            both["final_norm_g"] = dgf.reshape(D)
            split = {k: _split_blocks(both[k], SHARDED[k]) for k in SMALL}
            return [_pack128(both, REP_MAIN, f32), both["sg_w"]], [_pack128(split, SMALL, f32, lead=1)]

        rides = {"scan": lambda g: ([], [_grad_parts(grads1[k], k) for k in BIG]),
                 "attn": lambda g: ([], [_grad_parts(g[short[k]], k) for k in REST]),
                 "w_in": beside_w_in,
                 "inproj": lambda g: ([], [_grad_parts(_w_in_unpadded(g["w_in"]), "w_in")])}
    dx, g0, got = _layer_bwd(dx, w0, sv0, bsz, seq, 0, rides)
    grads0 = _layer_grads_to_full(g0)
    grads = {k: [grads0[k], grads1[k]] for k in grads0}
    grads["final_norm_g"] = dgf.reshape(D)
    parts = {}
    if blocks is not None:
        parts = {k: [None, p] for k, p in zip(BIG, got["scan"])}
        for k, p in zip(REST, got["attn"]):
            parts[k][0] = p
        parts["replicated"], parts["sg_w"], parts["small"] = got["w_in"]
        (parts["w_in"][0],) = got["inproj"]
    return loss[0, 0], dx.reshape(bsz, seq, D), grads, parts


def kernel(x, positions, attn_norm_g, w_in, gate_b, q_norm_g, w_uq, kv_norm_g, w_ukv, sg_ln_g, sg_ln_b, sg_w, sg_b, rw_mu, rw_w0, rw_w2, rw_a0, rw_a2, rw_g2, rw_k_k, rw_k_a, rw_r_k, rw_ln_g, rw_ln_b, w_branch, w_out, ffn_norm_g, w_ffn_gate, w_ffn_up, w_ffn_down, final_norm_g, loss_target, m_attn_norm_g, m_w_in, m_gate_b, m_q_norm_g, m_w_uq, m_kv_norm_g, m_w_ukv, m_sg_ln_g, m_sg_ln_b, m_sg_w, m_sg_b, m_rw_mu, m_rw_w0, m_rw_w2, m_rw_a0, m_rw_a2, m_rw_g2, m_rw_k_k, m_rw_k_a, m_rw_r_k, m_rw_ln_g, m_rw_ln_b, m_w_branch, m_w_out, m_ffn_norm_g, m_w_ffn_gate, m_w_ffn_up, m_w_ffn_down, m_final_norm_g, v_attn_norm_g, v_w_in, v_gate_b, v_q_norm_g, v_w_uq, v_kv_norm_g, v_w_ukv, v_sg_ln_g, v_sg_ln_b, v_sg_w, v_sg_b, v_rw_mu, v_rw_w0, v_rw_w2, v_rw_a0, v_rw_a2, v_rw_g2, v_rw_k_k, v_rw_k_a, v_rw_r_k, v_rw_ln_g, v_rw_ln_b, v_w_branch, v_w_out, v_ffn_norm_g, v_w_ffn_gate, v_w_ffn_up, v_w_ffn_down, v_final_norm_g):
    args = locals()
    wts = {k: args[k] for k in WEIGHTS}
    mom_m = {k: args["m_" + k] for k in WEIGHTS}
    mom_v = {k: args["v_" + k] for k in WEIGHTS}
    shapes = {k: wts[k].shape for k in WEIGHTS}
    blocks = {k: wts[k].astype(bf16) for k in BIG}
    got = _exchange([blocks["w_in"][0], _pack128(wts, SMALL_BF, bf16), _pack128(wts, GATHER_F32, f32)], [], "gather_first")
    small = {**_unpack128(got[1], shapes, SMALL_BF, lead=1), **_unpack128(got[2], shapes, GATHER_F32, lead=1)}
    full = {k: list(_join_blocks(small[k], SHARDED[k])) for k in SMALL}
    full["w_in"] = [_join_blocks(got[0], SHARDED["w_in"] - 1), None]
    full.update({k: [None, None] for k in REST})
    rep = {k: wts[k] for k in REPLICATED}
    loss, grad_x, grads, parts = _local_step(x, positions, full, rep, loss_target, blocks)
    loss = lax.psum(loss, ("x", "y", "c"))
    last = ("attn_norm_g",)
    (last_parts,) = _exchange([_pack128({"attn_norm_g": jnp.stack(grads["attn_norm_g"])}, last, f32, to=16)], [],
                              "exchange_last")
    gw, delta, new_m, new_v = {}, {}, {}, {}
    for k in BIG:
        three = lambda a, k=k: a.reshape(a.shape[0], -1, shapes[k][-1])
        res = _adamw(three(wts[k]), [three(p) for p in parts[k]], three(mom_m[k]), three(mom_v[k]), f"adamw_{k}")
        gw[k], delta[k], new_m[k], new_v[k] = (t.reshape(shapes[k]) for t in res)
    rows = lambda a: a.reshape(1, -1, 128)
    res = _adamw(rows(wts["sg_w"]), [parts["sg_w"].reshape(N_DEV, -1, 128)], rows(mom_m["sg_w"]), rows(mom_v["sg_w"]),
                 "adamw_sg_w")
    gw["sg_w"], delta["sg_w"], new_m["sg_w"], new_v["sg_w"] = (t.reshape(shapes["sg_w"]) for t in res)
    for names, got, to in ((SMALL, parts["small"], 256), (REP_MAIN, parts["replicated"], 256), (last, last_parts, 16)):
        pk = lambda dct: _pack128(dct, names, f32, to=to)[None]
        res = _adamw(pk(wts), [got], pk(mom_m), pk(mom_v), f"adamw_{names[0]}")
        for dst, t in zip((gw, delta, new_m, new_v), res):
            dst.update(_unpack128(t[0], shapes, names))
    return (loss, grad_x, *[gw[k] for k in WEIGHTS], *[delta[k] for k in WEIGHTS], *[new_m[k] for k in WEIGHTS],
            *[new_v[k] for k in WEIGHTS])
```

```python
import functools

import jax
import jax.numpy as jnp
from jax import lax
from jax.experimental import pallas as pl
from jax.experimental.pallas import tpu as pltpu

f32 = jnp.float32
bf16 = jnp.bfloat16
HI = lax.Precision.HIGHEST
NN, NT, TN = ((1,), (0,)), ((1,), (1,)), ((0,), (0,))

N_DEV = 8
D = 1024
HEADS = 8
Q_LORA, KV_LORA, QK_NOPE, QK_ROPE, V_HEAD = 384, 256, 64, 32, 64
SG_DIM, SG_CHUNK, SG_GROUPS = 512, 128, 8
RW_DIM, RW_HEAD, LORA = 512, 64, 64
D_FF = 2816
N_IN = 6688
NORM_EPS, LN_EPS, GN_EPS = 1e-6, 1e-5, 64e-5
ATT_SCALE = (QK_NOPE + QK_ROPE) ** -0.5
P_W = 7168
O_GATE, O_SG, O_RW, O_MLA = 0, 3072, 4096, 6144
RW_W = 2048
MLA_W = 768
O_CKV, O_SLAB, O_Q = O_MLA, O_MLA + 256, O_MLA + 384
CHUNK = 128
VMEM_LIMIT = 56 * 1024 * 1024

B1, B2, LR, EPS, WD, STEP = 0.9, 0.999, 0.001, 1e-8, 0.01, 10


def _pc(body, *, name, out_shape, grid=(), in_specs=(), out_specs=(), scratch=(), sem=None, ride=None):
    params = pltpu.CompilerParams(dimension_semantics=sem, vmem_limit_bytes=VMEM_LIMIT)
    if ride is None:
        return pl.pallas_call(body, out_shape=out_shape, grid=grid, in_specs=in_specs, out_specs=out_specs,
                              scratch_shapes=scratch, compiler_params=params, name=name, interpret=False)
    gathers, scatters = ride
    moved = list(gathers) + list(scatters)
    ng, nx = len(gathers), len(moved)
    single = not isinstance(out_shape, (list, tuple))
    outs = [out_shape] if single else list(out_shape)
    ospecs = [out_specs] if single else list(out_specs)
    n_in, n_out, n_scr = len(in_specs), len(outs), len(scratch)
    per = N_DEV - 1

    def riding(*refs):
        ins, xin = refs[:n_in], refs[n_in:n_in + nx]
        outs_r, xout = refs[n_in + nx:n_in + nx + n_out], refs[n_in + nx + n_out:n_in + 2 * nx + n_out]
        own = refs[n_in + 2 * nx + n_out:n_in + 2 * nx + n_out + n_scr]
        send_sems, recv_sems, local_sems = refs[n_in + 2 * nx + n_out + n_scr:]

        def copies():
            me, peers = _peers()
            cps = []
            for a in range(nx):
                whole = a < ng
                cps.append(pltpu.make_async_copy(xin[a] if whole else xin[a].at[me], xout[a].at[me], local_sems.at[a]))
                for k, peer in enumerate(peers):
                    dev = 4 * peer[0] + 2 * peer[1] + peer[2]
                    cps.append(pltpu.make_async_remote_copy(
                        src_ref=xin[a] if whole else xin[a].at[dev], dst_ref=xout[a].at[me],
                        send_sem=send_sems.at[a * per + k], recv_sem=recv_sems.at[a * per + k], device_id=peer,
                        device_id_type=pl.DeviceIdType.MESH))
            return cps

        if not grid:
            for cp in copies():
                cp.start()
            body(*ins, *outs_r, *own)
            for cp in copies():
                cp.wait()
            return
        ids = [pl.program_id(a) for a in range(len(grid))]
        first = functools.reduce(jnp.logical_and, [i == 0 for i in ids])
        last = functools.reduce(jnp.logical_and, [i == g - 1 for i, g in zip(ids, grid)])

        @pl.when(first)
        def _():
            for cp in copies():
                cp.start()

        body(*ins, *outs_r, *own)

        @pl.when(last)
        def _():
            for cp in copies():
                cp.wait()

    anyspec = pl.BlockSpec(memory_space=pl.ANY)
    call = pl.pallas_call(
        riding, grid=grid, in_specs=list(in_specs) + [anyspec] * nx, out_specs=ospecs + [anyspec] * nx,
        out_shape=outs + [_sds((N_DEV,) + a.shape, a.dtype) for a in gathers] + [_sds(a.shape, a.dtype) for a in scatters],
        scratch_shapes=list(scratch) + [pltpu.SemaphoreType.DMA((nx * per,)), pltpu.SemaphoreType.DMA((nx * per,)),
                                        pltpu.SemaphoreType.DMA((nx,))],
        compiler_params=params, name=name, interpret=False)

    def run(*args):
        res = call(*args, *moved)
        own = res[0] if single else list(res[:n_out])
        return own, list(res[n_out:])

    return run


def _sds(shape, dtype=f32):
    return jax.ShapeDtypeStruct(tuple(shape), dtype)


def _dot(a, b, dims, precision=None):
    return lax.dot_general(a, b, (dims, ((), ())), preferred_element_type=f32, precision=precision)


def _bdot(a, b, dims=NN):
    return _dot(a.astype(bf16), b.astype(bf16), dims)


@jax.custom_vjp
def _mm(a, w):
    return _bdot(a, w, NN)


def _mm_fwd(a, w):
    return _bdot(a, w, NN), (a, w)


def _mm_bwd(res, g):
    a, w = res
    return _bdot(g, w, NT), _bdot(a, g, TN)


_mm.defvjp(_mm_fwd, _mm_bwd)


@jax.custom_vjp
def _mm_nt(a, b):
    return _bdot(a, b, NT)


def _mm_nt_fwd(a, b):
    return _bdot(a, b, NT), (a, b)


def _mm_nt_bwd(res, g):
    a, b = res
    return _bdot(g, b, NN), _bdot(g, a, TN)


_mm_nt.defvjp(_mm_nt_fwd, _mm_nt_bwd)


@jax.custom_vjp
def _mm_tn(a, b):
    return _bdot(a, b, TN)


def _mm_tn_fwd(a, b):
    return _bdot(a, b, TN), (a, b)


def _mm_tn_bwd(res, g):
    a, b = res
    return _bdot(b, g, NT), _bdot(a, g, NN)


_mm_tn.defvjp(_mm_tn_fwd, _mm_tn_bwd)


def _rms(x, g):
    return x * lax.rsqrt(jnp.mean(x * x, axis=-1, keepdims=True) + NORM_EPS) * g


def _sigmoid(x):
    return 1.0 / (1.0 + jnp.exp(-x))


def _gelu(x):
    return 0.5 * x * (1.0 + jnp.tanh(0.7978845608028654 * (x + 0.044715 * x * x * x)))


def _softplus(x):
    return jnp.maximum(x, 0.0) + jnp.log(1.0 + jnp.exp(-jnp.abs(x)))


@jax.custom_vjp
def _group_sum(x):
    w = x.shape[-1]
    r = lax.broadcasted_iota(jnp.int32, (w, w), 0) // RW_HEAD
    c = lax.broadcasted_iota(jnp.int32, (w, w), 1) // RW_HEAD
    ones = (r == c).astype(bf16)
    hi = x.astype(bf16)
    lo = (x - hi.astype(f32)).astype(bf16)
    return _dot(jnp.concatenate([hi, lo], axis=1), jnp.concatenate([ones, ones], axis=0), NN)


_group_sum.defvjp(lambda x: (_group_sum(x), None), lambda _, g: (_group_sum(g),))


@jax.custom_vjp
def _swap(x):
    w = x.shape[-1]
    lane = lax.broadcasted_iota(jnp.int32, x.shape, 1) % 128
    lo = (lane >= 64) & (lane < 80)
    hi = (lane >= 80) & (lane < 96)
    return jnp.where(lo, pltpu.roll(x, w - 16, 1), jnp.where(hi, pltpu.roll(x, 16, 1), 0.0))


_swap.defvjp(lambda x: (_swap(x), None), lambda _, g: (_swap(g),))


def _rope(x, c, s):
    return x * c + _swap(x) * s


def _row_spec(tm, width, blk):
    return pl.BlockSpec((tm, width), lambda i, blk=blk: (i, blk))


def _full_spec(a):
    nd = a.ndim
    return pl.BlockSpec(a.shape, lambda i, nd=nd: (0,) * nd)


def _rowwise_fwd(name, f, rows, weights, outs, tm):
    n = rows[0][0].shape[0]
    nr, nw = len(rows), len(weights)

    def body(*refs):
        vals = [r[...].astype(f32) for r in refs[:nr + nw]]
        res = f(*vals)
        for o_ref, o in zip(refs[nr + nw:], res):
            o_ref[...] = o.astype(o_ref.dtype)

    return _pc(
        body, name=name, grid=(n // tm,),
        in_specs=[_row_spec(tm, w, b) for _, w, b in rows] + [_full_spec(w) for w in weights],
        out_specs=[_row_spec(tm, w, 0) for w, _ in outs],
        out_shape=[_sds((n, w), dt) for w, dt in outs], sem=("parallel",),
    )(*[a for a, _, _ in rows], *weights)


def _rowwise_bwd(name, f, rows, weights, cots, tm, drows):
    n = rows[0][0].shape[0]
    nr, nw, nc = len(rows), len(weights), len(cots)
    want = [k for k, dt in enumerate(drows) if dt is not None]

    def body(*refs):
        vals = [r[...].astype(f32) for r in refs[:nr + nw]]
        cot = tuple(r[...].astype(f32) for r in refs[nr + nw:nr + nw + nc])
        _, vjp = jax.vjp(f, *vals)
        grads = vjp(cot)
        outs = refs[nr + nw + nc:]
        for o_ref, k in zip(outs[:len(want)], want):
            o_ref[...] = grads[k].astype(o_ref.dtype)
        first = pl.program_id(0) == 0
        for o_ref, g in zip(outs[len(want):], grads[nr:]):
            @pl.when(first)
            def _(o_ref=o_ref, g=g):
                o_ref[...] = g

            @pl.when(jnp.logical_not(first))
            def _(o_ref=o_ref, g=g):
                o_ref[...] += g

    res = _pc(
        body, name=name, grid=(n // tm,),
        in_specs=[_row_spec(tm, w, b) for _, w, b in rows] + [_full_spec(w) for w in weights]
        + [_row_spec(tm, w, b) for _, w, b in cots],
        out_specs=[_row_spec(tm, rows[k][1], 0) for k in want] + [_full_spec(w) for w in weights],
        out_shape=[_sds((n, rows[k][1]), drows[k]) for k in want] + [_sds(w.shape) for w in weights],
        sem=("arbitrary",),
    )(*[a for a, _, _ in rows], *weights, *[a for a, _, _ in cots])
    return res[:len(want)], res[len(want):]


def _inproj_fwd(x2, g, w, name, ride=None):
    n = x2.shape[0]
    tm, tn = min(1024, n), 1024

    def body(x_ref, g_ref, w_ref, p_ref, h_ref):
        @pl.when(pl.program_id(1) == 0)
        def _():
            h_ref[...] = _rms(x_ref[...], g_ref[...]).astype(bf16)

        p_ref[...] = jnp.dot(h_ref[...], w_ref[...], preferred_element_type=f32)

    return _pc(
        body, name=name, grid=(n // tm, P_W // tn),
        in_specs=[pl.BlockSpec((tm, D), lambda i, j: (i, 0)), pl.BlockSpec((1, D), lambda i, j: (0, 0)),
                  pl.BlockSpec((D, tn), lambda i, j: (0, j))],
        out_specs=[pl.BlockSpec((tm, tn), lambda i, j: (i, j)), pl.BlockSpec((tm, D), lambda i, j: (i, 0))],
        out_shape=[_sds((n, P_W)), _sds((n, D), bf16)], sem=("parallel", "arbitrary"), ride=ride,
    )(x2, g, w)


def _norm_matmul_bwd(dy, w, x2, g, dres, name, ride=None):
    n, k = dy.shape
    tm = min(1024, n)
    tk = 1024 if k % 1024 == 0 else 1408
    nk = k // tk

    def body(dy_ref, w_ref, x_ref, g_ref, dr_ref, dx_ref, dg_ref, acc):
        i, j = pl.program_id(0), pl.program_id(1)

        @pl.when(j == 0)
        def _():
            acc[...] = jnp.zeros_like(acc)

        @pl.when((i == 0) & (j == 0))
        def _():
            dg_ref[...] = jnp.zeros_like(dg_ref)

        acc[...] += _dot(dy_ref[...], w_ref[...], NT)

        @pl.when(j == nk - 1)
        def _():
            _, vjp = jax.vjp(_rms, x_ref[...], g_ref[...])
            dx, dg = vjp(acc[...])
            dx_ref[...] = dr_ref[...] + dx
            dg_ref[...] += dg

    return _pc(
        body, name=name, grid=(n // tm, nk),
        in_specs=[pl.BlockSpec((tm, tk), lambda i, j: (i, j)), pl.BlockSpec((D, tk), lambda i, j: (0, j)),
                  pl.BlockSpec((tm, D), lambda i, j: (i, 0)), pl.BlockSpec((1, D), lambda i, j: (0, 0)),
                  pl.BlockSpec((tm, D), lambda i, j: (i, 0))],
        out_specs=[pl.BlockSpec((tm, D), lambda i, j: (i, 0)), pl.BlockSpec((1, D), lambda i, j: (0, 0))],
        out_shape=[_sds((n, D)), _sds((1, D))], scratch=[pltpu.VMEM((tm, D), f32)], sem=("arbitrary", "arbitrary"),
        ride=ride,
    )(dy, w, x2, g, dres)


def _matmul_tn(a, g, name, ride=None):
    n, k = a.shape
    m = g.shape[1]
    tr = min(1024, n)
    tk = k if k <= 1024 else 1408
    tn = m if m <= 1024 else (1024 if m % 1024 == 0 else 1408)
    nr = n // tr

    def body(a_ref, g_ref, o_ref):
        @pl.when(pl.program_id(2) == 0)
        def _():
            o_ref[...] = jnp.zeros_like(o_ref)

        o_ref[...] += _bdot(a_ref[...], g_ref[...], TN)

    return _pc(
        body, name=name, grid=(k // tk, m // tn, nr),
        in_specs=[pl.BlockSpec((tr, tk), lambda i, j, r: (r, i)), pl.BlockSpec((tr, tn), lambda i, j, r: (r, j))],
        out_specs=pl.BlockSpec((tk, tn), lambda i, j, r: (i, j)),
        out_shape=_sds((k, m)), sem=("parallel", "parallel", "arbitrary"), ride=ride,
    )(a, g)


def _f_mla_proj(ckv, slab, pq, c, s, qg, kg, wq, wk, wv):
    c8, s8 = jnp.concatenate([c] * HEADS, axis=1), jnp.concatenate([s] * HEADS, axis=1)
    q = _rope(_mm(_rms(pq, qg), wq), c8, s8)
    cn = _rms(ckv, kg)
    k = _mm(cn, wk) + jnp.concatenate([_rope(slab, c, s)] * HEADS, axis=1)
    return q, k, _mm(cn, wv)


def _attn_fwd(q, k, v, bsz, seq, name, ride=None):
    n = q.shape[0]
    tq = min(256, seq)
    nq = seq // tq

    def body(q_ref, k_ref, v_ref, o_ref, lse_ref):
        lane = lax.broadcasted_iota(jnp.int32, (tq, 128), 1) < 64
        vv = v_ref[...]
        two = range(2)
        s = [_dot(q_ref[:, h * 128:(h + 1) * 128], k_ref[:, h * 128:(h + 1) * 128], NT) * ATT_SCALE for h in two]
        m = [jnp.max(s[h], axis=-1, keepdims=True) for h in two]
        e = [jnp.exp(s[h] - m[h]) for h in two]
        l = [jnp.sum(e[h], axis=-1, keepdims=True) for h in two]
        p = [(e[h] / l[h]).astype(bf16) for h in two]
        outs = [_dot(p[h], vv, NN) for h in two]
        o_ref[...] = jnp.where(lane, outs[0], outs[1])
        lse_ref[...] = jnp.where(lane, m[0] + jnp.log(l[0]), m[1] + jnp.log(l[1]))

    row = pl.BlockSpec((tq, 128), lambda b, h, i: (b * nq + i, h))
    return _pc(
        body, name=name, grid=(bsz, HEADS // 2, nq),
        in_specs=[pl.BlockSpec((tq, 256), lambda b, h, i: (b * nq + i, h)),
                  pl.BlockSpec((seq, 256), lambda b, h, i: (b, h)),
                  pl.BlockSpec((seq, 128), lambda b, h, i: (b, h))],
        out_specs=[row, row],
        out_shape=[_sds((n, HEADS * V_HEAD)), _sds((n, HEADS * V_HEAD))], sem=("parallel", "parallel", "parallel"),
        ride=ride,
    )(q, k, v)


def _attn_bwd(q, k, v, o, lse, do, bsz, seq, name, ride=None):
    n = q.shape[0]
    tq = min(256, seq)
    nq = seq // tq

    def body(q_ref, k_ref, v_ref, o_ref, lse_ref, do_ref, dq_ref, dk_ref, dv_ref):
        @pl.when(pl.program_id(2) == 0)
        def _():
            dk_ref[...] = jnp.zeros_like(dk_ref)
            dv_ref[...] = jnp.zeros_like(dv_ref)

        lane = lax.broadcasted_iota(jnp.int32, (tq, 128), 1) < 64
        vv = v_ref[...]
        for h in range(2):
            qh, kh = q_ref[:, h * 128:(h + 1) * 128], k_ref[:, h * 128:(h + 1) * 128]
            p = jnp.exp(_dot(qh, kh, NT) * ATT_SCALE - lse_ref[:, 64 * h:64 * h + 1])
            doh = jnp.where(lane if h == 0 else jnp.logical_not(lane), do_ref[...], 0.0)
            delta = jnp.sum(doh * o_ref[...], axis=-1, keepdims=True)
            dob = doh.astype(bf16)
            dp = _dot(dob, vv, NT)
            ds = (p * (dp - delta) * ATT_SCALE).astype(bf16)
            dq_ref[:, h * 128:(h + 1) * 128] = _dot(ds, kh, NN)
            dk_ref[:, h * 128:(h + 1) * 128] += _dot(ds, qh, TN)
            dv_ref[...] += _dot(p.astype(bf16), dob, TN)

    return _pc(
        body, name=name, grid=(bsz, HEADS // 2, nq),
        in_specs=[pl.BlockSpec((tq, 256), lambda b, h, i: (b * nq + i, h)),
                  pl.BlockSpec((seq, 256), lambda b, h, i: (b, h)),
                  pl.BlockSpec((seq, 128), lambda b, h, i: (b, h)),
                  pl.BlockSpec((tq, 128), lambda b, h, i: (b * nq + i, h)),
                  pl.BlockSpec((tq, 128), lambda b, h, i: (b * nq + i, h)),
                  pl.BlockSpec((tq, 128), lambda b, h, i: (b * nq + i, h))],
        out_specs=[pl.BlockSpec((tq, 256), lambda b, h, i: (b * nq + i, h)),
                   pl.BlockSpec((seq, 256), lambda b, h, i: (b, h)),
                   pl.BlockSpec((seq, 128), lambda b, h, i: (b, h))],
        out_shape=[_sds((n, HEADS * 128)), _sds((n, HEADS * 128)), _sds((n, HEADS * V_HEAD))],
        sem=("parallel", "parallel", "arbitrary"), ride=ride,
    )(q, k, v, o, lse, do)


@jax.custom_vjp
def _group_mix(vv, *ws):
    lane = lax.broadcasted_iota(jnp.int32, (SG_CHUNK, 128), 1) < 64
    slabs = []
    for j in range(SG_GROUPS // 2):
        vp = vv[:, 128 * j:128 * (j + 1)]
        slabs.append(jnp.where(lane, _bdot(ws[2 * j], vp), _bdot(ws[2 * j + 1], vp)))
    return jnp.concatenate(slabs, axis=1)


def _group_mix_bwd(res, g):
    vv, ws = res
    lane = lax.broadcasted_iota(jnp.int32, (SG_CHUNK, 128), 1) < 64
    dvs, dws = [], []
    for j in range(SG_GROUPS // 2):
        vp, gp = vv[:, 128 * j:128 * (j + 1)], g[:, 128 * j:128 * (j + 1)]
        lo, hi = jnp.where(lane, gp, 0.0), jnp.where(lane, 0.0, gp)
        dvs.append(_bdot(ws[2 * j], lo, TN) + _bdot(ws[2 * j + 1], hi, TN))
        dws += [_bdot(lo, vp, NT), _bdot(hi, vp, NT)]
    return (jnp.concatenate(dvs, axis=1), *dws)


_group_mix.defvjp(lambda vv, *ws: (_group_mix(vv, *ws), (vv, ws)), _group_mix_bwd)


def _f_sg(pu, pv, lg, lb, bias, *ws):
    u, vv = _gelu(pu), _gelu(pv)
    mu = jnp.mean(vv, axis=-1, keepdims=True)
    d = vv - mu
    vv = d * lax.rsqrt(jnp.mean(d * d, axis=-1, keepdims=True) + LN_EPS) * lg + lb
    return (u * (bias + _group_mix(vv, *ws)),)


def _shift_mean(a, prev_row, next_row):
    t = a.shape[0]
    row = lax.broadcasted_iota(jnp.int32, a.shape, 0)
    prev = jnp.where(row == 0, prev_row, pltpu.roll(a, 1, 0))
    nxt = jnp.where(row == t - 1, next_row, pltpu.roll(a, t - 1, 0))
    return 0.5 * (prev + nxt)


def _halo_specs(tm, width, blk, nblk8):
    h = tm // 8
    return [pl.BlockSpec((tm, width), lambda i: (i, blk)),
            pl.BlockSpec((8, width), lambda i: (jnp.maximum(i * h - 1, 0), blk)),
            pl.BlockSpec((8, width), lambda i: (jnp.minimum((i + 1) * h, nblk8 - 1), blk))]


def _edge_rows(i, tm, seq, pv_ref, nx_ref, scale=None):
    first = (i * tm) % seq == 0
    last = ((i + 1) * tm) % seq == 0
    pv, nx = pv_ref[7:8, :], nx_ref[0:1, :]
    if scale is not None:
        pv, nx = pv * scale, nx * scale
    return jnp.where(first, 0.0, pv), jnp.where(last, 0.0, nx)


def _shift_fwd(p, mu, seq, name):
    n = p.shape[0]
    tm = min(256, seq)
    blk = O_RW // RW_W

    def body(x_ref, pv_ref, nx_ref, mu_ref, z_ref):
        x = x_ref[...]
        pv, nx = _edge_rows(pl.program_id(0), tm, seq, pv_ref, nx_ref)
        z_ref[...] = x + mu_ref[...] * (_shift_mean(x, pv, nx) - x)

    return _pc(
        body, name=name, grid=(n // tm,),
        in_specs=_halo_specs(tm, RW_W, blk, n // 8) + [pl.BlockSpec((1, RW_W), lambda i: (0, 0))],
        out_specs=pl.BlockSpec((tm, RW_W), lambda i: (i, 0)), out_shape=_sds((n, RW_W)), sem=("parallel",),
    )(p, p, p, mu)


def _shift_bwd(dz, p, mu, seq, name):
    n = p.shape[0]
    tm = min(256, seq)
    blk = O_RW // RW_W

    def body(dz_ref, dpv_ref, dnx_ref, x_ref, pv_ref, nx_ref, mu_ref, dx_ref, dmu_ref):
        i = pl.program_id(0)
        mu_v = mu_ref[...]
        dzv = dz_ref[...]
        m = dzv * mu_v
        mpv, mnx = _edge_rows(i, tm, seq, dpv_ref, dnx_ref, mu_v)
        dx_ref[...] = (dzv - m + _shift_mean(m, mpv, mnx)).astype(dx_ref.dtype)
        x = x_ref[...]
        pv, nx = _edge_rows(i, tm, seq, pv_ref, nx_ref)
        part = jnp.sum(dzv * (_shift_mean(x, pv, nx) - x), axis=0, keepdims=True)

        @pl.when(i == 0)
        def _():
            dmu_ref[...] = part

        @pl.when(i != 0)
        def _():
            dmu_ref[...] += part

    return _pc(
        body, name=name, grid=(n // tm,),
        in_specs=_halo_specs(tm, RW_W, 0, n // 8) + _halo_specs(tm, RW_W, blk, n // 8)
        + [pl.BlockSpec((1, RW_W), lambda i: (0, 0))],
        out_specs=[pl.BlockSpec((tm, RW_W), lambda i: (i, 0)), pl.BlockSpec((1, RW_W), lambda i: (0, 0))],
        out_shape=[_sds((n, RW_W), bf16), _sds((1, RW_W))], sem=("arbitrary",),
    )(dz, dz, dz, p, p, p, mu)


def _f_rw_pre(k, wl, al, gl, w0, a0, w2, a2, g2, k_k, k_a):
    w = w0 + _mm(jnp.tanh(wl), w2)
    lw = -jnp.exp(-_softplus(-w) - 0.5)
    a = _sigmoid(a0 + _mm(al, a2))
    g = _mm(_sigmoid(gl), g2)
    kkr = k * k_k
    kk = kkr / jnp.maximum(jnp.sqrt(_group_sum(kkr * kkr)), 1e-12)
    two = lambda t: jnp.concatenate([t, t], axis=1)
    kd = two(k) * (1.0 + (a - 1.0) * two(k_a))
    bd = two(kk) * a
    return lw, kd, kk, bd, g


def _f_rw_post(y0, y1, r, v, kd0, kd1, g, r_k, ln_g, ln_b):
    y = y0 + y1
    mean = _group_sum(y) * (1.0 / RW_HEAD)
    d = y - mean
    var = _group_sum(d * d) * (1.0 / RW_HEAD)
    yn = d * lax.rsqrt(var + GN_EPS) * ln_g + ln_b
    bonus = _group_sum(r * (kd0 + kd1) * r_k)
    return ((yn + bonus * v) * g,)


@jax.custom_vjp
def _tri_inv(mats):
    c = mats[0].shape[0]
    row = lax.broadcasted_iota(jnp.int32, (c, c), 0)
    col = lax.broadcasted_iota(jnp.int32, (c, c), 1)
    eye = (row == col).astype(f32)
    blk = lambda b: (row // b) == (col // b)
    ld = [jnp.where(blk(8), a, 0.0) for a in mats]
    l2 = [_bdot(x, x) for x in ld]
    l4 = [_bdot(x, x) for x in l2]
    t = [_bdot(eye - x, eye + y) for x, y in zip(ld, l2)]
    t = [_bdot(x, eye + y) for x, y in zip(t, l4)]
    b = 8
    while b < c:
        sub = blk(2 * b) & jnp.logical_not(blk(b))
        p = [_bdot(x, jnp.where(sub, a, 0.0)) for x, a in zip(t, mats)]
        t = [x - _bdot(y, x) for x, y in zip(t, p)]
        b *= 2
    return tuple(t)


def _tri_inv_fwd(mats):
    t = _tri_inv(mats)
    return t, t


def _tri_inv_bwd(ts, gs):
    p = [_bdot(t, g, TN) for t, g in zip(ts, gs)]
    return (tuple(-_bdot(x, t, NT) for x, t in zip(p, ts)),)


_tri_inv.defvjp(_tri_inv_fwd, _tri_inv_bwd)


@jax.custom_vjp
def _tri_inv_saved(mats, ts):
    return ts


_tri_inv_saved.defvjp(lambda mats, ts: (ts, ts),
                      lambda ts, gs: (_tri_inv_bwd(ts, gs)[0], tuple(jnp.zeros_like(t) for t in ts)))


def _split3(x):
    h = x.astype(bf16)
    r = x - h.astype(f32)
    m = r.astype(bf16)
    return h, m, (r - m.astype(f32)).astype(bf16)


@jax.custom_vjp
def _mask_mm(mask, x):
    mb = mask.astype(bf16)
    return _dot(jnp.concatenate([mb, mb, mb], axis=1), jnp.concatenate(_split3(x), axis=0), NN)


def _mask_mm_bwd(mask, g):
    mb = mask.astype(bf16)
    return jnp.zeros_like(mask), _dot(jnp.concatenate([mb, mb, mb], axis=0), jnp.concatenate(_split3(g), axis=0), TN)


_mask_mm.defvjp(lambda mask, x: (_mask_mm(mask, x), mask), _mask_mm_bwd)


@jax.custom_vjp
def _split_lanes(x):
    h = x.shape[1] // 2
    return x[:, :h], x[:, h:]


_split_lanes.defvjp(lambda x: (_split_lanes(x), None), lambda _, g: (jnp.concatenate(g, axis=1),))


def _scan_chunk(s0, r, v, kk, lw, kd, bd, rev, inv=None):
    n = len(r)
    each = range(n)
    c = r[0].shape[0]
    row = lax.broadcasted_iota(jnp.int32, (c, 2 * c), 0)
    col = lax.broadcasted_iota(jnp.int32, (c, 2 * c), 1) % c
    ahead = jnp.where(rev, col - row, row - col)
    before = ahead > 0
    incl = ahead >= 0
    lane = lax.broadcasted_iota(jnp.int32, (1, 128), 1)
    m0 = (lane < 64).astype(f32)
    heads = lambda t: jnp.concatenate([t * m0, t * (1.0 - m0)], axis=0)
    bd_mask = ((lax.broadcasted_iota(jnp.int32, (128, 128), 0) // 64)
               == (lax.broadcasted_iota(jnp.int32, (128, 128), 1) // 64)).astype(f32)
    tot = [jnp.sum(lw[i], axis=0, keepdims=True) for i in each]
    row1 = lax.broadcasted_iota(jnp.int32, (c, c), 0)
    col1 = lax.broadcasted_iota(jnp.int32, (c, c), 1)
    upto = (jnp.where(rev, col1 - row1, row1 - col1) >= 0).astype(f32)
    lp = [_mask_mm(upto, lw[i]) - 0.5 * tot[i] for i in each]
    eg = [jnp.exp(lp[i]) for i in each]
    ieg = [jnp.exp(-lp[i]) for i in each]
    rt = [r[i] * eg[i] for i in each]
    kt = [kd[i] * ieg[i] for i in each]
    bt = [bd[i] * ieg[i] for i in each]
    at = [kk[i] * jnp.exp(lp[i] - lw[i]) for i in each]
    etot = [jnp.exp(0.5 * tot[i]) for i in each]
    si = [s0[i] * etot[i] for i in each]
    bth = [heads(bt[i]) for i in each]
    kth = [heads(kt[i]) for i in each]
    vh = [heads(v[i]) for i in each]
    a_ab = [jnp.where(before, _mm_nt(at[i], bth[i]), 0.0) for i in each]
    a_ak = [jnp.where(before, _mm_nt(at[i], kth[i]), 0.0) for i in each]
    a_rb = [jnp.where(incl, _mm_nt(rt[i], bth[i]), 0.0) for i in each]
    a_rk = [jnp.where(incl, _mm_nt(rt[i], kth[i]), 0.0) for i in each]
    halves = [_split_lanes(a_ab[i]) for i in each]
    mats = tuple(m for pair in halves for m in pair)
    inv = _tri_inv(mats) if inv is None else _tri_inv_saved(mats, inv)
    t = [jnp.concatenate([inv[2 * i], inv[2 * i + 1]], axis=1) for i in each]
    x0 = [_mm_nt(at[i], si[i]) for i in each]
    x = [x0[i] + _mm(a_ak[i], vh[i]) for i in each]
    u = [-_mm(t[i], heads(x[i])) for i in each]
    y0 = [_mm_nt(rt[i], si[i]) for i in each]
    y = [y0[i] + _mm(jnp.concatenate([a_rb[i], a_rk[i]], axis=1), jnp.concatenate([heads(u[i]), vh[i]], axis=0))
         for i in each]
    ds = [_mm_tn(jnp.concatenate([u[i], v[i]], axis=0), jnp.concatenate([bt[i], kt[i]], axis=0)) for i in each]
    se = [(si[i] + ds[i] * bd_mask) * etot[i] for i in each]
    return tuple(y), tuple(se), inv


PAIRS = HEADS // 2


def _scan_specs(nc, bsz, flip=False):
    def cc(d, c):
        c = nc - 1 - c if flip else c
        return jnp.where(d == 0, c, nc - 1 - c)

    rowblk = lambda d, b, c: b * nc + cc(d, c)
    zspec = lambda blk: pl.BlockSpec((CHUNK, RW_DIM), lambda d, b, c: (rowblk(d, b, c), blk))
    dspec = pl.BlockSpec((CHUNK, RW_DIM), lambda d, b, c: (rowblk(d, b, c), d))
    yspec = pl.BlockSpec((None, CHUNK, RW_DIM), lambda d, b, c: (d, rowblk(d, b, c), 0))
    sspec = pl.BlockSpec((None, PAIRS, 128, 128), lambda d, b, c: ((d * bsz + b) * nc + cc(d, c), 0, 0, 0))
    tspec = pl.BlockSpec((None, PAIRS, 128, 256), lambda d, b, c: ((d * bsz + b) * nc + cc(d, c), 0, 0, 0))
    return zspec, dspec, yspec, sspec, tspec


def _scan_fwd(z, lw, kd, kk, bd, bsz, seq, name, ride=None):
    n = z.shape[0]
    nc = seq // CHUNK
    zspec, dspec, yspec, sspec, tspec = _scan_specs(nc, bsz)

    def body(r_ref, v_ref, kk_ref, lw_ref, kd_ref, bd_ref, y_ref, s_ref, t_ref, st):
        @pl.when(pl.program_id(2) == 0)
        def _():
            st[...] = jnp.zeros_like(st)

        rev = pl.program_id(0) == 1
        lanes = [slice(h * 128, (h + 1) * 128) for h in range(PAIRS)]
        s0 = tuple(st[h] for h in range(PAIRS))
        ops = [tuple(ref[:, ln] for ln in lanes) for ref in (r_ref, v_ref, kk_ref, lw_ref, kd_ref, bd_ref)]
        y, se, inv = _scan_chunk(s0, *ops, rev)
        for h, ln in enumerate(lanes):
            s_ref[h] = s0[h]
            t_ref[h, :, :128] = inv[2 * h]
            t_ref[h, :, 128:] = inv[2 * h + 1]
            y_ref[:, ln] = y[h]
            st[h] = se[h]

    return _pc(
        body, name=name, grid=(2, bsz, nc),
        in_specs=[zspec(0), zspec(2), zspec(0), dspec, dspec, dspec],
        out_specs=[yspec, sspec, tspec],
        out_shape=[_sds((2, n, RW_DIM)), _sds((2 * bsz * nc, PAIRS, 128, 128)), _sds((2 * bsz * nc, PAIRS, 128, 256))],
        scratch=[pltpu.VMEM((PAIRS, 128, 128), f32)], sem=("parallel", "parallel", "arbitrary"), ride=ride,
    )(z, z, kk, lw, kd, bd)


def _scan_bwd(z, lw, kd, kk, bd, s_in, t_in, dy, bsz, seq, name, ride=None):
    n = z.shape[0]
    nc = seq // CHUNK
    zspec, dspec, yspec, sspec, tspec = _scan_specs(nc, bsz, flip=True)

    def body(r_ref, v_ref, kk_ref, lw_ref, kd_ref, bd_ref, s_ref, t_ref, dy_ref,
             dr_ref, dv_ref, dkk_ref, dlw_ref, dkd_ref, dbd_ref, dst):
        @pl.when(pl.program_id(2) == 0)
        def _():
            dst[...] = jnp.zeros_like(dst)

        rev = pl.program_id(0) == 1
        lanes = [slice(h * 128, (h + 1) * 128) for h in range(PAIRS)]
        s0 = tuple(s_ref[h] for h in range(PAIRS))
        inv = tuple(t_ref[h, :, a * 128:(a + 1) * 128] for h in range(PAIRS) for a in range(2))
        ops = [tuple(ref[:, ln] for ln in lanes) for ref in (r_ref, v_ref, kk_ref, lw_ref, kd_ref, bd_ref)]
        cot = (tuple(dy_ref[:, ln] for ln in lanes), tuple(dst[h] for h in range(PAIRS)))
        _, vjp = jax.vjp(lambda *a: _scan_chunk(*a, rev=rev, inv=inv)[:2], s0, *ops)
        grads = vjp(cot)
        for h, ln in enumerate(lanes):
            dst[h] = grads[0][h]
            for o_ref, g in zip((dr_ref, dv_ref, dkk_ref, dlw_ref, dkd_ref, dbd_ref), grads[1:]):
                o_ref[:, ln] = g[h]

    return _pc(
        body, name=name, grid=(2, bsz, nc),
        in_specs=[zspec(0), zspec(2), zspec(0), dspec, dspec, dspec, sspec, tspec, zspec(0)],
        out_specs=[yspec, yspec, yspec, dspec, dspec, dspec],
        out_shape=[_sds((2, n, RW_DIM))] * 3 + [_sds((n, 2 * RW_DIM))] * 3,
        scratch=[pltpu.VMEM((PAIRS, 128, 128), f32)], sem=("parallel", "parallel", "arbitrary"), ride=ride,
    )(z, z, kk, lw, kd, bd, s_in, t_in, dy)


def _merge_fwd(x2, p, ya, yb, yc, gb, wb, wo, name):
    n = x2.shape[0]
    tm = min(256, n)

    def body(x_ref, pg_ref, ya_ref, yb_ref, yc_ref, gb_ref, wb_ref, wo_ref, o_ref):
        gates = _sigmoid(pg_ref[...] + gb_ref[...])
        merged = jnp.zeros((tm, D), f32)
        for k, y_ref in enumerate((ya_ref, yb_ref, yc_ref)):
            merged += gates[:, k * D:(k + 1) * D] * _bdot(y_ref[...], wb_ref[k])
        o_ref[...] = x_ref[...] + _bdot(merged, wo_ref[...])

    row = lambda w, b=0: pl.BlockSpec((tm, w), lambda i, b=b: (i, b))
    return _pc(
        body, name=name, grid=(n // tm,),
        in_specs=[row(D), row(3 * D, O_GATE // (3 * D)), row(512), row(512), row(512),
                  pl.BlockSpec((1, 3 * D), lambda i: (0, 0)), pl.BlockSpec((3, 512, D), lambda i: (0, 0, 0)),
                  pl.BlockSpec((D, D), lambda i: (0, 0))],
        out_specs=row(D), out_shape=_sds((n, D)), sem=("parallel",),
    )(x2, p, ya, yb, yc, gb, wb, wo)


def _merge_bwd(dx1, p, ya, yb, yc, gb, wb, wo, name):
    n = dx1.shape[0]
    tm = min(256, n)

    def body(dx_ref, pg_ref, ya_ref, yb_ref, yc_ref, gb_ref, wb_ref, wo_ref,
             dpg_ref, dya_ref, dyb_ref, dyc_ref, dt_ref, mg_ref, dgb_ref):
        gates = _sigmoid(pg_ref[...] + gb_ref[...])
        dmerged = _bdot(dx_ref[...], wo_ref[...], NT)
        merged = jnp.zeros((tm, D), f32)
        dpg = []
        for k, (y_ref, dy_ref) in enumerate(((ya_ref, dya_ref), (yb_ref, dyb_ref), (yc_ref, dyc_ref))):
            gk = gates[:, k * D:(k + 1) * D]
            tk = _bdot(y_ref[...], wb_ref[k])
            merged += gk * tk
            dpg.append(dmerged * tk * gk * (1.0 - gk))
            dtk = dmerged * gk
            dt_ref[:, k * D:(k + 1) * D] = dtk.astype(bf16)
            dy_ref[...] = _bdot(dtk, wb_ref[k], NT)
        dpg = jnp.concatenate(dpg, axis=1)
        dpg_ref[...] = dpg.astype(bf16)
        mg_ref[...] = merged.astype(bf16)
        part = jnp.sum(dpg, axis=0, keepdims=True)

        @pl.when(pl.program_id(0) == 0)
        def _():
            dgb_ref[...] = part

        @pl.when(pl.program_id(0) != 0)
        def _():
            dgb_ref[...] += part

    row = lambda w, b=0: pl.BlockSpec((tm, w), lambda i, b=b: (i, b))
    return _pc(
        body, name=name, grid=(n // tm,),
        in_specs=[row(D), row(3 * D, O_GATE // (3 * D)), row(512), row(512), row(512),
                  pl.BlockSpec((1, 3 * D), lambda i: (0, 0)), pl.BlockSpec((3, 512, D), lambda i: (0, 0, 0)),
                  pl.BlockSpec((D, D), lambda i: (0, 0))],
        out_specs=[row(3 * D), row(512), row(512), row(512), row(3 * D), row(D),
                   pl.BlockSpec((1, 3 * D), lambda i: (0, 0))],
        out_shape=[_sds((n, 3 * D), bf16), _sds((n, 512)), _sds((n, 512)), _sds((n, 512)), _sds((n, 3 * D), bf16),
                   _sds((n, D), bf16), _sds((1, 3 * D))],
        sem=("arbitrary",),
    )(dx1, p, ya, yb, yc, gb, wb, wo)


FF_T = 1408


def _ffn_fwd(x1, g, wg, wu, wd, name):
    n = x1.shape[0]
    tm = min(512, n)
    nf = D_FF // FF_T

    def body(x_ref, g_ref, wg_ref, wu_ref, wd_ref, o_ref, hs):
        j = pl.program_id(1)

        @pl.when(j == 0)
        def _():
            hs[...] = _rms(x_ref[...], g_ref[...]).astype(bf16)
            o_ref[...] = x_ref[...]

        a = _dot(hs[...], wg_ref[...], NN)
        b = _dot(hs[...], wu_ref[...], NN)
        o_ref[...] += _bdot(a * _sigmoid(a) * b, wd_ref[...])

    return _pc(
        body, name=name, grid=(n // tm, nf),
        in_specs=[pl.BlockSpec((tm, D), lambda i, j: (i, 0)), pl.BlockSpec((1, D), lambda i, j: (0, 0)),
                  pl.BlockSpec((D, FF_T), lambda i, j: (0, j)), pl.BlockSpec((D, FF_T), lambda i, j: (0, j)),
                  pl.BlockSpec((FF_T, D), lambda i, j: (j, 0))],
        out_specs=pl.BlockSpec((tm, D), lambda i, j: (i, 0)), out_shape=_sds((n, D)),
        scratch=[pltpu.VMEM((tm, D), bf16)], sem=("parallel", "arbitrary"),
    )(x1, g, wg, wu, wd)


def _ffn_bwd(dx2, x1, g, wg, wu, wd, name):
    n = x1.shape[0]
    tm = min(512, n)
    nf = D_FF // FF_T

    def body(dx_ref, x_ref, g_ref, wg_ref, wu_ref, wd_ref, dx1_ref, dg_ref, h_ref, da_ref, db_ref, hm_ref, acc):
        i, j = pl.program_id(0), pl.program_id(1)

        @pl.when(j == 0)
        def _():
            h_ref[...] = _rms(x_ref[...], g_ref[...]).astype(bf16)
            acc[...] = jnp.zeros_like(acc)

        @pl.when((i == 0) & (j == 0))
        def _():
            dg_ref[...] = jnp.zeros_like(dg_ref)

        h = h_ref[...]
        a = _dot(h, wg_ref[...], NN)
        b = _dot(h, wu_ref[...], NN)
        sg = _sigmoid(a)
        s = a * sg
        dhm = _bdot(dx_ref[...], wd_ref[...], NT)
        da = (dhm * b * (sg * (1.0 + a * (1.0 - sg)))).astype(bf16)
        db = (dhm * s).astype(bf16)
        da_ref[...] = da
        db_ref[...] = db
        hm_ref[...] = (s * b).astype(bf16)
        acc[...] += _dot(da, wg_ref[...], NT) + _dot(db, wu_ref[...], NT)

        @pl.when(j == nf - 1)
        def _():
            _, vjp = jax.vjp(_rms, x_ref[...], g_ref[...])
            dx, dg = vjp(acc[...])
            dx1_ref[...] = dx_ref[...] + dx
            dg_ref[...] += dg

    rowf = pl.BlockSpec((tm, FF_T), lambda i, j: (i, j))
    rowd = pl.BlockSpec((tm, D), lambda i, j: (i, 0))
    vec = pl.BlockSpec((1, D), lambda i, j: (0, 0))
    return _pc(
        body, name=name, grid=(n // tm, nf),
        in_specs=[rowd, rowd, vec, pl.BlockSpec((D, FF_T), lambda i, j: (0, j)),
                  pl.BlockSpec((D, FF_T), lambda i, j: (0, j)), pl.BlockSpec((FF_T, D), lambda i, j: (j, 0))],
        out_specs=[rowd, vec, rowd, rowf, rowf, rowf],
        out_shape=[_sds((n, D)), _sds((1, D)), _sds((n, D), bf16), _sds((n, D_FF), bf16), _sds((n, D_FF), bf16),
                   _sds((n, D_FF), bf16)],
        scratch=[pltpu.VMEM((tm, D), f32)], sem=("arbitrary", "arbitrary"),
    )(dx2, x1, g, wg, wu, wd)


def _loss_head(x2, g, tgt, name):
    n = x2.shape[0]
    tm = min(512, n)

    def f(x, gg, t):
        e = _rms(x, gg) - t
        return 0.5 * jnp.sum(jnp.mean(e * e, axis=-1, keepdims=True))

    def body(x_ref, g_ref, t_ref, l_ref, dx_ref, dg_ref):
        val, vjp = jax.vjp(f, x_ref[...], g_ref[...], t_ref[...])
        dx, dg, _ = vjp(jnp.ones((), f32))
        dx_ref[...] = dx

        @pl.when(pl.program_id(0) == 0)
        def _():
            l_ref[...] = jnp.zeros_like(l_ref)
            dg_ref[...] = jnp.zeros_like(dg_ref)

        l_ref[...] += val
        dg_ref[...] += dg

    rowd = pl.BlockSpec((tm, D), lambda i: (i, 0))
    return _pc(
        body, name=name, grid=(n // tm,),
        in_specs=[rowd, pl.BlockSpec((1, D), lambda i: (0, 0)), rowd],
        out_specs=[pl.BlockSpec((8, 128), lambda i: (0, 0)), rowd, pl.BlockSpec((1, D), lambda i: (0, 0))],
        out_shape=[_sds((8, 128)), _sds((n, D)), _sds((1, D))], sem=("arbitrary",),
    )(x2, g, tgt)


def _adamw(w, parts, m, v, name):
    nl, r, c = w.shape
    tr = r
    for cand in (1024, 512, 256, 128, 64, 32, 16, 8):
        if r % cand == 0 and cand * c * 4 <= 1024 * 1024:
            tr = cand
            break

    def body(*refs):
        w_ref, p_refs, (m_ref, v_ref, g_ref, d_ref, nm_ref, nv_ref) = refs[0], refs[1:1 + nl], refs[1 + nl:]

        def update(p_ref):
            gg = p_ref[0].astype(f32)
            for k in range(1, N_DEV):
                gg = gg + p_ref[k].astype(f32)
            g_ref[...] = gg
            nm = B1 * m_ref[...] + (1.0 - B1) * gg
            nv = B2 * v_ref[...] + (1.0 - B2) * (gg * gg)
            m_hat = nm / (1.0 - B1 ** STEP)
            v_hat = nv / (1.0 - B2 ** STEP)
            d_ref[...] = -LR * (m_hat / (jnp.sqrt(v_hat) + EPS) + WD * w_ref[...])
            nm_ref[...] = nm
            nv_ref[...] = nv

        for j in range(nl):
            pl.when(pl.program_id(0) == j)(functools.partial(update, p_refs[j]))

    spec = pl.BlockSpec((None, tr, c), lambda l, i: (l, i, 0))
    pspecs = [pl.BlockSpec((N_DEV, tr, c), lambda l, i, j=j: (0, jnp.where(l == j, i, 0), 0)) for j in range(nl)]
    return _pc(body, name=name, grid=(nl, r // tr), in_specs=[spec] + pspecs + [spec, spec], out_specs=[spec] * 4,
               out_shape=[_sds((nl, r, c))] * 4, sem=("arbitrary", "arbitrary"))(w, *parts, m, v)


def _peers():
    x, y, c = lax.axis_index("x"), lax.axis_index("y"), lax.axis_index("c")
    me = 4 * x + 2 * y + c
    peers = []
    for k in range(1, N_DEV):
        fx, fy, fc = (k >> 2) & 1, (k >> 1) & 1, k & 1
        peers.append(((1 - x) if fx else x, (1 - y) if fy else y, (1 - c) if fc else c))
    return me, peers


def _exchange(gathers, scatters, name):
    _, got = _pc(lambda: None, name=name, out_shape=[], ride=(gathers, scatters))()
    return got


SHARDED = {"w_in": 2, "gate_b": 2, "w_uq": 2, "w_ukv": 2, "rw_w0": 2, "rw_w2": 3, "rw_a0": 2, "rw_a2": 3, "rw_g2": 2,
           "w_branch": 3, "w_out": 1, "w_ffn_gate": 2, "w_ffn_up": 2, "w_ffn_down": 1}
GATHER_F32 = ("gate_b", "rw_w0", "rw_a0")
REPLICATED = ("attn_norm_g", "q_norm_g", "kv_norm_g", "sg_ln_g", "sg_ln_b", "sg_w", "sg_b", "rw_mu", "rw_k_k", "rw_k_a",
              "rw_r_k", "rw_ln_g", "rw_ln_b", "ffn_norm_g", "final_norm_g")
WEIGHTS = ("attn_norm_g", "w_in", "gate_b", "q_norm_g", "w_uq", "kv_norm_g", "w_ukv", "sg_ln_g", "sg_ln_b", "sg_w", "sg_b",
           "rw_mu", "rw_w0", "rw_w2", "rw_a0", "rw_a2", "rw_g2", "rw_k_k", "rw_k_a", "rw_r_k", "rw_ln_g", "rw_ln_b",
           "w_branch", "w_out", "ffn_norm_g", "w_ffn_gate", "w_ffn_up", "w_ffn_down", "final_norm_g")


REP_MAIN = tuple(k for k in REPLICATED if k not in ("attn_norm_g", "sg_w"))
BIG = ("w_in", "w_branch", "w_out", "w_ffn_gate", "w_ffn_up", "w_ffn_down")
SMALL_BF = ("w_uq", "w_ukv", "rw_w2", "rw_a2", "rw_g2")
SMALL = SMALL_BF + GATHER_F32


def _pack128(blocks, names, dtype, lead=0, to=256):
    parts = [blocks[k].astype(dtype).reshape(blocks[k].shape[:lead] + (-1, 128)) for k in names]
    rows = sum(p.shape[lead] for p in parts)
    pad = -rows % to
    if pad:
        parts.append(jnp.zeros(parts[0].shape[:lead] + (pad, 128), dtype))
    return jnp.concatenate(parts, axis=lead)


def _unpack128(packed, shapes, names, lead=0):
    out, off = {}, 0
    for k in names:
        rows = 1
        for d in shapes[k]:
            rows *= d
        rows //= 128
        idx = (slice(None),) * lead + (slice(off, off + rows),)
        out[k] = packed[idx].reshape(packed.shape[:lead] + tuple(shapes[k]))
        off += rows
    return out


def _join_blocks(g, ax):
    shp = g.shape[1:]
    return jnp.moveaxis(g, 0, ax).reshape(shp[:ax] + (N_DEV * shp[ax],) + shp[ax + 1:])


def _split_blocks(full, ax):
    shp = full.shape
    return jnp.moveaxis(full.reshape(shp[:ax] + (N_DEV, shp[ax] // N_DEV) + shp[ax + 1:]), ax, 0)


def _w_in_padded(w):
    z = lambda n: jnp.zeros((w.shape[0], n), w.dtype)
    q, ckv, kr = w[:, 0:384], w[:, 384:640], w[:, 640:672]
    sg, rw, gate = w[:, 672:1696], w[:, 1696:3616], w[:, 3616:6688]
    return jnp.concatenate([gate, sg, rw, z(128), ckv, z(64), kr, z(32), q, z(P_W - O_MLA - MLA_W)], axis=1)


def _w_in_unpadded(g):
    return jnp.concatenate([g[:, O_Q:O_Q + 384], g[:, O_CKV:O_CKV + 256], g[:, O_SLAB + 64:O_SLAB + 96],
                            g[:, O_SG:O_SG + 1024], g[:, O_RW:O_RW + 1920], g[:, O_GATE:O_GATE + 3072]], axis=1)


REST = ("w_branch", "w_out", "w_ffn_gate", "w_ffn_up", "w_ffn_down")


def _rest_weights(full, l):
    return dict(wb=full["w_branch"][l], wo=full["w_out"][l], wg=full["w_ffn_gate"][l], wu=full["w_ffn_up"][l],
                wd=full["w_ffn_down"][l])


def _layer_weights(full, rep, l):
    w = {}
    w["w_in"] = _w_in_padded(full["w_in"][l])
    if full["w_branch"][l] is not None:
        w.update(_rest_weights(full, l))
    uq = full["w_uq"][l].reshape(Q_LORA, HEADS, QK_NOPE + QK_ROPE)
    w["wq"] = jnp.pad(uq, ((0, 0), (0, 0), (0, 32))).reshape(Q_LORA, HEADS * 128).astype(f32)
    ukv = full["w_ukv"][l].reshape(KV_LORA, HEADS, QK_NOPE + V_HEAD)
    wk = jnp.pad(ukv[:, :, :QK_NOPE], ((0, 0), (0, 0), (0, 64))).reshape(KV_LORA, HEADS * 128)
    w["wk"], w["wv"] = wk.astype(f32), ukv[:, :, QK_NOPE:].reshape(KV_LORA, HEADS * V_HEAD).astype(f32)
    bdiag = lambda t: jnp.concatenate([jnp.concatenate([t[0], jnp.zeros_like(t[0])], axis=1),
                                       jnp.concatenate([jnp.zeros_like(t[1]), t[1]], axis=1)], axis=0).astype(f32)
    w["w2"], w["a2"] = bdiag(full["rw_w2"][l]), bdiag(full["rw_a2"][l])
    w["g2"] = full["rw_g2"][l].astype(f32)
    w["w0"], w["a0"] = full["rw_w0"][l].reshape(1, 2 * RW_DIM), full["rw_a0"][l].reshape(1, 2 * RW_DIM)
    w["gate_b"] = full["gate_b"][l].reshape(1, 3 * D)
    row = lambda a: a.reshape(1, -1)
    for k in ("attn_norm_g", "q_norm_g", "kv_norm_g", "sg_ln_g", "sg_ln_b", "rw_k_k", "rw_k_a", "rw_ln_g", "rw_ln_b",
              "ffn_norm_g"):
        w[k] = row(rep[k][l])
    w["r_k"] = row(rep["rw_r_k"][l])
    w["mu"] = jnp.pad(row(rep["rw_mu"][l]), ((0, 0), (0, RW_W - 1920)))
    w["sg_w"] = [rep["sg_w"][l, k] for k in range(SG_GROUPS)]
    w["sg_bias"] = jnp.repeat(rep["sg_b"][l].T, SG_DIM // SG_GROUPS, axis=1)
    return w


def _riding(res, ride, got, key):
    if ride is None:
        return res
    got[key] = res[1]
    return res[0]


def _layer_fwd(x2, w, tabs, bsz, seq, l, rides=None, on_gathered=None):
    nm = lambda s: f"l{l}_{s}"
    n = x2.shape[0]
    tm = min(512, n)
    rides = rides or {}
    ride = lambda key: (rides[key], []) if key in rides else None
    got = {}
    p, h = _riding(_inproj_fwd(x2, w["attn_norm_g"], w["w_in"], nm("inproj"), ride("inproj")), ride("inproj"), got, "inproj")
    mla_rows = [(p, 256, O_CKV // 256), (p, 128, O_SLAB // 128), (p, 384, O_Q // 384), (tabs[0], 128, 0), (tabs[1], 128, 0)]
    mla_w = [w["q_norm_g"], w["kv_norm_g"], w["wq"], w["wk"], w["wv"]]
    q, k, v = _rowwise_fwd(nm("mla_proj"), _f_mla_proj, mla_rows, mla_w, [(1024, bf16), (1024, bf16), (512, bf16)], tm)
    ya, lse = _riding(_attn_fwd(q, k, v, bsz, seq, nm("attn"), ride("attn")), ride("attn"), got, "attn")
    sg_rows = [(p, SG_DIM, O_SG // SG_DIM), (p, SG_DIM, O_SG // SG_DIM + 1)]
    sg_w = [w["sg_ln_g"], w["sg_ln_b"], w["sg_bias"]] + w["sg_w"]
    (yb,) = _rowwise_fwd(nm("sg"), _f_sg, sg_rows, sg_w, [(SG_DIM, f32)], SG_CHUNK)
    z = _shift_fwd(p, w["mu"], seq, nm("shift"))
    pre_rows = [(z, 512, 1), (z, 128, 12), (z, 128, 13), (z, 128, 14)]
    pre_w = [w["w0"], w["a0"], w["w2"], w["a2"], w["g2"], w["rw_k_k"], w["rw_k_a"]]
    lw, kd, kk, bd, g = _rowwise_fwd(nm("rw_pre"), _f_rw_pre, pre_rows, pre_w,
                                     [(1024, f32), (1024, f32), (512, f32), (1024, f32), (512, f32)], tm)
    y, s_in, t_in = _riding(_scan_fwd(z, lw, kd, kk, bd, bsz, seq, nm("scan"), ride("scan")), ride("scan"), got, "scan")
    post_rows = [(y[0], 512, 0), (y[1], 512, 0), (z, 512, 0), (z, 512, 2), (kd, 512, 0), (kd, 512, 1), (g, 512, 0)]
    post_w = [w["r_k"], w["rw_ln_g"], w["rw_ln_b"]]
    (yc,) = _rowwise_fwd(nm("rw_post"), _f_rw_post, post_rows, post_w, [(512, f32)], tm)
    if on_gathered is not None:
        w.update(on_gathered(got))
    x1 = _merge_fwd(x2, p, ya, yb, yc, w["gate_b"], w["wb"], w["wo"], nm("merge"))
    x3 = _ffn_fwd(x1, w["ffn_norm_g"], w["wg"], w["wu"], w["wd"], nm("ffn"))
    saved = dict(x=x2, p=p, h=h, q=q, k=k, v=v, ya=ya, lse=lse, yb=yb, z=z, lw=lw, kd=kd, kk=kk, bd=bd, g=g, y=y, s_in=s_in, t_in=t_in, yc=yc,
                 x1=x1, mla_rows=mla_rows, mla_w=mla_w, sg_rows=sg_rows, sg_w=sg_w, pre_rows=pre_rows, pre_w=pre_w,
                 post_rows=post_rows, post_w=post_w)
    return x3, saved, got


def _layer_bwd(dx3, w, sv, bsz, seq, l, rides=None):
    nm = lambda s: f"l{l}_{s}_bwd"
    n = dx3.shape[0]
    tm = min(256, n)
    g = {}
    rides = rides or {}
    ride = lambda key: rides[key](g) if key in rides else None
    got = {}
    dx1, g["ffn_norm_g"], h2, da, db, hm = _ffn_bwd(dx3, sv["x1"], w["ffn_norm_g"], w["wg"], w["wu"], w["wd"], nm("ffn"))
    g["wg"] = _matmul_tn(h2, da, nm("wg"))
    g["wu"] = _matmul_tn(h2, db, nm("wu"))
    g["wd"] = _matmul_tn(hm, dx3, nm("wd"))
    dpg, dya, dyb, dyc, dt, mg, g["gate_b"] = _merge_bwd(dx1, sv["p"], sv["ya"], sv["yb"], sv["yc"], w["gate_b"], w["wb"],
                                                         w["wo"], nm("merge"))
    g["wo"] = _matmul_tn(mg, dx1, nm("wo"))
    ys = (sv["ya"], sv["yb"], sv["yc"])
    g["wb"] = jnp.stack([_matmul_tn(ys[k], dt[:, k * D:(k + 1) * D], nm(f"wb{k}")) for k in range(3)])
    (dy, dr_p, dv_p, dkd0, dkd1, dg_), (g["r_k"], g["rw_ln_g"], g["rw_ln_b"]) = _rowwise_bwd(
        nm("rw_post"), _f_rw_post, sv["post_rows"], sv["post_w"], [(dyc, 512, 0)], tm, [f32, None] + [f32] * 5)
    dkd_p = jnp.concatenate([dkd0, dkd1], axis=1)
    rd = ride("scan")
    dr_s, dv_s, dkk_s, dlw, dkd_s, dbd = _riding(
        _scan_bwd(sv["z"], sv["lw"], sv["kd"], sv["kk"], sv["bd"], sv["s_in"], sv["t_in"], dy, bsz, seq, nm("scan"), rd),
        rd, got, "scan")
    pre_cots = [(dlw, 1024, 0), (dkd_s + dkd_p, 1024, 0), (dkk_s[0] + dkk_s[1], 512, 0), (dbd, 1024, 0), (dg_, 512, 0)]
    (dk, dwl, dal, dgl), (g["w0"], g["a0"], g["w2"], g["a2"], g["g2"], g["rw_k_k"], g["rw_k_a"]) = _rowwise_bwd(
        nm("rw_pre"), _f_rw_pre, sv["pre_rows"], sv["pre_w"], pre_cots, tm, [f32] * 4)
    dz = jnp.concatenate([dr_s[0] + dr_s[1] + dr_p, dk, dv_s[0] + dv_s[1] + dv_p, dwl, dal, dgl,
                          jnp.zeros((n, RW_W - 1920), f32)], axis=1)
    dp_rw, g["mu"] = _shift_bwd(dz, sv["p"], w["mu"], seq, nm("shift"))
    (dp_su, dp_sv), (g["sg_ln_g"], g["sg_ln_b"], g["sg_bias"], *sgw) = _rowwise_bwd(
        nm("sg"), _f_sg, sv["sg_rows"], sv["sg_w"], [(dyb, SG_DIM, 0)], SG_CHUNK, [bf16, bf16])
    g["sg_w"] = jnp.stack(sgw)
    rd = ride("attn")
    dq, dk_, dv_ = _riding(_attn_bwd(sv["q"], sv["k"], sv["v"], sv["ya"], sv["lse"], dya, bsz, seq, nm("attn"), rd), rd, got,
                           "attn")
    (dp_ckv, dp_slab, dp_q), (g["q_norm_g"], g["kv_norm_g"], g["wq"], g["wk"], g["wv"]) = _rowwise_bwd(
        nm("mla_proj"), _f_mla_proj, sv["mla_rows"], sv["mla_w"], [(dq, 1024, 0), (dk_, 1024, 0), (dv_, 512, 0)], tm,
        [bf16, bf16, bf16, None, None])
    dp = jnp.concatenate([dpg, dp_su, dp_sv, dp_rw, dp_ckv, dp_slab, dp_q, jnp.zeros((n, P_W - O_MLA - MLA_W), bf16)],
                         axis=1)
    rd = ride("w_in")
    g["w_in"] = _riding(_matmul_tn(sv["h"], dp, nm("w_in"), rd), rd, got, "w_in")
    rd = ride("inproj")
    dx, g["attn_norm_g"] = _riding(_norm_matmul_bwd(dp, w["w_in"], sv["x"], w["attn_norm_g"], dx1, nm("inproj"), rd), rd, got,
                                   "inproj")
    return dx, g, got


def _layer_grads_to_full(g):
    o = {}
    if "w_in" in g:
        o["w_in"] = _w_in_unpadded(g["w_in"])
    o["w_uq"] = g["wq"].reshape(Q_LORA, HEADS, 128)[:, :, :QK_NOPE + QK_ROPE].reshape(Q_LORA, -1)
    gk = g["wk"].reshape(KV_LORA, HEADS, 128)[:, :, :QK_NOPE]
    gv = g["wv"].reshape(KV_LORA, HEADS, V_HEAD)
    o["w_ukv"] = jnp.concatenate([gk, gv], axis=2).reshape(KV_LORA, -1)
    unb = lambda t: jnp.stack([t[:LORA, :RW_DIM], t[LORA:, RW_DIM:]])
    o["rw_w2"], o["rw_a2"], o["rw_g2"] = unb(g["w2"]), unb(g["a2"]), g["g2"]
    o["rw_w0"], o["rw_a0"] = g["w0"].reshape(2, RW_DIM), g["a0"].reshape(2, RW_DIM)
    o["gate_b"] = g["gate_b"].reshape(3, D)
    o["w_branch"], o["w_out"] = g["wb"], g["wo"]
    o["w_ffn_gate"], o["w_ffn_up"], o["w_ffn_down"] = g["wg"], g["wu"], g["wd"]
    for k in ("attn_norm_g", "q_norm_g", "kv_norm_g", "sg_ln_g", "sg_ln_b", "rw_k_k", "rw_k_a", "rw_ln_g", "rw_ln_b",
              "ffn_norm_g"):
        if k in g:
            o[k] = g[k].reshape(-1)
    o["rw_r_k"] = g["r_k"].reshape(HEADS, RW_HEAD)
    o["rw_mu"] = g["mu"].reshape(-1)[:1920]
    o["sg_w"] = g["sg_w"]
    o["sg_b"] = g["sg_bias"].reshape(SG_CHUNK, SG_GROUPS, SG_DIM // SG_GROUPS).sum(axis=2).T
    return o


def _rope_tables(positions):
    inv = 1.0 / (10000.0 ** (jnp.arange(0, QK_ROPE, 2, dtype=f32) / QK_ROPE))
    ang = positions.astype(f32)[:, None] * inv
    cos, sin = jnp.cos(ang), jnp.sin(ang)
    n = positions.shape[0]
    c = jnp.concatenate([jnp.ones((n, 64), f32), cos, cos, jnp.zeros((n, 32), f32)], axis=1)
    s = jnp.concatenate([jnp.zeros((n, 64), f32), -sin, sin, jnp.zeros((n, 32), f32)], axis=1)
    return c, s


def _grad_parts(grad, name):
    return _split_blocks(grad, SHARDED[name] - 1).astype(bf16)


def _local_step(x, positions, full, rep, loss_target, blocks=None):
    bsz, seq, _ = x.shape
    n = bsz * seq
    x2 = x.reshape(n, D)
    tabs = _rope_tables(positions.reshape(n))
    join = lambda k, g: _join_blocks(g, SHARDED[k] - 1)
    rides, on_gathered = None, None
    if blocks is not None:
        carried = {"inproj": [("w_branch", 0), ("w_out", 0)],
                   "attn": [("w_in", 1), ("w_ffn_gate", 0), ("w_ffn_up", 0)],
                   "scan": [("w_ffn_down", 0)] + [(k, 1) for k in REST]}
        rides = {key: [blocks[k][l] for k, l in what] for key, what in carried.items()}

        def on_gathered(got):
            for key, what in carried.items():
                for (k, l), g in zip(what, got[key]):
                    full[k][l] = join(k, g)
            return _rest_weights(full, 0)

    w0 = _layer_weights(full, rep, 0)
    x2, sv0, got = _layer_fwd(x2, w0, tabs, bsz, seq, 0, rides, on_gathered)
    w1 = _layer_weights(full, rep, 1)
    x2, sv1, _ = _layer_fwd(x2, w1, tabs, bsz, seq, 1)
    loss, dx, dgf = _loss_head(x2, rep["final_norm_g"].reshape(1, D), loss_target.reshape(n, D), "loss_head")
    dx, g1, _ = _layer_bwd(dx, w1, sv1, bsz, seq, 1)
    grads1 = _layer_grads_to_full(g1)
    rides = None
    if blocks is not None:
        short = dict(w_branch="wb", w_out="wo", w_ffn_gate="wg", w_ffn_up="wu", w_ffn_down="wd")

        def beside_w_in(g):
            g0 = _layer_grads_to_full(g)
            both = {k: jnp.stack([g0[k], grads1[k]]) for k in g0}
            both["final_norm_g"] = dgf.reshape(D)
            split = {k: _split_blocks(both[k], SHARDED[k]) for k in SMALL}
            return [_pack128(both, REP_MAIN, f32), both["sg_w"]], [_pack128(split, SMALL, f32, lead=1)]

        rides = {"scan": lambda g: ([], [_grad_parts(grads1[k], k) for k in BIG]),
                 "attn": lambda g: ([], [_grad_parts(g[short[k]], k) for k in REST]),
                 "w_in": beside_w_in,
                 "inproj": lambda g: ([], [_grad_parts(_w_in_unpadded(g["w_in"]), "w_in")])}
    dx, g0, got = _layer_bwd(dx, w0, sv0, bsz, seq, 0, rides)
    grads0 = _layer_grads_to_full(g0)
    grads = {k: [grads0[k], grads1[k]] for k in grads0}
    grads["final_norm_g"] = dgf.reshape(D)
    parts = {}
    if blocks is not None:
        parts = {k: [None, p] for k, p in zip(BIG, got["scan"])}
        for k, p in zip(REST, got["attn"]):
            parts[k][0] = p
        parts["replicated"], parts["sg_w"], parts["small"] = got["w_in"]
        (parts["w_in"][0],) = got["inproj"]
    return loss[0, 0], dx.reshape(bsz, seq, D), grads, parts


def kernel(x, positions, attn_norm_g, w_in, gate_b, q_norm_g, w_uq, kv_norm_g, w_ukv, sg_ln_g, sg_ln_b, sg_w, sg_b, rw_mu, rw_w0, rw_w2, rw_a0, rw_a2, rw_g2, rw_k_k, rw_k_a, rw_r_k, rw_ln_g, rw_ln_b, w_branch, w_out, ffn_norm_g, w_ffn_gate, w_ffn_up, w_ffn_down, final_norm_g, loss_target, m_attn_norm_g, m_w_in, m_gate_b, m_q_norm_g, m_w_uq, m_kv_norm_g, m_w_ukv, m_sg_ln_g, m_sg_ln_b, m_sg_w, m_sg_b, m_rw_mu, m_rw_w0, m_rw_w2, m_rw_a0, m_rw_a2, m_rw_g2, m_rw_k_k, m_rw_k_a, m_rw_r_k, m_rw_ln_g, m_rw_ln_b, m_w_branch, m_w_out, m_ffn_norm_g, m_w_ffn_gate, m_w_ffn_up, m_w_ffn_down, m_final_norm_g, v_attn_norm_g, v_w_in, v_gate_b, v_q_norm_g, v_w_uq, v_kv_norm_g, v_w_ukv, v_sg_ln_g, v_sg_ln_b, v_sg_w, v_sg_b, v_rw_mu, v_rw_w0, v_rw_w2, v_rw_a0, v_rw_a2, v_rw_g2, v_rw_k_k, v_rw_k_a, v_rw_r_k, v_rw_ln_g, v_rw_ln_b, v_w_branch, v_w_out, v_ffn_norm_g, v_w_ffn_gate, v_w_ffn_up, v_w_ffn_down, v_final_norm_g):
    args = locals()
    wts = {k: args[k] for k in WEIGHTS}
    mom_m = {k: args["m_" + k] for k in WEIGHTS}
    mom_v = {k: args["v_" + k] for k in WEIGHTS}
    shapes = {k: wts[k].shape for k in WEIGHTS}
    blocks = {k: wts[k].astype(bf16) for k in BIG}
    got = _exchange([blocks["w_in"][0], _pack128(wts, SMALL_BF, bf16), _pack128(wts, GATHER_F32, f32)], [], "gather_first")
    small = {**_unpack128(got[1], shapes, SMALL_BF, lead=1), **_unpack128(got[2], shapes, GATHER_F32, lead=1)}
    full = {k: list(_join_blocks(small[k], SHARDED[k])) for k in SMALL}
    full["w_in"] = [_join_blocks(got[0], SHARDED["w_in"] - 1), None]
    full.update({k: [None, None] for k in REST})
    rep = {k: wts[k] for k in REPLICATED}
    loss, grad_x, grads, parts = _local_step(x, positions, full, rep, loss_target, blocks)
    loss = lax.psum(loss, ("x", "y", "c"))
    last = ("attn_norm_g",)
    (last_parts,) = _exchange([_pack128({"attn_norm_g": jnp.stack(grads["attn_norm_g"])}, last, f32, to=16)], [],
                              "exchange_last")
    gw, delta, new_m, new_v = {}, {}, {}, {}
    for k in BIG:
        three = lambda a, k=k: a.reshape(a.shape[0], -1, shapes[k][-1])
        res = _adamw(three(wts[k]), [three(p) for p in parts[k]], three(mom_m[k]), three(mom_v[k]), f"adamw_{k}")
        gw[k], delta[k], new_m[k], new_v[k] = (t.reshape(shapes[k]) for t in res)
    rows = lambda a: a.reshape(1, -1, 128)
    res = _adamw(rows(wts["sg_w"]), [parts["sg_w"].reshape(N_DEV, -1, 128)], rows(mom_m["sg_w"]), rows(mom_v["sg_w"]),
                 "adamw_sg_w")
    gw["sg_w"], delta["sg_w"], new_m["sg_w"], new_v["sg_w"] = (t.reshape(shapes["sg_w"]) for t in res)
    for names, got, to in ((SMALL, parts["small"], 256), (REP_MAIN, parts["replicated"], 256), (last, last_parts, 16)):
        pk = lambda dct: _pack128(dct, names, f32, to=to)[None]
        res = _adamw(pk(wts), [got], pk(mom_m), pk(mom_v), f"adamw_{names[0]}")
        for dst, t in zip((gw, delta, new_m, new_v), res):
            dst.update(_unpack128(t[0], shapes, names))
    return (loss, grad_x, *[gw[k] for k in WEIGHTS], *[delta[k] for k in WEIGHTS], *[new_m[k] for k in WEIGHTS],
            *[new_v[k] for k in WEIGHTS])
```

```python
import functools

import jax
import jax.numpy as jnp
from jax import lax
from jax.experimental import pallas as pl
from jax.experimental.pallas import tpu as pltpu

f32 = jnp.float32
bf16 = jnp.bfloat16
HI = lax.Precision.HIGHEST
NN, NT, TN = ((1,), (0,)), ((1,), (1,)), ((0,), (0,))

N_DEV = 8
D = 1024
HEADS = 8
Q_LORA, KV_LORA, QK_NOPE, QK_ROPE, V_HEAD = 384, 256, 64, 32, 64
SG_DIM, SG_CHUNK, SG_GROUPS = 512, 128, 8
RW_DIM, RW_HEAD, LORA = 512, 64, 64
D_FF = 2816
N_IN = 6688
NORM_EPS, LN_EPS, GN_EPS = 1e-6, 1e-5, 64e-5
ATT_SCALE = (QK_NOPE + QK_ROPE) ** -0.5
P_W = 7168
O_GATE, O_SG, O_RW, O_MLA = 0, 3072, 4096, 6144
RW_W = 2048
MLA_W = 768
O_CKV, O_SLAB, O_Q = O_MLA, O_MLA + 256, O_MLA + 384
CHUNK = 128
VMEM_LIMIT = 56 * 1024 * 1024

B1, B2, LR, EPS, WD, STEP = 0.9, 0.999, 0.001, 1e-8, 0.01, 10


def _pc(body, *, name, out_shape, grid=(), in_specs=(), out_specs=(), scratch=(), sem=None, ride=None):
    params = pltpu.CompilerParams(dimension_semantics=sem, vmem_limit_bytes=VMEM_LIMIT)
    if ride is None:
        return pl.pallas_call(body, out_shape=out_shape, grid=grid, in_specs=in_specs, out_specs=out_specs,
                              scratch_shapes=scratch, compiler_params=params, name=name, interpret=False)
    gathers, scatters = ride
    moved = list(gathers) + list(scatters)
    ng, nx = len(gathers), len(moved)
    single = not isinstance(out_shape, (list, tuple))
    outs = [out_shape] if single else list(out_shape)
    ospecs = [out_specs] if single else list(out_specs)
    n_in, n_out, n_scr = len(in_specs), len(outs), len(scratch)
    per = N_DEV - 1

    def riding(*refs):
        ins, xin = refs[:n_in], refs[n_in:n_in + nx]
        outs_r, xout = refs[n_in + nx:n_in + nx + n_out], refs[n_in + nx + n_out:n_in + 2 * nx + n_out]
        own = refs[n_in + 2 * nx + n_out:n_in + 2 * nx + n_out + n_scr]
        send_sems, recv_sems, local_sems = refs[n_in + 2 * nx + n_out + n_scr:]

        def copies():
            me, peers = _peers()
            cps = []
            for a in range(nx):
                whole = a < ng
                cps.append(pltpu.make_async_copy(xin[a] if whole else xin[a].at[me], xout[a].at[me], local_sems.at[a]))
                for k, peer in enumerate(peers):
                    dev = 4 * peer[0] + 2 * peer[1] + peer[2]
                    cps.append(pltpu.make_async_remote_copy(
                        src_ref=xin[a] if whole else xin[a].at[dev], dst_ref=xout[a].at[me],
                        send_sem=send_sems.at[a * per + k], recv_sem=recv_sems.at[a * per + k], device_id=peer,
                        device_id_type=pl.DeviceIdType.MESH))
            return cps

        if not grid:
            for cp in copies():
                cp.start()
            body(*ins, *outs_r, *own)
            for cp in copies():
                cp.wait()
            return
        ids = [pl.program_id(a) for a in range(len(grid))]
        first = functools.reduce(jnp.logical_and, [i == 0 for i in ids])
        last = functools.reduce(jnp.logical_and, [i == g - 1 for i, g in zip(ids, grid)])

        @pl.when(first)
        def _():
            for cp in copies():
                cp.start()

        body(*ins, *outs_r, *own)

        @pl.when(last)
        def _():
            for cp in copies():
                cp.wait()

    anyspec = pl.BlockSpec(memory_space=pl.ANY)
    call = pl.pallas_call(
        riding, grid=grid, in_specs=list(in_specs) + [anyspec] * nx, out_specs=ospecs + [anyspec] * nx,
        out_shape=outs + [_sds((N_DEV,) + a.shape, a.dtype) for a in gathers] + [_sds(a.shape, a.dtype) for a in scatters],
        scratch_shapes=list(scratch) + [pltpu.SemaphoreType.DMA((nx * per,)), pltpu.SemaphoreType.DMA((nx * per,)),
                                        pltpu.SemaphoreType.DMA((nx,))],
        compiler_params=params, name=name, interpret=False)

    def run(*args):
        res = call(*args, *moved)
        own = res[0] if single else list(res[:n_out])
        return own, list(res[n_out:])

    return run


def _sds(shape, dtype=f32):
    return jax.ShapeDtypeStruct(tuple(shape), dtype)


def _dot(a, b, dims, precision=None):
    return lax.dot_general(a, b, (dims, ((), ())), preferred_element_type=f32, precision=precision)


def _bdot(a, b, dims=NN):
    return _dot(a.astype(bf16), b.astype(bf16), dims)


@jax.custom_vjp
def _mm(a, w):
    return _bdot(a, w, NN)


def _mm_fwd(a, w):
    return _bdot(a, w, NN), (a, w)


def _mm_bwd(res, g):
    a, w = res
    return _bdot(g, w, NT), _bdot(a, g, TN)


_mm.defvjp(_mm_fwd, _mm_bwd)


@jax.custom_vjp
def _mm_nt(a, b):
    return _bdot(a, b, NT)


def _mm_nt_fwd(a, b):
    return _bdot(a, b, NT), (a, b)


def _mm_nt_bwd(res, g):
    a, b = res
    return _bdot(g, b, NN), _bdot(g, a, TN)


_mm_nt.defvjp(_mm_nt_fwd, _mm_nt_bwd)


@jax.custom_vjp
def _mm_tn(a, b):
    return _bdot(a, b, TN)


def _mm_tn_fwd(a, b):
    return _bdot(a, b, TN), (a, b)


def _mm_tn_bwd(res, g):
    a, b = res
    return _bdot(b, g, NT), _bdot(a, g, NN)


_mm_tn.defvjp(_mm_tn_fwd, _mm_tn_bwd)


def _rms(x, g):
    return x * lax.rsqrt(jnp.mean(x * x, axis=-1, keepdims=True) + NORM_EPS) * g


def _sigmoid(x):
    return 1.0 / (1.0 + jnp.exp(-x))


def _gelu(x):
    return 0.5 * x * (1.0 + jnp.tanh(0.7978845608028654 * (x + 0.044715 * x * x * x)))


def _softplus(x):
    return jnp.maximum(x, 0.0) + jnp.log(1.0 + jnp.exp(-jnp.abs(x)))


@jax.custom_vjp
def _group_sum(x):
    w = x.shape[-1]
    r = lax.broadcasted_iota(jnp.int32, (w, w), 0) // RW_HEAD
    c = lax.broadcasted_iota(jnp.int32, (w, w), 1) // RW_HEAD
    ones = (r == c).astype(bf16)
    hi = x.astype(bf16)
    lo = (x - hi.astype(f32)).astype(bf16)
    return _dot(jnp.concatenate([hi, lo], axis=1), jnp.concatenate([ones, ones], axis=0), NN)


_group_sum.defvjp(lambda x: (_group_sum(x), None), lambda _, g: (_group_sum(g),))


@jax.custom_vjp
def _swap(x):
    w = x.shape[-1]
    lane = lax.broadcasted_iota(jnp.int32, x.shape, 1) % 128
    lo = (lane >= 64) & (lane < 80)
    hi = (lane >= 80) & (lane < 96)
    return jnp.where(lo, pltpu.roll(x, w - 16, 1), jnp.where(hi, pltpu.roll(x, 16, 1), 0.0))


_swap.defvjp(lambda x: (_swap(x), None), lambda _, g: (_swap(g),))


def _rope(x, c, s):
    return x * c + _swap(x) * s


def _row_spec(tm, width, blk):
    return pl.BlockSpec((tm, width), lambda i, blk=blk: (i, blk))


def _full_spec(a):
    nd = a.ndim
    return pl.BlockSpec(a.shape, lambda i, nd=nd: (0,) * nd)


def _rowwise_fwd(name, f, rows, weights, outs, tm):
    n = rows[0][0].shape[0]
    nr, nw = len(rows), len(weights)

    def body(*refs):
        vals = [r[...].astype(f32) for r in refs[:nr + nw]]
        res = f(*vals)
        for o_ref, o in zip(refs[nr + nw:], res):
            o_ref[...] = o.astype(o_ref.dtype)

    return _pc(
        body, name=name, grid=(n // tm,),
        in_specs=[_row_spec(tm, w, b) for _, w, b in rows] + [_full_spec(w) for w in weights],
        out_specs=[_row_spec(tm, w, 0) for w, _ in outs],
        out_shape=[_sds((n, w), dt) for w, dt in outs], sem=("parallel",),
    )(*[a for a, _, _ in rows], *weights)


def _rowwise_bwd(name, f, rows, weights, cots, tm, drows):
    n = rows[0][0].shape[0]
    nr, nw, nc = len(rows), len(weights), len(cots)
    want = [k for k, dt in enumerate(drows) if dt is not None]

    def body(*refs):
        vals = [r[...].astype(f32) for r in refs[:nr + nw]]
        cot = tuple(r[...].astype(f32) for r in refs[nr + nw:nr + nw + nc])
        _, vjp = jax.vjp(f, *vals)
        grads = vjp(cot)
        outs = refs[nr + nw + nc:]
        for o_ref, k in zip(outs[:len(want)], want):
            o_ref[...] = grads[k].astype(o_ref.dtype)
        first = pl.program_id(0) == 0
        for o_ref, g in zip(outs[len(want):], grads[nr:]):
            @pl.when(first)
            def _(o_ref=o_ref, g=g):
                o_ref[...] = g

            @pl.when(jnp.logical_not(first))
            def _(o_ref=o_ref, g=g):
                o_ref[...] += g

    res = _pc(
        body, name=name, grid=(n // tm,),
        in_specs=[_row_spec(tm, w, b) for _, w, b in rows] + [_full_spec(w) for w in weights]
        + [_row_spec(tm, w, b) for _, w, b in cots],
        out_specs=[_row_spec(tm, rows[k][1], 0) for k in want] + [_full_spec(w) for w in weights],
        out_shape=[_sds((n, rows[k][1]), drows[k]) for k in want] + [_sds(w.shape) for w in weights],
        sem=("arbitrary",),
    )(*[a for a, _, _ in rows], *weights, *[a for a, _, _ in cots])
    return res[:len(want)], res[len(want):]


def _inproj_fwd(x2, g, w, name, ride=None):
    n = x2.shape[0]
    tm, tn = min(1024, n), 1024

    def body(x_ref, g_ref, w_ref, p_ref, h_ref):
        @pl.when(pl.program_id(1) == 0)
        def _():
            h_ref[...] = _rms(x_ref[...], g_ref[...]).astype(bf16)

        p_ref[...] = jnp.dot(h_ref[...], w_ref[...], preferred_element_type=f32)

    return _pc(
        body, name=name, grid=(n // tm, P_W // tn),
        in_specs=[pl.BlockSpec((tm, D), lambda i, j: (i, 0)), pl.BlockSpec((1, D), lambda i, j: (0, 0)),
                  pl.BlockSpec((D, tn), lambda i, j: (0, j))],
        out_specs=[pl.BlockSpec((tm, tn), lambda i, j: (i, j)), pl.BlockSpec((tm, D), lambda i, j: (i, 0))],
        out_shape=[_sds((n, P_W)), _sds((n, D), bf16)], sem=("parallel", "arbitrary"), ride=ride,
    )(x2, g, w)


def _norm_matmul_bwd(dy, w, x2, g, dres, name, ride=None):
    n, k = dy.shape
    tm = min(1024, n)
    tk = 1024 if k % 1024 == 0 else 1408
    nk = k // tk

    def body(dy_ref, w_ref, x_ref, g_ref, dr_ref, dx_ref, dg_ref, acc):
        i, j = pl.program_id(0), pl.program_id(1)

        @pl.when(j == 0)
        def _():
            acc[...] = jnp.zeros_like(acc)

        @pl.when((i == 0) & (j == 0))
        def _():
            dg_ref[...] = jnp.zeros_like(dg_ref)

        acc[...] += _dot(dy_ref[...], w_ref[...], NT)

        @pl.when(j == nk - 1)
        def _():
            _, vjp = jax.vjp(_rms, x_ref[...], g_ref[...])
            dx, dg = vjp(acc[...])
            dx_ref[...] = dr_ref[...] + dx
            dg_ref[...] += dg

    return _pc(
        body, name=name, grid=(n // tm, nk),
        in_specs=[pl.BlockSpec((tm, tk), lambda i, j: (i, j)), pl.BlockSpec((D, tk), lambda i, j: (0, j)),
                  pl.BlockSpec((tm, D), lambda i, j: (i, 0)), pl.BlockSpec((1, D), lambda i, j: (0, 0)),
                  pl.BlockSpec((tm, D), lambda i, j: (i, 0))],
        out_specs=[pl.BlockSpec((tm, D), lambda i, j: (i, 0)), pl.BlockSpec((1, D), lambda i, j: (0, 0))],
        out_shape=[_sds((n, D)), _sds((1, D))], scratch=[pltpu.VMEM((tm, D), f32)], sem=("arbitrary", "arbitrary"),
        ride=ride,
    )(dy, w, x2, g, dres)


def _matmul_tn(a, g, name, ride=None):
    n, k = a.shape
    m = g.shape[1]
    tr = min(1024, n)
    tk = k if k <= 1024 else 1408
    tn = m if m <= 1024 else (1024 if m % 1024 == 0 else 1408)
    nr = n // tr

    def body(a_ref, g_ref, o_ref):
        @pl.when(pl.program_id(2) == 0)
        def _():
            o_ref[...] = jnp.zeros_like(o_ref)

        o_ref[...] += _bdot(a_ref[...], g_ref[...], TN)

    return _pc(
        body, name=name, grid=(k // tk, m // tn, nr),
        in_specs=[pl.BlockSpec((tr, tk), lambda i, j, r: (r, i)), pl.BlockSpec((tr, tn), lambda i, j, r: (r, j))],
        out_specs=pl.BlockSpec((tk, tn), lambda i, j, r: (i, j)),
        out_shape=_sds((k, m)), sem=("parallel", "parallel", "arbitrary"), ride=ride,
    )(a, g)


def _f_mla_proj(ckv, slab, pq, c, s, qg, kg, wq, wk, wv):
    c8, s8 = jnp.concatenate([c] * HEADS, axis=1), jnp.concatenate([s] * HEADS, axis=1)
    q = _rope(_mm(_rms(pq, qg), wq), c8, s8)
    cn = _rms(ckv, kg)
    k = _mm(cn, wk) + jnp.concatenate([_rope(slab, c, s)] * HEADS, axis=1)
    return q, k, _mm(cn, wv)


def _attn_fwd(q, k, v, bsz, seq, name, ride=None):
    n = q.shape[0]
    tq = min(256, seq)
    nq = seq // tq

    def body(q_ref, k_ref, v_ref, o_ref, lse_ref):
        lane = lax.broadcasted_iota(jnp.int32, (tq, 128), 1) < 64
        vv = v_ref[...]
        two = range(2)
        s = [_dot(q_ref[:, h * 128:(h + 1) * 128], k_ref[:, h * 128:(h + 1) * 128], NT) * ATT_SCALE for h in two]
        m = [jnp.max(s[h], axis=-1, keepdims=True) for h in two]
        e = [jnp.exp(s[h] - m[h]) for h in two]
        l = [jnp.sum(e[h], axis=-1, keepdims=True) for h in two]
        p = [(e[h] / l[h]).astype(bf16) for h in two]
        outs = [_dot(p[h], vv, NN) for h in two]
        o_ref[...] = jnp.where(lane, outs[0], outs[1])
        lse_ref[...] = jnp.where(lane, m[0] + jnp.log(l[0]), m[1] + jnp.log(l[1]))

    row = pl.BlockSpec((tq, 128), lambda b, h, i: (b * nq + i, h))
    return _pc(
        body, name=name, grid=(bsz, HEADS // 2, nq),
        in_specs=[pl.BlockSpec((tq, 256), lambda b, h, i: (b * nq + i, h)),
                  pl.BlockSpec((seq, 256), lambda b, h, i: (b, h)),
                  pl.BlockSpec((seq, 128), lambda b, h, i: (b, h))],
        out_specs=[row, row],
        out_shape=[_sds((n, HEADS * V_HEAD)), _sds((n, HEADS * V_HEAD))], sem=("parallel", "parallel", "parallel"),
        ride=ride,
    )(q, k, v)


def _attn_bwd(q, k, v, o, lse, do, bsz, seq, name, ride=None):
    n = q.shape[0]
    tq = min(256, seq)
    nq = seq // tq

    def body(q_ref, k_ref, v_ref, o_ref, lse_ref, do_ref, dq_ref, dk_ref, dv_ref):
        @pl.when(pl.program_id(2) == 0)
        def _():
            dk_ref[...] = jnp.zeros_like(dk_ref)
            dv_ref[...] = jnp.zeros_like(dv_ref)

        lane = lax.broadcasted_iota(jnp.int32, (tq, 128), 1) < 64
        vv = v_ref[...]
        for h in range(2):
            qh, kh = q_ref[:, h * 128:(h + 1) * 128], k_ref[:, h * 128:(h + 1) * 128]
            p = jnp.exp(_dot(qh, kh, NT) * ATT_SCALE - lse_ref[:, 64 * h:64 * h + 1])
            doh = jnp.where(lane if h == 0 else jnp.logical_not(lane), do_ref[...], 0.0)
            delta = jnp.sum(doh * o_ref[...], axis=-1, keepdims=True)
            dob = doh.astype(bf16)
            dp = _dot(dob, vv, NT)
            ds = (p * (dp - delta) * ATT_SCALE).astype(bf16)
            dq_ref[:, h * 128:(h + 1) * 128] = _dot(ds, kh, NN)
            dk_ref[:, h * 128:(h + 1) * 128] += _dot(ds, qh, TN)
            dv_ref[...] += _dot(p.astype(bf16), dob, TN)

    return _pc(
        body, name=name, grid=(bsz, HEADS // 2, nq),
        in_specs=[pl.BlockSpec((tq, 256), lambda b, h, i: (b * nq + i, h)),
                  pl.BlockSpec((seq, 256), lambda b, h, i: (b, h)),
                  pl.BlockSpec((seq, 128), lambda b, h, i: (b, h)),
                  pl.BlockSpec((tq, 128), lambda b, h, i: (b * nq + i, h)),
                  pl.BlockSpec((tq, 128), lambda b, h, i: (b * nq + i, h)),
                  pl.BlockSpec((tq, 128), lambda b, h, i: (b * nq + i, h))],
        out_specs=[pl.BlockSpec((tq, 256), lambda b, h, i: (b * nq + i, h)),
                   pl.BlockSpec((seq, 256), lambda b, h, i: (b, h)),
                   pl.BlockSpec((seq, 128), lambda b, h, i: (b, h))],
        out_shape=[_sds((n, HEADS * 128)), _sds((n, HEADS * 128)), _sds((n, HEADS * V_HEAD))],
        sem=("parallel", "parallel", "arbitrary"), ride=ride,
    )(q, k, v, o, lse, do)


@jax.custom_vjp
def _group_mix(vv, *ws):
    lane = lax.broadcasted_iota(jnp.int32, (SG_CHUNK, 128), 1) < 64
    slabs = []
    for j in range(SG_GROUPS // 2):
        vp = vv[:, 128 * j:128 * (j + 1)]
        slabs.append(jnp.where(lane, _bdot(ws[2 * j], vp), _bdot(ws[2 * j + 1], vp)))
    return jnp.concatenate(slabs, axis=1)


def _group_mix_bwd(res, g):
    vv, ws = res
    lane = lax.broadcasted_iota(jnp.int32, (SG_CHUNK, 128), 1) < 64
    dvs, dws = [], []
    for j in range(SG_GROUPS // 2):
        vp, gp = vv[:, 128 * j:128 * (j + 1)], g[:, 128 * j:128 * (j + 1)]
        lo, hi = jnp.where(lane, gp, 0.0), jnp.where(lane, 0.0, gp)
        dvs.append(_bdot(ws[2 * j], lo, TN) + _bdot(ws[2 * j + 1], hi, TN))
        dws += [_bdot(lo, vp, NT), _bdot(hi, vp, NT)]
    return (jnp.concatenate(dvs, axis=1), *dws)


_group_mix.defvjp(lambda vv, *ws: (_group_mix(vv, *ws), (vv, ws)), _group_mix_bwd)


def _f_sg(pu, pv, lg, lb, bias, *ws):
    u, vv = _gelu(pu), _gelu(pv)
    mu = jnp.mean(vv, axis=-1, keepdims=True)
    d = vv - mu
    vv = d * lax.rsqrt(jnp.mean(d * d, axis=-1, keepdims=True) + LN_EPS) * lg + lb
    return (u * (bias + _group_mix(vv, *ws)),)


def _shift_mean(a, prev_row, next_row):
    t = a.shape[0]
    row = lax.broadcasted_iota(jnp.int32, a.shape, 0)
    prev = jnp.where(row == 0, prev_row, pltpu.roll(a, 1, 0))
    nxt = jnp.where(row == t - 1, next_row, pltpu.roll(a, t - 1, 0))
    return 0.5 * (prev + nxt)


def _halo_specs(tm, width, blk, nblk8):
    h = tm // 8
    return [pl.BlockSpec((tm, width), lambda i: (i, blk)),
            pl.BlockSpec((8, width), lambda i: (jnp.maximum(i * h - 1, 0), blk)),
            pl.BlockSpec((8, width), lambda i: (jnp.minimum((i + 1) * h, nblk8 - 1), blk))]


def _edge_rows(i, tm, seq, pv_ref, nx_ref, scale=None):
    first = (i * tm) % seq == 0
    last = ((i + 1) * tm) % seq == 0
    pv, nx = pv_ref[7:8, :], nx_ref[0:1, :]
    if scale is not None:
        pv, nx = pv * scale, nx * scale
    return jnp.where(first, 0.0, pv), jnp.where(last, 0.0, nx)


def _shift_fwd(p, mu, seq, name):
    n = p.shape[0]
    tm = min(256, seq)
    blk = O_RW // RW_W

    def body(x_ref, pv_ref, nx_ref, mu_ref, z_ref):
        x = x_ref[...]
        pv, nx = _edge_rows(pl.program_id(0), tm, seq, pv_ref, nx_ref)
        z_ref[...] = x + mu_ref[...] * (_shift_mean(x, pv, nx) - x)

    return _pc(
        body, name=name, grid=(n // tm,),
        in_specs=_halo_specs(tm, RW_W, blk, n // 8) + [pl.BlockSpec((1, RW_W), lambda i: (0, 0))],
        out_specs=pl.BlockSpec((tm, RW_W), lambda i: (i, 0)), out_shape=_sds((n, RW_W)), sem=("parallel",),
    )(p, p, p, mu)


def _shift_bwd(dz, p, mu, seq, name):
    n = p.shape[0]
    tm = min(256, seq)
    blk = O_RW // RW_W

    def body(dz_ref, dpv_ref, dnx_ref, x_ref, pv_ref, nx_ref, mu_ref, dx_ref, dmu_ref):
        i = pl.program_id(0)
        mu_v = mu_ref[...]
        dzv = dz_ref[...]
        m = dzv * mu_v
        mpv, mnx = _edge_rows(i, tm, seq, dpv_ref, dnx_ref, mu_v)
        dx_ref[...] = (dzv - m + _shift_mean(m, mpv, mnx)).astype(dx_ref.dtype)
        x = x_ref[...]
        pv, nx = _edge_rows(i, tm, seq, pv_ref, nx_ref)
        part = jnp.sum(dzv * (_shift_mean(x, pv, nx) - x), axis=0, keepdims=True)

        @pl.when(i == 0)
        def _():
            dmu_ref[...] = part

        @pl.when(i != 0)
        def _():
            dmu_ref[...] += part

    return _pc(
        body, name=name, grid=(n // tm,),
        in_specs=_halo_specs(tm, RW_W, 0, n // 8) + _halo_specs(tm, RW_W, blk, n // 8)
        + [pl.BlockSpec((1, RW_W), lambda i: (0, 0))],
        out_specs=[pl.BlockSpec((tm, RW_W), lambda i: (i, 0)), pl.BlockSpec((1, RW_W), lambda i: (0, 0))],
        out_shape=[_sds((n, RW_W), bf16), _sds((1, RW_W))], sem=("arbitrary",),
    )(dz, dz, dz, p, p, p, mu)


def _f_rw_pre(k, wl, al, gl, w0, a0, w2, a2, g2, k_k, k_a):
    w = w0 + _mm(jnp.tanh(wl), w2)
    lw = -jnp.exp(-_softplus(-w) - 0.5)
    a = _sigmoid(a0 + _mm(al, a2))
    g = _mm(_sigmoid(gl), g2)
    kkr = k * k_k
    kk = kkr / jnp.maximum(jnp.sqrt(_group_sum(kkr * kkr)), 1e-12)
    two = lambda t: jnp.concatenate([t, t], axis=1)
    kd = two(k) * (1.0 + (a - 1.0) * two(k_a))
    bd = two(kk) * a
    return lw, kd, kk, bd, g


def _f_rw_post(y0, y1, r, v, kd0, kd1, g, r_k, ln_g, ln_b):
    y = y0 + y1
    mean = _group_sum(y) * (1.0 / RW_HEAD)
    d = y - mean
    var = _group_sum(d * d) * (1.0 / RW_HEAD)
    yn = d * lax.rsqrt(var + GN_EPS) * ln_g + ln_b
    bonus = _group_sum(r * (kd0 + kd1) * r_k)
    return ((yn + bonus * v) * g,)


@jax.custom_vjp
def _tri_inv(mats):
    c = mats[0].shape[0]
    row = lax.broadcasted_iota(jnp.int32, (c, c), 0)
    col = lax.broadcasted_iota(jnp.int32, (c, c), 1)
    eye = (row == col).astype(f32)
    blk = lambda b: (row // b) == (col // b)
    ld = [jnp.where(blk(8), a, 0.0) for a in mats]
    l2 = [_bdot(x, x) for x in ld]
    l4 = [_bdot(x, x) for x in l2]
    t = [_bdot(eye - x, eye + y) for x, y in zip(ld, l2)]
    t = [_bdot(x, eye + y) for x, y in zip(t, l4)]
    b = 8
    while b < c:
        sub = blk(2 * b) & jnp.logical_not(blk(b))
        p = [_bdot(x, jnp.where(sub, a, 0.0)) for x, a in zip(t, mats)]
        t = [x - _bdot(y, x) for x, y in zip(t, p)]
        b *= 2
    return tuple(t)


def _tri_inv_fwd(mats):
    t = _tri_inv(mats)
    return t, t


def _tri_inv_bwd(ts, gs):
    p = [_bdot(t, g, TN) for t, g in zip(ts, gs)]
    return (tuple(-_bdot(x, t, NT) for x, t in zip(p, ts)),)


_tri_inv.defvjp(_tri_inv_fwd, _tri_inv_bwd)


@jax.custom_vjp
def _tri_inv_saved(mats, ts):
    return ts


_tri_inv_saved.defvjp(lambda mats, ts: (ts, ts),
                      lambda ts, gs: (_tri_inv_bwd(ts, gs)[0], tuple(jnp.zeros_like(t) for t in ts)))


def _split3(x):
    h = x.astype(bf16)
    r = x - h.astype(f32)
    m = r.astype(bf16)
    return h, m, (r - m.astype(f32)).astype(bf16)


@jax.custom_vjp
def _mask_mm(mask, x):
    mb = mask.astype(bf16)
    return _dot(jnp.concatenate([mb, mb, mb], axis=1), jnp.concatenate(_split3(x), axis=0), NN)


def _mask_mm_bwd(mask, g):
    mb = mask.astype(bf16)
    return jnp.zeros_like(mask), _dot(jnp.concatenate([mb, mb, mb], axis=0), jnp.concatenate(_split3(g), axis=0), TN)


_mask_mm.defvjp(lambda mask, x: (_mask_mm(mask, x), mask), _mask_mm_bwd)


@jax.custom_vjp
def _split_lanes(x):
    h = x.shape[1] // 2
    return x[:, :h], x[:, h:]


_split_lanes.defvjp(lambda x: (_split_lanes(x), None), lambda _, g: (jnp.concatenate(g, axis=1),))


def _scan_chunk(s0, r, v, kk, lw, kd, bd, rev, inv=None):
    n = len(r)
    each = range(n)
    c = r[0].shape[0]
    row = lax.broadcasted_iota(jnp.int32, (c, 2 * c), 0)
    col = lax.broadcasted_iota(jnp.int32, (c, 2 * c), 1) % c
    ahead = jnp.where(rev, col - row, row - col)
    before = ahead > 0
    incl = ahead >= 0
    lane = lax.broadcasted_iota(jnp.int32, (1, 128), 1)
    m0 = (lane < 64).astype(f32)
    heads = lambda t: jnp.concatenate([t * m0, t * (1.0 - m0)], axis=0)
    bd_mask = ((lax.broadcasted_iota(jnp.int32, (128, 128), 0) // 64)
               == (lax.broadcasted_iota(jnp.int32, (128, 128), 1) // 64)).astype(f32)
    tot = [jnp.sum(lw[i], axis=0, keepdims=True) for i in each]
    row1 = lax.broadcasted_iota(jnp.int32, (c, c), 0)
    col1 = lax.broadcasted_iota(jnp.int32, (c, c), 1)
    upto = (jnp.where(rev, col1 - row1, row1 - col1) >= 0).astype(f32)
    lp = [_mask_mm(upto, lw[i]) - 0.5 * tot[i] for i in each]
    eg = [jnp.exp(lp[i]) for i in each]
    ieg = [jnp.exp(-lp[i]) for i in each]
    rt = [r[i] * eg[i] for i in each]
    kt = [kd[i] * ieg[i] for i in each]
    bt = [bd[i] * ieg[i] for i in each]
    at = [kk[i] * jnp.exp(lp[i] - lw[i]) for i in each]
    etot = [jnp.exp(0.5 * tot[i]) for i in each]
    si = [s0[i] * etot[i] for i in each]
    bth = [heads(bt[i]) for i in each]
    kth = [heads(kt[i]) for i in each]
    vh = [heads(v[i]) for i in each]
    a_ab = [jnp.where(before, _mm_nt(at[i], bth[i]), 0.0) for i in each]
    a_ak = [jnp.where(before, _mm_nt(at[i], kth[i]), 0.0) for i in each]
    a_rb = [jnp.where(incl, _mm_nt(rt[i], bth[i]), 0.0) for i in each]
    a_rk = [jnp.where(incl, _mm_nt(rt[i], kth[i]), 0.0) for i in each]
    halves = [_split_lanes(a_ab[i]) for i in each]
    mats = tuple(m for pair in halves for m in pair)
    inv = _tri_inv(mats) if inv is None else _tri_inv_saved(mats, inv)
    t = [jnp.concatenate([inv[2 * i], inv[2 * i + 1]], axis=1) for i in each]
    x0 = [_mm_nt(at[i], si[i]) for i in each]
    x = [x0[i] + _mm(a_ak[i], vh[i]) for i in each]
    u = [-_mm(t[i], heads(x[i])) for i in each]
    y0 = [_mm_nt(rt[i], si[i]) for i in each]
    y = [y0[i] + _mm(jnp.concatenate([a_rb[i], a_rk[i]], axis=1), jnp.concatenate([heads(u[i]), vh[i]], axis=0))
         for i in each]
    ds = [_mm_tn(jnp.concatenate([u[i], v[i]], axis=0), jnp.concatenate([bt[i], kt[i]], axis=0)) for i in each]
    se = [(si[i] + ds[i] * bd_mask) * etot[i] for i in each]
    return tuple(y), tuple(se), inv


PAIRS = HEADS // 2


def _scan_specs(nc, bsz, flip=False):
    def cc(d, c):
        c = nc - 1 - c if flip else c
        return jnp.where(d == 0, c, nc - 1 - c)

    rowblk = lambda d, b, c: b * nc + cc(d, c)
    zspec = lambda blk: pl.BlockSpec((CHUNK, RW_DIM), lambda d, b, c: (rowblk(d, b, c), blk))
    dspec = pl.BlockSpec((CHUNK, RW_DIM), lambda d, b, c: (rowblk(d, b, c), d))
    yspec = pl.BlockSpec((None, CHUNK, RW_DIM), lambda d, b, c: (d, rowblk(d, b, c), 0))
    sspec = pl.BlockSpec((None, PAIRS, 128, 128), lambda d, b, c: ((d * bsz + b) * nc + cc(d, c), 0, 0, 0))
    tspec = pl.BlockSpec((None, PAIRS, 128, 256), lambda d, b, c: ((d * bsz + b) * nc + cc(d, c), 0, 0, 0))
    return zspec, dspec, yspec, sspec, tspec


def _scan_fwd(z, lw, kd, kk, bd, bsz, seq, name, ride=None):
    n = z.shape[0]
    nc = seq // CHUNK
    zspec, dspec, yspec, sspec, tspec = _scan_specs(nc, bsz)

    def body(r_ref, v_ref, kk_ref, lw_ref, kd_ref, bd_ref, y_ref, s_ref, t_ref, st):
        @pl.when(pl.program_id(2) == 0)
        def _():
            st[...] = jnp.zeros_like(st)

        rev = pl.program_id(0) == 1
        lanes = [slice(h * 128, (h + 1) * 128) for h in range(PAIRS)]
        s0 = tuple(st[h] for h in range(PAIRS))
        ops = [tuple(ref[:, ln] for ln in lanes) for ref in (r_ref, v_ref, kk_ref, lw_ref, kd_ref, bd_ref)]
        y, se, inv = _scan_chunk(s0, *ops, rev)
        for h, ln in enumerate(lanes):
            s_ref[h] = s0[h]
            t_ref[h, :, :128] = inv[2 * h]
            t_ref[h, :, 128:] = inv[2 * h + 1]
            y_ref[:, ln] = y[h]
            st[h] = se[h]

    return _pc(
        body, name=name, grid=(2, bsz, nc),
        in_specs=[zspec(0), zspec(2), zspec(0), dspec, dspec, dspec],
        out_specs=[yspec, sspec, tspec],
        out_shape=[_sds((2, n, RW_DIM)), _sds((2 * bsz * nc, PAIRS, 128, 128)), _sds((2 * bsz * nc, PAIRS, 128, 256))],
        scratch=[pltpu.VMEM((PAIRS, 128, 128), f32)], sem=("parallel", "parallel", "arbitrary"), ride=ride,
    )(z, z, kk, lw, kd, bd)


def _scan_bwd(z, lw, kd, kk, bd, s_in, t_in, dy, bsz, seq, name, ride=None):
    n = z.shape[0]
    nc = seq // CHUNK
    zspec, dspec, yspec, sspec, tspec = _scan_specs(nc, bsz, flip=True)

    def body(r_ref, v_ref, kk_ref, lw_ref, kd_ref, bd_ref, s_ref, t_ref, dy_ref,
             dr_ref, dv_ref, dkk_ref, dlw_ref, dkd_ref, dbd_ref, dst):
        @pl.when(pl.program_id(2) == 0)
        def _():
            dst[...] = jnp.zeros_like(dst)

        rev = pl.program_id(0) == 1
        lanes = [slice(h * 128, (h + 1) * 128) for h in range(PAIRS)]
        s0 = tuple(s_ref[h] for h in range(PAIRS))
        inv = tuple(t_ref[h, :, a * 128:(a + 1) * 128] for h in range(PAIRS) for a in range(2))
        ops = [tuple(ref[:, ln] for ln in lanes) for ref in (r_ref, v_ref, kk_ref, lw_ref, kd_ref, bd_ref)]
        cot = (tuple(dy_ref[:, ln] for ln in lanes), tuple(dst[h] for h in range(PAIRS)))
        _, vjp = jax.vjp(lambda *a: _scan_chunk(*a, rev=rev, inv=inv)[:2], s0, *ops)
        grads = vjp(cot)
        for h, ln in enumerate(lanes):
            dst[h] = grads[0][h]
            for o_ref, g in zip((dr_ref, dv_ref, dkk_ref, dlw_ref, dkd_ref, dbd_ref), grads[1:]):
                o_ref[:, ln] = g[h]

    return _pc(
        body, name=name, grid=(2, bsz, nc),
        in_specs=[zspec(0), zspec(2), zspec(0), dspec, dspec, dspec, sspec, tspec, zspec(0)],
        out_specs=[yspec, yspec, yspec, dspec, dspec, dspec],
        out_shape=[_sds((2, n, RW_DIM))] * 3 + [_sds((n, 2 * RW_DIM))] * 3,
        scratch=[pltpu.VMEM((PAIRS, 128, 128), f32)], sem=("parallel", "parallel", "arbitrary"), ride=ride,
    )(z, z, kk, lw, kd, bd, s_in, t_in, dy)


def _merge_fwd(x2, p, ya, yb, yc, gb, wb, wo, name):
    n = x2.shape[0]
    tm = min(256, n)

    def body(x_ref, pg_ref, ya_ref, yb_ref, yc_ref, gb_ref, wb_ref, wo_ref, o_ref):
        gates = _sigmoid(pg_ref[...] + gb_ref[...])
        merged = jnp.zeros((tm, D), f32)
        for k, y_ref in enumerate((ya_ref, yb_ref, yc_ref)):
            merged += gates[:, k * D:(k + 1) * D] * _bdot(y_ref[...], wb_ref[k])
        o_ref[...] = x_ref[...] + _bdot(merged, wo_ref[...])

    row = lambda w, b=0: pl.BlockSpec((tm, w), lambda i, b=b: (i, b))
    return _pc(
        body, name=name, grid=(n // tm,),
        in_specs=[row(D), row(3 * D, O_GATE // (3 * D)), row(512), row(512), row(512),
                  pl.BlockSpec((1, 3 * D), lambda i: (0, 0)), pl.BlockSpec((3, 512, D), lambda i: (0, 0, 0)),
                  pl.BlockSpec((D, D), lambda i: (0, 0))],
        out_specs=row(D), out_shape=_sds((n, D)), sem=("parallel",),
    )(x2, p, ya, yb, yc, gb, wb, wo)


def _merge_bwd(dx1, p, ya, yb, yc, gb, wb, wo, name):
    n = dx1.shape[0]
    tm = min(256, n)

    def body(dx_ref, pg_ref, ya_ref, yb_ref, yc_ref, gb_ref, wb_ref, wo_ref,
             dpg_ref, dya_ref, dyb_ref, dyc_ref, dt_ref, mg_ref, dgb_ref):
        gates = _sigmoid(pg_ref[...] + gb_ref[...])
        dmerged = _bdot(dx_ref[...], wo_ref[...], NT)
        merged = jnp.zeros((tm, D), f32)
        dpg = []
        for k, (y_ref, dy_ref) in enumerate(((ya_ref, dya_ref), (yb_ref, dyb_ref), (yc_ref, dyc_ref))):
            gk = gates[:, k * D:(k + 1) * D]
            tk = _bdot(y_ref[...], wb_ref[k])
            merged += gk * tk
            dpg.append(dmerged * tk * gk * (1.0 - gk))
            dtk = dmerged * gk
            dt_ref[:, k * D:(k + 1) * D] = dtk.astype(bf16)
            dy_ref[...] = _bdot(dtk, wb_ref[k], NT)
        dpg = jnp.concatenate(dpg, axis=1)
        dpg_ref[...] = dpg.astype(bf16)
        mg_ref[...] = merged.astype(bf16)
        part = jnp.sum(dpg, axis=0, keepdims=True)

        @pl.when(pl.program_id(0) == 0)
        def _():
            dgb_ref[...] = part

        @pl.when(pl.program_id(0) != 0)
        def _():
            dgb_ref[...] += part

    row = lambda w, b=0: pl.BlockSpec((tm, w), lambda i, b=b: (i, b))
    return _pc(
        body, name=name, grid=(n // tm,),
        in_specs=[row(D), row(3 * D, O_GATE // (3 * D)), row(512), row(512), row(512),
                  pl.BlockSpec((1, 3 * D), lambda i: (0, 0)), pl.BlockSpec((3, 512, D), lambda i: (0, 0, 0)),
                  pl.BlockSpec((D, D), lambda i: (0, 0))],
        out_specs=[row(3 * D), row(512), row(512), row(512), row(3 * D), row(D),
                   pl.BlockSpec((1, 3 * D), lambda i: (0, 0))],
        out_shape=[_sds((n, 3 * D), bf16), _sds((n, 512)), _sds((n, 512)), _sds((n, 512)), _sds((n, 3 * D), bf16),
                   _sds((n, D), bf16), _sds((1, 3 * D))],
        sem=("arbitrary",),
    )(dx1, p, ya, yb, yc, gb, wb, wo)


FF_T = 1408


def _ffn_fwd(x1, g, wg, wu, wd, name):
    n = x1.shape[0]
    tm = min(512, n)
    nf = D_FF // FF_T

    def body(x_ref, g_ref, wg_ref, wu_ref, wd_ref, o_ref, hs):
        j = pl.program_id(1)

        @pl.when(j == 0)
        def _():
            hs[...] = _rms(x_ref[...], g_ref[...]).astype(bf16)
            o_ref[...] = x_ref[...]

        a = _dot(hs[...], wg_ref[...], NN)
        b = _dot(hs[...], wu_ref[...], NN)
        o_ref[...] += _bdot(a * _sigmoid(a) * b, wd_ref[...])

    return _pc(
        body, name=name, grid=(n // tm, nf),
        in_specs=[pl.BlockSpec((tm, D), lambda i, j: (i, 0)), pl.BlockSpec((1, D), lambda i, j: (0, 0)),
                  pl.BlockSpec((D, FF_T), lambda i, j: (0, j)), pl.BlockSpec((D, FF_T), lambda i, j: (0, j)),
                  pl.BlockSpec((FF_T, D), lambda i, j: (j, 0))],
        out_specs=pl.BlockSpec((tm, D), lambda i, j: (i, 0)), out_shape=_sds((n, D)),
        scratch=[pltpu.VMEM((tm, D), bf16)], sem=("parallel", "arbitrary"),
    )(x1, g, wg, wu, wd)


def _ffn_bwd(dx2, x1, g, wg, wu, wd, name):
    n = x1.shape[0]
    tm = min(512, n)
    nf = D_FF // FF_T

    def body(dx_ref, x_ref, g_ref, wg_ref, wu_ref, wd_ref, dx1_ref, dg_ref, h_ref, da_ref, db_ref, hm_ref, acc):
        i, j = pl.program_id(0), pl.program_id(1)

        @pl.when(j == 0)
        def _():
            h_ref[...] = _rms(x_ref[...], g_ref[...]).astype(bf16)
            acc[...] = jnp.zeros_like(acc)

        @pl.when((i == 0) & (j == 0))
        def _():
            dg_ref[...] = jnp.zeros_like(dg_ref)

        h = h_ref[...]
        a = _dot(h, wg_ref[...], NN)
        b = _dot(h, wu_ref[...], NN)
        sg = _sigmoid(a)
        s = a * sg
        dhm = _bdot(dx_ref[...], wd_ref[...], NT)
        da = (dhm * b * (sg * (1.0 + a * (1.0 - sg)))).astype(bf16)
        db = (dhm * s).astype(bf16)
        da_ref[...] = da
        db_ref[...] = db
        hm_ref[...] = (s * b).astype(bf16)
        acc[...] += _dot(da, wg_ref[...], NT) + _dot(db, wu_ref[...], NT)

        @pl.when(j == nf - 1)
        def _():
            _, vjp = jax.vjp(_rms, x_ref[...], g_ref[...])
            dx, dg = vjp(acc[...])
            dx1_ref[...] = dx_ref[...] + dx
            dg_ref[...] += dg

    rowf = pl.BlockSpec((tm, FF_T), lambda i, j: (i, j))
    rowd = pl.BlockSpec((tm, D), lambda i, j: (i, 0))
    vec = pl.BlockSpec((1, D), lambda i, j: (0, 0))
    return _pc(
        body, name=name, grid=(n // tm, nf),
        in_specs=[rowd, rowd, vec, pl.BlockSpec((D, FF_T), lambda i, j: (0, j)),
                  pl.BlockSpec((D, FF_T), lambda i, j: (0, j)), pl.BlockSpec((FF_T, D), lambda i, j: (j, 0))],
        out_specs=[rowd, vec, rowd, rowf, rowf, rowf],
        out_shape=[_sds((n, D)), _sds((1, D)), _sds((n, D), bf16), _sds((n, D_FF), bf16), _sds((n, D_FF), bf16),
                   _sds((n, D_FF), bf16)],
        scratch=[pltpu.VMEM((tm, D), f32)], sem=("arbitrary", "arbitrary"),
    )(dx2, x1, g, wg, wu, wd)


def _loss_head(x2, g, tgt, name):
    n = x2.shape[0]
    tm = min(512, n)

    def f(x, gg, t):
        e = _rms(x, gg) - t
        return 0.5 * jnp.sum(jnp.mean(e * e, axis=-1, keepdims=True))

    def body(x_ref, g_ref, t_ref, l_ref, dx_ref, dg_ref):
        val, vjp = jax.vjp(f, x_ref[...], g_ref[...], t_ref[...])
        dx, dg, _ = vjp(jnp.ones((), f32))
        dx_ref[...] = dx

        @pl.when(pl.program_id(0) == 0)
        def _():
            l_ref[...] = jnp.zeros_like(l_ref)
            dg_ref[...] = jnp.zeros_like(dg_ref)

        l_ref[...] += val
        dg_ref[...] += dg

    rowd = pl.BlockSpec((tm, D), lambda i: (i, 0))
    return _pc(
        body, name=name, grid=(n // tm,),
        in_specs=[rowd, pl.BlockSpec((1, D), lambda i: (0, 0)), rowd],
        out_specs=[pl.BlockSpec((8, 128), lambda i: (0, 0)), rowd, pl.BlockSpec((1, D), lambda i: (0, 0))],
        out_shape=[_sds((8, 128)), _sds((n, D)), _sds((1, D))], sem=("arbitrary",),
    )(x2, g, tgt)


def _adamw(w, parts, m, v, name):
    nl, r, c = w.shape
    tr = r
    for cand in (1024, 512, 256, 128, 64, 32, 16, 8):
        if r % cand == 0 and cand * c * 4 <= 1024 * 1024:
            tr = cand
            break

    def body(*refs):
        w_ref, p_refs, (m_ref, v_ref, g_ref, d_ref, nm_ref, nv_ref) = refs[0], refs[1:1 + nl], refs[1 + nl:]

        def update(p_ref):
            gg = p_ref[0].astype(f32)
            for k in range(1, N_DEV):
                gg = gg + p_ref[k].astype(f32)
            g_ref[...] = gg
            nm = B1 * m_ref[...] + (1.0 - B1) * gg
            nv = B2 * v_ref[...] + (1.0 - B2) * (gg * gg)
            m_hat = nm / (1.0 - B1 ** STEP)
            v_hat = nv / (1.0 - B2 ** STEP)
            d_ref[...] = -LR * (m_hat / (jnp.sqrt(v_hat) + EPS) + WD * w_ref[...])
            nm_ref[...] = nm
            nv_ref[...] = nv

        for j in range(nl):
            pl.when(pl.program_id(0) == j)(functools.partial(update, p_refs[j]))

    spec = pl.BlockSpec((None, tr, c), lambda l, i: (l, i, 0))
    pspecs = [pl.BlockSpec((N_DEV, tr, c), lambda l, i, j=j: (0, jnp.where(l == j, i, 0), 0)) for j in range(nl)]
    return _pc(body, name=name, grid=(nl, r // tr), in_specs=[spec] + pspecs + [spec, spec], out_specs=[spec] * 4,
               out_shape=[_sds((nl, r, c))] * 4, sem=("arbitrary", "arbitrary"))(w, *parts, m, v)


def _peers():
    x, y, c = lax.axis_index("x"), lax.axis_index("y"), lax.axis_index("c")
    me = 4 * x + 2 * y + c
    peers = []
    for k in range(1, N_DEV):
        fx, fy, fc = (k >> 2) & 1, (k >> 1) & 1, k & 1
        peers.append(((1 - x) if fx else x, (1 - y) if fy else y, (1 - c) if fc else c))
    return me, peers


def _exchange(gathers, scatters, name):
    _, got = _pc(lambda: None, name=name, out_shape=[], ride=(gathers, scatters))()
    return got


SHARDED = {"w_in": 2, "gate_b": 2, "w_uq": 2, "w_ukv": 2, "rw_w0": 2, "rw_w2": 3, "rw_a0": 2, "rw_a2": 3, "rw_g2": 2,
           "w_branch": 3, "w_out": 1, "w_ffn_gate": 2, "w_ffn_up": 2, "w_ffn_down": 1}
GATHER_F32 = ("gate_b", "rw_w0", "rw_a0")
REPLICATED = ("attn_norm_g", "q_norm_g", "kv_norm_g", "sg_ln_g", "sg_ln_b", "sg_w", "sg_b", "rw_mu", "rw_k_k", "rw_k_a",
              "rw_r_k", "rw_ln_g", "rw_ln_b", "ffn_norm_g", "final_norm_g")
WEIGHTS = ("attn_norm_g", "w_in", "gate_b", "q_norm_g", "w_uq", "kv_norm_g", "w_ukv", "sg_ln_g", "sg_ln_b", "sg_w", "sg_b",
           "rw_mu", "rw_w0", "rw_w2", "rw_a0", "rw_a2", "rw_g2", "rw_k_k", "rw_k_a", "rw_r_k", "rw_ln_g", "rw_ln_b",
           "w_branch", "w_out", "ffn_norm_g", "w_ffn_gate", "w_ffn_up", "w_ffn_down", "final_norm_g")


REP_MAIN = tuple(k for k in REPLICATED if k not in ("attn_norm_g", "sg_w"))
BIG = ("w_in", "w_branch", "w_out", "w_ffn_gate", "w_ffn_up", "w_ffn_down")
SMALL_BF = ("w_uq", "w_ukv", "rw_w2", "rw_a2", "rw_g2")
SMALL = SMALL_BF + GATHER_F32


def _pack128(blocks, names, dtype, lead=0, to=256):
    parts = [blocks[k].astype(dtype).reshape(blocks[k].shape[:lead] + (-1, 128)) for k in names]
    rows = sum(p.shape[lead] for p in parts)
    pad = -rows % to
    if pad:
        parts.append(jnp.zeros(parts[0].shape[:lead] + (pad, 128), dtype))
    return jnp.concatenate(parts, axis=lead)


def _unpack128(packed, shapes, names, lead=0):
    out, off = {}, 0
    for k in names:
        rows = 1
        for d in shapes[k]:
            rows *= d
        rows //= 128
        idx = (slice(None),) * lead + (slice(off, off + rows),)
        out[k] = packed[idx].reshape(packed.shape[:lead] + tuple(shapes[k]))
        off += rows
    return out


def _join_blocks(g, ax):
    shp = g.shape[1:]
    return jnp.moveaxis(g, 0, ax).reshape(shp[:ax] + (N_DEV * shp[ax],) + shp[ax + 1:])


def _split_blocks(full, ax):
    shp = full.shape
    return jnp.moveaxis(full.reshape(shp[:ax] + (N_DEV, shp[ax] // N_DEV) + shp[ax + 1:]), ax, 0)


def _w_in_padded(w):
    z = lambda n: jnp.zeros((w.shape[0], n), w.dtype)
    q, ckv, kr = w[:, 0:384], w[:, 384:640], w[:, 640:672]
    sg, rw, gate = w[:, 672:1696], w[:, 1696:3616], w[:, 3616:6688]
    return jnp.concatenate([gate, sg, rw, z(128), ckv, z(64), kr, z(32), q, z(P_W - O_MLA - MLA_W)], axis=1)


def _w_in_unpadded(g):
    return jnp.concatenate([g[:, O_Q:O_Q + 384], g[:, O_CKV:O_CKV + 256], g[:, O_SLAB + 64:O_SLAB + 96],
                            g[:, O_SG:O_SG + 1024], g[:, O_RW:O_RW + 1920], g[:, O_GATE:O_GATE + 3072]], axis=1)


REST = ("w_branch", "w_out", "w_ffn_gate", "w_ffn_up", "w_ffn_down")


def _rest_weights(full, l):
    return dict(wb=full["w_branch"][l], wo=full["w_out"][l], wg=full["w_ffn_gate"][l], wu=full["w_ffn_up"][l],
                wd=full["w_ffn_down"][l])


def _layer_weights(full, rep, l):
    w = {}
    w["w_in"] = _w_in_padded(full["w_in"][l])
    if full["w_branch"][l] is not None:
        w.update(_rest_weights(full, l))
    uq = full["w_uq"][l].reshape(Q_LORA, HEADS, QK_NOPE + QK_ROPE)
    w["wq"] = jnp.pad(uq, ((0, 0), (0, 0), (0, 32))).reshape(Q_LORA, HEADS * 128).astype(f32)
    ukv = full["w_ukv"][l].reshape(KV_LORA, HEADS, QK_NOPE + V_HEAD)
    wk = jnp.pad(ukv[:, :, :QK_NOPE], ((0, 0), (0, 0), (0, 64))).reshape(KV_LORA, HEADS * 128)
    w["wk"], w["wv"] = wk.astype(f32), ukv[:, :, QK_NOPE:].reshape(KV_LORA, HEADS * V_HEAD).astype(f32)
    bdiag = lambda t: jnp.concatenate([jnp.concatenate([t[0], jnp.zeros_like(t[0])], axis=1),
                                       jnp.concatenate([jnp.zeros_like(t[1]), t[1]], axis=1)], axis=0).astype(f32)
    w["w2"], w["a2"] = bdiag(full["rw_w2"][l]), bdiag(full["rw_a2"][l])
    w["g2"] = full["rw_g2"][l].astype(f32)
    w["w0"], w["a0"] = full["rw_w0"][l].reshape(1, 2 * RW_DIM), full["rw_a0"][l].reshape(1, 2 * RW_DIM)
    w["gate_b"] = full["gate_b"][l].reshape(1, 3 * D)
    row = lambda a: a.reshape(1, -1)
    for k in ("attn_norm_g", "q_norm_g", "kv_norm_g", "sg_ln_g", "sg_ln_b", "rw_k_k", "rw_k_a", "rw_ln_g", "rw_ln_b",
              "ffn_norm_g"):
        w[k] = row(rep[k][l])
    w["r_k"] = row(rep["rw_r_k"][l])
    w["mu"] = jnp.pad(row(rep["rw_mu"][l]), ((0, 0), (0, RW_W - 1920)))
    w["sg_w"] = [rep["sg_w"][l, k] for k in range(SG_GROUPS)]
    w["sg_bias"] = jnp.repeat(rep["sg_b"][l].T, SG_DIM // SG_GROUPS, axis=1)
    return w


def _riding(res, ride, got, key):
    if ride is None:
        return res
    got[key] = res[1]
    return res[0]


def _layer_fwd(x2, w, tabs, bsz, seq, l, rides=None, on_gathered=None):
    nm = lambda s: f"l{l}_{s}"
    n = x2.shape[0]
    tm = min(512, n)
    rides = rides or {}
    ride = lambda key: (rides[key], []) if key in rides else None
    got = {}
    p, h = _riding(_inproj_fwd(x2, w["attn_norm_g"], w["w_in"], nm("inproj"), ride("inproj")), ride("inproj"), got, "inproj")
    mla_rows = [(p, 256, O_CKV // 256), (p, 128, O_SLAB // 128), (p, 384, O_Q // 384), (tabs[0], 128, 0), (tabs[1], 128, 0)]
    mla_w = [w["q_norm_g"], w["kv_norm_g"], w["wq"], w["wk"], w["wv"]]
    q, k, v = _rowwise_fwd(nm("mla_proj"), _f_mla_proj, mla_rows, mla_w, [(1024, bf16), (1024, bf16), (512, bf16)], tm)
    ya, lse = _riding(_attn_fwd(q, k, v, bsz, seq, nm("attn"), ride("attn")), ride("attn"), got, "attn")
    sg_rows = [(p, SG_DIM, O_SG // SG_DIM), (p, SG_DIM, O_SG // SG_DIM + 1)]
    sg_w = [w["sg_ln_g"], w["sg_ln_b"], w["sg_bias"]] + w["sg_w"]
    (yb,) = _rowwise_fwd(nm("sg"), _f_sg, sg_rows, sg_w, [(SG_DIM, f32)], SG_CHUNK)
    z = _shift_fwd(p, w["mu"], seq, nm("shift"))
    pre_rows = [(z, 512, 1), (z, 128, 12), (z, 128, 13), (z, 128, 14)]
    pre_w = [w["w0"], w["a0"], w["w2"], w["a2"], w["g2"], w["rw_k_k"], w["rw_k_a"]]
    lw, kd, kk, bd, g = _rowwise_fwd(nm("rw_pre"), _f_rw_pre, pre_rows, pre_w,
                                     [(1024, f32), (1024, f32), (512, f32), (1024, f32), (512, f32)], tm)
    y, s_in, t_in = _riding(_scan_fwd(z, lw, kd, kk, bd, bsz, seq, nm("scan"), ride("scan")), ride("scan"), got, "scan")
    post_rows = [(y[0], 512, 0), (y[1], 512, 0), (z, 512, 0), (z, 512, 2), (kd, 512, 0), (kd, 512, 1), (g, 512, 0)]
    post_w = [w["r_k"], w["rw_ln_g"], w["rw_ln_b"]]
    (yc,) = _rowwise_fwd(nm("rw_post"), _f_rw_post, post_rows, post_w, [(512, f32)], tm)
    if on_gathered is not None:
        w.update(on_gathered(got))
    x1 = _merge_fwd(x2, p, ya, yb, yc, w["gate_b"], w["wb"], w["wo"], nm("merge"))
    x3 = _ffn_fwd(x1, w["ffn_norm_g"], w["wg"], w["wu"], w["wd"], nm("ffn"))
    saved = dict(x=x2, p=p, h=h, q=q, k=k, v=v, ya=ya, lse=lse, yb=yb, z=z, lw=lw, kd=kd, kk=kk, bd=bd, g=g, y=y, s_in=s_in, t_in=t_in, yc=yc,
                 x1=x1, mla_rows=mla_rows, mla_w=mla_w, sg_rows=sg_rows, sg_w=sg_w, pre_rows=pre_rows, pre_w=pre_w,
                 post_rows=post_rows, post_w=post_w)
    return x3, saved, got


def _layer_bwd(dx3, w, sv, bsz, seq, l, rides=None):
    nm = lambda s: f"l{l}_{s}_bwd"
    n = dx3.shape[0]
    tm = min(512, n)
    g = {}
    rides = rides or {}
    ride = lambda key: rides[key](g) if key in rides else None
    got = {}
    dx1, g["ffn_norm_g"], h2, da, db, hm = _ffn_bwd(dx3, sv["x1"], w["ffn_norm_g"], w["wg"], w["wu"], w["wd"], nm("ffn"))
    g["wg"] = _matmul_tn(h2, da, nm("wg"))
    g["wu"] = _matmul_tn(h2, db, nm("wu"))
    g["wd"] = _matmul_tn(hm, dx3, nm("wd"))
    dpg, dya, dyb, dyc, dt, mg, g["gate_b"] = _merge_bwd(dx1, sv["p"], sv["ya"], sv["yb"], sv["yc"], w["gate_b"], w["wb"],
                                                         w["wo"], nm("merge"))
    g["wo"] = _matmul_tn(mg, dx1, nm("wo"))
    ys = (sv["ya"], sv["yb"], sv["yc"])
    g["wb"] = jnp.stack([_matmul_tn(ys[k], dt[:, k * D:(k + 1) * D], nm(f"wb{k}")) for k in range(3)])
    (dy, dr_p, dv_p, dkd0, dkd1, dg_), (g["r_k"], g["rw_ln_g"], g["rw_ln_b"]) = _rowwise_bwd(
        nm("rw_post"), _f_rw_post, sv["post_rows"], sv["post_w"], [(dyc, 512, 0)], tm, [f32, None] + [f32] * 5)
    dkd_p = jnp.concatenate([dkd0, dkd1], axis=1)
    rd = ride("scan")
    dr_s, dv_s, dkk_s, dlw, dkd_s, dbd = _riding(
        _scan_bwd(sv["z"], sv["lw"], sv["kd"], sv["kk"], sv["bd"], sv["s_in"], sv["t_in"], dy, bsz, seq, nm("scan"), rd),
        rd, got, "scan")
    pre_cots = [(dlw, 1024, 0), (dkd_s + dkd_p, 1024, 0), (dkk_s[0] + dkk_s[1], 512, 0), (dbd, 1024, 0), (dg_, 512, 0)]
    (dk, dwl, dal, dgl), (g["w0"], g["a0"], g["w2"], g["a2"], g["g2"], g["rw_k_k"], g["rw_k_a"]) = _rowwise_bwd(
        nm("rw_pre"), _f_rw_pre, sv["pre_rows"], sv["pre_w"], pre_cots, tm, [f32] * 4)
    dz = jnp.concatenate([dr_s[0] + dr_s[1] + dr_p, dk, dv_s[0] + dv_s[1] + dv_p, dwl, dal, dgl,
                          jnp.zeros((n, RW_W - 1920), f32)], axis=1)
    dp_rw, g["mu"] = _shift_bwd(dz, sv["p"], w["mu"], seq, nm("shift"))
    (dp_su, dp_sv), (g["sg_ln_g"], g["sg_ln_b"], g["sg_bias"], *sgw) = _rowwise_bwd(
        nm("sg"), _f_sg, sv["sg_rows"], sv["sg_w"], [(dyb, SG_DIM, 0)], SG_CHUNK, [bf16, bf16])
    g["sg_w"] = jnp.stack(sgw)
    rd = ride("attn")
    dq, dk_, dv_ = _riding(_attn_bwd(sv["q"], sv["k"], sv["v"], sv["ya"], sv["lse"], dya, bsz, seq, nm("attn"), rd), rd, got,
                           "attn")
    (dp_ckv, dp_slab, dp_q), (g["q_norm_g"], g["kv_norm_g"], g["wq"], g["wk"], g["wv"]) = _rowwise_bwd(
        nm("mla_proj"), _f_mla_proj, sv["mla_rows"], sv["mla_w"], [(dq, 1024, 0), (dk_, 1024, 0), (dv_, 512, 0)], tm,
        [bf16, bf16, bf16, None, None])
    dp = jnp.concatenate([dpg, dp_su, dp_sv, dp_rw, dp_ckv, dp_slab, dp_q, jnp.zeros((n, P_W - O_MLA - MLA_W), bf16)],
                         axis=1)
    rd = ride("w_in")
    g["w_in"] = _riding(_matmul_tn(sv["h"], dp, nm("w_in"), rd), rd, got, "w_in")
    rd = ride("inproj")
    dx, g["attn_norm_g"] = _riding(_norm_matmul_bwd(dp, w["w_in"], sv["x"], w["attn_norm_g"], dx1, nm("inproj"), rd), rd, got,
                                   "inproj")
    return dx, g, got


def _layer_grads_to_full(g):
    o = {}
    if "w_in" in g:
        o["w_in"] = _w_in_unpadded(g["w_in"])
    o["w_uq"] = g["wq"].reshape(Q_LORA, HEADS, 128)[:, :, :QK_NOPE + QK_ROPE].reshape(Q_LORA, -1)
    gk = g["wk"].reshape(KV_LORA, HEADS, 128)[:, :, :QK_NOPE]
    gv = g["wv"].reshape(KV_LORA, HEADS, V_HEAD)
    o["w_ukv"] = jnp.concatenate([gk, gv], axis=2).reshape(KV_LORA, -1)
    unb = lambda t: jnp.stack([t[:LORA, :RW_DIM], t[LORA:, RW_DIM:]])
    o["rw_w2"], o["rw_a2"], o["rw_g2"] = unb(g["w2"]), unb(g["a2"]), g["g2"]
    o["rw_w0"], o["rw_a0"] = g["w0"].reshape(2, RW_DIM), g["a0"].reshape(2, RW_DIM)
    o["gate_b"] = g["gate_b"].reshape(3, D)
    o["w_branch"], o["w_out"] = g["wb"], g["wo"]
    o["w_ffn_gate"], o["w_ffn_up"], o["w_ffn_down"] = g["wg"], g["wu"], g["wd"]
    for k in ("attn_norm_g", "q_norm_g", "kv_norm_g", "sg_ln_g", "sg_ln_b", "rw_k_k", "rw_k_a", "rw_ln_g", "rw_ln_b",
              "ffn_norm_g"):
        if k in g:
            o[k] = g[k].reshape(-1)
    o["rw_r_k"] = g["r_k"].reshape(HEADS, RW_HEAD)
    o["rw_mu"] = g["mu"].reshape(-1)[:1920]
    o["sg_w"] = g["sg_w"]
    o["sg_b"] = g["sg_bias"].reshape(SG_CHUNK, SG_GROUPS, SG_DIM // SG_GROUPS).sum(axis=2).T
    return o


def _rope_tables(positions):
    inv = 1.0 / (10000.0 ** (jnp.arange(0, QK_ROPE, 2, dtype=f32) / QK_ROPE))
    ang = positions.astype(f32)[:, None] * inv
    cos, sin = jnp.cos(ang), jnp.sin(ang)
    n = positions.shape[0]
    c = jnp.concatenate([jnp.ones((n, 64), f32), cos, cos, jnp.zeros((n, 32), f32)], axis=1)
    s = jnp.concatenate([jnp.zeros((n, 64), f32), -sin, sin, jnp.zeros((n, 32), f32)], axis=1)
    return c, s


def _grad_parts(grad, name):
    return _split_blocks(grad, SHARDED[name] - 1).astype(bf16)


def _local_step(x, positions, full, rep, loss_target, blocks=None):
    bsz, seq, _ = x.shape
    n = bsz * seq
    x2 = x.reshape(n, D)
    tabs = _rope_tables(positions.reshape(n))
    join = lambda k, g: _join_blocks(g, SHARDED[k] - 1)
    rides, on_gathered = None, None
    if blocks is not None:
        carried = {"inproj": [("w_branch", 0), ("w_out", 0)],
                   "attn": [("w_in", 1), ("w_ffn_gate", 0), ("w_ffn_up", 0)],
                   "scan": [("w_ffn_down", 0)] + [(k, 1) for k in REST]}
        rides = {key: [blocks[k][l] for k, l in what] for key, what in carried.items()}

        def on_gathered(got):
            for key, what in carried.items():
                for (k, l), g in zip(what, got[key]):
                    full[k][l] = join(k, g)
            return _rest_weights(full, 0)

    w0 = _layer_weights(full, rep, 0)
    x2, sv0, got = _layer_fwd(x2, w0, tabs, bsz, seq, 0, rides, on_gathered)
    w1 = _layer_weights(full, rep, 1)
    x2, sv1, _ = _layer_fwd(x2, w1, tabs, bsz, seq, 1)
    loss, dx, dgf = _loss_head(x2, rep["final_norm_g"].reshape(1, D), loss_target.reshape(n, D), "loss_head")
    dx, g1, _ = _layer_bwd(dx, w1, sv1, bsz, seq, 1)
    grads1 = _layer_grads_to_full(g1)
    rides = None
    if blocks is not None:
        short = dict(w_branch="wb", w_out="wo", w_ffn_gate="wg", w_ffn_up="wu", w_ffn_down="wd")

        def beside_w_in(g):
            g0 = _layer_grads_to_full(g)
            both = {k: jnp.stack([g0[k], grads1[k]]) for k in g0}
            both["final_norm_g"] = dgf.reshape(D)
            split = {k: _split_blocks(both[k], SHARDED[k]) for k in SMALL}
            return [_pack128(both, REP_MAIN, f32), both["sg_w"]], [_pack128(split, SMALL, f32, lead=1)]

        rides = {"scan": lambda g: ([], [_grad_parts(grads1[k], k) for k in BIG]),
                 "attn": lambda g: ([], [_grad_parts(g[short[k]], k) for k in REST]),
                 "w_in": beside_w_in,
                 "inproj": lambda g: ([], [_grad_parts(_w_in_unpadded(g["w_in"]), "w_in")])}
    dx, g0, got = _layer_bwd(dx, w0, sv0, bsz, seq, 0, rides)
    grads0 = _layer_grads_to_full(g0)
    grads = {k: [grads0[k], grads1[k]] for k in grads0}
    grads["final_norm_g"] = dgf.reshape(D)
    parts = {}
    if blocks is not None:
        parts = {k: [None, p] for k, p in zip(BIG, got["scan"])}
        for k, p in zip(REST, got["attn"]):
            parts[k][0] = p
        parts["replicated"], parts["sg_w"], parts["small"] = got["w_in"]
        (parts["w_in"][0],) = got["inproj"]
    return loss[0, 0], dx.reshape(bsz, seq, D), grads, parts


def kernel(x, positions, attn_norm_g, w_in, gate_b, q_norm_g, w_uq, kv_norm_g, w_ukv, sg_ln_g, sg_ln_b, sg_w, sg_b, rw_mu, rw_w0, rw_w2, rw_a0, rw_a2, rw_g2, rw_k_k, rw_k_a, rw_r_k, rw_ln_g, rw_ln_b, w_branch, w_out, ffn_norm_g, w_ffn_gate, w_ffn_up, w_ffn_down, final_norm_g, loss_target, m_attn_norm_g, m_w_in, m_gate_b, m_q_norm_g, m_w_uq, m_kv_norm_g, m_w_ukv, m_sg_ln_g, m_sg_ln_b, m_sg_w, m_sg_b, m_rw_mu, m_rw_w0, m_rw_w2, m_rw_a0, m_rw_a2, m_rw_g2, m_rw_k_k, m_rw_k_a, m_rw_r_k, m_rw_ln_g, m_rw_ln_b, m_w_branch, m_w_out, m_ffn_norm_g, m_w_ffn_gate, m_w_ffn_up, m_w_ffn_down, m_final_norm_g, v_attn_norm_g, v_w_in, v_gate_b, v_q_norm_g, v_w_uq, v_kv_norm_g, v_w_ukv, v_sg_ln_g, v_sg_ln_b, v_sg_w, v_sg_b, v_rw_mu, v_rw_w0, v_rw_w2, v_rw_a0, v_rw_a2, v_rw_g2, v_rw_k_k, v_rw_k_a, v_rw_r_k, v_rw_ln_g, v_rw_ln_b, v_w_branch, v_w_out, v_ffn_norm_g, v_w_ffn_gate, v_w_ffn_up, v_w_ffn_down, v_final_norm_g):
    args = locals()
    wts = {k: args[k] for k in WEIGHTS}
    mom_m = {k: args["m_" + k] for k in WEIGHTS}
    mom_v = {k: args["v_" + k] for k in WEIGHTS}
    shapes = {k: wts[k].shape for k in WEIGHTS}
    blocks = {k: wts[k].astype(bf16) for k in BIG}
    got = _exchange([blocks["w_in"][0], _pack128(wts, SMALL_BF, bf16), _pack128(wts, GATHER_F32, f32)], [], "gather_first")
    small = {**_unpack128(got[1], shapes, SMALL_BF, lead=1), **_unpack128(got[2], shapes, GATHER_F32, lead=1)}
    full = {k: list(_join_blocks(small[k], SHARDED[k])) for k in SMALL}
    full["w_in"] = [_join_blocks(got[0], SHARDED["w_in"] - 1), None]
    full.update({k: [None, None] for k in REST})
    rep = {k: wts[k] for k in REPLICATED}
    loss, grad_x, grads, parts = _local_step(x, positions, full, rep, loss_target, blocks)
    loss = lax.psum(loss, ("x", "y", "c"))
    last = ("attn_norm_g",)
    (last_parts,) = _exchange([_pack128({"attn_norm_g": jnp.stack(grads["attn_norm_g"])}, last, f32, to=16)], [],
                              "exchange_last")
    gw, delta, new_m, new_v = {}, {}, {}, {}
    for k in BIG:
        three = lambda a, k=k: a.reshape(a.shape[0], -1, shapes[k][-1])
        res = _adamw(three(wts[k]), [three(p) for p in parts[k]], three(mom_m[k]), three(mom_v[k]), f"adamw_{k}")
        gw[k], delta[k], new_m[k], new_v[k] = (t.reshape(shapes[k]) for t in res)
    rows = lambda a: a.reshape(1, -1, 128)
    res = _adamw(rows(wts["sg_w"]), [parts["sg_w"].reshape(N_DEV, -1, 128)], rows(mom_m["sg_w"]), rows(mom_v["sg_w"]),
                 "adamw_sg_w")
    gw["sg_w"], delta["sg_w"], new_m["sg_w"], new_v["sg_w"] = (t.reshape(shapes["sg_w"]) for t in res)
    for names, got, to in ((SMALL, parts["small"], 256), (REP_MAIN, parts["replicated"], 256), (last, last_parts, 16)):
        pk = lambda dct: _pack128(dct, names, f32, to=to)[None]
        res = _adamw(pk(wts), [got], pk(mom_m), pk(mom_v), f"adamw_{names[0]}")
        for dst, t in zip((gw, delta, new_m, new_v), res):
            dst.update(_unpack128(t[0], shapes, names))
    return (loss, grad_x, *[gw[k] for k in WEIGHTS], *[delta[k] for k in WEIGHTS], *[new_m[k] for k in WEIGHTS],
            *[new_v[k] for k in WEIGHTS])
```
